```python
import jax, jax.numpy as jnp
from jax import lax
import numpy as np

D_MODEL = 1024
BATCH = 8
SEQ = 8192
DEPTH = 1

N_Q_HEADS = 8
N_KV_HEADS = 2
HEAD_DIM = 64
ATTN_WIDTH = N_Q_HEADS * HEAD_DIM
KV_WIDTH = N_KV_HEADS * HEAD_DIM
WINDOW = 128
BLOCK = 128
GMLP_GROUPS = 8
GMLP_GROUP_DIM = 64
GMLP_WIDTH = GMLP_GROUPS * GMLP_GROUP_DIM
CHUNK = 128
MIX_WIDTH = ATTN_WIDTH + GMLP_WIDTH
IN_WIDTH = ATTN_WIDTH + 2 * KV_WIDTH + 2 * GMLP_WIDTH
D_FF = 2816
FFN_RES = 0.5
EPS = 1e-6

kernel_name = "hybrid_swa_sink_gmlp_macaron"


def rmsnorm(x, g):
    xf = x.astype(jnp.float32)
    y = xf * lax.rsqrt(jnp.mean(xf * xf, axis=-1, keepdims=True) + EPS) * g.astype(jnp.float32)
    return y.astype(x.dtype)


def layernorm(x, g, b):
    xf = x.astype(jnp.float32)
    mu = jnp.mean(xf, axis=-1, keepdims=True)
    var = jnp.mean(jnp.square(xf - mu), axis=-1, keepdims=True)
    y = (xf - mu) * lax.rsqrt(var + EPS) * g.astype(jnp.float32) + b.astype(jnp.float32)
    return y.astype(x.dtype)


def swiglu(h, w_gate, w_up, w_down):
    return (jax.nn.silu(h @ w_gate) * (h @ w_up)) @ w_down


def sliding_window_sink_attention(q, k, v, sinks):
    b, s = q.shape[0], q.shape[1]
    nb = s // BLOCK
    rep = N_Q_HEADS // N_KV_HEADS
    qb = q.reshape(b, nb, BLOCK, N_KV_HEADS, rep, HEAD_DIM)

    def band(t):
        tb = t.reshape(b, nb, BLOCK, N_KV_HEADS, HEAD_DIM)
        prev = jnp.pad(tb, ((0, 0), (1, 0), (0, 0), (0, 0), (0, 0)))[:, :-1]
        return jnp.concatenate([prev, tb], axis=2)

    kw, vw = band(k), band(v)
    scores = jnp.einsum('bnqgrd,bnkgd->bngrqk', qb, kw,
                        preferred_element_type=jnp.float32) * (HEAD_DIM ** -0.5)
    qpos = jnp.arange(BLOCK)[:, None]
    kpos = jnp.arange(2 * BLOCK)[None, :] - BLOCK
    rel = qpos - kpos
    in_window = (rel >= 0) & (rel < WINDOW)
    real_key = (kpos >= 0)[None] | (jnp.arange(nb) > 0)[:, None, None]
    mask = (in_window[None] & real_key)[None, :, None, None]
    scores = jnp.where(mask, scores, -jnp.inf)
    sink = sinks.astype(jnp.float32).reshape(N_KV_HEADS, rep)[None, None, :, :, None, None]
    m = jnp.maximum(jnp.max(scores, axis=-1, keepdims=True), sink)
    p = jnp.exp(scores - m)
    p = p / (jnp.sum(p, axis=-1, keepdims=True) + jnp.exp(sink - m))
    out = jnp.einsum('bngrqk,bnkgd->bnqgrd', p.astype(v.dtype), vw)
    return out.reshape(b, s, ATTN_WIDTH)


def chunked_spatial_gating(z, ln_g, ln_b, w_s, b_s):
    u, vv = jnp.split(z, 2, axis=-1)
    vv = layernorm(vv, ln_g, ln_b)
    b, s = vv.shape[0], vv.shape[1]
    nc = s // CHUNK
    vc = vv.reshape(b, nc, CHUNK, GMLP_GROUPS, GMLP_GROUP_DIM)
    causal = jnp.tril(jnp.ones((CHUNK, CHUNK), dtype=bool))
    w = jnp.where(causal[None], w_s, 0).astype(vc.dtype)
    mixed = jnp.einsum('gts,bnsge->bntge', w, vc) + b_s.T.astype(vc.dtype)[None, None, :, :, None]
    return u * mixed.reshape(b, s, GMLP_WIDTH)


def _fwd_setup_inputs(seed: int = 0) -> dict:
    key = jax.random.key(seed)
    ks = jax.random.split(key, 24)
    f32 = jnp.float32
    L = DEPTH

    def nrm(k, shape, scale):
        return jax.random.normal(k, shape, f32) * scale

    def gain(k, n):
        return 1.0 + 0.02 * jax.random.normal(k, (L, n), f32)

    return {
        "x": jax.random.normal(ks[0], (BATCH, SEQ, D_MODEL), f32),
        "ffn1_norm_g": gain(ks[1], D_MODEL),
        "ffn1_w_gate": nrm(ks[2], (L, D_MODEL, D_FF), D_MODEL ** -0.5),
        "ffn1_w_up": nrm(ks[3], (L, D_MODEL, D_FF), D_MODEL ** -0.5),
        "ffn1_w_down": nrm(ks[4], (L, D_FF, D_MODEL), D_FF ** -0.5),
        "mix_norm_g": gain(ks[5], D_MODEL),
        "w_in": nrm(ks[6], (L, D_MODEL, IN_WIDTH), D_MODEL ** -0.5),
        "b_in": nrm(ks[7], (L, IN_WIDTH), 0.02),
        "attn_sinks": nrm(ks[8], (L, N_Q_HEADS), 1.0),
        "gmlp_ln_g": gain(ks[9], GMLP_WIDTH),
        "gmlp_ln_b": nrm(ks[10], (L, GMLP_WIDTH), 0.02),
        "gmlp_w_s": nrm(ks[11], (L, GMLP_GROUPS, CHUNK, CHUNK), CHUNK ** -0.5),
        "gmlp_b_s": 1.0 + nrm(ks[12], (L, GMLP_GROUPS, CHUNK), 0.02),
        "attn_out_norm_g": gain(ks[13], ATTN_WIDTH),
        "gmlp_out_norm_g": gain(ks[14], GMLP_WIDTH),
        "w_out": nrm(ks[15], (L, MIX_WIDTH, D_MODEL), MIX_WIDTH ** -0.5),
        "b_out": nrm(ks[16], (L, D_MODEL), 0.02),
        "ffn2_norm_g": gain(ks[17], D_MODEL),
        "ffn2_w_gate": nrm(ks[18], (L, D_MODEL, D_FF), D_MODEL ** -0.5),
        "ffn2_w_up": nrm(ks[19], (L, D_MODEL, D_FF), D_MODEL ** -0.5),
        "ffn2_w_down": nrm(ks[20], (L, D_FF, D_MODEL), D_FF ** -0.5),
        "final_norm_g": 1.0 + 0.02 * jax.random.normal(ks[21], (D_MODEL,), f32),
    }


def _fwd_reference(x, ffn1_norm_g, ffn1_w_gate, ffn1_w_up, ffn1_w_down, mix_norm_g, w_in, b_in,
              attn_sinks, gmlp_ln_g, gmlp_ln_b, gmlp_w_s, gmlp_b_s, attn_out_norm_g,
              gmlp_out_norm_g, w_out, b_out, ffn2_norm_g, ffn2_w_gate, ffn2_w_up, ffn2_w_down,
              final_norm_g):
    b, s, _ = x.shape
    for l in range(DEPTH):
        x = x + FFN_RES * swiglu(rmsnorm(x, ffn1_norm_g[l]), ffn1_w_gate[l], ffn1_w_up[l], ffn1_w_down[l])
        h = rmsnorm(x, mix_norm_g[l])
        proj = h @ w_in[l] + b_in[l]
        q, k, v, zg = jnp.split(proj, [ATTN_WIDTH, ATTN_WIDTH + KV_WIDTH,
                                       ATTN_WIDTH + 2 * KV_WIDTH], axis=-1)
        q = q.reshape(b, s, N_Q_HEADS, HEAD_DIM)
        k = k.reshape(b, s, N_KV_HEADS, HEAD_DIM)
        v = v.reshape(b, s, N_KV_HEADS, HEAD_DIM)
        y_attn = sliding_window_sink_attention(q, k, v, attn_sinks[l])
        y_gmlp = chunked_spatial_gating(jax.nn.gelu(zg), gmlp_ln_g[l], gmlp_ln_b[l],
                                        gmlp_w_s[l], gmlp_b_s[l])
        y = jnp.concatenate([rmsnorm(y_attn, attn_out_norm_g[l]),
                             rmsnorm(y_gmlp, gmlp_out_norm_g[l])], axis=-1)
        x = x + (y @ w_out[l] + b_out[l])
        x = x + FFN_RES * swiglu(rmsnorm(x, ffn2_norm_g[l]), ffn2_w_gate[l], ffn2_w_up[l], ffn2_w_down[l])
    return rmsnorm(x, final_norm_g)


import jax as _jax
import jax.numpy as _jnp

TWIN_FORMAT = 'train_step'
FWD_PARAMS = ['x', 'ffn1_norm_g', 'ffn1_w_gate', 'ffn1_w_up', 'ffn1_w_down', 'mix_norm_g', 'w_in', 'b_in', 'attn_sinks', 'gmlp_ln_g', 'gmlp_ln_b', 'gmlp_w_s', 'gmlp_b_s', 'attn_out_norm_g', 'gmlp_out_norm_g', 'w_out', 'b_out', 'ffn2_norm_g', 'ffn2_w_gate', 'ffn2_w_up', 'ffn2_w_down', 'final_norm_g']
TWIN_WEIGHTS = ['ffn1_norm_g', 'ffn1_w_gate', 'ffn1_w_up', 'ffn1_w_down', 'mix_norm_g', 'w_in', 'b_in', 'attn_sinks', 'gmlp_ln_g', 'gmlp_ln_b', 'gmlp_w_s', 'gmlp_b_s', 'attn_out_norm_g', 'gmlp_out_norm_g', 'w_out', 'b_out', 'ffn2_norm_g', 'ffn2_w_gate', 'ffn2_w_up', 'ffn2_w_down', 'final_norm_g']
TWIN_DIFF_INPUT = 'x'
TWIN_INPUTS = ['x', 'ffn1_norm_g', 'ffn1_w_gate', 'ffn1_w_up', 'ffn1_w_down', 'mix_norm_g', 'w_in', 'b_in', 'attn_sinks', 'gmlp_ln_g', 'gmlp_ln_b', 'gmlp_w_s', 'gmlp_b_s', 'attn_out_norm_g', 'gmlp_out_norm_g', 'w_out', 'b_out', 'ffn2_norm_g', 'ffn2_w_gate', 'ffn2_w_up', 'ffn2_w_down', 'final_norm_g', 'loss_target', 'm_ffn1_norm_g', 'm_ffn1_w_gate', 'm_ffn1_w_up', 'm_ffn1_w_down', 'm_mix_norm_g', 'm_w_in', 'm_b_in', 'm_attn_sinks', 'm_gmlp_ln_g', 'm_gmlp_ln_b', 'm_gmlp_w_s', 'm_gmlp_b_s', 'm_attn_out_norm_g', 'm_gmlp_out_norm_g', 'm_w_out', 'm_b_out', 'm_ffn2_norm_g', 'm_ffn2_w_gate', 'm_ffn2_w_up', 'm_ffn2_w_down', 'm_final_norm_g', 'v_ffn1_norm_g', 'v_ffn1_w_gate', 'v_ffn1_w_up', 'v_ffn1_w_down', 'v_mix_norm_g', 'v_w_in', 'v_b_in', 'v_attn_sinks', 'v_gmlp_ln_g', 'v_gmlp_ln_b', 'v_gmlp_w_s', 'v_gmlp_b_s', 'v_attn_out_norm_g', 'v_gmlp_out_norm_g', 'v_w_out', 'v_b_out', 'v_ffn2_norm_g', 'v_ffn2_w_gate', 'v_ffn2_w_up', 'v_ffn2_w_down', 'v_final_norm_g']
TWIN_OUTPUTS = ['loss', 'grad_x', 'grad_ffn1_norm_g', 'grad_ffn1_w_gate', 'grad_ffn1_w_up', 'grad_ffn1_w_down', 'grad_mix_norm_g', 'grad_w_in', 'grad_b_in', 'grad_attn_sinks', 'grad_gmlp_ln_g', 'grad_gmlp_ln_b', 'grad_gmlp_w_s', 'grad_gmlp_b_s', 'grad_attn_out_norm_g', 'grad_gmlp_out_norm_g', 'grad_w_out', 'grad_b_out', 'grad_ffn2_norm_g', 'grad_ffn2_w_gate', 'grad_ffn2_w_up', 'grad_ffn2_w_down', 'grad_final_norm_g', 'delta_ffn1_norm_g', 'delta_ffn1_w_gate', 'delta_ffn1_w_up', 'delta_ffn1_w_down', 'delta_mix_norm_g', 'delta_w_in', 'delta_b_in', 'delta_attn_sinks', 'delta_gmlp_ln_g', 'delta_gmlp_ln_b', 'delta_gmlp_w_s', 'delta_gmlp_b_s', 'delta_attn_out_norm_g', 'delta_gmlp_out_norm_g', 'delta_w_out', 'delta_b_out', 'delta_ffn2_norm_g', 'delta_ffn2_w_gate', 'delta_ffn2_w_up', 'delta_ffn2_w_down', 'delta_final_norm_g', 'new_m_ffn1_norm_g', 'new_m_ffn1_w_gate', 'new_m_ffn1_w_up', 'new_m_ffn1_w_down', 'new_m_mix_norm_g', 'new_m_w_in', 'new_m_b_in', 'new_m_attn_sinks', 'new_m_gmlp_ln_g', 'new_m_gmlp_ln_b', 'new_m_gmlp_w_s', 'new_m_gmlp_b_s', 'new_m_attn_out_norm_g', 'new_m_gmlp_out_norm_g', 'new_m_w_out', 'new_m_b_out', 'new_m_ffn2_norm_g', 'new_m_ffn2_w_gate', 'new_m_ffn2_w_up', 'new_m_ffn2_w_down', 'new_m_final_norm_g', 'new_v_ffn1_norm_g', 'new_v_ffn1_w_gate', 'new_v_ffn1_w_up', 'new_v_ffn1_w_down', 'new_v_mix_norm_g', 'new_v_w_in', 'new_v_b_in', 'new_v_attn_sinks', 'new_v_gmlp_ln_g', 'new_v_gmlp_ln_b', 'new_v_gmlp_w_s', 'new_v_gmlp_b_s', 'new_v_attn_out_norm_g', 'new_v_gmlp_out_norm_g', 'new_v_w_out', 'new_v_b_out', 'new_v_ffn2_norm_g', 'new_v_ffn2_w_gate', 'new_v_ffn2_w_up', 'new_v_ffn2_w_down', 'new_v_final_norm_g']
TWIN_LEAF_KINDS = {'loss': 'loss', 'grad_x': 'grad_x', 'grad_ffn1_norm_g': 'grad_w', 'grad_ffn1_w_gate': 'grad_w', 'grad_ffn1_w_up': 'grad_w', 'grad_ffn1_w_down': 'grad_w', 'grad_mix_norm_g': 'grad_w', 'grad_w_in': 'grad_w', 'grad_b_in': 'grad_w', 'grad_attn_sinks': 'grad_w', 'grad_gmlp_ln_g': 'grad_w', 'grad_gmlp_ln_b': 'grad_w', 'grad_gmlp_w_s': 'grad_w', 'grad_gmlp_b_s': 'grad_w', 'grad_attn_out_norm_g': 'grad_w', 'grad_gmlp_out_norm_g': 'grad_w', 'grad_w_out': 'grad_w', 'grad_b_out': 'grad_w', 'grad_ffn2_norm_g': 'grad_w', 'grad_ffn2_w_gate': 'grad_w', 'grad_ffn2_w_up': 'grad_w', 'grad_ffn2_w_down': 'grad_w', 'grad_final_norm_g': 'grad_w', 'delta_ffn1_norm_g': 'delta_w', 'delta_ffn1_w_gate': 'delta_w', 'delta_ffn1_w_up': 'delta_w', 'delta_ffn1_w_down': 'delta_w', 'delta_mix_norm_g': 'delta_w', 'delta_w_in': 'delta_w', 'delta_b_in': 'delta_w', 'delta_attn_sinks': 'delta_w', 'delta_gmlp_ln_g': 'delta_w', 'delta_gmlp_ln_b': 'delta_w', 'delta_gmlp_w_s': 'delta_w', 'delta_gmlp_b_s': 'delta_w', 'delta_attn_out_norm_g': 'delta_w', 'delta_gmlp_out_norm_g': 'delta_w', 'delta_w_out': 'delta_w', 'delta_b_out': 'delta_w', 'delta_ffn2_norm_g': 'delta_w', 'delta_ffn2_w_gate': 'delta_w', 'delta_ffn2_w_up': 'delta_w', 'delta_ffn2_w_down': 'delta_w', 'delta_final_norm_g': 'delta_w', 'new_m_ffn1_norm_g': 'new_m', 'new_m_ffn1_w_gate': 'new_m', 'new_m_ffn1_w_up': 'new_m', 'new_m_ffn1_w_down': 'new_m', 'new_m_mix_norm_g': 'new_m', 'new_m_w_in': 'new_m', 'new_m_b_in': 'new_m', 'new_m_attn_sinks': 'new_m', 'new_m_gmlp_ln_g': 'new_m', 'new_m_gmlp_ln_b': 'new_m', 'new_m_gmlp_w_s': 'new_m', 'new_m_gmlp_b_s': 'new_m', 'new_m_attn_out_norm_g': 'new_m', 'new_m_gmlp_out_norm_g': 'new_m', 'new_m_w_out': 'new_m', 'new_m_b_out': 'new_m', 'new_m_ffn2_norm_g': 'new_m', 'new_m_ffn2_w_gate': 'new_m', 'new_m_ffn2_w_up': 'new_m', 'new_m_ffn2_w_down': 'new_m', 'new_m_final_norm_g': 'new_m', 'new_v_ffn1_norm_g': 'new_v', 'new_v_ffn1_w_gate': 'new_v', 'new_v_ffn1_w_up': 'new_v', 'new_v_ffn1_w_down': 'new_v', 'new_v_mix_norm_g': 'new_v', 'new_v_w_in': 'new_v', 'new_v_b_in': 'new_v', 'new_v_attn_sinks': 'new_v', 'new_v_gmlp_ln_g': 'new_v', 'new_v_gmlp_ln_b': 'new_v', 'new_v_gmlp_w_s': 'new_v', 'new_v_gmlp_b_s': 'new_v', 'new_v_attn_out_norm_g': 'new_v', 'new_v_gmlp_out_norm_g': 'new_v', 'new_v_w_out': 'new_v', 'new_v_b_out': 'new_v', 'new_v_ffn2_norm_g': 'new_v', 'new_v_ffn2_w_gate': 'new_v', 'new_v_ffn2_w_up': 'new_v', 'new_v_ffn2_w_down': 'new_v', 'new_v_final_norm_g': 'new_v'}


def _forward(args):
    return _fwd_reference(*[args[k] for k in FWD_PARAMS])


def _output_shape():
    def fwd():
        inp = _fwd_setup_inputs(0)
        return _fwd_reference(*[inp[k] for k in FWD_PARAMS])
    out = _jax.eval_shape(fwd)
    return out.shape, out.dtype

N_MICROBATCH = 1
ADAM_LR = 0.001
ADAM_B1 = 0.9
ADAM_B2 = 0.999
ADAM_EPS = 1e-08
ADAM_WD = 0.01
ADAM_STEP = 10
PER_EXAMPLE_BATCH_AXIS = {'x': 0, 'loss_target': 0}
SHARED_INPUTS = []
_WEIGHT_DTYPES = {'ffn1_norm_g': _jnp.float32, 'ffn1_w_gate': _jnp.float32, 'ffn1_w_up': _jnp.float32, 'ffn1_w_down': _jnp.float32, 'mix_norm_g': _jnp.float32, 'w_in': _jnp.float32, 'b_in': _jnp.float32, 'attn_sinks': _jnp.float32, 'gmlp_ln_g': _jnp.float32, 'gmlp_ln_b': _jnp.float32, 'gmlp_w_s': _jnp.float32, 'gmlp_b_s': _jnp.float32, 'attn_out_norm_g': _jnp.float32, 'gmlp_out_norm_g': _jnp.float32, 'w_out': _jnp.float32, 'b_out': _jnp.float32, 'ffn2_norm_g': _jnp.float32, 'ffn2_w_gate': _jnp.float32, 'ffn2_w_up': _jnp.float32, 'ffn2_w_down': _jnp.float32, 'final_norm_g': _jnp.float32}
MOMENT_SCALE = {'ffn1_norm_g': 1.309756e-01, 'ffn1_w_gate': 5.679686e-02, 'ffn1_w_up': 5.513385e-02, 'ffn1_w_down': 9.133940e-02, 'mix_norm_g': 2.710758e-01, 'w_in': 1.979554e-01, 'b_in': 1.132481e+00, 'attn_sinks': 3.854464e-02, 'gmlp_ln_g': 1.006293e-01, 'gmlp_ln_b': 1.004053e-01, 'gmlp_w_s': 6.967430e-02, 'gmlp_b_s': 1.046232e-01, 'attn_out_norm_g': 2.038664e-01, 'gmlp_out_norm_g': 2.036813e-01, 'w_out': 1.860077e-01, 'b_out': 2.768564e-01, 'ffn2_norm_g': 8.506473e-02, 'ffn2_w_gate': 3.172214e-02, 'ffn2_w_up': 3.075688e-02, 'ffn2_w_down': 5.125875e-02, 'final_norm_g': 6.414976e+01}


def _to_microbatches(a, axis):
    t = _jnp.moveaxis(a, axis, 0)
    t = t.reshape((N_MICROBATCH, t.shape[0] // N_MICROBATCH) + t.shape[1:])
    return _jnp.moveaxis(t, 1, axis + 1)


def setup_inputs(seed: int = 0) -> dict:
    inp = _fwd_setup_inputs(seed)
    key = _jax.random.fold_in(_jax.random.key(seed), 7919)
    shape, _ = _output_shape()
    out = dict(inp)
    out["loss_target"] = _jax.random.normal(_jax.random.fold_in(key, 0), shape, _jnp.float32)
    for i, name in enumerate(TWIN_WEIGHTS):
        w = inp[name].astype(_jnp.float32)
        if MOMENT_SCALE is None:
            s = _jnp.sqrt(_jnp.mean(_jnp.square(w)) + 1e-30)
        else:
            s = MOMENT_SCALE[name]
        km, kv = _jax.random.split(_jax.random.fold_in(key, i + 1))
        out[name] = w
        out["m_" + name] = s * _jax.random.normal(km, w.shape, _jnp.float32)
        out["v_" + name] = (s * s) * _jax.random.uniform(kv, w.shape, _jnp.float32, 0.5, 1.5)
    if N_MICROBATCH > 1:
        for name, axis in PER_EXAMPLE_BATCH_AXIS.items():
            out[name] = _to_microbatches(out[name], axis)
    return {'x': out['x'], 'ffn1_norm_g': out['ffn1_norm_g'], 'ffn1_w_gate': out['ffn1_w_gate'], 'ffn1_w_up': out['ffn1_w_up'], 'ffn1_w_down': out['ffn1_w_down'], 'mix_norm_g': out['mix_norm_g'], 'w_in': out['w_in'], 'b_in': out['b_in'], 'attn_sinks': out['attn_sinks'], 'gmlp_ln_g': out['gmlp_ln_g'], 'gmlp_ln_b': out['gmlp_ln_b'], 'gmlp_w_s': out['gmlp_w_s'], 'gmlp_b_s': out['gmlp_b_s'], 'attn_out_norm_g': out['attn_out_norm_g'], 'gmlp_out_norm_g': out['gmlp_out_norm_g'], 'w_out': out['w_out'], 'b_out': out['b_out'], 'ffn2_norm_g': out['ffn2_norm_g'], 'ffn2_w_gate': out['ffn2_w_gate'], 'ffn2_w_up': out['ffn2_w_up'], 'ffn2_w_down': out['ffn2_w_down'], 'final_norm_g': out['final_norm_g'], 'loss_target': out['loss_target'], 'm_ffn1_norm_g': out['m_ffn1_norm_g'], 'm_ffn1_w_gate': out['m_ffn1_w_gate'], 'm_ffn1_w_up': out['m_ffn1_w_up'], 'm_ffn1_w_down': out['m_ffn1_w_down'], 'm_mix_norm_g': out['m_mix_norm_g'], 'm_w_in': out['m_w_in'], 'm_b_in': out['m_b_in'], 'm_attn_sinks': out['m_attn_sinks'], 'm_gmlp_ln_g': out['m_gmlp_ln_g'], 'm_gmlp_ln_b': out['m_gmlp_ln_b'], 'm_gmlp_w_s': out['m_gmlp_w_s'], 'm_gmlp_b_s': out['m_gmlp_b_s'], 'm_attn_out_norm_g': out['m_attn_out_norm_g'], 'm_gmlp_out_norm_g': out['m_gmlp_out_norm_g'], 'm_w_out': out['m_w_out'], 'm_b_out': out['m_b_out'], 'm_ffn2_norm_g': out['m_ffn2_norm_g'], 'm_ffn2_w_gate': out['m_ffn2_w_gate'], 'm_ffn2_w_up': out['m_ffn2_w_up'], 'm_ffn2_w_down': out['m_ffn2_w_down'], 'm_final_norm_g': out['m_final_norm_g'], 'v_ffn1_norm_g': out['v_ffn1_norm_g'], 'v_ffn1_w_gate': out['v_ffn1_w_gate'], 'v_ffn1_w_up': out['v_ffn1_w_up'], 'v_ffn1_w_down': out['v_ffn1_w_down'], 'v_mix_norm_g': out['v_mix_norm_g'], 'v_w_in': out['v_w_in'], 'v_b_in': out['v_b_in'], 'v_attn_sinks': out['v_attn_sinks'], 'v_gmlp_ln_g': out['v_gmlp_ln_g'], 'v_gmlp_ln_b': out['v_gmlp_ln_b'], 'v_gmlp_w_s': out['v_gmlp_w_s'], 'v_gmlp_b_s': out['v_gmlp_b_s'], 'v_attn_out_norm_g': out['v_attn_out_norm_g'], 'v_gmlp_out_norm_g': out['v_gmlp_out_norm_g'], 'v_w_out': out['v_w_out'], 'v_b_out': out['v_b_out'], 'v_ffn2_norm_g': out['v_ffn2_norm_g'], 'v_ffn2_w_gate': out['v_ffn2_w_gate'], 'v_ffn2_w_up': out['v_ffn2_w_up'], 'v_ffn2_w_down': out['v_ffn2_w_down'], 'v_final_norm_g': out['v_final_norm_g']}


def _loss(weights, diff, rest, loss_target):
    with _jax.named_scope("forward"):
        args = {**rest, TWIN_DIFF_INPUT: diff, **{k: w.astype(_WEIGHT_DTYPES[k]) for k, w in weights.items()}}
        y = _forward(args)
    with _jax.named_scope("loss_head"):
        err = _jnp.square(y.astype(_jnp.float32) - loss_target)
        return 0.5 * _jnp.sum(_jnp.mean(err, axis=-1)) if err.ndim else 0.5 * err


def _adamw(w, g, m, v):
    m = ADAM_B1 * m + (1.0 - ADAM_B1) * g
    v = ADAM_B2 * v + (1.0 - ADAM_B2) * _jnp.square(g)
    m_hat = m / (1.0 - ADAM_B1 ** ADAM_STEP)
    v_hat = v / (1.0 - ADAM_B2 ** ADAM_STEP)
    delta = -ADAM_LR * (m_hat / (_jnp.sqrt(v_hat) + ADAM_EPS) + ADAM_WD * w)
    return delta, m, v


def reference(x, ffn1_norm_g, ffn1_w_gate, ffn1_w_up, ffn1_w_down, mix_norm_g, w_in, b_in, attn_sinks, gmlp_ln_g, gmlp_ln_b, gmlp_w_s, gmlp_b_s, attn_out_norm_g, gmlp_out_norm_g, w_out, b_out, ffn2_norm_g, ffn2_w_gate, ffn2_w_up, ffn2_w_down, final_norm_g, loss_target, m_ffn1_norm_g, m_ffn1_w_gate, m_ffn1_w_up, m_ffn1_w_down, m_mix_norm_g, m_w_in, m_b_in, m_attn_sinks, m_gmlp_ln_g, m_gmlp_ln_b, m_gmlp_w_s, m_gmlp_b_s, m_attn_out_norm_g, m_gmlp_out_norm_g, m_w_out, m_b_out, m_ffn2_norm_g, m_ffn2_w_gate, m_ffn2_w_up, m_ffn2_w_down, m_final_norm_g, v_ffn1_norm_g, v_ffn1_w_gate, v_ffn1_w_up, v_ffn1_w_down, v_mix_norm_g, v_w_in, v_b_in, v_attn_sinks, v_gmlp_ln_g, v_gmlp_ln_b, v_gmlp_w_s, v_gmlp_b_s, v_attn_out_norm_g, v_gmlp_out_norm_g, v_w_out, v_b_out, v_ffn2_norm_g, v_ffn2_w_gate, v_ffn2_w_up, v_ffn2_w_down, v_final_norm_g):
    given = dict(x=x, ffn1_norm_g=ffn1_norm_g, ffn1_w_gate=ffn1_w_gate, ffn1_w_up=ffn1_w_up, ffn1_w_down=ffn1_w_down, mix_norm_g=mix_norm_g, w_in=w_in, b_in=b_in, attn_sinks=attn_sinks, gmlp_ln_g=gmlp_ln_g, gmlp_ln_b=gmlp_ln_b, gmlp_w_s=gmlp_w_s, gmlp_b_s=gmlp_b_s, attn_out_norm_g=attn_out_norm_g, gmlp_out_norm_g=gmlp_out_norm_g, w_out=w_out, b_out=b_out, ffn2_norm_g=ffn2_norm_g, ffn2_w_gate=ffn2_w_gate, ffn2_w_up=ffn2_w_up, ffn2_w_down=ffn2_w_down, final_norm_g=final_norm_g, loss_target=loss_target, m_ffn1_norm_g=m_ffn1_norm_g, m_ffn1_w_gate=m_ffn1_w_gate, m_ffn1_w_up=m_ffn1_w_up, m_ffn1_w_down=m_ffn1_w_down, m_mix_norm_g=m_mix_norm_g, m_w_in=m_w_in, m_b_in=m_b_in, m_attn_sinks=m_attn_sinks, m_gmlp_ln_g=m_gmlp_ln_g, m_gmlp_ln_b=m_gmlp_ln_b, m_gmlp_w_s=m_gmlp_w_s, m_gmlp_b_s=m_gmlp_b_s, m_attn_out_norm_g=m_attn_out_norm_g, m_gmlp_out_norm_g=m_gmlp_out_norm_g, m_w_out=m_w_out, m_b_out=m_b_out, m_ffn2_norm_g=m_ffn2_norm_g, m_ffn2_w_gate=m_ffn2_w_gate, m_ffn2_w_up=m_ffn2_w_up, m_ffn2_w_down=m_ffn2_w_down, m_final_norm_g=m_final_norm_g, v_ffn1_norm_g=v_ffn1_norm_g, v_ffn1_w_gate=v_ffn1_w_gate, v_ffn1_w_up=v_ffn1_w_up, v_ffn1_w_down=v_ffn1_w_down, v_mix_norm_g=v_mix_norm_g, v_w_in=v_w_in, v_b_in=v_b_in, v_attn_sinks=v_attn_sinks, v_gmlp_ln_g=v_gmlp_ln_g, v_gmlp_ln_b=v_gmlp_ln_b, v_gmlp_w_s=v_gmlp_w_s, v_gmlp_b_s=v_gmlp_b_s, v_attn_out_norm_g=v_attn_out_norm_g, v_gmlp_out_norm_g=v_gmlp_out_norm_g, v_w_out=v_w_out, v_b_out=v_b_out, v_ffn2_norm_g=v_ffn2_norm_g, v_ffn2_w_gate=v_ffn2_w_gate, v_ffn2_w_up=v_ffn2_w_up, v_ffn2_w_down=v_ffn2_w_down, v_final_norm_g=v_final_norm_g)
    weights = {n: given[n] for n in TWIN_WEIGHTS}
    shared = {n: given[n] for n in SHARED_INPUTS}
    per_example = {n: given[n] for n in ['x']}
    grad_fn = _jax.value_and_grad(_loss, argnums=(0, 1))

    def one_microbatch(ex, loss_target):
        ex = dict(ex)
        diff = ex.pop(TWIN_DIFF_INPUT)
        return grad_fn(weights, diff, {**shared, **ex}, loss_target)

    if N_MICROBATCH == 1:
        loss, (grad_w, grad_x) = one_microbatch(per_example, given["loss_target"])
    else:
        def body(carry, xs):
            loss_sum, grad_sum = carry
            l_k, (gw_k, gx_k) = one_microbatch(xs[0], xs[1])
            with _jax.named_scope("update"):
                return (loss_sum + l_k, _jax.tree.map(_jnp.add, grad_sum, gw_k)), gx_k

        init = (_jnp.zeros((), _jnp.float32), _jax.tree.map(_jnp.zeros_like, weights))
        (loss, grad_w), grad_x = _jax.lax.scan(body, init, (per_example, given["loss_target"]))
    with _jax.named_scope("update"):
        delta_w, new_m, new_v = {}, {}, {}
        for n in TWIN_WEIGHTS:
            delta_w[n], new_m[n], new_v[n] = _adamw(weights[n], grad_w[n], given["m_" + n], given["v_" + n])
    return (loss, grad_x, *[grad_w[n] for n in TWIN_WEIGHTS], *[delta_w[n] for n in TWIN_WEIGHTS],
            *[new_m[n] for n in TWIN_WEIGHTS], *[new_v[n] for n in TWIN_WEIGHTS])
```

```python
import functools

import jax
import jax.numpy as jnp
from jax import lax
from jax.experimental import pallas as pl
from jax.experimental.pallas import tpu as pltpu

F32 = jnp.float32
BF16 = jnp.bfloat16

D_MODEL = 1024
D_FF = 2816
N_CHIPS = 4
FF_SH = D_FF // N_CHIPS
N_Q_HEADS = 8
N_KV_HEADS = 2
REP = N_Q_HEADS // N_KV_HEADS
HEAD_DIM = 64
ATTN_W = 512
KV_W = 128
GMLP_W = 512
GMLP_GROUPS = 8
GROUP_DIM = 64
BLK = 128
IN_W = 1792
IN_SH = IN_W // N_CHIPS
OUT_SH = D_MODEL // N_CHIPS
EPS = 1e-6
FFN_RES = 0.5
ATTN_SCALE = HEAD_DIM ** -0.5

ADAM_LR = 0.001
ADAM_B1 = 0.9
ADAM_B2 = 0.999
ADAM_EPS = 1e-08
ADAM_WD = 0.01
ADAM_STEP = 10

V7X_VMEM_LIMIT = 56 * 1024 * 1024
MESH = pl.DeviceIdType.MESH


def _cparams(sem):
    return pltpu.CompilerParams(dimension_semantics=sem, vmem_limit_bytes=V7X_VMEM_LIMIT)


def _dot(a, b):
    return jnp.dot(a, b, preferred_element_type=F32)


def _dot_nt(a, b):
    return lax.dot_general(a, b, (((1,), (1,)), ((), ())), preferred_element_type=F32)


def _dot_tn(a, b):
    return lax.dot_general(a, b, (((0,), (0,)), ((), ())), preferred_element_type=F32)


def _rms(x, g):
    r = lax.rsqrt(jnp.mean(x * x, axis=-1, keepdims=True) + EPS)
    return x * r * g, r


def _rms_bwd(dh, x, r, g):
    gy = dh * g
    dx = r * gy - x * (r * r * r) * jnp.mean(gy * x, axis=-1, keepdims=True)
    dg = jnp.sum(dh * x * r, axis=0, keepdims=True)
    return dx, dg


def _const(shape):
    nd = len(shape)
    return pl.BlockSpec(shape, lambda *_: (0,) * nd)


def _rows(t, w):
    return pl.BlockSpec((t, w), lambda i: (i, 0))


PACK_ROWS = 7 * FF_SH
HALF_ROWS = PACK_ROWS // 2
RS_TILE = HALF_ROWS // 7
MIX_BLOCK = 6
BIG = ("ffn1_w_gate", "ffn1_w_up", "ffn1_w_down", "ffn2_w_gate", "ffn2_w_up", "ffn2_w_down", "w_in", "w_out")
BIG_ROWS = (FF_SH, FF_SH, FF_SH, FF_SH, FF_SH, FF_SH, IN_SH, OUT_SH)
BIG_TRANSPOSED = (True, True, False, True, True, False, True, False)

SMALL = (("ffn1_norm_g", 1024), ("mix_norm_g", 1024), ("b_in", 1792), ("attn_sinks", 8), ("gmlp_ln_g", 512),
         ("gmlp_ln_b", 512), ("gmlp_w_s", 131072), ("gmlp_b_s", 1024), ("attn_out_norm_g", 512),
         ("gmlp_out_norm_g", 512), ("b_out", 1024), ("ffn2_norm_g", 1024), ("final_norm_g", 1024), ("loss", 1))


def _small_rows(n):
    return -(-n // 1024) * 8


SMALL_ROWS = sum(_small_rows(n) for _, n in SMALL)


def _pack_small(parts):
    out = []
    for name, n in SMALL:
        flat = parts[name].reshape(-1).astype(F32)
        rows = _small_rows(n)
        out.append(jnp.pad(flat, (0, rows * 128 - n)).reshape(rows, 128))
    return jnp.concatenate(out, axis=0)


def _unpack_small(packed, shapes):
    res, off = {}, 0
    for name, n in SMALL:
        rows = _small_rows(n)
        res[name] = packed[off:off + rows].reshape(-1)[:n].reshape(shapes[name])
        off += rows
    return res


def _ffn_tile(x, g, wg_ref, wu_ref, wd_ref):
    h, _ = _rms(x, g)
    hb = h.astype(BF16)
    acc = jnp.zeros(x.shape, F32)
    for j in range(N_CHIPS):
        a = _dot_nt(hb, wg_ref[j])
        b = _dot_nt(hb, wu_ref[j])
        f = (a * jax.nn.sigmoid(a) * b).astype(BF16)
        acc = acc + _dot(f, wd_ref[j])
    return x + FFN_RES * acc


def _ffn_weight_specs(k0):
    one = pl.Buffered(1)
    return [pl.BlockSpec((N_CHIPS, FF_SH, D_MODEL), functools.partial(lambda kk, i: (0, kk, 0), k0 + d),
                         pipeline_mode=one) for d in range(3)]


def _ffn_fwd(x, g, pack, k0, *, tile, name):
    s = x.shape[0]

    def body(x_ref, g_ref, wg_ref, wu_ref, wd_ref, o_ref):
        o_ref[...] = _ffn_tile(x_ref[...], g_ref[...], wg_ref, wu_ref, wd_ref)

    return pl.pallas_call(
        body, name=name, grid=(s // tile,),
        in_specs=[_rows(tile, D_MODEL), _const((1, D_MODEL))] + _ffn_weight_specs(k0),
        out_specs=_rows(tile, D_MODEL),
        out_shape=jax.ShapeDtypeStruct(x.shape, F32),
        compiler_params=_cparams(("arbitrary",)),
    )(x, g, pack, pack, pack)


def _ffn_fwd_loss(x, g, pack, k0, gf, tgt, *, tile, name):
    s = x.shape[0]

    def body(x_ref, g_ref, wg_ref, wu_ref, wd_ref, gf_ref, t_ref, dx_ref, loss_ref, dgf_ref):
        @pl.when(pl.program_id(0) == 0)
        def _():
            loss_ref[...] = jnp.zeros_like(loss_ref)
            dgf_ref[...] = jnp.zeros_like(dgf_ref)

        x3 = _ffn_tile(x_ref[...], g_ref[...], wg_ref, wu_ref, wd_ref)
        gf_v = gf_ref[...]
        out, r = _rms(x3, gf_v)
        diff = out - t_ref[...]
        part = jnp.sum(jnp.sum(diff * diff, axis=-1, keepdims=True), axis=0, keepdims=True)
        loss_ref[...] += jnp.broadcast_to(part * (0.5 / D_MODEL), loss_ref.shape)
        dx, dg = _rms_bwd(diff * (1.0 / D_MODEL), x3, r, gf_v)
        dx_ref[...] = dx
        dgf_ref[...] += dg

    return pl.pallas_call(
        body, name=name, grid=(s // tile,),
        in_specs=[_rows(tile, D_MODEL), _const((1, D_MODEL))] + _ffn_weight_specs(k0)
                 + [_const((1, D_MODEL)), _rows(tile, D_MODEL)],
        out_specs=[_rows(tile, D_MODEL), _const((1, 128)), _const((1, D_MODEL))],
        out_shape=[jax.ShapeDtypeStruct(x.shape, F32),
                   jax.ShapeDtypeStruct((1, 128), F32),
                   jax.ShapeDtypeStruct((1, D_MODEL), F32)],
        compiler_params=_cparams(("arbitrary",)),
    )(x, g, pack, pack, pack, gf, tgt)


def _ffn_bwd(x, dy, g, pack, k0, grads, *, tile, name):
    s = x.shape[0]

    def body(x_ref, dy_ref, g_ref, wg_ref, wu_ref, wd_ref, *rest):
        dhp_ref, gw_ref = rest[-2], rest[-1]

        @pl.when(pl.program_id(1) == 0)
        def _():
            gw_ref[...] = jnp.zeros_like(gw_ref)

        h, _ = _rms(x_ref[...], g_ref[...])
        hb = h.astype(BF16)
        dob = (FFN_RES * dy_ref[...]).astype(BF16)
        wg_j, wu_j, wd_j = wg_ref[0], wu_ref[0], wd_ref[0]
        a = _dot_nt(hb, wg_j)
        b = _dot_nt(hb, wu_j)
        sg = jax.nn.sigmoid(a)
        sa = a * sg
        fb = (sa * b).astype(BF16)
        df = _dot_nt(dob, wd_j)
        dbb = (df * sa).astype(BF16)
        dab = (df * b * (sg + sa * (1.0 - sg))).astype(BF16)
        dhp_ref[0] = _dot(dab, wg_j) + _dot(dbb, wu_j)
        gw_ref[0, 0:FF_SH, :] += _dot_tn(dab, hb)
        gw_ref[0, FF_SH:2 * FF_SH, :] += _dot_tn(dbb, hb)
        gw_ref[0, 2 * FF_SH:3 * FF_SH, :] += _dot_tn(fb, dob)

    wspecs = [pl.BlockSpec((1, FF_SH, D_MODEL), functools.partial(lambda kk, j, i: (j, kk, 0), k0 + d))
              for d in range(3)]
    xspec = pl.BlockSpec((tile, D_MODEL), lambda j, i: (i, 0))
    in_specs = [xspec, xspec, pl.BlockSpec((1, D_MODEL), lambda j, i: (0, 0))] + wspecs
    operands = [x, dy, g, pack, pack, pack]
    aliases = {}
    if grads is not None:
        in_specs.append(pl.BlockSpec(memory_space=pl.ANY))
        operands.append(grads)
        aliases = {6: 1}
    k3 = k0 // 3
    return pl.pallas_call(
        body, name=name, grid=(N_CHIPS, s // tile),
        in_specs=in_specs,
        out_specs=[pl.BlockSpec((1, tile, D_MODEL), lambda j, i: (j, i, 0)),
                   pl.BlockSpec((1, 3 * FF_SH, D_MODEL), lambda j, i: (j, k3, 0))],
        out_shape=[jax.ShapeDtypeStruct((N_CHIPS, s, D_MODEL), F32),
                   jax.ShapeDtypeStruct((N_CHIPS, PACK_ROWS, D_MODEL), F32)],
        input_output_aliases=aliases,
        compiler_params=_cparams(("arbitrary", "arbitrary")),
    )(*operands)


def _mix_grads_pack(dw_in_t, dw_out, grads, *, name):
    def body(a_ref, b_ref, g_any, o_ref):
        o_ref[0, 0:IN_SH, :] = a_ref[0]
        o_ref[0, IN_SH:FF_SH, :] = b_ref[0]

    return pl.pallas_call(
        body, name=name, grid=(N_CHIPS,),
        in_specs=[pl.BlockSpec((1, IN_SH, D_MODEL), lambda j: (j, 0, 0)),
                  pl.BlockSpec((1, OUT_SH, D_MODEL), lambda j: (j, 0, 0)),
                  pl.BlockSpec(memory_space=pl.ANY)],
        out_specs=pl.BlockSpec((1, FF_SH, D_MODEL), lambda j: (j, MIX_BLOCK, 0)),
        out_shape=jax.ShapeDtypeStruct((N_CHIPS, PACK_ROWS, D_MODEL), F32),
        input_output_aliases={2: 0},
        compiler_params=_cparams(("arbitrary",)),
    )(dw_in_t.reshape(N_CHIPS, IN_SH, D_MODEL), dw_out.reshape(N_CHIPS, OUT_SH, D_MODEL), grads)


def _norm_bwd(dhp, x, dy, g, *, tile, name):
    s = x.shape[0]

    def body(dhp_ref, x_ref, dy_ref, g_ref, dx_ref, dg_ref):
        @pl.when(pl.program_id(0) == 0)
        def _():
            dg_ref[...] = jnp.zeros_like(dg_ref)

        dh = (dhp_ref[0] + dhp_ref[1]) + (dhp_ref[2] + dhp_ref[3])
        x_v = x_ref[...]
        r = lax.rsqrt(jnp.mean(x_v * x_v, axis=-1, keepdims=True) + EPS)
        dx, dg = _rms_bwd(dh, x_v, r, g_ref[...])
        dx_ref[...] = dy_ref[...] + dx
        dg_ref[...] += dg

    return pl.pallas_call(
        body, name=name, grid=(s // tile,),
        in_specs=[pl.BlockSpec((N_CHIPS, tile, D_MODEL), lambda i: (0, i, 0)),
                  _rows(tile, D_MODEL), _rows(tile, D_MODEL), _const((1, D_MODEL))],
        out_specs=[_rows(tile, D_MODEL), _const((1, D_MODEL))],
        out_shape=[jax.ShapeDtypeStruct(x.shape, F32), jax.ShapeDtypeStruct((1, D_MODEL), F32)],
        compiler_params=_cparams(("arbitrary",)),
    )(dhp, x, dy, g)


def _mix_in_fwd(x, g, w_in_t, b_in, *, tile, name):
    s = x.shape[0]

    def body(x_ref, g_ref, w_ref, b_ref, q_ref, k_ref, v_ref, z_ref):
        h, _ = _rms(x_ref[...], g_ref[...])
        proj = _dot_nt(h.astype(BF16), w_ref[...]) + b_ref[...]
        q_ref[...] = proj[:, :ATTN_W].astype(BF16)
        k_ref[...] = proj[:, ATTN_W:ATTN_W + KV_W].astype(BF16)
        v_ref[...] = proj[:, ATTN_W + KV_W:ATTN_W + 2 * KV_W].astype(BF16)
        z_ref[...] = proj[:, ATTN_W + 2 * KV_W:]

    return pl.pallas_call(
        body, name=name, grid=(s // tile,),
        in_specs=[_rows(tile, D_MODEL), _const((1, D_MODEL)), _const((IN_W, D_MODEL)), _const((1, IN_W))],
        out_specs=[_rows(tile, ATTN_W), _rows(tile, KV_W), _rows(tile, KV_W), _rows(tile, 2 * GMLP_W)],
        out_shape=[jax.ShapeDtypeStruct((s, ATTN_W), BF16), jax.ShapeDtypeStruct((s, KV_W), BF16),
                   jax.ShapeDtypeStruct((s, KV_W), BF16), jax.ShapeDtypeStruct((s, 2 * GMLP_W), F32)],
        compiler_params=_cparams(("arbitrary",)),
    )(x, g, w_in_t, b_in)


def _mix_in_bwd(x, dy, dq, dk, dv, dz, g, w_in_t, *, tile, name):
    s = x.shape[0]

    def body(x_ref, dy_ref, dq_ref, dk_ref, dv_ref, dz_ref, g_ref, w_ref, dx_ref, dw_ref, db_ref, dg_ref):
        @pl.when(pl.program_id(0) == 0)
        def _():
            dw_ref[...] = jnp.zeros_like(dw_ref)
            db_ref[...] = jnp.zeros_like(db_ref)
            dg_ref[...] = jnp.zeros_like(dg_ref)

        dproj = jnp.concatenate([dq_ref[...], dk_ref[...], dv_ref[...], dz_ref[...]], axis=-1)
        db_ref[...] += jnp.sum(dproj, axis=0, keepdims=True)
        dpb = dproj.astype(BF16)
        x_v = x_ref[...]
        g_v = g_ref[...]
        h, r = _rms(x_v, g_v)
        dw_ref[...] += _dot_tn(dpb, h.astype(BF16))
        dh = _dot(dpb, w_ref[...])
        dx, dg = _rms_bwd(dh, x_v, r, g_v)
        dx_ref[...] = dy_ref[...] + dx
        dg_ref[...] += dg

    return pl.pallas_call(
        body, name=name, grid=(s // tile,),
        in_specs=[_rows(tile, D_MODEL), _rows(tile, D_MODEL), _rows(tile, ATTN_W), _rows(tile, KV_W),
                  _rows(tile, KV_W), _rows(tile, 2 * GMLP_W), _const((1, D_MODEL)), _const((IN_W, D_MODEL))],
        out_specs=[_rows(tile, D_MODEL), _const((IN_W, D_MODEL)), _const((1, IN_W)), _const((1, D_MODEL))],
        out_shape=[jax.ShapeDtypeStruct(x.shape, F32), jax.ShapeDtypeStruct((IN_W, D_MODEL), F32),
                   jax.ShapeDtypeStruct((1, IN_W), F32), jax.ShapeDtypeStruct((1, D_MODEL), F32)],
        compiler_params=_cparams(("arbitrary",)),
    )(x, dy, dq, dk, dv, dz, g, w_in_t)


_GELU_C = 0.7978845608028654
_GELU_A = 0.044715


def _gelu(x):
    return 0.5 * x * (1.0 + jnp.tanh(_GELU_C * (x + _GELU_A * (x * x * x))))


def _gelu_grad(x):
    t = jnp.tanh(_GELU_C * (x + _GELU_A * (x * x * x)))
    return 0.5 * (1.0 + t) + 0.5 * x * (1.0 - t * t) * (_GELU_C * (1.0 + 3.0 * _GELU_A * (x * x)))


def _band(ref, i):
    prev = jnp.maximum(i - 1, 0)
    return jnp.concatenate([ref[pl.ds(pl.multiple_of(prev * BLK, BLK), BLK), :],
                            ref[pl.ds(pl.multiple_of(i * BLK, BLK), BLK), :]], axis=0)


def _band_mask(i):
    qpos = lax.broadcasted_iota(jnp.int32, (BLK, 2 * BLK), 0)
    kidx = lax.broadcasted_iota(jnp.int32, (BLK, 2 * BLK), 1)
    rel = qpos - kidx + BLK
    win = jnp.where(rel >= 0, jnp.where(rel < BLK, 1, 0), 0)
    real = jnp.where(kidx >= BLK, 1, jnp.where(i > 0, 1, 0))
    return (win * real) > 0


def _attn_probs(qh, kg, mask, sink):
    sc = _dot_nt(qh, kg) * ATTN_SCALE
    sc = jnp.where(mask, sc, -jnp.inf)
    m = jnp.maximum(jnp.max(sc, axis=-1, keepdims=True), sink)
    p = jnp.exp(sc - m)
    es = jnp.exp(sink - m)
    inv = 1.0 / (jnp.sum(p, axis=-1, keepdims=True) + es)
    return p * inv, es * inv


def _tril_mask():
    t = lax.broadcasted_iota(jnp.int32, (BLK, BLK), 0)
    s_ = lax.broadcasted_iota(jnp.int32, (BLK, BLK), 1)
    return s_ <= t


def _gmlp_fwd_parts(zg, lng, lnb, ws_ref, bs_full):
    z = _gelu(zg)
    u = z[:, :GMLP_W]
    zv = z[:, GMLP_W:]
    mu = jnp.mean(zv, axis=-1, keepdims=True)
    zc = zv - mu
    rstd = lax.rsqrt(jnp.mean(zc * zc, axis=-1, keepdims=True) + EPS)
    xh = zc * rstd
    vvb = (xh * lng + lnb).astype(BF16)
    tril = _tril_mask()
    wms, parts = [], []
    for gi in range(GMLP_GROUPS):
        wm = jnp.where(tril, ws_ref[gi], 0.0).astype(BF16)
        wms.append(wm)
        parts.append(_dot(wm, vvb[:, gi * GROUP_DIM:(gi + 1) * GROUP_DIM]))
    mixed = jnp.concatenate(parts, axis=-1) + bs_full
    return u, xh, rstd, vvb, wms, mixed


def _attn_fwd(q, kb, vb, mask, sink_ref):
    outs = []
    for h in range(N_Q_HEADS):
        gi = h // REP
        pn, _ = _attn_probs(q[:, h * HEAD_DIM:(h + 1) * HEAD_DIM], kb[:, gi * HEAD_DIM:(gi + 1) * HEAD_DIM],
                            mask, sink_ref[h])
        outs.append(_dot(pn.astype(BF16), vb[:, gi * HEAD_DIM:(gi + 1) * HEAD_DIM]))
    return jnp.concatenate(outs, axis=-1)


def _mix_core_fwd(x1, q, k, v, zg, sinks, lng, lnb, w_s, bs_full, gao, ggo, w_out, b_out, *, name):
    s = x1.shape[0]

    def body(sink_ref, x_ref, q_ref, k_ref, v_ref, z_ref, lng_ref, lnb_ref, ws_ref, bs_ref, gao_ref, ggo_ref,
             wo_ref, bo_ref, o_ref):
        i = pl.program_id(0)
        mask = _band_mask(i)
        y_attn = _attn_fwd(q_ref[...], _band(k_ref, i), _band(v_ref, i), mask, sink_ref)
        u, _, _, _, _, mixed = _gmlp_fwd_parts(z_ref[...], lng_ref[...], lnb_ref[...], ws_ref, bs_ref[...])
        ya, _ = _rms(y_attn, gao_ref[...])
        yg, _ = _rms(u * mixed, ggo_ref[...])
        yb = jnp.concatenate([ya, yg], axis=-1).astype(BF16)
        o_ref[...] = x_ref[...] + (_dot(yb, wo_ref[...]) + bo_ref[...])

    return pl.pallas_call(
        body, name=name, grid=(s // BLK,),
        in_specs=[pl.BlockSpec(memory_space=pltpu.SMEM),
                  _rows(BLK, D_MODEL), _rows(BLK, ATTN_W), _const((s, KV_W)), _const((s, KV_W)),
                  _rows(BLK, 2 * GMLP_W), _const((1, GMLP_W)), _const((1, GMLP_W)),
                  _const((GMLP_GROUPS, BLK, BLK)), _const((BLK, GMLP_W)), _const((1, ATTN_W)), _const((1, GMLP_W)),
                  _const((D_MODEL, D_MODEL)), _const((1, D_MODEL))],
        out_specs=_rows(BLK, D_MODEL),
        out_shape=jax.ShapeDtypeStruct(x1.shape, F32),
        compiler_params=_cparams(("arbitrary",)),
    )(sinks, x1, q, k, v, zg, lng, lnb, w_s, bs_full, gao, ggo, w_out, b_out)


def _mix_core_bwd(dy, q, k, v, zg, sinks, lng, lnb, w_s, bs_full, gao, ggo, w_out, *, name):
    s = dy.shape[0]
    nblk = s // BLK

    def body(sink_ref, dy_ref, q_ref, k_ref, v_ref, z_ref, lng_ref, lnb_ref, ws_ref, bs_ref, gao_ref, ggo_ref,
             wo_ref, dq_ref, dk_ref, dv_ref, dz_ref, dwo_ref, dbo_ref, dgao_ref, dggo_ref, dlng_ref, dlnb_ref,
             dws_ref, dms_ref, dsk_ref):
        i = pl.program_id(0)

        @pl.when(i == 0)
        def _():
            for ref in (dk_ref, dv_ref, dwo_ref, dbo_ref, dgao_ref, dggo_ref, dlng_ref, dlnb_ref, dws_ref,
                        dms_ref, dsk_ref):
                ref[...] = jnp.zeros_like(ref)

        mask = _band_mask(i)
        q_v = q_ref[...]
        kb = _band(k_ref, i)
        vb = _band(v_ref, i)
        lng_v = lng_ref[...]
        gao_v = gao_ref[...]
        ggo_v = ggo_ref[...]
        zg_v = z_ref[...]

        y_attn = _attn_fwd(q_v, kb, vb, mask, sink_ref)
        u, xh, rstd, vvb, wms, mixed = _gmlp_fwd_parts(zg_v, lng_v, lnb_ref[...], ws_ref, bs_ref[...])
        y_gmlp = u * mixed
        ya, ra = _rms(y_attn, gao_v)
        yg, rg = _rms(y_gmlp, ggo_v)
        yb = jnp.concatenate([ya, yg], axis=-1).astype(BF16)

        dy_v = dy_ref[...]
        dyb = dy_v.astype(BF16)
        dwo_ref[...] += _dot_tn(yb, dyb)
        dbo_ref[...] += jnp.sum(dy_v, axis=0, keepdims=True)
        dyy = _dot_nt(dyb, wo_ref[...])
        d_attn, dgao = _rms_bwd(dyy[:, :ATTN_W], y_attn, ra, gao_v)
        d_gmlp, dggo = _rms_bwd(dyy[:, ATTN_W:], y_gmlp, rg, ggo_v)
        dgao_ref[...] += dgao
        dggo_ref[...] += dggo

        du = d_gmlp * mixed
        dmixed = d_gmlp * u
        dms_ref[...] += dmixed
        dmb = dmixed.astype(BF16)
        dvv_parts = []
        for gi in range(GMLP_GROUPS):
            sl = slice(gi * GROUP_DIM, (gi + 1) * GROUP_DIM)
            dws_ref[gi] += _dot_nt(dmb[:, sl], vvb[:, sl])
            dvv_parts.append(_dot_tn(wms[gi], dmb[:, sl]))
        dvv = jnp.concatenate(dvv_parts, axis=-1)
        dlng_ref[...] += jnp.sum(dvv * xh, axis=0, keepdims=True)
        dlnb_ref[...] += jnp.sum(dvv, axis=0, keepdims=True)
        dxh = dvv * lng_v
        dzv = rstd * (dxh - jnp.mean(dxh, axis=-1, keepdims=True)
                      - xh * jnp.mean(dxh * xh, axis=-1, keepdims=True))
        dz_ref[...] = jnp.concatenate([du, dzv], axis=-1) * _gelu_grad(zg_v)

        dab = d_attn.astype(BF16)
        dq_parts = []
        dk_parts = []
        dv_parts = []
        for gi in range(N_KV_HEADS):
            kg = kb[:, gi * HEAD_DIM:(gi + 1) * HEAD_DIM]
            vg = vb[:, gi * HEAD_DIM:(gi + 1) * HEAD_DIM]
            dkg = jnp.zeros((2 * BLK, HEAD_DIM), F32)
            dvg = jnp.zeros((2 * BLK, HEAD_DIM), F32)
            for rr in range(REP):
                h = gi * REP + rr
                hs = slice(h * HEAD_DIM, (h + 1) * HEAD_DIM)
                qh = q_v[:, hs]
                doh = dab[:, hs]
                pn, psink = _attn_probs(qh, kg, mask, sink_ref[h])
                dp = _dot_nt(doh, vg)
                delta = jnp.sum(pn * dp, axis=-1, keepdims=True)
                dsb = (pn * (dp - delta) * ATTN_SCALE).astype(BF16)
                dsink = jnp.sum(-psink * delta, axis=0, keepdims=True)
                dsk_ref[pl.ds(h, 1), :] += jnp.broadcast_to(dsink, (1, 128))
                dq_parts.append(_dot(dsb, kg))
                dkg = dkg + _dot_tn(dsb, qh)
                dvg = dvg + _dot_tn(pn.astype(BF16), doh)
            dk_parts.append(dkg)
            dv_parts.append(dvg)
        dq_ref[...] = jnp.concatenate(dq_parts, axis=-1)
        dkb = jnp.concatenate(dk_parts, axis=-1)
        dvb = jnp.concatenate(dv_parts, axis=-1)
        prev = pl.ds(pl.multiple_of(jnp.maximum(i - 1, 0) * BLK, BLK), BLK)
        cur = pl.ds(pl.multiple_of(i * BLK, BLK), BLK)
        dk_ref[prev, :] += dkb[:BLK]
        dv_ref[prev, :] += dvb[:BLK]
        dk_ref[cur, :] += dkb[BLK:]
        dv_ref[cur, :] += dvb[BLK:]

        @pl.when(i == nblk - 1)
        def _():
            tril = _tril_mask()
            for gi in range(GMLP_GROUPS):
                dws_ref[gi] = jnp.where(tril, dws_ref[gi], 0.0)

    return pl.pallas_call(
        body, name=name, grid=(nblk,),
        in_specs=[pl.BlockSpec(memory_space=pltpu.SMEM),
                  _rows(BLK, D_MODEL), _rows(BLK, ATTN_W), _const((s, KV_W)), _const((s, KV_W)),
                  _rows(BLK, 2 * GMLP_W), _const((1, GMLP_W)), _const((1, GMLP_W)),
                  _const((GMLP_GROUPS, BLK, BLK)), _const((BLK, GMLP_W)), _const((1, ATTN_W)), _const((1, GMLP_W)),
                  _const((D_MODEL, D_MODEL))],
        out_specs=[_rows(BLK, ATTN_W), _const((s, KV_W)), _const((s, KV_W)), _rows(BLK, 2 * GMLP_W),
                   _const((D_MODEL, D_MODEL)), _const((1, D_MODEL)), _const((1, ATTN_W)), _const((1, GMLP_W)),
                   _const((1, GMLP_W)), _const((1, GMLP_W)), _const((GMLP_GROUPS, BLK, BLK)),
                   _const((BLK, GMLP_W)), _const((N_Q_HEADS, 128))],
        out_shape=[jax.ShapeDtypeStruct((s, ATTN_W), F32), jax.ShapeDtypeStruct((s, KV_W), F32),
                   jax.ShapeDtypeStruct((s, KV_W), F32), jax.ShapeDtypeStruct((s, 2 * GMLP_W), F32),
                   jax.ShapeDtypeStruct((D_MODEL, D_MODEL), F32), jax.ShapeDtypeStruct((1, D_MODEL), F32),
                   jax.ShapeDtypeStruct((1, ATTN_W), F32), jax.ShapeDtypeStruct((1, GMLP_W), F32),
                   jax.ShapeDtypeStruct((1, GMLP_W), F32), jax.ShapeDtypeStruct((1, GMLP_W), F32),
                   jax.ShapeDtypeStruct((GMLP_GROUPS, BLK, BLK), F32), jax.ShapeDtypeStruct((BLK, GMLP_W), F32),
                   jax.ShapeDtypeStruct((N_Q_HEADS, 128), F32)],
        compiler_params=_cparams(("arbitrary",)),
    )(sinks, dy, q, k, v, zg, lng, lnb, w_s, bs_full, gao, ggo, w_out)


def _local_step(x, tgt, p, pack, *, tile=512, bwd_tile=256):
    g = {}
    tile, bwd_tile = min(tile, x.shape[0]), min(bwd_tile, x.shape[0])
    x1 = _ffn_fwd(x, p["ffn1_norm_g"], pack, 0, tile=tile, name="ffn1_fwd")
    q, k, v, zg = _mix_in_fwd(x1, p["mix_norm_g"], p["w_in_t"], p["b_in"], tile=tile, name="mix_in_fwd")
    mix_args = (q, k, v, zg, p["attn_sinks"], p["gmlp_ln_g"], p["gmlp_ln_b"], p["gmlp_w_s"], p["bs_full"],
                p["attn_out_norm_g"], p["gmlp_out_norm_g"], p["w_out"])
    x2 = _mix_core_fwd(x1, *mix_args, p["b_out"], name="mix_core_fwd")
    dx3, loss, g["final_norm_g"] = _ffn_fwd_loss(x2, p["ffn2_norm_g"], pack, 3, p["final_norm_g"], tgt,
                                                 tile=tile, name="ffn2_fwd_loss")

    dhp, grads = _ffn_bwd(x2, dx3, p["ffn2_norm_g"], pack, 3, None, tile=bwd_tile, name="ffn2_bwd")
    dx2, g["ffn2_norm_g"] = _norm_bwd(dhp, x2, dx3, p["ffn2_norm_g"], tile=bwd_tile, name="ffn2_norm_bwd")

    (dq, dk, dv, dz, dw_out, g["b_out"], g["attn_out_norm_g"], g["gmlp_out_norm_g"], g["gmlp_ln_g"],
     g["gmlp_ln_b"], g["gmlp_w_s"], dmix_sum, dsinks) = _mix_core_bwd(dx2, *mix_args, name="mix_core_bwd")
    g["gmlp_b_s"] = dmix_sum
    g["attn_sinks"] = dsinks
    dx1, dw_in_t, g["b_in"], g["mix_norm_g"] = _mix_in_bwd(
        x1, dx2, dq, dk, dv, dz, p["mix_norm_g"], p["w_in_t"], tile=tile, name="mix_in_bwd")
    grads = _mix_grads_pack(dw_in_t, dw_out, grads, name="mix_grads_pack")

    dhp1, grads = _ffn_bwd(x, dx1, p["ffn1_norm_g"], pack, 0, grads, tile=bwd_tile, name="ffn1_bwd")
    dx0, g["ffn1_norm_g"] = _norm_bwd(dhp1, x, dx1, p["ffn1_norm_g"], tile=bwd_tile, name="ffn1_norm_bwd")
    return loss, dx0, grads, g


def _pack_cast(parts, *, name):
    def body(*refs):
        o_ref = refs[-1]
        off = 0
        for ref, rows in zip(refs[:-1], BIG_ROWS):
            o_ref[off:off + rows, :] = ref[...].astype(BF16)
            off += rows

    return pl.pallas_call(
        body, name=name,
        out_shape=jax.ShapeDtypeStruct((PACK_ROWS, D_MODEL), BF16),
        compiler_params=pltpu.CompilerParams(vmem_limit_bytes=V7X_VMEM_LIMIT),
    )(*parts)


def _mesh_place():
    x, y, c = lax.axis_index("x"), lax.axis_index("y"), lax.axis_index("c")
    others = [(1 - x, y), (x, 1 - y), (1 - x, 1 - y)]
    return x, y, c, others


def _half(c):
    return pl.ds(pl.multiple_of(c * HALF_ROWS, 16), HALF_ROWS)


def _all_gather_pack(pack_sh, *, name):
    def body(p_ref, o_ref, send_sems, recv_sems, local_sem):
        x, y, c, others = _mesh_place()
        me = 2 * x + y
        sibling = (x, y, 1 - c)
        mine, theirs = _half(c), _half(1 - c)

        def copy(k, src, dst, to):
            return pltpu.make_async_remote_copy(src_ref=src, dst_ref=dst, send_sem=send_sems.at[k],
                                                recv_sem=recv_sems.at[k], device_id=to, device_id_type=MESH)

        own = pltpu.make_async_copy(p_ref, o_ref.at[me], local_sem)
        own.start()
        first = [copy(j, p_ref.at[mine], o_ref.at[me, mine], (px, py, c)) for j, (px, py) in enumerate(others)]
        for cp in first:
            cp.start()
        passed = []
        for j, (px, py) in enumerate(others):
            slab = o_ref.at[2 * px + py, mine]
            copy(j, slab, slab, (px, py, c)).wait_recv()
            fwd = copy(3 + j, slab, slab, sibling)
            fwd.start()
            passed.append(fwd)
        for j, (px, py) in enumerate(others):
            slab = o_ref.at[2 * px + py, theirs]
            copy(3 + j, slab, slab, sibling).wait_recv()
        for cp in first + passed:
            cp.wait_send()
        own.wait()

    return pl.pallas_call(
        body, name=name,
        in_specs=[pl.BlockSpec(memory_space=pl.ANY)],
        out_specs=pl.BlockSpec(memory_space=pl.ANY),
        out_shape=jax.ShapeDtypeStruct((N_CHIPS, PACK_ROWS, D_MODEL), BF16),
        scratch_shapes=[pltpu.SemaphoreType.DMA((6,)), pltpu.SemaphoreType.DMA((6,)), pltpu.SemaphoreType.DMA],
    )(pack_sh)


def _rs_sibling(grads, *, name):
    def body(g_ref, o_ref, send_sem, recv_sem):
        x, y, c, _ = _mesh_place()
        cp = pltpu.make_async_remote_copy(src_ref=g_ref.at[:, _half(1 - c), :], dst_ref=o_ref, send_sem=send_sem,
                                          recv_sem=recv_sem, device_id=(x, y, 1 - c), device_id_type=MESH)
        cp.start()
        cp.wait()

    return pl.pallas_call(
        body, name=name,
        in_specs=[pl.BlockSpec(memory_space=pl.ANY)],
        out_specs=pl.BlockSpec(memory_space=pl.ANY),
        out_shape=jax.ShapeDtypeStruct((N_CHIPS, HALF_ROWS, D_MODEL), F32),
        scratch_shapes=[pltpu.SemaphoreType.DMA, pltpu.SemaphoreType.DMA],
    )(grads)


def _rs_add_halves(c_arr, grads, recv, *, name):
    nt = HALF_ROWS // RS_TILE

    def body(c_ref, a_ref, b_ref, o_ref):
        o_ref[...] = a_ref[...] + b_ref[...]

    grid_spec = pltpu.PrefetchScalarGridSpec(
        num_scalar_prefetch=1, grid=(N_CHIPS, nt),
        in_specs=[pl.BlockSpec((1, RS_TILE, D_MODEL), lambda s, i, c_ref: (s, c_ref[0] * nt + i, 0)),
                  pl.BlockSpec((1, RS_TILE, D_MODEL), lambda s, i, c_ref: (s, i, 0))],
        out_specs=pl.BlockSpec((1, RS_TILE, D_MODEL), lambda s, i, c_ref: (s, i, 0)))
    return pl.pallas_call(
        body, name=name, grid_spec=grid_spec,
        out_shape=jax.ShapeDtypeStruct((N_CHIPS, HALF_ROWS, D_MODEL), F32),
        compiler_params=_cparams(("arbitrary", "arbitrary")),
    )(c_arr, grads, recv)


def _rs_chips(part, *, name):
    def body(p_ref, o_ref, send_sems, recv_sems, local_sem):
        x, y, c, others = _mesh_place()
        me = 2 * x + y
        own = pltpu.make_async_copy(p_ref.at[me], o_ref.at[me], local_sem)
        own.start()
        sends = [pltpu.make_async_remote_copy(
            src_ref=p_ref.at[2 * px + py], dst_ref=o_ref.at[me], send_sem=send_sems.at[j], recv_sem=recv_sems.at[j],
            device_id=(px, py, c), device_id_type=MESH) for j, (px, py) in enumerate(others)]
        for cp in sends:
            cp.start()
        for j, (px, py) in enumerate(others):
            slab = o_ref.at[2 * px + py]
            pltpu.make_async_remote_copy(src_ref=slab, dst_ref=slab, send_sem=send_sems.at[j],
                                         recv_sem=recv_sems.at[j], device_id=(px, py, c),
                                         device_id_type=MESH).wait_recv()
        for cp in sends:
            cp.wait_send()
        own.wait()

    return pl.pallas_call(
        body, name=name,
        in_specs=[pl.BlockSpec(memory_space=pl.ANY)],
        out_specs=pl.BlockSpec(memory_space=pl.ANY),
        out_shape=jax.ShapeDtypeStruct((N_CHIPS, HALF_ROWS, D_MODEL), F32),
        scratch_shapes=[pltpu.SemaphoreType.DMA((3,)), pltpu.SemaphoreType.DMA((3,)), pltpu.SemaphoreType.DMA],
    )(part)


def _rs_add_chips(parts, *, name):
    def body(p_ref, o_ref):
        o_ref[...] = (p_ref[0] + p_ref[1]) + (p_ref[2] + p_ref[3])

    return pl.pallas_call(
        body, name=name, grid=(HALF_ROWS // RS_TILE,),
        in_specs=[pl.BlockSpec((N_CHIPS, RS_TILE, D_MODEL), lambda i: (0, i, 0))],
        out_specs=pl.BlockSpec((RS_TILE, D_MODEL), lambda i: (i, 0)),
        out_shape=jax.ShapeDtypeStruct((HALF_ROWS, D_MODEL), F32),
        compiler_params=_cparams(("arbitrary",)),
    )(parts)


def _rs_share(half, *, name):
    def body(h_ref, o_ref, send_sem, recv_sem, local_sem):
        x, y, c, _ = _mesh_place()
        own = pltpu.make_async_copy(h_ref, o_ref.at[_half(c)], local_sem)
        own.start()
        cp = pltpu.make_async_remote_copy(src_ref=h_ref, dst_ref=o_ref.at[_half(c)], send_sem=send_sem,
                                          recv_sem=recv_sem, device_id=(x, y, 1 - c), device_id_type=MESH)
        cp.start()
        theirs = o_ref.at[_half(1 - c)]
        pltpu.make_async_remote_copy(src_ref=theirs, dst_ref=theirs, send_sem=send_sem, recv_sem=recv_sem,
                                     device_id=(x, y, 1 - c), device_id_type=MESH).wait_recv()
        cp.wait_send()
        own.wait()

    return pl.pallas_call(
        body, name=name,
        in_specs=[pl.BlockSpec(memory_space=pl.ANY)],
        out_specs=pl.BlockSpec(memory_space=pl.ANY),
        out_shape=jax.ShapeDtypeStruct((PACK_ROWS, D_MODEL), F32),
        scratch_shapes=[pltpu.SemaphoreType.DMA, pltpu.SemaphoreType.DMA, pltpu.SemaphoreType.DMA],
    )(half)


def _small_all_reduce(packed, *, name):
    rows = packed.shape[0]

    def body(p_ref, o_ref, sib_ref, slots_ref, send_sems, recv_sems):
        x, y, c, others = _mesh_place()
        me = 2 * x + y
        sib = pltpu.make_async_remote_copy(src_ref=p_ref, dst_ref=sib_ref, send_sem=send_sems.at[0],
                                           recv_sem=recv_sems.at[0], device_id=(x, y, 1 - c), device_id_type=MESH)
        sib.start()
        sib.wait()
        slots_ref[me] = p_ref[...] + sib_ref[...]
        sends = [pltpu.make_async_remote_copy(
            src_ref=slots_ref.at[me], dst_ref=slots_ref.at[me], send_sem=send_sems.at[1 + j],
            recv_sem=recv_sems.at[1 + j], device_id=(px, py, c), device_id_type=MESH)
            for j, (px, py) in enumerate(others)]
        for cp in sends:
            cp.start()
        for j, (px, py) in enumerate(others):
            slab = slots_ref.at[2 * px + py]
            pltpu.make_async_remote_copy(src_ref=slab, dst_ref=slab, send_sem=send_sems.at[1 + j],
                                         recv_sem=recv_sems.at[1 + j], device_id=(px, py, c),
                                         device_id_type=MESH).wait_recv()
        for cp in sends:
            cp.wait_send()
        o_ref[...] = (slots_ref[0] + slots_ref[1]) + (slots_ref[2] + slots_ref[3])

    vm = pl.BlockSpec(memory_space=pltpu.VMEM)
    return pl.pallas_call(
        body, name=name, in_specs=[vm], out_specs=vm,
        out_shape=jax.ShapeDtypeStruct((rows, 128), F32),
        scratch_shapes=[pltpu.VMEM((rows, 128), F32), pltpu.VMEM((N_CHIPS, rows, 128), F32),
                        pltpu.SemaphoreType.DMA((4,)), pltpu.SemaphoreType.DMA((4,))],
    )(packed)


def _adamw(w, g, m, v, *, tile, name):
    rows, cols = w.shape

    def body(w_ref, g_ref, m_ref, v_ref, d_ref, nm_ref, nv_ref):
        g_v = g_ref[...]
        m_n = ADAM_B1 * m_ref[...] + (1.0 - ADAM_B1) * g_v
        v_n = ADAM_B2 * v_ref[...] + (1.0 - ADAM_B2) * (g_v * g_v)
        m_hat = m_n / (1.0 - ADAM_B1 ** ADAM_STEP)
        v_hat = v_n / (1.0 - ADAM_B2 ** ADAM_STEP)
        d_ref[...] = -ADAM_LR * (m_hat / (jnp.sqrt(v_hat) + ADAM_EPS) + ADAM_WD * w_ref[...])
        nm_ref[...] = m_n
        nv_ref[...] = v_n

    spec = pl.BlockSpec((tile, cols), lambda i: (i, 0))
    shape = jax.ShapeDtypeStruct((rows, cols), F32)
    return pl.pallas_call(
        body, name=name, grid=(rows // tile,),
        in_specs=[spec] * 4, out_specs=[spec] * 3, out_shape=[shape] * 3,
        compiler_params=_cparams(("arbitrary",)),
    )(w, g, m, v)


def kernel(x, ffn1_norm_g, ffn1_w_gate, ffn1_w_up, ffn1_w_down, mix_norm_g, w_in, b_in, attn_sinks, gmlp_ln_g, gmlp_ln_b, gmlp_w_s, gmlp_b_s, attn_out_norm_g, gmlp_out_norm_g, w_out, b_out, ffn2_norm_g, ffn2_w_gate, ffn2_w_up, ffn2_w_down, final_norm_g, loss_target, m_ffn1_norm_g, m_ffn1_w_gate, m_ffn1_w_up, m_ffn1_w_down, m_mix_norm_g, m_w_in, m_b_in, m_attn_sinks, m_gmlp_ln_g, m_gmlp_ln_b, m_gmlp_w_s, m_gmlp_b_s, m_attn_out_norm_g, m_gmlp_out_norm_g, m_w_out, m_b_out, m_ffn2_norm_g, m_ffn2_w_gate, m_ffn2_w_up, m_ffn2_w_down, m_final_norm_g, v_ffn1_norm_g, v_ffn1_w_gate, v_ffn1_w_up, v_ffn1_w_down, v_mix_norm_g, v_w_in, v_b_in, v_attn_sinks, v_gmlp_ln_g, v_gmlp_ln_b, v_gmlp_w_s, v_gmlp_b_s, v_attn_out_norm_g, v_gmlp_out_norm_g, v_w_out, v_b_out, v_ffn2_norm_g, v_ffn2_w_gate, v_ffn2_w_up, v_ffn2_w_down, v_final_norm_g):
    f_args = dict(locals())
    weights = {n: f_args[n] for n in [nm for nm, _ in SMALL if nm != "loss"] + list(BIG)}
    shapes = {n: weights[n].shape for n in weights}
    shapes["loss"] = ()
    c_idx = lax.axis_index("c")

    def with_cols(name, a):
        a2 = a.reshape(a.shape[-2], a.shape[-1])
        return a2.T if BIG_TRANSPOSED[BIG.index(name)] else a2

    pack = _all_gather_pack(_pack_cast([with_cols(n, weights[n]) for n in BIG], name="pack_cast"), name="ag_weights")
    mix_rows = pack[:, MIX_BLOCK * FF_SH:, :]
    p = {n: weights[n].reshape(1, -1) for n in ("ffn1_norm_g", "mix_norm_g", "b_in", "gmlp_ln_g", "gmlp_ln_b",
                                                "attn_out_norm_g", "gmlp_out_norm_g", "b_out", "ffn2_norm_g",
                                                "final_norm_g")}
    p["w_in_t"] = mix_rows[:, :IN_SH, :].reshape(IN_W, D_MODEL)
    p["w_out"] = mix_rows[:, IN_SH:, :].reshape(D_MODEL, D_MODEL)
    p["attn_sinks"] = attn_sinks.reshape(N_Q_HEADS)
    p["gmlp_w_s"] = gmlp_w_s.reshape(GMLP_GROUPS, BLK, BLK)
    p["bs_full"] = jnp.broadcast_to(gmlp_b_s.reshape(GMLP_GROUPS, BLK).T[:, :, None],
                                    (BLK, GMLP_GROUPS, GROUP_DIM)).reshape(BLK, GMLP_W)

    loss_part, dx0, grads, gs = _local_step(x[0], loss_target[0], p, pack)

    c_arr = c_idx.reshape(1).astype(jnp.int32)
    part = _rs_add_halves(c_arr, grads, _rs_sibling(grads, name="rs_sibling"), name="rs_add_halves")
    shard = _rs_share(_rs_add_chips(_rs_chips(part, name="rs_chips"), name="rs_add_chips"), name="rs_share")
    gs["gmlp_b_s"] = jnp.sum(gs["gmlp_b_s"].reshape(BLK, GMLP_GROUPS, GROUP_DIM), axis=-1).T
    gs["attn_sinks"] = gs["attn_sinks"][:, 0]
    gs["loss"] = loss_part[0, 0]
    small = _unpack_small(_small_all_reduce(_pack_small(gs), name="small_all_reduce"), shapes)
    grad_w = {n: small[n] for n, _ in SMALL if n != "loss"}
    off = 0
    for n, rows, tr in zip(BIG, BIG_ROWS, BIG_TRANSPOSED):
        blk = shard[off:off + rows]
        grad_w[n] = (blk.T if tr else blk).reshape(shapes[n])
        off += rows

    delta, new_m, new_v = {}, {}, {}
    for n in BIG:
        w2 = weights[n].reshape(shapes[n][-2:])
        tile = w2.shape[0] // 4
        d, nm, nv = _adamw(w2, grad_w[n].reshape(w2.shape), f_args["m_" + n].reshape(w2.shape),
                           f_args["v_" + n].reshape(w2.shape), tile=tile, name="adamw_" + n)
        delta[n], new_m[n], new_v[n] = d.reshape(shapes[n]), nm.reshape(shapes[n]), nv.reshape(shapes[n])
    sm = {k: {n: f_args[k + n] for n, _ in SMALL if n != "loss"} for k in ("", "m_", "v_")}
    for k in sm:
        sm[k]["loss"] = jnp.zeros((), F32)
    gpk = dict(grad_w)
    gpk["loss"] = jnp.zeros((), F32)
    d, nm, nv = _adamw(_pack_small(sm[""]), _pack_small(gpk), _pack_small(sm["m_"]), _pack_small(sm["v_"]),
                       tile=SMALL_ROWS, name="adamw_small")
    for res, packed in ((delta, d), (new_m, nm), (new_v, nv)):
        res.update({n: a for n, a in _unpack_small(packed, shapes).items() if n != "loss"})

    order = ('ffn1_norm_g', 'ffn1_w_gate', 'ffn1_w_up', 'ffn1_w_down', 'mix_norm_g', 'w_in', 'b_in', 'attn_sinks',
             'gmlp_ln_g', 'gmlp_ln_b', 'gmlp_w_s', 'gmlp_b_s', 'attn_out_norm_g', 'gmlp_out_norm_g', 'w_out', 'b_out',
             'ffn2_norm_g', 'ffn2_w_gate', 'ffn2_w_up', 'ffn2_w_down', 'final_norm_g')
    return (small["loss"], dx0.reshape(x.shape), *[grad_w[n] for n in order], *[delta[n] for n in order],
            *[new_m[n] for n in order], *[new_v[n] for n in order])
```

```python
import functools

import jax
import jax.numpy as jnp
from jax import lax
from jax.experimental import pallas as pl
from jax.experimental.pallas import tpu as pltpu

F32 = jnp.float32
BF16 = jnp.bfloat16

D_MODEL = 1024
D_FF = 2816
N_CHIPS = 4
FF_SH = D_FF // N_CHIPS
N_Q_HEADS = 8
N_KV_HEADS = 2
REP = N_Q_HEADS // N_KV_HEADS
HEAD_DIM = 64
ATTN_W = 512
KV_W = 128
GMLP_W = 512
GMLP_GROUPS = 8
GROUP_DIM = 64
BLK = 128
IN_W = 1792
IN_SH = IN_W // N_CHIPS
OUT_SH = D_MODEL // N_CHIPS
EPS = 1e-6
FFN_RES = 0.5
ATTN_SCALE = HEAD_DIM ** -0.5

ADAM_LR = 0.001
ADAM_B1 = 0.9
ADAM_B2 = 0.999
ADAM_EPS = 1e-08
ADAM_WD = 0.01
ADAM_STEP = 10

V7X_VMEM_LIMIT = 56 * 1024 * 1024
MESH = pl.DeviceIdType.MESH


def _cparams(sem):
    return pltpu.CompilerParams(dimension_semantics=sem, vmem_limit_bytes=V7X_VMEM_LIMIT)


def _dot(a, b):
    return jnp.dot(a, b, preferred_element_type=F32)


def _dot_nt(a, b):
    return lax.dot_general(a, b, (((1,), (1,)), ((), ())), preferred_element_type=F32)


def _dot_tn(a, b):
    return lax.dot_general(a, b, (((0,), (0,)), ((), ())), preferred_element_type=F32)


def _rms(x, g):
    r = lax.rsqrt(jnp.mean(x * x, axis=-1, keepdims=True) + EPS)
    return x * r * g, r


def _rms_bwd(dh, x, r, g):
    gy = dh * g
    dx = r * gy - x * (r * r * r) * jnp.mean(gy * x, axis=-1, keepdims=True)
    dg = jnp.sum(dh * x * r, axis=0, keepdims=True)
    return dx, dg


def _const(shape):
    nd = len(shape)
    return pl.BlockSpec(shape, lambda *_: (0,) * nd)


def _rows(t, w):
    return pl.BlockSpec((t, w), lambda i: (i, 0))


PACK_ROWS = 7 * FF_SH
HALF_ROWS = PACK_ROWS // 2
RS_TILE = HALF_ROWS // 7
MIX_BLOCK = 6
BIG = ("ffn1_w_gate", "ffn1_w_up", "ffn1_w_down", "ffn2_w_gate", "ffn2_w_up", "ffn2_w_down", "w_in", "w_out")
BIG_ROWS = (FF_SH, FF_SH, FF_SH, FF_SH, FF_SH, FF_SH, IN_SH, OUT_SH)
BIG_TRANSPOSED = (True, True, False, True, True, False, True, False)

SMALL = (("ffn1_norm_g", 1024), ("mix_norm_g", 1024), ("b_in", 1792), ("attn_sinks", 8), ("gmlp_ln_g", 512),
         ("gmlp_ln_b", 512), ("gmlp_w_s", 131072), ("gmlp_b_s", 1024), ("attn_out_norm_g", 512),
         ("gmlp_out_norm_g", 512), ("b_out", 1024), ("ffn2_norm_g", 1024), ("final_norm_g", 1024), ("loss", 1))


def _small_rows(n):
    return -(-n // 1024) * 8


SMALL_ROWS = sum(_small_rows(n) for _, n in SMALL)


def _pack_small(parts):
    out = []
    for name, n in SMALL:
        flat = parts[name].reshape(-1).astype(F32)
        rows = _small_rows(n)
        out.append(jnp.pad(flat, (0, rows * 128 - n)).reshape(rows, 128))
    return jnp.concatenate(out, axis=0)


def _unpack_small(packed, shapes):
    res, off = {}, 0
    for name, n in SMALL:
        rows = _small_rows(n)
        res[name] = packed[off:off + rows].reshape(-1)[:n].reshape(shapes[name])
        off += rows
    return res


def _ffn_tile(x, g, wg_ref, wu_ref, wd_ref):
    h, _ = _rms(x, g)
    hb = h.astype(BF16)
    acc = jnp.zeros(x.shape, F32)
    for j in range(N_CHIPS):
        a = _dot_nt(hb, wg_ref[j])
        b = _dot_nt(hb, wu_ref[j])
        f = (a * jax.nn.sigmoid(a) * b).astype(BF16)
        acc = acc + _dot(f, wd_ref[j])
    return x + FFN_RES * acc


def _ffn_weight_specs(k0):
    one = pl.Buffered(1)
    return [pl.BlockSpec((N_CHIPS, FF_SH, D_MODEL), functools.partial(lambda kk, i: (0, kk, 0), k0 + d),
                         pipeline_mode=one) for d in range(3)]


def _ffn_fwd(x, g, pack, k0, *, tile, name):
    s = x.shape[0]

    def body(x_ref, g_ref, wg_ref, wu_ref, wd_ref, o_ref):
        o_ref[...] = _ffn_tile(x_ref[...], g_ref[...], wg_ref, wu_ref, wd_ref)

    return pl.pallas_call(
        body, name=name, grid=(s // tile,),
        in_specs=[_rows(tile, D_MODEL), _const((1, D_MODEL))] + _ffn_weight_specs(k0),
        out_specs=_rows(tile, D_MODEL),
        out_shape=jax.ShapeDtypeStruct(x.shape, F32),
        compiler_params=_cparams(("arbitrary",)),
    )(x, g, pack, pack, pack)


def _ffn_fwd_loss(x, g, pack, k0, gf, tgt, *, tile, name):
    s = x.shape[0]

    def body(x_ref, g_ref, wg_ref, wu_ref, wd_ref, gf_ref, t_ref, dx_ref, loss_ref, dgf_ref):
        @pl.when(pl.program_id(0) == 0)
        def _():
            loss_ref[...] = jnp.zeros_like(loss_ref)
            dgf_ref[...] = jnp.zeros_like(dgf_ref)

        x3 = _ffn_tile(x_ref[...], g_ref[...], wg_ref, wu_ref, wd_ref)
        gf_v = gf_ref[...]
        out, r = _rms(x3, gf_v)
        diff = out - t_ref[...]
        part = jnp.sum(jnp.sum(diff * diff, axis=-1, keepdims=True), axis=0, keepdims=True)
        loss_ref[...] += jnp.broadcast_to(part * (0.5 / D_MODEL), loss_ref.shape)
        dx, dg = _rms_bwd(diff * (1.0 / D_MODEL), x3, r, gf_v)
        dx_ref[...] = dx
        dgf_ref[...] += dg

    return pl.pallas_call(
        body, name=name, grid=(s // tile,),
        in_specs=[_rows(tile, D_MODEL), _const((1, D_MODEL))] + _ffn_weight_specs(k0)
                 + [_const((1, D_MODEL)), _rows(tile, D_MODEL)],
        out_specs=[_rows(tile, D_MODEL), _const((1, 128)), _const((1, D_MODEL))],
        out_shape=[jax.ShapeDtypeStruct(x.shape, F32),
                   jax.ShapeDtypeStruct((1, 128), F32),
                   jax.ShapeDtypeStruct((1, D_MODEL), F32)],
        compiler_params=_cparams(("arbitrary",)),
    )(x, g, pack, pack, pack, gf, tgt)


def _ffn_bwd(x, dy, g, pack, k0, grads, *, tile, name):
    s = x.shape[0]

    def body(x_ref, dy_ref, g_ref, wg_ref, wu_ref, wd_ref, *rest):
        dhp_ref, gw_ref = rest[-2], rest[-1]

        @pl.when(pl.program_id(1) == 0)
        def _():
            gw_ref[...] = jnp.zeros_like(gw_ref)

        h, _ = _rms(x_ref[...], g_ref[...])
        hb = h.astype(BF16)
        dob = (FFN_RES * dy_ref[...]).astype(BF16)
        wg_j, wu_j, wd_j = wg_ref[0], wu_ref[0], wd_ref[0]
        a = _dot_nt(hb, wg_j)
        b = _dot_nt(hb, wu_j)
        sg = jax.nn.sigmoid(a)
        sa = a * sg
        fb = (sa * b).astype(BF16)
        df = _dot_nt(dob, wd_j)
        dbb = (df * sa).astype(BF16)
        dab = (df * b * (sg + sa * (1.0 - sg))).astype(BF16)
        dhp_ref[0] = _dot(dab, wg_j) + _dot(dbb, wu_j)
        gw_ref[0, 0:FF_SH, :] += _dot_tn(dab, hb)
        gw_ref[0, FF_SH:2 * FF_SH, :] += _dot_tn(dbb, hb)
        gw_ref[0, 2 * FF_SH:3 * FF_SH, :] += _dot_tn(fb, dob)

    wspecs = [pl.BlockSpec((1, FF_SH, D_MODEL), functools.partial(lambda kk, j, i: (j, kk, 0), k0 + d))
              for d in range(3)]
    xspec = pl.BlockSpec((tile, D_MODEL), lambda j, i: (i, 0))
    in_specs = [xspec, xspec, pl.BlockSpec((1, D_MODEL), lambda j, i: (0, 0))] + wspecs
    operands = [x, dy, g, pack, pack, pack]
    aliases = {}
    if grads is not None:
        in_specs.append(pl.BlockSpec(memory_space=pl.ANY))
        operands.append(grads)
        aliases = {6: 1}
    k3 = k0 // 3
    return pl.pallas_call(
        body, name=name, grid=(N_CHIPS, s // tile),
        in_specs=in_specs,
        out_specs=[pl.BlockSpec((1, tile, D_MODEL), lambda j, i: (j, i, 0)),
                   pl.BlockSpec((1, 3 * FF_SH, D_MODEL), lambda j, i: (j, k3, 0))],
        out_shape=[jax.ShapeDtypeStruct((N_CHIPS, s, D_MODEL), F32),
                   jax.ShapeDtypeStruct((N_CHIPS, PACK_ROWS, D_MODEL), F32)],
        input_output_aliases=aliases,
        compiler_params=_cparams(("arbitrary", "arbitrary")),
    )(*operands)


def _mix_grads_pack(dw_in_t, dw_out, grads, *, name):
    def body(a_ref, b_ref, g_any, o_ref):
        o_ref[0, 0:IN_SH, :] = a_ref[0]
        o_ref[0, IN_SH:FF_SH, :] = b_ref[0]

    return pl.pallas_call(
        body, name=name, grid=(N_CHIPS,),
        in_specs=[pl.BlockSpec((1, IN_SH, D_MODEL), lambda j: (j, 0, 0)),
                  pl.BlockSpec((1, OUT_SH, D_MODEL), lambda j: (j, 0, 0)),
                  pl.BlockSpec(memory_space=pl.ANY)],
        out_specs=pl.BlockSpec((1, FF_SH, D_MODEL), lambda j: (j, MIX_BLOCK, 0)),
        out_shape=jax.ShapeDtypeStruct((N_CHIPS, PACK_ROWS, D_MODEL), F32),
        input_output_aliases={2: 0},
        compiler_params=_cparams(("arbitrary",)),
    )(dw_in_t.reshape(N_CHIPS, IN_SH, D_MODEL), dw_out.reshape(N_CHIPS, OUT_SH, D_MODEL), grads)


def _norm_bwd(dhp, x, dy, g, *, tile, name):
    s = x.shape[0]

    def body(dhp_ref, x_ref, dy_ref, g_ref, dx_ref, dg_ref):
        @pl.when(pl.program_id(0) == 0)
        def _():
            dg_ref[...] = jnp.zeros_like(dg_ref)

        dh = (dhp_ref[0] + dhp_ref[1]) + (dhp_ref[2] + dhp_ref[3])
        x_v = x_ref[...]
        r = lax.rsqrt(jnp.mean(x_v * x_v, axis=-1, keepdims=True) + EPS)
        dx, dg = _rms_bwd(dh, x_v, r, g_ref[...])
        dx_ref[...] = dy_ref[...] + dx
        dg_ref[...] += dg

    return pl.pallas_call(
        body, name=name, grid=(s // tile,),
        in_specs=[pl.BlockSpec((N_CHIPS, tile, D_MODEL), lambda i: (0, i, 0)),
                  _rows(tile, D_MODEL), _rows(tile, D_MODEL), _const((1, D_MODEL))],
        out_specs=[_rows(tile, D_MODEL), _const((1, D_MODEL))],
        out_shape=[jax.ShapeDtypeStruct(x.shape, F32), jax.ShapeDtypeStruct((1, D_MODEL), F32)],
        compiler_params=_cparams(("arbitrary",)),
    )(dhp, x, dy, g)


def _mix_in_fwd(x, g, w_in_t, b_in, *, tile, name):
    s = x.shape[0]

    def body(x_ref, g_ref, w_ref, b_ref, q_ref, k_ref, v_ref, z_ref):
        h, _ = _rms(x_ref[...], g_ref[...])
        proj = _dot_nt(h.astype(BF16), w_ref[...]) + b_ref[...]
        q_ref[...] = proj[:, :ATTN_W].astype(BF16)
        k_ref[...] = proj[:, ATTN_W:ATTN_W + KV_W].astype(BF16)
        v_ref[...] = proj[:, ATTN_W + KV_W:ATTN_W + 2 * KV_W].astype(BF16)
        z_ref[...] = proj[:, ATTN_W + 2 * KV_W:]

    return pl.pallas_call(
        body, name=name, grid=(s // tile,),
        in_specs=[_rows(tile, D_MODEL), _const((1, D_MODEL)), _const((IN_W, D_MODEL)), _const((1, IN_W))],
        out_specs=[_rows(tile, ATTN_W), _rows(tile, KV_W), _rows(tile, KV_W), _rows(tile, 2 * GMLP_W)],
        out_shape=[jax.ShapeDtypeStruct((s, ATTN_W), BF16), jax.ShapeDtypeStruct((s, KV_W), BF16),
                   jax.ShapeDtypeStruct((s, KV_W), BF16), jax.ShapeDtypeStruct((s, 2 * GMLP_W), F32)],
        compiler_params=_cparams(("arbitrary",)),
    )(x, g, w_in_t, b_in)


def _mix_in_bwd(x, dy, dq, dk, dv, dz, g, w_in_t, *, tile, name):
    s = x.shape[0]

    def body(x_ref, dy_ref, dq_ref, dk_ref, dv_ref, dz_ref, g_ref, w_ref, dx_ref, dw_ref, db_ref, dg_ref):
        @pl.when(pl.program_id(0) == 0)
        def _():
            dw_ref[...] = jnp.zeros_like(dw_ref)
            db_ref[...] = jnp.zeros_like(db_ref)
            dg_ref[...] = jnp.zeros_like(dg_ref)

        dproj = jnp.concatenate([dq_ref[...], dk_ref[...], dv_ref[...], dz_ref[...]], axis=-1)
        db_ref[...] += jnp.sum(dproj, axis=0, keepdims=True)
        dpb = dproj.astype(BF16)
        x_v = x_ref[...]
        g_v = g_ref[...]
        h, r = _rms(x_v, g_v)
        dw_ref[...] += _dot_tn(dpb, h.astype(BF16))
        dh = _dot(dpb, w_ref[...])
        dx, dg = _rms_bwd(dh, x_v, r, g_v)
        dx_ref[...] = dy_ref[...] + dx
        dg_ref[...] += dg

    return pl.pallas_call(
        body, name=name, grid=(s // tile,),
        in_specs=[_rows(tile, D_MODEL), _rows(tile, D_MODEL), _rows(tile, ATTN_W), _rows(tile, KV_W),
                  _rows(tile, KV_W), _rows(tile, 2 * GMLP_W), _const((1, D_MODEL)), _const((IN_W, D_MODEL))],
        out_specs=[_rows(tile, D_MODEL), _const((IN_W, D_MODEL)), _const((1, IN_W)), _const((1, D_MODEL))],
        out_shape=[jax.ShapeDtypeStruct(x.shape, F32), jax.ShapeDtypeStruct((IN_W, D_MODEL), F32),
                   jax.ShapeDtypeStruct((1, IN_W), F32), jax.ShapeDtypeStruct((1, D_MODEL), F32)],
        compiler_params=_cparams(("arbitrary",)),
    )(x, dy, dq, dk, dv, dz, g, w_in_t)


_GELU_C = 0.7978845608028654
_GELU_A = 0.044715


def _gelu(x):
    return 0.5 * x * (1.0 + jnp.tanh(_GELU_C * (x + _GELU_A * (x * x * x))))


def _gelu_grad(x):
    t = jnp.tanh(_GELU_C * (x + _GELU_A * (x * x * x)))
    return 0.5 * (1.0 + t) + 0.5 * x * (1.0 - t * t) * (_GELU_C * (1.0 + 3.0 * _GELU_A * (x * x)))


def _band(ref, i):
    prev = jnp.maximum(i - 1, 0)
    return jnp.concatenate([ref[pl.ds(pl.multiple_of(prev * BLK, BLK), BLK), :],
                            ref[pl.ds(pl.multiple_of(i * BLK, BLK), BLK), :]], axis=0)


def _band_mask(i):
    qpos = lax.broadcasted_iota(jnp.int32, (BLK, 2 * BLK), 0)
    kidx = lax.broadcasted_iota(jnp.int32, (BLK, 2 * BLK), 1)
    rel = qpos - kidx + BLK
    win = jnp.where(rel >= 0, jnp.where(rel < BLK, 1, 0), 0)
    real = jnp.where(kidx >= BLK, 1, jnp.where(i > 0, 1, 0))
    return (win * real) > 0


def _attn_probs(qh, kg, mask, sink):
    sc = _dot_nt(qh, kg) * ATTN_SCALE
    sc = jnp.where(mask, sc, -jnp.inf)
    m = jnp.maximum(jnp.max(sc, axis=-1, keepdims=True), sink)
    p = jnp.exp(sc - m)
    es = jnp.exp(sink - m)
    inv = 1.0 / (jnp.sum(p, axis=-1, keepdims=True) + es)
    return p * inv, es * inv


def _tril_mask():
    t = lax.broadcasted_iota(jnp.int32, (BLK, BLK), 0)
    s_ = lax.broadcasted_iota(jnp.int32, (BLK, BLK), 1)
    return s_ <= t


def _gmlp_fwd_parts(zg, lng, lnb, ws_ref, bs_full):
    z = _gelu(zg)
    u = z[:, :GMLP_W]
    zv = z[:, GMLP_W:]
    mu = jnp.mean(zv, axis=-1, keepdims=True)
    zc = zv - mu
    rstd = lax.rsqrt(jnp.mean(zc * zc, axis=-1, keepdims=True) + EPS)
    xh = zc * rstd
    vvb = (xh * lng + lnb).astype(BF16)
    tril = _tril_mask()
    wms, parts = [], []
    for gi in range(GMLP_GROUPS):
        wm = jnp.where(tril, ws_ref[gi], 0.0).astype(BF16)
        wms.append(wm)
        parts.append(_dot(wm, vvb[:, gi * GROUP_DIM:(gi + 1) * GROUP_DIM]))
    mixed = jnp.concatenate(parts, axis=-1) + bs_full
    return u, xh, rstd, vvb, wms, mixed


def _attn_fwd(q, kb, vb, mask, sink_ref):
    outs = []
    for h in range(N_Q_HEADS):
        gi = h // REP
        pn, _ = _attn_probs(q[:, h * HEAD_DIM:(h + 1) * HEAD_DIM], kb[:, gi * HEAD_DIM:(gi + 1) * HEAD_DIM],
                            mask, sink_ref[h])
        outs.append(_dot(pn.astype(BF16), vb[:, gi * HEAD_DIM:(gi + 1) * HEAD_DIM]))
    return jnp.concatenate(outs, axis=-1)


def _mix_core_fwd(x1, q, k, v, zg, sinks, lng, lnb, w_s, bs_full, gao, ggo, w_out, b_out, *, name):
    s = x1.shape[0]

    def body(sink_ref, x_ref, q_ref, k_ref, v_ref, z_ref, lng_ref, lnb_ref, ws_ref, bs_ref, gao_ref, ggo_ref,
             wo_ref, bo_ref, o_ref):
        i = pl.program_id(0)
        mask = _band_mask(i)
        y_attn = _attn_fwd(q_ref[...], _band(k_ref, i), _band(v_ref, i), mask, sink_ref)
        u, _, _, _, _, mixed = _gmlp_fwd_parts(z_ref[...], lng_ref[...], lnb_ref[...], ws_ref, bs_ref[...])
        ya, _ = _rms(y_attn, gao_ref[...])
        yg, _ = _rms(u * mixed, ggo_ref[...])
        yb = jnp.concatenate([ya, yg], axis=-1).astype(BF16)
        o_ref[...] = x_ref[...] + (_dot(yb, wo_ref[...]) + bo_ref[...])

    return pl.pallas_call(
        body, name=name, grid=(s // BLK,),
        in_specs=[pl.BlockSpec(memory_space=pltpu.SMEM),
                  _rows(BLK, D_MODEL), _rows(BLK, ATTN_W), _const((s, KV_W)), _const((s, KV_W)),
                  _rows(BLK, 2 * GMLP_W), _const((1, GMLP_W)), _const((1, GMLP_W)),
                  _const((GMLP_GROUPS, BLK, BLK)), _const((BLK, GMLP_W)), _const((1, ATTN_W)), _const((1, GMLP_W)),
                  _const((D_MODEL, D_MODEL)), _const((1, D_MODEL))],
        out_specs=_rows(BLK, D_MODEL),
        out_shape=jax.ShapeDtypeStruct(x1.shape, F32),
        compiler_params=_cparams(("arbitrary",)),
    )(sinks, x1, q, k, v, zg, lng, lnb, w_s, bs_full, gao, ggo, w_out, b_out)


def _mix_core_bwd(dy, q, k, v, zg, sinks, lng, lnb, w_s, bs_full, gao, ggo, w_out, *, name):
    s = dy.shape[0]
    nblk = s // BLK

    def body(sink_ref, dy_ref, q_ref, k_ref, v_ref, z_ref, lng_ref, lnb_ref, ws_ref, bs_ref, gao_ref, ggo_ref,
             wo_ref, dq_ref, dk_ref, dv_ref, dz_ref, dwo_ref, dbo_ref, dgao_ref, dggo_ref, dlng_ref, dlnb_ref,
             dws_ref, dms_ref, dsk_ref):
        i = pl.program_id(0)

        @pl.when(i == 0)
        def _():
            for ref in (dk_ref, dv_ref, dwo_ref, dbo_ref, dgao_ref, dggo_ref, dlng_ref, dlnb_ref, dws_ref,
                        dms_ref, dsk_ref):
                ref[...] = jnp.zeros_like(ref)

        mask = _band_mask(i)
        q_v = q_ref[...]
        kb = _band(k_ref, i)
        vb = _band(v_ref, i)
        lng_v = lng_ref[...]
        gao_v = gao_ref[...]
        ggo_v = ggo_ref[...]
        zg_v = z_ref[...]

        y_attn = _attn_fwd(q_v, kb, vb, mask, sink_ref)
        u, xh, rstd, vvb, wms, mixed = _gmlp_fwd_parts(zg_v, lng_v, lnb_ref[...], ws_ref, bs_ref[...])
        y_gmlp = u * mixed
        ya, ra = _rms(y_attn, gao_v)
        yg, rg = _rms(y_gmlp, ggo_v)
        yb = jnp.concatenate([ya, yg], axis=-1).astype(BF16)

        dy_v = dy_ref[...]
        dyb = dy_v.astype(BF16)
        dwo_ref[...] += _dot_tn(yb, dyb)
        dbo_ref[...] += jnp.sum(dy_v, axis=0, keepdims=True)
        dyy = _dot_nt(dyb, wo_ref[...])
        d_attn, dgao = _rms_bwd(dyy[:, :ATTN_W], y_attn, ra, gao_v)
        d_gmlp, dggo = _rms_bwd(dyy[:, ATTN_W:], y_gmlp, rg, ggo_v)
        dgao_ref[...] += dgao
        dggo_ref[...] += dggo

        du = d_gmlp * mixed
        dmixed = d_gmlp * u
        dms_ref[...] += dmixed
        dmb = dmixed.astype(BF16)
        dvv_parts = []
        for gi in range(GMLP_GROUPS):
            sl = slice(gi * GROUP_DIM, (gi + 1) * GROUP_DIM)
            dws_ref[gi] += _dot_nt(dmb[:, sl], vvb[:, sl])
            dvv_parts.append(_dot_tn(wms[gi], dmb[:, sl]))
        dvv = jnp.concatenate(dvv_parts, axis=-1)
        dlng_ref[...] += jnp.sum(dvv * xh, axis=0, keepdims=True)
        dlnb_ref[...] += jnp.sum(dvv, axis=0, keepdims=True)
        dxh = dvv * lng_v
        dzv = rstd * (dxh - jnp.mean(dxh, axis=-1, keepdims=True)
                      - xh * jnp.mean(dxh * xh, axis=-1, keepdims=True))
        dz_ref[...] = jnp.concatenate([du, dzv], axis=-1) * _gelu_grad(zg_v)

        dab = d_attn.astype(BF16)
        dq_parts = []
        dk_parts = []
        dv_parts = []
        for gi in range(N_KV_HEADS):
            kg = kb[:, gi * HEAD_DIM:(gi + 1) * HEAD_DIM]
            vg = vb[:, gi * HEAD_DIM:(gi + 1) * HEAD_DIM]
            dkg = jnp.zeros((2 * BLK, HEAD_DIM), F32)
            dvg = jnp.zeros((2 * BLK, HEAD_DIM), F32)
            for rr in range(REP):
                h = gi * REP + rr
                hs = slice(h * HEAD_DIM, (h + 1) * HEAD_DIM)
                qh = q_v[:, hs]
                doh = dab[:, hs]
                pn, psink = _attn_probs(qh, kg, mask, sink_ref[h])
                dp = _dot_nt(doh, vg)
                delta = jnp.sum(pn * dp, axis=-1, keepdims=True)
                dsb = (pn * (dp - delta) * ATTN_SCALE).astype(BF16)
                dsink = jnp.sum(-psink * delta, axis=0, keepdims=True)
                dsk_ref[pl.ds(h, 1), :] += jnp.broadcast_to(dsink, (1, 128))
                dq_parts.append(_dot(dsb, kg))
                dkg = dkg + _dot_tn(dsb, qh)
                dvg = dvg + _dot_tn(pn.astype(BF16), doh)
            dk_parts.append(dkg)
            dv_parts.append(dvg)
        dq_ref[...] = jnp.concatenate(dq_parts, axis=-1)
        dkb = jnp.concatenate(dk_parts, axis=-1)
        dvb = jnp.concatenate(dv_parts, axis=-1)
        prev = pl.ds(pl.multiple_of(jnp.maximum(i - 1, 0) * BLK, BLK), BLK)
        cur = pl.ds(pl.multiple_of(i * BLK, BLK), BLK)
        dk_ref[prev, :] += dkb[:BLK]
        dv_ref[prev, :] += dvb[:BLK]
        dk_ref[cur, :] += dkb[BLK:]
        dv_ref[cur, :] += dvb[BLK:]

        @pl.when(i == nblk - 1)
        def _():
            tril = _tril_mask()
            for gi in range(GMLP_GROUPS):
                dws_ref[gi] = jnp.where(tril, dws_ref[gi], 0.0)

    return pl.pallas_call(
        body, name=name, grid=(nblk,),
        in_specs=[pl.BlockSpec(memory_space=pltpu.SMEM),
                  _rows(BLK, D_MODEL), _rows(BLK, ATTN_W), _const((s, KV_W)), _const((s, KV_W)),
                  _rows(BLK, 2 * GMLP_W), _const((1, GMLP_W)), _const((1, GMLP_W)),
                  _const((GMLP_GROUPS, BLK, BLK)), _const((BLK, GMLP_W)), _const((1, ATTN_W)), _const((1, GMLP_W)),
                  _const((D_MODEL, D_MODEL))],
        out_specs=[_rows(BLK, ATTN_W), _const((s, KV_W)), _const((s, KV_W)), _rows(BLK, 2 * GMLP_W),
                   _const((D_MODEL, D_MODEL)), _const((1, D_MODEL)), _const((1, ATTN_W)), _const((1, GMLP_W)),
                   _const((1, GMLP_W)), _const((1, GMLP_W)), _const((GMLP_GROUPS, BLK, BLK)),
                   _const((BLK, GMLP_W)), _const((N_Q_HEADS, 128))],
        out_shape=[jax.ShapeDtypeStruct((s, ATTN_W), F32), jax.ShapeDtypeStruct((s, KV_W), F32),
                   jax.ShapeDtypeStruct((s, KV_W), F32), jax.ShapeDtypeStruct((s, 2 * GMLP_W), F32),
                   jax.ShapeDtypeStruct((D_MODEL, D_MODEL), F32), jax.ShapeDtypeStruct((1, D_MODEL), F32),
                   jax.ShapeDtypeStruct((1, ATTN_W), F32), jax.ShapeDtypeStruct((1, GMLP_W), F32),
                   jax.ShapeDtypeStruct((1, GMLP_W), F32), jax.ShapeDtypeStruct((1, GMLP_W), F32),
                   jax.ShapeDtypeStruct((GMLP_GROUPS, BLK, BLK), F32), jax.ShapeDtypeStruct((BLK, GMLP_W), F32),
                   jax.ShapeDtypeStruct((N_Q_HEADS, 128), F32)],
        compiler_params=_cparams(("arbitrary",)),
    )(sinks, dy, q, k, v, zg, lng, lnb, w_s, bs_full, gao, ggo, w_out)


def _local_step(x, tgt, p, pack, *, tile=512, bwd_tile=256):
    g = {}
    tile, bwd_tile = min(tile, x.shape[0]), min(bwd_tile, x.shape[0])
    x1 = _ffn_fwd(x, p["ffn1_norm_g"], pack, 0, tile=tile, name="ffn1_fwd")
    q, k, v, zg = _mix_in_fwd(x1, p["mix_norm_g"], p["w_in_t"], p["b_in"], tile=tile, name="mix_in_fwd")
    mix_args = (q, k, v, zg, p["attn_sinks"], p["gmlp_ln_g"], p["gmlp_ln_b"], p["gmlp_w_s"], p["bs_full"],
                p["attn_out_norm_g"], p["gmlp_out_norm_g"], p["w_out"])
    x2 = _mix_core_fwd(x1, *mix_args, p["b_out"], name="mix_core_fwd")
    dx3, loss, g["final_norm_g"] = _ffn_fwd_loss(x2, p["ffn2_norm_g"], pack, 3, p["final_norm_g"], tgt,
                                                 tile=tile, name="ffn2_fwd_loss")

    dhp, grads = _ffn_bwd(x2, dx3, p["ffn2_norm_g"], pack, 3, None, tile=bwd_tile, name="ffn2_bwd")
    dx2, g["ffn2_norm_g"] = _norm_bwd(dhp, x2, dx3, p["ffn2_norm_g"], tile=bwd_tile, name="ffn2_norm_bwd")

    (dq, dk, dv, dz, dw_out, g["b_out"], g["attn_out_norm_g"], g["gmlp_out_norm_g"], g["gmlp_ln_g"],
     g["gmlp_ln_b"], g["gmlp_w_s"], dmix_sum, dsinks) = _mix_core_bwd(dx2, *mix_args, name="mix_core_bwd")
    g["gmlp_b_s"] = dmix_sum
    g["attn_sinks"] = dsinks
    dx1, dw_in_t, g["b_in"], g["mix_norm_g"] = _mix_in_bwd(
        x1, dx2, dq, dk, dv, dz, p["mix_norm_g"], p["w_in_t"], tile=tile, name="mix_in_bwd")
    grads = _mix_grads_pack(dw_in_t, dw_out, grads, name="mix_grads_pack")

    dhp1, grads = _ffn_bwd(x, dx1, p["ffn1_norm_g"], pack, 0, grads, tile=bwd_tile, name="ffn1_bwd")
    dx0, g["ffn1_norm_g"] = _norm_bwd(dhp1, x, dx1, p["ffn1_norm_g"], tile=bwd_tile, name="ffn1_norm_bwd")
    return loss, dx0, grads, g


def _pack_cast(place, parts, *, name):
    def body(place_ref, *refs):
        o_ref = refs[-1]
        off = 0
        for ref, rows in zip(refs[:-1], BIG_ROWS):
            o_ref[0, off:off + rows, :] = ref[...].astype(BF16)
            off += rows

    one = pl.Buffered(1)
    grid_spec = pltpu.PrefetchScalarGridSpec(
        num_scalar_prefetch=1, grid=(1,),
        in_specs=[pl.BlockSpec((rows, D_MODEL), lambda i, pr: (0, 0), pipeline_mode=one) for rows in BIG_ROWS],
        out_specs=pl.BlockSpec((1, PACK_ROWS, D_MODEL), lambda i, pr: (pr[0], 0, 0), pipeline_mode=one))
    return pl.pallas_call(
        body, name=name, grid_spec=grid_spec,
        out_shape=jax.ShapeDtypeStruct((N_CHIPS, PACK_ROWS, D_MODEL), BF16),
        compiler_params=_cparams(("arbitrary",)),
    )(place, *parts)


def _mesh_place():
    x, y, c = lax.axis_index("x"), lax.axis_index("y"), lax.axis_index("c")
    others = [(1 - x, y), (x, 1 - y), (1 - x, 1 - y)]
    return x, y, c, others


def _half(c):
    return pl.ds(pl.multiple_of(c * HALF_ROWS, 16), HALF_ROWS)


def _all_gather_pack(pack, *, name):
    def body(p_ref, o_ref, send_sems, recv_sems):
        x, y, c, others = _mesh_place()
        me = 2 * x + y
        sibling = (x, y, 1 - c)
        mine, theirs = _half(c), _half(1 - c)

        def copy(k, src, dst, to):
            return pltpu.make_async_remote_copy(src_ref=src, dst_ref=dst, send_sem=send_sems.at[k],
                                                recv_sem=recv_sems.at[k], device_id=to, device_id_type=MESH)

        first = [copy(j, o_ref.at[me, mine], o_ref.at[me, mine], (px, py, c)) for j, (px, py) in enumerate(others)]
        for cp in first:
            cp.start()
        passed = []
        for j, (px, py) in enumerate(others):
            slab = o_ref.at[2 * px + py, mine]
            copy(j, slab, slab, (px, py, c)).wait_recv()
            fwd = copy(3 + j, slab, slab, sibling)
            fwd.start()
            passed.append(fwd)
        for j, (px, py) in enumerate(others):
            slab = o_ref.at[2 * px + py, theirs]
            copy(3 + j, slab, slab, sibling).wait_recv()
        for cp in first + passed:
            cp.wait_send()

    return pl.pallas_call(
        body, name=name,
        in_specs=[pl.BlockSpec(memory_space=pl.ANY)],
        out_specs=pl.BlockSpec(memory_space=pl.ANY),
        out_shape=jax.ShapeDtypeStruct((N_CHIPS, PACK_ROWS, D_MODEL), BF16),
        input_output_aliases={0: 0},
        scratch_shapes=[pltpu.SemaphoreType.DMA((6,)), pltpu.SemaphoreType.DMA((6,))],
    )(pack)


def _rs_sibling(grads, *, name):
    def body(g_ref, o_ref, send_sem, recv_sem):
        x, y, c, _ = _mesh_place()
        cp = pltpu.make_async_remote_copy(src_ref=g_ref.at[:, _half(1 - c), :], dst_ref=o_ref, send_sem=send_sem,
                                          recv_sem=recv_sem, device_id=(x, y, 1 - c), device_id_type=MESH)
        cp.start()
        cp.wait()

    return pl.pallas_call(
        body, name=name,
        in_specs=[pl.BlockSpec(memory_space=pl.ANY)],
        out_specs=pl.BlockSpec(memory_space=pl.ANY),
        out_shape=jax.ShapeDtypeStruct((N_CHIPS, HALF_ROWS, D_MODEL), F32),
        scratch_shapes=[pltpu.SemaphoreType.DMA, pltpu.SemaphoreType.DMA],
    )(grads)


def _rs_add_halves(place, grads, recv, *, name):
    nt = HALF_ROWS // RS_TILE

    def body(place_ref, a_ref, b_ref, o_ref):
        o_ref[...] = a_ref[...] + b_ref[...]

    grid_spec = pltpu.PrefetchScalarGridSpec(
        num_scalar_prefetch=1, grid=(N_CHIPS, nt),
        in_specs=[pl.BlockSpec((1, RS_TILE, D_MODEL), lambda s, i, pr: (s, pr[1] * nt + i, 0)),
                  pl.BlockSpec((1, RS_TILE, D_MODEL), lambda s, i, pr: (s, i, 0))],
        out_specs=pl.BlockSpec((1, RS_TILE, D_MODEL), lambda s, i, pr: (s, i, 0)))
    return pl.pallas_call(
        body, name=name, grid_spec=grid_spec,
        out_shape=jax.ShapeDtypeStruct((N_CHIPS, HALF_ROWS, D_MODEL), F32),
        compiler_params=_cparams(("arbitrary", "arbitrary")),
    )(place, grads, recv)


def _rs_chips(part, *, name):
    def body(p_ref, o_ref, send_sems, recv_sems):
        x, y, c, others = _mesh_place()
        me = 2 * x + y
        sends = [pltpu.make_async_remote_copy(
            src_ref=p_ref.at[2 * px + py], dst_ref=o_ref.at[me], send_sem=send_sems.at[j], recv_sem=recv_sems.at[j],
            device_id=(px, py, c), device_id_type=MESH) for j, (px, py) in enumerate(others)]
        for cp in sends:
            cp.start()
        for j, (px, py) in enumerate(others):
            slab = o_ref.at[2 * px + py]
            pltpu.make_async_remote_copy(src_ref=slab, dst_ref=slab, send_sem=send_sems.at[j],
                                         recv_sem=recv_sems.at[j], device_id=(px, py, c),
                                         device_id_type=MESH).wait_recv()
        for cp in sends:
            cp.wait_send()

    return pl.pallas_call(
        body, name=name,
        in_specs=[pl.BlockSpec(memory_space=pl.ANY)],
        out_specs=pl.BlockSpec(memory_space=pl.ANY),
        out_shape=jax.ShapeDtypeStruct((N_CHIPS, HALF_ROWS, D_MODEL), F32),
        scratch_shapes=[pltpu.SemaphoreType.DMA((3,)), pltpu.SemaphoreType.DMA((3,))],
    )(part)


def _rs_add_chips(place, part, recv, *, name):
    nt = HALF_ROWS // RS_TILE

    def body(place_ref, own_ref, r1_ref, r2_ref, r3_ref, o_ref):
        o_ref[...] = (own_ref[0] + r1_ref[0]) + (r2_ref[0] + r3_ref[0])

    def slab(d):
        return pl.BlockSpec((1, RS_TILE, D_MODEL), lambda i, pr: ((pr[0] + d) % N_CHIPS, i, 0))

    grid_spec = pltpu.PrefetchScalarGridSpec(
        num_scalar_prefetch=1, grid=(nt,),
        in_specs=[slab(0), slab(1), slab(2), slab(3)],
        out_specs=pl.BlockSpec((RS_TILE, D_MODEL), lambda i, pr: (pr[1] * nt + i, 0)))
    return pl.pallas_call(
        body, name=name, grid_spec=grid_spec,
        out_shape=jax.ShapeDtypeStruct((PACK_ROWS, D_MODEL), F32),
        compiler_params=_cparams(("arbitrary",)),
    )(place, part, recv, recv, recv)


def _rs_share(shard, *, name):
    def body(s_ref, o_ref, send_sem, recv_sem):
        x, y, c, _ = _mesh_place()
        mine = o_ref.at[_half(c)]
        cp = pltpu.make_async_remote_copy(src_ref=mine, dst_ref=mine, send_sem=send_sem, recv_sem=recv_sem,
                                          device_id=(x, y, 1 - c), device_id_type=MESH)
        cp.start()
        theirs = o_ref.at[_half(1 - c)]
        pltpu.make_async_remote_copy(src_ref=theirs, dst_ref=theirs, send_sem=send_sem, recv_sem=recv_sem,
                                     device_id=(x, y, 1 - c), device_id_type=MESH).wait_recv()
        cp.wait_send()

    return pl.pallas_call(
        body, name=name,
        in_specs=[pl.BlockSpec(memory_space=pl.ANY)],
        out_specs=pl.BlockSpec(memory_space=pl.ANY),
        out_shape=jax.ShapeDtypeStruct((PACK_ROWS, D_MODEL), F32),
        input_output_aliases={0: 0},
        scratch_shapes=[pltpu.SemaphoreType.DMA, pltpu.SemaphoreType.DMA],
    )(shard)


def _small_all_reduce(packed, *, name):
    rows = packed.shape[0]

    def body(p_ref, o_ref, sib_ref, slots_ref, send_sems, recv_sems):
        x, y, c, others = _mesh_place()
        me = 2 * x + y
        sib = pltpu.make_async_remote_copy(src_ref=p_ref, dst_ref=sib_ref, send_sem=send_sems.at[0],
                                           recv_sem=recv_sems.at[0], device_id=(x, y, 1 - c), device_id_type=MESH)
        sib.start()
        sib.wait()
        slots_ref[me] = p_ref[...] + sib_ref[...]
        sends = [pltpu.make_async_remote_copy(
            src_ref=slots_ref.at[me], dst_ref=slots_ref.at[me], send_sem=send_sems.at[1 + j],
            recv_sem=recv_sems.at[1 + j], device_id=(px, py, c), device_id_type=MESH)
            for j, (px, py) in enumerate(others)]
        for cp in sends:
            cp.start()
        for j, (px, py) in enumerate(others):
            slab = slots_ref.at[2 * px + py]
            pltpu.make_async_remote_copy(src_ref=slab, dst_ref=slab, send_sem=send_sems.at[1 + j],
                                         recv_sem=recv_sems.at[1 + j], device_id=(px, py, c),
                                         device_id_type=MESH).wait_recv()
        for cp in sends:
            cp.wait_send()
        o_ref[...] = (slots_ref[0] + slots_ref[1]) + (slots_ref[2] + slots_ref[3])

    vm = pl.BlockSpec(memory_space=pltpu.VMEM)
    return pl.pallas_call(
        body, name=name, in_specs=[vm], out_specs=vm,
        out_shape=jax.ShapeDtypeStruct((rows, 128), F32),
        scratch_shapes=[pltpu.VMEM((rows, 128), F32), pltpu.VMEM((N_CHIPS, rows, 128), F32),
                        pltpu.SemaphoreType.DMA((4,)), pltpu.SemaphoreType.DMA((4,))],
    )(packed)


def _adamw(w, g, m, v, *, g_row0, tile, name):
    rows, cols = w.shape
    assert g_row0 % tile == 0 and rows % tile == 0

    def body(w_ref, g_ref, m_ref, v_ref, go_ref, d_ref, nm_ref, nv_ref):
        g_v = g_ref[...]
        m_n = ADAM_B1 * m_ref[...] + (1.0 - ADAM_B1) * g_v
        v_n = ADAM_B2 * v_ref[...] + (1.0 - ADAM_B2) * (g_v * g_v)
        m_hat = m_n / (1.0 - ADAM_B1 ** ADAM_STEP)
        v_hat = v_n / (1.0 - ADAM_B2 ** ADAM_STEP)
        d_ref[...] = -ADAM_LR * (m_hat / (jnp.sqrt(v_hat) + ADAM_EPS) + ADAM_WD * w_ref[...])
        go_ref[...] = g_v
        nm_ref[...] = m_n
        nv_ref[...] = v_n

    spec = pl.BlockSpec((tile, cols), lambda i: (i, 0))
    gspec = pl.BlockSpec((tile, cols), lambda i: (g_row0 // tile + i, 0))
    shape = jax.ShapeDtypeStruct((rows, cols), F32)
    return pl.pallas_call(
        body, name=name, grid=(rows // tile,),
        in_specs=[spec, gspec, spec, spec], out_specs=[spec] * 4, out_shape=[shape] * 4,
        compiler_params=_cparams(("arbitrary",)),
    )(w, g, m, v)


def kernel(x, ffn1_norm_g, ffn1_w_gate, ffn1_w_up, ffn1_w_down, mix_norm_g, w_in, b_in, attn_sinks, gmlp_ln_g, gmlp_ln_b, gmlp_w_s, gmlp_b_s, attn_out_norm_g, gmlp_out_norm_g, w_out, b_out, ffn2_norm_g, ffn2_w_gate, ffn2_w_up, ffn2_w_down, final_norm_g, loss_target, m_ffn1_norm_g, m_ffn1_w_gate, m_ffn1_w_up, m_ffn1_w_down, m_mix_norm_g, m_w_in, m_b_in, m_attn_sinks, m_gmlp_ln_g, m_gmlp_ln_b, m_gmlp_w_s, m_gmlp_b_s, m_attn_out_norm_g, m_gmlp_out_norm_g, m_w_out, m_b_out, m_ffn2_norm_g, m_ffn2_w_gate, m_ffn2_w_up, m_ffn2_w_down, m_final_norm_g, v_ffn1_norm_g, v_ffn1_w_gate, v_ffn1_w_up, v_ffn1_w_down, v_mix_norm_g, v_w_in, v_b_in, v_attn_sinks, v_gmlp_ln_g, v_gmlp_ln_b, v_gmlp_w_s, v_gmlp_b_s, v_attn_out_norm_g, v_gmlp_out_norm_g, v_w_out, v_b_out, v_ffn2_norm_g, v_ffn2_w_gate, v_ffn2_w_up, v_ffn2_w_down, v_final_norm_g):
    f_args = dict(locals())
    weights = {n: f_args[n] for n in [nm for nm, _ in SMALL if nm != "loss"] + list(BIG)}
    shapes = {n: weights[n].shape for n in weights}
    shapes["loss"] = ()
    place = jnp.stack([2 * lax.axis_index("x") + lax.axis_index("y"), lax.axis_index("c")]).astype(jnp.int32)

    def with_cols(name, a):
        a2 = a.reshape(a.shape[-2], a.shape[-1])
        return a2.T if BIG_TRANSPOSED[BIG.index(name)] else a2

    def natural(name, a2):
        return (a2.T if BIG_TRANSPOSED[BIG.index(name)] else a2).reshape(shapes[name])

    pack = _all_gather_pack(_pack_cast(place, [with_cols(n, weights[n]) for n in BIG], name="pack_cast"),
                            name="ag_weights")
    mix_rows = pack[:, MIX_BLOCK * FF_SH:, :]
    p = {n: weights[n].reshape(1, -1) for n in ("ffn1_norm_g", "mix_norm_g", "b_in", "gmlp_ln_g", "gmlp_ln_b",
                                                "attn_out_norm_g", "gmlp_out_norm_g", "b_out", "ffn2_norm_g",
                                                "final_norm_g")}
    p["w_in_t"] = mix_rows[:, :IN_SH, :].reshape(IN_W, D_MODEL)
    p["w_out"] = mix_rows[:, IN_SH:, :].reshape(D_MODEL, D_MODEL)
    p["attn_sinks"] = attn_sinks.reshape(N_Q_HEADS)
    p["gmlp_w_s"] = gmlp_w_s.reshape(GMLP_GROUPS, BLK, BLK)
    p["bs_full"] = jnp.broadcast_to(gmlp_b_s.reshape(GMLP_GROUPS, BLK).T[:, :, None],
                                    (BLK, GMLP_GROUPS, GROUP_DIM)).reshape(BLK, GMLP_W)

    loss_part, dx0, grads, gs = _local_step(x[0], loss_target[0], p, pack)

    part = _rs_add_halves(place, grads, _rs_sibling(grads, name="rs_sibling"), name="rs_add_halves")
    shard = _rs_share(_rs_add_chips(place, part, _rs_chips(part, name="rs_chips"), name="rs_add_chips"),
                      name="rs_share")
    gs["gmlp_b_s"] = jnp.sum(gs["gmlp_b_s"].reshape(BLK, GMLP_GROUPS, GROUP_DIM), axis=-1).T
    gs["attn_sinks"] = gs["attn_sinks"][:, 0]
    gs["loss"] = loss_part[0, 0]
    small_sum = _small_all_reduce(_pack_small(gs), name="small_all_reduce")

    grad_w, delta, new_m, new_v = {}, {}, {}, {}
    off = 0
    for n, rows in zip(BIG, BIG_ROWS):
        res = _adamw(with_cols(n, weights[n]), shard, with_cols(n, f_args["m_" + n]), with_cols(n, f_args["v_" + n]),
                     g_row0=off, tile=FF_SH // 2 if rows == FF_SH else 64, name="adamw_" + n)
        grad_w[n], delta[n], new_m[n], new_v[n] = [natural(n, a) for a in res]
        off += rows
    sm = {k: {n: f_args[k + n] for n, _ in SMALL if n != "loss"} for k in ("", "m_", "v_")}
    for k in sm:
        sm[k]["loss"] = jnp.zeros((), F32)
    res = _adamw(_pack_small(sm[""]), small_sum, _pack_small(sm["m_"]), _pack_small(sm["v_"]),
                 g_row0=0, tile=SMALL_ROWS, name="adamw_small")
    small = _unpack_small(res[0], shapes)
    for dst, packed in ((grad_w, res[0]), (delta, res[1]), (new_m, res[2]), (new_v, res[3])):
        dst.update({n: a for n, a in _unpack_small(packed, shapes).items() if n != "loss"})

    order = ('ffn1_norm_g', 'ffn1_w_gate', 'ffn1_w_up', 'ffn1_w_down', 'mix_norm_g', 'w_in', 'b_in', 'attn_sinks',
             'gmlp_ln_g', 'gmlp_ln_b', 'gmlp_w_s', 'gmlp_b_s', 'attn_out_norm_g', 'gmlp_out_norm_g', 'w_out', 'b_out',
             'ffn2_norm_g', 'ffn2_w_gate', 'ffn2_w_up', 'ffn2_w_down', 'final_norm_g')
    return (small["loss"], dx0.reshape(x.shape), *[grad_w[n] for n in order], *[delta[n] for n in order],
            *[new_m[n] for n in order], *[new_v[n] for n in order])
```

```python
import functools

import jax
import jax.numpy as jnp
from jax import lax
from jax.experimental import pallas as pl
from jax.experimental.pallas import tpu as pltpu

F32 = jnp.float32
BF16 = jnp.bfloat16

D_MODEL = 1024
D_FF = 2816
N_CHIPS = 4
FF_SH = D_FF // N_CHIPS
N_Q_HEADS = 8
N_KV_HEADS = 2
REP = N_Q_HEADS // N_KV_HEADS
HEAD_DIM = 64
ATTN_W = 512
KV_W = 128
GMLP_W = 512
GMLP_GROUPS = 8
GROUP_DIM = 64
BLK = 128
IN_W = 1792
IN_SH = IN_W // N_CHIPS
OUT_SH = D_MODEL // N_CHIPS
EPS = 1e-6
FFN_RES = 0.5
ATTN_SCALE = HEAD_DIM ** -0.5

ADAM_LR = 0.001
ADAM_B1 = 0.9
ADAM_B2 = 0.999
ADAM_EPS = 1e-08
ADAM_WD = 0.01
ADAM_STEP = 10

V7X_VMEM_LIMIT = 56 * 1024 * 1024
MESH = pl.DeviceIdType.MESH


def _cparams(sem):
    return pltpu.CompilerParams(dimension_semantics=sem, vmem_limit_bytes=V7X_VMEM_LIMIT)


def _dot(a, b):
    return jnp.dot(a, b, preferred_element_type=F32)


def _dot_nt(a, b):
    return lax.dot_general(a, b, (((1,), (1,)), ((), ())), preferred_element_type=F32)


def _dot_tn(a, b):
    return lax.dot_general(a, b, (((0,), (0,)), ((), ())), preferred_element_type=F32)


def _rms(x, g):
    r = lax.rsqrt(jnp.mean(x * x, axis=-1, keepdims=True) + EPS)
    return x * r * g, r


def _rms_bwd(dh, x, r, g):
    gy = dh * g
    dx = r * gy - x * (r * r * r) * jnp.mean(gy * x, axis=-1, keepdims=True)
    dg = jnp.sum(dh * x * r, axis=0, keepdims=True)
    return dx, dg


def _const(shape):
    nd = len(shape)
    return pl.BlockSpec(shape, lambda *_: (0,) * nd)


def _rows(t, w):
    return pl.BlockSpec((t, w), lambda i: (i, 0))


PACK_ROWS = 7 * FF_SH
HALF_ROWS = PACK_ROWS // 2
RS_TILE = HALF_ROWS // 7
MIX_BLOCK = 6
BIG = ("ffn1_w_gate", "ffn1_w_up", "ffn1_w_down", "ffn2_w_gate", "ffn2_w_up", "ffn2_w_down", "w_in", "w_out")
BIG_ROWS = (FF_SH, FF_SH, FF_SH, FF_SH, FF_SH, FF_SH, IN_SH, OUT_SH)
BIG_TRANSPOSED = (True, True, False, True, True, False, True, False)

SMALL = (("ffn1_norm_g", 1024), ("mix_norm_g", 1024), ("b_in", 1792), ("attn_sinks", 8), ("gmlp_ln_g", 512),
         ("gmlp_ln_b", 512), ("gmlp_w_s", 131072), ("gmlp_b_s", 1024), ("attn_out_norm_g", 512),
         ("gmlp_out_norm_g", 512), ("b_out", 1024), ("ffn2_norm_g", 1024), ("final_norm_g", 1024), ("loss", 1))


def _small_rows(n):
    return -(-n // 1024) * 8


SMALL_ROWS = sum(_small_rows(n) for _, n in SMALL)


def _pack_small(parts):
    out = []
    for name, n in SMALL:
        flat = parts[name].reshape(-1).astype(F32)
        rows = _small_rows(n)
        out.append(jnp.pad(flat, (0, rows * 128 - n)).reshape(rows, 128))
    return jnp.concatenate(out, axis=0)


def _unpack_small(packed, shapes):
    res, off = {}, 0
    for name, n in SMALL:
        rows = _small_rows(n)
        res[name] = packed[off:off + rows].reshape(-1)[:n].reshape(shapes[name])
        off += rows
    return res


def _ffn_tile(x, g, wg_ref, wu_ref, wd_ref):
    h, _ = _rms(x, g)
    hb = h.astype(BF16)
    acc = jnp.zeros(x.shape, F32)
    for j in range(N_CHIPS):
        a = _dot_nt(hb, wg_ref[j])
        b = _dot_nt(hb, wu_ref[j])
        f = (a * jax.nn.sigmoid(a) * b).astype(BF16)
        acc = acc + _dot(f, wd_ref[j])
    return x + FFN_RES * acc


def _ffn_weight_specs(k0):
    one = pl.Buffered(1)
    return [pl.BlockSpec((N_CHIPS, FF_SH, D_MODEL), functools.partial(lambda kk, i: (0, kk, 0), k0 + d),
                         pipeline_mode=one) for d in range(3)]


def _ffn_fwd(x, g, pack, k0, *, tile, name):
    s = x.shape[0]

    def body(x_ref, g_ref, wg_ref, wu_ref, wd_ref, o_ref):
        o_ref[...] = _ffn_tile(x_ref[...], g_ref[...], wg_ref, wu_ref, wd_ref)

    return pl.pallas_call(
        body, name=name, grid=(s // tile,),
        in_specs=[_rows(tile, D_MODEL), _const((1, D_MODEL))] + _ffn_weight_specs(k0),
        out_specs=_rows(tile, D_MODEL),
        out_shape=jax.ShapeDtypeStruct(x.shape, F32),
        compiler_params=_cparams(("arbitrary",)),
    )(x, g, pack, pack, pack)


def _ffn_fwd_loss(x, g, pack, k0, gf, tgt, *, tile, name):
    s = x.shape[0]

    def body(x_ref, g_ref, wg_ref, wu_ref, wd_ref, gf_ref, t_ref, dx_ref, loss_ref, dgf_ref):
        @pl.when(pl.program_id(0) == 0)
        def _():
            loss_ref[...] = jnp.zeros_like(loss_ref)
            dgf_ref[...] = jnp.zeros_like(dgf_ref)

        x3 = _ffn_tile(x_ref[...], g_ref[...], wg_ref, wu_ref, wd_ref)
        gf_v = gf_ref[...]
        out, r = _rms(x3, gf_v)
        diff = out - t_ref[...]
        part = jnp.sum(jnp.sum(diff * diff, axis=-1, keepdims=True), axis=0, keepdims=True)
        loss_ref[...] += jnp.broadcast_to(part * (0.5 / D_MODEL), loss_ref.shape)
        dx, dg = _rms_bwd(diff * (1.0 / D_MODEL), x3, r, gf_v)
        dx_ref[...] = dx
        dgf_ref[...] += dg

    return pl.pallas_call(
        body, name=name, grid=(s // tile,),
        in_specs=[_rows(tile, D_MODEL), _const((1, D_MODEL))] + _ffn_weight_specs(k0)
                 + [_const((1, D_MODEL)), _rows(tile, D_MODEL)],
        out_specs=[_rows(tile, D_MODEL), _const((1, 128)), _const((1, D_MODEL))],
        out_shape=[jax.ShapeDtypeStruct(x.shape, F32),
                   jax.ShapeDtypeStruct((1, 128), F32),
                   jax.ShapeDtypeStruct((1, D_MODEL), F32)],
        compiler_params=_cparams(("arbitrary",)),
    )(x, g, pack, pack, pack, gf, tgt)


def _ffn_bwd(x, dy, g, pack, k0, grads, *, tile, name):
    s = x.shape[0]

    def body(x_ref, dy_ref, g_ref, wg_ref, wu_ref, wd_ref, *rest):
        dhp_ref, gw_ref = rest[-2], rest[-1]

        @pl.when(pl.program_id(1) == 0)
        def _():
            gw_ref[...] = jnp.zeros_like(gw_ref)

        h, _ = _rms(x_ref[...], g_ref[...])
        hb = h.astype(BF16)
        dob = (FFN_RES * dy_ref[...]).astype(BF16)
        wg_j, wu_j, wd_j = wg_ref[0], wu_ref[0], wd_ref[0]
        a = _dot_nt(hb, wg_j)
        b = _dot_nt(hb, wu_j)
        sg = jax.nn.sigmoid(a)
        sa = a * sg
        fb = (sa * b).astype(BF16)
        df = _dot_nt(dob, wd_j)
        dbb = (df * sa).astype(BF16)
        dab = (df * b * (sg + sa * (1.0 - sg))).astype(BF16)
        dhp_ref[0] = _dot(dab, wg_j) + _dot(dbb, wu_j)
        gw_ref[0, 0:FF_SH, :] += _dot_tn(dab, hb)
        gw_ref[0, FF_SH:2 * FF_SH, :] += _dot_tn(dbb, hb)
        gw_ref[0, 2 * FF_SH:3 * FF_SH, :] += _dot_tn(fb, dob)

    wspecs = [pl.BlockSpec((1, FF_SH, D_MODEL), functools.partial(lambda kk, j, i: (j, kk, 0), k0 + d))
              for d in range(3)]
    xspec = pl.BlockSpec((tile, D_MODEL), lambda j, i: (i, 0))
    in_specs = [xspec, xspec, pl.BlockSpec((1, D_MODEL), lambda j, i: (0, 0))] + wspecs
    operands = [x, dy, g, pack, pack, pack]
    aliases = {}
    if grads is not None:
        in_specs.append(pl.BlockSpec(memory_space=pl.ANY))
        operands.append(grads)
        aliases = {6: 1}
    k3 = k0 // 3
    return pl.pallas_call(
        body, name=name, grid=(N_CHIPS, s // tile),
        in_specs=in_specs,
        out_specs=[pl.BlockSpec((1, tile, D_MODEL), lambda j, i: (j, i, 0)),
                   pl.BlockSpec((1, 3 * FF_SH, D_MODEL), lambda j, i: (j, k3, 0))],
        out_shape=[jax.ShapeDtypeStruct((N_CHIPS, s, D_MODEL), F32),
                   jax.ShapeDtypeStruct((N_CHIPS, PACK_ROWS, D_MODEL), F32)],
        input_output_aliases=aliases,
        compiler_params=_cparams(("arbitrary", "arbitrary")),
    )(*operands)


def _mix_grads_pack(dw_in_t, dw_out, grads, *, name):
    def body(a_ref, b_ref, g_any, o_ref):
        o_ref[0, 0:IN_SH, :] = a_ref[0]
        o_ref[0, IN_SH:FF_SH, :] = b_ref[0]

    return pl.pallas_call(
        body, name=name, grid=(N_CHIPS,),
        in_specs=[pl.BlockSpec((1, IN_SH, D_MODEL), lambda j: (j, 0, 0)),
                  pl.BlockSpec((1, OUT_SH, D_MODEL), lambda j: (j, 0, 0)),
                  pl.BlockSpec(memory_space=pl.ANY)],
        out_specs=pl.BlockSpec((1, FF_SH, D_MODEL), lambda j: (j, MIX_BLOCK, 0)),
        out_shape=jax.ShapeDtypeStruct((N_CHIPS, PACK_ROWS, D_MODEL), F32),
        input_output_aliases={2: 0},
        compiler_params=_cparams(("arbitrary",)),
    )(dw_in_t.reshape(N_CHIPS, IN_SH, D_MODEL), dw_out.reshape(N_CHIPS, OUT_SH, D_MODEL), grads)


def _norm_bwd(dhp, x, dy, g, *, tile, name):
    s = x.shape[0]

    def body(dhp_ref, x_ref, dy_ref, g_ref, dx_ref, dg_ref):
        @pl.when(pl.program_id(0) == 0)
        def _():
            dg_ref[...] = jnp.zeros_like(dg_ref)

        dh = (dhp_ref[0] + dhp_ref[1]) + (dhp_ref[2] + dhp_ref[3])
        x_v = x_ref[...]
        r = lax.rsqrt(jnp.mean(x_v * x_v, axis=-1, keepdims=True) + EPS)
        dx, dg = _rms_bwd(dh, x_v, r, g_ref[...])
        dx_ref[...] = dy_ref[...] + dx
        dg_ref[...] += dg

    return pl.pallas_call(
        body, name=name, grid=(s // tile,),
        in_specs=[pl.BlockSpec((N_CHIPS, tile, D_MODEL), lambda i: (0, i, 0)),
                  _rows(tile, D_MODEL), _rows(tile, D_MODEL), _const((1, D_MODEL))],
        out_specs=[_rows(tile, D_MODEL), _const((1, D_MODEL))],
        out_shape=[jax.ShapeDtypeStruct(x.shape, F32), jax.ShapeDtypeStruct((1, D_MODEL), F32)],
        compiler_params=_cparams(("arbitrary",)),
    )(dhp, x, dy, g)


def _mix_in_fwd(x, g, w_in_t, b_in, *, tile, name):
    s = x.shape[0]

    def body(x_ref, g_ref, w_ref, b_ref, q_ref, k_ref, v_ref, z_ref):
        h, _ = _rms(x_ref[...], g_ref[...])
        proj = _dot_nt(h.astype(BF16), w_ref[...]) + b_ref[...]
        q_ref[...] = proj[:, :ATTN_W].astype(BF16)
        k_ref[...] = proj[:, ATTN_W:ATTN_W + KV_W].astype(BF16)
        v_ref[...] = proj[:, ATTN_W + KV_W:ATTN_W + 2 * KV_W].astype(BF16)
        z_ref[...] = proj[:, ATTN_W + 2 * KV_W:]

    return pl.pallas_call(
        body, name=name, grid=(s // tile,),
        in_specs=[_rows(tile, D_MODEL), _const((1, D_MODEL)), _const((IN_W, D_MODEL)), _const((1, IN_W))],
        out_specs=[_rows(tile, ATTN_W), _rows(tile, KV_W), _rows(tile, KV_W), _rows(tile, 2 * GMLP_W)],
        out_shape=[jax.ShapeDtypeStruct((s, ATTN_W), BF16), jax.ShapeDtypeStruct((s, KV_W), BF16),
                   jax.ShapeDtypeStruct((s, KV_W), BF16), jax.ShapeDtypeStruct((s, 2 * GMLP_W), F32)],
        compiler_params=_cparams(("arbitrary",)),
    )(x, g, w_in_t, b_in)


def _mix_in_bwd(x, dy, dq, dk, dv, dz, g, w_in_t, *, tile, name):
    s = x.shape[0]

    def body(x_ref, dy_ref, dq_ref, dk_ref, dv_ref, dz_ref, g_ref, w_ref, dx_ref, dw_ref, db_ref, dg_ref):
        @pl.when(pl.program_id(0) == 0)
        def _():
            dw_ref[...] = jnp.zeros_like(dw_ref)
            db_ref[...] = jnp.zeros_like(db_ref)
            dg_ref[...] = jnp.zeros_like(dg_ref)

        dproj = jnp.concatenate([dq_ref[...], dk_ref[...], dv_ref[...], dz_ref[...]], axis=-1)
        db_ref[...] += jnp.sum(dproj, axis=0, keepdims=True)
        dpb = dproj.astype(BF16)
        x_v = x_ref[...]
        g_v = g_ref[...]
        h, r = _rms(x_v, g_v)
        dw_ref[...] += _dot_tn(dpb, h.astype(BF16))
        dh = _dot(dpb, w_ref[...])
        dx, dg = _rms_bwd(dh, x_v, r, g_v)
        dx_ref[...] = dy_ref[...] + dx
        dg_ref[...] += dg

    return pl.pallas_call(
        body, name=name, grid=(s // tile,),
        in_specs=[_rows(tile, D_MODEL), _rows(tile, D_MODEL), _rows(tile, ATTN_W), _rows(tile, KV_W),
                  _rows(tile, KV_W), _rows(tile, 2 * GMLP_W), _const((1, D_MODEL)), _const((IN_W, D_MODEL))],
        out_specs=[_rows(tile, D_MODEL), _const((IN_W, D_MODEL)), _const((1, IN_W)), _const((1, D_MODEL))],
        out_shape=[jax.ShapeDtypeStruct(x.shape, F32), jax.ShapeDtypeStruct((IN_W, D_MODEL), F32),
                   jax.ShapeDtypeStruct((1, IN_W), F32), jax.ShapeDtypeStruct((1, D_MODEL), F32)],
        compiler_params=_cparams(("arbitrary",)),
    )(x, dy, dq, dk, dv, dz, g, w_in_t)


_GELU_C = 0.7978845608028654
_GELU_A = 0.044715


def _gelu(x):
    return 0.5 * x * (1.0 + jnp.tanh(_GELU_C * (x + _GELU_A * (x * x * x))))


def _gelu_grad(x):
    t = jnp.tanh(_GELU_C * (x + _GELU_A * (x * x * x)))
    return 0.5 * (1.0 + t) + 0.5 * x * (1.0 - t * t) * (_GELU_C * (1.0 + 3.0 * _GELU_A * (x * x)))


def _band(ref, i):
    prev = jnp.maximum(i - 1, 0)
    return jnp.concatenate([ref[pl.ds(pl.multiple_of(prev * BLK, BLK), BLK), :],
                            ref[pl.ds(pl.multiple_of(i * BLK, BLK), BLK), :]], axis=0)


def _band_mask(i):
    qpos = lax.broadcasted_iota(jnp.int32, (BLK, 2 * BLK), 0)
    kidx = lax.broadcasted_iota(jnp.int32, (BLK, 2 * BLK), 1)
    rel = qpos - kidx + BLK
    win = jnp.where(rel >= 0, jnp.where(rel < BLK, 1, 0), 0)
    real = jnp.where(kidx >= BLK, 1, jnp.where(i > 0, 1, 0))
    return (win * real) > 0


def _attn_probs(qh, kg, mask, sink):
    sc = _dot_nt(qh, kg) * ATTN_SCALE
    sc = jnp.where(mask, sc, -jnp.inf)
    m = jnp.maximum(jnp.max(sc, axis=-1, keepdims=True), sink)
    p = jnp.exp(sc - m)
    es = jnp.exp(sink - m)
    inv = 1.0 / (jnp.sum(p, axis=-1, keepdims=True) + es)
    return p * inv, es * inv


def _tril_mask():
    t = lax.broadcasted_iota(jnp.int32, (BLK, BLK), 0)
    s_ = lax.broadcasted_iota(jnp.int32, (BLK, BLK), 1)
    return s_ <= t


def _gmlp_fwd_parts(zg, lng, lnb, ws_ref, bs_full):
    z = _gelu(zg)
    u = z[:, :GMLP_W]
    zv = z[:, GMLP_W:]
    mu = jnp.mean(zv, axis=-1, keepdims=True)
    zc = zv - mu
    rstd = lax.rsqrt(jnp.mean(zc * zc, axis=-1, keepdims=True) + EPS)
    xh = zc * rstd
    vvb = (xh * lng + lnb).astype(BF16)
    tril = _tril_mask()
    wms, parts = [], []
    for gi in range(GMLP_GROUPS):
        wm = jnp.where(tril, ws_ref[gi], 0.0).astype(BF16)
        wms.append(wm)
        parts.append(_dot(wm, vvb[:, gi * GROUP_DIM:(gi + 1) * GROUP_DIM]))
    mixed = jnp.concatenate(parts, axis=-1) + bs_full
    return u, xh, rstd, vvb, wms, mixed


def _attn_fwd(q, kb, vb, mask, sink_ref):
    outs = []
    for h in range(N_Q_HEADS):
        gi = h // REP
        pn, _ = _attn_probs(q[:, h * HEAD_DIM:(h + 1) * HEAD_DIM], kb[:, gi * HEAD_DIM:(gi + 1) * HEAD_DIM],
                            mask, sink_ref[h])
        outs.append(_dot(pn.astype(BF16), vb[:, gi * HEAD_DIM:(gi + 1) * HEAD_DIM]))
    return jnp.concatenate(outs, axis=-1)


def _mix_core_fwd(x1, q, k, v, zg, sinks, lng, lnb, w_s, bs_full, gao, ggo, w_out, b_out, *, name):
    s = x1.shape[0]

    def body(sink_ref, x_ref, q_ref, k_ref, v_ref, z_ref, lng_ref, lnb_ref, ws_ref, bs_ref, gao_ref, ggo_ref,
             wo_ref, bo_ref, o_ref):
        i = pl.program_id(0)
        mask = _band_mask(i)
        y_attn = _attn_fwd(q_ref[...], _band(k_ref, i), _band(v_ref, i), mask, sink_ref)
        u, _, _, _, _, mixed = _gmlp_fwd_parts(z_ref[...], lng_ref[...], lnb_ref[...], ws_ref, bs_ref[...])
        ya, _ = _rms(y_attn, gao_ref[...])
        yg, _ = _rms(u * mixed, ggo_ref[...])
        yb = jnp.concatenate([ya, yg], axis=-1).astype(BF16)
        o_ref[...] = x_ref[...] + (_dot(yb, wo_ref[...]) + bo_ref[...])

    return pl.pallas_call(
        body, name=name, grid=(s // BLK,),
        in_specs=[pl.BlockSpec(memory_space=pltpu.SMEM),
                  _rows(BLK, D_MODEL), _rows(BLK, ATTN_W), _const((s, KV_W)), _const((s, KV_W)),
                  _rows(BLK, 2 * GMLP_W), _const((1, GMLP_W)), _const((1, GMLP_W)),
                  _const((GMLP_GROUPS, BLK, BLK)), _const((BLK, GMLP_W)), _const((1, ATTN_W)), _const((1, GMLP_W)),
                  _const((D_MODEL, D_MODEL)), _const((1, D_MODEL))],
        out_specs=_rows(BLK, D_MODEL),
        out_shape=jax.ShapeDtypeStruct(x1.shape, F32),
        compiler_params=_cparams(("arbitrary",)),
    )(sinks, x1, q, k, v, zg, lng, lnb, w_s, bs_full, gao, ggo, w_out, b_out)


def _mix_core_bwd(dy, q, k, v, zg, sinks, lng, lnb, w_s, bs_full, gao, ggo, w_out, *, name):
    s = dy.shape[0]
    nblk = s // BLK

    def body(sink_ref, dy_ref, q_ref, k_ref, v_ref, z_ref, lng_ref, lnb_ref, ws_ref, bs_ref, gao_ref, ggo_ref,
             wo_ref, dq_ref, dk_ref, dv_ref, dz_ref, dwo_ref, dbo_ref, dgao_ref, dggo_ref, dlng_ref, dlnb_ref,
             dws_ref, dms_ref, dsk_ref):
        i = pl.program_id(0)

        @pl.when(i == 0)
        def _():
            for ref in (dk_ref, dv_ref, dwo_ref, dbo_ref, dgao_ref, dggo_ref, dlng_ref, dlnb_ref, dws_ref,
                        dms_ref, dsk_ref):
                ref[...] = jnp.zeros_like(ref)

        mask = _band_mask(i)
        q_v = q_ref[...]
        kb = _band(k_ref, i)
        vb = _band(v_ref, i)
        lng_v = lng_ref[...]
        gao_v = gao_ref[...]
        ggo_v = ggo_ref[...]
        zg_v = z_ref[...]

        y_attn = _attn_fwd(q_v, kb, vb, mask, sink_ref)
        u, xh, rstd, vvb, wms, mixed = _gmlp_fwd_parts(zg_v, lng_v, lnb_ref[...], ws_ref, bs_ref[...])
        y_gmlp = u * mixed
        ya, ra = _rms(y_attn, gao_v)
        yg, rg = _rms(y_gmlp, ggo_v)
        yb = jnp.concatenate([ya, yg], axis=-1).astype(BF16)

        dy_v = dy_ref[...]
        dyb = dy_v.astype(BF16)
        dwo_ref[...] += _dot_tn(yb, dyb)
        dbo_ref[...] += jnp.sum(dy_v, axis=0, keepdims=True)
        dyy = _dot_nt(dyb, wo_ref[...])
        d_attn, dgao = _rms_bwd(dyy[:, :ATTN_W], y_attn, ra, gao_v)
        d_gmlp, dggo = _rms_bwd(dyy[:, ATTN_W:], y_gmlp, rg, ggo_v)
        dgao_ref[...] += dgao
        dggo_ref[...] += dggo

        du = d_gmlp * mixed
        dmixed = d_gmlp * u
        dms_ref[...] += dmixed
        dmb = dmixed.astype(BF16)
        dvv_parts = []
        for gi in range(GMLP_GROUPS):
            sl = slice(gi * GROUP_DIM, (gi + 1) * GROUP_DIM)
            dws_ref[gi] += _dot_nt(dmb[:, sl], vvb[:, sl])
            dvv_parts.append(_dot_tn(wms[gi], dmb[:, sl]))
        dvv = jnp.concatenate(dvv_parts, axis=-1)
        dlng_ref[...] += jnp.sum(dvv * xh, axis=0, keepdims=True)
        dlnb_ref[...] += jnp.sum(dvv, axis=0, keepdims=True)
        dxh = dvv * lng_v
        dzv = rstd * (dxh - jnp.mean(dxh, axis=-1, keepdims=True)
                      - xh * jnp.mean(dxh * xh, axis=-1, keepdims=True))
        dz_ref[...] = jnp.concatenate([du, dzv], axis=-1) * _gelu_grad(zg_v)

        dab = d_attn.astype(BF16)
        dq_parts = []
        dk_parts = []
        dv_parts = []
        for gi in range(N_KV_HEADS):
            kg = kb[:, gi * HEAD_DIM:(gi + 1) * HEAD_DIM]
            vg = vb[:, gi * HEAD_DIM:(gi + 1) * HEAD_DIM]
            dkg = jnp.zeros((2 * BLK, HEAD_DIM), F32)
            dvg = jnp.zeros((2 * BLK, HEAD_DIM), F32)
            for rr in range(REP):
                h = gi * REP + rr
                hs = slice(h * HEAD_DIM, (h + 1) * HEAD_DIM)
                qh = q_v[:, hs]
                doh = dab[:, hs]
                pn, psink = _attn_probs(qh, kg, mask, sink_ref[h])
                dp = _dot_nt(doh, vg)
                delta = jnp.sum(pn * dp, axis=-1, keepdims=True)
                dsb = (pn * (dp - delta) * ATTN_SCALE).astype(BF16)
                dsink = jnp.sum(-psink * delta, axis=0, keepdims=True)
                dsk_ref[pl.ds(h, 1), :] += jnp.broadcast_to(dsink, (1, 128))
                dq_parts.append(_dot(dsb, kg))
                dkg = dkg + _dot_tn(dsb, qh)
                dvg = dvg + _dot_tn(pn.astype(BF16), doh)
            dk_parts.append(dkg)
            dv_parts.append(dvg)
        dq_ref[...] = jnp.concatenate(dq_parts, axis=-1)
        dkb = jnp.concatenate(dk_parts, axis=-1)
        dvb = jnp.concatenate(dv_parts, axis=-1)
        prev = pl.ds(pl.multiple_of(jnp.maximum(i - 1, 0) * BLK, BLK), BLK)
        cur = pl.ds(pl.multiple_of(i * BLK, BLK), BLK)
        dk_ref[prev, :] += dkb[:BLK]
        dv_ref[prev, :] += dvb[:BLK]
        dk_ref[cur, :] += dkb[BLK:]
        dv_ref[cur, :] += dvb[BLK:]

        @pl.when(i == nblk - 1)
        def _():
            tril = _tril_mask()
            for gi in range(GMLP_GROUPS):
                dws_ref[gi] = jnp.where(tril, dws_ref[gi], 0.0)

    return pl.pallas_call(
        body, name=name, grid=(nblk,),
        in_specs=[pl.BlockSpec(memory_space=pltpu.SMEM),
                  _rows(BLK, D_MODEL), _rows(BLK, ATTN_W), _const((s, KV_W)), _const((s, KV_W)),
                  _rows(BLK, 2 * GMLP_W), _const((1, GMLP_W)), _const((1, GMLP_W)),
                  _const((GMLP_GROUPS, BLK, BLK)), _const((BLK, GMLP_W)), _const((1, ATTN_W)), _const((1, GMLP_W)),
                  _const((D_MODEL, D_MODEL))],
        out_specs=[_rows(BLK, ATTN_W), _const((s, KV_W)), _const((s, KV_W)), _rows(BLK, 2 * GMLP_W),
                   _const((D_MODEL, D_MODEL)), _const((1, D_MODEL)), _const((1, ATTN_W)), _const((1, GMLP_W)),
                   _const((1, GMLP_W)), _const((1, GMLP_W)), _const((GMLP_GROUPS, BLK, BLK)),
                   _const((BLK, GMLP_W)), _const((N_Q_HEADS, 128))],
        out_shape=[jax.ShapeDtypeStruct((s, ATTN_W), F32), jax.ShapeDtypeStruct((s, KV_W), F32),
                   jax.ShapeDtypeStruct((s, KV_W), F32), jax.ShapeDtypeStruct((s, 2 * GMLP_W), F32),
                   jax.ShapeDtypeStruct((D_MODEL, D_MODEL), F32), jax.ShapeDtypeStruct((1, D_MODEL), F32),
                   jax.ShapeDtypeStruct((1, ATTN_W), F32), jax.ShapeDtypeStruct((1, GMLP_W), F32),
                   jax.ShapeDtypeStruct((1, GMLP_W), F32), jax.ShapeDtypeStruct((1, GMLP_W), F32),
                   jax.ShapeDtypeStruct((GMLP_GROUPS, BLK, BLK), F32), jax.ShapeDtypeStruct((BLK, GMLP_W), F32),
                   jax.ShapeDtypeStruct((N_Q_HEADS, 128), F32)],
        compiler_params=_cparams(("arbitrary",)),
    )(sinks, dy, q, k, v, zg, lng, lnb, w_s, bs_full, gao, ggo, w_out)


def _local_step(x, tgt, p, pack, *, tile=512, bwd_tile=256):
    g = {}
    tile, bwd_tile = min(tile, x.shape[0]), min(bwd_tile, x.shape[0])
    x1 = _ffn_fwd(x, p["ffn1_norm_g"], pack, 0, tile=tile, name="ffn1_fwd")
    q, k, v, zg = _mix_in_fwd(x1, p["mix_norm_g"], p["w_in_t"], p["b_in"], tile=tile, name="mix_in_fwd")
    mix_args = (q, k, v, zg, p["attn_sinks"], p["gmlp_ln_g"], p["gmlp_ln_b"], p["gmlp_w_s"], p["bs_full"],
                p["attn_out_norm_g"], p["gmlp_out_norm_g"], p["w_out"])
    x2 = _mix_core_fwd(x1, *mix_args, p["b_out"], name="mix_core_fwd")
    dx3, loss, g["final_norm_g"] = _ffn_fwd_loss(x2, p["ffn2_norm_g"], pack, 3, p["final_norm_g"], tgt,
                                                 tile=tile, name="ffn2_fwd_loss")

    dhp, grads = _ffn_bwd(x2, dx3, p["ffn2_norm_g"], pack, 3, None, tile=bwd_tile, name="ffn2_bwd")
    dx2, g["ffn2_norm_g"] = _norm_bwd(dhp, x2, dx3, p["ffn2_norm_g"], tile=bwd_tile, name="ffn2_norm_bwd")

    (dq, dk, dv, dz, dw_out, g["b_out"], g["attn_out_norm_g"], g["gmlp_out_norm_g"], g["gmlp_ln_g"],
     g["gmlp_ln_b"], g["gmlp_w_s"], dmix_sum, dsinks) = _mix_core_bwd(dx2, *mix_args, name="mix_core_bwd")
    g["gmlp_b_s"] = dmix_sum
    g["attn_sinks"] = dsinks
    dx1, dw_in_t, g["b_in"], g["mix_norm_g"] = _mix_in_bwd(
        x1, dx2, dq, dk, dv, dz, p["mix_norm_g"], p["w_in_t"], tile=tile, name="mix_in_bwd")
    grads = _mix_grads_pack(dw_in_t, dw_out, grads, name="mix_grads_pack")

    dhp1, grads = _ffn_bwd(x, dx1, p["ffn1_norm_g"], pack, 0, grads, tile=bwd_tile, name="ffn1_bwd")
    dx0, g["ffn1_norm_g"] = _norm_bwd(dhp1, x, dx1, p["ffn1_norm_g"], tile=bwd_tile, name="ffn1_norm_bwd")
    return loss, dx0, grads, g


def _pack_cast(place, parts, *, name):
    def body(place_ref, *refs):
        o_ref = refs[-1]
        off = 0
        for ref, rows in zip(refs[:-1], BIG_ROWS):
            o_ref[0, off:off + rows, :] = ref[...].astype(BF16)
            off += rows

    one = pl.Buffered(1)
    grid_spec = pltpu.PrefetchScalarGridSpec(
        num_scalar_prefetch=1, grid=(1,),
        in_specs=[pl.BlockSpec((rows, D_MODEL), lambda i, pr: (0, 0), pipeline_mode=one) for rows in BIG_ROWS],
        out_specs=pl.BlockSpec((1, PACK_ROWS, D_MODEL), lambda i, pr: (pr[0], 0, 0), pipeline_mode=one))
    return pl.pallas_call(
        body, name=name, grid_spec=grid_spec,
        out_shape=jax.ShapeDtypeStruct((N_CHIPS, PACK_ROWS, D_MODEL), BF16),
        compiler_params=_cparams(("arbitrary",)),
    )(place, *parts)


def _mesh_place():
    x, y, c = lax.axis_index("x"), lax.axis_index("y"), lax.axis_index("c")
    others = [(1 - x, y), (x, 1 - y), (1 - x, 1 - y)]
    return x, y, c, others


def _half(c):
    return pl.ds(pl.multiple_of(c * HALF_ROWS, 16), HALF_ROWS)


def _all_gather_pack(pack, *, name):
    def body(p_ref, o_ref, send_sems, recv_sems):
        x, y, c, others = _mesh_place()
        me = 2 * x + y
        sibling = (x, y, 1 - c)
        mine, theirs = _half(c), _half(1 - c)

        def copy(k, src, dst, to):
            return pltpu.make_async_remote_copy(src_ref=src, dst_ref=dst, send_sem=send_sems.at[k],
                                                recv_sem=recv_sems.at[k], device_id=to, device_id_type=MESH)

        first = [copy(j, o_ref.at[me, mine], o_ref.at[me, mine], (px, py, c)) for j, (px, py) in enumerate(others)]
        for cp in first:
            cp.start()
        passed = []
        for j, (px, py) in enumerate(others):
            slab = o_ref.at[2 * px + py, mine]
            copy(j, slab, slab, (px, py, c)).wait_recv()
            fwd = copy(3 + j, slab, slab, sibling)
            fwd.start()
            passed.append(fwd)
        for j, (px, py) in enumerate(others):
            slab = o_ref.at[2 * px + py, theirs]
            copy(3 + j, slab, slab, sibling).wait_recv()
        for cp in first + passed:
            cp.wait_send()

    return pl.pallas_call(
        body, name=name,
        in_specs=[pl.BlockSpec(memory_space=pl.ANY)],
        out_specs=pl.BlockSpec(memory_space=pl.ANY),
        out_shape=jax.ShapeDtypeStruct((N_CHIPS, PACK_ROWS, D_MODEL), BF16),
        input_output_aliases={0: 0},
        scratch_shapes=[pltpu.SemaphoreType.DMA((6,)), pltpu.SemaphoreType.DMA((6,))],
    )(pack)


def _rs_sibling(grads, *, name):
    def body(g_ref, o_ref, send_sem, recv_sem):
        x, y, c, _ = _mesh_place()
        cp = pltpu.make_async_remote_copy(src_ref=g_ref.at[:, _half(1 - c), :], dst_ref=o_ref, send_sem=send_sem,
                                          recv_sem=recv_sem, device_id=(x, y, 1 - c), device_id_type=MESH)
        cp.start()
        cp.wait()

    return pl.pallas_call(
        body, name=name,
        in_specs=[pl.BlockSpec(memory_space=pl.ANY)],
        out_specs=pl.BlockSpec(memory_space=pl.ANY),
        out_shape=jax.ShapeDtypeStruct((N_CHIPS, HALF_ROWS, D_MODEL), F32),
        scratch_shapes=[pltpu.SemaphoreType.DMA, pltpu.SemaphoreType.DMA],
    )(grads)


def _rs_add_halves(place, grads, recv, *, name):
    nt = HALF_ROWS // RS_TILE

    def body(place_ref, a_ref, b_ref, o_ref):
        o_ref[...] = (a_ref[...] + b_ref[...]).astype(BF16)

    grid_spec = pltpu.PrefetchScalarGridSpec(
        num_scalar_prefetch=1, grid=(N_CHIPS, nt),
        in_specs=[pl.BlockSpec((1, RS_TILE, D_MODEL), lambda s, i, pr: (s, pr[1] * nt + i, 0)),
                  pl.BlockSpec((1, RS_TILE, D_MODEL), lambda s, i, pr: (s, i, 0))],
        out_specs=pl.BlockSpec((1, RS_TILE, D_MODEL), lambda s, i, pr: (s, i, 0)))
    return pl.pallas_call(
        body, name=name, grid_spec=grid_spec,
        out_shape=jax.ShapeDtypeStruct((N_CHIPS, HALF_ROWS, D_MODEL), BF16),
        compiler_params=_cparams(("arbitrary", "arbitrary")),
    )(place, grads, recv)


def _rs_chips(part, *, name):
    def body(p_ref, o_ref, send_sems, recv_sems):
        x, y, c, others = _mesh_place()
        me = 2 * x + y
        sends = [pltpu.make_async_remote_copy(
            src_ref=p_ref.at[2 * px + py], dst_ref=o_ref.at[me], send_sem=send_sems.at[j], recv_sem=recv_sems.at[j],
            device_id=(px, py, c), device_id_type=MESH) for j, (px, py) in enumerate(others)]
        for cp in sends:
            cp.start()
        for j, (px, py) in enumerate(others):
            slab = o_ref.at[2 * px + py]
            pltpu.make_async_remote_copy(src_ref=slab, dst_ref=slab, send_sem=send_sems.at[j],
                                         recv_sem=recv_sems.at[j], device_id=(px, py, c),
                                         device_id_type=MESH).wait_recv()
        for cp in sends:
            cp.wait_send()

    return pl.pallas_call(
        body, name=name,
        in_specs=[pl.BlockSpec(memory_space=pl.ANY)],
        out_specs=pl.BlockSpec(memory_space=pl.ANY),
        out_shape=jax.ShapeDtypeStruct(part.shape, part.dtype),
        scratch_shapes=[pltpu.SemaphoreType.DMA((3,)), pltpu.SemaphoreType.DMA((3,))],
    )(part)


def _rs_add_chips(place, part, recv, *, name):
    nt = HALF_ROWS // RS_TILE

    def body(place_ref, own_ref, r1_ref, r2_ref, r3_ref, o_ref):
        o_ref[...] = ((own_ref[0].astype(F32) + r1_ref[0].astype(F32))
                      + (r2_ref[0].astype(F32) + r3_ref[0].astype(F32)))

    def slab(d):
        return pl.BlockSpec((1, RS_TILE, D_MODEL), lambda i, pr: ((pr[0] + d) % N_CHIPS, i, 0))

    grid_spec = pltpu.PrefetchScalarGridSpec(
        num_scalar_prefetch=1, grid=(nt,),
        in_specs=[slab(0), slab(1), slab(2), slab(3)],
        out_specs=pl.BlockSpec((RS_TILE, D_MODEL), lambda i, pr: (pr[1] * nt + i, 0)))
    return pl.pallas_call(
        body, name=name, grid_spec=grid_spec,
        out_shape=jax.ShapeDtypeStruct((PACK_ROWS, D_MODEL), F32),
        compiler_params=_cparams(("arbitrary",)),
    )(place, part, recv, recv, recv)


def _rs_share(shard, *, name):
    def body(s_ref, o_ref, send_sem, recv_sem):
        x, y, c, _ = _mesh_place()
        mine = o_ref.at[_half(c)]
        cp = pltpu.make_async_remote_copy(src_ref=mine, dst_ref=mine, send_sem=send_sem, recv_sem=recv_sem,
                                          device_id=(x, y, 1 - c), device_id_type=MESH)
        cp.start()
        theirs = o_ref.at[_half(1 - c)]
        pltpu.make_async_remote_copy(src_ref=theirs, dst_ref=theirs, send_sem=send_sem, recv_sem=recv_sem,
                                     device_id=(x, y, 1 - c), device_id_type=MESH).wait_recv()
        cp.wait_send()

    return pl.pallas_call(
        body, name=name,
        in_specs=[pl.BlockSpec(memory_space=pl.ANY)],
        out_specs=pl.BlockSpec(memory_space=pl.ANY),
        out_shape=jax.ShapeDtypeStruct((PACK_ROWS, D_MODEL), F32),
        input_output_aliases={0: 0},
        scratch_shapes=[pltpu.SemaphoreType.DMA, pltpu.SemaphoreType.DMA],
    )(shard)


def _small_all_reduce(packed, *, name):
    rows = packed.shape[0]

    def body(p_ref, o_ref, sib_ref, slots_ref, send_sems, recv_sems):
        x, y, c, others = _mesh_place()
        me = 2 * x + y
        sib = pltpu.make_async_remote_copy(src_ref=p_ref, dst_ref=sib_ref, send_sem=send_sems.at[0],
                                           recv_sem=recv_sems.at[0], device_id=(x, y, 1 - c), device_id_type=MESH)
        sib.start()
        sib.wait()
        slots_ref[me] = p_ref[...] + sib_ref[...]
        sends = [pltpu.make_async_remote_copy(
            src_ref=slots_ref.at[me], dst_ref=slots_ref.at[me], send_sem=send_sems.at[1 + j],
            recv_sem=recv_sems.at[1 + j], device_id=(px, py, c), device_id_type=MESH)
            for j, (px, py) in enumerate(others)]
        for cp in sends:
            cp.start()
        for j, (px, py) in enumerate(others):
            slab = slots_ref.at[2 * px + py]
            pltpu.make_async_remote_copy(src_ref=slab, dst_ref=slab, send_sem=send_sems.at[1 + j],
                                         recv_sem=recv_sems.at[1 + j], device_id=(px, py, c),
                                         device_id_type=MESH).wait_recv()
        for cp in sends:
            cp.wait_send()
        o_ref[...] = (slots_ref[0] + slots_ref[1]) + (slots_ref[2] + slots_ref[3])

    vm = pl.BlockSpec(memory_space=pltpu.VMEM)
    return pl.pallas_call(
        body, name=name, in_specs=[vm], out_specs=vm,
        out_shape=jax.ShapeDtypeStruct((rows, 128), F32),
        scratch_shapes=[pltpu.VMEM((rows, 128), F32), pltpu.VMEM((N_CHIPS, rows, 128), F32),
                        pltpu.SemaphoreType.DMA((4,)), pltpu.SemaphoreType.DMA((4,))],
    )(packed)


def _adamw(w, g, m, v, *, g_row0, tile, name):
    rows, cols = w.shape
    assert g_row0 % tile == 0 and rows % tile == 0

    def body(w_ref, g_ref, m_ref, v_ref, go_ref, d_ref, nm_ref, nv_ref):
        g_v = g_ref[...]
        m_n = ADAM_B1 * m_ref[...] + (1.0 - ADAM_B1) * g_v
        v_n = ADAM_B2 * v_ref[...] + (1.0 - ADAM_B2) * (g_v * g_v)
        m_hat = m_n / (1.0 - ADAM_B1 ** ADAM_STEP)
        v_hat = v_n / (1.0 - ADAM_B2 ** ADAM_STEP)
        d_ref[...] = -ADAM_LR * (m_hat / (jnp.sqrt(v_hat) + ADAM_EPS) + ADAM_WD * w_ref[...])
        go_ref[...] = g_v
        nm_ref[...] = m_n
        nv_ref[...] = v_n

    spec = pl.BlockSpec((tile, cols), lambda i: (i, 0))
    gspec = pl.BlockSpec((tile, cols), lambda i: (g_row0 // tile + i, 0))
    shape = jax.ShapeDtypeStruct((rows, cols), F32)
    return pl.pallas_call(
        body, name=name, grid=(rows // tile,),
        in_specs=[spec, gspec, spec, spec], out_specs=[spec] * 4, out_shape=[shape] * 4,
        compiler_params=_cparams(("arbitrary",)),
    )(w, g, m, v)


def kernel(x, ffn1_norm_g, ffn1_w_gate, ffn1_w_up, ffn1_w_down, mix_norm_g, w_in, b_in, attn_sinks, gmlp_ln_g, gmlp_ln_b, gmlp_w_s, gmlp_b_s, attn_out_norm_g, gmlp_out_norm_g, w_out, b_out, ffn2_norm_g, ffn2_w_gate, ffn2_w_up, ffn2_w_down, final_norm_g, loss_target, m_ffn1_norm_g, m_ffn1_w_gate, m_ffn1_w_up, m_ffn1_w_down, m_mix_norm_g, m_w_in, m_b_in, m_attn_sinks, m_gmlp_ln_g, m_gmlp_ln_b, m_gmlp_w_s, m_gmlp_b_s, m_attn_out_norm_g, m_gmlp_out_norm_g, m_w_out, m_b_out, m_ffn2_norm_g, m_ffn2_w_gate, m_ffn2_w_up, m_ffn2_w_down, m_final_norm_g, v_ffn1_norm_g, v_ffn1_w_gate, v_ffn1_w_up, v_ffn1_w_down, v_mix_norm_g, v_w_in, v_b_in, v_attn_sinks, v_gmlp_ln_g, v_gmlp_ln_b, v_gmlp_w_s, v_gmlp_b_s, v_attn_out_norm_g, v_gmlp_out_norm_g, v_w_out, v_b_out, v_ffn2_norm_g, v_ffn2_w_gate, v_ffn2_w_up, v_ffn2_w_down, v_final_norm_g):
    f_args = dict(locals())
    weights = {n: f_args[n] for n in [nm for nm, _ in SMALL if nm != "loss"] + list(BIG)}
    shapes = {n: weights[n].shape for n in weights}
    shapes["loss"] = ()
    place = jnp.stack([2 * lax.axis_index("x") + lax.axis_index("y"), lax.axis_index("c")]).astype(jnp.int32)

    def with_cols(name, a):
        a2 = a.reshape(a.shape[-2], a.shape[-1])
        return a2.T if BIG_TRANSPOSED[BIG.index(name)] else a2

    def natural(name, a2):
        return (a2.T if BIG_TRANSPOSED[BIG.index(name)] else a2).reshape(shapes[name])

    pack = _all_gather_pack(_pack_cast(place, [with_cols(n, weights[n]) for n in BIG], name="pack_cast"),
                            name="ag_weights")
    mix_rows = pack[:, MIX_BLOCK * FF_SH:, :]
    p = {n: weights[n].reshape(1, -1) for n in ("ffn1_norm_g", "mix_norm_g", "b_in", "gmlp_ln_g", "gmlp_ln_b",
                                                "attn_out_norm_g", "gmlp_out_norm_g", "b_out", "ffn2_norm_g",
                                                "final_norm_g")}
    p["w_in_t"] = mix_rows[:, :IN_SH, :].reshape(IN_W, D_MODEL)
    p["w_out"] = mix_rows[:, IN_SH:, :].reshape(D_MODEL, D_MODEL)
    p["attn_sinks"] = attn_sinks.reshape(N_Q_HEADS)
    p["gmlp_w_s"] = gmlp_w_s.reshape(GMLP_GROUPS, BLK, BLK)
    p["bs_full"] = jnp.broadcast_to(gmlp_b_s.reshape(GMLP_GROUPS, BLK).T[:, :, None],
                                    (BLK, GMLP_GROUPS, GROUP_DIM)).reshape(BLK, GMLP_W)

    loss_part, dx0, grads, gs = _local_step(x[0], loss_target[0], p, pack)

    part = _rs_add_halves(place, grads, _rs_sibling(grads, name="rs_sibling"), name="rs_add_halves")
    shard = _rs_share(_rs_add_chips(place, part, _rs_chips(part, name="rs_chips"), name="rs_add_chips"),
                      name="rs_share")
    gs["gmlp_b_s"] = jnp.sum(gs["gmlp_b_s"].reshape(BLK, GMLP_GROUPS, GROUP_DIM), axis=-1).T
    gs["attn_sinks"] = gs["attn_sinks"][:, 0]
    gs["loss"] = loss_part[0, 0]
    small_sum = _small_all_reduce(_pack_small(gs), name="small_all_reduce")

    grad_w, delta, new_m, new_v = {}, {}, {}, {}
    off = 0
    for n, rows in zip(BIG, BIG_ROWS):
        res = _adamw(with_cols(n, weights[n]), shard, with_cols(n, f_args["m_" + n]), with_cols(n, f_args["v_" + n]),
                     g_row0=off, tile=FF_SH // 2 if rows == FF_SH else 64, name="adamw_" + n)
        grad_w[n], delta[n], new_m[n], new_v[n] = [natural(n, a) for a in res]
        off += rows
    sm = {k: {n: f_args[k + n] for n, _ in SMALL if n != "loss"} for k in ("", "m_", "v_")}
    for k in sm:
        sm[k]["loss"] = jnp.zeros((), F32)
    res = _adamw(_pack_small(sm[""]), small_sum, _pack_small(sm["m_"]), _pack_small(sm["v_"]),
                 g_row0=0, tile=SMALL_ROWS, name="adamw_small")
    small = _unpack_small(res[0], shapes)
    for dst, packed in ((grad_w, res[0]), (delta, res[1]), (new_m, res[2]), (new_v, res[3])):
        dst.update({n: a for n, a in _unpack_small(packed, shapes).items() if n != "loss"})

    order = ('ffn1_norm_g', 'ffn1_w_gate', 'ffn1_w_up', 'ffn1_w_down', 'mix_norm_g', 'w_in', 'b_in', 'attn_sinks',
             'gmlp_ln_g', 'gmlp_ln_b', 'gmlp_w_s', 'gmlp_b_s', 'attn_out_norm_g', 'gmlp_out_norm_g', 'w_out', 'b_out',
             'ffn2_norm_g', 'ffn2_w_gate', 'ffn2_w_up', 'ffn2_w_down', 'final_norm_g')
    return (small["loss"], dx0.reshape(x.shape), *[grad_w[n] for n in order], *[delta[n] for n in order],
            *[new_m[n] for n in order], *[new_v[n] for n in order])
```

```python
import functools

import jax
import jax.numpy as jnp
from jax import lax
from jax.experimental import pallas as pl
from jax.experimental.pallas import tpu as pltpu

F32 = jnp.float32
BF16 = jnp.bfloat16

D_MODEL = 1024
D_FF = 2816
N_CHIPS = 4
FF_SH = D_FF // N_CHIPS
N_Q_HEADS = 8
N_KV_HEADS = 2
REP = N_Q_HEADS // N_KV_HEADS
HEAD_DIM = 64
ATTN_W = 512
KV_W = 128
GMLP_W = 512
GMLP_GROUPS = 8
GROUP_DIM = 64
BLK = 128
IN_W = 1792
IN_SH = IN_W // N_CHIPS
OUT_SH = D_MODEL // N_CHIPS
EPS = 1e-6
FFN_RES = 0.5
ATTN_SCALE = HEAD_DIM ** -0.5

ADAM_LR = 0.001
ADAM_B1 = 0.9
ADAM_B2 = 0.999
ADAM_EPS = 1e-08
ADAM_WD = 0.01
ADAM_STEP = 10

V7X_VMEM_LIMIT = 56 * 1024 * 1024
MESH = pl.DeviceIdType.MESH


def _cparams(sem):
    return pltpu.CompilerParams(dimension_semantics=sem, vmem_limit_bytes=V7X_VMEM_LIMIT)


def _dot(a, b):
    return jnp.dot(a, b, preferred_element_type=F32)


def _dot_nt(a, b):
    return lax.dot_general(a, b, (((1,), (1,)), ((), ())), preferred_element_type=F32)


def _dot_tn(a, b):
    return lax.dot_general(a, b, (((0,), (0,)), ((), ())), preferred_element_type=F32)


def _rms(x, g):
    r = lax.rsqrt(jnp.mean(x * x, axis=-1, keepdims=True) + EPS)
    return x * r * g, r


def _rms_bwd(dh, x, r, g):
    gy = dh * g
    dx = r * gy - x * (r * r * r) * jnp.mean(gy * x, axis=-1, keepdims=True)
    dg = jnp.sum(dh * x * r, axis=0, keepdims=True)
    return dx, dg


def _const(shape):
    nd = len(shape)
    return pl.BlockSpec(shape, lambda *_: (0,) * nd)


def _rows(t, w):
    return pl.BlockSpec((t, w), lambda i: (i, 0))


PACK_ROWS = 7 * FF_SH
HALF_ROWS = PACK_ROWS // 2
RS_TILE = HALF_ROWS // 7
MIX_BLOCK = 6
BIG = ("ffn1_w_gate", "ffn1_w_up", "ffn1_w_down", "ffn2_w_gate", "ffn2_w_up", "ffn2_w_down", "w_in", "w_out")
BIG_ROWS = (FF_SH, FF_SH, FF_SH, FF_SH, FF_SH, FF_SH, IN_SH, OUT_SH)
BIG_TRANSPOSED = (True, True, False, True, True, False, True, False)

SMALL = (("ffn1_norm_g", 1024), ("mix_norm_g", 1024), ("b_in", 1792), ("attn_sinks", 8), ("gmlp_ln_g", 512),
         ("gmlp_ln_b", 512), ("gmlp_w_s", 131072), ("gmlp_b_s", 1024), ("attn_out_norm_g", 512),
         ("gmlp_out_norm_g", 512), ("b_out", 1024), ("ffn2_norm_g", 1024), ("final_norm_g", 1024), ("loss", 1))


def _small_rows(n):
    return -(-n // 1024) * 8


SMALL_ROWS = sum(_small_rows(n) for _, n in SMALL)


def _pack_small(parts):
    out = []
    for name, n in SMALL:
        flat = parts[name].reshape(-1).astype(F32)
        rows = _small_rows(n)
        out.append(jnp.pad(flat, (0, rows * 128 - n)).reshape(rows, 128))
    return jnp.concatenate(out, axis=0)


def _unpack_small(packed, shapes):
    res, off = {}, 0
    for name, n in SMALL:
        rows = _small_rows(n)
        res[name] = packed[off:off + rows].reshape(-1)[:n].reshape(shapes[name])
        off += rows
    return res


def _ffn_tile(x, g, wg_ref, wu_ref, wd_ref):
    h, _ = _rms(x, g)
    hb = h.astype(BF16)
    acc = jnp.zeros(x.shape, F32)
    for j in range(N_CHIPS):
        a = _dot_nt(hb, wg_ref[j])
        b = _dot_nt(hb, wu_ref[j])
        f = (a * jax.nn.sigmoid(a) * b).astype(BF16)
        acc = acc + _dot(f, wd_ref[j])
    return x + FFN_RES * acc


def _ffn_weight_specs(k0):
    one = pl.Buffered(1)
    return [pl.BlockSpec((N_CHIPS, FF_SH, D_MODEL), functools.partial(lambda kk, i: (0, kk, 0), k0 + d),
                         pipeline_mode=one) for d in range(3)]


def _ffn_fwd(x, g, pack, k0, *, tile, name):
    s = x.shape[0]

    def body(x_ref, g_ref, wg_ref, wu_ref, wd_ref, o_ref):
        o_ref[...] = _ffn_tile(x_ref[...], g_ref[...], wg_ref, wu_ref, wd_ref)

    return pl.pallas_call(
        body, name=name, grid=(s // tile,),
        in_specs=[_rows(tile, D_MODEL), _const((1, D_MODEL))] + _ffn_weight_specs(k0),
        out_specs=_rows(tile, D_MODEL),
        out_shape=jax.ShapeDtypeStruct(x.shape, F32),
        compiler_params=_cparams(("arbitrary",)),
    )(x, g, pack, pack, pack)


def _ffn_fwd_loss(x, g, pack, k0, gf, tgt, *, tile, name):
    s = x.shape[0]

    def body(x_ref, g_ref, wg_ref, wu_ref, wd_ref, gf_ref, t_ref, dx_ref, loss_ref, dgf_ref):
        @pl.when(pl.program_id(0) == 0)
        def _():
            loss_ref[...] = jnp.zeros_like(loss_ref)
            dgf_ref[...] = jnp.zeros_like(dgf_ref)

        x3 = _ffn_tile(x_ref[...], g_ref[...], wg_ref, wu_ref, wd_ref)
        gf_v = gf_ref[...]
        out, r = _rms(x3, gf_v)
        diff = out - t_ref[...]
        part = jnp.sum(jnp.sum(diff * diff, axis=-1, keepdims=True), axis=0, keepdims=True)
        loss_ref[...] += jnp.broadcast_to(part * (0.5 / D_MODEL), loss_ref.shape)
        dx, dg = _rms_bwd(diff * (1.0 / D_MODEL), x3, r, gf_v)
        dx_ref[...] = dx
        dgf_ref[...] += dg

    return pl.pallas_call(
        body, name=name, grid=(s // tile,),
        in_specs=[_rows(tile, D_MODEL), _const((1, D_MODEL))] + _ffn_weight_specs(k0)
                 + [_const((1, D_MODEL)), _rows(tile, D_MODEL)],
        out_specs=[_rows(tile, D_MODEL), _const((1, 128)), _const((1, D_MODEL))],
        out_shape=[jax.ShapeDtypeStruct(x.shape, F32),
                   jax.ShapeDtypeStruct((1, 128), F32),
                   jax.ShapeDtypeStruct((1, D_MODEL), F32)],
        compiler_params=_cparams(("arbitrary",)),
    )(x, g, pack, pack, pack, gf, tgt)


def _ffn_bwd(x, dy, g, pack, k0, grads, *, tile, name):
    s = x.shape[0]

    def body(x_ref, dy_ref, g_ref, wg_ref, wu_ref, wd_ref, *rest):
        dhp_ref, gw_ref = rest[-2], rest[-1]

        @pl.when(pl.program_id(1) == 0)
        def _():
            gw_ref[...] = jnp.zeros_like(gw_ref)

        h, _ = _rms(x_ref[...], g_ref[...])
        hb = h.astype(BF16)
        dob = (FFN_RES * dy_ref[...]).astype(BF16)
        wg_j, wu_j, wd_j = wg_ref[0], wu_ref[0], wd_ref[0]
        a = _dot_nt(hb, wg_j)
        b = _dot_nt(hb, wu_j)
        sg = jax.nn.sigmoid(a)
        sa = a * sg
        fb = (sa * b).astype(BF16)
        df = _dot_nt(dob, wd_j)
        dbb = (df * sa).astype(BF16)
        dab = (df * b * (sg + sa * (1.0 - sg))).astype(BF16)
        dhp_ref[0] = _dot(dab, wg_j) + _dot(dbb, wu_j)
        gw_ref[0, 0:FF_SH, :] += _dot_tn(dab, hb)
        gw_ref[0, FF_SH:2 * FF_SH, :] += _dot_tn(dbb, hb)
        gw_ref[0, 2 * FF_SH:3 * FF_SH, :] += _dot_tn(fb, dob)

    wspecs = [pl.BlockSpec((1, FF_SH, D_MODEL), functools.partial(lambda kk, j, i: (j, kk, 0), k0 + d))
              for d in range(3)]
    xspec = pl.BlockSpec((tile, D_MODEL), lambda j, i: (i, 0))
    in_specs = [xspec, xspec, pl.BlockSpec((1, D_MODEL), lambda j, i: (0, 0))] + wspecs
    operands = [x, dy, g, pack, pack, pack]
    aliases = {}
    if grads is not None:
        in_specs.append(pl.BlockSpec(memory_space=pl.ANY))
        operands.append(grads)
        aliases = {6: 1}
    k3 = k0 // 3
    return pl.pallas_call(
        body, name=name, grid=(N_CHIPS, s // tile),
        in_specs=in_specs,
        out_specs=[pl.BlockSpec((1, tile, D_MODEL), lambda j, i: (j, i, 0)),
                   pl.BlockSpec((1, 3 * FF_SH, D_MODEL), lambda j, i: (j, k3, 0))],
        out_shape=[jax.ShapeDtypeStruct((N_CHIPS, s, D_MODEL), F32),
                   jax.ShapeDtypeStruct((N_CHIPS, PACK_ROWS, D_MODEL), F32)],
        input_output_aliases=aliases,
        compiler_params=_cparams(("arbitrary", "arbitrary")),
    )(*operands)


def _mix_grads_pack(dw_in_t, dw_out, grads, *, name):
    def body(a_ref, b_ref, g_any, o_ref):
        o_ref[0, 0:IN_SH, :] = a_ref[0]
        o_ref[0, IN_SH:FF_SH, :] = b_ref[0]

    return pl.pallas_call(
        body, name=name, grid=(N_CHIPS,),
        in_specs=[pl.BlockSpec((1, IN_SH, D_MODEL), lambda j: (j, 0, 0)),
                  pl.BlockSpec((1, OUT_SH, D_MODEL), lambda j: (j, 0, 0)),
                  pl.BlockSpec(memory_space=pl.ANY)],
        out_specs=pl.BlockSpec((1, FF_SH, D_MODEL), lambda j: (j, MIX_BLOCK, 0)),
        out_shape=jax.ShapeDtypeStruct((N_CHIPS, PACK_ROWS, D_MODEL), F32),
        input_output_aliases={2: 0},
        compiler_params=_cparams(("arbitrary",)),
    )(dw_in_t.reshape(N_CHIPS, IN_SH, D_MODEL), dw_out.reshape(N_CHIPS, OUT_SH, D_MODEL), grads)


def _norm_bwd(dhp, x, dy, g, *, tile, name):
    s = x.shape[0]

    def body(dhp_ref, x_ref, dy_ref, g_ref, dx_ref, dg_ref):
        @pl.when(pl.program_id(0) == 0)
        def _():
            dg_ref[...] = jnp.zeros_like(dg_ref)

        dh = (dhp_ref[0] + dhp_ref[1]) + (dhp_ref[2] + dhp_ref[3])
        x_v = x_ref[...]
        r = lax.rsqrt(jnp.mean(x_v * x_v, axis=-1, keepdims=True) + EPS)
        dx, dg = _rms_bwd(dh, x_v, r, g_ref[...])
        dx_ref[...] = dy_ref[...] + dx
        dg_ref[...] += dg

    return pl.pallas_call(
        body, name=name, grid=(s // tile,),
        in_specs=[pl.BlockSpec((N_CHIPS, tile, D_MODEL), lambda i: (0, i, 0)),
                  _rows(tile, D_MODEL), _rows(tile, D_MODEL), _const((1, D_MODEL))],
        out_specs=[_rows(tile, D_MODEL), _const((1, D_MODEL))],
        out_shape=[jax.ShapeDtypeStruct(x.shape, F32), jax.ShapeDtypeStruct((1, D_MODEL), F32)],
        compiler_params=_cparams(("arbitrary",)),
    )(dhp, x, dy, g)


def _mix_in_fwd(x, g, w_in_t, b_in, *, tile, name):
    s = x.shape[0]

    def body(x_ref, g_ref, w_ref, b_ref, q_ref, k_ref, v_ref, z_ref):
        h, _ = _rms(x_ref[...], g_ref[...])
        proj = _dot_nt(h.astype(BF16), w_ref[...]) + b_ref[...]
        q_ref[...] = proj[:, :ATTN_W].astype(BF16)
        k_ref[...] = proj[:, ATTN_W:ATTN_W + KV_W].astype(BF16)
        v_ref[...] = proj[:, ATTN_W + KV_W:ATTN_W + 2 * KV_W].astype(BF16)
        z_ref[...] = proj[:, ATTN_W + 2 * KV_W:]

    return pl.pallas_call(
        body, name=name, grid=(s // tile,),
        in_specs=[_rows(tile, D_MODEL), _const((1, D_MODEL)), _const((IN_W, D_MODEL)), _const((1, IN_W))],
        out_specs=[_rows(tile, ATTN_W), _rows(tile, KV_W), _rows(tile, KV_W), _rows(tile, 2 * GMLP_W)],
        out_shape=[jax.ShapeDtypeStruct((s, ATTN_W), BF16), jax.ShapeDtypeStruct((s, KV_W), BF16),
                   jax.ShapeDtypeStruct((s, KV_W), BF16), jax.ShapeDtypeStruct((s, 2 * GMLP_W), F32)],
        compiler_params=_cparams(("arbitrary",)),
    )(x, g, w_in_t, b_in)


def _mix_in_bwd(x, dy, dq, dk, dv, dz, g, w_in_t, *, tile, name):
    s = x.shape[0]

    def body(x_ref, dy_ref, dq_ref, dk_ref, dv_ref, dz_ref, g_ref, w_ref, dx_ref, dw_ref, db_ref, dg_ref):
        @pl.when(pl.program_id(0) == 0)
        def _():
            dw_ref[...] = jnp.zeros_like(dw_ref)
            db_ref[...] = jnp.zeros_like(db_ref)
            dg_ref[...] = jnp.zeros_like(dg_ref)

        dproj = jnp.concatenate([dq_ref[...], dk_ref[...], dv_ref[...], dz_ref[...]], axis=-1)
        db_ref[...] += jnp.sum(dproj, axis=0, keepdims=True)
        dpb = dproj.astype(BF16)
        x_v = x_ref[...]
        g_v = g_ref[...]
        h, r = _rms(x_v, g_v)
        dw_ref[...] += _dot_tn(dpb, h.astype(BF16))
        dh = _dot(dpb, w_ref[...])
        dx, dg = _rms_bwd(dh, x_v, r, g_v)
        dx_ref[...] = dy_ref[...] + dx
        dg_ref[...] += dg

    return pl.pallas_call(
        body, name=name, grid=(s // tile,),
        in_specs=[_rows(tile, D_MODEL), _rows(tile, D_MODEL), _rows(tile, ATTN_W), _rows(tile, KV_W),
                  _rows(tile, KV_W), _rows(tile, 2 * GMLP_W), _const((1, D_MODEL)), _const((IN_W, D_MODEL))],
        out_specs=[_rows(tile, D_MODEL), _const((IN_W, D_MODEL)), _const((1, IN_W)), _const((1, D_MODEL))],
        out_shape=[jax.ShapeDtypeStruct(x.shape, F32), jax.ShapeDtypeStruct((IN_W, D_MODEL), F32),
                   jax.ShapeDtypeStruct((1, IN_W), F32), jax.ShapeDtypeStruct((1, D_MODEL), F32)],
        compiler_params=_cparams(("arbitrary",)),
    )(x, dy, dq, dk, dv, dz, g, w_in_t)


_GELU_C = 0.7978845608028654
_GELU_A = 0.044715


def _gelu(x):
    return 0.5 * x * (1.0 + jnp.tanh(_GELU_C * (x + _GELU_A * (x * x * x))))


def _gelu_grad(x):
    t = jnp.tanh(_GELU_C * (x + _GELU_A * (x * x * x)))
    return 0.5 * (1.0 + t) + 0.5 * x * (1.0 - t * t) * (_GELU_C * (1.0 + 3.0 * _GELU_A * (x * x)))


def _band(ref, i):
    prev = jnp.maximum(i - 1, 0)
    return jnp.concatenate([ref[pl.ds(pl.multiple_of(prev * BLK, BLK), BLK), :],
                            ref[pl.ds(pl.multiple_of(i * BLK, BLK), BLK), :]], axis=0)


def _band_mask(i):
    qpos = lax.broadcasted_iota(jnp.int32, (BLK, 2 * BLK), 0)
    kidx = lax.broadcasted_iota(jnp.int32, (BLK, 2 * BLK), 1)
    rel = qpos - kidx + BLK
    win = jnp.where(rel >= 0, jnp.where(rel < BLK, 1, 0), 0)
    real = jnp.where(kidx >= BLK, 1, jnp.where(i > 0, 1, 0))
    return (win * real) > 0


def _attn_probs(qh, kg, mask, sink):
    sc = _dot_nt(qh, kg) * ATTN_SCALE
    sc = jnp.where(mask, sc, -jnp.inf)
    m = jnp.maximum(jnp.max(sc, axis=-1, keepdims=True), sink)
    p = jnp.exp(sc - m)
    es = jnp.exp(sink - m)
    inv = 1.0 / (jnp.sum(p, axis=-1, keepdims=True) + es)
    return p * inv, es * inv


def _tril_mask():
    t = lax.broadcasted_iota(jnp.int32, (BLK, BLK), 0)
    s_ = lax.broadcasted_iota(jnp.int32, (BLK, BLK), 1)
    return s_ <= t


def _gmlp_fwd_parts(zg, lng, lnb, ws_ref, bs_full):
    z = _gelu(zg)
    u = z[:, :GMLP_W]
    zv = z[:, GMLP_W:]
    mu = jnp.mean(zv, axis=-1, keepdims=True)
    zc = zv - mu
    rstd = lax.rsqrt(jnp.mean(zc * zc, axis=-1, keepdims=True) + EPS)
    xh = zc * rstd
    vvb = (xh * lng + lnb).astype(BF16)
    tril = _tril_mask()
    wms, parts = [], []
    for gi in range(GMLP_GROUPS):
        wm = jnp.where(tril, ws_ref[gi], 0.0).astype(BF16)
        wms.append(wm)
        parts.append(_dot(wm, vvb[:, gi * GROUP_DIM:(gi + 1) * GROUP_DIM]))
    mixed = jnp.concatenate(parts, axis=-1) + bs_full
    return u, xh, rstd, vvb, wms, mixed


def _attn_fwd(q, kb, vb, mask, sink_ref):
    outs = []
    for h in range(N_Q_HEADS):
        gi = h // REP
        pn, _ = _attn_probs(q[:, h * HEAD_DIM:(h + 1) * HEAD_DIM], kb[:, gi * HEAD_DIM:(gi + 1) * HEAD_DIM],
                            mask, sink_ref[h])
        outs.append(_dot(pn.astype(BF16), vb[:, gi * HEAD_DIM:(gi + 1) * HEAD_DIM]))
    return jnp.concatenate(outs, axis=-1)


def _mix_core_fwd(q, k, v, zg, sinks, lng, lnb, w_s, bs_full, gao, ggo, *, name):
    s = q.shape[0]

    def body(sink_ref, q_ref, k_ref, v_ref, z_ref, lng_ref, lnb_ref, ws_ref, bs_ref, gao_ref, ggo_ref, o_ref):
        i = pl.program_id(0)
        mask = _band_mask(i)
        y_attn = _attn_fwd(q_ref[...], _band(k_ref, i), _band(v_ref, i), mask, sink_ref)
        u, _, _, _, _, mixed = _gmlp_fwd_parts(z_ref[...], lng_ref[...], lnb_ref[...], ws_ref, bs_ref[...])
        ya, _ = _rms(y_attn, gao_ref[...])
        yg, _ = _rms(u * mixed, ggo_ref[...])
        o_ref[...] = jnp.concatenate([ya, yg], axis=-1).astype(BF16)

    return pl.pallas_call(
        body, name=name, grid=(s // BLK,),
        in_specs=[pl.BlockSpec(memory_space=pltpu.SMEM),
                  _rows(BLK, ATTN_W), _const((s, KV_W)), _const((s, KV_W)),
                  _rows(BLK, 2 * GMLP_W), _const((1, GMLP_W)), _const((1, GMLP_W)),
                  _const((GMLP_GROUPS, BLK, BLK)), _const((BLK, GMLP_W)), _const((1, ATTN_W)), _const((1, GMLP_W))],
        out_specs=_rows(BLK, D_MODEL),
        out_shape=jax.ShapeDtypeStruct((s, D_MODEL), BF16),
        compiler_params=_cparams(("arbitrary",)),
    )(sinks, q, k, v, zg, lng, lnb, w_s, bs_full, gao, ggo)


def _mix_out_fwd(x1, yb, w_out, b_out, *, tile, name):
    s = x1.shape[0]

    def body(x_ref, y_ref, w_ref, b_ref, o_ref):
        o_ref[...] = x_ref[...] + (_dot(y_ref[...], w_ref[...]) + b_ref[...])

    return pl.pallas_call(
        body, name=name, grid=(s // tile,),
        in_specs=[_rows(tile, D_MODEL), _rows(tile, D_MODEL), _const((D_MODEL, D_MODEL)), _const((1, D_MODEL))],
        out_specs=_rows(tile, D_MODEL),
        out_shape=jax.ShapeDtypeStruct(x1.shape, F32),
        compiler_params=_cparams(("arbitrary",)),
    )(x1, yb, w_out, b_out)


def _norm_bwd_mix_out(dhp, x, dy, g, yb, w_out, *, tile, name):
    s = x.shape[0]

    def body(dhp_ref, x_ref, dy_ref, g_ref, y_ref, w_ref, dx_ref, dg_ref, dyy_ref, dw_ref, db_ref):
        @pl.when(pl.program_id(0) == 0)
        def _():
            dg_ref[...] = jnp.zeros_like(dg_ref)
            dw_ref[...] = jnp.zeros_like(dw_ref)
            db_ref[...] = jnp.zeros_like(db_ref)

        dh = (dhp_ref[0] + dhp_ref[1]) + (dhp_ref[2] + dhp_ref[3])
        x_v = x_ref[...]
        r = lax.rsqrt(jnp.mean(x_v * x_v, axis=-1, keepdims=True) + EPS)
        dxn, dg = _rms_bwd(dh, x_v, r, g_ref[...])
        dx = dy_ref[...] + dxn
        dx_ref[...] = dx
        dg_ref[...] += dg
        dxb = dx.astype(BF16)
        db_ref[...] += jnp.sum(dx, axis=0, keepdims=True)
        dw_ref[...] += _dot_tn(y_ref[...], dxb)
        dyy_ref[...] = _dot_nt(dxb, w_ref[...])

    return pl.pallas_call(
        body, name=name, grid=(s // tile,),
        in_specs=[pl.BlockSpec((N_CHIPS, tile, D_MODEL), lambda i: (0, i, 0)),
                  _rows(tile, D_MODEL), _rows(tile, D_MODEL), _const((1, D_MODEL)), _rows(tile, D_MODEL),
                  _const((D_MODEL, D_MODEL))],
        out_specs=[_rows(tile, D_MODEL), _const((1, D_MODEL)), _rows(tile, D_MODEL), _const((D_MODEL, D_MODEL)),
                   _const((1, D_MODEL))],
        out_shape=[jax.ShapeDtypeStruct(x.shape, F32), jax.ShapeDtypeStruct((1, D_MODEL), F32),
                   jax.ShapeDtypeStruct(x.shape, F32), jax.ShapeDtypeStruct((D_MODEL, D_MODEL), F32),
                   jax.ShapeDtypeStruct((1, D_MODEL), F32)],
        compiler_params=_cparams(("arbitrary",)),
    )(dhp, x, dy, g, yb, w_out)


def _mix_core_bwd(dyy, q, k, v, zg, sinks, lng, lnb, w_s, bs_full, gao, ggo, *, name):
    s = dyy.shape[0]
    nblk = s // BLK

    def body(sink_ref, dyy_ref, q_ref, k_ref, v_ref, z_ref, lng_ref, lnb_ref, ws_ref, bs_ref, gao_ref, ggo_ref,
             dq_ref, dk_ref, dv_ref, dz_ref, dgao_ref, dggo_ref, dlng_ref, dlnb_ref, dws_ref, dms_ref, dsk_ref):
        i = pl.program_id(0)

        @pl.when(i == 0)
        def _():
            for ref in (dk_ref, dv_ref, dgao_ref, dggo_ref, dlng_ref, dlnb_ref, dws_ref, dms_ref, dsk_ref):
                ref[...] = jnp.zeros_like(ref)

        mask = _band_mask(i)
        q_v = q_ref[...]
        kb = _band(k_ref, i)
        vb = _band(v_ref, i)
        lng_v = lng_ref[...]
        gao_v = gao_ref[...]
        ggo_v = ggo_ref[...]
        zg_v = z_ref[...]

        y_attn = _attn_fwd(q_v, kb, vb, mask, sink_ref)
        u, xh, rstd, vvb, wms, mixed = _gmlp_fwd_parts(zg_v, lng_v, lnb_ref[...], ws_ref, bs_ref[...])
        y_gmlp = u * mixed
        ra = lax.rsqrt(jnp.mean(y_attn * y_attn, axis=-1, keepdims=True) + EPS)
        rg = lax.rsqrt(jnp.mean(y_gmlp * y_gmlp, axis=-1, keepdims=True) + EPS)

        dyy = dyy_ref[...]
        d_attn, dgao = _rms_bwd(dyy[:, :ATTN_W], y_attn, ra, gao_v)
        d_gmlp, dggo = _rms_bwd(dyy[:, ATTN_W:], y_gmlp, rg, ggo_v)
        dgao_ref[...] += dgao
        dggo_ref[...] += dggo

        du = d_gmlp * mixed
        dmixed = d_gmlp * u
        dms_ref[...] += dmixed
        dmb = dmixed.astype(BF16)
        dvv_parts = []
        for gi in range(GMLP_GROUPS):
            sl = slice(gi * GROUP_DIM, (gi + 1) * GROUP_DIM)
            dws_ref[gi] += _dot_nt(dmb[:, sl], vvb[:, sl])
            dvv_parts.append(_dot_tn(wms[gi], dmb[:, sl]))
        dvv = jnp.concatenate(dvv_parts, axis=-1)
        dlng_ref[...] += jnp.sum(dvv * xh, axis=0, keepdims=True)
        dlnb_ref[...] += jnp.sum(dvv, axis=0, keepdims=True)
        dxh = dvv * lng_v
        dzv = rstd * (dxh - jnp.mean(dxh, axis=-1, keepdims=True)
                      - xh * jnp.mean(dxh * xh, axis=-1, keepdims=True))
        dz_ref[...] = jnp.concatenate([du, dzv], axis=-1) * _gelu_grad(zg_v)

        dab = d_attn.astype(BF16)
        dq_parts = []
        dk_parts = []
        dv_parts = []
        for gi in range(N_KV_HEADS):
            kg = kb[:, gi * HEAD_DIM:(gi + 1) * HEAD_DIM]
            vg = vb[:, gi * HEAD_DIM:(gi + 1) * HEAD_DIM]
            dkg = jnp.zeros((2 * BLK, HEAD_DIM), F32)
            dvg = jnp.zeros((2 * BLK, HEAD_DIM), F32)
            for rr in range(REP):
                h = gi * REP + rr
                hs = slice(h * HEAD_DIM, (h + 1) * HEAD_DIM)
                qh = q_v[:, hs]
                doh = dab[:, hs]
                pn, psink = _attn_probs(qh, kg, mask, sink_ref[h])
                dp = _dot_nt(doh, vg)
                delta = jnp.sum(pn * dp, axis=-1, keepdims=True)
                dsb = (pn * (dp - delta) * ATTN_SCALE).astype(BF16)
                dsink = jnp.sum(-psink * delta, axis=0, keepdims=True)
                dsk_ref[pl.ds(h, 1), :] += jnp.broadcast_to(dsink, (1, 128))
                dq_parts.append(_dot(dsb, kg))
                dkg = dkg + _dot_tn(dsb, qh)
                dvg = dvg + _dot_tn(pn.astype(BF16), doh)
            dk_parts.append(dkg)
            dv_parts.append(dvg)
        dq_ref[...] = jnp.concatenate(dq_parts, axis=-1)
        dkb = jnp.concatenate(dk_parts, axis=-1)
        dvb = jnp.concatenate(dv_parts, axis=-1)
        prev = pl.ds(pl.multiple_of(jnp.maximum(i - 1, 0) * BLK, BLK), BLK)
        cur = pl.ds(pl.multiple_of(i * BLK, BLK), BLK)
        dk_ref[prev, :] += dkb[:BLK]
        dv_ref[prev, :] += dvb[:BLK]
        dk_ref[cur, :] += dkb[BLK:]
        dv_ref[cur, :] += dvb[BLK:]

        @pl.when(i == nblk - 1)
        def _():
            tril = _tril_mask()
            for gi in range(GMLP_GROUPS):
                dws_ref[gi] = jnp.where(tril, dws_ref[gi], 0.0)

    return pl.pallas_call(
        body, name=name, grid=(nblk,),
        in_specs=[pl.BlockSpec(memory_space=pltpu.SMEM),
                  _rows(BLK, D_MODEL), _rows(BLK, ATTN_W), _const((s, KV_W)), _const((s, KV_W)),
                  _rows(BLK, 2 * GMLP_W), _const((1, GMLP_W)), _const((1, GMLP_W)),
                  _const((GMLP_GROUPS, BLK, BLK)), _const((BLK, GMLP_W)), _const((1, ATTN_W)), _const((1, GMLP_W))],
        out_specs=[_rows(BLK, ATTN_W), _const((s, KV_W)), _const((s, KV_W)), _rows(BLK, 2 * GMLP_W),
                   _const((1, ATTN_W)), _const((1, GMLP_W)),
                   _const((1, GMLP_W)), _const((1, GMLP_W)), _const((GMLP_GROUPS, BLK, BLK)),
                   _const((BLK, GMLP_W)), _const((N_Q_HEADS, 128))],
        out_shape=[jax.ShapeDtypeStruct((s, ATTN_W), F32), jax.ShapeDtypeStruct((s, KV_W), F32),
                   jax.ShapeDtypeStruct((s, KV_W), F32), jax.ShapeDtypeStruct((s, 2 * GMLP_W), F32),
                   jax.ShapeDtypeStruct((1, ATTN_W), F32), jax.ShapeDtypeStruct((1, GMLP_W), F32),
                   jax.ShapeDtypeStruct((1, GMLP_W), F32), jax.ShapeDtypeStruct((1, GMLP_W), F32),
                   jax.ShapeDtypeStruct((GMLP_GROUPS, BLK, BLK), F32), jax.ShapeDtypeStruct((BLK, GMLP_W), F32),
                   jax.ShapeDtypeStruct((N_Q_HEADS, 128), F32)],
        compiler_params=_cparams(("arbitrary",)),
    )(sinks, dyy, q, k, v, zg, lng, lnb, w_s, bs_full, gao, ggo)


def _local_step(x, tgt, p, pack, *, tile=512, bwd_tile=256):
    g = {}
    tile, bwd_tile = min(tile, x.shape[0]), min(bwd_tile, x.shape[0])
    x1 = _ffn_fwd(x, p["ffn1_norm_g"], pack, 0, tile=tile, name="ffn1_fwd")
    q, k, v, zg = _mix_in_fwd(x1, p["mix_norm_g"], p["w_in_t"], p["b_in"], tile=tile, name="mix_in_fwd")
    mix_args = (q, k, v, zg, p["attn_sinks"], p["gmlp_ln_g"], p["gmlp_ln_b"], p["gmlp_w_s"], p["bs_full"],
                p["attn_out_norm_g"], p["gmlp_out_norm_g"])
    yb = _mix_core_fwd(*mix_args, name="mix_core_fwd")
    x2 = _mix_out_fwd(x1, yb, p["w_out"], p["b_out"], tile=tile, name="mix_out_fwd")
    dx3, loss, g["final_norm_g"] = _ffn_fwd_loss(x2, p["ffn2_norm_g"], pack, 3, p["final_norm_g"], tgt,
                                                 tile=tile, name="ffn2_fwd_loss")

    dhp, grads = _ffn_bwd(x2, dx3, p["ffn2_norm_g"], pack, 3, None, tile=bwd_tile, name="ffn2_bwd")
    dx2, g["ffn2_norm_g"], dyy, dw_out, g["b_out"] = _norm_bwd_mix_out(
        dhp, x2, dx3, p["ffn2_norm_g"], yb, p["w_out"], tile=bwd_tile, name="ffn2_norm_bwd")

    (dq, dk, dv, dz, g["attn_out_norm_g"], g["gmlp_out_norm_g"], g["gmlp_ln_g"],
     g["gmlp_ln_b"], g["gmlp_w_s"], dmix_sum, dsinks) = _mix_core_bwd(dyy, *mix_args, name="mix_core_bwd")
    g["gmlp_b_s"] = dmix_sum
    g["attn_sinks"] = dsinks
    dx1, dw_in_t, g["b_in"], g["mix_norm_g"] = _mix_in_bwd(
        x1, dx2, dq, dk, dv, dz, p["mix_norm_g"], p["w_in_t"], tile=tile, name="mix_in_bwd")
    grads = _mix_grads_pack(dw_in_t, dw_out, grads, name="mix_grads_pack")

    dhp1, grads = _ffn_bwd(x, dx1, p["ffn1_norm_g"], pack, 0, grads, tile=bwd_tile, name="ffn1_bwd")
    dx0, g["ffn1_norm_g"] = _norm_bwd(dhp1, x, dx1, p["ffn1_norm_g"], tile=bwd_tile, name="ffn1_norm_bwd")
    return loss, dx0, grads, g


def _pack_cast(place, parts, *, name):
    def body(place_ref, *refs):
        o_ref = refs[-1]
        off = 0
        for ref, rows in zip(refs[:-1], BIG_ROWS):
            o_ref[0, off:off + rows, :] = ref[...].astype(BF16)
            off += rows

    one = pl.Buffered(1)
    grid_spec = pltpu.PrefetchScalarGridSpec(
        num_scalar_prefetch=1, grid=(1,),
        in_specs=[pl.BlockSpec((rows, D_MODEL), lambda i, pr: (0, 0), pipeline_mode=one) for rows in BIG_ROWS],
        out_specs=pl.BlockSpec((1, PACK_ROWS, D_MODEL), lambda i, pr: (pr[0], 0, 0), pipeline_mode=one))
    return pl.pallas_call(
        body, name=name, grid_spec=grid_spec,
        out_shape=jax.ShapeDtypeStruct((N_CHIPS, PACK_ROWS, D_MODEL), BF16),
        compiler_params=_cparams(("arbitrary",)),
    )(place, *parts)


def _mesh_place():
    x, y, c = lax.axis_index("x"), lax.axis_index("y"), lax.axis_index("c")
    others = [(1 - x, y), (x, 1 - y), (1 - x, 1 - y)]
    return x, y, c, others


def _half(c):
    return pl.ds(pl.multiple_of(c * HALF_ROWS, 16), HALF_ROWS)


def _all_gather_pack(pack, *, name):
    def body(p_ref, o_ref, send_sems, recv_sems):
        x, y, c, others = _mesh_place()
        me = 2 * x + y
        sibling = (x, y, 1 - c)
        mine, theirs = _half(c), _half(1 - c)

        def copy(k, src, dst, to):
            return pltpu.make_async_remote_copy(src_ref=src, dst_ref=dst, send_sem=send_sems.at[k],
                                                recv_sem=recv_sems.at[k], device_id=to, device_id_type=MESH)

        first = [copy(j, o_ref.at[me, mine], o_ref.at[me, mine], (px, py, c)) for j, (px, py) in enumerate(others)]
        for cp in first:
            cp.start()
        passed = []
        for j, (px, py) in enumerate(others):
            slab = o_ref.at[2 * px + py, mine]
            copy(j, slab, slab, (px, py, c)).wait_recv()
            fwd = copy(3 + j, slab, slab, sibling)
            fwd.start()
            passed.append(fwd)
        for j, (px, py) in enumerate(others):
            slab = o_ref.at[2 * px + py, theirs]
            copy(3 + j, slab, slab, sibling).wait_recv()
        for cp in first + passed:
            cp.wait_send()

    return pl.pallas_call(
        body, name=name,
        in_specs=[pl.BlockSpec(memory_space=pl.ANY)],
        out_specs=pl.BlockSpec(memory_space=pl.ANY),
        out_shape=jax.ShapeDtypeStruct((N_CHIPS, PACK_ROWS, D_MODEL), BF16),
        input_output_aliases={0: 0},
        scratch_shapes=[pltpu.SemaphoreType.DMA((6,)), pltpu.SemaphoreType.DMA((6,))],
    )(pack)


def _rs_sibling(grads, *, name):
    def body(g_ref, o_ref, send_sem, recv_sem):
        x, y, c, _ = _mesh_place()
        cp = pltpu.make_async_remote_copy(src_ref=g_ref.at[:, _half(1 - c), :], dst_ref=o_ref, send_sem=send_sem,
                                          recv_sem=recv_sem, device_id=(x, y, 1 - c), device_id_type=MESH)
        cp.start()
        cp.wait()

    return pl.pallas_call(
        body, name=name,
        in_specs=[pl.BlockSpec(memory_space=pl.ANY)],
        out_specs=pl.BlockSpec(memory_space=pl.ANY),
        out_shape=jax.ShapeDtypeStruct((N_CHIPS, HALF_ROWS, D_MODEL), F32),
        scratch_shapes=[pltpu.SemaphoreType.DMA, pltpu.SemaphoreType.DMA],
    )(grads)


def _rs_add_halves(place, grads, recv, *, name):
    nt = HALF_ROWS // RS_TILE

    def body(place_ref, a_ref, b_ref, o_ref):
        o_ref[...] = (a_ref[...] + b_ref[...]).astype(BF16)

    grid_spec = pltpu.PrefetchScalarGridSpec(
        num_scalar_prefetch=1, grid=(N_CHIPS, nt),
        in_specs=[pl.BlockSpec((1, RS_TILE, D_MODEL), lambda s, i, pr: (s, pr[1] * nt + i, 0)),
                  pl.BlockSpec((1, RS_TILE, D_MODEL), lambda s, i, pr: (s, i, 0))],
        out_specs=pl.BlockSpec((1, RS_TILE, D_MODEL), lambda s, i, pr: (s, i, 0)))
    return pl.pallas_call(
        body, name=name, grid_spec=grid_spec,
        out_shape=jax.ShapeDtypeStruct((N_CHIPS, HALF_ROWS, D_MODEL), BF16),
        compiler_params=_cparams(("arbitrary", "arbitrary")),
    )(place, grads, recv)


def _rs_chips(part, *, name):
    def body(p_ref, o_ref, send_sems, recv_sems):
        x, y, c, others = _mesh_place()
        me = 2 * x + y
        sends = [pltpu.make_async_remote_copy(
            src_ref=p_ref.at[2 * px + py], dst_ref=o_ref.at[me], send_sem=send_sems.at[j], recv_sem=recv_sems.at[j],
            device_id=(px, py, c), device_id_type=MESH) for j, (px, py) in enumerate(others)]
        for cp in sends:
            cp.start()
        for j, (px, py) in enumerate(others):
            slab = o_ref.at[2 * px + py]
            pltpu.make_async_remote_copy(src_ref=slab, dst_ref=slab, send_sem=send_sems.at[j],
                                         recv_sem=recv_sems.at[j], device_id=(px, py, c),
                                         device_id_type=MESH).wait_recv()
        for cp in sends:
            cp.wait_send()

    return pl.pallas_call(
        body, name=name,
        in_specs=[pl.BlockSpec(memory_space=pl.ANY)],
        out_specs=pl.BlockSpec(memory_space=pl.ANY),
        out_shape=jax.ShapeDtypeStruct(part.shape, part.dtype),
        scratch_shapes=[pltpu.SemaphoreType.DMA((3,)), pltpu.SemaphoreType.DMA((3,))],
    )(part)


def _rs_add_chips(place, part, recv, *, name):
    nt = HALF_ROWS // RS_TILE

    def body(place_ref, own_ref, r1_ref, r2_ref, r3_ref, o_ref):
        o_ref[...] = ((own_ref[0].astype(F32) + r1_ref[0].astype(F32))
                      + (r2_ref[0].astype(F32) + r3_ref[0].astype(F32)))

    def slab(d):
        return pl.BlockSpec((1, RS_TILE, D_MODEL), lambda i, pr: ((pr[0] + d) % N_CHIPS, i, 0))

    grid_spec = pltpu.PrefetchScalarGridSpec(
        num_scalar_prefetch=1, grid=(nt,),
        in_specs=[slab(0), slab(1), slab(2), slab(3)],
        out_specs=pl.BlockSpec((RS_TILE, D_MODEL), lambda i, pr: (pr[1] * nt + i, 0)))
    return pl.pallas_call(
        body, name=name, grid_spec=grid_spec,
        out_shape=jax.ShapeDtypeStruct((PACK_ROWS, D_MODEL), F32),
        compiler_params=_cparams(("arbitrary",)),
    )(place, part, recv, recv, recv)


def _rs_share(shard, *, name):
    def body(s_ref, o_ref, send_sem, recv_sem):
        x, y, c, _ = _mesh_place()
        mine = o_ref.at[_half(c)]
        cp = pltpu.make_async_remote_copy(src_ref=mine, dst_ref=mine, send_sem=send_sem, recv_sem=recv_sem,
                                          device_id=(x, y, 1 - c), device_id_type=MESH)
        cp.start()
        theirs = o_ref.at[_half(1 - c)]
        pltpu.make_async_remote_copy(src_ref=theirs, dst_ref=theirs, send_sem=send_sem, recv_sem=recv_sem,
                                     device_id=(x, y, 1 - c), device_id_type=MESH).wait_recv()
        cp.wait_send()

    return pl.pallas_call(
        body, name=name,
        in_specs=[pl.BlockSpec(memory_space=pl.ANY)],
        out_specs=pl.BlockSpec(memory_space=pl.ANY),
        out_shape=jax.ShapeDtypeStruct((PACK_ROWS, D_MODEL), F32),
        input_output_aliases={0: 0},
        scratch_shapes=[pltpu.SemaphoreType.DMA, pltpu.SemaphoreType.DMA],
    )(shard)


def _small_all_reduce(packed, *, name):
    rows = packed.shape[0]

    def body(p_ref, o_ref, sib_ref, slots_ref, send_sems, recv_sems):
        x, y, c, others = _mesh_place()
        me = 2 * x + y
        sib = pltpu.make_async_remote_copy(src_ref=p_ref, dst_ref=sib_ref, send_sem=send_sems.at[0],
                                           recv_sem=recv_sems.at[0], device_id=(x, y, 1 - c), device_id_type=MESH)
        sib.start()
        sib.wait()
        slots_ref[me] = p_ref[...] + sib_ref[...]
        sends = [pltpu.make_async_remote_copy(
            src_ref=slots_ref.at[me], dst_ref=slots_ref.at[me], send_sem=send_sems.at[1 + j],
            recv_sem=recv_sems.at[1 + j], device_id=(px, py, c), device_id_type=MESH)
            for j, (px, py) in enumerate(others)]
        for cp in sends:
            cp.start()
        for j, (px, py) in enumerate(others):
            slab = slots_ref.at[2 * px + py]
            pltpu.make_async_remote_copy(src_ref=slab, dst_ref=slab, send_sem=send_sems.at[1 + j],
                                         recv_sem=recv_sems.at[1 + j], device_id=(px, py, c),
                                         device_id_type=MESH).wait_recv()
        for cp in sends:
            cp.wait_send()
        o_ref[...] = (slots_ref[0] + slots_ref[1]) + (slots_ref[2] + slots_ref[3])

    vm = pl.BlockSpec(memory_space=pltpu.VMEM)
    return pl.pallas_call(
        body, name=name, in_specs=[vm], out_specs=vm,
        out_shape=jax.ShapeDtypeStruct((rows, 128), F32),
        scratch_shapes=[pltpu.VMEM((rows, 128), F32), pltpu.VMEM((N_CHIPS, rows, 128), F32),
                        pltpu.SemaphoreType.DMA((4,)), pltpu.SemaphoreType.DMA((4,))],
    )(packed)


def _adamw(w, g, m, v, *, g_row0, tile, name):
    rows, cols = w.shape
    assert g_row0 % tile == 0 and rows % tile == 0

    def body(w_ref, g_ref, m_ref, v_ref, go_ref, d_ref, nm_ref, nv_ref):
        g_v = g_ref[...]
        m_n = ADAM_B1 * m_ref[...] + (1.0 - ADAM_B1) * g_v
        v_n = ADAM_B2 * v_ref[...] + (1.0 - ADAM_B2) * (g_v * g_v)
        m_hat = m_n / (1.0 - ADAM_B1 ** ADAM_STEP)
        v_hat = v_n / (1.0 - ADAM_B2 ** ADAM_STEP)
        d_ref[...] = -ADAM_LR * (m_hat / (jnp.sqrt(v_hat) + ADAM_EPS) + ADAM_WD * w_ref[...])
        go_ref[...] = g_v
        nm_ref[...] = m_n
        nv_ref[...] = v_n

    spec = pl.BlockSpec((tile, cols), lambda i: (i, 0))
    gspec = pl.BlockSpec((tile, cols), lambda i: (g_row0 // tile + i, 0))
    shape = jax.ShapeDtypeStruct((rows, cols), F32)
    return pl.pallas_call(
        body, name=name, grid=(rows // tile,),
        in_specs=[spec, gspec, spec, spec], out_specs=[spec] * 4, out_shape=[shape] * 4,
        compiler_params=_cparams(("arbitrary",)),
    )(w, g, m, v)


def kernel(x, ffn1_norm_g, ffn1_w_gate, ffn1_w_up, ffn1_w_down, mix_norm_g, w_in, b_in, attn_sinks, gmlp_ln_g, gmlp_ln_b, gmlp_w_s, gmlp_b_s, attn_out_norm_g, gmlp_out_norm_g, w_out, b_out, ffn2_norm_g, ffn2_w_gate, ffn2_w_up, ffn2_w_down, final_norm_g, loss_target, m_ffn1_norm_g, m_ffn1_w_gate, m_ffn1_w_up, m_ffn1_w_down, m_mix_norm_g, m_w_in, m_b_in, m_attn_sinks, m_gmlp_ln_g, m_gmlp_ln_b, m_gmlp_w_s, m_gmlp_b_s, m_attn_out_norm_g, m_gmlp_out_norm_g, m_w_out, m_b_out, m_ffn2_norm_g, m_ffn2_w_gate, m_ffn2_w_up, m_ffn2_w_down, m_final_norm_g, v_ffn1_norm_g, v_ffn1_w_gate, v_ffn1_w_up, v_ffn1_w_down, v_mix_norm_g, v_w_in, v_b_in, v_attn_sinks, v_gmlp_ln_g, v_gmlp_ln_b, v_gmlp_w_s, v_gmlp_b_s, v_attn_out_norm_g, v_gmlp_out_norm_g, v_w_out, v_b_out, v_ffn2_norm_g, v_ffn2_w_gate, v_ffn2_w_up, v_ffn2_w_down, v_final_norm_g):
    f_args = dict(locals())
    weights = {n: f_args[n] for n in [nm for nm, _ in SMALL if nm != "loss"] + list(BIG)}
    shapes = {n: weights[n].shape for n in weights}
    shapes["loss"] = ()
    place = jnp.stack([2 * lax.axis_index("x") + lax.axis_index("y"), lax.axis_index("c")]).astype(jnp.int32)

    def with_cols(name, a):
        a2 = a.reshape(a.shape[-2], a.shape[-1])
        return a2.T if BIG_TRANSPOSED[BIG.index(name)] else a2

    def natural(name, a2):
        return (a2.T if BIG_TRANSPOSED[BIG.index(name)] else a2).reshape(shapes[name])

    pack = _all_gather_pack(_pack_cast(place, [with_cols(n, weights[n]) for n in BIG], name="pack_cast"),
                            name="ag_weights")
    mix_rows = pack[:, MIX_BLOCK * FF_SH:, :]
    p = {n: weights[n].reshape(1, -1) for n in ("ffn1_norm_g", "mix_norm_g", "b_in", "gmlp_ln_g", "gmlp_ln_b",
                                                "attn_out_norm_g", "gmlp_out_norm_g", "b_out", "ffn2_norm_g",
                                                "final_norm_g")}
    p["w_in_t"] = mix_rows[:, :IN_SH, :].reshape(IN_W, D_MODEL)
    p["w_out"] = mix_rows[:, IN_SH:, :].reshape(D_MODEL, D_MODEL)
    p["attn_sinks"] = attn_sinks.reshape(N_Q_HEADS)
    p["gmlp_w_s"] = gmlp_w_s.reshape(GMLP_GROUPS, BLK, BLK)
    p["bs_full"] = jnp.broadcast_to(gmlp_b_s.reshape(GMLP_GROUPS, BLK).T[:, :, None],
                                    (BLK, GMLP_GROUPS, GROUP_DIM)).reshape(BLK, GMLP_W)

    loss_part, dx0, grads, gs = _local_step(x[0], loss_target[0], p, pack)

    part = _rs_add_halves(place, grads, _rs_sibling(grads, name="rs_sibling"), name="rs_add_halves")
    shard = _rs_share(_rs_add_chips(place, part, _rs_chips(part, name="rs_chips"), name="rs_add_chips"),
                      name="rs_share")
    gs["gmlp_b_s"] = jnp.sum(gs["gmlp_b_s"].reshape(BLK, GMLP_GROUPS, GROUP_DIM), axis=-1).T
    gs["attn_sinks"] = gs["attn_sinks"][:, 0]
    gs["loss"] = loss_part[0, 0]
    small_sum = _small_all_reduce(_pack_small(gs), name="small_all_reduce")

    grad_w, delta, new_m, new_v = {}, {}, {}, {}
    off = 0
    for n, rows in zip(BIG, BIG_ROWS):
        res = _adamw(with_cols(n, weights[n]), shard, with_cols(n, f_args["m_" + n]), with_cols(n, f_args["v_" + n]),
                     g_row0=off, tile=FF_SH // 2 if rows == FF_SH else 64, name="adamw_" + n)
        grad_w[n], delta[n], new_m[n], new_v[n] = [natural(n, a) for a in res]
        off += rows
    sm = {k: {n: f_args[k + n] for n, _ in SMALL if n != "loss"} for k in ("", "m_", "v_")}
    for k in sm:
        sm[k]["loss"] = jnp.zeros((), F32)
    res = _adamw(_pack_small(sm[""]), small_sum, _pack_small(sm["m_"]), _pack_small(sm["v_"]),
                 g_row0=0, tile=SMALL_ROWS, name="adamw_small")
    small = _unpack_small(res[0], shapes)
    for dst, packed in ((grad_w, res[0]), (delta, res[1]), (new_m, res[2]), (new_v, res[3])):
        dst.update({n: a for n, a in _unpack_small(packed, shapes).items() if n != "loss"})

    order = ('ffn1_norm_g', 'ffn1_w_gate', 'ffn1_w_up', 'ffn1_w_down', 'mix_norm_g', 'w_in', 'b_in', 'attn_sinks',
             'gmlp_ln_g', 'gmlp_ln_b', 'gmlp_w_s', 'gmlp_b_s', 'attn_out_norm_g', 'gmlp_out_norm_g', 'w_out', 'b_out',
             'ffn2_norm_g', 'ffn2_w_gate', 'ffn2_w_up', 'ffn2_w_down', 'final_norm_g')
    return (small["loss"], dx0.reshape(x.shape), *[grad_w[n] for n in order], *[delta[n] for n in order],
            *[new_m[n] for n in order], *[new_v[n] for n in order])
```

```python
import functools

import jax
import jax.numpy as jnp
from jax import lax
from jax.experimental import pallas as pl
from jax.experimental.pallas import tpu as pltpu

F32 = jnp.float32
BF16 = jnp.bfloat16

D_MODEL = 1024
D_FF = 2816
N_CHIPS = 4
FF_SH = D_FF // N_CHIPS
N_Q_HEADS = 8
N_KV_HEADS = 2
REP = N_Q_HEADS // N_KV_HEADS
HEAD_DIM = 64
ATTN_W = 512
KV_W = 128
GMLP_W = 512
GMLP_GROUPS = 8
GROUP_DIM = 64
BLK = 128
IN_W = 1792
IN_SH = IN_W // N_CHIPS
OUT_SH = D_MODEL // N_CHIPS
EPS = 1e-6
FFN_RES = 0.5
ATTN_SCALE = HEAD_DIM ** -0.5

ADAM_LR = 0.001
ADAM_B1 = 0.9
ADAM_B2 = 0.999
ADAM_EPS = 1e-08
ADAM_WD = 0.01
ADAM_STEP = 10

V7X_VMEM_LIMIT = 56 * 1024 * 1024
MESH = pl.DeviceIdType.MESH


def _cparams(sem):
    return pltpu.CompilerParams(dimension_semantics=sem, vmem_limit_bytes=V7X_VMEM_LIMIT)


def _dot(a, b):
    return jnp.dot(a, b, preferred_element_type=F32)


def _dot_nt(a, b):
    return lax.dot_general(a, b, (((1,), (1,)), ((), ())), preferred_element_type=F32)


def _dot_tn(a, b):
    return lax.dot_general(a, b, (((0,), (0,)), ((), ())), preferred_element_type=F32)


def _rms(x, g):
    r = lax.rsqrt(jnp.mean(x * x, axis=-1, keepdims=True) + EPS)
    return x * r * g, r


def _rms_bwd(dh, x, r, g):
    gy = dh * g
    dx = r * gy - x * (r * r * r) * jnp.mean(gy * x, axis=-1, keepdims=True)
    dg = jnp.sum(dh * x * r, axis=0, keepdims=True)
    return dx, dg


def _const(shape):
    nd = len(shape)
    return pl.BlockSpec(shape, lambda *_: (0,) * nd)


def _rows(t, w):
    return pl.BlockSpec((t, w), lambda i: (i, 0))


PACK_ROWS = 7 * FF_SH
HALF_ROWS = PACK_ROWS // 2
FFN_HALF = 3 * FF_SH // 2
MIX_HALF = FF_SH // 2
MIX_BLOCK = 6
BIG = ("ffn1_w_gate", "ffn1_w_up", "ffn1_w_down", "ffn2_w_gate", "ffn2_w_up", "ffn2_w_down", "w_in", "w_out")
BIG_ROWS = (FF_SH, FF_SH, FF_SH, FF_SH, FF_SH, FF_SH, IN_SH, OUT_SH)
BIG_TRANSPOSED = (True, True, False, True, True, False, True, False)

SMALL = (("ffn1_norm_g", 1024), ("mix_norm_g", 1024), ("b_in", 1792), ("attn_sinks", 8), ("gmlp_ln_g", 512),
         ("gmlp_ln_b", 512), ("gmlp_w_s", 131072), ("gmlp_b_s", 1024), ("attn_out_norm_g", 512),
         ("gmlp_out_norm_g", 512), ("b_out", 1024), ("ffn2_norm_g", 1024), ("final_norm_g", 1024), ("loss", 1))


def _small_rows(n):
    return -(-n // 1024) * 8


SMALL_ROWS = sum(_small_rows(n) for _, n in SMALL)


def _pack_small(parts):
    out = []
    for name, n in SMALL:
        flat = parts[name].reshape(-1).astype(F32)
        rows = _small_rows(n)
        out.append(jnp.pad(flat, (0, rows * 128 - n)).reshape(rows, 128))
    return jnp.concatenate(out, axis=0)


def _unpack_small(packed, shapes):
    res, off = {}, 0
    for name, n in SMALL:
        rows = _small_rows(n)
        res[name] = packed[off:off + rows].reshape(-1)[:n].reshape(shapes[name])
        off += rows
    return res


def _ffn_tile(x, g, wg_ref, wu_ref, wd_ref):
    h, _ = _rms(x, g)
    hb = h.astype(BF16)
    acc = jnp.zeros(x.shape, F32)
    for j in range(N_CHIPS):
        a = _dot_nt(hb, wg_ref[j])
        b = _dot_nt(hb, wu_ref[j])
        f = (a * jax.nn.sigmoid(a) * b).astype(BF16)
        acc = acc + _dot(f, wd_ref[j])
    return x + FFN_RES * acc


def _ffn_weight_specs(k0):
    one = pl.Buffered(1)
    return [pl.BlockSpec((N_CHIPS, FF_SH, D_MODEL), functools.partial(lambda kk, i: (0, kk, 0), k0 + d),
                         pipeline_mode=one) for d in range(3)]


def _ffn_fwd(x, g, pack, k0, *, tile, name):
    s = x.shape[0]

    def body(x_ref, g_ref, wg_ref, wu_ref, wd_ref, o_ref):
        o_ref[...] = _ffn_tile(x_ref[...], g_ref[...], wg_ref, wu_ref, wd_ref)

    return pl.pallas_call(
        body, name=name, grid=(s // tile,),
        in_specs=[_rows(tile, D_MODEL), _const((1, D_MODEL))] + _ffn_weight_specs(k0),
        out_specs=_rows(tile, D_MODEL),
        out_shape=jax.ShapeDtypeStruct(x.shape, F32),
        compiler_params=_cparams(("arbitrary",)),
    )(x, g, pack, pack, pack)


def _ffn_fwd_loss(x, g, pack, k0, gf, tgt, *, tile, name):
    s = x.shape[0]

    def body(x_ref, g_ref, wg_ref, wu_ref, wd_ref, gf_ref, t_ref, dx_ref, loss_ref, dgf_ref):
        @pl.when(pl.program_id(0) == 0)
        def _():
            loss_ref[...] = jnp.zeros_like(loss_ref)
            dgf_ref[...] = jnp.zeros_like(dgf_ref)

        x3 = _ffn_tile(x_ref[...], g_ref[...], wg_ref, wu_ref, wd_ref)
        gf_v = gf_ref[...]
        out, r = _rms(x3, gf_v)
        diff = out - t_ref[...]
        part = jnp.sum(jnp.sum(diff * diff, axis=-1, keepdims=True), axis=0, keepdims=True)
        loss_ref[...] += jnp.broadcast_to(part * (0.5 / D_MODEL), loss_ref.shape)
        dx, dg = _rms_bwd(diff * (1.0 / D_MODEL), x3, r, gf_v)
        dx_ref[...] = dx
        dgf_ref[...] += dg

    return pl.pallas_call(
        body, name=name, grid=(s // tile,),
        in_specs=[_rows(tile, D_MODEL), _const((1, D_MODEL))] + _ffn_weight_specs(k0)
                 + [_const((1, D_MODEL)), _rows(tile, D_MODEL)],
        out_specs=[_rows(tile, D_MODEL), _const((1, 128)), _const((1, D_MODEL))],
        out_shape=[jax.ShapeDtypeStruct(x.shape, F32),
                   jax.ShapeDtypeStruct((1, 128), F32),
                   jax.ShapeDtypeStruct((1, D_MODEL), F32)],
        compiler_params=_cparams(("arbitrary",)),
    )(x, g, pack, pack, pack, gf, tgt)


def _ffn_bwd(place, x, dy, g, pack, k0, land, mix_grads, *, tile, name):
    s = x.shape[0]
    nt = s // tile
    land_rows = pl.ds((k0 // 3) * FFN_HALF, FFN_HALF)
    mix_rows = pl.ds(2 * FFN_HALF, MIX_HALF)
    with_mix = mix_grads is not None
    with_land = land is not None
    n_others = 2 * N_CHIPS - 1

    def body(place_ref, x_ref, dy_ref, g_ref, wg_ref, wu_ref, wd_ref, *rest):
        rest = list(rest)
        mix_ref = rest.pop(0) if with_mix else None
        if with_land:
            rest.pop(0)
        dhp_ref, land_ref, acc_ref, stage_ref, send_sems, recv_sem, local_sem = rest[:7]
        t, i = pl.program_id(0), pl.program_id(1)
        xi, yi, c = lax.axis_index("x"), lax.axis_index("y"), lax.axis_index("c")
        dev = 4 * xi + 2 * yi + c
        tt = (t + 1) % N_CHIPS
        tx, ty = jnp.bitwise_xor(xi, tt // 2), jnp.bitwise_xor(yi, tt % 2)

        def remote(src, dst, ssem, rsem, to):
            return pltpu.make_async_remote_copy(src_ref=src, dst_ref=dst, send_sem=ssem, recv_sem=rsem,
                                                device_id=to, device_id_type=MESH)

        def stage_half(h):
            return stage_ref.at[pl.ds(pl.multiple_of(h * FFN_HALF, 16), FFN_HALF)]

        if with_mix:
            mix_send, mix_recv, mix_local = rest[7:10]

            @pl.when(jnp.logical_and(t == 0, i == 0))
            def _():
                for chip in range(N_CHIPS):
                    for h in range(2):
                        src = mix_ref.at[chip, pl.ds(h * MIX_HALF, MIX_HALF)]
                        dst = land_ref.at[dev, mix_rows]
                        mine = jnp.logical_and(2 * xi + yi == chip, c == h)

                        @pl.when(mine)
                        def _():
                            pltpu.make_async_copy(src, dst, mix_local).start()

                        @pl.when(jnp.logical_not(mine))
                        def _():
                            remote(src, dst, mix_send, mix_recv, (chip // 2, chip % 2, h)).start()

        @pl.when(i == 0)
        def _():
            acc_ref[...] = jnp.zeros_like(acc_ref)

        h_, _ = _rms(x_ref[...], g_ref[...])
        hb = h_.astype(BF16)
        dob = (FFN_RES * dy_ref[...]).astype(BF16)
        wg_j, wu_j, wd_j = wg_ref[0], wu_ref[0], wd_ref[0]
        a = _dot_nt(hb, wg_j)
        b = _dot_nt(hb, wu_j)
        sg = jax.nn.sigmoid(a)
        sa = a * sg
        fb = (sa * b).astype(BF16)
        df = _dot_nt(dob, wd_j)
        dbb = (df * sa).astype(BF16)
        dab = (df * b * (sg + sa * (1.0 - sg))).astype(BF16)
        dhp_ref[0] = _dot(dab, wg_j) + _dot(dbb, wu_j)
        acc_ref[0:FF_SH, :] += _dot_tn(dab, hb)
        acc_ref[FF_SH:2 * FF_SH, :] += _dot_tn(dbb, hb)
        acc_ref[2 * FF_SH:3 * FF_SH, :] += _dot_tn(fb, dob)

        @pl.when(i == nt - 1)
        def _():
            dst = land_ref.at[dev, land_rows]

            @pl.when(t > 0)
            def _():
                for h in range(2):
                    remote(stage_half(h), dst, send_sems.at[h], recv_sem, (tx, ty, h)).wait_send()

            def cast_rows(r, carry):
                rows = pl.ds(pl.multiple_of(r * MIX_HALF, 16), MIX_HALF)
                stage_ref[rows, :] = acc_ref[rows, :].astype(BF16)
                return carry

            lax.fori_loop(0, 3 * FF_SH // MIX_HALF, cast_rows, 0)

            @pl.when(t < N_CHIPS - 1)
            def _():
                for h in range(2):
                    remote(stage_half(h), dst, send_sems.at[h], recv_sem, (tx, ty, h)).start()

            @pl.when(t == N_CHIPS - 1)
            def _():
                own = pltpu.make_async_copy(stage_half(c), dst, local_sem)
                own.start()
                sib = remote(stage_half(1 - c), dst, send_sems.at[0], recv_sem, (xi, yi, 1 - c))
                sib.start()
                sib.wait_send()
                own.wait()
                arrivals = land_ref.at[pl.ds(0, n_others), land_rows]
                remote(arrivals, arrivals, send_sems.at[0], recv_sem, (xi, yi, 1 - c)).wait_recv()
                if with_mix:
                    seven = land_ref.at[pl.ds(0, n_others), mix_rows]
                    both = remote(seven, seven, mix_send, mix_recv, (xi, yi, 1 - c))
                    both.wait_send()
                    both.wait_recv()
                    pltpu.make_async_copy(mix_ref.at[0, pl.ds(0, MIX_HALF)], land_ref.at[dev, mix_rows],
                                          mix_local).wait()

    def wspec(kk):
        return pl.BlockSpec((1, FF_SH, D_MODEL),
                            lambda t, i, pr: (jnp.bitwise_xor(pr[0], (t + 1) % N_CHIPS), kk, 0))

    xspec = pl.BlockSpec((tile, D_MODEL), lambda t, i, pr: (i, 0))
    hbm = pl.BlockSpec(memory_space=pl.ANY)
    in_specs = [xspec, xspec, pl.BlockSpec((1, D_MODEL), lambda t, i, pr: (0, 0)), wspec(k0), wspec(k0 + 1),
                wspec(k0 + 2)]
    operands = [place, x, dy, g, pack, pack, pack]
    scratch = [pltpu.VMEM((3 * FF_SH, D_MODEL), F32), pltpu.VMEM((3 * FF_SH, D_MODEL), BF16),
               pltpu.SemaphoreType.DMA((2,)), pltpu.SemaphoreType.DMA, pltpu.SemaphoreType.DMA]
    if with_mix:
        in_specs.append(hbm)
        operands.append(mix_grads)
        scratch += [pltpu.SemaphoreType.DMA, pltpu.SemaphoreType.DMA, pltpu.SemaphoreType.DMA]
    aliases = {}
    if with_land:
        in_specs.append(hbm)
        operands.append(land)
        aliases = {len(operands) - 1: 1}
    grid_spec = pltpu.PrefetchScalarGridSpec(
        num_scalar_prefetch=1, grid=(N_CHIPS, nt), in_specs=in_specs,
        out_specs=[pl.BlockSpec((1, tile, D_MODEL), lambda t, i, pr: (t, i, 0)), hbm],
        scratch_shapes=scratch)
    return pl.pallas_call(
        body, name=name, grid_spec=grid_spec,
        out_shape=[jax.ShapeDtypeStruct((N_CHIPS, s, D_MODEL), F32),
                   jax.ShapeDtypeStruct((2 * N_CHIPS, HALF_ROWS, D_MODEL), BF16)],
        input_output_aliases=aliases,
        compiler_params=_cparams(("arbitrary", "arbitrary")),
    )(*operands)


def _mix_grads_pack(dw_in_t, dw_out, *, name):
    def body(a_ref, b_ref, o_ref):
        o_ref[0, 0:IN_SH, :] = a_ref[0].astype(BF16)
        o_ref[0, IN_SH:FF_SH, :] = b_ref[0].astype(BF16)

    return pl.pallas_call(
        body, name=name, grid=(N_CHIPS,),
        in_specs=[pl.BlockSpec((1, IN_SH, D_MODEL), lambda j: (j, 0, 0)),
                  pl.BlockSpec((1, OUT_SH, D_MODEL), lambda j: (j, 0, 0))],
        out_specs=pl.BlockSpec((1, FF_SH, D_MODEL), lambda j: (j, 0, 0)),
        out_shape=jax.ShapeDtypeStruct((N_CHIPS, FF_SH, D_MODEL), BF16),
        compiler_params=_cparams(("arbitrary",)),
    )(dw_in_t.reshape(N_CHIPS, IN_SH, D_MODEL), dw_out.reshape(N_CHIPS, OUT_SH, D_MODEL))


def _norm_bwd(dhp, x, dy, g, *, tile, name):
    s = x.shape[0]

    def body(dhp_ref, x_ref, dy_ref, g_ref, dx_ref, dg_ref):
        @pl.when(pl.program_id(0) == 0)
        def _():
            dg_ref[...] = jnp.zeros_like(dg_ref)

        dh = (dhp_ref[0] + dhp_ref[1]) + (dhp_ref[2] + dhp_ref[3])
        x_v = x_ref[...]
        r = lax.rsqrt(jnp.mean(x_v * x_v, axis=-1, keepdims=True) + EPS)
        dx, dg = _rms_bwd(dh, x_v, r, g_ref[...])
        dx_ref[...] = dy_ref[...] + dx
        dg_ref[...] += dg

    return pl.pallas_call(
        body, name=name, grid=(s // tile,),
        in_specs=[pl.BlockSpec((N_CHIPS, tile, D_MODEL), lambda i: (0, i, 0)),
                  _rows(tile, D_MODEL), _rows(tile, D_MODEL), _const((1, D_MODEL))],
        out_specs=[_rows(tile, D_MODEL), _const((1, D_MODEL))],
        out_shape=[jax.ShapeDtypeStruct(x.shape, F32), jax.ShapeDtypeStruct((1, D_MODEL), F32)],
        compiler_params=_cparams(("arbitrary",)),
    )(dhp, x, dy, g)


def _mix_in_fwd(x, g, w_in_t, b_in, *, tile, name):
    s = x.shape[0]

    def body(x_ref, g_ref, w_ref, b_ref, q_ref, k_ref, v_ref, z_ref):
        h, _ = _rms(x_ref[...], g_ref[...])
        proj = _dot_nt(h.astype(BF16), w_ref[...]) + b_ref[...]
        q_ref[...] = proj[:, :ATTN_W].astype(BF16)
        k_ref[...] = proj[:, ATTN_W:ATTN_W + KV_W].astype(BF16)
        v_ref[...] = proj[:, ATTN_W + KV_W:ATTN_W + 2 * KV_W].astype(BF16)
        z_ref[...] = proj[:, ATTN_W + 2 * KV_W:]

    return pl.pallas_call(
        body, name=name, grid=(s // tile,),
        in_specs=[_rows(tile, D_MODEL), _const((1, D_MODEL)), _const((IN_W, D_MODEL)), _const((1, IN_W))],
        out_specs=[_rows(tile, ATTN_W), _rows(tile, KV_W), _rows(tile, KV_W), _rows(tile, 2 * GMLP_W)],
        out_shape=[jax.ShapeDtypeStruct((s, ATTN_W), BF16), jax.ShapeDtypeStruct((s, KV_W), BF16),
                   jax.ShapeDtypeStruct((s, KV_W), BF16), jax.ShapeDtypeStruct((s, 2 * GMLP_W), F32)],
        compiler_params=_cparams(("arbitrary",)),
    )(x, g, w_in_t, b_in)


def _mix_in_bwd(x, dy, dq, dk, dv, dz, g, w_in_t, *, tile, name):
    s = x.shape[0]

    def body(x_ref, dy_ref, dq_ref, dk_ref, dv_ref, dz_ref, g_ref, w_ref, dx_ref, dw_ref, db_ref, dg_ref):
        @pl.when(pl.program_id(0) == 0)
        def _():
            dw_ref[...] = jnp.zeros_like(dw_ref)
            db_ref[...] = jnp.zeros_like(db_ref)
            dg_ref[...] = jnp.zeros_like(dg_ref)

        dproj = jnp.concatenate([dq_ref[...], dk_ref[...], dv_ref[...], dz_ref[...]], axis=-1)
        db_ref[...] += jnp.sum(dproj, axis=0, keepdims=True)
        dpb = dproj.astype(BF16)
        x_v = x_ref[...]
        g_v = g_ref[...]
        h, r = _rms(x_v, g_v)
        dw_ref[...] += _dot_tn(dpb, h.astype(BF16))
        dh = _dot(dpb, w_ref[...])
        dx, dg = _rms_bwd(dh, x_v, r, g_v)
        dx_ref[...] = dy_ref[...] + dx
        dg_ref[...] += dg

    return pl.pallas_call(
        body, name=name, grid=(s // tile,),
        in_specs=[_rows(tile, D_MODEL), _rows(tile, D_MODEL), _rows(tile, ATTN_W), _rows(tile, KV_W),
                  _rows(tile, KV_W), _rows(tile, 2 * GMLP_W), _const((1, D_MODEL)), _const((IN_W, D_MODEL))],
        out_specs=[_rows(tile, D_MODEL), _const((IN_W, D_MODEL)), _const((1, IN_W)), _const((1, D_MODEL))],
        out_shape=[jax.ShapeDtypeStruct(x.shape, F32), jax.ShapeDtypeStruct((IN_W, D_MODEL), F32),
                   jax.ShapeDtypeStruct((1, IN_W), F32), jax.ShapeDtypeStruct((1, D_MODEL), F32)],
        compiler_params=_cparams(("arbitrary",)),
    )(x, dy, dq, dk, dv, dz, g, w_in_t)


_GELU_C = 0.7978845608028654
_GELU_A = 0.044715


def _gelu(x):
    return 0.5 * x * (1.0 + jnp.tanh(_GELU_C * (x + _GELU_A * (x * x * x))))


def _gelu_grad(x):
    t = jnp.tanh(_GELU_C * (x + _GELU_A * (x * x * x)))
    return 0.5 * (1.0 + t) + 0.5 * x * (1.0 - t * t) * (_GELU_C * (1.0 + 3.0 * _GELU_A * (x * x)))


def _band(ref, i):
    prev = jnp.maximum(i - 1, 0)
    return jnp.concatenate([ref[pl.ds(pl.multiple_of(prev * BLK, BLK), BLK), :],
                            ref[pl.ds(pl.multiple_of(i * BLK, BLK), BLK), :]], axis=0)


def _band_mask(i):
    qpos = lax.broadcasted_iota(jnp.int32, (BLK, 2 * BLK), 0)
    kidx = lax.broadcasted_iota(jnp.int32, (BLK, 2 * BLK), 1)
    rel = qpos - kidx + BLK
    win = jnp.where(rel >= 0, jnp.where(rel < BLK, 1, 0), 0)
    real = jnp.where(kidx >= BLK, 1, jnp.where(i > 0, 1, 0))
    return (win * real) > 0


def _attn_probs(qh, kg, mask, sink):
    sc = _dot_nt(qh, kg) * ATTN_SCALE
    sc = jnp.where(mask, sc, -jnp.inf)
    m = jnp.maximum(jnp.max(sc, axis=-1, keepdims=True), sink)
    p = jnp.exp(sc - m)
    es = jnp.exp(sink - m)
    inv = 1.0 / (jnp.sum(p, axis=-1, keepdims=True) + es)
    return p * inv, es * inv


def _tril_mask():
    t = lax.broadcasted_iota(jnp.int32, (BLK, BLK), 0)
    s_ = lax.broadcasted_iota(jnp.int32, (BLK, BLK), 1)
    return s_ <= t


def _gmlp_fwd_parts(zg, lng, lnb, ws_ref, bs_full):
    z = _gelu(zg)
    u = z[:, :GMLP_W]
    zv = z[:, GMLP_W:]
    mu = jnp.mean(zv, axis=-1, keepdims=True)
    zc = zv - mu
    rstd = lax.rsqrt(jnp.mean(zc * zc, axis=-1, keepdims=True) + EPS)
    xh = zc * rstd
    vvb = (xh * lng + lnb).astype(BF16)
    tril = _tril_mask()
    wms, parts = [], []
    for gi in range(GMLP_GROUPS):
        wm = jnp.where(tril, ws_ref[gi], 0.0).astype(BF16)
        wms.append(wm)
        parts.append(_dot(wm, vvb[:, gi * GROUP_DIM:(gi + 1) * GROUP_DIM]))
    mixed = jnp.concatenate(parts, axis=-1) + bs_full
    return u, xh, rstd, vvb, wms, mixed


def _attn_fwd(q, kb, vb, mask, sink_ref):
    outs = []
    for h in range(N_Q_HEADS):
        gi = h // REP
        pn, _ = _attn_probs(q[:, h * HEAD_DIM:(h + 1) * HEAD_DIM], kb[:, gi * HEAD_DIM:(gi + 1) * HEAD_DIM],
                            mask, sink_ref[h])
        outs.append(_dot(pn.astype(BF16), vb[:, gi * HEAD_DIM:(gi + 1) * HEAD_DIM]))
    return jnp.concatenate(outs, axis=-1)


def _mix_core_fwd(q, k, v, zg, sinks, lng, lnb, w_s, bs_full, gao, ggo, *, name):
    s = q.shape[0]

    def body(sink_ref, q_ref, k_ref, v_ref, z_ref, lng_ref, lnb_ref, ws_ref, bs_ref, gao_ref, ggo_ref, o_ref):
        i = pl.program_id(0)
        mask = _band_mask(i)
        y_attn = _attn_fwd(q_ref[...], _band(k_ref, i), _band(v_ref, i), mask, sink_ref)
        u, _, _, _, _, mixed = _gmlp_fwd_parts(z_ref[...], lng_ref[...], lnb_ref[...], ws_ref, bs_ref[...])
        ya, _ = _rms(y_attn, gao_ref[...])
        yg, _ = _rms(u * mixed, ggo_ref[...])
        o_ref[...] = jnp.concatenate([ya, yg], axis=-1).astype(BF16)

    return pl.pallas_call(
        body, name=name, grid=(s // BLK,),
        in_specs=[pl.BlockSpec(memory_space=pltpu.SMEM),
                  _rows(BLK, ATTN_W), _const((s, KV_W)), _const((s, KV_W)),
                  _rows(BLK, 2 * GMLP_W), _const((1, GMLP_W)), _const((1, GMLP_W)),
                  _const((GMLP_GROUPS, BLK, BLK)), _const((BLK, GMLP_W)), _const((1, ATTN_W)), _const((1, GMLP_W))],
        out_specs=_rows(BLK, D_MODEL),
        out_shape=jax.ShapeDtypeStruct((s, D_MODEL), BF16),
        compiler_params=_cparams(("arbitrary",)),
    )(sinks, q, k, v, zg, lng, lnb, w_s, bs_full, gao, ggo)


def _mix_out_fwd(x1, yb, w_out, b_out, *, tile, name):
    s = x1.shape[0]

    def body(x_ref, y_ref, w_ref, b_ref, o_ref):
        o_ref[...] = x_ref[...] + (_dot(y_ref[...], w_ref[...]) + b_ref[...])

    return pl.pallas_call(
        body, name=name, grid=(s // tile,),
        in_specs=[_rows(tile, D_MODEL), _rows(tile, D_MODEL), _const((D_MODEL, D_MODEL)), _const((1, D_MODEL))],
        out_specs=_rows(tile, D_MODEL),
        out_shape=jax.ShapeDtypeStruct(x1.shape, F32),
        compiler_params=_cparams(("arbitrary",)),
    )(x1, yb, w_out, b_out)


def _norm_bwd_mix_out(dhp, x, dy, g, yb, w_out, *, tile, name):
    s = x.shape[0]

    def body(dhp_ref, x_ref, dy_ref, g_ref, y_ref, w_ref, dx_ref, dg_ref, dyy_ref, dw_ref, db_ref):
        @pl.when(pl.program_id(0) == 0)
        def _():
            dg_ref[...] = jnp.zeros_like(dg_ref)
            dw_ref[...] = jnp.zeros_like(dw_ref)
            db_ref[...] = jnp.zeros_like(db_ref)

        dh = (dhp_ref[0] + dhp_ref[1]) + (dhp_ref[2] + dhp_ref[3])
        x_v = x_ref[...]
        r = lax.rsqrt(jnp.mean(x_v * x_v, axis=-1, keepdims=True) + EPS)
        dxn, dg = _rms_bwd(dh, x_v, r, g_ref[...])
        dx = dy_ref[...] + dxn
        dx_ref[...] = dx
        dg_ref[...] += dg
        dxb = dx.astype(BF16)
        db_ref[...] += jnp.sum(dx, axis=0, keepdims=True)
        dw_ref[...] += _dot_tn(y_ref[...], dxb)
        dyy_ref[...] = _dot_nt(dxb, w_ref[...])

    return pl.pallas_call(
        body, name=name, grid=(s // tile,),
        in_specs=[pl.BlockSpec((N_CHIPS, tile, D_MODEL), lambda i: (0, i, 0)),
                  _rows(tile, D_MODEL), _rows(tile, D_MODEL), _const((1, D_MODEL)), _rows(tile, D_MODEL),
                  _const((D_MODEL, D_MODEL))],
        out_specs=[_rows(tile, D_MODEL), _const((1, D_MODEL)), _rows(tile, D_MODEL), _const((D_MODEL, D_MODEL)),
                   _const((1, D_MODEL))],
        out_shape=[jax.ShapeDtypeStruct(x.shape, F32), jax.ShapeDtypeStruct((1, D_MODEL), F32),
                   jax.ShapeDtypeStruct(x.shape, F32), jax.ShapeDtypeStruct((D_MODEL, D_MODEL), F32),
                   jax.ShapeDtypeStruct((1, D_MODEL), F32)],
        compiler_params=_cparams(("arbitrary",)),
    )(dhp, x, dy, g, yb, w_out)


def _mix_core_bwd(dyy, q, k, v, zg, sinks, lng, lnb, w_s, bs_full, gao, ggo, *, name):
    s = dyy.shape[0]
    nblk = s // BLK

    def body(sink_ref, dyy_ref, q_ref, k_ref, v_ref, z_ref, lng_ref, lnb_ref, ws_ref, bs_ref, gao_ref, ggo_ref,
             dq_ref, dk_ref, dv_ref, dz_ref, dgao_ref, dggo_ref, dlng_ref, dlnb_ref, dws_ref, dms_ref, dsk_ref):
        i = pl.program_id(0)

        @pl.when(i == 0)
        def _():
            for ref in (dk_ref, dv_ref, dgao_ref, dggo_ref, dlng_ref, dlnb_ref, dws_ref, dms_ref, dsk_ref):
                ref[...] = jnp.zeros_like(ref)

        mask = _band_mask(i)
        q_v = q_ref[...]
        kb = _band(k_ref, i)
        vb = _band(v_ref, i)
        lng_v = lng_ref[...]
        gao_v = gao_ref[...]
        ggo_v = ggo_ref[...]
        zg_v = z_ref[...]

        y_attn = _attn_fwd(q_v, kb, vb, mask, sink_ref)
        u, xh, rstd, vvb, wms, mixed = _gmlp_fwd_parts(zg_v, lng_v, lnb_ref[...], ws_ref, bs_ref[...])
        y_gmlp = u * mixed
        ra = lax.rsqrt(jnp.mean(y_attn * y_attn, axis=-1, keepdims=True) + EPS)
        rg = lax.rsqrt(jnp.mean(y_gmlp * y_gmlp, axis=-1, keepdims=True) + EPS)

        dyy = dyy_ref[...]
        d_attn, dgao = _rms_bwd(dyy[:, :ATTN_W], y_attn, ra, gao_v)
        d_gmlp, dggo = _rms_bwd(dyy[:, ATTN_W:], y_gmlp, rg, ggo_v)
        dgao_ref[...] += dgao
        dggo_ref[...] += dggo

        du = d_gmlp * mixed
        dmixed = d_gmlp * u
        dms_ref[...] += dmixed
        dmb = dmixed.astype(BF16)
        dvv_parts = []
        for gi in range(GMLP_GROUPS):
            sl = slice(gi * GROUP_DIM, (gi + 1) * GROUP_DIM)
            dws_ref[gi] += _dot_nt(dmb[:, sl], vvb[:, sl])
            dvv_parts.append(_dot_tn(wms[gi], dmb[:, sl]))
        dvv = jnp.concatenate(dvv_parts, axis=-1)
        dlng_ref[...] += jnp.sum(dvv * xh, axis=0, keepdims=True)
        dlnb_ref[...] += jnp.sum(dvv, axis=0, keepdims=True)
        dxh = dvv * lng_v
        dzv = rstd * (dxh - jnp.mean(dxh, axis=-1, keepdims=True)
                      - xh * jnp.mean(dxh * xh, axis=-1, keepdims=True))
        dz_ref[...] = jnp.concatenate([du, dzv], axis=-1) * _gelu_grad(zg_v)

        dab = d_attn.astype(BF16)
        dq_parts = []
        dk_parts = []
        dv_parts = []
        for gi in range(N_KV_HEADS):
            kg = kb[:, gi * HEAD_DIM:(gi + 1) * HEAD_DIM]
            vg = vb[:, gi * HEAD_DIM:(gi + 1) * HEAD_DIM]
            dkg = jnp.zeros((2 * BLK, HEAD_DIM), F32)
            dvg = jnp.zeros((2 * BLK, HEAD_DIM), F32)
            for rr in range(REP):
                h = gi * REP + rr
                hs = slice(h * HEAD_DIM, (h + 1) * HEAD_DIM)
                qh = q_v[:, hs]
                doh = dab[:, hs]
                pn, psink = _attn_probs(qh, kg, mask, sink_ref[h])
                dp = _dot_nt(doh, vg)
                delta = jnp.sum(pn * dp, axis=-1, keepdims=True)
                dsb = (pn * (dp - delta) * ATTN_SCALE).astype(BF16)
                dsink = jnp.sum(-psink * delta, axis=0, keepdims=True)
                dsk_ref[pl.ds(h, 1), :] += jnp.broadcast_to(dsink, (1, 128))
                dq_parts.append(_dot(dsb, kg))
                dkg = dkg + _dot_tn(dsb, qh)
                dvg = dvg + _dot_tn(pn.astype(BF16), doh)
            dk_parts.append(dkg)
            dv_parts.append(dvg)
        dq_ref[...] = jnp.concatenate(dq_parts, axis=-1)
        dkb = jnp.concatenate(dk_parts, axis=-1)
        dvb = jnp.concatenate(dv_parts, axis=-1)
        prev = pl.ds(pl.multiple_of(jnp.maximum(i - 1, 0) * BLK, BLK), BLK)
        cur = pl.ds(pl.multiple_of(i * BLK, BLK), BLK)
        dk_ref[prev, :] += dkb[:BLK]
        dv_ref[prev, :] += dvb[:BLK]
        dk_ref[cur, :] += dkb[BLK:]
        dv_ref[cur, :] += dvb[BLK:]

        @pl.when(i == nblk - 1)
        def _():
            tril = _tril_mask()
            for gi in range(GMLP_GROUPS):
                dws_ref[gi] = jnp.where(tril, dws_ref[gi], 0.0)

    return pl.pallas_call(
        body, name=name, grid=(nblk,),
        in_specs=[pl.BlockSpec(memory_space=pltpu.SMEM),
                  _rows(BLK, D_MODEL), _rows(BLK, ATTN_W), _const((s, KV_W)), _const((s, KV_W)),
                  _rows(BLK, 2 * GMLP_W), _const((1, GMLP_W)), _const((1, GMLP_W)),
                  _const((GMLP_GROUPS, BLK, BLK)), _const((BLK, GMLP_W)), _const((1, ATTN_W)), _const((1, GMLP_W))],
        out_specs=[_rows(BLK, ATTN_W), _const((s, KV_W)), _const((s, KV_W)), _rows(BLK, 2 * GMLP_W),
                   _const((1, ATTN_W)), _const((1, GMLP_W)),
                   _const((1, GMLP_W)), _const((1, GMLP_W)), _const((GMLP_GROUPS, BLK, BLK)),
                   _const((BLK, GMLP_W)), _const((N_Q_HEADS, 128))],
        out_shape=[jax.ShapeDtypeStruct((s, ATTN_W), F32), jax.ShapeDtypeStruct((s, KV_W), F32),
                   jax.ShapeDtypeStruct((s, KV_W), F32), jax.ShapeDtypeStruct((s, 2 * GMLP_W), F32),
                   jax.ShapeDtypeStruct((1, ATTN_W), F32), jax.ShapeDtypeStruct((1, GMLP_W), F32),
                   jax.ShapeDtypeStruct((1, GMLP_W), F32), jax.ShapeDtypeStruct((1, GMLP_W), F32),
                   jax.ShapeDtypeStruct((GMLP_GROUPS, BLK, BLK), F32), jax.ShapeDtypeStruct((BLK, GMLP_W), F32),
                   jax.ShapeDtypeStruct((N_Q_HEADS, 128), F32)],
        compiler_params=_cparams(("arbitrary",)),
    )(sinks, dyy, q, k, v, zg, lng, lnb, w_s, bs_full, gao, ggo)


def _local_step(place, x, tgt, p, pack, *, tile=512, bwd_tile=256):
    g = {}
    tile, bwd_tile = min(tile, x.shape[0]), min(bwd_tile, x.shape[0])
    x1 = _ffn_fwd(x, p["ffn1_norm_g"], pack, 0, tile=tile, name="ffn1_fwd")
    q, k, v, zg = _mix_in_fwd(x1, p["mix_norm_g"], p["w_in_t"], p["b_in"], tile=tile, name="mix_in_fwd")
    mix_args = (q, k, v, zg, p["attn_sinks"], p["gmlp_ln_g"], p["gmlp_ln_b"], p["gmlp_w_s"], p["bs_full"],
                p["attn_out_norm_g"], p["gmlp_out_norm_g"])
    yb = _mix_core_fwd(*mix_args, name="mix_core_fwd")
    x2 = _mix_out_fwd(x1, yb, p["w_out"], p["b_out"], tile=tile, name="mix_out_fwd")
    dx3, loss, g["final_norm_g"] = _ffn_fwd_loss(x2, p["ffn2_norm_g"], pack, 3, p["final_norm_g"], tgt,
                                                 tile=tile, name="ffn2_fwd_loss")

    dhp, land = _ffn_bwd(place, x2, dx3, p["ffn2_norm_g"], pack, 3, None, None, tile=bwd_tile, name="ffn2_bwd")
    dx2, g["ffn2_norm_g"], dyy, dw_out, g["b_out"] = _norm_bwd_mix_out(
        dhp, x2, dx3, p["ffn2_norm_g"], yb, p["w_out"], tile=bwd_tile, name="ffn2_norm_bwd")

    (dq, dk, dv, dz, g["attn_out_norm_g"], g["gmlp_out_norm_g"], g["gmlp_ln_g"],
     g["gmlp_ln_b"], g["gmlp_w_s"], dmix_sum, dsinks) = _mix_core_bwd(dyy, *mix_args, name="mix_core_bwd")
    g["gmlp_b_s"] = dmix_sum
    g["attn_sinks"] = dsinks
    dx1, dw_in_t, g["b_in"], g["mix_norm_g"] = _mix_in_bwd(
        x1, dx2, dq, dk, dv, dz, p["mix_norm_g"], p["w_in_t"], tile=tile, name="mix_in_bwd")
    mix_grads = _mix_grads_pack(dw_in_t, dw_out, name="mix_grads_pack")

    dhp1, land = _ffn_bwd(place, x, dx1, p["ffn1_norm_g"], pack, 0, land, mix_grads, tile=bwd_tile, name="ffn1_bwd")
    dx0, g["ffn1_norm_g"] = _norm_bwd(dhp1, x, dx1, p["ffn1_norm_g"], tile=bwd_tile, name="ffn1_norm_bwd")
    return loss, dx0, land, g


def _pack_cast(place, parts, *, name):
    def body(place_ref, *refs):
        o_ref = refs[-1]
        off = 0
        for ref, rows in zip(refs[:-1], BIG_ROWS):
            o_ref[0, off:off + rows, :] = ref[...].astype(BF16)
            off += rows

    one = pl.Buffered(1)
    grid_spec = pltpu.PrefetchScalarGridSpec(
        num_scalar_prefetch=1, grid=(1,),
        in_specs=[pl.BlockSpec((rows, D_MODEL), lambda i, pr: (0, 0), pipeline_mode=one) for rows in BIG_ROWS],
        out_specs=pl.BlockSpec((1, PACK_ROWS, D_MODEL), lambda i, pr: (pr[0], 0, 0), pipeline_mode=one))
    return pl.pallas_call(
        body, name=name, grid_spec=grid_spec,
        out_shape=jax.ShapeDtypeStruct((N_CHIPS, PACK_ROWS, D_MODEL), BF16),
        compiler_params=_cparams(("arbitrary",)),
    )(place, *parts)


def _mesh_place():
    x, y, c = lax.axis_index("x"), lax.axis_index("y"), lax.axis_index("c")
    others = [(1 - x, y), (x, 1 - y), (1 - x, 1 - y)]
    return x, y, c, others


def _half(c):
    return pl.ds(pl.multiple_of(c * HALF_ROWS, 16), HALF_ROWS)


def _all_gather_pack(pack, *, name):
    def body(p_ref, o_ref, send_sems, recv_sems):
        x, y, c, others = _mesh_place()
        me = 2 * x + y
        sibling = (x, y, 1 - c)
        mine, theirs = _half(c), _half(1 - c)

        def copy(k, src, dst, to):
            return pltpu.make_async_remote_copy(src_ref=src, dst_ref=dst, send_sem=send_sems.at[k],
                                                recv_sem=recv_sems.at[k], device_id=to, device_id_type=MESH)

        first = [copy(j, o_ref.at[me, mine], o_ref.at[me, mine], (px, py, c)) for j, (px, py) in enumerate(others)]
        for cp in first:
            cp.start()
        passed = []
        for j, (px, py) in enumerate(others):
            slab = o_ref.at[2 * px + py, mine]
            copy(j, slab, slab, (px, py, c)).wait_recv()
            fwd = copy(3 + j, slab, slab, sibling)
            fwd.start()
            passed.append(fwd)
        for j, (px, py) in enumerate(others):
            slab = o_ref.at[2 * px + py, theirs]
            copy(3 + j, slab, slab, sibling).wait_recv()
        for cp in first + passed:
            cp.wait_send()

    return pl.pallas_call(
        body, name=name,
        in_specs=[pl.BlockSpec(memory_space=pl.ANY)],
        out_specs=pl.BlockSpec(memory_space=pl.ANY),
        out_shape=jax.ShapeDtypeStruct((N_CHIPS, PACK_ROWS, D_MODEL), BF16),
        input_output_aliases={0: 0},
        scratch_shapes=[pltpu.SemaphoreType.DMA((6,)), pltpu.SemaphoreType.DMA((6,))],
    )(pack)


def _shard_tile(i, c):
    return jnp.where(i < 3, 3 * c + i, jnp.where(i < 6, 3 + 3 * c + i, 12 + c))


def _rs_reduce(place, land, *, name):
    def body(place_ref, l_ref, o_ref):
        acc = l_ref[0].astype(F32)
        for d in range(1, 2 * N_CHIPS):
            acc = acc + l_ref[d].astype(F32)
        o_ref[...] = acc

    grid_spec = pltpu.PrefetchScalarGridSpec(
        num_scalar_prefetch=1, grid=(HALF_ROWS // MIX_HALF,),
        in_specs=[pl.BlockSpec((2 * N_CHIPS, MIX_HALF, D_MODEL), lambda i, pr: (0, i, 0))],
        out_specs=pl.BlockSpec((MIX_HALF, D_MODEL), lambda i, pr: (_shard_tile(i, pr[1]), 0)))
    return pl.pallas_call(
        body, name=name, grid_spec=grid_spec,
        out_shape=jax.ShapeDtypeStruct((PACK_ROWS, D_MODEL), F32),
        compiler_params=_cparams(("arbitrary",)),
    )(place, land)


def _rs_share(shard, *, name):
    def body(s_ref, o_ref, send_sems, recv_sems):
        x, y, c, _ = _mesh_place()

        def rows(k, core):
            if k < 2:
                return o_ref.at[pl.ds(pl.multiple_of(k * 2 * FFN_HALF + core * FFN_HALF, 8), FFN_HALF)]
            return o_ref.at[pl.ds(pl.multiple_of(4 * FFN_HALF + core * MIX_HALF, 8), MIX_HALF)]

        def copy(k, core):
            return pltpu.make_async_remote_copy(src_ref=rows(k, core), dst_ref=rows(k, core), send_sem=send_sems.at[k],
                                                recv_sem=recv_sems.at[k], device_id=(x, y, 1 - c),
                                                device_id_type=MESH)

        sends = [copy(k, c) for k in range(3)]
        for cp in sends:
            cp.start()
        for k in range(3):
            copy(k, 1 - c).wait_recv()
        for cp in sends:
            cp.wait_send()

    return pl.pallas_call(
        body, name=name,
        in_specs=[pl.BlockSpec(memory_space=pl.ANY)],
        out_specs=pl.BlockSpec(memory_space=pl.ANY),
        out_shape=jax.ShapeDtypeStruct((PACK_ROWS, D_MODEL), F32),
        input_output_aliases={0: 0},
        scratch_shapes=[pltpu.SemaphoreType.DMA((3,)), pltpu.SemaphoreType.DMA((3,))],
    )(shard)


def _small_all_reduce(packed, *, name):
    rows = packed.shape[0]

    def body(p_ref, o_ref, sib_ref, slots_ref, send_sems, recv_sems):
        x, y, c, others = _mesh_place()
        me = 2 * x + y
        sib = pltpu.make_async_remote_copy(src_ref=p_ref, dst_ref=sib_ref, send_sem=send_sems.at[0],
                                           recv_sem=recv_sems.at[0], device_id=(x, y, 1 - c), device_id_type=MESH)
        sib.start()
        sib.wait()
        slots_ref[me] = p_ref[...] + sib_ref[...]
        sends = [pltpu.make_async_remote_copy(
            src_ref=slots_ref.at[me], dst_ref=slots_ref.at[me], send_sem=send_sems.at[1 + j],
            recv_sem=recv_sems.at[1 + j], device_id=(px, py, c), device_id_type=MESH)
            for j, (px, py) in enumerate(others)]
        for cp in sends:
            cp.start()
        for j, (px, py) in enumerate(others):
            slab = slots_ref.at[2 * px + py]
            pltpu.make_async_remote_copy(src_ref=slab, dst_ref=slab, send_sem=send_sems.at[1 + j],
                                         recv_sem=recv_sems.at[1 + j], device_id=(px, py, c),
                                         device_id_type=MESH).wait_recv()
        for cp in sends:
            cp.wait_send()
        o_ref[...] = (slots_ref[0] + slots_ref[1]) + (slots_ref[2] + slots_ref[3])

    vm = pl.BlockSpec(memory_space=pltpu.VMEM)
    return pl.pallas_call(
        body, name=name, in_specs=[vm], out_specs=vm,
        out_shape=jax.ShapeDtypeStruct((rows, 128), F32),
        scratch_shapes=[pltpu.VMEM((rows, 128), F32), pltpu.VMEM((N_CHIPS, rows, 128), F32),
                        pltpu.SemaphoreType.DMA((4,)), pltpu.SemaphoreType.DMA((4,))],
    )(packed)


def _adamw(w, g, m, v, *, g_row0, tile, name):
    rows, cols = w.shape
    assert g_row0 % tile == 0 and rows % tile == 0

    def body(w_ref, g_ref, m_ref, v_ref, go_ref, d_ref, nm_ref, nv_ref):
        g_v = g_ref[...]
        m_n = ADAM_B1 * m_ref[...] + (1.0 - ADAM_B1) * g_v
        v_n = ADAM_B2 * v_ref[...] + (1.0 - ADAM_B2) * (g_v * g_v)
        m_hat = m_n / (1.0 - ADAM_B1 ** ADAM_STEP)
        v_hat = v_n / (1.0 - ADAM_B2 ** ADAM_STEP)
        d_ref[...] = -ADAM_LR * (m_hat / (jnp.sqrt(v_hat) + ADAM_EPS) + ADAM_WD * w_ref[...])
        go_ref[...] = g_v
        nm_ref[...] = m_n
        nv_ref[...] = v_n

    spec = pl.BlockSpec((tile, cols), lambda i: (i, 0))
    gspec = pl.BlockSpec((tile, cols), lambda i: (g_row0 // tile + i, 0))
    shape = jax.ShapeDtypeStruct((rows, cols), F32)
    return pl.pallas_call(
        body, name=name, grid=(rows // tile,),
        in_specs=[spec, gspec, spec, spec], out_specs=[spec] * 4, out_shape=[shape] * 4,
        compiler_params=_cparams(("arbitrary",)),
    )(w, g, m, v)


def kernel(x, ffn1_norm_g, ffn1_w_gate, ffn1_w_up, ffn1_w_down, mix_norm_g, w_in, b_in, attn_sinks, gmlp_ln_g, gmlp_ln_b, gmlp_w_s, gmlp_b_s, attn_out_norm_g, gmlp_out_norm_g, w_out, b_out, ffn2_norm_g, ffn2_w_gate, ffn2_w_up, ffn2_w_down, final_norm_g, loss_target, m_ffn1_norm_g, m_ffn1_w_gate, m_ffn1_w_up, m_ffn1_w_down, m_mix_norm_g, m_w_in, m_b_in, m_attn_sinks, m_gmlp_ln_g, m_gmlp_ln_b, m_gmlp_w_s, m_gmlp_b_s, m_attn_out_norm_g, m_gmlp_out_norm_g, m_w_out, m_b_out, m_ffn2_norm_g, m_ffn2_w_gate, m_ffn2_w_up, m_ffn2_w_down, m_final_norm_g, v_ffn1_norm_g, v_ffn1_w_gate, v_ffn1_w_up, v_ffn1_w_down, v_mix_norm_g, v_w_in, v_b_in, v_attn_sinks, v_gmlp_ln_g, v_gmlp_ln_b, v_gmlp_w_s, v_gmlp_b_s, v_attn_out_norm_g, v_gmlp_out_norm_g, v_w_out, v_b_out, v_ffn2_norm_g, v_ffn2_w_gate, v_ffn2_w_up, v_ffn2_w_down, v_final_norm_g):
    f_args = dict(locals())
    weights = {n: f_args[n] for n in [nm for nm, _ in SMALL if nm != "loss"] + list(BIG)}
    shapes = {n: weights[n].shape for n in weights}
    shapes["loss"] = ()
    place = jnp.stack([2 * lax.axis_index("x") + lax.axis_index("y"), lax.axis_index("c")]).astype(jnp.int32)

    def with_cols(name, a):
        a2 = a.reshape(a.shape[-2], a.shape[-1])
        return a2.T if BIG_TRANSPOSED[BIG.index(name)] else a2

    def natural(name, a2):
        return (a2.T if BIG_TRANSPOSED[BIG.index(name)] else a2).reshape(shapes[name])

    pack = _all_gather_pack(_pack_cast(place, [with_cols(n, weights[n]) for n in BIG], name="pack_cast"),
                            name="ag_weights")
    mix_rows = pack[:, MIX_BLOCK * FF_SH:, :]
    p = {n: weights[n].reshape(1, -1) for n in ("ffn1_norm_g", "mix_norm_g", "b_in", "gmlp_ln_g", "gmlp_ln_b",
                                                "attn_out_norm_g", "gmlp_out_norm_g", "b_out", "ffn2_norm_g",
                                                "final_norm_g")}
    p["w_in_t"] = mix_rows[:, :IN_SH, :].reshape(IN_W, D_MODEL)
    p["w_out"] = mix_rows[:, IN_SH:, :].reshape(D_MODEL, D_MODEL)
    p["attn_sinks"] = attn_sinks.reshape(N_Q_HEADS)
    p["gmlp_w_s"] = gmlp_w_s.reshape(GMLP_GROUPS, BLK, BLK)
    p["bs_full"] = jnp.broadcast_to(gmlp_b_s.reshape(GMLP_GROUPS, BLK).T[:, :, None],
                                    (BLK, GMLP_GROUPS, GROUP_DIM)).reshape(BLK, GMLP_W)

    loss_part, dx0, land, gs = _local_step(place, x[0], loss_target[0], p, pack)

    shard = _rs_share(_rs_reduce(place, land, name="rs_reduce"), name="rs_share")
    gs["gmlp_b_s"] = jnp.sum(gs["gmlp_b_s"].reshape(BLK, GMLP_GROUPS, GROUP_DIM), axis=-1).T
    gs["attn_sinks"] = gs["attn_sinks"][:, 0]
    gs["loss"] = loss_part[0, 0]
    small_sum = _small_all_reduce(_pack_small(gs), name="small_all_reduce")

    grad_w, delta, new_m, new_v = {}, {}, {}, {}
    off = 0
    for n, rows in zip(BIG, BIG_ROWS):
        res = _adamw(with_cols(n, weights[n]), shard, with_cols(n, f_args["m_" + n]), with_cols(n, f_args["v_" + n]),
                     g_row0=off, tile=FF_SH // 2 if rows == FF_SH else 64, name="adamw_" + n)
        grad_w[n], delta[n], new_m[n], new_v[n] = [natural(n, a) for a in res]
        off += rows
    sm = {k: {n: f_args[k + n] for n, _ in SMALL if n != "loss"} for k in ("", "m_", "v_")}
    for k in sm:
        sm[k]["loss"] = jnp.zeros((), F32)
    res = _adamw(_pack_small(sm[""]), small_sum, _pack_small(sm["m_"]), _pack_small(sm["v_"]),
                 g_row0=0, tile=SMALL_ROWS, name="adamw_small")
    small = _unpack_small(res[0], shapes)
    for dst, packed in ((grad_w, res[0]), (delta, res[1]), (new_m, res[2]), (new_v, res[3])):
        dst.update({n: a for n, a in _unpack_small(packed, shapes).items() if n != "loss"})

    order = ('ffn1_norm_g', 'ffn1_w_gate', 'ffn1_w_up', 'ffn1_w_down', 'mix_norm_g', 'w_in', 'b_in', 'attn_sinks',
             'gmlp_ln_g', 'gmlp_ln_b', 'gmlp_w_s', 'gmlp_b_s', 'attn_out_norm_g', 'gmlp_out_norm_g', 'w_out', 'b_out',
             'ffn2_norm_g', 'ffn2_w_gate', 'ffn2_w_up', 'ffn2_w_down', 'final_norm_g')
    return (small["loss"], dx0.reshape(x.shape), *[grad_w[n] for n in order], *[delta[n] for n in order],
            *[new_m[n] for n in order], *[new_v[n] for n in order])
```

```python
import functools

import jax
import jax.numpy as jnp
from jax import lax
from jax.experimental import pallas as pl
from jax.experimental.pallas import tpu as pltpu

F32 = jnp.float32
BF16 = jnp.bfloat16

D_MODEL = 1024
D_FF = 2816
N_CHIPS = 4
FF_SH = D_FF // N_CHIPS
N_Q_HEADS = 8
N_KV_HEADS = 2
REP = N_Q_HEADS // N_KV_HEADS
HEAD_DIM = 64
ATTN_W = 512
KV_W = 128
GMLP_W = 512
GMLP_GROUPS = 8
GROUP_DIM = 64
BLK = 128
IN_W = 1792
IN_SH = IN_W // N_CHIPS
OUT_SH = D_MODEL // N_CHIPS
EPS = 1e-6
FFN_RES = 0.5
ATTN_SCALE = HEAD_DIM ** -0.5

ADAM_LR = 0.001
ADAM_B1 = 0.9
ADAM_B2 = 0.999
ADAM_EPS = 1e-08
ADAM_WD = 0.01
ADAM_STEP = 10

V7X_VMEM_LIMIT = 56 * 1024 * 1024
MESH = pl.DeviceIdType.MESH


def _cparams(sem):
    return pltpu.CompilerParams(dimension_semantics=sem, vmem_limit_bytes=V7X_VMEM_LIMIT)


def _dot(a, b):
    return jnp.dot(a, b, preferred_element_type=F32)


def _dot_nt(a, b):
    return lax.dot_general(a, b, (((1,), (1,)), ((), ())), preferred_element_type=F32)


def _dot_tn(a, b):
    return lax.dot_general(a, b, (((0,), (0,)), ((), ())), preferred_element_type=F32)


def _rms(x, g):
    r = lax.rsqrt(jnp.mean(x * x, axis=-1, keepdims=True) + EPS)
    return x * r * g, r


def _rms_bwd(dh, x, r, g):
    gy = dh * g
    dx = r * gy - x * (r * r * r) * jnp.mean(gy * x, axis=-1, keepdims=True)
    dg = jnp.sum(dh * x * r, axis=0, keepdims=True)
    return dx, dg


def _const(shape):
    nd = len(shape)
    return pl.BlockSpec(shape, lambda *_: (0,) * nd)


def _rows(t, w):
    return pl.BlockSpec((t, w), lambda i: (i, 0))


PACK_ROWS = 7 * FF_SH
HALF_ROWS = PACK_ROWS // 2
FFN_HALF = 3 * FF_SH // 2
MIX_HALF = FF_SH // 2
MIX_BLOCK = 6
BIG = ("ffn1_w_gate", "ffn1_w_up", "ffn1_w_down", "ffn2_w_gate", "ffn2_w_up", "ffn2_w_down", "w_in", "w_out")
BIG_ROWS = (FF_SH, FF_SH, FF_SH, FF_SH, FF_SH, FF_SH, IN_SH, OUT_SH)
BIG_TRANSPOSED = (True, True, False, True, True, False, True, False)

SMALL = (("ffn1_norm_g", 1024), ("mix_norm_g", 1024), ("b_in", 1792), ("attn_sinks", 8), ("gmlp_ln_g", 512),
         ("gmlp_ln_b", 512), ("gmlp_w_s", 131072), ("gmlp_b_s", 1024), ("attn_out_norm_g", 512),
         ("gmlp_out_norm_g", 512), ("b_out", 1024), ("ffn2_norm_g", 1024), ("final_norm_g", 1024), ("loss", 1))


def _small_rows(n):
    return -(-n // 1024) * 8


SMALL_ROWS = sum(_small_rows(n) for _, n in SMALL)


def _pack_small(parts):
    out = []
    for name, n in SMALL:
        flat = parts[name].reshape(-1).astype(F32)
        rows = _small_rows(n)
        out.append(jnp.pad(flat, (0, rows * 128 - n)).reshape(rows, 128))
    return jnp.concatenate(out, axis=0)


def _unpack_small(packed, shapes):
    res, off = {}, 0
    for name, n in SMALL:
        rows = _small_rows(n)
        res[name] = packed[off:off + rows].reshape(-1)[:n].reshape(shapes[name])
        off += rows
    return res


def _ffn_tile(x, g, wg_ref, wu_ref, wd_ref, hb_ref, a_ref, b_ref):
    h, _ = _rms(x, g)
    hb = h.astype(BF16)
    hb_ref[...] = hb
    acc = jnp.zeros(x.shape, F32)
    for j in range(N_CHIPS):
        a = _dot_nt(hb, wg_ref[j])
        b = _dot_nt(hb, wu_ref[j])
        a_ref[j] = a
        b_ref[j] = b
        f = (a * jax.nn.sigmoid(a) * b).astype(BF16)
        acc = acc + _dot(f, wd_ref[j])
    return x + FFN_RES * acc


def _ffn_saved_specs(s, tile):
    ab = pl.BlockSpec((N_CHIPS, tile, FF_SH), lambda i: (0, i, 0))
    shape = jax.ShapeDtypeStruct((N_CHIPS, s, FF_SH), F32)
    return [_rows(tile, D_MODEL), ab, ab], [jax.ShapeDtypeStruct((s, D_MODEL), BF16), shape, shape]


def _ffn_weight_specs(k0):
    one = pl.Buffered(1)
    return [pl.BlockSpec((N_CHIPS, FF_SH, D_MODEL), functools.partial(lambda kk, i: (0, kk, 0), k0 + d),
                         pipeline_mode=one) for d in range(3)]


def _ffn_fwd(x, g, pack, k0, *, tile, name):
    s = x.shape[0]

    def body(x_ref, g_ref, wg_ref, wu_ref, wd_ref, o_ref, hb_ref, a_ref, b_ref):
        o_ref[...] = _ffn_tile(x_ref[...], g_ref[...], wg_ref, wu_ref, wd_ref, hb_ref, a_ref, b_ref)

    saved_specs, saved_shapes = _ffn_saved_specs(s, tile)
    return pl.pallas_call(
        body, name=name, grid=(s // tile,),
        in_specs=[_rows(tile, D_MODEL), _const((1, D_MODEL))] + _ffn_weight_specs(k0),
        out_specs=[_rows(tile, D_MODEL)] + saved_specs,
        out_shape=[jax.ShapeDtypeStruct(x.shape, F32)] + saved_shapes,
        compiler_params=_cparams(("arbitrary",)),
    )(x, g, pack, pack, pack)


def _ffn_fwd_loss(x, g, pack, k0, gf, tgt, *, tile, name):
    s = x.shape[0]

    def body(x_ref, g_ref, wg_ref, wu_ref, wd_ref, gf_ref, t_ref, dx_ref, loss_ref, dgf_ref, hb_ref, a_ref, b_ref):
        @pl.when(pl.program_id(0) == 0)
        def _():
            loss_ref[...] = jnp.zeros_like(loss_ref)
            dgf_ref[...] = jnp.zeros_like(dgf_ref)

        x3 = _ffn_tile(x_ref[...], g_ref[...], wg_ref, wu_ref, wd_ref, hb_ref, a_ref, b_ref)
        gf_v = gf_ref[...]
        out, r = _rms(x3, gf_v)
        diff = out - t_ref[...]
        part = jnp.sum(jnp.sum(diff * diff, axis=-1, keepdims=True), axis=0, keepdims=True)
        loss_ref[...] += jnp.broadcast_to(part * (0.5 / D_MODEL), loss_ref.shape)
        dx, dg = _rms_bwd(diff * (1.0 / D_MODEL), x3, r, gf_v)
        dx_ref[...] = dx
        dgf_ref[...] += dg

    saved_specs, saved_shapes = _ffn_saved_specs(s, tile)
    return pl.pallas_call(
        body, name=name, grid=(s // tile,),
        in_specs=[_rows(tile, D_MODEL), _const((1, D_MODEL))] + _ffn_weight_specs(k0)
                 + [_const((1, D_MODEL)), _rows(tile, D_MODEL)],
        out_specs=[_rows(tile, D_MODEL), _const((1, 128)), _const((1, D_MODEL))] + saved_specs,
        out_shape=[jax.ShapeDtypeStruct(x.shape, F32),
                   jax.ShapeDtypeStruct((1, 128), F32),
                   jax.ShapeDtypeStruct((1, D_MODEL), F32)] + saved_shapes,
        compiler_params=_cparams(("arbitrary",)),
    )(x, g, pack, pack, pack, gf, tgt)


def _ffn_bwd(place, hb, a, b, dy, pack, k0, land, mix_grads, *, tile, name):
    s = hb.shape[0]
    nt = s // tile
    land_rows = pl.ds((k0 // 3) * FFN_HALF, FFN_HALF)
    mix_rows = pl.ds(2 * FFN_HALF, MIX_HALF)
    with_mix = mix_grads is not None
    with_land = land is not None
    n_others = 2 * N_CHIPS - 1

    def body(place_ref, hb_ref, a_ref, b_ref, dy_ref, wg_ref, wu_ref, wd_ref, *rest):
        rest = list(rest)
        mix_ref = rest.pop(0) if with_mix else None
        if with_land:
            rest.pop(0)
        dhp_ref, land_ref, acc_ref, stage_ref, send_sems, recv_sem, local_sem = rest[:7]
        t, i = pl.program_id(0), pl.program_id(1)
        xi, yi, c = lax.axis_index("x"), lax.axis_index("y"), lax.axis_index("c")
        dev = 4 * xi + 2 * yi + c
        tt = (t + 1) % N_CHIPS
        tx, ty = jnp.bitwise_xor(xi, tt // 2), jnp.bitwise_xor(yi, tt % 2)

        def remote(src, dst, ssem, rsem, to):
            return pltpu.make_async_remote_copy(src_ref=src, dst_ref=dst, send_sem=ssem, recv_sem=rsem,
                                                device_id=to, device_id_type=MESH)

        def stage_half(h):
            return stage_ref.at[pl.ds(pl.multiple_of(h * FFN_HALF, 16), FFN_HALF)]

        if with_mix:
            mix_send, mix_recv, mix_local = rest[7:10]

            @pl.when(jnp.logical_and(t == 0, i == 0))
            def _():
                for chip in range(N_CHIPS):
                    for h in range(2):
                        src = mix_ref.at[chip, pl.ds(h * MIX_HALF, MIX_HALF)]
                        dst = land_ref.at[dev, mix_rows]
                        mine = jnp.logical_and(2 * xi + yi == chip, c == h)

                        @pl.when(mine)
                        def _():
                            pltpu.make_async_copy(src, dst, mix_local).start()

                        @pl.when(jnp.logical_not(mine))
                        def _():
                            remote(src, dst, mix_send, mix_recv, (chip // 2, chip % 2, h)).start()

        @pl.when(i == 0)
        def _():
            acc_ref[...] = jnp.zeros_like(acc_ref)

        hb = hb_ref[...]
        dob = (FFN_RES * dy_ref[...]).astype(BF16)
        wg_j, wu_j, wd_j = wg_ref[0], wu_ref[0], wd_ref[0]
        a = a_ref[0]
        b = b_ref[0]
        sg = jax.nn.sigmoid(a)
        sa = a * sg
        fb = (sa * b).astype(BF16)
        df = _dot_nt(dob, wd_j)
        dbb = (df * sa).astype(BF16)
        dab = (df * b * (sg + sa * (1.0 - sg))).astype(BF16)
        dhp_ref[0] = _dot(dab, wg_j) + _dot(dbb, wu_j)
        acc_ref[0:FF_SH, :] += _dot_tn(dab, hb)
        acc_ref[FF_SH:2 * FF_SH, :] += _dot_tn(dbb, hb)
        acc_ref[2 * FF_SH:3 * FF_SH, :] += _dot_tn(fb, dob)

        @pl.when(i == nt - 1)
        def _():
            dst = land_ref.at[dev, land_rows]

            @pl.when(t > 0)
            def _():
                for h in range(2):
                    remote(stage_half(h), dst, send_sems.at[h], recv_sem, (tx, ty, h)).wait_send()

            def cast_rows(r, carry):
                rows = pl.ds(pl.multiple_of(r * MIX_HALF, 16), MIX_HALF)
                stage_ref[rows, :] = acc_ref[rows, :].astype(BF16)
                return carry

            lax.fori_loop(0, 3 * FF_SH // MIX_HALF, cast_rows, 0)

            @pl.when(t < N_CHIPS - 1)
            def _():
                for h in range(2):
                    remote(stage_half(h), dst, send_sems.at[h], recv_sem, (tx, ty, h)).start()

            @pl.when(t == N_CHIPS - 1)
            def _():
                own = pltpu.make_async_copy(stage_half(c), dst, local_sem)
                own.start()
                sib = remote(stage_half(1 - c), dst, send_sems.at[0], recv_sem, (xi, yi, 1 - c))
                sib.start()
                sib.wait_send()
                own.wait()
                arrivals = land_ref.at[pl.ds(0, n_others), land_rows]
                remote(arrivals, arrivals, send_sems.at[0], recv_sem, (xi, yi, 1 - c)).wait_recv()
                if with_mix:
                    seven = land_ref.at[pl.ds(0, n_others), mix_rows]
                    both = remote(seven, seven, mix_send, mix_recv, (xi, yi, 1 - c))
                    both.wait_send()
                    both.wait_recv()
                    pltpu.make_async_copy(mix_ref.at[0, pl.ds(0, MIX_HALF)], land_ref.at[dev, mix_rows],
                                          mix_local).wait()

    def wspec(kk):
        return pl.BlockSpec((1, FF_SH, D_MODEL),
                            lambda t, i, pr: (jnp.bitwise_xor(pr[0], (t + 1) % N_CHIPS), kk, 0))

    xspec = pl.BlockSpec((tile, D_MODEL), lambda t, i, pr: (i, 0))
    abspec = pl.BlockSpec((1, tile, FF_SH), lambda t, i, pr: (jnp.bitwise_xor(pr[0], (t + 1) % N_CHIPS), i, 0))
    hbm = pl.BlockSpec(memory_space=pl.ANY)
    in_specs = [xspec, abspec, abspec, xspec, wspec(k0), wspec(k0 + 1), wspec(k0 + 2)]
    operands = [place, hb, a, b, dy, pack, pack, pack]
    scratch = [pltpu.VMEM((3 * FF_SH, D_MODEL), F32), pltpu.VMEM((3 * FF_SH, D_MODEL), BF16),
               pltpu.SemaphoreType.DMA((2,)), pltpu.SemaphoreType.DMA, pltpu.SemaphoreType.DMA]
    if with_mix:
        in_specs.append(hbm)
        operands.append(mix_grads)
        scratch += [pltpu.SemaphoreType.DMA, pltpu.SemaphoreType.DMA, pltpu.SemaphoreType.DMA]
    aliases = {}
    if with_land:
        in_specs.append(hbm)
        operands.append(land)
        aliases = {len(operands) - 1: 1}
    grid_spec = pltpu.PrefetchScalarGridSpec(
        num_scalar_prefetch=1, grid=(N_CHIPS, nt), in_specs=in_specs,
        out_specs=[pl.BlockSpec((1, tile, D_MODEL), lambda t, i, pr: (t, i, 0)), hbm],
        scratch_shapes=scratch)
    return pl.pallas_call(
        body, name=name, grid_spec=grid_spec,
        out_shape=[jax.ShapeDtypeStruct((N_CHIPS, s, D_MODEL), F32),
                   jax.ShapeDtypeStruct((2 * N_CHIPS, HALF_ROWS, D_MODEL), BF16)],
        input_output_aliases=aliases,
        compiler_params=_cparams(("arbitrary", "arbitrary")),
    )(*operands)


def _mix_grads_pack(dw_in_t, dw_out, *, name):
    def body(a_ref, b_ref, o_ref):
        o_ref[0, 0:IN_SH, :] = a_ref[0].astype(BF16)
        o_ref[0, IN_SH:FF_SH, :] = b_ref[0].astype(BF16)

    return pl.pallas_call(
        body, name=name, grid=(N_CHIPS,),
        in_specs=[pl.BlockSpec((1, IN_SH, D_MODEL), lambda j: (j, 0, 0)),
                  pl.BlockSpec((1, OUT_SH, D_MODEL), lambda j: (j, 0, 0))],
        out_specs=pl.BlockSpec((1, FF_SH, D_MODEL), lambda j: (j, 0, 0)),
        out_shape=jax.ShapeDtypeStruct((N_CHIPS, FF_SH, D_MODEL), BF16),
        compiler_params=_cparams(("arbitrary",)),
    )(dw_in_t.reshape(N_CHIPS, IN_SH, D_MODEL), dw_out.reshape(N_CHIPS, OUT_SH, D_MODEL))


def _norm_bwd(dhp, x, dy, g, *, tile, name):
    s = x.shape[0]

    def body(dhp_ref, x_ref, dy_ref, g_ref, dx_ref, dg_ref):
        @pl.when(pl.program_id(0) == 0)
        def _():
            dg_ref[...] = jnp.zeros_like(dg_ref)

        dh = (dhp_ref[0] + dhp_ref[1]) + (dhp_ref[2] + dhp_ref[3])
        x_v = x_ref[...]
        r = lax.rsqrt(jnp.mean(x_v * x_v, axis=-1, keepdims=True) + EPS)
        dx, dg = _rms_bwd(dh, x_v, r, g_ref[...])
        dx_ref[...] = dy_ref[...] + dx
        dg_ref[...] += dg

    return pl.pallas_call(
        body, name=name, grid=(s // tile,),
        in_specs=[pl.BlockSpec((N_CHIPS, tile, D_MODEL), lambda i: (0, i, 0)),
                  _rows(tile, D_MODEL), _rows(tile, D_MODEL), _const((1, D_MODEL))],
        out_specs=[_rows(tile, D_MODEL), _const((1, D_MODEL))],
        out_shape=[jax.ShapeDtypeStruct(x.shape, F32), jax.ShapeDtypeStruct((1, D_MODEL), F32)],
        compiler_params=_cparams(("arbitrary",)),
    )(dhp, x, dy, g)


def _mix_in_fwd(x, g, w_in_t, b_in, *, tile, name):
    s = x.shape[0]

    def body(x_ref, g_ref, w_ref, b_ref, q_ref, k_ref, v_ref, z_ref):
        h, _ = _rms(x_ref[...], g_ref[...])
        proj = _dot_nt(h.astype(BF16), w_ref[...]) + b_ref[...]
        q_ref[...] = proj[:, :ATTN_W].astype(BF16)
        k_ref[...] = proj[:, ATTN_W:ATTN_W + KV_W].astype(BF16)
        v_ref[...] = proj[:, ATTN_W + KV_W:ATTN_W + 2 * KV_W].astype(BF16)
        z_ref[...] = proj[:, ATTN_W + 2 * KV_W:]

    return pl.pallas_call(
        body, name=name, grid=(s // tile,),
        in_specs=[_rows(tile, D_MODEL), _const((1, D_MODEL)), _const((IN_W, D_MODEL)), _const((1, IN_W))],
        out_specs=[_rows(tile, ATTN_W), _rows(tile, KV_W), _rows(tile, KV_W), _rows(tile, 2 * GMLP_W)],
        out_shape=[jax.ShapeDtypeStruct((s, ATTN_W), BF16), jax.ShapeDtypeStruct((s, KV_W), BF16),
                   jax.ShapeDtypeStruct((s, KV_W), BF16), jax.ShapeDtypeStruct((s, 2 * GMLP_W), F32)],
        compiler_params=_cparams(("arbitrary",)),
    )(x, g, w_in_t, b_in)


def _mix_in_bwd(x, dy, dq, dk, dv, dz, g, w_in_t, *, tile, name):
    s = x.shape[0]

    def body(x_ref, dy_ref, dq_ref, dk_ref, dv_ref, dz_ref, g_ref, w_ref, dx_ref, dw_ref, db_ref, dg_ref):
        @pl.when(pl.program_id(0) == 0)
        def _():
            dw_ref[...] = jnp.zeros_like(dw_ref)
            db_ref[...] = jnp.zeros_like(db_ref)
            dg_ref[...] = jnp.zeros_like(dg_ref)

        dproj = jnp.concatenate([dq_ref[...], dk_ref[...], dv_ref[...], dz_ref[...]], axis=-1)
        db_ref[...] += jnp.sum(dproj, axis=0, keepdims=True)
        dpb = dproj.astype(BF16)
        x_v = x_ref[...]
        g_v = g_ref[...]
        h, r = _rms(x_v, g_v)
        dw_ref[...] += _dot_tn(dpb, h.astype(BF16))
        dh = _dot(dpb, w_ref[...])
        dx, dg = _rms_bwd(dh, x_v, r, g_v)
        dx_ref[...] = dy_ref[...] + dx
        dg_ref[...] += dg

    return pl.pallas_call(
        body, name=name, grid=(s // tile,),
        in_specs=[_rows(tile, D_MODEL), _rows(tile, D_MODEL), _rows(tile, ATTN_W), _rows(tile, KV_W),
                  _rows(tile, KV_W), _rows(tile, 2 * GMLP_W), _const((1, D_MODEL)), _const((IN_W, D_MODEL))],
        out_specs=[_rows(tile, D_MODEL), _const((IN_W, D_MODEL)), _const((1, IN_W)), _const((1, D_MODEL))],
        out_shape=[jax.ShapeDtypeStruct(x.shape, F32), jax.ShapeDtypeStruct((IN_W, D_MODEL), F32),
                   jax.ShapeDtypeStruct((1, IN_W), F32), jax.ShapeDtypeStruct((1, D_MODEL), F32)],
        compiler_params=_cparams(("arbitrary",)),
    )(x, dy, dq, dk, dv, dz, g, w_in_t)


_GELU_C = 0.7978845608028654
_GELU_A = 0.044715


def _gelu(x):
    return 0.5 * x * (1.0 + jnp.tanh(_GELU_C * (x + _GELU_A * (x * x * x))))


def _gelu_grad(x):
    t = jnp.tanh(_GELU_C * (x + _GELU_A * (x * x * x)))
    return 0.5 * (1.0 + t) + 0.5 * x * (1.0 - t * t) * (_GELU_C * (1.0 + 3.0 * _GELU_A * (x * x)))


def _band(ref, i):
    prev = jnp.maximum(i - 1, 0)
    return jnp.concatenate([ref[pl.ds(pl.multiple_of(prev * BLK, BLK), BLK), :],
                            ref[pl.ds(pl.multiple_of(i * BLK, BLK), BLK), :]], axis=0)


def _band_mask(i):
    qpos = lax.broadcasted_iota(jnp.int32, (BLK, 2 * BLK), 0)
    kidx = lax.broadcasted_iota(jnp.int32, (BLK, 2 * BLK), 1)
    rel = qpos - kidx + BLK
    win = jnp.where(rel >= 0, jnp.where(rel < BLK, 1, 0), 0)
    real = jnp.where(kidx >= BLK, 1, jnp.where(i > 0, 1, 0))
    return (win * real) > 0


def _attn_probs(qh, kg, mask, sink):
    sc = _dot_nt(qh, kg) * ATTN_SCALE
    sc = jnp.where(mask, sc, -jnp.inf)
    m = jnp.maximum(jnp.max(sc, axis=-1, keepdims=True), sink)
    p = jnp.exp(sc - m)
    es = jnp.exp(sink - m)
    inv = 1.0 / (jnp.sum(p, axis=-1, keepdims=True) + es)
    return p * inv, es * inv


def _tril_mask():
    t = lax.broadcasted_iota(jnp.int32, (BLK, BLK), 0)
    s_ = lax.broadcasted_iota(jnp.int32, (BLK, BLK), 1)
    return s_ <= t


def _gmlp_fwd_parts(zg, lng, lnb, ws_ref, bs_full):
    z = _gelu(zg)
    u = z[:, :GMLP_W]
    zv = z[:, GMLP_W:]
    mu = jnp.mean(zv, axis=-1, keepdims=True)
    zc = zv - mu
    rstd = lax.rsqrt(jnp.mean(zc * zc, axis=-1, keepdims=True) + EPS)
    xh = zc * rstd
    vvb = (xh * lng + lnb).astype(BF16)
    tril = _tril_mask()
    wms, parts = [], []
    for gi in range(GMLP_GROUPS):
        wm = jnp.where(tril, ws_ref[gi], 0.0).astype(BF16)
        wms.append(wm)
        parts.append(_dot(wm, vvb[:, gi * GROUP_DIM:(gi + 1) * GROUP_DIM]))
    mixed = jnp.concatenate(parts, axis=-1) + bs_full
    return u, xh, rstd, vvb, wms, mixed


def _attn_fwd(q, kb, vb, mask, sink_ref):
    outs = []
    for h in range(N_Q_HEADS):
        gi = h // REP
        pn, _ = _attn_probs(q[:, h * HEAD_DIM:(h + 1) * HEAD_DIM], kb[:, gi * HEAD_DIM:(gi + 1) * HEAD_DIM],
                            mask, sink_ref[h])
        outs.append(_dot(pn.astype(BF16), vb[:, gi * HEAD_DIM:(gi + 1) * HEAD_DIM]))
    return jnp.concatenate(outs, axis=-1)


def _mix_core_fwd(q, k, v, zg, sinks, lng, lnb, w_s, bs_full, gao, ggo, *, name):
    s = q.shape[0]

    def body(sink_ref, q_ref, k_ref, v_ref, z_ref, lng_ref, lnb_ref, ws_ref, bs_ref, gao_ref, ggo_ref, o_ref):
        i = pl.program_id(0)
        mask = _band_mask(i)
        y_attn = _attn_fwd(q_ref[...], _band(k_ref, i), _band(v_ref, i), mask, sink_ref)
        u, _, _, _, _, mixed = _gmlp_fwd_parts(z_ref[...], lng_ref[...], lnb_ref[...], ws_ref, bs_ref[...])
        ya, _ = _rms(y_attn, gao_ref[...])
        yg, _ = _rms(u * mixed, ggo_ref[...])
        o_ref[...] = jnp.concatenate([ya, yg], axis=-1).astype(BF16)

    return pl.pallas_call(
        body, name=name, grid=(s // BLK,),
        in_specs=[pl.BlockSpec(memory_space=pltpu.SMEM),
                  _rows(BLK, ATTN_W), _const((s, KV_W)), _const((s, KV_W)),
                  _rows(BLK, 2 * GMLP_W), _const((1, GMLP_W)), _const((1, GMLP_W)),
                  _const((GMLP_GROUPS, BLK, BLK)), _const((BLK, GMLP_W)), _const((1, ATTN_W)), _const((1, GMLP_W))],
        out_specs=_rows(BLK, D_MODEL),
        out_shape=jax.ShapeDtypeStruct((s, D_MODEL), BF16),
        compiler_params=_cparams(("arbitrary",)),
    )(sinks, q, k, v, zg, lng, lnb, w_s, bs_full, gao, ggo)


def _mix_out_fwd(x1, yb, w_out, b_out, *, tile, name):
    s = x1.shape[0]

    def body(x_ref, y_ref, w_ref, b_ref, o_ref):
        o_ref[...] = x_ref[...] + (_dot(y_ref[...], w_ref[...]) + b_ref[...])

    return pl.pallas_call(
        body, name=name, grid=(s // tile,),
        in_specs=[_rows(tile, D_MODEL), _rows(tile, D_MODEL), _const((D_MODEL, D_MODEL)), _const((1, D_MODEL))],
        out_specs=_rows(tile, D_MODEL),
        out_shape=jax.ShapeDtypeStruct(x1.shape, F32),
        compiler_params=_cparams(("arbitrary",)),
    )(x1, yb, w_out, b_out)


def _norm_bwd_mix_out(dhp, x, dy, g, yb, w_out, *, tile, name):
    s = x.shape[0]

    def body(dhp_ref, x_ref, dy_ref, g_ref, y_ref, w_ref, dx_ref, dg_ref, dyy_ref, dw_ref, db_ref):
        @pl.when(pl.program_id(0) == 0)
        def _():
            dg_ref[...] = jnp.zeros_like(dg_ref)
            dw_ref[...] = jnp.zeros_like(dw_ref)
            db_ref[...] = jnp.zeros_like(db_ref)

        dh = (dhp_ref[0] + dhp_ref[1]) + (dhp_ref[2] + dhp_ref[3])
        x_v = x_ref[...]
        r = lax.rsqrt(jnp.mean(x_v * x_v, axis=-1, keepdims=True) + EPS)
        dxn, dg = _rms_bwd(dh, x_v, r, g_ref[...])
        dx = dy_ref[...] + dxn
        dx_ref[...] = dx
        dg_ref[...] += dg
        dxb = dx.astype(BF16)
        db_ref[...] += jnp.sum(dx, axis=0, keepdims=True)
        dw_ref[...] += _dot_tn(y_ref[...], dxb)
        dyy_ref[...] = _dot_nt(dxb, w_ref[...])

    return pl.pallas_call(
        body, name=name, grid=(s // tile,),
        in_specs=[pl.BlockSpec((N_CHIPS, tile, D_MODEL), lambda i: (0, i, 0)),
                  _rows(tile, D_MODEL), _rows(tile, D_MODEL), _const((1, D_MODEL)), _rows(tile, D_MODEL),
                  _const((D_MODEL, D_MODEL))],
        out_specs=[_rows(tile, D_MODEL), _const((1, D_MODEL)), _rows(tile, D_MODEL), _const((D_MODEL, D_MODEL)),
                   _const((1, D_MODEL))],
        out_shape=[jax.ShapeDtypeStruct(x.shape, F32), jax.ShapeDtypeStruct((1, D_MODEL), F32),
                   jax.ShapeDtypeStruct(x.shape, F32), jax.ShapeDtypeStruct((D_MODEL, D_MODEL), F32),
                   jax.ShapeDtypeStruct((1, D_MODEL), F32)],
        compiler_params=_cparams(("arbitrary",)),
    )(dhp, x, dy, g, yb, w_out)


def _mix_core_bwd(dyy, q, k, v, zg, sinks, lng, lnb, w_s, bs_full, gao, ggo, *, name):
    s = dyy.shape[0]
    nblk = s // BLK

    def body(sink_ref, dyy_ref, q_ref, k_ref, v_ref, z_ref, lng_ref, lnb_ref, ws_ref, bs_ref, gao_ref, ggo_ref,
             dq_ref, dk_ref, dv_ref, dz_ref, dgao_ref, dggo_ref, dlng_ref, dlnb_ref, dws_ref, dms_ref, dsk_ref):
        i = pl.program_id(0)

        @pl.when(i == 0)
        def _():
            for ref in (dk_ref, dv_ref, dgao_ref, dggo_ref, dlng_ref, dlnb_ref, dws_ref, dms_ref, dsk_ref):
                ref[...] = jnp.zeros_like(ref)

        mask = _band_mask(i)
        q_v = q_ref[...]
        kb = _band(k_ref, i)
        vb = _band(v_ref, i)
        lng_v = lng_ref[...]
        gao_v = gao_ref[...]
        ggo_v = ggo_ref[...]
        zg_v = z_ref[...]

        y_attn = _attn_fwd(q_v, kb, vb, mask, sink_ref)
        u, xh, rstd, vvb, wms, mixed = _gmlp_fwd_parts(zg_v, lng_v, lnb_ref[...], ws_ref, bs_ref[...])
        y_gmlp = u * mixed
        ra = lax.rsqrt(jnp.mean(y_attn * y_attn, axis=-1, keepdims=True) + EPS)
        rg = lax.rsqrt(jnp.mean(y_gmlp * y_gmlp, axis=-1, keepdims=True) + EPS)

        dyy = dyy_ref[...]
        d_attn, dgao = _rms_bwd(dyy[:, :ATTN_W], y_attn, ra, gao_v)
        d_gmlp, dggo = _rms_bwd(dyy[:, ATTN_W:], y_gmlp, rg, ggo_v)
        dgao_ref[...] += dgao
        dggo_ref[...] += dggo

        du = d_gmlp * mixed
        dmixed = d_gmlp * u
        dms_ref[...] += dmixed
        dmb = dmixed.astype(BF16)
        dvv_parts = []
        for gi in range(GMLP_GROUPS):
            sl = slice(gi * GROUP_DIM, (gi + 1) * GROUP_DIM)
            dws_ref[gi] += _dot_nt(dmb[:, sl], vvb[:, sl])
            dvv_parts.append(_dot_tn(wms[gi], dmb[:, sl]))
        dvv = jnp.concatenate(dvv_parts, axis=-1)
        dlng_ref[...] += jnp.sum(dvv * xh, axis=0, keepdims=True)
        dlnb_ref[...] += jnp.sum(dvv, axis=0, keepdims=True)
        dxh = dvv * lng_v
        dzv = rstd * (dxh - jnp.mean(dxh, axis=-1, keepdims=True)
                      - xh * jnp.mean(dxh * xh, axis=-1, keepdims=True))
        dz_ref[...] = jnp.concatenate([du, dzv], axis=-1) * _gelu_grad(zg_v)

        dab = d_attn.astype(BF16)
        dq_parts = []
        dk_parts = []
        dv_parts = []
        for gi in range(N_KV_HEADS):
            kg = kb[:, gi * HEAD_DIM:(gi + 1) * HEAD_DIM]
            vg = vb[:, gi * HEAD_DIM:(gi + 1) * HEAD_DIM]
            dkg = jnp.zeros((2 * BLK, HEAD_DIM), F32)
            dvg = jnp.zeros((2 * BLK, HEAD_DIM), F32)
            for rr in range(REP):
                h = gi * REP + rr
                hs = slice(h * HEAD_DIM, (h + 1) * HEAD_DIM)
                qh = q_v[:, hs]
                doh = dab[:, hs]
                pn, psink = _attn_probs(qh, kg, mask, sink_ref[h])
                dp = _dot_nt(doh, vg)
                delta = jnp.sum(pn * dp, axis=-1, keepdims=True)
                dsb = (pn * (dp - delta) * ATTN_SCALE).astype(BF16)
                dsink = jnp.sum(-psink * delta, axis=0, keepdims=True)
                dsk_ref[pl.ds(h, 1), :] += jnp.broadcast_to(dsink, (1, 128))
                dq_parts.append(_dot(dsb, kg))
                dkg = dkg + _dot_tn(dsb, qh)
                dvg = dvg + _dot_tn(pn.astype(BF16), doh)
            dk_parts.append(dkg)
            dv_parts.append(dvg)
        dq_ref[...] = jnp.concatenate(dq_parts, axis=-1)
        dkb = jnp.concatenate(dk_parts, axis=-1)
        dvb = jnp.concatenate(dv_parts, axis=-1)
        prev = pl.ds(pl.multiple_of(jnp.maximum(i - 1, 0) * BLK, BLK), BLK)
        cur = pl.ds(pl.multiple_of(i * BLK, BLK), BLK)
        dk_ref[prev, :] += dkb[:BLK]
        dv_ref[prev, :] += dvb[:BLK]
        dk_ref[cur, :] += dkb[BLK:]
        dv_ref[cur, :] += dvb[BLK:]

        @pl.when(i == nblk - 1)
        def _():
            tril = _tril_mask()
            for gi in range(GMLP_GROUPS):
                dws_ref[gi] = jnp.where(tril, dws_ref[gi], 0.0)

    return pl.pallas_call(
        body, name=name, grid=(nblk,),
        in_specs=[pl.BlockSpec(memory_space=pltpu.SMEM),
                  _rows(BLK, D_MODEL), _rows(BLK, ATTN_W), _const((s, KV_W)), _const((s, KV_W)),
                  _rows(BLK, 2 * GMLP_W), _const((1, GMLP_W)), _const((1, GMLP_W)),
                  _const((GMLP_GROUPS, BLK, BLK)), _const((BLK, GMLP_W)), _const((1, ATTN_W)), _const((1, GMLP_W))],
        out_specs=[_rows(BLK, ATTN_W), _const((s, KV_W)), _const((s, KV_W)), _rows(BLK, 2 * GMLP_W),
                   _const((1, ATTN_W)), _const((1, GMLP_W)),
                   _const((1, GMLP_W)), _const((1, GMLP_W)), _const((GMLP_GROUPS, BLK, BLK)),
                   _const((BLK, GMLP_W)), _const((N_Q_HEADS, 128))],
        out_shape=[jax.ShapeDtypeStruct((s, ATTN_W), F32), jax.ShapeDtypeStruct((s, KV_W), F32),
                   jax.ShapeDtypeStruct((s, KV_W), F32), jax.ShapeDtypeStruct((s, 2 * GMLP_W), F32),
                   jax.ShapeDtypeStruct((1, ATTN_W), F32), jax.ShapeDtypeStruct((1, GMLP_W), F32),
                   jax.ShapeDtypeStruct((1, GMLP_W), F32), jax.ShapeDtypeStruct((1, GMLP_W), F32),
                   jax.ShapeDtypeStruct((GMLP_GROUPS, BLK, BLK), F32), jax.ShapeDtypeStruct((BLK, GMLP_W), F32),
                   jax.ShapeDtypeStruct((N_Q_HEADS, 128), F32)],
        compiler_params=_cparams(("arbitrary",)),
    )(sinks, dyy, q, k, v, zg, lng, lnb, w_s, bs_full, gao, ggo)


def _local_step(place, x, tgt, p, pack, *, tile=512, fwd_tile=256, bwd_tile=512, norm_tile=256):
    g = {}
    tile, fwd_tile, bwd_tile, norm_tile = (min(t_, x.shape[0]) for t_ in (tile, fwd_tile, bwd_tile, norm_tile))
    x1, hb1, a1, b1 = _ffn_fwd(x, p["ffn1_norm_g"], pack, 0, tile=fwd_tile, name="ffn1_fwd")
    q, k, v, zg = _mix_in_fwd(x1, p["mix_norm_g"], p["w_in_t"], p["b_in"], tile=tile, name="mix_in_fwd")
    mix_args = (q, k, v, zg, p["attn_sinks"], p["gmlp_ln_g"], p["gmlp_ln_b"], p["gmlp_w_s"], p["bs_full"],
                p["attn_out_norm_g"], p["gmlp_out_norm_g"])
    yb = _mix_core_fwd(*mix_args, name="mix_core_fwd")
    x2 = _mix_out_fwd(x1, yb, p["w_out"], p["b_out"], tile=tile, name="mix_out_fwd")
    dx3, loss, g["final_norm_g"], hb2, a2, b2 = _ffn_fwd_loss(
        x2, p["ffn2_norm_g"], pack, 3, p["final_norm_g"], tgt, tile=fwd_tile, name="ffn2_fwd_loss")

    dhp, land = _ffn_bwd(place, hb2, a2, b2, dx3, pack, 3, None, None, tile=bwd_tile, name="ffn2_bwd")
    dx2, g["ffn2_norm_g"], dyy, dw_out, g["b_out"] = _norm_bwd_mix_out(
        dhp, x2, dx3, p["ffn2_norm_g"], yb, p["w_out"], tile=norm_tile, name="ffn2_norm_bwd")

    (dq, dk, dv, dz, g["attn_out_norm_g"], g["gmlp_out_norm_g"], g["gmlp_ln_g"],
     g["gmlp_ln_b"], g["gmlp_w_s"], dmix_sum, dsinks) = _mix_core_bwd(dyy, *mix_args, name="mix_core_bwd")
    g["gmlp_b_s"] = dmix_sum
    g["attn_sinks"] = dsinks
    dx1, dw_in_t, g["b_in"], g["mix_norm_g"] = _mix_in_bwd(
        x1, dx2, dq, dk, dv, dz, p["mix_norm_g"], p["w_in_t"], tile=tile, name="mix_in_bwd")
    mix_grads = _mix_grads_pack(dw_in_t, dw_out, name="mix_grads_pack")

    dhp1, land = _ffn_bwd(place, hb1, a1, b1, dx1, pack, 0, land, mix_grads, tile=bwd_tile, name="ffn1_bwd")
    dx0, g["ffn1_norm_g"] = _norm_bwd(dhp1, x, dx1, p["ffn1_norm_g"], tile=norm_tile, name="ffn1_norm_bwd")
    return loss, dx0, land, g


def _pack_cast(place, parts, *, name):
    def body(place_ref, *refs):
        o_ref = refs[-1]
        off = 0
        for ref, rows in zip(refs[:-1], BIG_ROWS):
            o_ref[0, off:off + rows, :] = ref[...].astype(BF16)
            off += rows

    one = pl.Buffered(1)
    grid_spec = pltpu.PrefetchScalarGridSpec(
        num_scalar_prefetch=1, grid=(1,),
        in_specs=[pl.BlockSpec((rows, D_MODEL), lambda i, pr: (0, 0), pipeline_mode=one) for rows in BIG_ROWS],
        out_specs=pl.BlockSpec((1, PACK_ROWS, D_MODEL), lambda i, pr: (pr[0], 0, 0), pipeline_mode=one))
    return pl.pallas_call(
        body, name=name, grid_spec=grid_spec,
        out_shape=jax.ShapeDtypeStruct((N_CHIPS, PACK_ROWS, D_MODEL), BF16),
        compiler_params=_cparams(("arbitrary",)),
    )(place, *parts)


def _mesh_place():
    x, y, c = lax.axis_index("x"), lax.axis_index("y"), lax.axis_index("c")
    others = [(1 - x, y), (x, 1 - y), (1 - x, 1 - y)]
    return x, y, c, others


def _half(c):
    return pl.ds(pl.multiple_of(c * HALF_ROWS, 16), HALF_ROWS)


def _all_gather_pack(pack, *, name):
    def body(p_ref, o_ref, send_sems, recv_sems):
        x, y, c, others = _mesh_place()
        me = 2 * x + y
        sibling = (x, y, 1 - c)
        mine, theirs = _half(c), _half(1 - c)

        def copy(k, src, dst, to):
            return pltpu.make_async_remote_copy(src_ref=src, dst_ref=dst, send_sem=send_sems.at[k],
                                                recv_sem=recv_sems.at[k], device_id=to, device_id_type=MESH)

        first = [copy(j, o_ref.at[me, mine], o_ref.at[me, mine], (px, py, c)) for j, (px, py) in enumerate(others)]
        for cp in first:
            cp.start()
        passed = []
        for j, (px, py) in enumerate(others):
            slab = o_ref.at[2 * px + py, mine]
            copy(j, slab, slab, (px, py, c)).wait_recv()
            fwd = copy(3 + j, slab, slab, sibling)
            fwd.start()
            passed.append(fwd)
        for j, (px, py) in enumerate(others):
            slab = o_ref.at[2 * px + py, theirs]
            copy(3 + j, slab, slab, sibling).wait_recv()
        for cp in first + passed:
            cp.wait_send()

    return pl.pallas_call(
        body, name=name,
        in_specs=[pl.BlockSpec(memory_space=pl.ANY)],
        out_specs=pl.BlockSpec(memory_space=pl.ANY),
        out_shape=jax.ShapeDtypeStruct((N_CHIPS, PACK_ROWS, D_MODEL), BF16),
        input_output_aliases={0: 0},
        scratch_shapes=[pltpu.SemaphoreType.DMA((6,)), pltpu.SemaphoreType.DMA((6,))],
    )(pack)


def _shard_tile(i, c):
    return jnp.where(i < 3, 3 * c + i, jnp.where(i < 6, 3 + 3 * c + i, 12 + c))


def _rs_reduce(place, land, *, name):
    def body(place_ref, l_ref, o_ref):
        acc = l_ref[0].astype(F32)
        for d in range(1, 2 * N_CHIPS):
            acc = acc + l_ref[d].astype(F32)
        o_ref[...] = acc

    grid_spec = pltpu.PrefetchScalarGridSpec(
        num_scalar_prefetch=1, grid=(HALF_ROWS // MIX_HALF,),
        in_specs=[pl.BlockSpec((2 * N_CHIPS, MIX_HALF, D_MODEL), lambda i, pr: (0, i, 0))],
        out_specs=pl.BlockSpec((MIX_HALF, D_MODEL), lambda i, pr: (_shard_tile(i, pr[1]), 0)))
    return pl.pallas_call(
        body, name=name, grid_spec=grid_spec,
        out_shape=jax.ShapeDtypeStruct((PACK_ROWS, D_MODEL), F32),
        compiler_params=_cparams(("arbitrary",)),
    )(place, land)


def _rs_share(shard, *, name):
    def body(s_ref, o_ref, send_sems, recv_sems):
        x, y, c, _ = _mesh_place()

        def rows(k, core):
            if k < 2:
                return o_ref.at[pl.ds(pl.multiple_of(k * 2 * FFN_HALF + core * FFN_HALF, 8), FFN_HALF)]
            return o_ref.at[pl.ds(pl.multiple_of(4 * FFN_HALF + core * MIX_HALF, 8), MIX_HALF)]

        def copy(k, core):
            return pltpu.make_async_remote_copy(src_ref=rows(k, core), dst_ref=rows(k, core), send_sem=send_sems.at[k],
                                                recv_sem=recv_sems.at[k], device_id=(x, y, 1 - c),
                                                device_id_type=MESH)

        sends = [copy(k, c) for k in range(3)]
        for cp in sends:
            cp.start()
        for k in range(3):
            copy(k, 1 - c).wait_recv()
        for cp in sends:
            cp.wait_send()

    return pl.pallas_call(
        body, name=name,
        in_specs=[pl.BlockSpec(memory_space=pl.ANY)],
        out_specs=pl.BlockSpec(memory_space=pl.ANY),
        out_shape=jax.ShapeDtypeStruct((PACK_ROWS, D_MODEL), F32),
        input_output_aliases={0: 0},
        scratch_shapes=[pltpu.SemaphoreType.DMA((3,)), pltpu.SemaphoreType.DMA((3,))],
    )(shard)


def _small_all_reduce(packed, *, name):
    rows = packed.shape[0]

    def body(p_ref, o_ref, sib_ref, slots_ref, send_sems, recv_sems):
        x, y, c, others = _mesh_place()
        me = 2 * x + y
        sib = pltpu.make_async_remote_copy(src_ref=p_ref, dst_ref=sib_ref, send_sem=send_sems.at[0],
                                           recv_sem=recv_sems.at[0], device_id=(x, y, 1 - c), device_id_type=MESH)
        sib.start()
        sib.wait()
        slots_ref[me] = p_ref[...] + sib_ref[...]
        sends = [pltpu.make_async_remote_copy(
            src_ref=slots_ref.at[me], dst_ref=slots_ref.at[me], send_sem=send_sems.at[1 + j],
            recv_sem=recv_sems.at[1 + j], device_id=(px, py, c), device_id_type=MESH)
            for j, (px, py) in enumerate(others)]
        for cp in sends:
            cp.start()
        for j, (px, py) in enumerate(others):
            slab = slots_ref.at[2 * px + py]
            pltpu.make_async_remote_copy(src_ref=slab, dst_ref=slab, send_sem=send_sems.at[1 + j],
                                         recv_sem=recv_sems.at[1 + j], device_id=(px, py, c),
                                         device_id_type=MESH).wait_recv()
        for cp in sends:
            cp.wait_send()
        o_ref[...] = (slots_ref[0] + slots_ref[1]) + (slots_ref[2] + slots_ref[3])

    vm = pl.BlockSpec(memory_space=pltpu.VMEM)
    return pl.pallas_call(
        body, name=name, in_specs=[vm], out_specs=vm,
        out_shape=jax.ShapeDtypeStruct((rows, 128), F32),
        scratch_shapes=[pltpu.VMEM((rows, 128), F32), pltpu.VMEM((N_CHIPS, rows, 128), F32),
                        pltpu.SemaphoreType.DMA((4,)), pltpu.SemaphoreType.DMA((4,))],
    )(packed)


def _adamw(w, g, m, v, *, g_row0, tile, name):
    rows, cols = w.shape
    assert g_row0 % tile == 0 and rows % tile == 0

    def body(w_ref, g_ref, m_ref, v_ref, go_ref, d_ref, nm_ref, nv_ref):
        g_v = g_ref[...]
        m_n = ADAM_B1 * m_ref[...] + (1.0 - ADAM_B1) * g_v
        v_n = ADAM_B2 * v_ref[...] + (1.0 - ADAM_B2) * (g_v * g_v)
        m_hat = m_n / (1.0 - ADAM_B1 ** ADAM_STEP)
        v_hat = v_n / (1.0 - ADAM_B2 ** ADAM_STEP)
        d_ref[...] = -ADAM_LR * (m_hat / (jnp.sqrt(v_hat) + ADAM_EPS) + ADAM_WD * w_ref[...])
        go_ref[...] = g_v
        nm_ref[...] = m_n
        nv_ref[...] = v_n

    spec = pl.BlockSpec((tile, cols), lambda i: (i, 0))
    gspec = pl.BlockSpec((tile, cols), lambda i: (g_row0 // tile + i, 0))
    shape = jax.ShapeDtypeStruct((rows, cols), F32)
    return pl.pallas_call(
        body, name=name, grid=(rows // tile,),
        in_specs=[spec, gspec, spec, spec], out_specs=[spec] * 4, out_shape=[shape] * 4,
        compiler_params=_cparams(("arbitrary",)),
    )(w, g, m, v)


def kernel(x, ffn1_norm_g, ffn1_w_gate, ffn1_w_up, ffn1_w_down, mix_norm_g, w_in, b_in, attn_sinks, gmlp_ln_g, gmlp_ln_b, gmlp_w_s, gmlp_b_s, attn_out_norm_g, gmlp_out_norm_g, w_out, b_out, ffn2_norm_g, ffn2_w_gate, ffn2_w_up, ffn2_w_down, final_norm_g, loss_target, m_ffn1_norm_g, m_ffn1_w_gate, m_ffn1_w_up, m_ffn1_w_down, m_mix_norm_g, m_w_in, m_b_in, m_attn_sinks, m_gmlp_ln_g, m_gmlp_ln_b, m_gmlp_w_s, m_gmlp_b_s, m_attn_out_norm_g, m_gmlp_out_norm_g, m_w_out, m_b_out, m_ffn2_norm_g, m_ffn2_w_gate, m_ffn2_w_up, m_ffn2_w_down, m_final_norm_g, v_ffn1_norm_g, v_ffn1_w_gate, v_ffn1_w_up, v_ffn1_w_down, v_mix_norm_g, v_w_in, v_b_in, v_attn_sinks, v_gmlp_ln_g, v_gmlp_ln_b, v_gmlp_w_s, v_gmlp_b_s, v_attn_out_norm_g, v_gmlp_out_norm_g, v_w_out, v_b_out, v_ffn2_norm_g, v_ffn2_w_gate, v_ffn2_w_up, v_ffn2_w_down, v_final_norm_g):
    f_args = dict(locals())
    weights = {n: f_args[n] for n in [nm for nm, _ in SMALL if nm != "loss"] + list(BIG)}
    shapes = {n: weights[n].shape for n in weights}
    shapes["loss"] = ()
    place = jnp.stack([2 * lax.axis_index("x") + lax.axis_index("y"), lax.axis_index("c")]).astype(jnp.int32)

    def with_cols(name, a):
        a2 = a.reshape(a.shape[-2], a.shape[-1])
        return a2.T if BIG_TRANSPOSED[BIG.index(name)] else a2

    def natural(name, a2):
        return (a2.T if BIG_TRANSPOSED[BIG.index(name)] else a2).reshape(shapes[name])

    pack = _all_gather_pack(_pack_cast(place, [with_cols(n, weights[n]) for n in BIG], name="pack_cast"),
                            name="ag_weights")
    mix_rows = pack[:, MIX_BLOCK * FF_SH:, :]
    p = {n: weights[n].reshape(1, -1) for n in ("ffn1_norm_g", "mix_norm_g", "b_in", "gmlp_ln_g", "gmlp_ln_b",
                                                "attn_out_norm_g", "gmlp_out_norm_g", "b_out", "ffn2_norm_g",
                                                "final_norm_g")}
    p["w_in_t"] = mix_rows[:, :IN_SH, :].reshape(IN_W, D_MODEL)
    p["w_out"] = mix_rows[:, IN_SH:, :].reshape(D_MODEL, D_MODEL)
    p["attn_sinks"] = attn_sinks.reshape(N_Q_HEADS)
    p["gmlp_w_s"] = gmlp_w_s.reshape(GMLP_GROUPS, BLK, BLK)
    p["bs_full"] = jnp.broadcast_to(gmlp_b_s.reshape(GMLP_GROUPS, BLK).T[:, :, None],
                                    (BLK, GMLP_GROUPS, GROUP_DIM)).reshape(BLK, GMLP_W)

    loss_part, dx0, land, gs = _local_step(place, x[0], loss_target[0], p, pack)

    shard = _rs_share(_rs_reduce(place, land, name="rs_reduce"), name="rs_share")
    gs["gmlp_b_s"] = jnp.sum(gs["gmlp_b_s"].reshape(BLK, GMLP_GROUPS, GROUP_DIM), axis=-1).T
    gs["attn_sinks"] = gs["attn_sinks"][:, 0]
    gs["loss"] = loss_part[0, 0]
    small_sum = _small_all_reduce(_pack_small(gs), name="small_all_reduce")

    grad_w, delta, new_m, new_v = {}, {}, {}, {}
    off = 0
    for n, rows in zip(BIG, BIG_ROWS):
        res = _adamw(with_cols(n, weights[n]), shard, with_cols(n, f_args["m_" + n]), with_cols(n, f_args["v_" + n]),
                     g_row0=off, tile=FF_SH // 2 if rows == FF_SH else 64, name="adamw_" + n)
        grad_w[n], delta[n], new_m[n], new_v[n] = [natural(n, a) for a in res]
        off += rows
    sm = {k: {n: f_args[k + n] for n, _ in SMALL if n != "loss"} for k in ("", "m_", "v_")}
    for k in sm:
        sm[k]["loss"] = jnp.zeros((), F32)
    res = _adamw(_pack_small(sm[""]), small_sum, _pack_small(sm["m_"]), _pack_small(sm["v_"]),
                 g_row0=0, tile=SMALL_ROWS, name="adamw_small")
    small = _unpack_small(res[0], shapes)
    for dst, packed in ((grad_w, res[0]), (delta, res[1]), (new_m, res[2]), (new_v, res[3])):
        dst.update({n: a for n, a in _unpack_small(packed, shapes).items() if n != "loss"})

    order = ('ffn1_norm_g', 'ffn1_w_gate', 'ffn1_w_up', 'ffn1_w_down', 'mix_norm_g', 'w_in', 'b_in', 'attn_sinks',
             'gmlp_ln_g', 'gmlp_ln_b', 'gmlp_w_s', 'gmlp_b_s', 'attn_out_norm_g', 'gmlp_out_norm_g', 'w_out', 'b_out',
             'ffn2_norm_g', 'ffn2_w_gate', 'ffn2_w_up', 'ffn2_w_down', 'final_norm_g')
    return (small["loss"], dx0.reshape(x.shape), *[grad_w[n] for n in order], *[delta[n] for n in order],
            *[new_m[n] for n in order], *[new_v[n] for n in order])
```

```python
import functools

import jax
import jax.numpy as jnp
from jax import lax
from jax.experimental import pallas as pl
from jax.experimental.pallas import tpu as pltpu

F32 = jnp.float32
BF16 = jnp.bfloat16

D_MODEL = 1024
D_FF = 2816
N_CHIPS = 4
FF_SH = D_FF // N_CHIPS
N_Q_HEADS = 8
N_KV_HEADS = 2
REP = N_Q_HEADS // N_KV_HEADS
HEAD_DIM = 64
ATTN_W = 512
KV_W = 128
GMLP_W = 512
GMLP_GROUPS = 8
GROUP_DIM = 64
BLK = 128
IN_W = 1792
IN_SH = IN_W // N_CHIPS
OUT_SH = D_MODEL // N_CHIPS
EPS = 1e-6
FFN_RES = 0.5
ATTN_SCALE = HEAD_DIM ** -0.5

ADAM_LR = 0.001
ADAM_B1 = 0.9
ADAM_B2 = 0.999
ADAM_EPS = 1e-08
ADAM_WD = 0.01
ADAM_STEP = 10

V7X_VMEM_LIMIT = 56 * 1024 * 1024
MESH = pl.DeviceIdType.MESH


def _cparams(sem):
    return pltpu.CompilerParams(dimension_semantics=sem, vmem_limit_bytes=V7X_VMEM_LIMIT)


def _dot(a, b):
    return jnp.dot(a, b, preferred_element_type=F32)


def _dot_nt(a, b):
    return lax.dot_general(a, b, (((1,), (1,)), ((), ())), preferred_element_type=F32)


def _dot_tn(a, b):
    return lax.dot_general(a, b, (((0,), (0,)), ((), ())), preferred_element_type=F32)


def _rms(x, g):
    r = lax.rsqrt(jnp.mean(x * x, axis=-1, keepdims=True) + EPS)
    return x * r * g, r


def _rms_bwd(dh, x, r, g):
    gy = dh * g
    dx = r * gy - x * (r * r * r) * jnp.mean(gy * x, axis=-1, keepdims=True)
    dg = jnp.sum(dh * x * r, axis=0, keepdims=True)
    return dx, dg


def _const(shape):
    nd = len(shape)
    return pl.BlockSpec(shape, lambda *_: (0,) * nd)


def _rows(t, w):
    return pl.BlockSpec((t, w), lambda i: (i, 0))


PACK_ROWS = 7 * FF_SH
HALF_ROWS = PACK_ROWS // 2
FFN_HALF = 3 * FF_SH // 2
MIX_HALF = FF_SH // 2
PACK_A_ROWS = 3 * FF_SH
PACK_B_ROWS = 4 * FF_SH
BIG = ("ffn1_w_gate", "ffn1_w_up", "ffn1_w_down", "ffn2_w_gate", "ffn2_w_up", "ffn2_w_down", "w_in", "w_out")
BIG_ROWS = (FF_SH, FF_SH, FF_SH, FF_SH, FF_SH, FF_SH, IN_SH, OUT_SH)
BIG_TRANSPOSED = (True, True, False, True, True, False, True, False)

SMALL = (("ffn1_norm_g", 1024), ("mix_norm_g", 1024), ("b_in", 1792), ("attn_sinks", 8), ("gmlp_ln_g", 512),
         ("gmlp_ln_b", 512), ("gmlp_w_s", 131072), ("gmlp_b_s", 1024), ("attn_out_norm_g", 512),
         ("gmlp_out_norm_g", 512), ("b_out", 1024), ("ffn2_norm_g", 1024), ("final_norm_g", 1024), ("loss", 1))


def _small_rows(n):
    return -(-n // 1024) * 8


SMALL_ROWS = sum(_small_rows(n) for _, n in SMALL)


def _pack_small(parts):
    out = []
    for name, n in SMALL:
        flat = parts[name].reshape(-1).astype(F32)
        rows = _small_rows(n)
        out.append(jnp.pad(flat, (0, rows * 128 - n)).reshape(rows, 128))
    return jnp.concatenate(out, axis=0)


def _unpack_small(packed, shapes):
    res, off = {}, 0
    for name, n in SMALL:
        rows = _small_rows(n)
        res[name] = packed[off:off + rows].reshape(-1)[:n].reshape(shapes[name])
        off += rows
    return res


def _ffn_tile(x, g, wg_ref, wu_ref, wd_ref, hb_ref, a_ref, b_ref):
    h, _ = _rms(x, g)
    hb = h.astype(BF16)
    hb_ref[...] = hb
    acc = jnp.zeros(x.shape, F32)
    for j in range(N_CHIPS):
        a = _dot_nt(hb, wg_ref[j])
        b = _dot_nt(hb, wu_ref[j])
        a_ref[j] = a
        b_ref[j] = b
        f = (a * jax.nn.sigmoid(a) * b).astype(BF16)
        acc = acc + _dot(f, wd_ref[j])
    return x + FFN_RES * acc


def _ffn_saved_specs(s, tile):
    ab = pl.BlockSpec((N_CHIPS, tile, FF_SH), lambda i: (0, i, 0))
    shape = jax.ShapeDtypeStruct((N_CHIPS, s, FF_SH), F32)
    return [_rows(tile, D_MODEL), ab, ab], [jax.ShapeDtypeStruct((s, D_MODEL), BF16), shape, shape]


def _ffn_weight_specs(k0):
    one = pl.Buffered(1)
    return [pl.BlockSpec((N_CHIPS, FF_SH, D_MODEL), functools.partial(lambda kk, i: (0, kk, 0), k0 + d),
                         pipeline_mode=one) for d in range(3)]


def _mesh_place():
    x, y, c = lax.axis_index("x"), lax.axis_index("y"), lax.axis_index("c")
    others = [(1 - x, y), (x, 1 - y), (1 - x, 1 - y)]
    return x, y, c, others


def _gather_stages(o_ref, send_sems, recv_sems):
    x, y, c, others = _mesh_place()
    me = 2 * x + y
    sibling = (x, y, 1 - c)
    half_rows = o_ref.shape[1] // 2

    def half(slab, core):
        return o_ref.at[slab, pl.ds(pl.multiple_of(core * half_rows, 16), half_rows)]

    def copy(k, rows, to):
        return pltpu.make_async_remote_copy(src_ref=rows, dst_ref=rows, send_sem=send_sems.at[k],
                                            recv_sem=recv_sems.at[k], device_id=to, device_id_type=MESH)

    first = [copy(j, half(me, c), (px, py, c)) for j, (px, py) in enumerate(others)]
    passed = [copy(3 + j, half(2 * px + py, c), sibling) for j, (px, py) in enumerate(others)]

    def start():
        for cp in first:
            cp.start()

    def forward():
        for j, (px, py) in enumerate(others):
            copy(j, half(2 * px + py, c), (px, py, c)).wait_recv()
            passed[j].start()

    def finish():
        for j, (px, py) in enumerate(others):
            copy(3 + j, half(2 * px + py, 1 - c), sibling).wait_recv()
        for cp in first + passed:
            cp.wait_send()

    return start, forward, finish


def _ffn_fwd(x, g, pack, k0, gather, *, tile, name):
    s = x.shape[0]
    nt = s // tile
    forward_at = max(nt - 6, 0)

    def body(x_ref, g_ref, wg_ref, wu_ref, wd_ref, gin_ref, o_ref, hb_ref, a_ref, b_ref, gat_ref, send_sems, recv_sems):
        i = pl.program_id(0)
        start, forward, finish = _gather_stages(gat_ref, send_sems, recv_sems)
        pl.when(i == 0)(start)
        o_ref[...] = _ffn_tile(x_ref[...], g_ref[...], wg_ref, wu_ref, wd_ref, hb_ref, a_ref, b_ref)
        pl.when(i == forward_at)(forward)
        pl.when(i == nt - 1)(finish)

    saved_specs, saved_shapes = _ffn_saved_specs(s, tile)
    hbm = pl.BlockSpec(memory_space=pl.ANY)
    return pl.pallas_call(
        body, name=name, grid=(nt,),
        in_specs=[_rows(tile, D_MODEL), _const((1, D_MODEL))] + _ffn_weight_specs(k0) + [hbm],
        out_specs=[_rows(tile, D_MODEL)] + saved_specs + [hbm],
        out_shape=[jax.ShapeDtypeStruct(x.shape, F32)] + saved_shapes
                  + [jax.ShapeDtypeStruct(gather.shape, gather.dtype)],
        input_output_aliases={5: 4},
        scratch_shapes=[pltpu.SemaphoreType.DMA((6,)), pltpu.SemaphoreType.DMA((6,))],
        compiler_params=_cparams(("arbitrary",)),
    )(x, g, pack, pack, pack, gather)


def _ffn_fwd_loss(x, g, pack, k0, gf, tgt, *, tile, name):
    s = x.shape[0]

    def body(x_ref, g_ref, wg_ref, wu_ref, wd_ref, gf_ref, t_ref, dx_ref, loss_ref, dgf_ref, hb_ref, a_ref, b_ref):
        @pl.when(pl.program_id(0) == 0)
        def _():
            loss_ref[...] = jnp.zeros_like(loss_ref)
            dgf_ref[...] = jnp.zeros_like(dgf_ref)

        x3 = _ffn_tile(x_ref[...], g_ref[...], wg_ref, wu_ref, wd_ref, hb_ref, a_ref, b_ref)
        gf_v = gf_ref[...]
        out, r = _rms(x3, gf_v)
        diff = out - t_ref[...]
        part = jnp.sum(jnp.sum(diff * diff, axis=-1, keepdims=True), axis=0, keepdims=True)
        loss_ref[...] += jnp.broadcast_to(part * (0.5 / D_MODEL), loss_ref.shape)
        dx, dg = _rms_bwd(diff * (1.0 / D_MODEL), x3, r, gf_v)
        dx_ref[...] = dx
        dgf_ref[...] += dg

    saved_specs, saved_shapes = _ffn_saved_specs(s, tile)
    return pl.pallas_call(
        body, name=name, grid=(s // tile,),
        in_specs=[_rows(tile, D_MODEL), _const((1, D_MODEL))] + _ffn_weight_specs(k0)
                 + [_const((1, D_MODEL)), _rows(tile, D_MODEL)],
        out_specs=[_rows(tile, D_MODEL), _const((1, 128)), _const((1, D_MODEL))] + saved_specs,
        out_shape=[jax.ShapeDtypeStruct(x.shape, F32),
                   jax.ShapeDtypeStruct((1, 128), F32),
                   jax.ShapeDtypeStruct((1, D_MODEL), F32)] + saved_shapes,
        compiler_params=_cparams(("arbitrary",)),
    )(x, g, pack, pack, pack, gf, tgt)


def _ffn_bwd(place, hb, a, b, dy, pack, region, land, mix_grads, *, tile, name):
    s = hb.shape[0]
    nt = s // tile
    land_rows = pl.ds(region * FFN_HALF, FFN_HALF)
    mix_rows = pl.ds(2 * FFN_HALF, MIX_HALF)
    with_mix = mix_grads is not None
    with_land = land is not None
    n_others = 2 * N_CHIPS - 1

    def body(place_ref, hb_ref, a_ref, b_ref, dy_ref, wg_ref, wu_ref, wd_ref, *rest):
        rest = list(rest)
        mix_ref = rest.pop(0) if with_mix else None
        if with_land:
            rest.pop(0)
        dhp_ref, land_ref, acc_ref, stage_ref, send_sems, recv_sem, local_sem = rest[:7]
        t, i = pl.program_id(0), pl.program_id(1)
        xi, yi, c = lax.axis_index("x"), lax.axis_index("y"), lax.axis_index("c")
        dev = 4 * xi + 2 * yi + c
        tt = (t + 1) % N_CHIPS
        tx, ty = jnp.bitwise_xor(xi, tt // 2), jnp.bitwise_xor(yi, tt % 2)

        def remote(src, dst, ssem, rsem, to):
            return pltpu.make_async_remote_copy(src_ref=src, dst_ref=dst, send_sem=ssem, recv_sem=rsem,
                                                device_id=to, device_id_type=MESH)

        def stage_half(h):
            return stage_ref.at[pl.ds(pl.multiple_of(h * FFN_HALF, 16), FFN_HALF)]

        if with_mix:
            mix_send, mix_recv, mix_local = rest[7:10]

            @pl.when(jnp.logical_and(t == 0, i == 0))
            def _():
                for chip in range(N_CHIPS):
                    for h in range(2):
                        src = mix_ref.at[chip, pl.ds(h * MIX_HALF, MIX_HALF)]
                        dst = land_ref.at[dev, mix_rows]
                        mine = jnp.logical_and(2 * xi + yi == chip, c == h)

                        @pl.when(mine)
                        def _():
                            pltpu.make_async_copy(src, dst, mix_local).start()

                        @pl.when(jnp.logical_not(mine))
                        def _():
                            remote(src, dst, mix_send, mix_recv, (chip // 2, chip % 2, h)).start()

        @pl.when(i == 0)
        def _():
            acc_ref[...] = jnp.zeros_like(acc_ref)

        hb = hb_ref[...]
        dob = (FFN_RES * dy_ref[...]).astype(BF16)
        wg_j, wu_j, wd_j = wg_ref[0], wu_ref[0], wd_ref[0]
        a = a_ref[0]
        b = b_ref[0]
        sg = jax.nn.sigmoid(a)
        sa = a * sg
        fb = (sa * b).astype(BF16)
        df = _dot_nt(dob, wd_j)
        dbb = (df * sa).astype(BF16)
        dab = (df * b * (sg + sa * (1.0 - sg))).astype(BF16)
        dhp_ref[0] = _dot(dab, wg_j) + _dot(dbb, wu_j)
        acc_ref[0:FF_SH, :] += _dot_tn(dab, hb)
        acc_ref[FF_SH:2 * FF_SH, :] += _dot_tn(dbb, hb)
        acc_ref[2 * FF_SH:3 * FF_SH, :] += _dot_tn(fb, dob)

        @pl.when(i == nt - 1)
        def _():
            dst = land_ref.at[dev, land_rows]

            @pl.when(t > 0)
            def _():
                for h in range(2):
                    remote(stage_half(h), dst, send_sems.at[h], recv_sem, (tx, ty, h)).wait_send()

            def cast_rows(r, carry):
                rows = pl.ds(pl.multiple_of(r * MIX_HALF, 16), MIX_HALF)
                stage_ref[rows, :] = acc_ref[rows, :].astype(BF16)
                return carry

            lax.fori_loop(0, 3 * FF_SH // MIX_HALF, cast_rows, 0)

            @pl.when(t < N_CHIPS - 1)
            def _():
                for h in range(2):
                    remote(stage_half(h), dst, send_sems.at[h], recv_sem, (tx, ty, h)).start()

            @pl.when(t == N_CHIPS - 1)
            def _():
                own = pltpu.make_async_copy(stage_half(c), dst, local_sem)
                own.start()
                sib = remote(stage_half(1 - c), dst, send_sems.at[0], recv_sem, (xi, yi, 1 - c))
                sib.start()
                sib.wait_send()
                own.wait()
                arrivals = land_ref.at[pl.ds(0, n_others), land_rows]
                remote(arrivals, arrivals, send_sems.at[0], recv_sem, (xi, yi, 1 - c)).wait_recv()
                if with_mix:
                    seven = land_ref.at[pl.ds(0, n_others), mix_rows]
                    both = remote(seven, seven, mix_send, mix_recv, (xi, yi, 1 - c))
                    both.wait_send()
                    both.wait_recv()
                    pltpu.make_async_copy(mix_ref.at[0, pl.ds(0, MIX_HALF)], land_ref.at[dev, mix_rows],
                                          mix_local).wait()

    def wspec(kk):
        return pl.BlockSpec((1, FF_SH, D_MODEL),
                            lambda t, i, pr: (jnp.bitwise_xor(pr[0], (t + 1) % N_CHIPS), kk, 0))

    xspec = pl.BlockSpec((tile, D_MODEL), lambda t, i, pr: (i, 0))
    abspec = pl.BlockSpec((1, tile, FF_SH), lambda t, i, pr: (jnp.bitwise_xor(pr[0], (t + 1) % N_CHIPS), i, 0))
    hbm = pl.BlockSpec(memory_space=pl.ANY)
    in_specs = [xspec, abspec, abspec, xspec, wspec(0), wspec(1), wspec(2)]
    operands = [place, hb, a, b, dy, pack, pack, pack]
    scratch = [pltpu.VMEM((3 * FF_SH, D_MODEL), F32), pltpu.VMEM((3 * FF_SH, D_MODEL), BF16),
               pltpu.SemaphoreType.DMA((2,)), pltpu.SemaphoreType.DMA, pltpu.SemaphoreType.DMA]
    if with_mix:
        in_specs.append(hbm)
        operands.append(mix_grads)
        scratch += [pltpu.SemaphoreType.DMA, pltpu.SemaphoreType.DMA, pltpu.SemaphoreType.DMA]
    aliases = {}
    if with_land:
        in_specs.append(hbm)
        operands.append(land)
        aliases = {len(operands) - 1: 1}
    grid_spec = pltpu.PrefetchScalarGridSpec(
        num_scalar_prefetch=1, grid=(N_CHIPS, nt), in_specs=in_specs,
        out_specs=[pl.BlockSpec((1, tile, D_MODEL), lambda t, i, pr: (t, i, 0)), hbm],
        scratch_shapes=scratch)
    return pl.pallas_call(
        body, name=name, grid_spec=grid_spec,
        out_shape=[jax.ShapeDtypeStruct((N_CHIPS, s, D_MODEL), F32),
                   jax.ShapeDtypeStruct((2 * N_CHIPS, HALF_ROWS, D_MODEL), BF16)],
        input_output_aliases=aliases,
        compiler_params=_cparams(("arbitrary", "arbitrary")),
    )(*operands)


def _mix_grads_pack(dw_in_t, dw_out, *, name):
    def body(a_ref, b_ref, o_ref):
        o_ref[0, 0:IN_SH, :] = a_ref[0].astype(BF16)
        o_ref[0, IN_SH:FF_SH, :] = b_ref[0].astype(BF16)

    return pl.pallas_call(
        body, name=name, grid=(N_CHIPS,),
        in_specs=[pl.BlockSpec((1, IN_SH, D_MODEL), lambda j: (j, 0, 0)),
                  pl.BlockSpec((1, OUT_SH, D_MODEL), lambda j: (j, 0, 0))],
        out_specs=pl.BlockSpec((1, FF_SH, D_MODEL), lambda j: (j, 0, 0)),
        out_shape=jax.ShapeDtypeStruct((N_CHIPS, FF_SH, D_MODEL), BF16),
        compiler_params=_cparams(("arbitrary",)),
    )(dw_in_t.reshape(N_CHIPS, IN_SH, D_MODEL), dw_out.reshape(N_CHIPS, OUT_SH, D_MODEL))


def _norm_bwd(dhp, x, dy, g, *, tile, name):
    s = x.shape[0]

    def body(dhp_ref, x_ref, dy_ref, g_ref, dx_ref, dg_ref):
        @pl.when(pl.program_id(0) == 0)
        def _():
            dg_ref[...] = jnp.zeros_like(dg_ref)

        dh = (dhp_ref[0] + dhp_ref[1]) + (dhp_ref[2] + dhp_ref[3])
        x_v = x_ref[...]
        r = lax.rsqrt(jnp.mean(x_v * x_v, axis=-1, keepdims=True) + EPS)
        dx, dg = _rms_bwd(dh, x_v, r, g_ref[...])
        dx_ref[...] = dy_ref[...] + dx
        dg_ref[...] += dg

    return pl.pallas_call(
        body, name=name, grid=(s // tile,),
        in_specs=[pl.BlockSpec((N_CHIPS, tile, D_MODEL), lambda i: (0, i, 0)),
                  _rows(tile, D_MODEL), _rows(tile, D_MODEL), _const((1, D_MODEL))],
        out_specs=[_rows(tile, D_MODEL), _const((1, D_MODEL))],
        out_shape=[jax.ShapeDtypeStruct(x.shape, F32), jax.ShapeDtypeStruct((1, D_MODEL), F32)],
        compiler_params=_cparams(("arbitrary",)),
    )(dhp, x, dy, g)


def _mix_in_fwd(x, g, w_in_t, b_in, *, tile, name):
    s = x.shape[0]

    def body(x_ref, g_ref, w_ref, b_ref, q_ref, k_ref, v_ref, z_ref):
        h, _ = _rms(x_ref[...], g_ref[...])
        proj = _dot_nt(h.astype(BF16), w_ref[...]) + b_ref[...]
        q_ref[...] = proj[:, :ATTN_W].astype(BF16)
        k_ref[...] = proj[:, ATTN_W:ATTN_W + KV_W].astype(BF16)
        v_ref[...] = proj[:, ATTN_W + KV_W:ATTN_W + 2 * KV_W].astype(BF16)
        z_ref[...] = proj[:, ATTN_W + 2 * KV_W:]

    return pl.pallas_call(
        body, name=name, grid=(s // tile,),
        in_specs=[_rows(tile, D_MODEL), _const((1, D_MODEL)), _const((IN_W, D_MODEL)), _const((1, IN_W))],
        out_specs=[_rows(tile, ATTN_W), _rows(tile, KV_W), _rows(tile, KV_W), _rows(tile, 2 * GMLP_W)],
        out_shape=[jax.ShapeDtypeStruct((s, ATTN_W), BF16), jax.ShapeDtypeStruct((s, KV_W), BF16),
                   jax.ShapeDtypeStruct((s, KV_W), BF16), jax.ShapeDtypeStruct((s, 2 * GMLP_W), F32)],
        compiler_params=_cparams(("arbitrary",)),
    )(x, g, w_in_t, b_in)


def _mix_in_bwd(x, dy, dq, dk, dv, dz, g, w_in_t, *, tile, name):
    s = x.shape[0]

    def body(x_ref, dy_ref, dq_ref, dk_ref, dv_ref, dz_ref, g_ref, w_ref, dx_ref, dw_ref, db_ref, dg_ref):
        @pl.when(pl.program_id(0) == 0)
        def _():
            dw_ref[...] = jnp.zeros_like(dw_ref)
            db_ref[...] = jnp.zeros_like(db_ref)
            dg_ref[...] = jnp.zeros_like(dg_ref)

        dproj = jnp.concatenate([dq_ref[...], dk_ref[...], dv_ref[...], dz_ref[...]], axis=-1)
        db_ref[...] += jnp.sum(dproj, axis=0, keepdims=True)
        dpb = dproj.astype(BF16)
        x_v = x_ref[...]
        g_v = g_ref[...]
        h, r = _rms(x_v, g_v)
        dw_ref[...] += _dot_tn(dpb, h.astype(BF16))
        dh = _dot(dpb, w_ref[...])
        dx, dg = _rms_bwd(dh, x_v, r, g_v)
        dx_ref[...] = dy_ref[...] + dx
        dg_ref[...] += dg

    return pl.pallas_call(
        body, name=name, grid=(s // tile,),
        in_specs=[_rows(tile, D_MODEL), _rows(tile, D_MODEL), _rows(tile, ATTN_W), _rows(tile, KV_W),
                  _rows(tile, KV_W), _rows(tile, 2 * GMLP_W), _const((1, D_MODEL)), _const((IN_W, D_MODEL))],
        out_specs=[_rows(tile, D_MODEL), _const((IN_W, D_MODEL)), _const((1, IN_W)), _const((1, D_MODEL))],
        out_shape=[jax.ShapeDtypeStruct(x.shape, F32), jax.ShapeDtypeStruct((IN_W, D_MODEL), F32),
                   jax.ShapeDtypeStruct((1, IN_W), F32), jax.ShapeDtypeStruct((1, D_MODEL), F32)],
        compiler_params=_cparams(("arbitrary",)),
    )(x, dy, dq, dk, dv, dz, g, w_in_t)


_GELU_C = 0.7978845608028654
_GELU_A = 0.044715


def _gelu(x):
    return 0.5 * x * (1.0 + jnp.tanh(_GELU_C * (x + _GELU_A * (x * x * x))))


def _gelu_grad(x):
    t = jnp.tanh(_GELU_C * (x + _GELU_A * (x * x * x)))
    return 0.5 * (1.0 + t) + 0.5 * x * (1.0 - t * t) * (_GELU_C * (1.0 + 3.0 * _GELU_A * (x * x)))


def _band(ref, i):
    prev = jnp.maximum(i - 1, 0)
    return jnp.concatenate([ref[pl.ds(pl.multiple_of(prev * BLK, BLK), BLK), :],
                            ref[pl.ds(pl.multiple_of(i * BLK, BLK), BLK), :]], axis=0)


def _band_mask(i):
    qpos = lax.broadcasted_iota(jnp.int32, (BLK, 2 * BLK), 0)
    kidx = lax.broadcasted_iota(jnp.int32, (BLK, 2 * BLK), 1)
    rel = qpos - kidx + BLK
    win = jnp.where(rel >= 0, jnp.where(rel < BLK, 1, 0), 0)
    real = jnp.where(kidx >= BLK, 1, jnp.where(i > 0, 1, 0))
    return (win * real) > 0


def _attn_probs(qh, kg, mask, sink):
    sc = _dot_nt(qh, kg) * ATTN_SCALE
    sc = jnp.where(mask, sc, -jnp.inf)
    m = jnp.maximum(jnp.max(sc, axis=-1, keepdims=True), sink)
    p = jnp.exp(sc - m)
    es = jnp.exp(sink - m)
    inv = 1.0 / (jnp.sum(p, axis=-1, keepdims=True) + es)
    return p * inv, es * inv


def _tril_mask():
    t = lax.broadcasted_iota(jnp.int32, (BLK, BLK), 0)
    s_ = lax.broadcasted_iota(jnp.int32, (BLK, BLK), 1)
    return s_ <= t


def _gmlp_fwd_parts(zg, lng, lnb, ws_ref, bs_full):
    z = _gelu(zg)
    u = z[:, :GMLP_W]
    zv = z[:, GMLP_W:]
    mu = jnp.mean(zv, axis=-1, keepdims=True)
    zc = zv - mu
    rstd = lax.rsqrt(jnp.mean(zc * zc, axis=-1, keepdims=True) + EPS)
    xh = zc * rstd
    vvb = (xh * lng + lnb).astype(BF16)
    tril = _tril_mask()
    wms, parts = [], []
    for gi in range(GMLP_GROUPS):
        wm = jnp.where(tril, ws_ref[gi], 0.0).astype(BF16)
        wms.append(wm)
        parts.append(_dot(wm, vvb[:, gi * GROUP_DIM:(gi + 1) * GROUP_DIM]))
    mixed = jnp.concatenate(parts, axis=-1) + bs_full
    return u, xh, rstd, vvb, wms, mixed


def _attn_fwd(q, kb, vb, mask, sink_ref):
    outs = []
    for h in range(N_Q_HEADS):
        gi = h // REP
        pn, _ = _attn_probs(q[:, h * HEAD_DIM:(h + 1) * HEAD_DIM], kb[:, gi * HEAD_DIM:(gi + 1) * HEAD_DIM],
                            mask, sink_ref[h])
        outs.append(_dot(pn.astype(BF16), vb[:, gi * HEAD_DIM:(gi + 1) * HEAD_DIM]))
    return jnp.concatenate(outs, axis=-1)


def _mix_core_fwd(q, k, v, zg, sinks, lng, lnb, w_s, bs_full, gao, ggo, *, name):
    s = q.shape[0]

    def body(sink_ref, q_ref, k_ref, v_ref, z_ref, lng_ref, lnb_ref, ws_ref, bs_ref, gao_ref, ggo_ref, o_ref):
        i = pl.program_id(0)
        mask = _band_mask(i)
        y_attn = _attn_fwd(q_ref[...], _band(k_ref, i), _band(v_ref, i), mask, sink_ref)
        u, _, _, _, _, mixed = _gmlp_fwd_parts(z_ref[...], lng_ref[...], lnb_ref[...], ws_ref, bs_ref[...])
        ya, _ = _rms(y_attn, gao_ref[...])
        yg, _ = _rms(u * mixed, ggo_ref[...])
        o_ref[...] = jnp.concatenate([ya, yg], axis=-1).astype(BF16)

    return pl.pallas_call(
        body, name=name, grid=(s // BLK,),
        in_specs=[pl.BlockSpec(memory_space=pltpu.SMEM),
                  _rows(BLK, ATTN_W), _const((s, KV_W)), _const((s, KV_W)),
                  _rows(BLK, 2 * GMLP_W), _const((1, GMLP_W)), _const((1, GMLP_W)),
                  _const((GMLP_GROUPS, BLK, BLK)), _const((BLK, GMLP_W)), _const((1, ATTN_W)), _const((1, GMLP_W))],
        out_specs=_rows(BLK, D_MODEL),
        out_shape=jax.ShapeDtypeStruct((s, D_MODEL), BF16),
        compiler_params=_cparams(("arbitrary",)),
    )(sinks, q, k, v, zg, lng, lnb, w_s, bs_full, gao, ggo)


def _mix_out_fwd(x1, yb, w_out, b_out, *, tile, name):
    s = x1.shape[0]

    def body(x_ref, y_ref, w_ref, b_ref, o_ref):
        o_ref[...] = x_ref[...] + (_dot(y_ref[...], w_ref[...]) + b_ref[...])

    return pl.pallas_call(
        body, name=name, grid=(s // tile,),
        in_specs=[_rows(tile, D_MODEL), _rows(tile, D_MODEL), _const((D_MODEL, D_MODEL)), _const((1, D_MODEL))],
        out_specs=_rows(tile, D_MODEL),
        out_shape=jax.ShapeDtypeStruct(x1.shape, F32),
        compiler_params=_cparams(("arbitrary",)),
    )(x1, yb, w_out, b_out)


def _norm_bwd_mix_out(dhp, x, dy, g, yb, w_out, *, tile, name):
    s = x.shape[0]

    def body(dhp_ref, x_ref, dy_ref, g_ref, y_ref, w_ref, dx_ref, dg_ref, dyy_ref, dw_ref, db_ref):
        @pl.when(pl.program_id(0) == 0)
        def _():
            dg_ref[...] = jnp.zeros_like(dg_ref)
            dw_ref[...] = jnp.zeros_like(dw_ref)
            db_ref[...] = jnp.zeros_like(db_ref)

        dh = (dhp_ref[0] + dhp_ref[1]) + (dhp_ref[2] + dhp_ref[3])
        x_v = x_ref[...]
        r = lax.rsqrt(jnp.mean(x_v * x_v, axis=-1, keepdims=True) + EPS)
        dxn, dg = _rms_bwd(dh, x_v, r, g_ref[...])
        dx = dy_ref[...] + dxn
        dx_ref[...] = dx
        dg_ref[...] += dg
        dxb = dx.astype(BF16)
        db_ref[...] += jnp.sum(dx, axis=0, keepdims=True)
        dw_ref[...] += _dot_tn(y_ref[...], dxb)
        dyy_ref[...] = _dot_nt(dxb, w_ref[...])

    return pl.pallas_call(
        body, name=name, grid=(s // tile,),
        in_specs=[pl.BlockSpec((N_CHIPS, tile, D_MODEL), lambda i: (0, i, 0)),
                  _rows(tile, D_MODEL), _rows(tile, D_MODEL), _const((1, D_MODEL)), _rows(tile, D_MODEL),
                  _const((D_MODEL, D_MODEL))],
        out_specs=[_rows(tile, D_MODEL), _const((1, D_MODEL)), _rows(tile, D_MODEL), _const((D_MODEL, D_MODEL)),
                   _const((1, D_MODEL))],
        out_shape=[jax.ShapeDtypeStruct(x.shape, F32), jax.ShapeDtypeStruct((1, D_MODEL), F32),
                   jax.ShapeDtypeStruct(x.shape, F32), jax.ShapeDtypeStruct((D_MODEL, D_MODEL), F32),
                   jax.ShapeDtypeStruct((1, D_MODEL), F32)],
        compiler_params=_cparams(("arbitrary",)),
    )(dhp, x, dy, g, yb, w_out)


def _mix_core_bwd(dyy, q, k, v, zg, sinks, lng, lnb, w_s, bs_full, gao, ggo, *, name):
    s = dyy.shape[0]
    nblk = s // BLK

    def body(sink_ref, dyy_ref, q_ref, k_ref, v_ref, z_ref, lng_ref, lnb_ref, ws_ref, bs_ref, gao_ref, ggo_ref,
             dq_ref, dk_ref, dv_ref, dz_ref, dgao_ref, dggo_ref, dlng_ref, dlnb_ref, dws_ref, dms_ref, dsk_ref):
        i = pl.program_id(0)

        @pl.when(i == 0)
        def _():
            for ref in (dk_ref, dv_ref, dgao_ref, dggo_ref, dlng_ref, dlnb_ref, dws_ref, dms_ref, dsk_ref):
                ref[...] = jnp.zeros_like(ref)

        mask = _band_mask(i)
        q_v = q_ref[...]
        kb = _band(k_ref, i)
        vb = _band(v_ref, i)
        lng_v = lng_ref[...]
        gao_v = gao_ref[...]
        ggo_v = ggo_ref[...]
        zg_v = z_ref[...]

        y_attn = _attn_fwd(q_v, kb, vb, mask, sink_ref)
        u, xh, rstd, vvb, wms, mixed = _gmlp_fwd_parts(zg_v, lng_v, lnb_ref[...], ws_ref, bs_ref[...])
        y_gmlp = u * mixed
        ra = lax.rsqrt(jnp.mean(y_attn * y_attn, axis=-1, keepdims=True) + EPS)
        rg = lax.rsqrt(jnp.mean(y_gmlp * y_gmlp, axis=-1, keepdims=True) + EPS)

        dyy = dyy_ref[...]
        d_attn, dgao = _rms_bwd(dyy[:, :ATTN_W], y_attn, ra, gao_v)
        d_gmlp, dggo = _rms_bwd(dyy[:, ATTN_W:], y_gmlp, rg, ggo_v)
        dgao_ref[...] += dgao
        dggo_ref[...] += dggo

        du = d_gmlp * mixed
        dmixed = d_gmlp * u
        dms_ref[...] += dmixed
        dmb = dmixed.astype(BF16)
        dvv_parts = []
        for gi in range(GMLP_GROUPS):
            sl = slice(gi * GROUP_DIM, (gi + 1) * GROUP_DIM)
            dws_ref[gi] += _dot_nt(dmb[:, sl], vvb[:, sl])
            dvv_parts.append(_dot_tn(wms[gi], dmb[:, sl]))
        dvv = jnp.concatenate(dvv_parts, axis=-1)
        dlng_ref[...] += jnp.sum(dvv * xh, axis=0, keepdims=True)
        dlnb_ref[...] += jnp.sum(dvv, axis=0, keepdims=True)
        dxh = dvv * lng_v
        dzv = rstd * (dxh - jnp.mean(dxh, axis=-1, keepdims=True)
                      - xh * jnp.mean(dxh * xh, axis=-1, keepdims=True))
        dz_ref[...] = jnp.concatenate([du, dzv], axis=-1) * _gelu_grad(zg_v)

        dab = d_attn.astype(BF16)
        dq_parts = []
        dk_parts = []
        dv_parts = []
        for gi in range(N_KV_HEADS):
            kg = kb[:, gi * HEAD_DIM:(gi + 1) * HEAD_DIM]
            vg = vb[:, gi * HEAD_DIM:(gi + 1) * HEAD_DIM]
            dkg = jnp.zeros((2 * BLK, HEAD_DIM), F32)
            dvg = jnp.zeros((2 * BLK, HEAD_DIM), F32)
            for rr in range(REP):
                h = gi * REP + rr
                hs = slice(h * HEAD_DIM, (h + 1) * HEAD_DIM)
                qh = q_v[:, hs]
                doh = dab[:, hs]
                pn, psink = _attn_probs(qh, kg, mask, sink_ref[h])
                dp = _dot_nt(doh, vg)
                delta = jnp.sum(pn * dp, axis=-1, keepdims=True)
                dsb = (pn * (dp - delta) * ATTN_SCALE).astype(BF16)
                dsink = jnp.sum(-psink * delta, axis=0, keepdims=True)
                dsk_ref[pl.ds(h, 1), :] += jnp.broadcast_to(dsink, (1, 128))
                dq_parts.append(_dot(dsb, kg))
                dkg = dkg + _dot_tn(dsb, qh)
                dvg = dvg + _dot_tn(pn.astype(BF16), doh)
            dk_parts.append(dkg)
            dv_parts.append(dvg)
        dq_ref[...] = jnp.concatenate(dq_parts, axis=-1)
        dkb = jnp.concatenate(dk_parts, axis=-1)
        dvb = jnp.concatenate(dv_parts, axis=-1)
        prev = pl.ds(pl.multiple_of(jnp.maximum(i - 1, 0) * BLK, BLK), BLK)
        cur = pl.ds(pl.multiple_of(i * BLK, BLK), BLK)
        dk_ref[prev, :] += dkb[:BLK]
        dv_ref[prev, :] += dvb[:BLK]
        dk_ref[cur, :] += dkb[BLK:]
        dv_ref[cur, :] += dvb[BLK:]

        @pl.when(i == nblk - 1)
        def _():
            tril = _tril_mask()
            for gi in range(GMLP_GROUPS):
                dws_ref[gi] = jnp.where(tril, dws_ref[gi], 0.0)

    return pl.pallas_call(
        body, name=name, grid=(nblk,),
        in_specs=[pl.BlockSpec(memory_space=pltpu.SMEM),
                  _rows(BLK, D_MODEL), _rows(BLK, ATTN_W), _const((s, KV_W)), _const((s, KV_W)),
                  _rows(BLK, 2 * GMLP_W), _const((1, GMLP_W)), _const((1, GMLP_W)),
                  _const((GMLP_GROUPS, BLK, BLK)), _const((BLK, GMLP_W)), _const((1, ATTN_W)), _const((1, GMLP_W))],
        out_specs=[_rows(BLK, ATTN_W), _const((s, KV_W)), _const((s, KV_W)), _rows(BLK, 2 * GMLP_W),
                   _const((1, ATTN_W)), _const((1, GMLP_W)),
                   _const((1, GMLP_W)), _const((1, GMLP_W)), _const((GMLP_GROUPS, BLK, BLK)),
                   _const((BLK, GMLP_W)), _const((N_Q_HEADS, 128))],
        out_shape=[jax.ShapeDtypeStruct((s, ATTN_W), F32), jax.ShapeDtypeStruct((s, KV_W), F32),
                   jax.ShapeDtypeStruct((s, KV_W), F32), jax.ShapeDtypeStruct((s, 2 * GMLP_W), F32),
                   jax.ShapeDtypeStruct((1, ATTN_W), F32), jax.ShapeDtypeStruct((1, GMLP_W), F32),
                   jax.ShapeDtypeStruct((1, GMLP_W), F32), jax.ShapeDtypeStruct((1, GMLP_W), F32),
                   jax.ShapeDtypeStruct((GMLP_GROUPS, BLK, BLK), F32), jax.ShapeDtypeStruct((BLK, GMLP_W), F32),
                   jax.ShapeDtypeStruct((N_Q_HEADS, 128), F32)],
        compiler_params=_cparams(("arbitrary",)),
    )(sinks, dyy, q, k, v, zg, lng, lnb, w_s, bs_full, gao, ggo)


def _local_step(place, x, tgt, p, pack_a, pack_b, *, tile=512, fwd_tile=256, bwd_tile=512, norm_tile=256):
    g = {}
    tile, fwd_tile, bwd_tile, norm_tile = (min(t_, x.shape[0]) for t_ in (tile, fwd_tile, bwd_tile, norm_tile))
    x1, hb1, a1, b1, pack_b = _ffn_fwd(x, p["ffn1_norm_g"], pack_a, 0, pack_b, tile=fwd_tile, name="ffn1_fwd")
    mix_rows = pack_b[:, 3 * FF_SH:, :]
    w_in_t = mix_rows[:, :IN_SH, :].reshape(IN_W, D_MODEL)
    w_out = mix_rows[:, IN_SH:, :].reshape(D_MODEL, D_MODEL)
    q, k, v, zg = _mix_in_fwd(x1, p["mix_norm_g"], w_in_t, p["b_in"], tile=tile, name="mix_in_fwd")
    mix_args = (q, k, v, zg, p["attn_sinks"], p["gmlp_ln_g"], p["gmlp_ln_b"], p["gmlp_w_s"], p["bs_full"],
                p["attn_out_norm_g"], p["gmlp_out_norm_g"])
    yb = _mix_core_fwd(*mix_args, name="mix_core_fwd")
    x2 = _mix_out_fwd(x1, yb, w_out, p["b_out"], tile=tile, name="mix_out_fwd")
    dx3, loss, g["final_norm_g"], hb2, a2, b2 = _ffn_fwd_loss(
        x2, p["ffn2_norm_g"], pack_b, 0, p["final_norm_g"], tgt, tile=fwd_tile, name="ffn2_fwd_loss")

    dhp, land = _ffn_bwd(place, hb2, a2, b2, dx3, pack_b, 1, None, None, tile=bwd_tile, name="ffn2_bwd")
    dx2, g["ffn2_norm_g"], dyy, dw_out, g["b_out"] = _norm_bwd_mix_out(
        dhp, x2, dx3, p["ffn2_norm_g"], yb, w_out, tile=norm_tile, name="ffn2_norm_bwd")

    (dq, dk, dv, dz, g["attn_out_norm_g"], g["gmlp_out_norm_g"], g["gmlp_ln_g"],
     g["gmlp_ln_b"], g["gmlp_w_s"], dmix_sum, dsinks) = _mix_core_bwd(dyy, *mix_args, name="mix_core_bwd")
    g["gmlp_b_s"] = dmix_sum
    g["attn_sinks"] = dsinks
    dx1, dw_in_t, g["b_in"], g["mix_norm_g"] = _mix_in_bwd(
        x1, dx2, dq, dk, dv, dz, p["mix_norm_g"], w_in_t, tile=tile, name="mix_in_bwd")
    mix_grads = _mix_grads_pack(dw_in_t, dw_out, name="mix_grads_pack")

    dhp1, land = _ffn_bwd(place, hb1, a1, b1, dx1, pack_a, 0, land, mix_grads, tile=bwd_tile, name="ffn1_bwd")
    dx0, g["ffn1_norm_g"] = _norm_bwd(dhp1, x, dx1, p["ffn1_norm_g"], tile=norm_tile, name="ffn1_norm_bwd")
    return loss, dx0, land, g


def _pack_cast(place, parts, *, name):
    def body(place_ref, *refs):
        oa_ref, ob_ref = refs[-2], refs[-1]
        off = 0
        for k, (ref, rows) in enumerate(zip(refs[:-2], BIG_ROWS)):
            if k == 3:
                off = 0
            (oa_ref if k < 3 else ob_ref)[0, off:off + rows, :] = ref[...].astype(BF16)
            off += rows

    one = pl.Buffered(1)

    def slab(rows):
        return pl.BlockSpec((1, rows, D_MODEL), lambda i, pr: (pr[0], 0, 0), pipeline_mode=one)

    grid_spec = pltpu.PrefetchScalarGridSpec(
        num_scalar_prefetch=1, grid=(1,),
        in_specs=[pl.BlockSpec((rows, D_MODEL), lambda i, pr: (0, 0), pipeline_mode=one) for rows in BIG_ROWS],
        out_specs=[slab(PACK_A_ROWS), slab(PACK_B_ROWS)])
    return pl.pallas_call(
        body, name=name, grid_spec=grid_spec,
        out_shape=[jax.ShapeDtypeStruct((N_CHIPS, PACK_A_ROWS, D_MODEL), BF16),
                   jax.ShapeDtypeStruct((N_CHIPS, PACK_B_ROWS, D_MODEL), BF16)],
        compiler_params=_cparams(("arbitrary",)),
    )(place, *parts)


def _all_gather_pack(pack, *, name):
    def body(p_ref, o_ref, send_sems, recv_sems):
        start, forward, finish = _gather_stages(o_ref, send_sems, recv_sems)
        start()
        forward()
        finish()

    return pl.pallas_call(
        body, name=name,
        in_specs=[pl.BlockSpec(memory_space=pl.ANY)],
        out_specs=pl.BlockSpec(memory_space=pl.ANY),
        out_shape=jax.ShapeDtypeStruct(pack.shape, pack.dtype),
        input_output_aliases={0: 0},
        scratch_shapes=[pltpu.SemaphoreType.DMA((6,)), pltpu.SemaphoreType.DMA((6,))],
    )(pack)


def _shard_tile(i, c):
    return jnp.where(i < 3, 3 * c + i, jnp.where(i < 6, 3 + 3 * c + i, 12 + c))


def _rs_reduce(place, land, *, name):
    def body(place_ref, l_ref, o_ref):
        acc = l_ref[0].astype(F32)
        for d in range(1, 2 * N_CHIPS):
            acc = acc + l_ref[d].astype(F32)
        o_ref[...] = acc

    grid_spec = pltpu.PrefetchScalarGridSpec(
        num_scalar_prefetch=1, grid=(HALF_ROWS // MIX_HALF,),
        in_specs=[pl.BlockSpec((2 * N_CHIPS, MIX_HALF, D_MODEL), lambda i, pr: (0, i, 0))],
        out_specs=pl.BlockSpec((MIX_HALF, D_MODEL), lambda i, pr: (_shard_tile(i, pr[1]), 0)))
    return pl.pallas_call(
        body, name=name, grid_spec=grid_spec,
        out_shape=jax.ShapeDtypeStruct((PACK_ROWS, D_MODEL), F32),
        compiler_params=_cparams(("arbitrary",)),
    )(place, land)


def _rs_share(shard, *, name):
    def body(s_ref, o_ref, send_sems, recv_sems):
        x, y, c, _ = _mesh_place()

        def rows(k, core):
            if k < 2:
                return o_ref.at[pl.ds(pl.multiple_of(k * 2 * FFN_HALF + core * FFN_HALF, 8), FFN_HALF)]
            return o_ref.at[pl.ds(pl.multiple_of(4 * FFN_HALF + core * MIX_HALF, 8), MIX_HALF)]

        def copy(k, core):
            return pltpu.make_async_remote_copy(src_ref=rows(k, core), dst_ref=rows(k, core), send_sem=send_sems.at[k],
                                                recv_sem=recv_sems.at[k], device_id=(x, y, 1 - c),
                                                device_id_type=MESH)

        sends = [copy(k, c) for k in range(3)]
        for cp in sends:
            cp.start()
        for k in range(3):
            copy(k, 1 - c).wait_recv()
        for cp in sends:
            cp.wait_send()

    return pl.pallas_call(
        body, name=name,
        in_specs=[pl.BlockSpec(memory_space=pl.ANY)],
        out_specs=pl.BlockSpec(memory_space=pl.ANY),
        out_shape=jax.ShapeDtypeStruct((PACK_ROWS, D_MODEL), F32),
        input_output_aliases={0: 0},
        scratch_shapes=[pltpu.SemaphoreType.DMA((3,)), pltpu.SemaphoreType.DMA((3,))],
    )(shard)


def _small_all_reduce(packed, *, name):
    rows = packed.shape[0]

    def body(p_ref, o_ref, sib_ref, slots_ref, send_sems, recv_sems):
        x, y, c, others = _mesh_place()
        me = 2 * x + y
        sib = pltpu.make_async_remote_copy(src_ref=p_ref, dst_ref=sib_ref, send_sem=send_sems.at[0],
                                           recv_sem=recv_sems.at[0], device_id=(x, y, 1 - c), device_id_type=MESH)
        sib.start()
        sib.wait()
        slots_ref[me] = p_ref[...] + sib_ref[...]
        sends = [pltpu.make_async_remote_copy(
            src_ref=slots_ref.at[me], dst_ref=slots_ref.at[me], send_sem=send_sems.at[1 + j],
            recv_sem=recv_sems.at[1 + j], device_id=(px, py, c), device_id_type=MESH)
            for j, (px, py) in enumerate(others)]
        for cp in sends:
            cp.start()
        for j, (px, py) in enumerate(others):
            slab = slots_ref.at[2 * px + py]
            pltpu.make_async_remote_copy(src_ref=slab, dst_ref=slab, send_sem=send_sems.at[1 + j],
                                         recv_sem=recv_sems.at[1 + j], device_id=(px, py, c),
                                         device_id_type=MESH).wait_recv()
        for cp in sends:
            cp.wait_send()
        o_ref[...] = (slots_ref[0] + slots_ref[1]) + (slots_ref[2] + slots_ref[3])

    vm = pl.BlockSpec(memory_space=pltpu.VMEM)
    return pl.pallas_call(
        body, name=name, in_specs=[vm], out_specs=vm,
        out_shape=jax.ShapeDtypeStruct((rows, 128), F32),
        scratch_shapes=[pltpu.VMEM((rows, 128), F32), pltpu.VMEM((N_CHIPS, rows, 128), F32),
                        pltpu.SemaphoreType.DMA((4,)), pltpu.SemaphoreType.DMA((4,))],
    )(packed)


def _adamw(w, g, m, v, *, g_row0, tile, name):
    rows, cols = w.shape
    assert g_row0 % tile == 0 and rows % tile == 0

    def body(w_ref, g_ref, m_ref, v_ref, go_ref, d_ref, nm_ref, nv_ref):
        g_v = g_ref[...]
        m_n = ADAM_B1 * m_ref[...] + (1.0 - ADAM_B1) * g_v
        v_n = ADAM_B2 * v_ref[...] + (1.0 - ADAM_B2) * (g_v * g_v)
        m_hat = m_n / (1.0 - ADAM_B1 ** ADAM_STEP)
        v_hat = v_n / (1.0 - ADAM_B2 ** ADAM_STEP)
        d_ref[...] = -ADAM_LR * (m_hat / (jnp.sqrt(v_hat) + ADAM_EPS) + ADAM_WD * w_ref[...])
        go_ref[...] = g_v
        nm_ref[...] = m_n
        nv_ref[...] = v_n

    spec = pl.BlockSpec((tile, cols), lambda i: (i, 0))
    gspec = pl.BlockSpec((tile, cols), lambda i: (g_row0 // tile + i, 0))
    shape = jax.ShapeDtypeStruct((rows, cols), F32)
    return pl.pallas_call(
        body, name=name, grid=(rows // tile,),
        in_specs=[spec, gspec, spec, spec], out_specs=[spec] * 4, out_shape=[shape] * 4,
        compiler_params=_cparams(("arbitrary",)),
    )(w, g, m, v)


def kernel(x, ffn1_norm_g, ffn1_w_gate, ffn1_w_up, ffn1_w_down, mix_norm_g, w_in, b_in, attn_sinks, gmlp_ln_g, gmlp_ln_b, gmlp_w_s, gmlp_b_s, attn_out_norm_g, gmlp_out_norm_g, w_out, b_out, ffn2_norm_g, ffn2_w_gate, ffn2_w_up, ffn2_w_down, final_norm_g, loss_target, m_ffn1_norm_g, m_ffn1_w_gate, m_ffn1_w_up, m_ffn1_w_down, m_mix_norm_g, m_w_in, m_b_in, m_attn_sinks, m_gmlp_ln_g, m_gmlp_ln_b, m_gmlp_w_s, m_gmlp_b_s, m_attn_out_norm_g, m_gmlp_out_norm_g, m_w_out, m_b_out, m_ffn2_norm_g, m_ffn2_w_gate, m_ffn2_w_up, m_ffn2_w_down, m_final_norm_g, v_ffn1_norm_g, v_ffn1_w_gate, v_ffn1_w_up, v_ffn1_w_down, v_mix_norm_g, v_w_in, v_b_in, v_attn_sinks, v_gmlp_ln_g, v_gmlp_ln_b, v_gmlp_w_s, v_gmlp_b_s, v_attn_out_norm_g, v_gmlp_out_norm_g, v_w_out, v_b_out, v_ffn2_norm_g, v_ffn2_w_gate, v_ffn2_w_up, v_ffn2_w_down, v_final_norm_g):
    f_args = dict(locals())
    weights = {n: f_args[n] for n in [nm for nm, _ in SMALL if nm != "loss"] + list(BIG)}
    shapes = {n: weights[n].shape for n in weights}
    shapes["loss"] = ()
    place = jnp.stack([2 * lax.axis_index("x") + lax.axis_index("y"), lax.axis_index("c")]).astype(jnp.int32)

    def with_cols(name, a):
        a2 = a.reshape(a.shape[-2], a.shape[-1])
        return a2.T if BIG_TRANSPOSED[BIG.index(name)] else a2

    def natural(name, a2):
        return (a2.T if BIG_TRANSPOSED[BIG.index(name)] else a2).reshape(shapes[name])

    pack_a, pack_b = _pack_cast(place, [with_cols(n, weights[n]) for n in BIG], name="pack_cast")
    pack_a = _all_gather_pack(pack_a, name="ag_weights")
    p = {n: weights[n].reshape(1, -1) for n in ("ffn1_norm_g", "mix_norm_g", "b_in", "gmlp_ln_g", "gmlp_ln_b",
                                                "attn_out_norm_g", "gmlp_out_norm_g", "b_out", "ffn2_norm_g",
                                                "final_norm_g")}
    p["attn_sinks"] = attn_sinks.reshape(N_Q_HEADS)
    p["gmlp_w_s"] = gmlp_w_s.reshape(GMLP_GROUPS, BLK, BLK)
    p["bs_full"] = jnp.broadcast_to(gmlp_b_s.reshape(GMLP_GROUPS, BLK).T[:, :, None],
                                    (BLK, GMLP_GROUPS, GROUP_DIM)).reshape(BLK, GMLP_W)

    loss_part, dx0, land, gs = _local_step(place, x[0], loss_target[0], p, pack_a, pack_b)

    shard = _rs_share(_rs_reduce(place, land, name="rs_reduce"), name="rs_share")
    gs["gmlp_b_s"] = jnp.sum(gs["gmlp_b_s"].reshape(BLK, GMLP_GROUPS, GROUP_DIM), axis=-1).T
    gs["attn_sinks"] = gs["attn_sinks"][:, 0]
    gs["loss"] = loss_part[0, 0]
    small_sum = _small_all_reduce(_pack_small(gs), name="small_all_reduce")

    grad_w, delta, new_m, new_v = {}, {}, {}, {}
    off = 0
    for n, rows in zip(BIG, BIG_ROWS):
        res = _adamw(with_cols(n, weights[n]), shard, with_cols(n, f_args["m_" + n]), with_cols(n, f_args["v_" + n]),
                     g_row0=off, tile=FF_SH // 2 if rows == FF_SH else 64, name="adamw_" + n)
        grad_w[n], delta[n], new_m[n], new_v[n] = [natural(n, a) for a in res]
        off += rows
    sm = {k: {n: f_args[k + n] for n, _ in SMALL if n != "loss"} for k in ("", "m_", "v_")}
    for k in sm:
        sm[k]["loss"] = jnp.zeros((), F32)
    res = _adamw(_pack_small(sm[""]), small_sum, _pack_small(sm["m_"]), _pack_small(sm["v_"]),
                 g_row0=0, tile=SMALL_ROWS, name="adamw_small")
    small = _unpack_small(res[0], shapes)
    for dst, packed in ((grad_w, res[0]), (delta, res[1]), (new_m, res[2]), (new_v, res[3])):
        dst.update({n: a for n, a in _unpack_small(packed, shapes).items() if n != "loss"})

    order = ('ffn1_norm_g', 'ffn1_w_gate', 'ffn1_w_up', 'ffn1_w_down', 'mix_norm_g', 'w_in', 'b_in', 'attn_sinks',
             'gmlp_ln_g', 'gmlp_ln_b', 'gmlp_w_s', 'gmlp_b_s', 'attn_out_norm_g', 'gmlp_out_norm_g', 'w_out', 'b_out',
             'ffn2_norm_g', 'ffn2_w_gate', 'ffn2_w_up', 'ffn2_w_down', 'final_norm_g')
    return (small["loss"], dx0.reshape(x.shape), *[grad_w[n] for n in order], *[delta[n] for n in order],
            *[new_m[n] for n in order], *[new_v[n] for n in order])
```

```python
import functools

import jax
import jax.numpy as jnp
from jax import lax
from jax.experimental import pallas as pl
from jax.experimental.pallas import tpu as pltpu

F32 = jnp.float32
BF16 = jnp.bfloat16

D_MODEL = 1024
D_FF = 2816
N_CHIPS = 4
FF_SH = D_FF // N_CHIPS
N_Q_HEADS = 8
N_KV_HEADS = 2
REP = N_Q_HEADS // N_KV_HEADS
HEAD_DIM = 64
ATTN_W = 512
KV_W = 128
GMLP_W = 512
GMLP_GROUPS = 8
GROUP_DIM = 64
BLK = 128
MIX_FWD_BLOCKS = 2
MIX_BWD_BLOCKS = 4
IN_W = 1792
IN_SH = IN_W // N_CHIPS
OUT_SH = D_MODEL // N_CHIPS
EPS = 1e-6
FFN_RES = 0.5
ATTN_SCALE = HEAD_DIM ** -0.5

ADAM_LR = 0.001
ADAM_B1 = 0.9
ADAM_B2 = 0.999
ADAM_EPS = 1e-08
ADAM_WD = 0.01
ADAM_STEP = 10

V7X_VMEM_LIMIT = 56 * 1024 * 1024
MESH = pl.DeviceIdType.MESH


def _cparams(sem):
    return pltpu.CompilerParams(dimension_semantics=sem, vmem_limit_bytes=V7X_VMEM_LIMIT)


def _dot(a, b):
    return jnp.dot(a, b, preferred_element_type=F32)


def _dot_nt(a, b):
    return lax.dot_general(a, b, (((1,), (1,)), ((), ())), preferred_element_type=F32)


def _dot_tn(a, b):
    return lax.dot_general(a, b, (((0,), (0,)), ((), ())), preferred_element_type=F32)


def _rms(x, g):
    r = lax.rsqrt(jnp.mean(x * x, axis=-1, keepdims=True) + EPS)
    return x * r * g, r


def _rms_bwd(dh, x, r, g):
    gy = dh * g
    dx = r * gy - x * (r * r * r) * jnp.mean(gy * x, axis=-1, keepdims=True)
    dg = jnp.sum(dh * x * r, axis=0, keepdims=True)
    return dx, dg


def _const(shape):
    nd = len(shape)
    return pl.BlockSpec(shape, lambda *_: (0,) * nd)


def _rows(t, w):
    return pl.BlockSpec((t, w), lambda i: (i, 0))


PACK_ROWS = 7 * FF_SH
HALF_ROWS = PACK_ROWS // 2
FFN_HALF = 3 * FF_SH // 2
MIX_HALF = FF_SH // 2
PACK_A_ROWS = 3 * FF_SH
PACK_B_ROWS = 4 * FF_SH
BIG = ("ffn1_w_gate", "ffn1_w_up", "ffn1_w_down", "ffn2_w_gate", "ffn2_w_up", "ffn2_w_down", "w_in", "w_out")
BIG_ROWS = (FF_SH, FF_SH, FF_SH, FF_SH, FF_SH, FF_SH, IN_SH, OUT_SH)
BIG_TRANSPOSED = (True, True, False, True, True, False, True, False)

SMALL = (("ffn1_norm_g", 1024), ("mix_norm_g", 1024), ("b_in", 1792), ("attn_sinks", 8), ("gmlp_ln_g", 512),
         ("gmlp_ln_b", 512), ("gmlp_w_s", 131072), ("gmlp_b_s", 1024), ("attn_out_norm_g", 512),
         ("gmlp_out_norm_g", 512), ("b_out", 1024), ("ffn2_norm_g", 1024), ("final_norm_g", 1024), ("loss", 1))


def _small_rows(n):
    return -(-n // 1024) * 8


SMALL_ROWS = sum(_small_rows(n) for _, n in SMALL)


def _pack_small(parts):
    out = []
    for name, n in SMALL:
        flat = parts[name].reshape(-1).astype(F32)
        rows = _small_rows(n)
        out.append(jnp.pad(flat, (0, rows * 128 - n)).reshape(rows, 128))
    return jnp.concatenate(out, axis=0)


def _unpack_small(packed, shapes):
    res, off = {}, 0
    for name, n in SMALL:
        rows = _small_rows(n)
        res[name] = packed[off:off + rows].reshape(-1)[:n].reshape(shapes[name])
        off += rows
    return res


def _ffn_tile(x, g, wg_ref, wu_ref, wd_ref, hb_ref, a_ref, b_ref):
    h, _ = _rms(x, g)
    hb = h.astype(BF16)
    hb_ref[...] = hb
    acc = jnp.zeros(x.shape, F32)
    for j in range(N_CHIPS):
        a = _dot_nt(hb, wg_ref[j])
        b = _dot_nt(hb, wu_ref[j])
        a_ref[j] = a
        b_ref[j] = b
        f = (a * jax.nn.sigmoid(a) * b).astype(BF16)
        acc = acc + _dot(f, wd_ref[j])
    return x + FFN_RES * acc


def _ffn_saved_specs(s, tile):
    ab = pl.BlockSpec((N_CHIPS, tile, FF_SH), lambda i: (0, i, 0))
    shape = jax.ShapeDtypeStruct((N_CHIPS, s, FF_SH), F32)
    return [_rows(tile, D_MODEL), ab, ab], [jax.ShapeDtypeStruct((s, D_MODEL), BF16), shape, shape]


def _ffn_weight_specs(k0):
    one = pl.Buffered(1)
    return [pl.BlockSpec((N_CHIPS, FF_SH, D_MODEL), functools.partial(lambda kk, i: (0, kk, 0), k0 + d),
                         pipeline_mode=one) for d in range(3)]


def _mesh_place():
    x, y, c = lax.axis_index("x"), lax.axis_index("y"), lax.axis_index("c")
    others = [(1 - x, y), (x, 1 - y), (1 - x, 1 - y)]
    return x, y, c, others


def _gather_stages(o_ref, send_sems, recv_sems):
    x, y, c, others = _mesh_place()
    me = 2 * x + y
    sibling = (x, y, 1 - c)
    half_rows = o_ref.shape[1] // 2

    def half(slab, core):
        return o_ref.at[slab, pl.ds(pl.multiple_of(core * half_rows, 16), half_rows)]

    def copy(k, rows, to):
        return pltpu.make_async_remote_copy(src_ref=rows, dst_ref=rows, send_sem=send_sems.at[k],
                                            recv_sem=recv_sems.at[k], device_id=to, device_id_type=MESH)

    first = [copy(j, half(me, c), (px, py, c)) for j, (px, py) in enumerate(others)]
    passed = [copy(3 + j, half(2 * px + py, c), sibling) for j, (px, py) in enumerate(others)]

    def start():
        for cp in first:
            cp.start()

    def forward():
        for j, (px, py) in enumerate(others):
            copy(j, half(2 * px + py, c), (px, py, c)).wait_recv()
            passed[j].start()

    def finish():
        for j, (px, py) in enumerate(others):
            copy(3 + j, half(2 * px + py, 1 - c), sibling).wait_recv()
        for cp in first + passed:
            cp.wait_send()

    return start, forward, finish


def _ffn_fwd(x, g, pack, k0, gather, *, tile, name):
    s = x.shape[0]
    nt = s // tile
    forward_at = max(nt - 6, 0)

    def body(x_ref, g_ref, wg_ref, wu_ref, wd_ref, gin_ref, o_ref, hb_ref, a_ref, b_ref, gat_ref, send_sems, recv_sems):
        i = pl.program_id(0)
        start, forward, finish = _gather_stages(gat_ref, send_sems, recv_sems)
        pl.when(i == 0)(start)
        o_ref[...] = _ffn_tile(x_ref[...], g_ref[...], wg_ref, wu_ref, wd_ref, hb_ref, a_ref, b_ref)
        pl.when(i == forward_at)(forward)
        pl.when(i == nt - 1)(finish)

    saved_specs, saved_shapes = _ffn_saved_specs(s, tile)
    hbm = pl.BlockSpec(memory_space=pl.ANY)
    return pl.pallas_call(
        body, name=name, grid=(nt,),
        in_specs=[_rows(tile, D_MODEL), _const((1, D_MODEL))] + _ffn_weight_specs(k0) + [hbm],
        out_specs=[_rows(tile, D_MODEL)] + saved_specs + [hbm],
        out_shape=[jax.ShapeDtypeStruct(x.shape, F32)] + saved_shapes
                  + [jax.ShapeDtypeStruct(gather.shape, gather.dtype)],
        input_output_aliases={5: 4},
        scratch_shapes=[pltpu.SemaphoreType.DMA((6,)), pltpu.SemaphoreType.DMA((6,))],
        compiler_params=_cparams(("arbitrary",)),
    )(x, g, pack, pack, pack, gather)


def _ffn_fwd_loss(x, g, pack, k0, gf, tgt, *, tile, name):
    s = x.shape[0]

    def body(x_ref, g_ref, wg_ref, wu_ref, wd_ref, gf_ref, t_ref, dx_ref, loss_ref, dgf_ref, hb_ref, a_ref, b_ref):
        @pl.when(pl.program_id(0) == 0)
        def _():
            loss_ref[...] = jnp.zeros_like(loss_ref)
            dgf_ref[...] = jnp.zeros_like(dgf_ref)

        x3 = _ffn_tile(x_ref[...], g_ref[...], wg_ref, wu_ref, wd_ref, hb_ref, a_ref, b_ref)
        gf_v = gf_ref[...]
        out, r = _rms(x3, gf_v)
        diff = out - t_ref[...]
        part = jnp.sum(jnp.sum(diff * diff, axis=-1, keepdims=True), axis=0, keepdims=True)
        loss_ref[...] += jnp.broadcast_to(part * (0.5 / D_MODEL), loss_ref.shape)
        dx, dg = _rms_bwd(diff * (1.0 / D_MODEL), x3, r, gf_v)
        dx_ref[...] = dx
        dgf_ref[...] += dg

    saved_specs, saved_shapes = _ffn_saved_specs(s, tile)
    return pl.pallas_call(
        body, name=name, grid=(s // tile,),
        in_specs=[_rows(tile, D_MODEL), _const((1, D_MODEL))] + _ffn_weight_specs(k0)
                 + [_const((1, D_MODEL)), _rows(tile, D_MODEL)],
        out_specs=[_rows(tile, D_MODEL), _const((1, 128)), _const((1, D_MODEL))] + saved_specs,
        out_shape=[jax.ShapeDtypeStruct(x.shape, F32),
                   jax.ShapeDtypeStruct((1, 128), F32),
                   jax.ShapeDtypeStruct((1, D_MODEL), F32)] + saved_shapes,
        compiler_params=_cparams(("arbitrary",)),
    )(x, g, pack, pack, pack, gf, tgt)


def _ffn_bwd(place, hb, a, b, dy, pack, region, land, mix_grads, *, tile, name):
    s = hb.shape[0]
    nt = s // tile
    land_rows = pl.ds(region * FFN_HALF, FFN_HALF)
    mix_rows = pl.ds(2 * FFN_HALF, MIX_HALF)
    with_mix = mix_grads is not None
    with_land = land is not None
    n_others = 2 * N_CHIPS - 1

    def body(place_ref, hb_ref, a_ref, b_ref, dy_ref, wg_ref, wu_ref, wd_ref, *rest):
        rest = list(rest)
        mix_ref = rest.pop(0) if with_mix else None
        if with_land:
            rest.pop(0)
        dhp_ref, land_ref, acc_ref, stage_ref, send_sems, recv_sem, local_sem = rest[:7]
        t, i = pl.program_id(0), pl.program_id(1)
        xi, yi, c = lax.axis_index("x"), lax.axis_index("y"), lax.axis_index("c")
        dev = 4 * xi + 2 * yi + c
        tt = (t + 1) % N_CHIPS
        tx, ty = jnp.bitwise_xor(xi, tt // 2), jnp.bitwise_xor(yi, tt % 2)

        def remote(src, dst, ssem, rsem, to):
            return pltpu.make_async_remote_copy(src_ref=src, dst_ref=dst, send_sem=ssem, recv_sem=rsem,
                                                device_id=to, device_id_type=MESH)

        def stage_half(h):
            return stage_ref.at[pl.ds(pl.multiple_of(h * FFN_HALF, 16), FFN_HALF)]

        if with_mix:
            mix_send, mix_recv, mix_local = rest[7:10]

            @pl.when(jnp.logical_and(t == 0, i == 0))
            def _():
                for chip in range(N_CHIPS):
                    for h in range(2):
                        src = mix_ref.at[chip, pl.ds(h * MIX_HALF, MIX_HALF)]
                        dst = land_ref.at[dev, mix_rows]
                        mine = jnp.logical_and(2 * xi + yi == chip, c == h)

                        @pl.when(mine)
                        def _():
                            pltpu.make_async_copy(src, dst, mix_local).start()

                        @pl.when(jnp.logical_not(mine))
                        def _():
                            remote(src, dst, mix_send, mix_recv, (chip // 2, chip % 2, h)).start()

        @pl.when(i == 0)
        def _():
            acc_ref[...] = jnp.zeros_like(acc_ref)

        hb = hb_ref[...]
        dob = (FFN_RES * dy_ref[...]).astype(BF16)
        wg_j, wu_j, wd_j = wg_ref[0], wu_ref[0], wd_ref[0]
        a = a_ref[0]
        b = b_ref[0]
        sg = jax.nn.sigmoid(a)
        sa = a * sg
        fb = (sa * b).astype(BF16)
        df = _dot_nt(dob, wd_j)
        dbb = (df * sa).astype(BF16)
        dab = (df * b * (sg + sa * (1.0 - sg))).astype(BF16)
        dhp_ref[0] = _dot(dab, wg_j) + _dot(dbb, wu_j)
        acc_ref[0:FF_SH, :] += _dot_tn(dab, hb)
        acc_ref[FF_SH:2 * FF_SH, :] += _dot_tn(dbb, hb)
        acc_ref[2 * FF_SH:3 * FF_SH, :] += _dot_tn(fb, dob)

        @pl.when(i == nt - 1)
        def _():
            dst = land_ref.at[dev, land_rows]

            @pl.when(t > 0)
            def _():
                for h in range(2):
                    remote(stage_half(h), dst, send_sems.at[h], recv_sem, (tx, ty, h)).wait_send()

            def cast_rows(r, carry):
                rows = pl.ds(pl.multiple_of(r * MIX_HALF, 16), MIX_HALF)
                stage_ref[rows, :] = acc_ref[rows, :].astype(BF16)
                return carry

            lax.fori_loop(0, 3 * FF_SH // MIX_HALF, cast_rows, 0)

            @pl.when(t < N_CHIPS - 1)
            def _():
                for h in range(2):
                    remote(stage_half(h), dst, send_sems.at[h], recv_sem, (tx, ty, h)).start()

            @pl.when(t == N_CHIPS - 1)
            def _():
                own = pltpu.make_async_copy(stage_half(c), dst, local_sem)
                own.start()
                sib = remote(stage_half(1 - c), dst, send_sems.at[0], recv_sem, (xi, yi, 1 - c))
                sib.start()
                sib.wait_send()
                own.wait()
                arrivals = land_ref.at[pl.ds(0, n_others), land_rows]
                remote(arrivals, arrivals, send_sems.at[0], recv_sem, (xi, yi, 1 - c)).wait_recv()
                if with_mix:
                    seven = land_ref.at[pl.ds(0, n_others), mix_rows]
                    both = remote(seven, seven, mix_send, mix_recv, (xi, yi, 1 - c))
                    both.wait_send()
                    both.wait_recv()
                    pltpu.make_async_copy(mix_ref.at[0, pl.ds(0, MIX_HALF)], land_ref.at[dev, mix_rows],
                                          mix_local).wait()

    def wspec(kk):
        return pl.BlockSpec((1, FF_SH, D_MODEL),
                            lambda t, i, pr: (jnp.bitwise_xor(pr[0], (t + 1) % N_CHIPS), kk, 0))

    xspec = pl.BlockSpec((tile, D_MODEL), lambda t, i, pr: (i, 0))
    abspec = pl.BlockSpec((1, tile, FF_SH), lambda t, i, pr: (jnp.bitwise_xor(pr[0], (t + 1) % N_CHIPS), i, 0))
    hbm = pl.BlockSpec(memory_space=pl.ANY)
    in_specs = [xspec, abspec, abspec, xspec, wspec(0), wspec(1), wspec(2)]
    operands = [place, hb, a, b, dy, pack, pack, pack]
    scratch = [pltpu.VMEM((3 * FF_SH, D_MODEL), F32), pltpu.VMEM((3 * FF_SH, D_MODEL), BF16),
               pltpu.SemaphoreType.DMA((2,)), pltpu.SemaphoreType.DMA, pltpu.SemaphoreType.DMA]
    if with_mix:
        in_specs.append(hbm)
        operands.append(mix_grads)
        scratch += [pltpu.SemaphoreType.DMA, pltpu.SemaphoreType.DMA, pltpu.SemaphoreType.DMA]
    aliases = {}
    if with_land:
        in_specs.append(hbm)
        operands.append(land)
        aliases = {len(operands) - 1: 1}
    grid_spec = pltpu.PrefetchScalarGridSpec(
        num_scalar_prefetch=1, grid=(N_CHIPS, nt), in_specs=in_specs,
        out_specs=[pl.BlockSpec((1, tile, D_MODEL), lambda t, i, pr: (t, i, 0)), hbm],
        scratch_shapes=scratch)
    return pl.pallas_call(
        body, name=name, grid_spec=grid_spec,
        out_shape=[jax.ShapeDtypeStruct((N_CHIPS, s, D_MODEL), F32),
                   jax.ShapeDtypeStruct((2 * N_CHIPS, HALF_ROWS, D_MODEL), BF16)],
        input_output_aliases=aliases,
        compiler_params=_cparams(("arbitrary", "arbitrary")),
    )(*operands)


def _mix_grads_pack(dw_in_t, dw_out, *, name):
    def body(a_ref, b_ref, o_ref):
        o_ref[0, 0:IN_SH, :] = a_ref[0].astype(BF16)
        o_ref[0, IN_SH:FF_SH, :] = b_ref[0].astype(BF16)

    return pl.pallas_call(
        body, name=name, grid=(N_CHIPS,),
        in_specs=[pl.BlockSpec((1, IN_SH, D_MODEL), lambda j: (j, 0, 0)),
                  pl.BlockSpec((1, OUT_SH, D_MODEL), lambda j: (j, 0, 0))],
        out_specs=pl.BlockSpec((1, FF_SH, D_MODEL), lambda j: (j, 0, 0)),
        out_shape=jax.ShapeDtypeStruct((N_CHIPS, FF_SH, D_MODEL), BF16),
        compiler_params=_cparams(("arbitrary",)),
    )(dw_in_t.reshape(N_CHIPS, IN_SH, D_MODEL), dw_out.reshape(N_CHIPS, OUT_SH, D_MODEL))


def _norm_bwd(dhp, x, dy, g, *, tile, name):
    s = x.shape[0]

    def body(dhp_ref, x_ref, dy_ref, g_ref, dx_ref, dg_ref):
        @pl.when(pl.program_id(0) == 0)
        def _():
            dg_ref[...] = jnp.zeros_like(dg_ref)

        dh = (dhp_ref[0] + dhp_ref[1]) + (dhp_ref[2] + dhp_ref[3])
        x_v = x_ref[...]
        r = lax.rsqrt(jnp.mean(x_v * x_v, axis=-1, keepdims=True) + EPS)
        dx, dg = _rms_bwd(dh, x_v, r, g_ref[...])
        dx_ref[...] = dy_ref[...] + dx
        dg_ref[...] += dg

    return pl.pallas_call(
        body, name=name, grid=(s // tile,),
        in_specs=[pl.BlockSpec((N_CHIPS, tile, D_MODEL), lambda i: (0, i, 0)),
                  _rows(tile, D_MODEL), _rows(tile, D_MODEL), _const((1, D_MODEL))],
        out_specs=[_rows(tile, D_MODEL), _const((1, D_MODEL))],
        out_shape=[jax.ShapeDtypeStruct(x.shape, F32), jax.ShapeDtypeStruct((1, D_MODEL), F32)],
        compiler_params=_cparams(("arbitrary",)),
    )(dhp, x, dy, g)


def _mix_in_fwd(x, g, w_in_t, b_in, *, tile, name):
    s = x.shape[0]

    def body(x_ref, g_ref, w_ref, b_ref, q_ref, k_ref, v_ref, z_ref):
        h, _ = _rms(x_ref[...], g_ref[...])
        proj = _dot_nt(h.astype(BF16), w_ref[...]) + b_ref[...]
        q_ref[...] = proj[:, :ATTN_W].astype(BF16)
        k_ref[...] = proj[:, ATTN_W:ATTN_W + KV_W].astype(BF16)
        v_ref[...] = proj[:, ATTN_W + KV_W:ATTN_W + 2 * KV_W].astype(BF16)
        z_ref[...] = proj[:, ATTN_W + 2 * KV_W:]

    return pl.pallas_call(
        body, name=name, grid=(s // tile,),
        in_specs=[_rows(tile, D_MODEL), _const((1, D_MODEL)), _const((IN_W, D_MODEL)), _const((1, IN_W))],
        out_specs=[_rows(tile, ATTN_W), _rows(tile, KV_W), _rows(tile, KV_W), _rows(tile, 2 * GMLP_W)],
        out_shape=[jax.ShapeDtypeStruct((s, ATTN_W), BF16), jax.ShapeDtypeStruct((s, KV_W), BF16),
                   jax.ShapeDtypeStruct((s, KV_W), BF16), jax.ShapeDtypeStruct((s, 2 * GMLP_W), F32)],
        compiler_params=_cparams(("arbitrary",)),
    )(x, g, w_in_t, b_in)


def _mix_in_bwd(x, dy, dq, dk, dv, dz, g, w_in_t, *, tile, name):
    s = x.shape[0]

    def body(x_ref, dy_ref, dq_ref, dk_ref, dv_ref, dz_ref, g_ref, w_ref, dx_ref, dw_ref, db_ref, dg_ref):
        @pl.when(pl.program_id(0) == 0)
        def _():
            dw_ref[...] = jnp.zeros_like(dw_ref)
            db_ref[...] = jnp.zeros_like(db_ref)
            dg_ref[...] = jnp.zeros_like(dg_ref)

        dproj = jnp.concatenate([dq_ref[...], dk_ref[...], dv_ref[...], dz_ref[...]], axis=-1)
        db_ref[...] += jnp.sum(dproj, axis=0, keepdims=True)
        dpb = dproj.astype(BF16)
        x_v = x_ref[...]
        g_v = g_ref[...]
        h, r = _rms(x_v, g_v)
        dw_ref[...] += _dot_tn(dpb, h.astype(BF16))
        dh = _dot(dpb, w_ref[...])
        dx, dg = _rms_bwd(dh, x_v, r, g_v)
        dx_ref[...] = dy_ref[...] + dx
        dg_ref[...] += dg

    return pl.pallas_call(
        body, name=name, grid=(s // tile,),
        in_specs=[_rows(tile, D_MODEL), _rows(tile, D_MODEL), _rows(tile, ATTN_W), _rows(tile, KV_W),
                  _rows(tile, KV_W), _rows(tile, 2 * GMLP_W), _const((1, D_MODEL)), _const((IN_W, D_MODEL))],
        out_specs=[_rows(tile, D_MODEL), _const((IN_W, D_MODEL)), _const((1, IN_W)), _const((1, D_MODEL))],
        out_shape=[jax.ShapeDtypeStruct(x.shape, F32), jax.ShapeDtypeStruct((IN_W, D_MODEL), F32),
                   jax.ShapeDtypeStruct((1, IN_W), F32), jax.ShapeDtypeStruct((1, D_MODEL), F32)],
        compiler_params=_cparams(("arbitrary",)),
    )(x, dy, dq, dk, dv, dz, g, w_in_t)


_GELU_C = 0.7978845608028654
_GELU_A = 0.044715


def _gelu_tanh(x):
    x2 = x * x
    return jnp.tanh(_GELU_C * (x + _GELU_A * (x2 * x))), x2


def _band(ref, i):
    prev = jnp.maximum(i - 1, 0)
    return jnp.concatenate([ref[pl.ds(pl.multiple_of(prev * BLK, BLK), BLK), :],
                            ref[pl.ds(pl.multiple_of(i * BLK, BLK), BLK), :]], axis=0)


def _fold(band, tri):
    return jnp.where(tri, band[:, BLK:], band[:, :BLK])


def _unfold(a, tri):
    zero = jnp.zeros_like(a)
    return jnp.concatenate([jnp.where(tri, zero, a), jnp.where(tri, a, zero)], axis=-1).astype(BF16)


def _attn_probs(qh, kbg, tri, has_prev, sink):
    s2 = _dot_nt(qh, kbg)
    sc = jnp.where(tri, s2[:, BLK:], jnp.where(has_prev, s2[:, :BLK], -jnp.inf)) * ATTN_SCALE
    m = jnp.maximum(jnp.max(sc, axis=-1, keepdims=True), sink)
    p = jnp.exp(sc - m)
    es = jnp.exp(sink - m)
    inv = 1.0 / (jnp.sum(p, axis=-1, keepdims=True) + es)
    return p * inv, es * inv


def _attn_fwd(q, kb, vb, i, sink_ref):
    tri = _tril_mask()
    outs, saved = [], []
    for h in range(N_Q_HEADS):
        cols = slice((h // REP) * HEAD_DIM, (h // REP + 1) * HEAD_DIM)
        pn, psink = _attn_probs(q[:, h * HEAD_DIM:(h + 1) * HEAD_DIM], kb[:, cols], tri, i > 0, sink_ref[h])
        outs.append(_dot(_unfold(pn, tri), vb[:, cols]))
        saved.append((pn, psink))
    return jnp.concatenate(outs, axis=-1), saved


def _tril_mask():
    t = lax.broadcasted_iota(jnp.int32, (BLK, BLK), 0)
    s_ = lax.broadcasted_iota(jnp.int32, (BLK, BLK), 1)
    return s_ <= t


def _gmlp_fwd_parts(zg, lng, lnb, ws_ref, bs_full):
    th, zg2 = _gelu_tanh(zg)
    z = 0.5 * zg * (1.0 + th)
    u = z[:, :GMLP_W]
    zv = z[:, GMLP_W:]
    mu = jnp.mean(zv, axis=-1, keepdims=True)
    zc = zv - mu
    rstd = lax.rsqrt(jnp.mean(zc * zc, axis=-1, keepdims=True) + EPS)
    xh = zc * rstd
    vvb = (xh * lng + lnb).astype(BF16)
    tril = _tril_mask()
    wms, parts = [], []
    for gi in range(GMLP_GROUPS):
        wm = jnp.where(tril, ws_ref[gi], 0.0).astype(BF16)
        wms.append(wm)
        parts.append(_dot(wm, vvb[:, gi * GROUP_DIM:(gi + 1) * GROUP_DIM]))
    mixed = jnp.concatenate(parts, axis=-1) + bs_full
    gelu_grad = 0.5 * (1.0 + th) + 0.5 * zg * (1.0 - th * th) * (_GELU_C * (1.0 + 3.0 * _GELU_A * zg2))
    return u, xh, rstd, vvb, wms, mixed, gelu_grad


def _mix_core_fwd(q, k, v, zg, sinks, lng, lnb, w_s, bs_full, gao, ggo, *, name):
    s = q.shape[0]
    nb = min(MIX_FWD_BLOCKS, s // BLK)

    def body(sink_ref, q_ref, k_ref, v_ref, z_ref, lng_ref, lnb_ref, ws_ref, bs_ref, gao_ref, ggo_ref, o_ref):
        for b in range(nb):
            blk = pl.program_id(0) * nb + b
            rows = slice(b * BLK, (b + 1) * BLK)
            y_attn, _ = _attn_fwd(q_ref[rows, :], _band(k_ref, blk), _band(v_ref, blk), blk, sink_ref)
            u, _, _, _, _, mixed, _ = _gmlp_fwd_parts(z_ref[rows, :], lng_ref[...], lnb_ref[...], ws_ref,
                                                      bs_ref[...])
            ya, _ = _rms(y_attn, gao_ref[...])
            yg, _ = _rms(u * mixed, ggo_ref[...])
            o_ref[rows, :] = jnp.concatenate([ya, yg], axis=-1).astype(BF16)

    return pl.pallas_call(
        body, name=name, grid=(s // (nb * BLK),),
        in_specs=[pl.BlockSpec(memory_space=pltpu.SMEM),
                  _rows(nb * BLK, ATTN_W), _const((s, KV_W)), _const((s, KV_W)),
                  _rows(nb * BLK, 2 * GMLP_W), _const((1, GMLP_W)), _const((1, GMLP_W)),
                  _const((GMLP_GROUPS, BLK, BLK)), _const((BLK, GMLP_W)), _const((1, ATTN_W)), _const((1, GMLP_W))],
        out_specs=_rows(nb * BLK, D_MODEL),
        out_shape=jax.ShapeDtypeStruct((s, D_MODEL), BF16),
        compiler_params=_cparams(("arbitrary",)),
    )(sinks, q, k, v, zg, lng, lnb, w_s, bs_full, gao, ggo)


def _mix_out_fwd(x1, yb, w_out, b_out, *, tile, name):
    s = x1.shape[0]

    def body(x_ref, y_ref, w_ref, b_ref, o_ref):
        o_ref[...] = x_ref[...] + (_dot(y_ref[...], w_ref[...]) + b_ref[...])

    return pl.pallas_call(
        body, name=name, grid=(s // tile,),
        in_specs=[_rows(tile, D_MODEL), _rows(tile, D_MODEL), _const((D_MODEL, D_MODEL)), _const((1, D_MODEL))],
        out_specs=_rows(tile, D_MODEL),
        out_shape=jax.ShapeDtypeStruct(x1.shape, F32),
        compiler_params=_cparams(("arbitrary",)),
    )(x1, yb, w_out, b_out)


def _norm_bwd_mix_out(dhp, x, dy, g, yb, w_out, *, tile, name):
    s = x.shape[0]

    def body(dhp_ref, x_ref, dy_ref, g_ref, y_ref, w_ref, dx_ref, dg_ref, dyy_ref, dw_ref, db_ref):
        @pl.when(pl.program_id(0) == 0)
        def _():
            dg_ref[...] = jnp.zeros_like(dg_ref)
            dw_ref[...] = jnp.zeros_like(dw_ref)
            db_ref[...] = jnp.zeros_like(db_ref)

        dh = (dhp_ref[0] + dhp_ref[1]) + (dhp_ref[2] + dhp_ref[3])
        x_v = x_ref[...]
        r = lax.rsqrt(jnp.mean(x_v * x_v, axis=-1, keepdims=True) + EPS)
        dxn, dg = _rms_bwd(dh, x_v, r, g_ref[...])
        dx = dy_ref[...] + dxn
        dx_ref[...] = dx
        dg_ref[...] += dg
        dxb = dx.astype(BF16)
        db_ref[...] += jnp.sum(dx, axis=0, keepdims=True)
        dw_ref[...] += _dot_tn(y_ref[...], dxb)
        dyy_ref[...] = _dot_nt(dxb, w_ref[...])

    return pl.pallas_call(
        body, name=name, grid=(s // tile,),
        in_specs=[pl.BlockSpec((N_CHIPS, tile, D_MODEL), lambda i: (0, i, 0)),
                  _rows(tile, D_MODEL), _rows(tile, D_MODEL), _const((1, D_MODEL)), _rows(tile, D_MODEL),
                  _const((D_MODEL, D_MODEL))],
        out_specs=[_rows(tile, D_MODEL), _const((1, D_MODEL)), _rows(tile, D_MODEL), _const((D_MODEL, D_MODEL)),
                   _const((1, D_MODEL))],
        out_shape=[jax.ShapeDtypeStruct(x.shape, F32), jax.ShapeDtypeStruct((1, D_MODEL), F32),
                   jax.ShapeDtypeStruct(x.shape, F32), jax.ShapeDtypeStruct((D_MODEL, D_MODEL), F32),
                   jax.ShapeDtypeStruct((1, D_MODEL), F32)],
        compiler_params=_cparams(("arbitrary",)),
    )(dhp, x, dy, g, yb, w_out)


def _mix_core_bwd(dyy, q, k, v, zg, sinks, lng, lnb, w_s, bs_full, gao, ggo, *, name):
    s = dyy.shape[0]
    nb = min(MIX_BWD_BLOCKS, s // BLK)
    nsteps = s // (nb * BLK)

    def body(*refs):
        accumulators = refs[13:15] + refs[16:]

        @pl.when(pl.program_id(0) == 0)
        def _():
            for ref in accumulators:
                ref[...] = jnp.zeros_like(ref)

        for b in range(nb):
            one_block(pl.program_id(0) * nb + b, slice(b * BLK, (b + 1) * BLK), *refs)

        @pl.when(pl.program_id(0) == nsteps - 1)
        def _():
            tril = _tril_mask()
            for gi in range(GMLP_GROUPS):
                refs[20][gi] = jnp.where(tril, refs[20][gi], 0.0)

    def one_block(i, rows, sink_ref, dyy_ref, q_ref, k_ref, v_ref, z_ref, lng_ref, lnb_ref, ws_ref, bs_ref, gao_ref,
                  ggo_ref, dq_ref, dk_ref, dv_ref, dz_ref, dgao_ref, dggo_ref, dlng_ref, dlnb_ref, dws_ref, dms_ref,
                  dsk_ref):
        q_v = q_ref[rows, :]
        kb = _band(k_ref, i)
        vb = _band(v_ref, i)
        lng_v = lng_ref[...]
        gao_v = gao_ref[...]
        ggo_v = ggo_ref[...]

        y_attn, probs = _attn_fwd(q_v, kb, vb, i, sink_ref)
        u, xh, rstd, vvb, wms, mixed, gelu_grad = _gmlp_fwd_parts(z_ref[rows, :], lng_v, lnb_ref[...], ws_ref,
                                                                  bs_ref[...])
        y_gmlp = u * mixed
        ra = lax.rsqrt(jnp.mean(y_attn * y_attn, axis=-1, keepdims=True) + EPS)
        rg = lax.rsqrt(jnp.mean(y_gmlp * y_gmlp, axis=-1, keepdims=True) + EPS)

        dyy = dyy_ref[rows, :]
        d_attn, dgao = _rms_bwd(dyy[:, :ATTN_W], y_attn, ra, gao_v)
        d_gmlp, dggo = _rms_bwd(dyy[:, ATTN_W:], y_gmlp, rg, ggo_v)
        dgao_ref[...] += dgao
        dggo_ref[...] += dggo

        du = d_gmlp * mixed
        dmixed = d_gmlp * u
        dms_ref[...] += dmixed
        dmb = dmixed.astype(BF16)
        dvv_parts = []
        for gi in range(GMLP_GROUPS):
            sl = slice(gi * GROUP_DIM, (gi + 1) * GROUP_DIM)
            dws_ref[gi] += _dot_nt(dmb[:, sl], vvb[:, sl])
            dvv_parts.append(_dot_tn(wms[gi], dmb[:, sl]))
        dvv = jnp.concatenate(dvv_parts, axis=-1)
        dlng_ref[...] += jnp.sum(dvv * xh, axis=0, keepdims=True)
        dlnb_ref[...] += jnp.sum(dvv, axis=0, keepdims=True)
        dxh = dvv * lng_v
        dzv = rstd * (dxh - jnp.mean(dxh, axis=-1, keepdims=True)
                      - xh * jnp.mean(dxh * xh, axis=-1, keepdims=True))
        dz_ref[rows, :] = jnp.concatenate([du, dzv], axis=-1) * gelu_grad

        dab = d_attn.astype(BF16)
        tri = _tril_mask()
        dq_parts = []
        dk_parts = []
        dv_parts = []
        for gi in range(N_KV_HEADS):
            cols = slice(gi * HEAD_DIM, (gi + 1) * HEAD_DIM)
            kg, vg = kb[:, cols], vb[:, cols]
            dkg = jnp.zeros((2 * BLK, HEAD_DIM), F32)
            dvg = jnp.zeros((2 * BLK, HEAD_DIM), F32)
            for rr in range(REP):
                h = gi * REP + rr
                hs = slice(h * HEAD_DIM, (h + 1) * HEAD_DIM)
                qh, doh = q_v[:, hs], dab[:, hs]
                pn, psink = probs[h]
                dp = _fold(_dot_nt(doh, vg), tri)
                delta = jnp.sum(pn * dp, axis=-1, keepdims=True)
                ds2 = _unfold(pn * (dp - delta) * ATTN_SCALE, tri)
                dsink = jnp.sum(-psink * delta, axis=0, keepdims=True)
                dsk_ref[pl.ds(h, 1), :] += jnp.broadcast_to(dsink, (1, 128))
                dq_parts.append(_dot(ds2, kg))
                dkg = dkg + _dot_tn(ds2, qh)
                dvg = dvg + _dot_tn(_unfold(pn, tri), doh)
            dk_parts.append(dkg)
            dv_parts.append(dvg)
        dq_ref[rows, :] = jnp.concatenate(dq_parts, axis=-1)
        dkb = jnp.concatenate(dk_parts, axis=-1)
        dvb = jnp.concatenate(dv_parts, axis=-1)
        prev = pl.ds(pl.multiple_of(jnp.maximum(i - 1, 0) * BLK, BLK), BLK)
        cur = pl.ds(pl.multiple_of(i * BLK, BLK), BLK)
        dk_ref[prev, :] += dkb[:BLK]
        dv_ref[prev, :] += dvb[:BLK]
        dk_ref[cur, :] += dkb[BLK:]
        dv_ref[cur, :] += dvb[BLK:]

    return pl.pallas_call(
        body, name=name, grid=(nsteps,),
        in_specs=[pl.BlockSpec(memory_space=pltpu.SMEM),
                  _rows(nb * BLK, D_MODEL), _rows(nb * BLK, ATTN_W), _const((s, KV_W)), _const((s, KV_W)),
                  _rows(nb * BLK, 2 * GMLP_W), _const((1, GMLP_W)), _const((1, GMLP_W)),
                  _const((GMLP_GROUPS, BLK, BLK)), _const((BLK, GMLP_W)), _const((1, ATTN_W)), _const((1, GMLP_W))],
        out_specs=[_rows(nb * BLK, ATTN_W), _const((s, KV_W)), _const((s, KV_W)), _rows(nb * BLK, 2 * GMLP_W),
                   _const((1, ATTN_W)), _const((1, GMLP_W)),
                   _const((1, GMLP_W)), _const((1, GMLP_W)), _const((GMLP_GROUPS, BLK, BLK)),
                   _const((BLK, GMLP_W)), _const((N_Q_HEADS, 128))],
        out_shape=[jax.ShapeDtypeStruct((s, ATTN_W), F32), jax.ShapeDtypeStruct((s, KV_W), F32),
                   jax.ShapeDtypeStruct((s, KV_W), F32), jax.ShapeDtypeStruct((s, 2 * GMLP_W), F32),
                   jax.ShapeDtypeStruct((1, ATTN_W), F32), jax.ShapeDtypeStruct((1, GMLP_W), F32),
                   jax.ShapeDtypeStruct((1, GMLP_W), F32), jax.ShapeDtypeStruct((1, GMLP_W), F32),
                   jax.ShapeDtypeStruct((GMLP_GROUPS, BLK, BLK), F32), jax.ShapeDtypeStruct((BLK, GMLP_W), F32),
                   jax.ShapeDtypeStruct((N_Q_HEADS, 128), F32)],
        compiler_params=_cparams(("arbitrary",)),
    )(sinks, dyy, q, k, v, zg, lng, lnb, w_s, bs_full, gao, ggo)


def _local_step(place, x, tgt, p, pack_a, pack_b, *, tile=512, fwd_tile=256, bwd_tile=512, norm_tile=256):
    g = {}
    tile, fwd_tile, bwd_tile, norm_tile = (min(t_, x.shape[0]) for t_ in (tile, fwd_tile, bwd_tile, norm_tile))
    x1, hb1, a1, b1, pack_b = _ffn_fwd(x, p["ffn1_norm_g"], pack_a, 0, pack_b, tile=fwd_tile, name="ffn1_fwd")
    mix_rows = pack_b[:, 3 * FF_SH:, :]
    w_in_t = mix_rows[:, :IN_SH, :].reshape(IN_W, D_MODEL)
    w_out = mix_rows[:, IN_SH:, :].reshape(D_MODEL, D_MODEL)
    q, k, v, zg = _mix_in_fwd(x1, p["mix_norm_g"], w_in_t, p["b_in"], tile=tile, name="mix_in_fwd")
    mix_args = (q, k, v, zg, p["attn_sinks"], p["gmlp_ln_g"], p["gmlp_ln_b"], p["gmlp_w_s"], p["bs_full"],
                p["attn_out_norm_g"], p["gmlp_out_norm_g"])
    yb = _mix_core_fwd(*mix_args, name="mix_core_fwd")
    x2 = _mix_out_fwd(x1, yb, w_out, p["b_out"], tile=tile, name="mix_out_fwd")
    dx3, loss, g["final_norm_g"], hb2, a2, b2 = _ffn_fwd_loss(
        x2, p["ffn2_norm_g"], pack_b, 0, p["final_norm_g"], tgt, tile=fwd_tile, name="ffn2_fwd_loss")

    dhp, land = _ffn_bwd(place, hb2, a2, b2, dx3, pack_b, 1, None, None, tile=bwd_tile, name="ffn2_bwd")
    dx2, g["ffn2_norm_g"], dyy, dw_out, g["b_out"] = _norm_bwd_mix_out(
        dhp, x2, dx3, p["ffn2_norm_g"], yb, w_out, tile=norm_tile, name="ffn2_norm_bwd")

    (dq, dk, dv, dz, g["attn_out_norm_g"], g["gmlp_out_norm_g"], g["gmlp_ln_g"],
     g["gmlp_ln_b"], g["gmlp_w_s"], dmix_sum, dsinks) = _mix_core_bwd(dyy, *mix_args, name="mix_core_bwd")
    g["gmlp_b_s"] = dmix_sum
    g["attn_sinks"] = dsinks
    dx1, dw_in_t, g["b_in"], g["mix_norm_g"] = _mix_in_bwd(
        x1, dx2, dq, dk, dv, dz, p["mix_norm_g"], w_in_t, tile=tile, name="mix_in_bwd")
    mix_grads = _mix_grads_pack(dw_in_t, dw_out, name="mix_grads_pack")

    dhp1, land = _ffn_bwd(place, hb1, a1, b1, dx1, pack_a, 0, land, mix_grads, tile=bwd_tile, name="ffn1_bwd")
    dx0, g["ffn1_norm_g"] = _norm_bwd(dhp1, x, dx1, p["ffn1_norm_g"], tile=norm_tile, name="ffn1_norm_bwd")
    return loss, dx0, land, g


def _pack_cast(place, parts, *, name):
    def body(place_ref, *refs):
        oa_ref, ob_ref = refs[-2], refs[-1]
        off = 0
        for k, (ref, rows) in enumerate(zip(refs[:-2], BIG_ROWS)):
            if k == 3:
                off = 0
            (oa_ref if k < 3 else ob_ref)[0, off:off + rows, :] = ref[...].astype(BF16)
            off += rows

    one = pl.Buffered(1)

    def slab(rows):
        return pl.BlockSpec((1, rows, D_MODEL), lambda i, pr: (pr[0], 0, 0), pipeline_mode=one)

    grid_spec = pltpu.PrefetchScalarGridSpec(
        num_scalar_prefetch=1, grid=(1,),
        in_specs=[pl.BlockSpec((rows, D_MODEL), lambda i, pr: (0, 0), pipeline_mode=one) for rows in BIG_ROWS],
        out_specs=[slab(PACK_A_ROWS), slab(PACK_B_ROWS)])
    return pl.pallas_call(
        body, name=name, grid_spec=grid_spec,
        out_shape=[jax.ShapeDtypeStruct((N_CHIPS, PACK_A_ROWS, D_MODEL), BF16),
                   jax.ShapeDtypeStruct((N_CHIPS, PACK_B_ROWS, D_MODEL), BF16)],
        compiler_params=_cparams(("arbitrary",)),
    )(place, *parts)


def _all_gather_pack(pack, *, name):
    def body(p_ref, o_ref, send_sems, recv_sems):
        start, forward, finish = _gather_stages(o_ref, send_sems, recv_sems)
        start()
        forward()
        finish()

    return pl.pallas_call(
        body, name=name,
        in_specs=[pl.BlockSpec(memory_space=pl.ANY)],
        out_specs=pl.BlockSpec(memory_space=pl.ANY),
        out_shape=jax.ShapeDtypeStruct(pack.shape, pack.dtype),
        input_output_aliases={0: 0},
        scratch_shapes=[pltpu.SemaphoreType.DMA((6,)), pltpu.SemaphoreType.DMA((6,))],
    )(pack)


def _shard_tile(i, c):
    return jnp.where(i < 3, 3 * c + i, jnp.where(i < 6, 3 + 3 * c + i, 12 + c))


def _rs_reduce(place, land, *, name):
    def body(place_ref, l_ref, o_ref):
        acc = l_ref[0].astype(F32)
        for d in range(1, 2 * N_CHIPS):
            acc = acc + l_ref[d].astype(F32)
        o_ref[...] = acc

    grid_spec = pltpu.PrefetchScalarGridSpec(
        num_scalar_prefetch=1, grid=(HALF_ROWS // MIX_HALF,),
        in_specs=[pl.BlockSpec((2 * N_CHIPS, MIX_HALF, D_MODEL), lambda i, pr: (0, i, 0))],
        out_specs=pl.BlockSpec((MIX_HALF, D_MODEL), lambda i, pr: (_shard_tile(i, pr[1]), 0)))
    return pl.pallas_call(
        body, name=name, grid_spec=grid_spec,
        out_shape=jax.ShapeDtypeStruct((PACK_ROWS, D_MODEL), F32),
        compiler_params=_cparams(("arbitrary",)),
    )(place, land)


def _rs_share(shard, *, name):
    def body(s_ref, o_ref, send_sems, recv_sems):
        x, y, c, _ = _mesh_place()

        def rows(k, core):
            if k < 2:
                return o_ref.at[pl.ds(pl.multiple_of(k * 2 * FFN_HALF + core * FFN_HALF, 8), FFN_HALF)]
            return o_ref.at[pl.ds(pl.multiple_of(4 * FFN_HALF + core * MIX_HALF, 8), MIX_HALF)]

        def copy(k, core):
            return pltpu.make_async_remote_copy(src_ref=rows(k, core), dst_ref=rows(k, core), send_sem=send_sems.at[k],
                                                recv_sem=recv_sems.at[k], device_id=(x, y, 1 - c),
                                                device_id_type=MESH)

        sends = [copy(k, c) for k in range(3)]
        for cp in sends:
            cp.start()
        for k in range(3):
            copy(k, 1 - c).wait_recv()
        for cp in sends:
            cp.wait_send()

    return pl.pallas_call(
        body, name=name,
        in_specs=[pl.BlockSpec(memory_space=pl.ANY)],
        out_specs=pl.BlockSpec(memory_space=pl.ANY),
        out_shape=jax.ShapeDtypeStruct((PACK_ROWS, D_MODEL), F32),
        input_output_aliases={0: 0},
        scratch_shapes=[pltpu.SemaphoreType.DMA((3,)), pltpu.SemaphoreType.DMA((3,))],
    )(shard)


def _small_all_reduce(packed, *, name):
    rows = packed.shape[0]

    def body(p_ref, o_ref, sib_ref, slots_ref, send_sems, recv_sems):
        x, y, c, others = _mesh_place()
        me = 2 * x + y
        sib = pltpu.make_async_remote_copy(src_ref=p_ref, dst_ref=sib_ref, send_sem=send_sems.at[0],
                                           recv_sem=recv_sems.at[0], device_id=(x, y, 1 - c), device_id_type=MESH)
        sib.start()
        sib.wait()
        slots_ref[me] = p_ref[...] + sib_ref[...]
        sends = [pltpu.make_async_remote_copy(
            src_ref=slots_ref.at[me], dst_ref=slots_ref.at[me], send_sem=send_sems.at[1 + j],
            recv_sem=recv_sems.at[1 + j], device_id=(px, py, c), device_id_type=MESH)
            for j, (px, py) in enumerate(others)]
        for cp in sends:
            cp.start()
        for j, (px, py) in enumerate(others):
            slab = slots_ref.at[2 * px + py]
            pltpu.make_async_remote_copy(src_ref=slab, dst_ref=slab, send_sem=send_sems.at[1 + j],
                                         recv_sem=recv_sems.at[1 + j], device_id=(px, py, c),
                                         device_id_type=MESH).wait_recv()
        for cp in sends:
            cp.wait_send()
        o_ref[...] = (slots_ref[0] + slots_ref[1]) + (slots_ref[2] + slots_ref[3])

    vm = pl.BlockSpec(memory_space=pltpu.VMEM)
    return pl.pallas_call(
        body, name=name, in_specs=[vm], out_specs=vm,
        out_shape=jax.ShapeDtypeStruct((rows, 128), F32),
        scratch_shapes=[pltpu.VMEM((rows, 128), F32), pltpu.VMEM((N_CHIPS, rows, 128), F32),
                        pltpu.SemaphoreType.DMA((4,)), pltpu.SemaphoreType.DMA((4,))],
    )(packed)


def _adamw(w, g, m, v, *, g_row0, tile, name):
    rows, cols = w.shape
    assert g_row0 % tile == 0 and rows % tile == 0

    def body(w_ref, g_ref, m_ref, v_ref, go_ref, d_ref, nm_ref, nv_ref):
        g_v = g_ref[...]
        m_n = ADAM_B1 * m_ref[...] + (1.0 - ADAM_B1) * g_v
        v_n = ADAM_B2 * v_ref[...] + (1.0 - ADAM_B2) * (g_v * g_v)
        m_hat = m_n / (1.0 - ADAM_B1 ** ADAM_STEP)
        v_hat = v_n / (1.0 - ADAM_B2 ** ADAM_STEP)
        d_ref[...] = -ADAM_LR * (m_hat / (jnp.sqrt(v_hat) + ADAM_EPS) + ADAM_WD * w_ref[...])
        go_ref[...] = g_v
        nm_ref[...] = m_n
        nv_ref[...] = v_n

    spec = pl.BlockSpec((tile, cols), lambda i: (i, 0))
    gspec = pl.BlockSpec((tile, cols), lambda i: (g_row0 // tile + i, 0))
    shape = jax.ShapeDtypeStruct((rows, cols), F32)
    return pl.pallas_call(
        body, name=name, grid=(rows // tile,),
        in_specs=[spec, gspec, spec, spec], out_specs=[spec] * 4, out_shape=[shape] * 4,
        compiler_params=_cparams(("arbitrary",)),
    )(w, g, m, v)


def kernel(x, ffn1_norm_g, ffn1_w_gate, ffn1_w_up, ffn1_w_down, mix_norm_g, w_in, b_in, attn_sinks, gmlp_ln_g, gmlp_ln_b, gmlp_w_s, gmlp_b_s, attn_out_norm_g, gmlp_out_norm_g, w_out, b_out, ffn2_norm_g, ffn2_w_gate, ffn2_w_up, ffn2_w_down, final_norm_g, loss_target, m_ffn1_norm_g, m_ffn1_w_gate, m_ffn1_w_up, m_ffn1_w_down, m_mix_norm_g, m_w_in, m_b_in, m_attn_sinks, m_gmlp_ln_g, m_gmlp_ln_b, m_gmlp_w_s, m_gmlp_b_s, m_attn_out_norm_g, m_gmlp_out_norm_g, m_w_out, m_b_out, m_ffn2_norm_g, m_ffn2_w_gate, m_ffn2_w_up, m_ffn2_w_down, m_final_norm_g, v_ffn1_norm_g, v_ffn1_w_gate, v_ffn1_w_up, v_ffn1_w_down, v_mix_norm_g, v_w_in, v_b_in, v_attn_sinks, v_gmlp_ln_g, v_gmlp_ln_b, v_gmlp_w_s, v_gmlp_b_s, v_attn_out_norm_g, v_gmlp_out_norm_g, v_w_out, v_b_out, v_ffn2_norm_g, v_ffn2_w_gate, v_ffn2_w_up, v_ffn2_w_down, v_final_norm_g):
    f_args = dict(locals())
    weights = {n: f_args[n] for n in [nm for nm, _ in SMALL if nm != "loss"] + list(BIG)}
    shapes = {n: weights[n].shape for n in weights}
    shapes["loss"] = ()
    place = jnp.stack([2 * lax.axis_index("x") + lax.axis_index("y"), lax.axis_index("c")]).astype(jnp.int32)

    def with_cols(name, a):
        a2 = a.reshape(a.shape[-2], a.shape[-1])
        return a2.T if BIG_TRANSPOSED[BIG.index(name)] else a2

    def natural(name, a2):
        return (a2.T if BIG_TRANSPOSED[BIG.index(name)] else a2).reshape(shapes[name])

    pack_a, pack_b = _pack_cast(place, [with_cols(n, weights[n]) for n in BIG], name="pack_cast")
    pack_a = _all_gather_pack(pack_a, name="ag_weights")
    p = {n: weights[n].reshape(1, -1) for n in ("ffn1_norm_g", "mix_norm_g", "b_in", "gmlp_ln_g", "gmlp_ln_b",
                                                "attn_out_norm_g", "gmlp_out_norm_g", "b_out", "ffn2_norm_g",
                                                "final_norm_g")}
    p["attn_sinks"] = attn_sinks.reshape(N_Q_HEADS)
    p["gmlp_w_s"] = gmlp_w_s.reshape(GMLP_GROUPS, BLK, BLK)
    p["bs_full"] = jnp.broadcast_to(gmlp_b_s.reshape(GMLP_GROUPS, BLK).T[:, :, None],
                                    (BLK, GMLP_GROUPS, GROUP_DIM)).reshape(BLK, GMLP_W)

    loss_part, dx0, land, gs = _local_step(place, x[0], loss_target[0], p, pack_a, pack_b)

    shard = _rs_share(_rs_reduce(place, land, name="rs_reduce"), name="rs_share")
    gs["gmlp_b_s"] = jnp.sum(gs["gmlp_b_s"].reshape(BLK, GMLP_GROUPS, GROUP_DIM), axis=-1).T
    gs["attn_sinks"] = gs["attn_sinks"][:, 0]
    gs["loss"] = loss_part[0, 0]
    small_sum = _small_all_reduce(_pack_small(gs), name="small_all_reduce")

    grad_w, delta, new_m, new_v = {}, {}, {}, {}
    off = 0
    for n, rows in zip(BIG, BIG_ROWS):
        res = _adamw(with_cols(n, weights[n]), shard, with_cols(n, f_args["m_" + n]), with_cols(n, f_args["v_" + n]),
                     g_row0=off, tile=FF_SH // 2 if rows == FF_SH else 64, name="adamw_" + n)
        grad_w[n], delta[n], new_m[n], new_v[n] = [natural(n, a) for a in res]
        off += rows
    sm = {k: {n: f_args[k + n] for n, _ in SMALL if n != "loss"} for k in ("", "m_", "v_")}
    for k in sm:
        sm[k]["loss"] = jnp.zeros((), F32)
    res = _adamw(_pack_small(sm[""]), small_sum, _pack_small(sm["m_"]), _pack_small(sm["v_"]),
                 g_row0=0, tile=SMALL_ROWS, name="adamw_small")
    small = _unpack_small(res[0], shapes)
    for dst, packed in ((grad_w, res[0]), (delta, res[1]), (new_m, res[2]), (new_v, res[3])):
        dst.update({n: a for n, a in _unpack_small(packed, shapes).items() if n != "loss"})

    order = ('ffn1_norm_g', 'ffn1_w_gate', 'ffn1_w_up', 'ffn1_w_down', 'mix_norm_g', 'w_in', 'b_in', 'attn_sinks',
             'gmlp_ln_g', 'gmlp_ln_b', 'gmlp_w_s', 'gmlp_b_s', 'attn_out_norm_g', 'gmlp_out_norm_g', 'w_out', 'b_out',
             'ffn2_norm_g', 'ffn2_w_gate', 'ffn2_w_up', 'ffn2_w_down', 'final_norm_g')
    return (small["loss"], dx0.reshape(x.shape), *[grad_w[n] for n in order], *[delta[n] for n in order],
            *[new_m[n] for n in order], *[new_v[n] for n in order])
```

```python
import functools

import jax
import jax.numpy as jnp
from jax import lax
from jax.experimental import pallas as pl
from jax.experimental.pallas import tpu as pltpu

F32 = jnp.float32
BF16 = jnp.bfloat16

D_MODEL = 1024
D_FF = 2816
N_CHIPS = 4
FF_SH = D_FF // N_CHIPS
N_Q_HEADS = 8
N_KV_HEADS = 2
REP = N_Q_HEADS // N_KV_HEADS
HEAD_DIM = 64
ATTN_W = 512
KV_W = 128
GMLP_W = 512
GMLP_GROUPS = 8
GROUP_DIM = 64
BLK = 128
MIX_FWD_BLOCKS = 2
MIX_BWD_BLOCKS = 4
IN_W = 1792
IN_SH = IN_W // N_CHIPS
OUT_SH = D_MODEL // N_CHIPS
EPS = 1e-6
FFN_RES = 0.5
ATTN_SCALE = HEAD_DIM ** -0.5

ADAM_LR = 0.001
ADAM_B1 = 0.9
ADAM_B2 = 0.999
ADAM_EPS = 1e-08
ADAM_WD = 0.01
ADAM_STEP = 10

V7X_VMEM_LIMIT = 56 * 1024 * 1024
MESH = pl.DeviceIdType.MESH


def _cparams(sem):
    return pltpu.CompilerParams(dimension_semantics=sem, vmem_limit_bytes=V7X_VMEM_LIMIT)


def _dot(a, b):
    return jnp.dot(a, b, preferred_element_type=F32)


def _dot_nt(a, b):
    return lax.dot_general(a, b, (((1,), (1,)), ((), ())), preferred_element_type=F32)


def _dot_tn(a, b):
    return lax.dot_general(a, b, (((0,), (0,)), ((), ())), preferred_element_type=F32)


def _rms(x, g):
    r = lax.rsqrt(jnp.mean(x * x, axis=-1, keepdims=True) + EPS)
    return x * r * g, r


def _rms_bwd(dh, x, r, g):
    gy = dh * g
    dx = r * gy - x * (r * r * r) * jnp.mean(gy * x, axis=-1, keepdims=True)
    dg = jnp.sum(dh * x * r, axis=0, keepdims=True)
    return dx, dg


def _const(shape):
    nd = len(shape)
    return pl.BlockSpec(shape, lambda *_: (0,) * nd)


def _rows(t, w):
    return pl.BlockSpec((t, w), lambda i: (i, 0))


PACK_ROWS = 7 * FF_SH
HALF_ROWS = PACK_ROWS // 2
FFN_HALF = 3 * FF_SH // 2
MIX_HALF = FF_SH // 2
PACK_A_ROWS = 3 * FF_SH
PACK_B_ROWS = 4 * FF_SH
BIG = ("ffn1_w_gate", "ffn1_w_up", "ffn1_w_down", "ffn2_w_gate", "ffn2_w_up", "ffn2_w_down", "w_in", "w_out")
BIG_ROWS = (FF_SH, FF_SH, FF_SH, FF_SH, FF_SH, FF_SH, IN_SH, OUT_SH)
BIG_TRANSPOSED = (True, True, False, True, True, False, True, False)

SMALL = (("ffn1_norm_g", 1024), ("mix_norm_g", 1024), ("b_in", 1792), ("attn_sinks", 8), ("gmlp_ln_g", 512),
         ("gmlp_ln_b", 512), ("gmlp_w_s", 131072), ("gmlp_b_s", 1024), ("attn_out_norm_g", 512),
         ("gmlp_out_norm_g", 512), ("b_out", 1024), ("ffn2_norm_g", 1024), ("final_norm_g", 1024), ("loss", 1))


def _small_rows(n):
    return -(-n // 1024) * 8


SMALL_USED_ROWS = sum(_small_rows(n) for _, n in SMALL)
SMALL_ROWS = -(-SMALL_USED_ROWS // 16) * 16


def _pack_small(parts):
    out = []
    for name, n in SMALL:
        flat = parts[name].reshape(-1).astype(F32)
        rows = _small_rows(n)
        out.append(jnp.pad(flat, (0, rows * 128 - n)).reshape(rows, 128))
    if SMALL_ROWS > SMALL_USED_ROWS:
        out.append(jnp.zeros((SMALL_ROWS - SMALL_USED_ROWS, 128), F32))
    return jnp.concatenate(out, axis=0)


def _unpack_small(packed, shapes):
    res, off = {}, 0
    for name, n in SMALL:
        rows = _small_rows(n)
        res[name] = packed[off:off + rows].reshape(-1)[:n].reshape(shapes[name])
        off += rows
    return res


def _ffn_tile(x, g, wg_ref, wu_ref, wd_ref, hb_ref, a_ref, b_ref):
    h, _ = _rms(x, g)
    hb = h.astype(BF16)
    hb_ref[...] = hb
    acc = jnp.zeros(x.shape, F32)
    for j in range(N_CHIPS):
        a = _dot_nt(hb, wg_ref[j])
        b = _dot_nt(hb, wu_ref[j])
        a_ref[j] = a
        b_ref[j] = b
        f = (a * jax.nn.sigmoid(a) * b).astype(BF16)
        acc = acc + _dot(f, wd_ref[j])
    return x + FFN_RES * acc


def _ffn_saved_specs(s, tile):
    ab = pl.BlockSpec((N_CHIPS, tile, FF_SH), lambda i: (0, i, 0))
    shape = jax.ShapeDtypeStruct((N_CHIPS, s, FF_SH), F32)
    return [_rows(tile, D_MODEL), ab, ab], [jax.ShapeDtypeStruct((s, D_MODEL), BF16), shape, shape]


def _ffn_weight_specs(k0):
    one = pl.Buffered(1)
    return [pl.BlockSpec((N_CHIPS, FF_SH, D_MODEL), functools.partial(lambda kk, i: (0, kk, 0), k0 + d),
                         pipeline_mode=one) for d in range(3)]


def _mesh_place():
    x, y, c = lax.axis_index("x"), lax.axis_index("y"), lax.axis_index("c")
    others = [(1 - x, y), (x, 1 - y), (1 - x, 1 - y)]
    return x, y, c, others


def _gather_stages(o_ref, send_sems, recv_sems):
    x, y, c, others = _mesh_place()
    me = 2 * x + y
    sibling = (x, y, 1 - c)
    half_rows = o_ref.shape[1] // 2

    def half(slab, core):
        return o_ref.at[slab, pl.ds(pl.multiple_of(core * half_rows, 16), half_rows)]

    def copy(k, rows, to):
        return pltpu.make_async_remote_copy(src_ref=rows, dst_ref=rows, send_sem=send_sems.at[k],
                                            recv_sem=recv_sems.at[k], device_id=to, device_id_type=MESH)

    first = [copy(j, half(me, c), (px, py, c)) for j, (px, py) in enumerate(others)]
    passed = [copy(3 + j, half(2 * px + py, c), sibling) for j, (px, py) in enumerate(others)]

    def start():
        for cp in first:
            cp.start()

    def forward():
        for j, (px, py) in enumerate(others):
            copy(j, half(2 * px + py, c), (px, py, c)).wait_recv()
            passed[j].start()

    def finish():
        for j, (px, py) in enumerate(others):
            copy(3 + j, half(2 * px + py, 1 - c), sibling).wait_recv()
        for cp in first + passed:
            cp.wait_send()

    return start, forward, finish


def _ffn_fwd(x, g, pack, k0, gather, *, tile, name):
    s = x.shape[0]
    nt = s // tile
    forward_at = max(nt - 6, 0)

    def body(x_ref, g_ref, wg_ref, wu_ref, wd_ref, gin_ref, o_ref, hb_ref, a_ref, b_ref, gat_ref, send_sems, recv_sems):
        i = pl.program_id(0)
        start, forward, finish = _gather_stages(gat_ref, send_sems, recv_sems)
        pl.when(i == 0)(start)
        o_ref[...] = _ffn_tile(x_ref[...], g_ref[...], wg_ref, wu_ref, wd_ref, hb_ref, a_ref, b_ref)
        pl.when(i == forward_at)(forward)
        pl.when(i == nt - 1)(finish)

    saved_specs, saved_shapes = _ffn_saved_specs(s, tile)
    hbm = pl.BlockSpec(memory_space=pl.ANY)
    return pl.pallas_call(
        body, name=name, grid=(nt,),
        in_specs=[_rows(tile, D_MODEL), _const((1, D_MODEL))] + _ffn_weight_specs(k0) + [hbm],
        out_specs=[_rows(tile, D_MODEL)] + saved_specs + [hbm],
        out_shape=[jax.ShapeDtypeStruct(x.shape, F32)] + saved_shapes
                  + [jax.ShapeDtypeStruct(gather.shape, gather.dtype)],
        input_output_aliases={5: 4},
        scratch_shapes=[pltpu.SemaphoreType.DMA((6,)), pltpu.SemaphoreType.DMA((6,))],
        compiler_params=_cparams(("arbitrary",)),
    )(x, g, pack, pack, pack, gather)


def _ffn_fwd_loss(x, g, pack, k0, gf, tgt, *, tile, name):
    s = x.shape[0]

    def body(x_ref, g_ref, wg_ref, wu_ref, wd_ref, gf_ref, t_ref, dx_ref, loss_ref, dgf_ref, hb_ref, a_ref, b_ref):
        @pl.when(pl.program_id(0) == 0)
        def _():
            loss_ref[...] = jnp.zeros_like(loss_ref)
            dgf_ref[...] = jnp.zeros_like(dgf_ref)

        x3 = _ffn_tile(x_ref[...], g_ref[...], wg_ref, wu_ref, wd_ref, hb_ref, a_ref, b_ref)
        gf_v = gf_ref[...]
        out, r = _rms(x3, gf_v)
        diff = out - t_ref[...]
        part = jnp.sum(jnp.sum(diff * diff, axis=-1, keepdims=True), axis=0, keepdims=True)
        loss_ref[...] += jnp.broadcast_to(part * (0.5 / D_MODEL), loss_ref.shape)
        dx, dg = _rms_bwd(diff * (1.0 / D_MODEL), x3, r, gf_v)
        dx_ref[...] = dx
        dgf_ref[...] += dg

    saved_specs, saved_shapes = _ffn_saved_specs(s, tile)
    return pl.pallas_call(
        body, name=name, grid=(s // tile,),
        in_specs=[_rows(tile, D_MODEL), _const((1, D_MODEL))] + _ffn_weight_specs(k0)
                 + [_const((1, D_MODEL)), _rows(tile, D_MODEL)],
        out_specs=[_rows(tile, D_MODEL), _const((1, 128)), _const((1, D_MODEL))] + saved_specs,
        out_shape=[jax.ShapeDtypeStruct(x.shape, F32),
                   jax.ShapeDtypeStruct((1, 128), F32),
                   jax.ShapeDtypeStruct((1, D_MODEL), F32)] + saved_shapes,
        compiler_params=_cparams(("arbitrary",)),
    )(x, g, pack, pack, pack, gf, tgt)


def _ffn_bwd(place, hb, a, b, dy, pack, region, land, mix_grads, *, tile, name):
    s = hb.shape[0]
    nt = s // tile
    land_rows = pl.ds(region * FFN_HALF, FFN_HALF)
    mix_rows = pl.ds(2 * FFN_HALF, MIX_HALF)
    with_mix = mix_grads is not None
    with_land = land is not None
    n_others = 2 * N_CHIPS - 1

    def body(place_ref, hb_ref, a_ref, b_ref, dy_ref, wg_ref, wu_ref, wd_ref, *rest):
        rest = list(rest)
        mix_ref = rest.pop(0) if with_mix else None
        if with_land:
            rest.pop(0)
        dhp_ref, land_ref, acc_ref, stage_ref, send_sems, recv_sem, local_sem = rest[:7]
        t, i = pl.program_id(0), pl.program_id(1)
        xi, yi, c = lax.axis_index("x"), lax.axis_index("y"), lax.axis_index("c")
        dev = 4 * xi + 2 * yi + c
        tt = (t + 1) % N_CHIPS
        tx, ty = jnp.bitwise_xor(xi, tt // 2), jnp.bitwise_xor(yi, tt % 2)

        def remote(src, dst, ssem, rsem, to):
            return pltpu.make_async_remote_copy(src_ref=src, dst_ref=dst, send_sem=ssem, recv_sem=rsem,
                                                device_id=to, device_id_type=MESH)

        def stage_half(h):
            return stage_ref.at[pl.ds(pl.multiple_of(h * FFN_HALF, 16), FFN_HALF)]

        if with_mix:
            mix_send, mix_recv, mix_local = rest[7:10]

            @pl.when(jnp.logical_and(t == 0, i == 0))
            def _():
                for chip in range(N_CHIPS):
                    for h in range(2):
                        src = mix_ref.at[chip, pl.ds(h * MIX_HALF, MIX_HALF)]
                        dst = land_ref.at[dev, mix_rows]
                        mine = jnp.logical_and(2 * xi + yi == chip, c == h)

                        @pl.when(mine)
                        def _():
                            pltpu.make_async_copy(src, dst, mix_local).start()

                        @pl.when(jnp.logical_not(mine))
                        def _():
                            remote(src, dst, mix_send, mix_recv, (chip // 2, chip % 2, h)).start()

        @pl.when(i == 0)
        def _():
            acc_ref[...] = jnp.zeros_like(acc_ref)

        hb = hb_ref[...]
        dob = (FFN_RES * dy_ref[...]).astype(BF16)
        wg_j, wu_j, wd_j = wg_ref[0], wu_ref[0], wd_ref[0]
        a = a_ref[0]
        b = b_ref[0]
        sg = jax.nn.sigmoid(a)
        sa = a * sg
        fb = (sa * b).astype(BF16)
        df = _dot_nt(dob, wd_j)
        dbb = (df * sa).astype(BF16)
        dab = (df * b * (sg + sa * (1.0 - sg))).astype(BF16)
        dhp_ref[0] = (_dot(dab, wg_j) + _dot(dbb, wu_j)).astype(BF16)
        acc_ref[0:FF_SH, :] += _dot_tn(dab, hb)
        acc_ref[FF_SH:2 * FF_SH, :] += _dot_tn(dbb, hb)
        acc_ref[2 * FF_SH:3 * FF_SH, :] += _dot_tn(fb, dob)

        @pl.when(i == nt - 1)
        def _():
            dst = land_ref.at[dev, land_rows]

            @pl.when(t > 0)
            def _():
                for h in range(2):
                    remote(stage_half(h), dst, send_sems.at[h], recv_sem, (tx, ty, h)).wait_send()

            def cast_rows(r, carry):
                rows = pl.ds(pl.multiple_of(r * MIX_HALF, 16), MIX_HALF)
                stage_ref[rows, :] = acc_ref[rows, :].astype(BF16)
                return carry

            lax.fori_loop(0, 3 * FF_SH // MIX_HALF, cast_rows, 0)

            @pl.when(t < N_CHIPS - 1)
            def _():
                for h in range(2):
                    remote(stage_half(h), dst, send_sems.at[h], recv_sem, (tx, ty, h)).start()

            @pl.when(t == N_CHIPS - 1)
            def _():
                own = pltpu.make_async_copy(stage_half(c), dst, local_sem)
                own.start()
                sib = remote(stage_half(1 - c), dst, send_sems.at[0], recv_sem, (xi, yi, 1 - c))
                sib.start()
                sib.wait_send()
                own.wait()
                arrivals = land_ref.at[pl.ds(0, n_others), land_rows]
                remote(arrivals, arrivals, send_sems.at[0], recv_sem, (xi, yi, 1 - c)).wait_recv()
                if with_mix:
                    seven = land_ref.at[pl.ds(0, n_others), mix_rows]
                    both = remote(seven, seven, mix_send, mix_recv, (xi, yi, 1 - c))
                    both.wait_send()
                    both.wait_recv()
                    pltpu.make_async_copy(mix_ref.at[0, pl.ds(0, MIX_HALF)], land_ref.at[dev, mix_rows],
                                          mix_local).wait()

    def wspec(kk):
        return pl.BlockSpec((1, FF_SH, D_MODEL),
                            lambda t, i, pr: (jnp.bitwise_xor(pr[0], (t + 1) % N_CHIPS), kk, 0))

    xspec = pl.BlockSpec((tile, D_MODEL), lambda t, i, pr: (i, 0))
    abspec = pl.BlockSpec((1, tile, FF_SH), lambda t, i, pr: (jnp.bitwise_xor(pr[0], (t + 1) % N_CHIPS), i, 0))
    hbm = pl.BlockSpec(memory_space=pl.ANY)
    in_specs = [xspec, abspec, abspec, xspec, wspec(0), wspec(1), wspec(2)]
    operands = [place, hb, a, b, dy, pack, pack, pack]
    scratch = [pltpu.VMEM((3 * FF_SH, D_MODEL), F32), pltpu.VMEM((3 * FF_SH, D_MODEL), BF16),
               pltpu.SemaphoreType.DMA((2,)), pltpu.SemaphoreType.DMA, pltpu.SemaphoreType.DMA]
    if with_mix:
        in_specs.append(hbm)
        operands.append(mix_grads)
        scratch += [pltpu.SemaphoreType.DMA, pltpu.SemaphoreType.DMA, pltpu.SemaphoreType.DMA]
    aliases = {}
    if with_land:
        in_specs.append(hbm)
        operands.append(land)
        aliases = {len(operands) - 1: 1}
    grid_spec = pltpu.PrefetchScalarGridSpec(
        num_scalar_prefetch=1, grid=(N_CHIPS, nt), in_specs=in_specs,
        out_specs=[pl.BlockSpec((1, tile, D_MODEL), lambda t, i, pr: (t, i, 0)), hbm],
        scratch_shapes=scratch)
    return pl.pallas_call(
        body, name=name, grid_spec=grid_spec,
        out_shape=[jax.ShapeDtypeStruct((N_CHIPS, s, D_MODEL), BF16),
                   jax.ShapeDtypeStruct((2 * N_CHIPS, HALF_ROWS, D_MODEL), BF16)],
        input_output_aliases=aliases,
        compiler_params=_cparams(("arbitrary", "arbitrary")),
    )(*operands)


def _mix_grads_pack(dw_in_t, dw_out, *, name):
    def body(a_ref, b_ref, o_ref):
        o_ref[0, 0:IN_SH, :] = a_ref[0].astype(BF16)
        o_ref[0, IN_SH:FF_SH, :] = b_ref[0].astype(BF16)

    return pl.pallas_call(
        body, name=name, grid=(N_CHIPS,),
        in_specs=[pl.BlockSpec((1, IN_SH, D_MODEL), lambda j: (j, 0, 0)),
                  pl.BlockSpec((1, OUT_SH, D_MODEL), lambda j: (j, 0, 0))],
        out_specs=pl.BlockSpec((1, FF_SH, D_MODEL), lambda j: (j, 0, 0)),
        out_shape=jax.ShapeDtypeStruct((N_CHIPS, FF_SH, D_MODEL), BF16),
        compiler_params=_cparams(("arbitrary",)),
    )(dw_in_t.reshape(N_CHIPS, IN_SH, D_MODEL), dw_out.reshape(N_CHIPS, OUT_SH, D_MODEL))


def _norm_bwd(dhp, x, dy, g, *, tile, name):
    s = x.shape[0]

    def body(dhp_ref, x_ref, dy_ref, g_ref, dx_ref, dg_ref):
        @pl.when(pl.program_id(0) == 0)
        def _():
            dg_ref[...] = jnp.zeros_like(dg_ref)

        dh = ((dhp_ref[0].astype(F32) + dhp_ref[1].astype(F32))
              + (dhp_ref[2].astype(F32) + dhp_ref[3].astype(F32)))
        x_v = x_ref[...]
        r = lax.rsqrt(jnp.mean(x_v * x_v, axis=-1, keepdims=True) + EPS)
        dx, dg = _rms_bwd(dh, x_v, r, g_ref[...])
        dx_ref[...] = dy_ref[...] + dx
        dg_ref[...] += dg

    return pl.pallas_call(
        body, name=name, grid=(s // tile,),
        in_specs=[pl.BlockSpec((N_CHIPS, tile, D_MODEL), lambda i: (0, i, 0)),
                  _rows(tile, D_MODEL), _rows(tile, D_MODEL), _const((1, D_MODEL))],
        out_specs=[_rows(tile, D_MODEL), _const((1, D_MODEL))],
        out_shape=[jax.ShapeDtypeStruct(x.shape, F32), jax.ShapeDtypeStruct((1, D_MODEL), F32)],
        compiler_params=_cparams(("arbitrary",)),
    )(dhp, x, dy, g)


def _mix_in_fwd(x, g, w_in_t, b_in, *, tile, name):
    s = x.shape[0]

    def body(x_ref, g_ref, w_ref, b_ref, q_ref, k_ref, v_ref, z_ref):
        h, _ = _rms(x_ref[...], g_ref[...])
        proj = _dot_nt(h.astype(BF16), w_ref[...]) + b_ref[...]
        q_ref[...] = proj[:, :ATTN_W].astype(BF16)
        k_ref[...] = proj[:, ATTN_W:ATTN_W + KV_W].astype(BF16)
        v_ref[...] = proj[:, ATTN_W + KV_W:ATTN_W + 2 * KV_W].astype(BF16)
        z_ref[...] = proj[:, ATTN_W + 2 * KV_W:]

    return pl.pallas_call(
        body, name=name, grid=(s // tile,),
        in_specs=[_rows(tile, D_MODEL), _const((1, D_MODEL)), _const((IN_W, D_MODEL)), _const((1, IN_W))],
        out_specs=[_rows(tile, ATTN_W), _rows(tile, KV_W), _rows(tile, KV_W), _rows(tile, 2 * GMLP_W)],
        out_shape=[jax.ShapeDtypeStruct((s, ATTN_W), BF16), jax.ShapeDtypeStruct((s, KV_W), BF16),
                   jax.ShapeDtypeStruct((s, KV_W), BF16), jax.ShapeDtypeStruct((s, 2 * GMLP_W), F32)],
        compiler_params=_cparams(("arbitrary",)),
    )(x, g, w_in_t, b_in)


def _mix_in_bwd(x, dy, dq, dk, dv, dz, g, w_in_t, *, tile, name):
    s = x.shape[0]

    def body(x_ref, dy_ref, dq_ref, dk_ref, dv_ref, dz_ref, g_ref, w_ref, dx_ref, dw_ref, db_ref, dg_ref):
        @pl.when(pl.program_id(0) == 0)
        def _():
            dw_ref[...] = jnp.zeros_like(dw_ref)
            db_ref[...] = jnp.zeros_like(db_ref)
            dg_ref[...] = jnp.zeros_like(dg_ref)

        dproj = jnp.concatenate([dq_ref[...], dk_ref[...], dv_ref[...], dz_ref[...]], axis=-1)
        db_ref[...] += jnp.sum(dproj, axis=0, keepdims=True)
        dpb = dproj.astype(BF16)
        x_v = x_ref[...]
        g_v = g_ref[...]
        h, r = _rms(x_v, g_v)
        dw_ref[...] += _dot_tn(dpb, h.astype(BF16))
        dh = _dot(dpb, w_ref[...])
        dx, dg = _rms_bwd(dh, x_v, r, g_v)
        dx_ref[...] = dy_ref[...] + dx
        dg_ref[...] += dg

    return pl.pallas_call(
        body, name=name, grid=(s // tile,),
        in_specs=[_rows(tile, D_MODEL), _rows(tile, D_MODEL), _rows(tile, ATTN_W), _rows(tile, KV_W),
                  _rows(tile, KV_W), _rows(tile, 2 * GMLP_W), _const((1, D_MODEL)), _const((IN_W, D_MODEL))],
        out_specs=[_rows(tile, D_MODEL), _const((IN_W, D_MODEL)), _const((1, IN_W)), _const((1, D_MODEL))],
        out_shape=[jax.ShapeDtypeStruct(x.shape, F32), jax.ShapeDtypeStruct((IN_W, D_MODEL), F32),
                   jax.ShapeDtypeStruct((1, IN_W), F32), jax.ShapeDtypeStruct((1, D_MODEL), F32)],
        compiler_params=_cparams(("arbitrary",)),
    )(x, dy, dq, dk, dv, dz, g, w_in_t)


_GELU_C = 0.7978845608028654
_GELU_A = 0.044715


def _gelu_tanh(x):
    x2 = x * x
    return jnp.tanh(_GELU_C * (x + _GELU_A * (x2 * x))), x2


def _band(ref, i):
    prev = jnp.maximum(i - 1, 0)
    return jnp.concatenate([ref[pl.ds(pl.multiple_of(prev * BLK, BLK), BLK), :],
                            ref[pl.ds(pl.multiple_of(i * BLK, BLK), BLK), :]], axis=0)


def _fold(band, tri):
    return jnp.where(tri, band[:, BLK:], band[:, :BLK])


def _unfold(a, tri):
    zero = jnp.zeros_like(a)
    return jnp.concatenate([jnp.where(tri, zero, a), jnp.where(tri, a, zero)], axis=-1).astype(BF16)


def _attn_probs(qh, kbg, tri, has_prev, sink):
    s2 = _dot_nt(qh, kbg)
    sc = jnp.where(tri, s2[:, BLK:], jnp.where(has_prev, s2[:, :BLK], -jnp.inf)) * ATTN_SCALE
    m = jnp.maximum(jnp.max(sc, axis=-1, keepdims=True), sink)
    p = jnp.exp(sc - m)
    es = jnp.exp(sink - m)
    inv = 1.0 / (jnp.sum(p, axis=-1, keepdims=True) + es)
    return p * inv, es * inv


def _attn_fwd(q, kb, vb, i, sink_ref):
    tri = _tril_mask()
    outs, saved = [], []
    for h in range(N_Q_HEADS):
        cols = slice((h // REP) * HEAD_DIM, (h // REP + 1) * HEAD_DIM)
        pn, psink = _attn_probs(q[:, h * HEAD_DIM:(h + 1) * HEAD_DIM], kb[:, cols], tri, i > 0, sink_ref[h])
        outs.append(_dot(_unfold(pn, tri), vb[:, cols]))
        saved.append((pn, psink))
    return jnp.concatenate(outs, axis=-1), saved


def _tril_mask():
    t = lax.broadcasted_iota(jnp.int32, (BLK, BLK), 0)
    s_ = lax.broadcasted_iota(jnp.int32, (BLK, BLK), 1)
    return s_ <= t


def _gmlp_fwd_parts(zg, lng, lnb, ws_ref, bs_full):
    th, zg2 = _gelu_tanh(zg)
    z = 0.5 * zg * (1.0 + th)
    u = z[:, :GMLP_W]
    zv = z[:, GMLP_W:]
    mu = jnp.mean(zv, axis=-1, keepdims=True)
    zc = zv - mu
    rstd = lax.rsqrt(jnp.mean(zc * zc, axis=-1, keepdims=True) + EPS)
    xh = zc * rstd
    vvb = (xh * lng + lnb).astype(BF16)
    tril = _tril_mask()
    wms, parts = [], []
    for gi in range(GMLP_GROUPS):
        wm = jnp.where(tril, ws_ref[gi], 0.0).astype(BF16)
        wms.append(wm)
        parts.append(_dot(wm, vvb[:, gi * GROUP_DIM:(gi + 1) * GROUP_DIM]))
    mixed = jnp.concatenate(parts, axis=-1) + bs_full
    gelu_grad = 0.5 * (1.0 + th) + 0.5 * zg * (1.0 - th * th) * (_GELU_C * (1.0 + 3.0 * _GELU_A * zg2))
    return u, xh, rstd, vvb, wms, mixed, gelu_grad


def _mix_core_fwd(q, k, v, zg, sinks, lng, lnb, w_s, bs_full, gao, ggo, *, name):
    s = q.shape[0]
    nb = min(MIX_FWD_BLOCKS, s // BLK)

    def body(sink_ref, q_ref, k_ref, v_ref, z_ref, lng_ref, lnb_ref, ws_ref, bs_ref, gao_ref, ggo_ref, o_ref):
        for b in range(nb):
            blk = pl.program_id(0) * nb + b
            rows = slice(b * BLK, (b + 1) * BLK)
            y_attn, _ = _attn_fwd(q_ref[rows, :], _band(k_ref, blk), _band(v_ref, blk), blk, sink_ref)
            u, _, _, _, _, mixed, _ = _gmlp_fwd_parts(z_ref[rows, :], lng_ref[...], lnb_ref[...], ws_ref,
                                                      bs_ref[...])
            ya, _ = _rms(y_attn, gao_ref[...])
            yg, _ = _rms(u * mixed, ggo_ref[...])
            o_ref[rows, :] = jnp.concatenate([ya, yg], axis=-1).astype(BF16)

    return pl.pallas_call(
        body, name=name, grid=(s // (nb * BLK),),
        in_specs=[pl.BlockSpec(memory_space=pltpu.SMEM),
                  _rows(nb * BLK, ATTN_W), _const((s, KV_W)), _const((s, KV_W)),
                  _rows(nb * BLK, 2 * GMLP_W), _const((1, GMLP_W)), _const((1, GMLP_W)),
                  _const((GMLP_GROUPS, BLK, BLK)), _const((BLK, GMLP_W)), _const((1, ATTN_W)), _const((1, GMLP_W))],
        out_specs=_rows(nb * BLK, D_MODEL),
        out_shape=jax.ShapeDtypeStruct((s, D_MODEL), BF16),
        compiler_params=_cparams(("arbitrary",)),
    )(sinks, q, k, v, zg, lng, lnb, w_s, bs_full, gao, ggo)


def _mix_out_fwd(x1, yb, w_out, b_out, *, tile, name):
    s = x1.shape[0]

    def body(x_ref, y_ref, w_ref, b_ref, o_ref):
        o_ref[...] = x_ref[...] + (_dot(y_ref[...], w_ref[...]) + b_ref[...])

    return pl.pallas_call(
        body, name=name, grid=(s // tile,),
        in_specs=[_rows(tile, D_MODEL), _rows(tile, D_MODEL), _const((D_MODEL, D_MODEL)), _const((1, D_MODEL))],
        out_specs=_rows(tile, D_MODEL),
        out_shape=jax.ShapeDtypeStruct(x1.shape, F32),
        compiler_params=_cparams(("arbitrary",)),
    )(x1, yb, w_out, b_out)


def _norm_bwd_mix_out(dhp, x, dy, g, yb, w_out, *, tile, name):
    s = x.shape[0]

    def body(dhp_ref, x_ref, dy_ref, g_ref, y_ref, w_ref, dx_ref, dg_ref, dyy_ref, dw_ref, db_ref):
        @pl.when(pl.program_id(0) == 0)
        def _():
            dg_ref[...] = jnp.zeros_like(dg_ref)
            dw_ref[...] = jnp.zeros_like(dw_ref)
            db_ref[...] = jnp.zeros_like(db_ref)

        dh = ((dhp_ref[0].astype(F32) + dhp_ref[1].astype(F32))
              + (dhp_ref[2].astype(F32) + dhp_ref[3].astype(F32)))
        x_v = x_ref[...]
        r = lax.rsqrt(jnp.mean(x_v * x_v, axis=-1, keepdims=True) + EPS)
        dxn, dg = _rms_bwd(dh, x_v, r, g_ref[...])
        dx = dy_ref[...] + dxn
        dx_ref[...] = dx
        dg_ref[...] += dg
        dxb = dx.astype(BF16)
        db_ref[...] += jnp.sum(dx, axis=0, keepdims=True)
        dw_ref[...] += _dot_tn(y_ref[...], dxb)
        dyy_ref[...] = _dot_nt(dxb, w_ref[...])

    return pl.pallas_call(
        body, name=name, grid=(s // tile,),
        in_specs=[pl.BlockSpec((N_CHIPS, tile, D_MODEL), lambda i: (0, i, 0)),
                  _rows(tile, D_MODEL), _rows(tile, D_MODEL), _const((1, D_MODEL)), _rows(tile, D_MODEL),
                  _const((D_MODEL, D_MODEL))],
        out_specs=[_rows(tile, D_MODEL), _const((1, D_MODEL)), _rows(tile, D_MODEL), _const((D_MODEL, D_MODEL)),
                   _const((1, D_MODEL))],
        out_shape=[jax.ShapeDtypeStruct(x.shape, F32), jax.ShapeDtypeStruct((1, D_MODEL), F32),
                   jax.ShapeDtypeStruct(x.shape, F32), jax.ShapeDtypeStruct((D_MODEL, D_MODEL), F32),
                   jax.ShapeDtypeStruct((1, D_MODEL), F32)],
        compiler_params=_cparams(("arbitrary",)),
    )(dhp, x, dy, g, yb, w_out)


def _mix_core_bwd(dyy, q, k, v, zg, sinks, lng, lnb, w_s, bs_full, gao, ggo, *, name):
    s = dyy.shape[0]
    nb = min(MIX_BWD_BLOCKS, s // BLK)
    nsteps = s // (nb * BLK)

    def body(*refs):
        accumulators = refs[13:15] + refs[16:]

        @pl.when(pl.program_id(0) == 0)
        def _():
            for ref in accumulators:
                ref[...] = jnp.zeros_like(ref)

        for b in range(nb):
            one_block(pl.program_id(0) * nb + b, slice(b * BLK, (b + 1) * BLK), *refs)

        @pl.when(pl.program_id(0) == nsteps - 1)
        def _():
            tril = _tril_mask()
            for gi in range(GMLP_GROUPS):
                refs[20][gi] = jnp.where(tril, refs[20][gi], 0.0)

    def one_block(i, rows, sink_ref, dyy_ref, q_ref, k_ref, v_ref, z_ref, lng_ref, lnb_ref, ws_ref, bs_ref, gao_ref,
                  ggo_ref, dq_ref, dk_ref, dv_ref, dz_ref, dgao_ref, dggo_ref, dlng_ref, dlnb_ref, dws_ref, dms_ref,
                  dsk_ref):
        q_v = q_ref[rows, :]
        kb = _band(k_ref, i)
        vb = _band(v_ref, i)
        lng_v = lng_ref[...]
        gao_v = gao_ref[...]
        ggo_v = ggo_ref[...]

        y_attn, probs = _attn_fwd(q_v, kb, vb, i, sink_ref)
        u, xh, rstd, vvb, wms, mixed, gelu_grad = _gmlp_fwd_parts(z_ref[rows, :], lng_v, lnb_ref[...], ws_ref,
                                                                  bs_ref[...])
        y_gmlp = u * mixed
        ra = lax.rsqrt(jnp.mean(y_attn * y_attn, axis=-1, keepdims=True) + EPS)
        rg = lax.rsqrt(jnp.mean(y_gmlp * y_gmlp, axis=-1, keepdims=True) + EPS)

        dyy = dyy_ref[rows, :]
        d_attn, dgao = _rms_bwd(dyy[:, :ATTN_W], y_attn, ra, gao_v)
        d_gmlp, dggo = _rms_bwd(dyy[:, ATTN_W:], y_gmlp, rg, ggo_v)
        dgao_ref[...] += dgao
        dggo_ref[...] += dggo

        du = d_gmlp * mixed
        dmixed = d_gmlp * u
        dms_ref[...] += dmixed
        dmb = dmixed.astype(BF16)
        dvv_parts = []
        for gi in range(GMLP_GROUPS):
            sl = slice(gi * GROUP_DIM, (gi + 1) * GROUP_DIM)
            dws_ref[gi] += _dot_nt(dmb[:, sl], vvb[:, sl])
            dvv_parts.append(_dot_tn(wms[gi], dmb[:, sl]))
        dvv = jnp.concatenate(dvv_parts, axis=-1)
        dlng_ref[...] += jnp.sum(dvv * xh, axis=0, keepdims=True)
        dlnb_ref[...] += jnp.sum(dvv, axis=0, keepdims=True)
        dxh = dvv * lng_v
        dzv = rstd * (dxh - jnp.mean(dxh, axis=-1, keepdims=True)
                      - xh * jnp.mean(dxh * xh, axis=-1, keepdims=True))
        dz_ref[rows, :] = jnp.concatenate([du, dzv], axis=-1) * gelu_grad

        dab = d_attn.astype(BF16)
        tri = _tril_mask()
        dq_parts = []
        dk_parts = []
        dv_parts = []
        for gi in range(N_KV_HEADS):
            cols = slice(gi * HEAD_DIM, (gi + 1) * HEAD_DIM)
            kg, vg = kb[:, cols], vb[:, cols]
            dkg = jnp.zeros((2 * BLK, HEAD_DIM), F32)
            dvg = jnp.zeros((2 * BLK, HEAD_DIM), F32)
            for rr in range(REP):
                h = gi * REP + rr
                hs = slice(h * HEAD_DIM, (h + 1) * HEAD_DIM)
                qh, doh = q_v[:, hs], dab[:, hs]
                pn, psink = probs[h]
                dp = _fold(_dot_nt(doh, vg), tri)
                delta = jnp.sum(pn * dp, axis=-1, keepdims=True)
                ds2 = _unfold(pn * (dp - delta) * ATTN_SCALE, tri)
                dsink = jnp.sum(-psink * delta, axis=0, keepdims=True)
                dsk_ref[pl.ds(h, 1), :] += jnp.broadcast_to(dsink, (1, 128))
                dq_parts.append(_dot(ds2, kg))
                dkg = dkg + _dot_tn(ds2, qh)
                dvg = dvg + _dot_tn(_unfold(pn, tri), doh)
            dk_parts.append(dkg)
            dv_parts.append(dvg)
        dq_ref[rows, :] = jnp.concatenate(dq_parts, axis=-1)
        dkb = jnp.concatenate(dk_parts, axis=-1)
        dvb = jnp.concatenate(dv_parts, axis=-1)
        prev = pl.ds(pl.multiple_of(jnp.maximum(i - 1, 0) * BLK, BLK), BLK)
        cur = pl.ds(pl.multiple_of(i * BLK, BLK), BLK)
        dk_ref[prev, :] += dkb[:BLK]
        dv_ref[prev, :] += dvb[:BLK]
        dk_ref[cur, :] += dkb[BLK:]
        dv_ref[cur, :] += dvb[BLK:]

    return pl.pallas_call(
        body, name=name, grid=(nsteps,),
        in_specs=[pl.BlockSpec(memory_space=pltpu.SMEM),
                  _rows(nb * BLK, D_MODEL), _rows(nb * BLK, ATTN_W), _const((s, KV_W)), _const((s, KV_W)),
                  _rows(nb * BLK, 2 * GMLP_W), _const((1, GMLP_W)), _const((1, GMLP_W)),
                  _const((GMLP_GROUPS, BLK, BLK)), _const((BLK, GMLP_W)), _const((1, ATTN_W)), _const((1, GMLP_W))],
        out_specs=[_rows(nb * BLK, ATTN_W), _const((s, KV_W)), _const((s, KV_W)), _rows(nb * BLK, 2 * GMLP_W),
                   _const((1, ATTN_W)), _const((1, GMLP_W)),
                   _const((1, GMLP_W)), _const((1, GMLP_W)), _const((GMLP_GROUPS, BLK, BLK)),
                   _const((BLK, GMLP_W)), _const((N_Q_HEADS, 128))],
        out_shape=[jax.ShapeDtypeStruct((s, ATTN_W), F32), jax.ShapeDtypeStruct((s, KV_W), F32),
                   jax.ShapeDtypeStruct((s, KV_W), F32), jax.ShapeDtypeStruct((s, 2 * GMLP_W), F32),
                   jax.ShapeDtypeStruct((1, ATTN_W), F32), jax.ShapeDtypeStruct((1, GMLP_W), F32),
                   jax.ShapeDtypeStruct((1, GMLP_W), F32), jax.ShapeDtypeStruct((1, GMLP_W), F32),
                   jax.ShapeDtypeStruct((GMLP_GROUPS, BLK, BLK), F32), jax.ShapeDtypeStruct((BLK, GMLP_W), F32),
                   jax.ShapeDtypeStruct((N_Q_HEADS, 128), F32)],
        compiler_params=_cparams(("arbitrary",)),
    )(sinks, dyy, q, k, v, zg, lng, lnb, w_s, bs_full, gao, ggo)


def _local_step(place, x, tgt, p, pack_a, pack_b, *, tile=512, fwd_tile=256, bwd_tile=512, norm_tile=512):
    g = {}
    tile, fwd_tile, bwd_tile, norm_tile = (min(t_, x.shape[0]) for t_ in (tile, fwd_tile, bwd_tile, norm_tile))
    x1, hb1, a1, b1, pack_b = _ffn_fwd(x, p["ffn1_norm_g"], pack_a, 0, pack_b, tile=fwd_tile, name="ffn1_fwd")
    mix_rows = pack_b[:, 3 * FF_SH:, :]
    w_in_t = mix_rows[:, :IN_SH, :].reshape(IN_W, D_MODEL)
    w_out = mix_rows[:, IN_SH:, :].reshape(D_MODEL, D_MODEL)
    q, k, v, zg = _mix_in_fwd(x1, p["mix_norm_g"], w_in_t, p["b_in"], tile=tile, name="mix_in_fwd")
    mix_args = (q, k, v, zg, p["attn_sinks"], p["gmlp_ln_g"], p["gmlp_ln_b"], p["gmlp_w_s"], p["bs_full"],
                p["attn_out_norm_g"], p["gmlp_out_norm_g"])
    yb = _mix_core_fwd(*mix_args, name="mix_core_fwd")
    x2 = _mix_out_fwd(x1, yb, w_out, p["b_out"], tile=tile, name="mix_out_fwd")
    dx3, loss, g["final_norm_g"], hb2, a2, b2 = _ffn_fwd_loss(
        x2, p["ffn2_norm_g"], pack_b, 0, p["final_norm_g"], tgt, tile=fwd_tile, name="ffn2_fwd_loss")

    dhp, land = _ffn_bwd(place, hb2, a2, b2, dx3, pack_b, 1, None, None, tile=bwd_tile, name="ffn2_bwd")
    dx2, g["ffn2_norm_g"], dyy, dw_out, g["b_out"] = _norm_bwd_mix_out(
        dhp, x2, dx3, p["ffn2_norm_g"], yb, w_out, tile=norm_tile, name="ffn2_norm_bwd")

    (dq, dk, dv, dz, g["attn_out_norm_g"], g["gmlp_out_norm_g"], g["gmlp_ln_g"],
     g["gmlp_ln_b"], g["gmlp_w_s"], dmix_sum, dsinks) = _mix_core_bwd(dyy, *mix_args, name="mix_core_bwd")
    g["gmlp_b_s"] = dmix_sum
    g["attn_sinks"] = dsinks
    dx1, dw_in_t, g["b_in"], g["mix_norm_g"] = _mix_in_bwd(
        x1, dx2, dq, dk, dv, dz, p["mix_norm_g"], w_in_t, tile=tile, name="mix_in_bwd")
    mix_grads = _mix_grads_pack(dw_in_t, dw_out, name="mix_grads_pack")

    dhp1, land = _ffn_bwd(place, hb1, a1, b1, dx1, pack_a, 0, land, mix_grads, tile=bwd_tile, name="ffn1_bwd")
    dx0, g["ffn1_norm_g"] = _norm_bwd(dhp1, x, dx1, p["ffn1_norm_g"], tile=norm_tile, name="ffn1_norm_bwd")
    return loss, dx0, land, g


def _pack_cast(place, parts, *, name):
    def body(place_ref, *refs):
        oa_ref, ob_ref = refs[-2], refs[-1]
        off = 0
        for k, (ref, rows) in enumerate(zip(refs[:-2], BIG_ROWS)):
            if k == 3:
                off = 0
            (oa_ref if k < 3 else ob_ref)[0, off:off + rows, :] = ref[...].astype(BF16)
            off += rows

    one = pl.Buffered(1)

    def slab(rows):
        return pl.BlockSpec((1, rows, D_MODEL), lambda i, pr: (pr[0], 0, 0), pipeline_mode=one)

    grid_spec = pltpu.PrefetchScalarGridSpec(
        num_scalar_prefetch=1, grid=(1,),
        in_specs=[pl.BlockSpec((rows, D_MODEL), lambda i, pr: (0, 0), pipeline_mode=one) for rows in BIG_ROWS],
        out_specs=[slab(PACK_A_ROWS), slab(PACK_B_ROWS)])
    return pl.pallas_call(
        body, name=name, grid_spec=grid_spec,
        out_shape=[jax.ShapeDtypeStruct((N_CHIPS, PACK_A_ROWS, D_MODEL), BF16),
                   jax.ShapeDtypeStruct((N_CHIPS, PACK_B_ROWS, D_MODEL), BF16)],
        compiler_params=_cparams(("arbitrary",)),
    )(place, *parts)


def _all_gather_pack(pack, *, name):
    def body(p_ref, o_ref, send_sems, recv_sems):
        start, forward, finish = _gather_stages(o_ref, send_sems, recv_sems)
        start()
        forward()
        finish()

    return pl.pallas_call(
        body, name=name,
        in_specs=[pl.BlockSpec(memory_space=pl.ANY)],
        out_specs=pl.BlockSpec(memory_space=pl.ANY),
        out_shape=jax.ShapeDtypeStruct(pack.shape, pack.dtype),
        input_output_aliases={0: 0},
        scratch_shapes=[pltpu.SemaphoreType.DMA((6,)), pltpu.SemaphoreType.DMA((6,))],
    )(pack)


def _shard_tile(i, c):
    return jnp.where(i < 3, 3 * c + i, jnp.where(i < 6, 3 + 3 * c + i, 12 + c))


def _rs_reduce(place, land, *, name):
    def body(place_ref, l_ref, o_ref):
        acc = l_ref[0].astype(F32)
        for d in range(1, 2 * N_CHIPS):
            acc = acc + l_ref[d].astype(F32)
        o_ref[...] = acc

    grid_spec = pltpu.PrefetchScalarGridSpec(
        num_scalar_prefetch=1, grid=(HALF_ROWS // MIX_HALF,),
        in_specs=[pl.BlockSpec((2 * N_CHIPS, MIX_HALF, D_MODEL), lambda i, pr: (0, i, 0))],
        out_specs=pl.BlockSpec((MIX_HALF, D_MODEL), lambda i, pr: (_shard_tile(i, pr[1]), 0)))
    return pl.pallas_call(
        body, name=name, grid_spec=grid_spec,
        out_shape=jax.ShapeDtypeStruct((PACK_ROWS, D_MODEL), F32),
        compiler_params=_cparams(("arbitrary",)),
    )(place, land)


def _rs_share(shard, *, name):
    def body(s_ref, o_ref, send_sems, recv_sems):
        x, y, c, _ = _mesh_place()

        def rows(k, core):
            if k < 2:
                return o_ref.at[pl.ds(pl.multiple_of(k * 2 * FFN_HALF + core * FFN_HALF, 8), FFN_HALF)]
            return o_ref.at[pl.ds(pl.multiple_of(4 * FFN_HALF + core * MIX_HALF, 8), MIX_HALF)]

        def copy(k, core):
            return pltpu.make_async_remote_copy(src_ref=rows(k, core), dst_ref=rows(k, core), send_sem=send_sems.at[k],
                                                recv_sem=recv_sems.at[k], device_id=(x, y, 1 - c),
                                                device_id_type=MESH)

        sends = [copy(k, c) for k in range(3)]
        for cp in sends:
            cp.start()
        for k in range(3):
            copy(k, 1 - c).wait_recv()
        for cp in sends:
            cp.wait_send()

    return pl.pallas_call(
        body, name=name,
        in_specs=[pl.BlockSpec(memory_space=pl.ANY)],
        out_specs=pl.BlockSpec(memory_space=pl.ANY),
        out_shape=jax.ShapeDtypeStruct((PACK_ROWS, D_MODEL), F32),
        input_output_aliases={0: 0},
        scratch_shapes=[pltpu.SemaphoreType.DMA((3,)), pltpu.SemaphoreType.DMA((3,))],
    )(shard)


def _small_all_reduce(packed, *, name):
    rows = packed.shape[0]
    half = rows // 2

    def body(p_ref, o_ref, sib_ref, slots_ref, send_sems, recv_sems):
        x, y, c, others = _mesh_place()
        me = 2 * x + y
        sibling = (x, y, 1 - c)

        def half_of(core):
            return pl.ds(pl.multiple_of(core * half, 8), half)

        def remote(k, src, dst, to):
            return pltpu.make_async_remote_copy(src_ref=src, dst_ref=dst, send_sem=send_sems.at[k],
                                                recv_sem=recv_sems.at[k], device_id=to, device_id_type=MESH)

        sib = remote(0, p_ref.at[half_of(1 - c)], sib_ref, sibling)
        sib.start()
        sib.wait()
        slots_ref[me] = p_ref[half_of(c), :] + sib_ref[...]
        sends = [remote(1 + j, slots_ref.at[me], slots_ref.at[me], (px, py, c)) for j, (px, py) in enumerate(others)]
        for cp in sends:
            cp.start()
        for j, (px, py) in enumerate(others):
            slab = slots_ref.at[2 * px + py]
            remote(1 + j, slab, slab, (px, py, c)).wait_recv()
        for cp in sends:
            cp.wait_send()
        o_ref[half_of(c), :] = (slots_ref[0] + slots_ref[1]) + (slots_ref[2] + slots_ref[3])
        back = remote(4, o_ref.at[half_of(c)], o_ref.at[half_of(c)], sibling)
        back.start()
        remote(4, o_ref.at[half_of(1 - c)], o_ref.at[half_of(1 - c)], sibling).wait_recv()
        back.wait_send()

    vm = pl.BlockSpec(memory_space=pltpu.VMEM)
    return pl.pallas_call(
        body, name=name, in_specs=[vm], out_specs=vm,
        out_shape=jax.ShapeDtypeStruct((rows, 128), F32),
        scratch_shapes=[pltpu.VMEM((half, 128), F32), pltpu.VMEM((N_CHIPS, half, 128), F32),
                        pltpu.SemaphoreType.DMA((5,)), pltpu.SemaphoreType.DMA((5,))],
    )(packed)


def _adamw(w, g, m, v, *, g_row0, tile, name):
    rows, cols = w.shape
    assert g_row0 % tile == 0 and rows % tile == 0

    def body(w_ref, g_ref, m_ref, v_ref, go_ref, d_ref, nm_ref, nv_ref):
        g_v = g_ref[...]
        m_n = ADAM_B1 * m_ref[...] + (1.0 - ADAM_B1) * g_v
        v_n = ADAM_B2 * v_ref[...] + (1.0 - ADAM_B2) * (g_v * g_v)
        m_hat = m_n / (1.0 - ADAM_B1 ** ADAM_STEP)
        v_hat = v_n / (1.0 - ADAM_B2 ** ADAM_STEP)
        d_ref[...] = -ADAM_LR * (m_hat / (jnp.sqrt(v_hat) + ADAM_EPS) + ADAM_WD * w_ref[...])
        go_ref[...] = g_v
        nm_ref[...] = m_n
        nv_ref[...] = v_n

    spec = pl.BlockSpec((tile, cols), lambda i: (i, 0))
    gspec = pl.BlockSpec((tile, cols), lambda i: (g_row0 // tile + i, 0))
    shape = jax.ShapeDtypeStruct((rows, cols), F32)
    return pl.pallas_call(
        body, name=name, grid=(rows // tile,),
        in_specs=[spec, gspec, spec, spec], out_specs=[spec] * 4, out_shape=[shape] * 4,
        compiler_params=_cparams(("arbitrary",)),
    )(w, g, m, v)


def kernel(x, ffn1_norm_g, ffn1_w_gate, ffn1_w_up, ffn1_w_down, mix_norm_g, w_in, b_in, attn_sinks, gmlp_ln_g, gmlp_ln_b, gmlp_w_s, gmlp_b_s, attn_out_norm_g, gmlp_out_norm_g, w_out, b_out, ffn2_norm_g, ffn2_w_gate, ffn2_w_up, ffn2_w_down, final_norm_g, loss_target, m_ffn1_norm_g, m_ffn1_w_gate, m_ffn1_w_up, m_ffn1_w_down, m_mix_norm_g, m_w_in, m_b_in, m_attn_sinks, m_gmlp_ln_g, m_gmlp_ln_b, m_gmlp_w_s, m_gmlp_b_s, m_attn_out_norm_g, m_gmlp_out_norm_g, m_w_out, m_b_out, m_ffn2_norm_g, m_ffn2_w_gate, m_ffn2_w_up, m_ffn2_w_down, m_final_norm_g, v_ffn1_norm_g, v_ffn1_w_gate, v_ffn1_w_up, v_ffn1_w_down, v_mix_norm_g, v_w_in, v_b_in, v_attn_sinks, v_gmlp_ln_g, v_gmlp_ln_b, v_gmlp_w_s, v_gmlp_b_s, v_attn_out_norm_g, v_gmlp_out_norm_g, v_w_out, v_b_out, v_ffn2_norm_g, v_ffn2_w_gate, v_ffn2_w_up, v_ffn2_w_down, v_final_norm_g):
    f_args = dict(locals())
    weights = {n: f_args[n] for n in [nm for nm, _ in SMALL if nm != "loss"] + list(BIG)}
    shapes = {n: weights[n].shape for n in weights}
    shapes["loss"] = ()
    place = jnp.stack([2 * lax.axis_index("x") + lax.axis_index("y"), lax.axis_index("c")]).astype(jnp.int32)

    def with_cols(name, a):
        a2 = a.reshape(a.shape[-2], a.shape[-1])
        return a2.T if BIG_TRANSPOSED[BIG.index(name)] else a2

    def natural(name, a2):
        return (a2.T if BIG_TRANSPOSED[BIG.index(name)] else a2).reshape(shapes[name])

    pack_a, pack_b = _pack_cast(place, [with_cols(n, weights[n]) for n in BIG], name="pack_cast")
    pack_a = _all_gather_pack(pack_a, name="ag_weights")
    p = {n: weights[n].reshape(1, -1) for n in ("ffn1_norm_g", "mix_norm_g", "b_in", "gmlp_ln_g", "gmlp_ln_b",
                                                "attn_out_norm_g", "gmlp_out_norm_g", "b_out", "ffn2_norm_g",
                                                "final_norm_g")}
    p["attn_sinks"] = attn_sinks.reshape(N_Q_HEADS)
    p["gmlp_w_s"] = gmlp_w_s.reshape(GMLP_GROUPS, BLK, BLK)
    p["bs_full"] = jnp.broadcast_to(gmlp_b_s.reshape(GMLP_GROUPS, BLK).T[:, :, None],
                                    (BLK, GMLP_GROUPS, GROUP_DIM)).reshape(BLK, GMLP_W)

    loss_part, dx0, land, gs = _local_step(place, x[0], loss_target[0], p, pack_a, pack_b)

    shard = _rs_share(_rs_reduce(place, land, name="rs_reduce"), name="rs_share")
    gs["gmlp_b_s"] = jnp.sum(gs["gmlp_b_s"].reshape(BLK, GMLP_GROUPS, GROUP_DIM), axis=-1).T
    gs["attn_sinks"] = gs["attn_sinks"][:, 0]
    gs["loss"] = loss_part[0, 0]
    small_sum = _small_all_reduce(_pack_small(gs), name="small_all_reduce")

    grad_w, delta, new_m, new_v = {}, {}, {}, {}
    off = 0
    for n, rows in zip(BIG, BIG_ROWS):
        res = _adamw(with_cols(n, weights[n]), shard, with_cols(n, f_args["m_" + n]), with_cols(n, f_args["v_" + n]),
                     g_row0=off, tile=FF_SH // 2 if rows == FF_SH else 64, name="adamw_" + n)
        grad_w[n], delta[n], new_m[n], new_v[n] = [natural(n, a) for a in res]
        off += rows
    sm = {k: {n: f_args[k + n] for n, _ in SMALL if n != "loss"} for k in ("", "m_", "v_")}
    for k in sm:
        sm[k]["loss"] = jnp.zeros((), F32)
    res = _adamw(_pack_small(sm[""]), small_sum, _pack_small(sm["m_"]), _pack_small(sm["v_"]),
                 g_row0=0, tile=SMALL_ROWS, name="adamw_small")
    small = _unpack_small(res[0], shapes)
    for dst, packed in ((grad_w, res[0]), (delta, res[1]), (new_m, res[2]), (new_v, res[3])):
        dst.update({n: a for n, a in _unpack_small(packed, shapes).items() if n != "loss"})

    order = ('ffn1_norm_g', 'ffn1_w_gate', 'ffn1_w_up', 'ffn1_w_down', 'mix_norm_g', 'w_in', 'b_in', 'attn_sinks',
             'gmlp_ln_g', 'gmlp_ln_b', 'gmlp_w_s', 'gmlp_b_s', 'attn_out_norm_g', 'gmlp_out_norm_g', 'w_out', 'b_out',
             'ffn2_norm_g', 'ffn2_w_gate', 'ffn2_w_up', 'ffn2_w_down', 'final_norm_g')
    return (small["loss"], dx0.reshape(x.shape), *[grad_w[n] for n in order], *[delta[n] for n in order],
            *[new_m[n] for n in order], *[new_v[n] for n in order])
```

```python
import functools

import jax
import jax.numpy as jnp
from jax import lax
from jax.experimental import pallas as pl
from jax.experimental.pallas import tpu as pltpu

F32 = jnp.float32
BF16 = jnp.bfloat16

D_MODEL = 1024
D_FF = 2816
N_CHIPS = 4
FF_SH = D_FF // N_CHIPS
N_Q_HEADS = 8
N_KV_HEADS = 2
REP = N_Q_HEADS // N_KV_HEADS
HEAD_DIM = 64
ATTN_W = 512
KV_W = 128
GMLP_W = 512
GMLP_GROUPS = 8
GROUP_DIM = 64
BLK = 128
MIX_FWD_BLOCKS = 2
MIX_BWD_BLOCKS = 4
IN_W = 1792
IN_SH = IN_W // N_CHIPS
OUT_SH = D_MODEL // N_CHIPS
EPS = 1e-6
FFN_RES = 0.5
ATTN_SCALE = HEAD_DIM ** -0.5

ADAM_LR = 0.001
ADAM_B1 = 0.9
ADAM_B2 = 0.999
ADAM_EPS = 1e-08
ADAM_WD = 0.01
ADAM_STEP = 10

V7X_VMEM_LIMIT = 56 * 1024 * 1024
MESH = pl.DeviceIdType.MESH


def _cparams(sem):
    return pltpu.CompilerParams(dimension_semantics=sem, vmem_limit_bytes=V7X_VMEM_LIMIT)


def _dot(a, b):
    return jnp.dot(a, b, preferred_element_type=F32)


def _dot_nt(a, b):
    return lax.dot_general(a, b, (((1,), (1,)), ((), ())), preferred_element_type=F32)


def _dot_tn(a, b):
    return lax.dot_general(a, b, (((0,), (0,)), ((), ())), preferred_element_type=F32)


def _rms(x, g):
    r = lax.rsqrt(jnp.mean(x * x, axis=-1, keepdims=True) + EPS)
    return x * r * g, r


def _rms_bwd(dh, x, r, g):
    gy = dh * g
    dx = r * gy - x * (r * r * r) * jnp.mean(gy * x, axis=-1, keepdims=True)
    dg = jnp.sum(dh * x * r, axis=0, keepdims=True)
    return dx, dg


def _const(shape):
    nd = len(shape)
    return pl.BlockSpec(shape, lambda *_: (0,) * nd)


def _rows(t, w):
    return pl.BlockSpec((t, w), lambda i: (i, 0))


PACK_ROWS = 7 * FF_SH
HALF_ROWS = PACK_ROWS // 2
FFN_HALF = 3 * FF_SH // 2
MIX_HALF = FF_SH // 2
PACK_A_ROWS = 3 * FF_SH
PACK_B_ROWS = 4 * FF_SH
BIG = ("ffn1_w_gate", "ffn1_w_up", "ffn1_w_down", "ffn2_w_gate", "ffn2_w_up", "ffn2_w_down", "w_in", "w_out")
BIG_ROWS = (FF_SH, FF_SH, FF_SH, FF_SH, FF_SH, FF_SH, IN_SH, OUT_SH)
BIG_TRANSPOSED = (True, True, False, True, True, False, True, False)

SMALL = (("ffn1_norm_g", 1024), ("mix_norm_g", 1024), ("b_in", 1792), ("attn_sinks", 8), ("gmlp_ln_g", 512),
         ("gmlp_ln_b", 512), ("gmlp_w_s", 131072), ("gmlp_b_s", 1024), ("attn_out_norm_g", 512),
         ("gmlp_out_norm_g", 512), ("b_out", 1024), ("ffn2_norm_g", 1024), ("final_norm_g", 1024), ("loss", 1))


def _small_rows(n):
    return -(-n // 1024) * 8


SMALL_USED_ROWS = sum(_small_rows(n) for _, n in SMALL)
SMALL_ROWS = -(-SMALL_USED_ROWS // 16) * 16


def _pack_small(parts):
    out = []
    for name, n in SMALL:
        flat = parts[name].reshape(-1).astype(F32)
        rows = _small_rows(n)
        out.append(jnp.pad(flat, (0, rows * 128 - n)).reshape(rows, 128))
    if SMALL_ROWS > SMALL_USED_ROWS:
        out.append(jnp.zeros((SMALL_ROWS - SMALL_USED_ROWS, 128), F32))
    return jnp.concatenate(out, axis=0)


def _unpack_small(packed, shapes):
    res, off = {}, 0
    for name, n in SMALL:
        rows = _small_rows(n)
        res[name] = packed[off:off + rows].reshape(-1)[:n].reshape(shapes[name])
        off += rows
    return res


def _ffn_tile(x, g, wg_ref, wu_ref, wd_ref, hb_ref, a_ref, b_ref):
    h, _ = _rms(x, g)
    hb = h.astype(BF16)
    hb_ref[...] = hb
    acc = jnp.zeros(x.shape, F32)
    for j in range(N_CHIPS):
        a = _dot_nt(hb, wg_ref[j])
        b = _dot_nt(hb, wu_ref[j])
        a_ref[j] = a
        b_ref[j] = b
        f = (a * jax.nn.sigmoid(a) * b).astype(BF16)
        acc = acc + _dot(f, wd_ref[j])
    return x + FFN_RES * acc


def _ffn_saved_specs(s, tile):
    ab = pl.BlockSpec((N_CHIPS, tile, FF_SH), lambda i: (0, i, 0))
    shape = jax.ShapeDtypeStruct((N_CHIPS, s, FF_SH), F32)
    return [_rows(tile, D_MODEL), ab, ab], [jax.ShapeDtypeStruct((s, D_MODEL), BF16), shape, shape]


def _ffn_weight_specs(k0):
    one = pl.Buffered(1)
    return [pl.BlockSpec((N_CHIPS, FF_SH, D_MODEL), functools.partial(lambda kk, i: (0, kk, 0), k0 + d),
                         pipeline_mode=one) for d in range(3)]


def _mesh_place():
    x, y, c = lax.axis_index("x"), lax.axis_index("y"), lax.axis_index("c")
    others = [(1 - x, y), (x, 1 - y), (1 - x, 1 - y)]
    return x, y, c, others


def _gather_stages(o_ref, send_sems, recv_sems):
    x, y, c, others = _mesh_place()
    me = 2 * x + y
    sibling = (x, y, 1 - c)
    half_rows = o_ref.shape[1] // 2

    def half(slab, core):
        return o_ref.at[slab, pl.ds(pl.multiple_of(core * half_rows, 16), half_rows)]

    def copy(k, rows, to):
        return pltpu.make_async_remote_copy(src_ref=rows, dst_ref=rows, send_sem=send_sems.at[k],
                                            recv_sem=recv_sems.at[k], device_id=to, device_id_type=MESH)

    first = [copy(j, half(me, c), (px, py, c)) for j, (px, py) in enumerate(others)]
    passed = [copy(3 + j, half(2 * px + py, c), sibling) for j, (px, py) in enumerate(others)]

    def start():
        for cp in first:
            cp.start()

    def forward():
        for j, (px, py) in enumerate(others):
            copy(j, half(2 * px + py, c), (px, py, c)).wait_recv()
            passed[j].start()

    def finish():
        for j, (px, py) in enumerate(others):
            copy(3 + j, half(2 * px + py, 1 - c), sibling).wait_recv()
        for cp in first + passed:
            cp.wait_send()

    return start, forward, finish


def _ffn_fwd(x, g, pack, k0, gather, *, tile, name):
    s = x.shape[0]
    nt = s // tile
    forward_at = max(nt - 6, 0)

    def body(x_ref, g_ref, wg_ref, wu_ref, wd_ref, gin_ref, o_ref, hb_ref, a_ref, b_ref, gat_ref, send_sems, recv_sems):
        i = pl.program_id(0)
        start, forward, finish = _gather_stages(gat_ref, send_sems, recv_sems)
        pl.when(i == 0)(start)
        o_ref[...] = _ffn_tile(x_ref[...], g_ref[...], wg_ref, wu_ref, wd_ref, hb_ref, a_ref, b_ref)
        pl.when(i == forward_at)(forward)
        pl.when(i == nt - 1)(finish)

    saved_specs, saved_shapes = _ffn_saved_specs(s, tile)
    hbm = pl.BlockSpec(memory_space=pl.ANY)
    return pl.pallas_call(
        body, name=name, grid=(nt,),
        in_specs=[_rows(tile, D_MODEL), _const((1, D_MODEL))] + _ffn_weight_specs(k0) + [hbm],
        out_specs=[_rows(tile, D_MODEL)] + saved_specs + [hbm],
        out_shape=[jax.ShapeDtypeStruct(x.shape, F32)] + saved_shapes
                  + [jax.ShapeDtypeStruct(gather.shape, gather.dtype)],
        input_output_aliases={5: 4},
        scratch_shapes=[pltpu.SemaphoreType.DMA((6,)), pltpu.SemaphoreType.DMA((6,))],
        compiler_params=_cparams(("arbitrary",)),
    )(x, g, pack, pack, pack, gather)


def _ffn_fwd_loss(x, g, pack, k0, gf, tgt, *, tile, name):
    s = x.shape[0]

    def body(x_ref, g_ref, wg_ref, wu_ref, wd_ref, gf_ref, t_ref, dx_ref, loss_ref, dgf_ref, hb_ref, a_ref, b_ref):
        @pl.when(pl.program_id(0) == 0)
        def _():
            loss_ref[...] = jnp.zeros_like(loss_ref)
            dgf_ref[...] = jnp.zeros_like(dgf_ref)

        x3 = _ffn_tile(x_ref[...], g_ref[...], wg_ref, wu_ref, wd_ref, hb_ref, a_ref, b_ref)
        gf_v = gf_ref[...]
        out, r = _rms(x3, gf_v)
        diff = out - t_ref[...]
        part = jnp.sum(jnp.sum(diff * diff, axis=-1, keepdims=True), axis=0, keepdims=True)
        loss_ref[...] += jnp.broadcast_to(part * (0.5 / D_MODEL), loss_ref.shape)
        dx, dg = _rms_bwd(diff * (1.0 / D_MODEL), x3, r, gf_v)
        dx_ref[...] = dx
        dgf_ref[...] += dg

    saved_specs, saved_shapes = _ffn_saved_specs(s, tile)
    return pl.pallas_call(
        body, name=name, grid=(s // tile,),
        in_specs=[_rows(tile, D_MODEL), _const((1, D_MODEL))] + _ffn_weight_specs(k0)
                 + [_const((1, D_MODEL)), _rows(tile, D_MODEL)],
        out_specs=[_rows(tile, D_MODEL), _const((1, 128)), _const((1, D_MODEL))] + saved_specs,
        out_shape=[jax.ShapeDtypeStruct(x.shape, F32),
                   jax.ShapeDtypeStruct((1, 128), F32),
                   jax.ShapeDtypeStruct((1, D_MODEL), F32)] + saved_shapes,
        compiler_params=_cparams(("arbitrary",)),
    )(x, g, pack, pack, pack, gf, tgt)


def _ffn_bwd(place, hb, a, b, dy, pack, region, land, mix_grads, *, tile, name):
    s = hb.shape[0]
    nt = s // tile
    land_rows = pl.ds(region * FFN_HALF, FFN_HALF)
    mix_rows = pl.ds(2 * FFN_HALF, MIX_HALF)
    with_mix = mix_grads is not None
    with_land = land is not None
    n_others = 2 * N_CHIPS - 1

    def body(place_ref, hb_ref, a_ref, b_ref, dy_ref, wg_ref, wu_ref, wd_ref, *rest):
        rest = list(rest)
        mix_ref = rest.pop(0) if with_mix else None
        if with_land:
            rest.pop(0)
        dhp_ref, land_ref, acc_ref, stage_ref, send_sems, recv_sem, local_sem = rest[:7]
        t, i = pl.program_id(0), pl.program_id(1)
        xi, yi, c = lax.axis_index("x"), lax.axis_index("y"), lax.axis_index("c")
        dev = 4 * xi + 2 * yi + c
        tt = (t + 1) % N_CHIPS
        tx, ty = jnp.bitwise_xor(xi, tt // 2), jnp.bitwise_xor(yi, tt % 2)

        def remote(src, dst, ssem, rsem, to):
            return pltpu.make_async_remote_copy(src_ref=src, dst_ref=dst, send_sem=ssem, recv_sem=rsem,
                                                device_id=to, device_id_type=MESH)

        def stage_half(h):
            return stage_ref.at[pl.ds(pl.multiple_of(h * FFN_HALF, 16), FFN_HALF)]

        if with_mix:
            mix_send, mix_recv, mix_local = rest[7:10]

            @pl.when(jnp.logical_and(t == 0, i == 0))
            def _():
                for chip in range(N_CHIPS):
                    for h in range(2):
                        src = mix_ref.at[chip, pl.ds(h * MIX_HALF, MIX_HALF)]
                        dst = land_ref.at[dev, mix_rows]
                        mine = jnp.logical_and(2 * xi + yi == chip, c == h)

                        @pl.when(mine)
                        def _():
                            pltpu.make_async_copy(src, dst, mix_local).start()

                        @pl.when(jnp.logical_not(mine))
                        def _():
                            remote(src, dst, mix_send, mix_recv, (chip // 2, chip % 2, h)).start()

        @pl.when(i == 0)
        def _():
            acc_ref[...] = jnp.zeros_like(acc_ref)

        hb = hb_ref[...]
        dob = (FFN_RES * dy_ref[...]).astype(BF16)
        wg_j, wu_j, wd_j = wg_ref[0], wu_ref[0], wd_ref[0]
        a = a_ref[0]
        b = b_ref[0]
        sg = jax.nn.sigmoid(a)
        sa = a * sg
        fb = (sa * b).astype(BF16)
        df = _dot_nt(dob, wd_j)
        dbb = (df * sa).astype(BF16)
        dab = (df * b * (sg + sa * (1.0 - sg))).astype(BF16)
        dhp_ref[0] = (_dot(dab, wg_j) + _dot(dbb, wu_j)).astype(BF16)
        acc_ref[0:FF_SH, :] += _dot_tn(dab, hb)
        acc_ref[FF_SH:2 * FF_SH, :] += _dot_tn(dbb, hb)
        acc_ref[2 * FF_SH:3 * FF_SH, :] += _dot_tn(fb, dob)

        @pl.when(i == nt - 1)
        def _():
            dst = land_ref.at[dev, land_rows]

            @pl.when(t > 0)
            def _():
                for h in range(2):
                    remote(stage_half(h), dst, send_sems.at[h], recv_sem, (tx, ty, h)).wait_send()

            def cast_rows(r, carry):
                rows = pl.ds(pl.multiple_of(r * MIX_HALF, 16), MIX_HALF)
                stage_ref[rows, :] = acc_ref[rows, :].astype(BF16)
                return carry

            lax.fori_loop(0, 3 * FF_SH // MIX_HALF, cast_rows, 0)

            @pl.when(t < N_CHIPS - 1)
            def _():
                for h in range(2):
                    remote(stage_half(h), dst, send_sems.at[h], recv_sem, (tx, ty, h)).start()

            @pl.when(t == N_CHIPS - 1)
            def _():
                own = pltpu.make_async_copy(stage_half(c), dst, local_sem)
                own.start()
                sib = remote(stage_half(1 - c), dst, send_sems.at[0], recv_sem, (xi, yi, 1 - c))
                sib.start()
                sib.wait_send()
                own.wait()
                arrivals = land_ref.at[pl.ds(0, n_others), land_rows]
                remote(arrivals, arrivals, send_sems.at[0], recv_sem, (xi, yi, 1 - c)).wait_recv()
                if with_mix:
                    seven = land_ref.at[pl.ds(0, n_others), mix_rows]
                    both = remote(seven, seven, mix_send, mix_recv, (xi, yi, 1 - c))
                    both.wait_send()
                    both.wait_recv()
                    pltpu.make_async_copy(mix_ref.at[0, pl.ds(0, MIX_HALF)], land_ref.at[dev, mix_rows],
                                          mix_local).wait()

    def wspec(kk):
        return pl.BlockSpec((1, FF_SH, D_MODEL),
                            lambda t, i, pr: (jnp.bitwise_xor(pr[0], (t + 1) % N_CHIPS), kk, 0))

    xspec = pl.BlockSpec((tile, D_MODEL), lambda t, i, pr: (i, 0))
    abspec = pl.BlockSpec((1, tile, FF_SH), lambda t, i, pr: (jnp.bitwise_xor(pr[0], (t + 1) % N_CHIPS), i, 0))
    hbm = pl.BlockSpec(memory_space=pl.ANY)
    in_specs = [xspec, abspec, abspec, xspec, wspec(0), wspec(1), wspec(2)]
    operands = [place, hb, a, b, dy, pack, pack, pack]
    scratch = [pltpu.VMEM((3 * FF_SH, D_MODEL), F32), pltpu.VMEM((3 * FF_SH, D_MODEL), BF16),
               pltpu.SemaphoreType.DMA((2,)), pltpu.SemaphoreType.DMA, pltpu.SemaphoreType.DMA]
    if with_mix:
        in_specs.append(hbm)
        operands.append(mix_grads)
        scratch += [pltpu.SemaphoreType.DMA, pltpu.SemaphoreType.DMA, pltpu.SemaphoreType.DMA]
    aliases = {}
    if with_land:
        in_specs.append(hbm)
        operands.append(land)
        aliases = {len(operands) - 1: 1}
    grid_spec = pltpu.PrefetchScalarGridSpec(
        num_scalar_prefetch=1, grid=(N_CHIPS, nt), in_specs=in_specs,
        out_specs=[pl.BlockSpec((1, tile, D_MODEL), lambda t, i, pr: (t, i, 0)), hbm],
        scratch_shapes=scratch)
    return pl.pallas_call(
        body, name=name, grid_spec=grid_spec,
        out_shape=[jax.ShapeDtypeStruct((N_CHIPS, s, D_MODEL), BF16),
                   jax.ShapeDtypeStruct((2 * N_CHIPS, HALF_ROWS, D_MODEL), BF16)],
        input_output_aliases=aliases,
        compiler_params=_cparams(("arbitrary", "arbitrary")),
    )(*operands)


def _mix_grads_pack(dw_in_t, dw_out, *, name):
    def body(a_ref, b_ref, o_ref):
        o_ref[0, 0:IN_SH, :] = a_ref[0].astype(BF16)
        o_ref[0, IN_SH:FF_SH, :] = b_ref[0].astype(BF16)

    return pl.pallas_call(
        body, name=name, grid=(N_CHIPS,),
        in_specs=[pl.BlockSpec((1, IN_SH, D_MODEL), lambda j: (j, 0, 0)),
                  pl.BlockSpec((1, OUT_SH, D_MODEL), lambda j: (j, 0, 0))],
        out_specs=pl.BlockSpec((1, FF_SH, D_MODEL), lambda j: (j, 0, 0)),
        out_shape=jax.ShapeDtypeStruct((N_CHIPS, FF_SH, D_MODEL), BF16),
        compiler_params=_cparams(("arbitrary",)),
    )(dw_in_t.reshape(N_CHIPS, IN_SH, D_MODEL), dw_out.reshape(N_CHIPS, OUT_SH, D_MODEL))


def _norm_bwd(dhp, x, dy, g, *, tile, name):
    s = x.shape[0]

    def body(dhp_ref, x_ref, dy_ref, g_ref, dx_ref, dg_ref):
        @pl.when(pl.program_id(0) == 0)
        def _():
            dg_ref[...] = jnp.zeros_like(dg_ref)

        dh = ((dhp_ref[0].astype(F32) + dhp_ref[1].astype(F32))
              + (dhp_ref[2].astype(F32) + dhp_ref[3].astype(F32)))
        x_v = x_ref[...]
        r = lax.rsqrt(jnp.mean(x_v * x_v, axis=-1, keepdims=True) + EPS)
        dx, dg = _rms_bwd(dh, x_v, r, g_ref[...])
        dx_ref[...] = dy_ref[...] + dx
        dg_ref[...] += dg

    return pl.pallas_call(
        body, name=name, grid=(s // tile,),
        in_specs=[pl.BlockSpec((N_CHIPS, tile, D_MODEL), lambda i: (0, i, 0)),
                  _rows(tile, D_MODEL), _rows(tile, D_MODEL), _const((1, D_MODEL))],
        out_specs=[_rows(tile, D_MODEL), _const((1, D_MODEL))],
        out_shape=[jax.ShapeDtypeStruct(x.shape, F32), jax.ShapeDtypeStruct((1, D_MODEL), F32)],
        compiler_params=_cparams(("arbitrary",)),
    )(dhp, x, dy, g)


def _mix_in_fwd(x, g, w_in_t, b_in, *, tile, name):
    s = x.shape[0]

    def body(x_ref, g_ref, w_ref, b_ref, q_ref, k_ref, v_ref, z_ref):
        h, _ = _rms(x_ref[...], g_ref[...])
        proj = _dot_nt(h.astype(BF16), w_ref[...]) + b_ref[...]
        q_ref[...] = (proj[:, :ATTN_W] * ATTN_SCALE).astype(BF16)
        k_ref[...] = proj[:, ATTN_W:ATTN_W + KV_W].astype(BF16)
        v_ref[...] = proj[:, ATTN_W + KV_W:ATTN_W + 2 * KV_W].astype(BF16)
        z_ref[...] = proj[:, ATTN_W + 2 * KV_W:]

    return pl.pallas_call(
        body, name=name, grid=(s // tile,),
        in_specs=[_rows(tile, D_MODEL), _const((1, D_MODEL)), _const((IN_W, D_MODEL)), _const((1, IN_W))],
        out_specs=[_rows(tile, ATTN_W), _rows(tile, KV_W), _rows(tile, KV_W), _rows(tile, 2 * GMLP_W)],
        out_shape=[jax.ShapeDtypeStruct((s, ATTN_W), BF16), jax.ShapeDtypeStruct((s, KV_W), BF16),
                   jax.ShapeDtypeStruct((s, KV_W), BF16), jax.ShapeDtypeStruct((s, 2 * GMLP_W), F32)],
        compiler_params=_cparams(("arbitrary",)),
    )(x, g, w_in_t, b_in)


def _mix_in_bwd(x, dy, dq, dk, dv, dz, g, w_in_t, *, tile, name):
    s = x.shape[0]

    def body(x_ref, dy_ref, dq_ref, dk_ref, dv_ref, dz_ref, g_ref, w_ref, dx_ref, dw_ref, db_ref, dg_ref):
        @pl.when(pl.program_id(0) == 0)
        def _():
            dw_ref[...] = jnp.zeros_like(dw_ref)
            db_ref[...] = jnp.zeros_like(db_ref)
            dg_ref[...] = jnp.zeros_like(dg_ref)

        dproj = jnp.concatenate([dq_ref[...], dk_ref[...], dv_ref[...], dz_ref[...]], axis=-1)
        db_ref[...] += jnp.sum(dproj, axis=0, keepdims=True)
        dpb = dproj.astype(BF16)
        x_v = x_ref[...]
        g_v = g_ref[...]
        h, r = _rms(x_v, g_v)
        dw_ref[...] += _dot_tn(dpb, h.astype(BF16))
        dh = _dot(dpb, w_ref[...])
        dx, dg = _rms_bwd(dh, x_v, r, g_v)
        dx_ref[...] = dy_ref[...] + dx
        dg_ref[...] += dg

    return pl.pallas_call(
        body, name=name, grid=(s // tile,),
        in_specs=[_rows(tile, D_MODEL), _rows(tile, D_MODEL), _rows(tile, ATTN_W), _rows(tile, KV_W),
                  _rows(tile, KV_W), _rows(tile, 2 * GMLP_W), _const((1, D_MODEL)), _const((IN_W, D_MODEL))],
        out_specs=[_rows(tile, D_MODEL), _const((IN_W, D_MODEL)), _const((1, IN_W)), _const((1, D_MODEL))],
        out_shape=[jax.ShapeDtypeStruct(x.shape, F32), jax.ShapeDtypeStruct((IN_W, D_MODEL), F32),
                   jax.ShapeDtypeStruct((1, IN_W), F32), jax.ShapeDtypeStruct((1, D_MODEL), F32)],
        compiler_params=_cparams(("arbitrary",)),
    )(x, dy, dq, dk, dv, dz, g, w_in_t)


_GELU_C = 0.7978845608028654
_GELU_A = 0.044715


def _gelu_tanh(x):
    x2 = x * x
    return jnp.tanh(_GELU_C * (x + _GELU_A * (x2 * x))), x2


def _band(ref, i):
    prev = jnp.maximum(i - 1, 0)
    return jnp.concatenate([ref[pl.ds(pl.multiple_of(prev * BLK, BLK), BLK), :],
                            ref[pl.ds(pl.multiple_of(i * BLK, BLK), BLK), :]], axis=0)


def _key_in_block():
    return lax.broadcasted_iota(jnp.int32, (BLK, BLK), 0) <= lax.broadcasted_iota(jnp.int32, (BLK, BLK), 1)


def _fold(band, own):
    return jnp.where(own, band[BLK:], band[:BLK])


def _unfold(a, own):
    zero = jnp.zeros_like(a)
    return jnp.concatenate([jnp.where(own, zero, a), jnp.where(own, a, zero)], axis=0).astype(BF16)


def _attn_fwd(q, kb, vb, i, sink_ref):
    own = _key_in_block()
    outs, saved = [], []
    for h in range(N_Q_HEADS):
        cols = slice((h // REP) * HEAD_DIM, (h // REP + 1) * HEAD_DIM)
        s2 = _dot_nt(kb[:, cols], q[:, h * HEAD_DIM:(h + 1) * HEAD_DIM])
        sc = jnp.where(own, s2[BLK:], jnp.where(i > 0, s2[:BLK], -jnp.inf))
        sink = sink_ref[h]
        m = jnp.maximum(jnp.max(sc, axis=0, keepdims=True), sink)
        p = jnp.exp(sc - m)
        es = jnp.exp(sink - m)
        inv = 1.0 / (jnp.sum(p, axis=0, keepdims=True) + es)
        pn = p * inv
        band = _unfold(pn, own)
        outs.append(_dot_tn(band, vb[:, cols]))
        saved.append((pn, band, es * inv))
    return jnp.concatenate(outs, axis=-1), saved


def _tril_mask():
    t = lax.broadcasted_iota(jnp.int32, (BLK, BLK), 0)
    s_ = lax.broadcasted_iota(jnp.int32, (BLK, BLK), 1)
    return s_ <= t


def _gmlp_fwd_parts(zg, lng, lnb, ws_ref, bs_full):
    th, zg2 = _gelu_tanh(zg)
    z = 0.5 * zg * (1.0 + th)
    u = z[:, :GMLP_W]
    zv = z[:, GMLP_W:]
    mu = jnp.mean(zv, axis=-1, keepdims=True)
    zc = zv - mu
    rstd = lax.rsqrt(jnp.mean(zc * zc, axis=-1, keepdims=True) + EPS)
    xh = zc * rstd
    vvb = (xh * lng + lnb).astype(BF16)
    tril = _tril_mask()
    wms, parts = [], []
    for gi in range(GMLP_GROUPS):
        wm = jnp.where(tril, ws_ref[gi], 0.0).astype(BF16)
        wms.append(wm)
        parts.append(_dot(wm, vvb[:, gi * GROUP_DIM:(gi + 1) * GROUP_DIM]))
    mixed = jnp.concatenate(parts, axis=-1) + bs_full
    gelu_grad = 0.5 * (1.0 + th) + 0.5 * zg * (1.0 - th * th) * (_GELU_C * (1.0 + 3.0 * _GELU_A * zg2))
    return u, xh, rstd, vvb, wms, mixed, gelu_grad


def _mix_core_fwd(q, k, v, zg, sinks, lng, lnb, w_s, bs_full, gao, ggo, *, name):
    s = q.shape[0]
    nb = min(MIX_FWD_BLOCKS, s // BLK)

    def body(sink_ref, q_ref, k_ref, v_ref, z_ref, lng_ref, lnb_ref, ws_ref, bs_ref, gao_ref, ggo_ref, o_ref):
        for b in range(nb):
            blk = pl.program_id(0) * nb + b
            rows = slice(b * BLK, (b + 1) * BLK)
            y_attn, _ = _attn_fwd(q_ref[rows, :], _band(k_ref, blk), _band(v_ref, blk), blk, sink_ref)
            u, _, _, _, _, mixed, _ = _gmlp_fwd_parts(z_ref[rows, :], lng_ref[...], lnb_ref[...], ws_ref,
                                                      bs_ref[...])
            ya, _ = _rms(y_attn, gao_ref[...])
            yg, _ = _rms(u * mixed, ggo_ref[...])
            o_ref[rows, :] = jnp.concatenate([ya, yg], axis=-1).astype(BF16)

    return pl.pallas_call(
        body, name=name, grid=(s // (nb * BLK),),
        in_specs=[pl.BlockSpec(memory_space=pltpu.SMEM),
                  _rows(nb * BLK, ATTN_W), _const((s, KV_W)), _const((s, KV_W)),
                  _rows(nb * BLK, 2 * GMLP_W), _const((1, GMLP_W)), _const((1, GMLP_W)),
                  _const((GMLP_GROUPS, BLK, BLK)), _const((BLK, GMLP_W)), _const((1, ATTN_W)), _const((1, GMLP_W))],
        out_specs=_rows(nb * BLK, D_MODEL),
        out_shape=jax.ShapeDtypeStruct((s, D_MODEL), BF16),
        compiler_params=_cparams(("arbitrary",)),
    )(sinks, q, k, v, zg, lng, lnb, w_s, bs_full, gao, ggo)


def _mix_out_fwd(x1, yb, w_out, b_out, *, tile, name):
    s = x1.shape[0]

    def body(x_ref, y_ref, w_ref, b_ref, o_ref):
        o_ref[...] = x_ref[...] + (_dot(y_ref[...], w_ref[...]) + b_ref[...])

    return pl.pallas_call(
        body, name=name, grid=(s // tile,),
        in_specs=[_rows(tile, D_MODEL), _rows(tile, D_MODEL), _const((D_MODEL, D_MODEL)), _const((1, D_MODEL))],
        out_specs=_rows(tile, D_MODEL),
        out_shape=jax.ShapeDtypeStruct(x1.shape, F32),
        compiler_params=_cparams(("arbitrary",)),
    )(x1, yb, w_out, b_out)


def _norm_bwd_mix_out(dhp, x, dy, g, yb, w_out, *, tile, name):
    s = x.shape[0]

    def body(dhp_ref, x_ref, dy_ref, g_ref, y_ref, w_ref, dx_ref, dg_ref, dyy_ref, dw_ref, db_ref):
        @pl.when(pl.program_id(0) == 0)
        def _():
            dg_ref[...] = jnp.zeros_like(dg_ref)
            dw_ref[...] = jnp.zeros_like(dw_ref)
            db_ref[...] = jnp.zeros_like(db_ref)

        dh = ((dhp_ref[0].astype(F32) + dhp_ref[1].astype(F32))
              + (dhp_ref[2].astype(F32) + dhp_ref[3].astype(F32)))
        x_v = x_ref[...]
        r = lax.rsqrt(jnp.mean(x_v * x_v, axis=-1, keepdims=True) + EPS)
        dxn, dg = _rms_bwd(dh, x_v, r, g_ref[...])
        dx = dy_ref[...] + dxn
        dx_ref[...] = dx
        dg_ref[...] += dg
        dxb = dx.astype(BF16)
        db_ref[...] += jnp.sum(dx, axis=0, keepdims=True)
        dw_ref[...] += _dot_tn(y_ref[...], dxb)
        dyy_ref[...] = _dot_nt(dxb, w_ref[...])

    return pl.pallas_call(
        body, name=name, grid=(s // tile,),
        in_specs=[pl.BlockSpec((N_CHIPS, tile, D_MODEL), lambda i: (0, i, 0)),
                  _rows(tile, D_MODEL), _rows(tile, D_MODEL), _const((1, D_MODEL)), _rows(tile, D_MODEL),
                  _const((D_MODEL, D_MODEL))],
        out_specs=[_rows(tile, D_MODEL), _const((1, D_MODEL)), _rows(tile, D_MODEL), _const((D_MODEL, D_MODEL)),
                   _const((1, D_MODEL))],
        out_shape=[jax.ShapeDtypeStruct(x.shape, F32), jax.ShapeDtypeStruct((1, D_MODEL), F32),
                   jax.ShapeDtypeStruct(x.shape, F32), jax.ShapeDtypeStruct((D_MODEL, D_MODEL), F32),
                   jax.ShapeDtypeStruct((1, D_MODEL), F32)],
        compiler_params=_cparams(("arbitrary",)),
    )(dhp, x, dy, g, yb, w_out)


def _mix_core_bwd(dyy, q, k, v, zg, sinks, lng, lnb, w_s, bs_full, gao, ggo, *, name):
    s = dyy.shape[0]
    nb = min(MIX_BWD_BLOCKS, s // BLK)
    nsteps = s // (nb * BLK)

    def body(*refs):
        accumulators = refs[13:15] + refs[16:]

        @pl.when(pl.program_id(0) == 0)
        def _():
            for ref in accumulators:
                ref[...] = jnp.zeros_like(ref)

        for b in range(nb):
            one_block(pl.program_id(0) * nb + b, slice(b * BLK, (b + 1) * BLK), *refs)

        @pl.when(pl.program_id(0) == nsteps - 1)
        def _():
            tril = _tril_mask()
            for gi in range(GMLP_GROUPS):
                refs[20][gi] = jnp.where(tril, refs[20][gi], 0.0)

    def one_block(i, rows, sink_ref, dyy_ref, q_ref, k_ref, v_ref, z_ref, lng_ref, lnb_ref, ws_ref, bs_ref, gao_ref,
                  ggo_ref, dq_ref, dk_ref, dv_ref, dz_ref, dgao_ref, dggo_ref, dlng_ref, dlnb_ref, dws_ref, dms_ref,
                  dsk_ref):
        q_v = q_ref[rows, :]
        kb = _band(k_ref, i)
        vb = _band(v_ref, i)
        lng_v = lng_ref[...]
        gao_v = gao_ref[...]
        ggo_v = ggo_ref[...]

        y_attn, probs = _attn_fwd(q_v, kb, vb, i, sink_ref)
        u, xh, rstd, vvb, wms, mixed, gelu_grad = _gmlp_fwd_parts(z_ref[rows, :], lng_v, lnb_ref[...], ws_ref,
                                                                  bs_ref[...])
        y_gmlp = u * mixed
        ra = lax.rsqrt(jnp.mean(y_attn * y_attn, axis=-1, keepdims=True) + EPS)
        rg = lax.rsqrt(jnp.mean(y_gmlp * y_gmlp, axis=-1, keepdims=True) + EPS)

        dyy = dyy_ref[rows, :]
        d_attn, dgao = _rms_bwd(dyy[:, :ATTN_W], y_attn, ra, gao_v)
        d_gmlp, dggo = _rms_bwd(dyy[:, ATTN_W:], y_gmlp, rg, ggo_v)
        dgao_ref[...] += dgao
        dggo_ref[...] += dggo

        du = d_gmlp * mixed
        dmixed = d_gmlp * u
        dms_ref[...] += dmixed
        dmb = dmixed.astype(BF16)
        dvv_parts = []
        for gi in range(GMLP_GROUPS):
            sl = slice(gi * GROUP_DIM, (gi + 1) * GROUP_DIM)
            dws_ref[gi] += _dot_nt(dmb[:, sl], vvb[:, sl])
            dvv_parts.append(_dot_tn(wms[gi], dmb[:, sl]))
        dvv = jnp.concatenate(dvv_parts, axis=-1)
        dlng_ref[...] += jnp.sum(dvv * xh, axis=0, keepdims=True)
        dlnb_ref[...] += jnp.sum(dvv, axis=0, keepdims=True)
        dxh = dvv * lng_v
        dzv = rstd * (dxh - jnp.mean(dxh, axis=-1, keepdims=True)
                      - xh * jnp.mean(dxh * xh, axis=-1, keepdims=True))
        dz_ref[rows, :] = jnp.concatenate([du, dzv], axis=-1) * gelu_grad

        dab = d_attn.astype(BF16)
        own = _key_in_block()
        dq_parts = []
        dk_parts = []
        dv_parts = []
        for gi in range(N_KV_HEADS):
            cols = slice(gi * HEAD_DIM, (gi + 1) * HEAD_DIM)
            kg, vg = kb[:, cols], vb[:, cols]
            dkg = jnp.zeros((2 * BLK, HEAD_DIM), F32)
            dvg = jnp.zeros((2 * BLK, HEAD_DIM), F32)
            for rr in range(REP):
                h = gi * REP + rr
                hs = slice(h * HEAD_DIM, (h + 1) * HEAD_DIM)
                qh, doh = q_v[:, hs], dab[:, hs]
                pn, band, psink = probs[h]
                dp = _fold(_dot_nt(vg, doh), own)
                delta = jnp.sum(pn * dp, axis=0, keepdims=True)
                ds2 = _unfold(pn * (dp - delta), own)
                dsink = jnp.sum(-psink * delta, axis=-1, keepdims=True)
                dsk_ref[pl.ds(h, 1), :] += jnp.broadcast_to(dsink, (1, 128))
                dq_parts.append(_dot_tn(ds2, kg) * ATTN_SCALE)
                dkg = dkg + _dot(ds2, qh)
                dvg = dvg + _dot(band, doh)
            dk_parts.append(dkg)
            dv_parts.append(dvg)
        dq_ref[rows, :] = jnp.concatenate(dq_parts, axis=-1)
        dkb = jnp.concatenate(dk_parts, axis=-1)
        dvb = jnp.concatenate(dv_parts, axis=-1)
        prev = pl.ds(pl.multiple_of(jnp.maximum(i - 1, 0) * BLK, BLK), BLK)
        cur = pl.ds(pl.multiple_of(i * BLK, BLK), BLK)
        dk_ref[prev, :] += dkb[:BLK]
        dv_ref[prev, :] += dvb[:BLK]
        dk_ref[cur, :] += dkb[BLK:]
        dv_ref[cur, :] += dvb[BLK:]

    return pl.pallas_call(
        body, name=name, grid=(nsteps,),
        in_specs=[pl.BlockSpec(memory_space=pltpu.SMEM),
                  _rows(nb * BLK, D_MODEL), _rows(nb * BLK, ATTN_W), _const((s, KV_W)), _const((s, KV_W)),
                  _rows(nb * BLK, 2 * GMLP_W), _const((1, GMLP_W)), _const((1, GMLP_W)),
                  _const((GMLP_GROUPS, BLK, BLK)), _const((BLK, GMLP_W)), _const((1, ATTN_W)), _const((1, GMLP_W))],
        out_specs=[_rows(nb * BLK, ATTN_W), _const((s, KV_W)), _const((s, KV_W)), _rows(nb * BLK, 2 * GMLP_W),
                   _const((1, ATTN_W)), _const((1, GMLP_W)),
                   _const((1, GMLP_W)), _const((1, GMLP_W)), _const((GMLP_GROUPS, BLK, BLK)),
                   _const((BLK, GMLP_W)), _const((N_Q_HEADS, 128))],
        out_shape=[jax.ShapeDtypeStruct((s, ATTN_W), F32), jax.ShapeDtypeStruct((s, KV_W), F32),
                   jax.ShapeDtypeStruct((s, KV_W), F32), jax.ShapeDtypeStruct((s, 2 * GMLP_W), F32),
                   jax.ShapeDtypeStruct((1, ATTN_W), F32), jax.ShapeDtypeStruct((1, GMLP_W), F32),
                   jax.ShapeDtypeStruct((1, GMLP_W), F32), jax.ShapeDtypeStruct((1, GMLP_W), F32),
                   jax.ShapeDtypeStruct((GMLP_GROUPS, BLK, BLK), F32), jax.ShapeDtypeStruct((BLK, GMLP_W), F32),
                   jax.ShapeDtypeStruct((N_Q_HEADS, 128), F32)],
        compiler_params=_cparams(("arbitrary",)),
    )(sinks, dyy, q, k, v, zg, lng, lnb, w_s, bs_full, gao, ggo)


def _local_step(place, x, tgt, p, pack_a, pack_b, *, tile=512, fwd_tile=256, bwd_tile=512, norm_tile=512):
    g = {}
    tile, fwd_tile, bwd_tile, norm_tile = (min(t_, x.shape[0]) for t_ in (tile, fwd_tile, bwd_tile, norm_tile))
    x1, hb1, a1, b1, pack_b = _ffn_fwd(x, p["ffn1_norm_g"], pack_a, 0, pack_b, tile=fwd_tile, name="ffn1_fwd")
    mix_rows = pack_b[:, 3 * FF_SH:, :]
    w_in_t = mix_rows[:, :IN_SH, :].reshape(IN_W, D_MODEL)
    w_out = mix_rows[:, IN_SH:, :].reshape(D_MODEL, D_MODEL)
    q, k, v, zg = _mix_in_fwd(x1, p["mix_norm_g"], w_in_t, p["b_in"], tile=tile, name="mix_in_fwd")
    mix_args = (q, k, v, zg, p["attn_sinks"], p["gmlp_ln_g"], p["gmlp_ln_b"], p["gmlp_w_s"], p["bs_full"],
                p["attn_out_norm_g"], p["gmlp_out_norm_g"])
    yb = _mix_core_fwd(*mix_args, name="mix_core_fwd")
    x2 = _mix_out_fwd(x1, yb, w_out, p["b_out"], tile=tile, name="mix_out_fwd")
    dx3, loss, g["final_norm_g"], hb2, a2, b2 = _ffn_fwd_loss(
        x2, p["ffn2_norm_g"], pack_b, 0, p["final_norm_g"], tgt, tile=fwd_tile, name="ffn2_fwd_loss")

    dhp, land = _ffn_bwd(place, hb2, a2, b2, dx3, pack_b, 1, None, None, tile=bwd_tile, name="ffn2_bwd")
    dx2, g["ffn2_norm_g"], dyy, dw_out, g["b_out"] = _norm_bwd_mix_out(
        dhp, x2, dx3, p["ffn2_norm_g"], yb, w_out, tile=norm_tile, name="ffn2_norm_bwd")

    (dq, dk, dv, dz, g["attn_out_norm_g"], g["gmlp_out_norm_g"], g["gmlp_ln_g"],
     g["gmlp_ln_b"], g["gmlp_w_s"], dmix_sum, dsinks) = _mix_core_bwd(dyy, *mix_args, name="mix_core_bwd")
    g["gmlp_b_s"] = dmix_sum
    g["attn_sinks"] = dsinks
    dx1, dw_in_t, g["b_in"], g["mix_norm_g"] = _mix_in_bwd(
        x1, dx2, dq, dk, dv, dz, p["mix_norm_g"], w_in_t, tile=tile, name="mix_in_bwd")
    mix_grads = _mix_grads_pack(dw_in_t, dw_out, name="mix_grads_pack")

    dhp1, land = _ffn_bwd(place, hb1, a1, b1, dx1, pack_a, 0, land, mix_grads, tile=bwd_tile, name="ffn1_bwd")
    dx0, g["ffn1_norm_g"] = _norm_bwd(dhp1, x, dx1, p["ffn1_norm_g"], tile=norm_tile, name="ffn1_norm_bwd")
    return loss, dx0, land, g


def _pack_cast(place, parts, *, name):
    def body(place_ref, *refs):
        oa_ref, ob_ref = refs[-2], refs[-1]
        off = 0
        for k, (ref, rows) in enumerate(zip(refs[:-2], BIG_ROWS)):
            if k == 3:
                off = 0
            (oa_ref if k < 3 else ob_ref)[0, off:off + rows, :] = ref[...].astype(BF16)
            off += rows

    one = pl.Buffered(1)

    def slab(rows):
        return pl.BlockSpec((1, rows, D_MODEL), lambda i, pr: (pr[0], 0, 0), pipeline_mode=one)

    grid_spec = pltpu.PrefetchScalarGridSpec(
        num_scalar_prefetch=1, grid=(1,),
        in_specs=[pl.BlockSpec((rows, D_MODEL), lambda i, pr: (0, 0), pipeline_mode=one) for rows in BIG_ROWS],
        out_specs=[slab(PACK_A_ROWS), slab(PACK_B_ROWS)])
    return pl.pallas_call(
        body, name=name, grid_spec=grid_spec,
        out_shape=[jax.ShapeDtypeStruct((N_CHIPS, PACK_A_ROWS, D_MODEL), BF16),
                   jax.ShapeDtypeStruct((N_CHIPS, PACK_B_ROWS, D_MODEL), BF16)],
        compiler_params=_cparams(("arbitrary",)),
    )(place, *parts)


def _all_gather_pack(pack, *, name):
    def body(p_ref, o_ref, send_sems, recv_sems):
        start, forward, finish = _gather_stages(o_ref, send_sems, recv_sems)
        start()
        forward()
        finish()

    return pl.pallas_call(
        body, name=name,
        in_specs=[pl.BlockSpec(memory_space=pl.ANY)],
        out_specs=pl.BlockSpec(memory_space=pl.ANY),
        out_shape=jax.ShapeDtypeStruct(pack.shape, pack.dtype),
        input_output_aliases={0: 0},
        scratch_shapes=[pltpu.SemaphoreType.DMA((6,)), pltpu.SemaphoreType.DMA((6,))],
    )(pack)


def _shard_tile(i, c):
    return jnp.where(i < 3, 3 * c + i, jnp.where(i < 6, 3 + 3 * c + i, 12 + c))


def _rs_reduce(place, land, *, name):
    def body(place_ref, l_ref, o_ref):
        acc = l_ref[0].astype(F32)
        for d in range(1, 2 * N_CHIPS):
            acc = acc + l_ref[d].astype(F32)
        o_ref[...] = acc

    grid_spec = pltpu.PrefetchScalarGridSpec(
        num_scalar_prefetch=1, grid=(HALF_ROWS // MIX_HALF,),
        in_specs=[pl.BlockSpec((2 * N_CHIPS, MIX_HALF, D_MODEL), lambda i, pr: (0, i, 0))],
        out_specs=pl.BlockSpec((MIX_HALF, D_MODEL), lambda i, pr: (_shard_tile(i, pr[1]), 0)))
    return pl.pallas_call(
        body, name=name, grid_spec=grid_spec,
        out_shape=jax.ShapeDtypeStruct((PACK_ROWS, D_MODEL), F32),
        compiler_params=_cparams(("arbitrary",)),
    )(place, land)


def _rs_share(shard, *, name):
    def body(s_ref, o_ref, send_sems, recv_sems):
        x, y, c, _ = _mesh_place()

        def rows(k, core):
            if k < 2:
                return o_ref.at[pl.ds(pl.multiple_of(k * 2 * FFN_HALF + core * FFN_HALF, 8), FFN_HALF)]
            return o_ref.at[pl.ds(pl.multiple_of(4 * FFN_HALF + core * MIX_HALF, 8), MIX_HALF)]

        def copy(k, core):
            return pltpu.make_async_remote_copy(src_ref=rows(k, core), dst_ref=rows(k, core), send_sem=send_sems.at[k],
                                                recv_sem=recv_sems.at[k], device_id=(x, y, 1 - c),
                                                device_id_type=MESH)

        sends = [copy(k, c) for k in range(3)]
        for cp in sends:
            cp.start()
        for k in range(3):
            copy(k, 1 - c).wait_recv()
        for cp in sends:
            cp.wait_send()

    return pl.pallas_call(
        body, name=name,
        in_specs=[pl.BlockSpec(memory_space=pl.ANY)],
        out_specs=pl.BlockSpec(memory_space=pl.ANY),
        out_shape=jax.ShapeDtypeStruct((PACK_ROWS, D_MODEL), F32),
        input_output_aliases={0: 0},
        scratch_shapes=[pltpu.SemaphoreType.DMA((3,)), pltpu.SemaphoreType.DMA((3,))],
    )(shard)


def _small_all_reduce(packed, *, name):
    rows = packed.shape[0]
    half = rows // 2

    def body(p_ref, o_ref, sib_ref, slots_ref, send_sems, recv_sems):
        x, y, c, others = _mesh_place()
        me = 2 * x + y
        sibling = (x, y, 1 - c)

        def half_of(core):
            return pl.ds(pl.multiple_of(core * half, 8), half)

        def remote(k, src, dst, to):
            return pltpu.make_async_remote_copy(src_ref=src, dst_ref=dst, send_sem=send_sems.at[k],
                                                recv_sem=recv_sems.at[k], device_id=to, device_id_type=MESH)

        sib = remote(0, p_ref.at[half_of(1 - c)], sib_ref, sibling)
        sib.start()
        sib.wait()
        slots_ref[me] = p_ref[half_of(c), :] + sib_ref[...]
        sends = [remote(1 + j, slots_ref.at[me], slots_ref.at[me], (px, py, c)) for j, (px, py) in enumerate(others)]
        for cp in sends:
            cp.start()
        for j, (px, py) in enumerate(others):
            slab = slots_ref.at[2 * px + py]
            remote(1 + j, slab, slab, (px, py, c)).wait_recv()
        for cp in sends:
            cp.wait_send()
        o_ref[half_of(c), :] = (slots_ref[0] + slots_ref[1]) + (slots_ref[2] + slots_ref[3])
        back = remote(4, o_ref.at[half_of(c)], o_ref.at[half_of(c)], sibling)
        back.start()
        remote(4, o_ref.at[half_of(1 - c)], o_ref.at[half_of(1 - c)], sibling).wait_recv()
        back.wait_send()

    vm = pl.BlockSpec(memory_space=pltpu.VMEM)
    return pl.pallas_call(
        body, name=name, in_specs=[vm], out_specs=vm,
        out_shape=jax.ShapeDtypeStruct((rows, 128), F32),
        scratch_shapes=[pltpu.VMEM((half, 128), F32), pltpu.VMEM((N_CHIPS, half, 128), F32),
                        pltpu.SemaphoreType.DMA((5,)), pltpu.SemaphoreType.DMA((5,))],
    )(packed)


def _adamw(w, g, m, v, *, g_row0, tile, name):
    rows, cols = w.shape
    assert g_row0 % tile == 0 and rows % tile == 0

    def body(w_ref, g_ref, m_ref, v_ref, go_ref, d_ref, nm_ref, nv_ref):
        g_v = g_ref[...]
        m_n = ADAM_B1 * m_ref[...] + (1.0 - ADAM_B1) * g_v
        v_n = ADAM_B2 * v_ref[...] + (1.0 - ADAM_B2) * (g_v * g_v)
        m_hat = m_n / (1.0 - ADAM_B1 ** ADAM_STEP)
        v_hat = v_n / (1.0 - ADAM_B2 ** ADAM_STEP)
        d_ref[...] = -ADAM_LR * (m_hat / (jnp.sqrt(v_hat) + ADAM_EPS) + ADAM_WD * w_ref[...])
        go_ref[...] = g_v
        nm_ref[...] = m_n
        nv_ref[...] = v_n

    spec = pl.BlockSpec((tile, cols), lambda i: (i, 0))
    gspec = pl.BlockSpec((tile, cols), lambda i: (g_row0 // tile + i, 0))
    shape = jax.ShapeDtypeStruct((rows, cols), F32)
    return pl.pallas_call(
        body, name=name, grid=(rows // tile,),
        in_specs=[spec, gspec, spec, spec], out_specs=[spec] * 4, out_shape=[shape] * 4,
        compiler_params=_cparams(("arbitrary",)),
    )(w, g, m, v)


def kernel(x, ffn1_norm_g, ffn1_w_gate, ffn1_w_up, ffn1_w_down, mix_norm_g, w_in, b_in, attn_sinks, gmlp_ln_g, gmlp_ln_b, gmlp_w_s, gmlp_b_s, attn_out_norm_g, gmlp_out_norm_g, w_out, b_out, ffn2_norm_g, ffn2_w_gate, ffn2_w_up, ffn2_w_down, final_norm_g, loss_target, m_ffn1_norm_g, m_ffn1_w_gate, m_ffn1_w_up, m_ffn1_w_down, m_mix_norm_g, m_w_in, m_b_in, m_attn_sinks, m_gmlp_ln_g, m_gmlp_ln_b, m_gmlp_w_s, m_gmlp_b_s, m_attn_out_norm_g, m_gmlp_out_norm_g, m_w_out, m_b_out, m_ffn2_norm_g, m_ffn2_w_gate, m_ffn2_w_up, m_ffn2_w_down, m_final_norm_g, v_ffn1_norm_g, v_ffn1_w_gate, v_ffn1_w_up, v_ffn1_w_down, v_mix_norm_g, v_w_in, v_b_in, v_attn_sinks, v_gmlp_ln_g, v_gmlp_ln_b, v_gmlp_w_s, v_gmlp_b_s, v_attn_out_norm_g, v_gmlp_out_norm_g, v_w_out, v_b_out, v_ffn2_norm_g, v_ffn2_w_gate, v_ffn2_w_up, v_ffn2_w_down, v_final_norm_g):
    f_args = dict(locals())
    weights = {n: f_args[n] for n in [nm for nm, _ in SMALL if nm != "loss"] + list(BIG)}
    shapes = {n: weights[n].shape for n in weights}
    shapes["loss"] = ()
    place = jnp.stack([2 * lax.axis_index("x") + lax.axis_index("y"), lax.axis_index("c")]).astype(jnp.int32)

    def with_cols(name, a):
        a2 = a.reshape(a.shape[-2], a.shape[-1])
        return a2.T if BIG_TRANSPOSED[BIG.index(name)] else a2

    def natural(name, a2):
        return (a2.T if BIG_TRANSPOSED[BIG.index(name)] else a2).reshape(shapes[name])

    pack_a, pack_b = _pack_cast(place, [with_cols(n, weights[n]) for n in BIG], name="pack_cast")
    pack_a = _all_gather_pack(pack_a, name="ag_weights")
    p = {n: weights[n].reshape(1, -1) for n in ("ffn1_norm_g", "mix_norm_g", "b_in", "gmlp_ln_g", "gmlp_ln_b",
                                                "attn_out_norm_g", "gmlp_out_norm_g", "b_out", "ffn2_norm_g",
                                                "final_norm_g")}
    p["attn_sinks"] = attn_sinks.reshape(N_Q_HEADS)
    p["gmlp_w_s"] = gmlp_w_s.reshape(GMLP_GROUPS, BLK, BLK)
    p["bs_full"] = jnp.broadcast_to(gmlp_b_s.reshape(GMLP_GROUPS, BLK).T[:, :, None],
                                    (BLK, GMLP_GROUPS, GROUP_DIM)).reshape(BLK, GMLP_W)

    loss_part, dx0, land, gs = _local_step(place, x[0], loss_target[0], p, pack_a, pack_b)

    shard = _rs_share(_rs_reduce(place, land, name="rs_reduce"), name="rs_share")
    gs["gmlp_b_s"] = jnp.sum(gs["gmlp_b_s"].reshape(BLK, GMLP_GROUPS, GROUP_DIM), axis=-1).T
    gs["attn_sinks"] = gs["attn_sinks"][:, 0]
    gs["loss"] = loss_part[0, 0]
    small_sum = _small_all_reduce(_pack_small(gs), name="small_all_reduce")

    grad_w, delta, new_m, new_v = {}, {}, {}, {}
    off = 0
    for n, rows in zip(BIG, BIG_ROWS):
        res = _adamw(with_cols(n, weights[n]), shard, with_cols(n, f_args["m_" + n]), with_cols(n, f_args["v_" + n]),
                     g_row0=off, tile=FF_SH // 2 if rows == FF_SH else 64, name="adamw_" + n)
        grad_w[n], delta[n], new_m[n], new_v[n] = [natural(n, a) for a in res]
        off += rows
    sm = {k: {n: f_args[k + n] for n, _ in SMALL if n != "loss"} for k in ("", "m_", "v_")}
    for k in sm:
        sm[k]["loss"] = jnp.zeros((), F32)
    res = _adamw(_pack_small(sm[""]), small_sum, _pack_small(sm["m_"]), _pack_small(sm["v_"]),
                 g_row0=0, tile=SMALL_ROWS, name="adamw_small")
    small = _unpack_small(res[0], shapes)
    for dst, packed in ((grad_w, res[0]), (delta, res[1]), (new_m, res[2]), (new_v, res[3])):
        dst.update({n: a for n, a in _unpack_small(packed, shapes).items() if n != "loss"})

    order = ('ffn1_norm_g', 'ffn1_w_gate', 'ffn1_w_up', 'ffn1_w_down', 'mix_norm_g', 'w_in', 'b_in', 'attn_sinks',
             'gmlp_ln_g', 'gmlp_ln_b', 'gmlp_w_s', 'gmlp_b_s', 'attn_out_norm_g', 'gmlp_out_norm_g', 'w_out', 'b_out',
             'ffn2_norm_g', 'ffn2_w_gate', 'ffn2_w_up', 'ffn2_w_down', 'final_norm_g')
    return (small["loss"], dx0.reshape(x.shape), *[grad_w[n] for n in order], *[delta[n] for n in order],
            *[new_m[n] for n in order], *[new_v[n] for n in order])
```

```python
import functools

import jax
import jax.numpy as jnp
from jax import lax
from jax.experimental import pallas as pl
from jax.experimental.pallas import tpu as pltpu

F32 = jnp.float32
BF16 = jnp.bfloat16

D_MODEL = 1024
D_FF = 2816
N_CHIPS = 4
FF_SH = D_FF // N_CHIPS
N_Q_HEADS = 8
N_KV_HEADS = 2
REP = N_Q_HEADS // N_KV_HEADS
HEAD_DIM = 64
ATTN_W = 512
KV_W = 128
GMLP_W = 512
GMLP_GROUPS = 8
GROUP_DIM = 64
BLK = 128
MIX_FWD_BLOCKS = 2
MIX_BWD_BLOCKS = 4
IN_W = 1792
IN_SH = IN_W // N_CHIPS
OUT_SH = D_MODEL // N_CHIPS
EPS = 1e-6
FFN_RES = 0.5
ATTN_SCALE = HEAD_DIM ** -0.5

ADAM_LR = 0.001
ADAM_B1 = 0.9
ADAM_B2 = 0.999
ADAM_EPS = 1e-08
ADAM_WD = 0.01
ADAM_STEP = 10

V7X_VMEM_LIMIT = 56 * 1024 * 1024
MESH = pl.DeviceIdType.MESH


def _cparams(sem):
    return pltpu.CompilerParams(dimension_semantics=sem, vmem_limit_bytes=V7X_VMEM_LIMIT)


def _dot(a, b):
    return jnp.dot(a, b, preferred_element_type=F32)


def _dot_nt(a, b):
    return lax.dot_general(a, b, (((1,), (1,)), ((), ())), preferred_element_type=F32)


def _dot_tn(a, b):
    return lax.dot_general(a, b, (((0,), (0,)), ((), ())), preferred_element_type=F32)


def _rms(x, g):
    r = lax.rsqrt(jnp.mean(x * x, axis=-1, keepdims=True) + EPS)
    return x * r * g, r


def _rms_bwd(dh, x, r, g):
    gy = dh * g
    dx = r * gy - x * (r * r * r) * jnp.mean(gy * x, axis=-1, keepdims=True)
    dg = jnp.sum(dh * x * r, axis=0, keepdims=True)
    return dx, dg


def _const(shape):
    nd = len(shape)
    return pl.BlockSpec(shape, lambda *_: (0,) * nd)


def _rows(t, w):
    return pl.BlockSpec((t, w), lambda i: (i, 0))


PACK_ROWS = 7 * FF_SH
HALF_ROWS = PACK_ROWS // 2
FFN_HALF = 3 * FF_SH // 2
MIX_HALF = FF_SH // 2
PACK_A_ROWS = 3 * FF_SH
PACK_B_ROWS = 4 * FF_SH
BIG = ("ffn1_w_gate", "ffn1_w_up", "ffn1_w_down", "ffn2_w_gate", "ffn2_w_up", "ffn2_w_down", "w_in", "w_out")
BIG_ROWS = (FF_SH, FF_SH, FF_SH, FF_SH, FF_SH, FF_SH, IN_SH, OUT_SH)
BIG_TRANSPOSED = (True, True, False, True, True, False, True, False)

SMALL = (("ffn1_norm_g", 1024), ("mix_norm_g", 1024), ("b_in", 1792), ("attn_sinks", 8), ("gmlp_ln_g", 512),
         ("gmlp_ln_b", 512), ("gmlp_w_s", 131072), ("gmlp_b_s", 1024), ("attn_out_norm_g", 512),
         ("gmlp_out_norm_g", 512), ("b_out", 1024), ("ffn2_norm_g", 1024), ("final_norm_g", 1024), ("loss", 1))


def _small_rows(n):
    return -(-n // 1024) * 8


SMALL_USED_ROWS = sum(_small_rows(n) for _, n in SMALL)
SMALL_ROWS = -(-SMALL_USED_ROWS // 16) * 16


def _pack_small(parts):
    out = []
    for name, n in SMALL:
        flat = parts[name].reshape(-1).astype(F32)
        rows = _small_rows(n)
        out.append(jnp.pad(flat, (0, rows * 128 - n)).reshape(rows, 128))
    if SMALL_ROWS > SMALL_USED_ROWS:
        out.append(jnp.zeros((SMALL_ROWS - SMALL_USED_ROWS, 128), F32))
    return jnp.concatenate(out, axis=0)


def _unpack_small(packed, shapes):
    res, off = {}, 0
    for name, n in SMALL:
        rows = _small_rows(n)
        res[name] = packed[off:off + rows].reshape(-1)[:n].reshape(shapes[name])
        off += rows
    return res


def _ffn_tile(x, g, wg_ref, wu_ref, wd_ref, hb_ref, a_ref, b_ref):
    h, _ = _rms(x, g)
    hb = h.astype(BF16)
    hb_ref[...] = hb
    acc = jnp.zeros(x.shape, F32)
    for j in range(N_CHIPS):
        a = _dot_nt(hb, wg_ref[j])
        b = _dot_nt(hb, wu_ref[j])
        a_ref[j] = a
        b_ref[j] = b
        f = (a * jax.nn.sigmoid(a) * b).astype(BF16)
        acc = acc + _dot(f, wd_ref[j])
    return x + FFN_RES * acc


def _ffn_saved_specs(s, tile):
    ab = pl.BlockSpec((N_CHIPS, tile, FF_SH), lambda i: (0, i, 0))
    shape = jax.ShapeDtypeStruct((N_CHIPS, s, FF_SH), F32)
    return [_rows(tile, D_MODEL), ab, ab], [jax.ShapeDtypeStruct((s, D_MODEL), BF16), shape, shape]


def _ffn_weight_specs(k0):
    one = pl.Buffered(1)
    return [pl.BlockSpec((N_CHIPS, FF_SH, D_MODEL), functools.partial(lambda kk, i: (0, kk, 0), k0 + d),
                         pipeline_mode=one) for d in range(3)]


def _mesh_place():
    x, y, c = lax.axis_index("x"), lax.axis_index("y"), lax.axis_index("c")
    others = [(1 - x, y), (x, 1 - y), (1 - x, 1 - y)]
    return x, y, c, others


def _gather_stages(o_ref, send_sems, recv_sems):
    x, y, c, others = _mesh_place()
    me = 2 * x + y
    sibling = (x, y, 1 - c)
    half_rows = o_ref.shape[1] // 2

    def half(slab, core):
        return o_ref.at[slab, pl.ds(pl.multiple_of(core * half_rows, 16), half_rows)]

    def copy(k, rows, to):
        return pltpu.make_async_remote_copy(src_ref=rows, dst_ref=rows, send_sem=send_sems.at[k],
                                            recv_sem=recv_sems.at[k], device_id=to, device_id_type=MESH)

    first = [copy(j, half(me, c), (px, py, c)) for j, (px, py) in enumerate(others)]
    passed = [copy(3 + j, half(2 * px + py, c), sibling) for j, (px, py) in enumerate(others)]

    def start():
        for cp in first:
            cp.start()

    def forward():
        for j, (px, py) in enumerate(others):
            copy(j, half(2 * px + py, c), (px, py, c)).wait_recv()
            passed[j].start()

    def finish():
        for j, (px, py) in enumerate(others):
            copy(3 + j, half(2 * px + py, 1 - c), sibling).wait_recv()
        for cp in first + passed:
            cp.wait_send()

    return start, forward, finish


def _ffn_fwd(x, g, pack, k0, gather, *, tile, name):
    s = x.shape[0]
    nt = s // tile
    forward_at = max(nt - 6, 0)

    def body(x_ref, g_ref, wg_ref, wu_ref, wd_ref, gin_ref, o_ref, hb_ref, a_ref, b_ref, gat_ref, send_sems, recv_sems):
        i = pl.program_id(0)
        start, forward, finish = _gather_stages(gat_ref, send_sems, recv_sems)
        pl.when(i == 0)(start)
        o_ref[...] = _ffn_tile(x_ref[...], g_ref[...], wg_ref, wu_ref, wd_ref, hb_ref, a_ref, b_ref)
        pl.when(i == forward_at)(forward)
        pl.when(i == nt - 1)(finish)

    saved_specs, saved_shapes = _ffn_saved_specs(s, tile)
    hbm = pl.BlockSpec(memory_space=pl.ANY)
    return pl.pallas_call(
        body, name=name, grid=(nt,),
        in_specs=[_rows(tile, D_MODEL), _const((1, D_MODEL))] + _ffn_weight_specs(k0) + [hbm],
        out_specs=[_rows(tile, D_MODEL)] + saved_specs + [hbm],
        out_shape=[jax.ShapeDtypeStruct(x.shape, F32)] + saved_shapes
                  + [jax.ShapeDtypeStruct(gather.shape, gather.dtype)],
        input_output_aliases={5: 4},
        scratch_shapes=[pltpu.SemaphoreType.DMA((6,)), pltpu.SemaphoreType.DMA((6,))],
        compiler_params=_cparams(("arbitrary",)),
    )(x, g, pack, pack, pack, gather)


def _ffn_fwd_loss(x, g, pack, k0, gf, tgt, *, tile, name):
    s = x.shape[0]

    def body(x_ref, g_ref, wg_ref, wu_ref, wd_ref, gf_ref, t_ref, dx_ref, loss_ref, dgf_ref, hb_ref, a_ref, b_ref,
             do_ref):
        @pl.when(pl.program_id(0) == 0)
        def _():
            loss_ref[...] = jnp.zeros_like(loss_ref)
            dgf_ref[...] = jnp.zeros_like(dgf_ref)

        x3 = _ffn_tile(x_ref[...], g_ref[...], wg_ref, wu_ref, wd_ref, hb_ref, a_ref, b_ref)
        gf_v = gf_ref[...]
        out, r = _rms(x3, gf_v)
        diff = out - t_ref[...]
        part = jnp.sum(jnp.sum(diff * diff, axis=-1, keepdims=True), axis=0, keepdims=True)
        loss_ref[...] += jnp.broadcast_to(part * (0.5 / D_MODEL), loss_ref.shape)
        dx, dg = _rms_bwd(diff * (1.0 / D_MODEL), x3, r, gf_v)
        dx_ref[...] = dx
        do_ref[...] = (FFN_RES * dx).astype(BF16)
        dgf_ref[...] += dg

    saved_specs, saved_shapes = _ffn_saved_specs(s, tile)
    return pl.pallas_call(
        body, name=name, grid=(s // tile,),
        in_specs=[_rows(tile, D_MODEL), _const((1, D_MODEL))] + _ffn_weight_specs(k0)
                 + [_const((1, D_MODEL)), _rows(tile, D_MODEL)],
        out_specs=[_rows(tile, D_MODEL), _const((1, 128)), _const((1, D_MODEL))] + saved_specs
                  + [_rows(tile, D_MODEL)],
        out_shape=[jax.ShapeDtypeStruct(x.shape, F32),
                   jax.ShapeDtypeStruct((1, 128), F32),
                   jax.ShapeDtypeStruct((1, D_MODEL), F32)] + saved_shapes
                  + [jax.ShapeDtypeStruct(x.shape, BF16)],
        compiler_params=_cparams(("arbitrary",)),
    )(x, g, pack, pack, pack, gf, tgt)


def _ffn_bwd(place, hb, a, b, do, pack, region, land, mix_grads, *, tile, name):
    s = hb.shape[0]
    nt = s // tile
    land_rows = pl.ds(region * FFN_HALF, FFN_HALF)
    mix_rows = pl.ds(2 * FFN_HALF, MIX_HALF)
    with_mix = mix_grads is not None
    with_land = land is not None
    n_others = 2 * N_CHIPS - 1

    def body(place_ref, hb_ref, a_ref, b_ref, do_ref, wg_ref, wu_ref, wd_ref, *rest):
        rest = list(rest)
        mix_ref = rest.pop(0) if with_mix else None
        if with_land:
            rest.pop(0)
        dhp_ref, land_ref, acc_ref, stage_ref, send_sems, recv_sem, local_sem = rest[:7]
        t, i = pl.program_id(0), pl.program_id(1)
        xi, yi, c = lax.axis_index("x"), lax.axis_index("y"), lax.axis_index("c")
        dev = 4 * xi + 2 * yi + c
        tt = (t + 1) % N_CHIPS
        tx, ty = jnp.bitwise_xor(xi, tt // 2), jnp.bitwise_xor(yi, tt % 2)

        def remote(src, dst, ssem, rsem, to):
            return pltpu.make_async_remote_copy(src_ref=src, dst_ref=dst, send_sem=ssem, recv_sem=rsem,
                                                device_id=to, device_id_type=MESH)

        def stage_half(h):
            return stage_ref.at[pl.ds(pl.multiple_of(h * FFN_HALF, 16), FFN_HALF)]

        if with_mix:
            mix_send, mix_recv, mix_local = rest[7:10]

            @pl.when(jnp.logical_and(t == 0, i == 0))
            def _():
                for chip in range(N_CHIPS):
                    for h in range(2):
                        src = mix_ref.at[chip, pl.ds(h * MIX_HALF, MIX_HALF)]
                        dst = land_ref.at[dev, mix_rows]
                        mine = jnp.logical_and(2 * xi + yi == chip, c == h)

                        @pl.when(mine)
                        def _():
                            pltpu.make_async_copy(src, dst, mix_local).start()

                        @pl.when(jnp.logical_not(mine))
                        def _():
                            remote(src, dst, mix_send, mix_recv, (chip // 2, chip % 2, h)).start()

        @pl.when(i == 0)
        def _():
            acc_ref[...] = jnp.zeros_like(acc_ref)

        hb = hb_ref[...]
        dob = do_ref[...]
        wg_j, wu_j, wd_j = wg_ref[0], wu_ref[0], wd_ref[0]
        a = a_ref[0]
        b = b_ref[0]
        sg = jax.nn.sigmoid(a)
        sa = a * sg
        fb = (sa * b).astype(BF16)
        df = _dot_nt(dob, wd_j)
        dbb = (df * sa).astype(BF16)
        dab = (df * b * (sg + sa * (1.0 - sg))).astype(BF16)
        dhp_ref[0] = (_dot(dab, wg_j) + _dot(dbb, wu_j)).astype(BF16)
        acc_ref[0:FF_SH, :] += _dot_tn(dab, hb)
        acc_ref[FF_SH:2 * FF_SH, :] += _dot_tn(dbb, hb)
        acc_ref[2 * FF_SH:3 * FF_SH, :] += _dot_tn(fb, dob)

        @pl.when(i == nt - 1)
        def _():
            dst = land_ref.at[dev, land_rows]

            @pl.when(t > 0)
            def _():
                for h in range(2):
                    remote(stage_half(h), dst, send_sems.at[h], recv_sem, (tx, ty, h)).wait_send()

            def cast_rows(r, carry):
                rows = pl.ds(pl.multiple_of(r * MIX_HALF, 16), MIX_HALF)
                stage_ref[rows, :] = acc_ref[rows, :].astype(BF16)
                return carry

            lax.fori_loop(0, 3 * FF_SH // MIX_HALF, cast_rows, 0)

            @pl.when(t < N_CHIPS - 1)
            def _():
                for h in range(2):
                    remote(stage_half(h), dst, send_sems.at[h], recv_sem, (tx, ty, h)).start()

            @pl.when(t == N_CHIPS - 1)
            def _():
                own = pltpu.make_async_copy(stage_half(c), dst, local_sem)
                own.start()
                sib = remote(stage_half(1 - c), dst, send_sems.at[0], recv_sem, (xi, yi, 1 - c))
                sib.start()
                sib.wait_send()
                own.wait()
                arrivals = land_ref.at[pl.ds(0, n_others), land_rows]
                remote(arrivals, arrivals, send_sems.at[0], recv_sem, (xi, yi, 1 - c)).wait_recv()
                if with_mix:
                    seven = land_ref.at[pl.ds(0, n_others), mix_rows]
                    both = remote(seven, seven, mix_send, mix_recv, (xi, yi, 1 - c))
                    both.wait_send()
                    both.wait_recv()
                    pltpu.make_async_copy(mix_ref.at[0, pl.ds(0, MIX_HALF)], land_ref.at[dev, mix_rows],
                                          mix_local).wait()

    def wspec(kk):
        return pl.BlockSpec((1, FF_SH, D_MODEL),
                            lambda t, i, pr: (jnp.bitwise_xor(pr[0], (t + 1) % N_CHIPS), kk, 0))

    xspec = pl.BlockSpec((tile, D_MODEL), lambda t, i, pr: (i, 0))
    abspec = pl.BlockSpec((1, tile, FF_SH), lambda t, i, pr: (jnp.bitwise_xor(pr[0], (t + 1) % N_CHIPS), i, 0))
    hbm = pl.BlockSpec(memory_space=pl.ANY)
    in_specs = [xspec, abspec, abspec, xspec, wspec(0), wspec(1), wspec(2)]
    operands = [place, hb, a, b, do, pack, pack, pack]
    scratch = [pltpu.VMEM((3 * FF_SH, D_MODEL), F32), pltpu.VMEM((3 * FF_SH, D_MODEL), BF16),
               pltpu.SemaphoreType.DMA((2,)), pltpu.SemaphoreType.DMA, pltpu.SemaphoreType.DMA]
    if with_mix:
        in_specs.append(hbm)
        operands.append(mix_grads)
        scratch += [pltpu.SemaphoreType.DMA, pltpu.SemaphoreType.DMA, pltpu.SemaphoreType.DMA]
    aliases = {}
    if with_land:
        in_specs.append(hbm)
        operands.append(land)
        aliases = {len(operands) - 1: 1}
    grid_spec = pltpu.PrefetchScalarGridSpec(
        num_scalar_prefetch=1, grid=(N_CHIPS, nt), in_specs=in_specs,
        out_specs=[pl.BlockSpec((1, tile, D_MODEL), lambda t, i, pr: (t, i, 0)), hbm],
        scratch_shapes=scratch)
    return pl.pallas_call(
        body, name=name, grid_spec=grid_spec,
        out_shape=[jax.ShapeDtypeStruct((N_CHIPS, s, D_MODEL), BF16),
                   jax.ShapeDtypeStruct((2 * N_CHIPS, HALF_ROWS, D_MODEL), BF16)],
        input_output_aliases=aliases,
        compiler_params=_cparams(("arbitrary", "arbitrary")),
    )(*operands)


def _mix_grads_pack(dw_in_t, dw_out, *, name):
    def body(a_ref, b_ref, o_ref):
        o_ref[0, 0:IN_SH, :] = a_ref[0].astype(BF16)
        o_ref[0, IN_SH:FF_SH, :] = b_ref[0].astype(BF16)

    return pl.pallas_call(
        body, name=name, grid=(N_CHIPS,),
        in_specs=[pl.BlockSpec((1, IN_SH, D_MODEL), lambda j: (j, 0, 0)),
                  pl.BlockSpec((1, OUT_SH, D_MODEL), lambda j: (j, 0, 0))],
        out_specs=pl.BlockSpec((1, FF_SH, D_MODEL), lambda j: (j, 0, 0)),
        out_shape=jax.ShapeDtypeStruct((N_CHIPS, FF_SH, D_MODEL), BF16),
        compiler_params=_cparams(("arbitrary",)),
    )(dw_in_t.reshape(N_CHIPS, IN_SH, D_MODEL), dw_out.reshape(N_CHIPS, OUT_SH, D_MODEL))


def _norm_bwd(dhp, x, dy, g, *, tile, name):
    s = x.shape[0]

    def body(dhp_ref, x_ref, dy_ref, g_ref, dx_ref, dg_ref):
        @pl.when(pl.program_id(0) == 0)
        def _():
            dg_ref[...] = jnp.zeros_like(dg_ref)

        dh = ((dhp_ref[0].astype(F32) + dhp_ref[1].astype(F32))
              + (dhp_ref[2].astype(F32) + dhp_ref[3].astype(F32)))
        x_v = x_ref[...]
        r = lax.rsqrt(jnp.mean(x_v * x_v, axis=-1, keepdims=True) + EPS)
        dx, dg = _rms_bwd(dh, x_v, r, g_ref[...])
        dx_ref[...] = dy_ref[...] + dx
        dg_ref[...] += dg

    return pl.pallas_call(
        body, name=name, grid=(s // tile,),
        in_specs=[pl.BlockSpec((N_CHIPS, tile, D_MODEL), lambda i: (0, i, 0)),
                  _rows(tile, D_MODEL), _rows(tile, D_MODEL), _const((1, D_MODEL))],
        out_specs=[_rows(tile, D_MODEL), _const((1, D_MODEL))],
        out_shape=[jax.ShapeDtypeStruct(x.shape, F32), jax.ShapeDtypeStruct((1, D_MODEL), F32)],
        compiler_params=_cparams(("arbitrary",)),
    )(dhp, x, dy, g)


def _mix_in_fwd(x, g, w_in_t, b_in, *, tile, name):
    s = x.shape[0]

    def body(x_ref, g_ref, w_ref, b_ref, q_ref, k_ref, v_ref, z_ref):
        h, _ = _rms(x_ref[...], g_ref[...])
        proj = _dot_nt(h.astype(BF16), w_ref[...]) + b_ref[...]
        q_ref[...] = (proj[:, :ATTN_W] * ATTN_SCALE).astype(BF16)
        k_ref[...] = proj[:, ATTN_W:ATTN_W + KV_W].astype(BF16)
        v_ref[...] = proj[:, ATTN_W + KV_W:ATTN_W + 2 * KV_W].astype(BF16)
        z_ref[...] = proj[:, ATTN_W + 2 * KV_W:]

    return pl.pallas_call(
        body, name=name, grid=(s // tile,),
        in_specs=[_rows(tile, D_MODEL), _const((1, D_MODEL)), _const((IN_W, D_MODEL)), _const((1, IN_W))],
        out_specs=[_rows(tile, ATTN_W), _rows(tile, KV_W), _rows(tile, KV_W), _rows(tile, 2 * GMLP_W)],
        out_shape=[jax.ShapeDtypeStruct((s, ATTN_W), BF16), jax.ShapeDtypeStruct((s, KV_W), BF16),
                   jax.ShapeDtypeStruct((s, KV_W), BF16), jax.ShapeDtypeStruct((s, 2 * GMLP_W), F32)],
        compiler_params=_cparams(("arbitrary",)),
    )(x, g, w_in_t, b_in)


def _mix_in_bwd(x, dy, dq, dk, dv, dz, g, w_in_t, *, tile, name):
    s = x.shape[0]

    def body(x_ref, dy_ref, dq_ref, dk_ref, dv_ref, dz_ref, g_ref, w_ref, dx_ref, dw_ref, db_ref, dg_ref, do_ref):
        @pl.when(pl.program_id(0) == 0)
        def _():
            dw_ref[...] = jnp.zeros_like(dw_ref)
            db_ref[...] = jnp.zeros_like(db_ref)
            dg_ref[...] = jnp.zeros_like(dg_ref)

        dproj = jnp.concatenate([dq_ref[...], dk_ref[...], dv_ref[...], dz_ref[...]], axis=-1)
        db_ref[...] += jnp.sum(dproj, axis=0, keepdims=True)
        dpb = dproj.astype(BF16)
        x_v = x_ref[...]
        g_v = g_ref[...]
        h, r = _rms(x_v, g_v)
        dw_ref[...] += _dot_tn(dpb, h.astype(BF16))
        dh = _dot(dpb, w_ref[...])
        dxn, dg = _rms_bwd(dh, x_v, r, g_v)
        dx = dy_ref[...] + dxn
        dx_ref[...] = dx
        do_ref[...] = (FFN_RES * dx).astype(BF16)
        dg_ref[...] += dg

    return pl.pallas_call(
        body, name=name, grid=(s // tile,),
        in_specs=[_rows(tile, D_MODEL), _rows(tile, D_MODEL), _rows(tile, ATTN_W), _rows(tile, KV_W),
                  _rows(tile, KV_W), _rows(tile, 2 * GMLP_W), _const((1, D_MODEL)), _const((IN_W, D_MODEL))],
        out_specs=[_rows(tile, D_MODEL), _const((IN_W, D_MODEL)), _const((1, IN_W)), _const((1, D_MODEL)),
                   _rows(tile, D_MODEL)],
        out_shape=[jax.ShapeDtypeStruct(x.shape, F32), jax.ShapeDtypeStruct((IN_W, D_MODEL), F32),
                   jax.ShapeDtypeStruct((1, IN_W), F32), jax.ShapeDtypeStruct((1, D_MODEL), F32),
                   jax.ShapeDtypeStruct(x.shape, BF16)],
        compiler_params=_cparams(("arbitrary",)),
    )(x, dy, dq, dk, dv, dz, g, w_in_t)


_GELU_C = 0.7978845608028654
_GELU_A = 0.044715


def _gelu_tanh(x):
    x2 = x * x
    return jnp.tanh(_GELU_C * (x + _GELU_A * (x2 * x))), x2


def _band(ref, i):
    prev = jnp.maximum(i - 1, 0)
    return jnp.concatenate([ref[pl.ds(pl.multiple_of(prev * BLK, BLK), BLK), :],
                            ref[pl.ds(pl.multiple_of(i * BLK, BLK), BLK), :]], axis=0)


def _key_in_block():
    return lax.broadcasted_iota(jnp.int32, (BLK, BLK), 0) <= lax.broadcasted_iota(jnp.int32, (BLK, BLK), 1)


def _fold(band, own):
    return jnp.where(own, band[BLK:], band[:BLK])


def _unfold(a, own):
    zero = jnp.zeros_like(a)
    return jnp.concatenate([jnp.where(own, zero, a), jnp.where(own, a, zero)], axis=0).astype(BF16)


def _attn_fwd(q, kb, vb, i, sink_ref):
    own = _key_in_block()
    outs, saved = [], []
    for h in range(N_Q_HEADS):
        cols = slice((h // REP) * HEAD_DIM, (h // REP + 1) * HEAD_DIM)
        s2 = _dot_nt(kb[:, cols], q[:, h * HEAD_DIM:(h + 1) * HEAD_DIM])
        sc = jnp.where(own, s2[BLK:], jnp.where(i > 0, s2[:BLK], -jnp.inf))
        sink = sink_ref[h]
        m = jnp.maximum(jnp.max(sc, axis=0, keepdims=True), sink)
        p = jnp.exp(sc - m)
        es = jnp.exp(sink - m)
        inv = 1.0 / (jnp.sum(p, axis=0, keepdims=True) + es)
        pn = p * inv
        band = _unfold(pn, own)
        outs.append(_dot_tn(band, vb[:, cols]))
        saved.append((pn, band, es * inv))
    return jnp.concatenate(outs, axis=-1), saved


def _tril_mask():
    t = lax.broadcasted_iota(jnp.int32, (BLK, BLK), 0)
    s_ = lax.broadcasted_iota(jnp.int32, (BLK, BLK), 1)
    return s_ <= t


def _gmlp_fwd_parts(zg, lng, lnb, ws_ref, bs_full):
    th, zg2 = _gelu_tanh(zg)
    z = 0.5 * zg * (1.0 + th)
    u = z[:, :GMLP_W]
    zv = z[:, GMLP_W:]
    mu = jnp.mean(zv, axis=-1, keepdims=True)
    zc = zv - mu
    rstd = lax.rsqrt(jnp.mean(zc * zc, axis=-1, keepdims=True) + EPS)
    xh = zc * rstd
    vvb = (xh * lng + lnb).astype(BF16)
    tril = _tril_mask()
    wms, parts = [], []
    for gi in range(GMLP_GROUPS):
        wm = jnp.where(tril, ws_ref[gi], 0.0).astype(BF16)
        wms.append(wm)
        parts.append(_dot(wm, vvb[:, gi * GROUP_DIM:(gi + 1) * GROUP_DIM]))
    mixed = jnp.concatenate(parts, axis=-1) + bs_full
    gelu_grad = 0.5 * (1.0 + th) + 0.5 * zg * (1.0 - th * th) * (_GELU_C * (1.0 + 3.0 * _GELU_A * zg2))
    return u, xh, rstd, vvb, wms, mixed, gelu_grad


def _mix_core_fwd(q, k, v, zg, sinks, lng, lnb, w_s, bs_full, gao, ggo, *, name):
    s = q.shape[0]
    nb = min(MIX_FWD_BLOCKS, s // BLK)

    def body(sink_ref, q_ref, k_ref, v_ref, z_ref, lng_ref, lnb_ref, ws_ref, bs_ref, gao_ref, ggo_ref, o_ref):
        for b in range(nb):
            blk = pl.program_id(0) * nb + b
            rows = slice(b * BLK, (b + 1) * BLK)
            y_attn, _ = _attn_fwd(q_ref[rows, :], _band(k_ref, blk), _band(v_ref, blk), blk, sink_ref)
            u, _, _, _, _, mixed, _ = _gmlp_fwd_parts(z_ref[rows, :], lng_ref[...], lnb_ref[...], ws_ref,
                                                      bs_ref[...])
            ya, _ = _rms(y_attn, gao_ref[...])
            yg, _ = _rms(u * mixed, ggo_ref[...])
            o_ref[rows, :] = jnp.concatenate([ya, yg], axis=-1).astype(BF16)

    return pl.pallas_call(
        body, name=name, grid=(s // (nb * BLK),),
        in_specs=[pl.BlockSpec(memory_space=pltpu.SMEM),
                  _rows(nb * BLK, ATTN_W), _const((s, KV_W)), _const((s, KV_W)),
                  _rows(nb * BLK, 2 * GMLP_W), _const((1, GMLP_W)), _const((1, GMLP_W)),
                  _const((GMLP_GROUPS, BLK, BLK)), _const((BLK, GMLP_W)), _const((1, ATTN_W)), _const((1, GMLP_W))],
        out_specs=_rows(nb * BLK, D_MODEL),
        out_shape=jax.ShapeDtypeStruct((s, D_MODEL), BF16),
        compiler_params=_cparams(("arbitrary",)),
    )(sinks, q, k, v, zg, lng, lnb, w_s, bs_full, gao, ggo)


def _mix_out_fwd(x1, yb, w_out, b_out, *, tile, name):
    s = x1.shape[0]

    def body(x_ref, y_ref, w_ref, b_ref, o_ref):
        o_ref[...] = x_ref[...] + (_dot(y_ref[...], w_ref[...]) + b_ref[...])

    return pl.pallas_call(
        body, name=name, grid=(s // tile,),
        in_specs=[_rows(tile, D_MODEL), _rows(tile, D_MODEL), _const((D_MODEL, D_MODEL)), _const((1, D_MODEL))],
        out_specs=_rows(tile, D_MODEL),
        out_shape=jax.ShapeDtypeStruct(x1.shape, F32),
        compiler_params=_cparams(("arbitrary",)),
    )(x1, yb, w_out, b_out)


def _norm_bwd_mix_out(dhp, x, dy, g, yb, w_out, *, tile, name):
    s = x.shape[0]

    def body(dhp_ref, x_ref, dy_ref, g_ref, y_ref, w_ref, dx_ref, dg_ref, dyy_ref, dw_ref, db_ref):
        @pl.when(pl.program_id(0) == 0)
        def _():
            dg_ref[...] = jnp.zeros_like(dg_ref)
            dw_ref[...] = jnp.zeros_like(dw_ref)
            db_ref[...] = jnp.zeros_like(db_ref)

        dh = ((dhp_ref[0].astype(F32) + dhp_ref[1].astype(F32))
              + (dhp_ref[2].astype(F32) + dhp_ref[3].astype(F32)))
        x_v = x_ref[...]
        r = lax.rsqrt(jnp.mean(x_v * x_v, axis=-1, keepdims=True) + EPS)
        dxn, dg = _rms_bwd(dh, x_v, r, g_ref[...])
        dx = dy_ref[...] + dxn
        dx_ref[...] = dx
        dg_ref[...] += dg
        dxb = dx.astype(BF16)
        db_ref[...] += jnp.sum(dx, axis=0, keepdims=True)
        dw_ref[...] += _dot_tn(y_ref[...], dxb)
        dyy_ref[...] = _dot_nt(dxb, w_ref[...])

    return pl.pallas_call(
        body, name=name, grid=(s // tile,),
        in_specs=[pl.BlockSpec((N_CHIPS, tile, D_MODEL), lambda i: (0, i, 0)),
                  _rows(tile, D_MODEL), _rows(tile, D_MODEL), _const((1, D_MODEL)), _rows(tile, D_MODEL),
                  _const((D_MODEL, D_MODEL))],
        out_specs=[_rows(tile, D_MODEL), _const((1, D_MODEL)), _rows(tile, D_MODEL), _const((D_MODEL, D_MODEL)),
                   _const((1, D_MODEL))],
        out_shape=[jax.ShapeDtypeStruct(x.shape, F32), jax.ShapeDtypeStruct((1, D_MODEL), F32),
                   jax.ShapeDtypeStruct(x.shape, F32), jax.ShapeDtypeStruct((D_MODEL, D_MODEL), F32),
                   jax.ShapeDtypeStruct((1, D_MODEL), F32)],
        compiler_params=_cparams(("arbitrary",)),
    )(dhp, x, dy, g, yb, w_out)


def _mix_core_bwd(dyy, q, k, v, zg, sinks, lng, lnb, w_s, bs_full, gao, ggo, *, name):
    s = dyy.shape[0]
    nb = min(MIX_BWD_BLOCKS, s // BLK)
    nsteps = s // (nb * BLK)

    def body(*refs):
        accumulators = refs[13:15] + refs[16:]

        @pl.when(pl.program_id(0) == 0)
        def _():
            for ref in accumulators:
                ref[...] = jnp.zeros_like(ref)

        for b in range(nb):
            one_block(pl.program_id(0) * nb + b, slice(b * BLK, (b + 1) * BLK), *refs)

        @pl.when(pl.program_id(0) == nsteps - 1)
        def _():
            tril = _tril_mask()
            for gi in range(GMLP_GROUPS):
                refs[20][gi] = jnp.where(tril, refs[20][gi], 0.0)

    def one_block(i, rows, sink_ref, dyy_ref, q_ref, k_ref, v_ref, z_ref, lng_ref, lnb_ref, ws_ref, bs_ref, gao_ref,
                  ggo_ref, dq_ref, dk_ref, dv_ref, dz_ref, dgao_ref, dggo_ref, dlng_ref, dlnb_ref, dws_ref, dms_ref,
                  dsk_ref):
        q_v = q_ref[rows, :]
        kb = _band(k_ref, i)
        vb = _band(v_ref, i)
        lng_v = lng_ref[...]
        gao_v = gao_ref[...]
        ggo_v = ggo_ref[...]

        y_attn, probs = _attn_fwd(q_v, kb, vb, i, sink_ref)
        u, xh, rstd, vvb, wms, mixed, gelu_grad = _gmlp_fwd_parts(z_ref[rows, :], lng_v, lnb_ref[...], ws_ref,
                                                                  bs_ref[...])
        y_gmlp = u * mixed
        ra = lax.rsqrt(jnp.mean(y_attn * y_attn, axis=-1, keepdims=True) + EPS)
        rg = lax.rsqrt(jnp.mean(y_gmlp * y_gmlp, axis=-1, keepdims=True) + EPS)

        dyy = dyy_ref[rows, :]
        d_attn, dgao = _rms_bwd(dyy[:, :ATTN_W], y_attn, ra, gao_v)
        d_gmlp, dggo = _rms_bwd(dyy[:, ATTN_W:], y_gmlp, rg, ggo_v)
        dgao_ref[...] += dgao
        dggo_ref[...] += dggo

        du = d_gmlp * mixed
        dmixed = d_gmlp * u
        dms_ref[...] += dmixed
        dmb = dmixed.astype(BF16)
        dvv_parts = []
        for gi in range(GMLP_GROUPS):
            sl = slice(gi * GROUP_DIM, (gi + 1) * GROUP_DIM)
            dws_ref[gi] += _dot_nt(dmb[:, sl], vvb[:, sl])
            dvv_parts.append(_dot_tn(wms[gi], dmb[:, sl]))
        dvv = jnp.concatenate(dvv_parts, axis=-1)
        dlng_ref[...] += jnp.sum(dvv * xh, axis=0, keepdims=True)
        dlnb_ref[...] += jnp.sum(dvv, axis=0, keepdims=True)
        dxh = dvv * lng_v
        dzv = rstd * (dxh - jnp.mean(dxh, axis=-1, keepdims=True)
                      - xh * jnp.mean(dxh * xh, axis=-1, keepdims=True))
        dz_ref[rows, :] = jnp.concatenate([du, dzv], axis=-1) * gelu_grad

        dab = d_attn.astype(BF16)
        own = _key_in_block()
        dq_parts = []
        dk_parts = []
        dv_parts = []
        for gi in range(N_KV_HEADS):
            cols = slice(gi * HEAD_DIM, (gi + 1) * HEAD_DIM)
            kg, vg = kb[:, cols], vb[:, cols]
            dkg = jnp.zeros((2 * BLK, HEAD_DIM), F32)
            dvg = jnp.zeros((2 * BLK, HEAD_DIM), F32)
            for rr in range(REP):
                h = gi * REP + rr
                hs = slice(h * HEAD_DIM, (h + 1) * HEAD_DIM)
                qh, doh = q_v[:, hs], dab[:, hs]
                pn, band, psink = probs[h]
                dp = _fold(_dot_nt(vg, doh), own)
                delta = jnp.sum(pn * dp, axis=0, keepdims=True)
                ds2 = _unfold(pn * (dp - delta), own)
                dsink = jnp.sum(-psink * delta, axis=-1, keepdims=True)
                dsk_ref[pl.ds(h, 1), :] += jnp.broadcast_to(dsink, (1, 128))
                dq_parts.append(_dot_tn(ds2, kg) * ATTN_SCALE)
                dkg = dkg + _dot(ds2, qh)
                dvg = dvg + _dot(band, doh)
            dk_parts.append(dkg)
            dv_parts.append(dvg)
        dq_ref[rows, :] = jnp.concatenate(dq_parts, axis=-1)
        dkb = jnp.concatenate(dk_parts, axis=-1)
        dvb = jnp.concatenate(dv_parts, axis=-1)
        prev = pl.ds(pl.multiple_of(jnp.maximum(i - 1, 0) * BLK, BLK), BLK)
        cur = pl.ds(pl.multiple_of(i * BLK, BLK), BLK)
        dk_ref[prev, :] += dkb[:BLK]
        dv_ref[prev, :] += dvb[:BLK]
        dk_ref[cur, :] += dkb[BLK:]
        dv_ref[cur, :] += dvb[BLK:]

    return pl.pallas_call(
        body, name=name, grid=(nsteps,),
        in_specs=[pl.BlockSpec(memory_space=pltpu.SMEM),
                  _rows(nb * BLK, D_MODEL), _rows(nb * BLK, ATTN_W), _const((s, KV_W)), _const((s, KV_W)),
                  _rows(nb * BLK, 2 * GMLP_W), _const((1, GMLP_W)), _const((1, GMLP_W)),
                  _const((GMLP_GROUPS, BLK, BLK)), _const((BLK, GMLP_W)), _const((1, ATTN_W)), _const((1, GMLP_W))],
        out_specs=[_rows(nb * BLK, ATTN_W), _const((s, KV_W)), _const((s, KV_W)), _rows(nb * BLK, 2 * GMLP_W),
                   _const((1, ATTN_W)), _const((1, GMLP_W)),
                   _const((1, GMLP_W)), _const((1, GMLP_W)), _const((GMLP_GROUPS, BLK, BLK)),
                   _const((BLK, GMLP_W)), _const((N_Q_HEADS, 128))],
        out_shape=[jax.ShapeDtypeStruct((s, ATTN_W), F32), jax.ShapeDtypeStruct((s, KV_W), F32),
                   jax.ShapeDtypeStruct((s, KV_W), F32), jax.ShapeDtypeStruct((s, 2 * GMLP_W), F32),
                   jax.ShapeDtypeStruct((1, ATTN_W), F32), jax.ShapeDtypeStruct((1, GMLP_W), F32),
                   jax.ShapeDtypeStruct((1, GMLP_W), F32), jax.ShapeDtypeStruct((1, GMLP_W), F32),
                   jax.ShapeDtypeStruct((GMLP_GROUPS, BLK, BLK), F32), jax.ShapeDtypeStruct((BLK, GMLP_W), F32),
                   jax.ShapeDtypeStruct((N_Q_HEADS, 128), F32)],
        compiler_params=_cparams(("arbitrary",)),
    )(sinks, dyy, q, k, v, zg, lng, lnb, w_s, bs_full, gao, ggo)


def _local_step(place, x, tgt, p, pack_a, pack_b, *, tile=512, fwd_tile=256, bwd_tile=512, norm_tile=512):
    g = {}
    tile, fwd_tile, bwd_tile, norm_tile = (min(t_, x.shape[0]) for t_ in (tile, fwd_tile, bwd_tile, norm_tile))
    x1, hb1, a1, b1, pack_b = _ffn_fwd(x, p["ffn1_norm_g"], pack_a, 0, pack_b, tile=fwd_tile, name="ffn1_fwd")
    mix_rows = pack_b[:, 3 * FF_SH:, :]
    w_in_t = mix_rows[:, :IN_SH, :].reshape(IN_W, D_MODEL)
    w_out = mix_rows[:, IN_SH:, :].reshape(D_MODEL, D_MODEL)
    q, k, v, zg = _mix_in_fwd(x1, p["mix_norm_g"], w_in_t, p["b_in"], tile=tile, name="mix_in_fwd")
    mix_args = (q, k, v, zg, p["attn_sinks"], p["gmlp_ln_g"], p["gmlp_ln_b"], p["gmlp_w_s"], p["bs_full"],
                p["attn_out_norm_g"], p["gmlp_out_norm_g"])
    yb = _mix_core_fwd(*mix_args, name="mix_core_fwd")
    x2 = _mix_out_fwd(x1, yb, w_out, p["b_out"], tile=tile, name="mix_out_fwd")
    dx3, loss, g["final_norm_g"], hb2, a2, b2, do3 = _ffn_fwd_loss(
        x2, p["ffn2_norm_g"], pack_b, 0, p["final_norm_g"], tgt, tile=fwd_tile, name="ffn2_fwd_loss")

    dhp, land = _ffn_bwd(place, hb2, a2, b2, do3, pack_b, 1, None, None, tile=bwd_tile, name="ffn2_bwd")
    dx2, g["ffn2_norm_g"], dyy, dw_out, g["b_out"] = _norm_bwd_mix_out(
        dhp, x2, dx3, p["ffn2_norm_g"], yb, w_out, tile=norm_tile, name="ffn2_norm_bwd")

    (dq, dk, dv, dz, g["attn_out_norm_g"], g["gmlp_out_norm_g"], g["gmlp_ln_g"],
     g["gmlp_ln_b"], g["gmlp_w_s"], dmix_sum, dsinks) = _mix_core_bwd(dyy, *mix_args, name="mix_core_bwd")
    g["gmlp_b_s"] = dmix_sum
    g["attn_sinks"] = dsinks
    dx1, dw_in_t, g["b_in"], g["mix_norm_g"], do1 = _mix_in_bwd(
        x1, dx2, dq, dk, dv, dz, p["mix_norm_g"], w_in_t, tile=tile, name="mix_in_bwd")
    mix_grads = _mix_grads_pack(dw_in_t, dw_out, name="mix_grads_pack")

    dhp1, land = _ffn_bwd(place, hb1, a1, b1, do1, pack_a, 0, land, mix_grads, tile=bwd_tile, name="ffn1_bwd")
    dx0, g["ffn1_norm_g"] = _norm_bwd(dhp1, x, dx1, p["ffn1_norm_g"], tile=norm_tile, name="ffn1_norm_bwd")
    return loss, dx0, land, g


def _pack_cast(place, parts, *, name):
    def body(place_ref, *refs):
        oa_ref, ob_ref = refs[-2], refs[-1]
        off = 0
        for k, (ref, rows) in enumerate(zip(refs[:-2], BIG_ROWS)):
            if k == 3:
                off = 0
            (oa_ref if k < 3 else ob_ref)[0, off:off + rows, :] = ref[...].astype(BF16)
            off += rows

    one = pl.Buffered(1)

    def slab(rows):
        return pl.BlockSpec((1, rows, D_MODEL), lambda i, pr: (pr[0], 0, 0), pipeline_mode=one)

    grid_spec = pltpu.PrefetchScalarGridSpec(
        num_scalar_prefetch=1, grid=(1,),
        in_specs=[pl.BlockSpec((rows, D_MODEL), lambda i, pr: (0, 0), pipeline_mode=one) for rows in BIG_ROWS],
        out_specs=[slab(PACK_A_ROWS), slab(PACK_B_ROWS)])
    return pl.pallas_call(
        body, name=name, grid_spec=grid_spec,
        out_shape=[jax.ShapeDtypeStruct((N_CHIPS, PACK_A_ROWS, D_MODEL), BF16),
                   jax.ShapeDtypeStruct((N_CHIPS, PACK_B_ROWS, D_MODEL), BF16)],
        compiler_params=_cparams(("arbitrary",)),
    )(place, *parts)


def _all_gather_pack(pack, *, name):
    def body(p_ref, o_ref, send_sems, recv_sems):
        start, forward, finish = _gather_stages(o_ref, send_sems, recv_sems)
        start()
        forward()
        finish()

    return pl.pallas_call(
        body, name=name,
        in_specs=[pl.BlockSpec(memory_space=pl.ANY)],
        out_specs=pl.BlockSpec(memory_space=pl.ANY),
        out_shape=jax.ShapeDtypeStruct(pack.shape, pack.dtype),
        input_output_aliases={0: 0},
        scratch_shapes=[pltpu.SemaphoreType.DMA((6,)), pltpu.SemaphoreType.DMA((6,))],
    )(pack)


def _shard_tile(i, c):
    return jnp.where(i < 3, 3 * c + i, jnp.where(i < 6, 3 + 3 * c + i, 12 + c))


def _rs_reduce(place, land, *, name):
    def body(place_ref, l_ref, o_ref):
        acc = l_ref[0].astype(F32)
        for d in range(1, 2 * N_CHIPS):
            acc = acc + l_ref[d].astype(F32)
        o_ref[...] = acc

    grid_spec = pltpu.PrefetchScalarGridSpec(
        num_scalar_prefetch=1, grid=(HALF_ROWS // MIX_HALF,),
        in_specs=[pl.BlockSpec((2 * N_CHIPS, MIX_HALF, D_MODEL), lambda i, pr: (0, i, 0))],
        out_specs=pl.BlockSpec((MIX_HALF, D_MODEL), lambda i, pr: (_shard_tile(i, pr[1]), 0)))
    return pl.pallas_call(
        body, name=name, grid_spec=grid_spec,
        out_shape=jax.ShapeDtypeStruct((PACK_ROWS, D_MODEL), F32),
        compiler_params=_cparams(("arbitrary",)),
    )(place, land)


def _rs_share(shard, *, name):
    def body(s_ref, o_ref, send_sems, recv_sems):
        x, y, c, _ = _mesh_place()

        def rows(k, core):
            if k < 2:
                return o_ref.at[pl.ds(pl.multiple_of(k * 2 * FFN_HALF + core * FFN_HALF, 8), FFN_HALF)]
            return o_ref.at[pl.ds(pl.multiple_of(4 * FFN_HALF + core * MIX_HALF, 8), MIX_HALF)]

        def copy(k, core):
            return pltpu.make_async_remote_copy(src_ref=rows(k, core), dst_ref=rows(k, core), send_sem=send_sems.at[k],
                                                recv_sem=recv_sems.at[k], device_id=(x, y, 1 - c),
                                                device_id_type=MESH)

        sends = [copy(k, c) for k in range(3)]
        for cp in sends:
            cp.start()
        for k in range(3):
            copy(k, 1 - c).wait_recv()
        for cp in sends:
            cp.wait_send()

    return pl.pallas_call(
        body, name=name,
        in_specs=[pl.BlockSpec(memory_space=pl.ANY)],
        out_specs=pl.BlockSpec(memory_space=pl.ANY),
        out_shape=jax.ShapeDtypeStruct((PACK_ROWS, D_MODEL), F32),
        input_output_aliases={0: 0},
        scratch_shapes=[pltpu.SemaphoreType.DMA((3,)), pltpu.SemaphoreType.DMA((3,))],
    )(shard)


def _small_all_reduce(packed, *, name):
    rows = packed.shape[0]
    half = rows // 2

    def body(p_ref, o_ref, sib_ref, slots_ref, send_sems, recv_sems):
        x, y, c, others = _mesh_place()
        me = 2 * x + y
        sibling = (x, y, 1 - c)

        def half_of(core):
            return pl.ds(pl.multiple_of(core * half, 8), half)

        def remote(k, src, dst, to):
            return pltpu.make_async_remote_copy(src_ref=src, dst_ref=dst, send_sem=send_sems.at[k],
                                                recv_sem=recv_sems.at[k], device_id=to, device_id_type=MESH)

        sib = remote(0, p_ref.at[half_of(1 - c)], sib_ref, sibling)
        sib.start()
        sib.wait()
        slots_ref[me] = p_ref[half_of(c), :] + sib_ref[...]
        sends = [remote(1 + j, slots_ref.at[me], slots_ref.at[me], (px, py, c)) for j, (px, py) in enumerate(others)]
        for cp in sends:
            cp.start()
        for j, (px, py) in enumerate(others):
            slab = slots_ref.at[2 * px + py]
            remote(1 + j, slab, slab, (px, py, c)).wait_recv()
        for cp in sends:
            cp.wait_send()
        o_ref[half_of(c), :] = (slots_ref[0] + slots_ref[1]) + (slots_ref[2] + slots_ref[3])
        back = remote(4, o_ref.at[half_of(c)], o_ref.at[half_of(c)], sibling)
        back.start()
        remote(4, o_ref.at[half_of(1 - c)], o_ref.at[half_of(1 - c)], sibling).wait_recv()
        back.wait_send()

    vm = pl.BlockSpec(memory_space=pltpu.VMEM)
    return pl.pallas_call(
        body, name=name, in_specs=[vm], out_specs=vm,
        out_shape=jax.ShapeDtypeStruct((rows, 128), F32),
        scratch_shapes=[pltpu.VMEM((half, 128), F32), pltpu.VMEM((N_CHIPS, half, 128), F32),
                        pltpu.SemaphoreType.DMA((5,)), pltpu.SemaphoreType.DMA((5,))],
    )(packed)


def _adamw(w, g, m, v, *, g_row0, tile, name):
    rows, cols = w.shape
    assert g_row0 % tile == 0 and rows % tile == 0

    def body(w_ref, g_ref, m_ref, v_ref, go_ref, d_ref, nm_ref, nv_ref):
        g_v = g_ref[...]
        m_n = ADAM_B1 * m_ref[...] + (1.0 - ADAM_B1) * g_v
        v_n = ADAM_B2 * v_ref[...] + (1.0 - ADAM_B2) * (g_v * g_v)
        m_hat = m_n / (1.0 - ADAM_B1 ** ADAM_STEP)
        v_hat = v_n / (1.0 - ADAM_B2 ** ADAM_STEP)
        d_ref[...] = -ADAM_LR * (m_hat / (jnp.sqrt(v_hat) + ADAM_EPS) + ADAM_WD * w_ref[...])
        go_ref[...] = g_v
        nm_ref[...] = m_n
        nv_ref[...] = v_n

    spec = pl.BlockSpec((tile, cols), lambda i: (i, 0))
    gspec = pl.BlockSpec((tile, cols), lambda i: (g_row0 // tile + i, 0))
    shape = jax.ShapeDtypeStruct((rows, cols), F32)
    return pl.pallas_call(
        body, name=name, grid=(rows // tile,),
        in_specs=[spec, gspec, spec, spec], out_specs=[spec] * 4, out_shape=[shape] * 4,
        compiler_params=_cparams(("arbitrary",)),
    )(w, g, m, v)


def kernel(x, ffn1_norm_g, ffn1_w_gate, ffn1_w_up, ffn1_w_down, mix_norm_g, w_in, b_in, attn_sinks, gmlp_ln_g, gmlp_ln_b, gmlp_w_s, gmlp_b_s, attn_out_norm_g, gmlp_out_norm_g, w_out, b_out, ffn2_norm_g, ffn2_w_gate, ffn2_w_up, ffn2_w_down, final_norm_g, loss_target, m_ffn1_norm_g, m_ffn1_w_gate, m_ffn1_w_up, m_ffn1_w_down, m_mix_norm_g, m_w_in, m_b_in, m_attn_sinks, m_gmlp_ln_g, m_gmlp_ln_b, m_gmlp_w_s, m_gmlp_b_s, m_attn_out_norm_g, m_gmlp_out_norm_g, m_w_out, m_b_out, m_ffn2_norm_g, m_ffn2_w_gate, m_ffn2_w_up, m_ffn2_w_down, m_final_norm_g, v_ffn1_norm_g, v_ffn1_w_gate, v_ffn1_w_up, v_ffn1_w_down, v_mix_norm_g, v_w_in, v_b_in, v_attn_sinks, v_gmlp_ln_g, v_gmlp_ln_b, v_gmlp_w_s, v_gmlp_b_s, v_attn_out_norm_g, v_gmlp_out_norm_g, v_w_out, v_b_out, v_ffn2_norm_g, v_ffn2_w_gate, v_ffn2_w_up, v_ffn2_w_down, v_final_norm_g):
    f_args = dict(locals())
    weights = {n: f_args[n] for n in [nm for nm, _ in SMALL if nm != "loss"] + list(BIG)}
    shapes = {n: weights[n].shape for n in weights}
    shapes["loss"] = ()
    place = jnp.stack([2 * lax.axis_index("x") + lax.axis_index("y"), lax.axis_index("c")]).astype(jnp.int32)

    def with_cols(name, a):
        a2 = a.reshape(a.shape[-2], a.shape[-1])
        return a2.T if BIG_TRANSPOSED[BIG.index(name)] else a2

    def natural(name, a2):
        return (a2.T if BIG_TRANSPOSED[BIG.index(name)] else a2).reshape(shapes[name])

    pack_a, pack_b = _pack_cast(place, [with_cols(n, weights[n]) for n in BIG], name="pack_cast")
    pack_a = _all_gather_pack(pack_a, name="ag_weights")
    p = {n: weights[n].reshape(1, -1) for n in ("ffn1_norm_g", "mix_norm_g", "b_in", "gmlp_ln_g", "gmlp_ln_b",
                                                "attn_out_norm_g", "gmlp_out_norm_g", "b_out", "ffn2_norm_g",
                                                "final_norm_g")}
    p["attn_sinks"] = attn_sinks.reshape(N_Q_HEADS)
    p["gmlp_w_s"] = gmlp_w_s.reshape(GMLP_GROUPS, BLK, BLK)
    p["bs_full"] = jnp.broadcast_to(gmlp_b_s.reshape(GMLP_GROUPS, BLK).T[:, :, None],
                                    (BLK, GMLP_GROUPS, GROUP_DIM)).reshape(BLK, GMLP_W)

    loss_part, dx0, land, gs = _local_step(place, x[0], loss_target[0], p, pack_a, pack_b)

    shard = _rs_share(_rs_reduce(place, land, name="rs_reduce"), name="rs_share")
    gs["gmlp_b_s"] = jnp.sum(gs["gmlp_b_s"].reshape(BLK, GMLP_GROUPS, GROUP_DIM), axis=-1).T
    gs["attn_sinks"] = gs["attn_sinks"][:, 0]
    gs["loss"] = loss_part[0, 0]
    small_sum = _small_all_reduce(_pack_small(gs), name="small_all_reduce")

    grad_w, delta, new_m, new_v = {}, {}, {}, {}
    off = 0
    for n, rows in zip(BIG, BIG_ROWS):
        res = _adamw(with_cols(n, weights[n]), shard, with_cols(n, f_args["m_" + n]), with_cols(n, f_args["v_" + n]),
                     g_row0=off, tile=FF_SH // 2 if rows == FF_SH else 64, name="adamw_" + n)
        grad_w[n], delta[n], new_m[n], new_v[n] = [natural(n, a) for a in res]
        off += rows
    sm = {k: {n: f_args[k + n] for n, _ in SMALL if n != "loss"} for k in ("", "m_", "v_")}
    for k in sm:
        sm[k]["loss"] = jnp.zeros((), F32)
    res = _adamw(_pack_small(sm[""]), small_sum, _pack_small(sm["m_"]), _pack_small(sm["v_"]),
                 g_row0=0, tile=SMALL_ROWS, name="adamw_small")
    small = _unpack_small(res[0], shapes)
    for dst, packed in ((grad_w, res[0]), (delta, res[1]), (new_m, res[2]), (new_v, res[3])):
        dst.update({n: a for n, a in _unpack_small(packed, shapes).items() if n != "loss"})

    order = ('ffn1_norm_g', 'ffn1_w_gate', 'ffn1_w_up', 'ffn1_w_down', 'mix_norm_g', 'w_in', 'b_in', 'attn_sinks',
             'gmlp_ln_g', 'gmlp_ln_b', 'gmlp_w_s', 'gmlp_b_s', 'attn_out_norm_g', 'gmlp_out_norm_g', 'w_out', 'b_out',
             'ffn2_norm_g', 'ffn2_w_gate', 'ffn2_w_up', 'ffn2_w_down', 'final_norm_g')
    return (small["loss"], dx0.reshape(x.shape), *[grad_w[n] for n in order], *[delta[n] for n in order],
            *[new_m[n] for n in order], *[new_v[n] for n in order])
```

```python
import functools

import jax
import jax.numpy as jnp
from jax import lax
from jax.experimental import pallas as pl
from jax.experimental.pallas import tpu as pltpu

F32 = jnp.float32
BF16 = jnp.bfloat16

D_MODEL = 1024
D_FF = 2816
N_CHIPS = 4
FF_SH = D_FF // N_CHIPS
N_Q_HEADS = 8
N_KV_HEADS = 2
REP = N_Q_HEADS // N_KV_HEADS
HEAD_DIM = 64
ATTN_W = 512
KV_W = 128
GMLP_W = 512
GMLP_GROUPS = 8
GROUP_DIM = 64
BLK = 128
MIX_FWD_BLOCKS = 2
MIX_BWD_BLOCKS = 4
IN_W = 1792
IN_SH = IN_W // N_CHIPS
OUT_SH = D_MODEL // N_CHIPS
EPS = 1e-6
FFN_RES = 0.5
ATTN_SCALE = HEAD_DIM ** -0.5

ADAM_LR = 0.001
ADAM_B1 = 0.9
ADAM_B2 = 0.999
ADAM_EPS = 1e-08
ADAM_WD = 0.01
ADAM_STEP = 10

V7X_VMEM_LIMIT = 56 * 1024 * 1024
MESH = pl.DeviceIdType.MESH


def _cparams(sem):
    return pltpu.CompilerParams(dimension_semantics=sem, vmem_limit_bytes=V7X_VMEM_LIMIT)


def _dot(a, b):
    return jnp.dot(a, b, preferred_element_type=F32)


def _dot_nt(a, b):
    return lax.dot_general(a, b, (((1,), (1,)), ((), ())), preferred_element_type=F32)


def _dot_tn(a, b):
    return lax.dot_general(a, b, (((0,), (0,)), ((), ())), preferred_element_type=F32)


def _rms(x, g):
    r = lax.rsqrt(jnp.mean(x * x, axis=-1, keepdims=True) + EPS)
    return x * r * g, r


def _rms_bwd(dh, x, r, g):
    gy = dh * g
    dx = r * gy - x * (r * r * r) * jnp.mean(gy * x, axis=-1, keepdims=True)
    dg = jnp.sum(dh * x * r, axis=0, keepdims=True)
    return dx, dg


def _const(shape):
    nd = len(shape)
    return pl.BlockSpec(shape, lambda *_: (0,) * nd)


def _rows(t, w):
    return pl.BlockSpec((t, w), lambda i: (i, 0))


PACK_ROWS = 7 * FF_SH
HALF_ROWS = PACK_ROWS // 2
FFN_HALF = 3 * FF_SH // 2
MIX_HALF = FF_SH // 2
PACK_A_ROWS = 3 * FF_SH
PACK_B_ROWS = 4 * FF_SH
BIG = ("ffn1_w_gate", "ffn1_w_up", "ffn1_w_down", "ffn2_w_gate", "ffn2_w_up", "ffn2_w_down", "w_in", "w_out")
BIG_ROWS = (FF_SH, FF_SH, FF_SH, FF_SH, FF_SH, FF_SH, IN_SH, OUT_SH)
BIG_TRANSPOSED = (True, True, False, True, True, False, True, False)

SMALL = (("ffn1_norm_g", 1024), ("mix_norm_g", 1024), ("b_in", 1792), ("attn_sinks", 8), ("gmlp_ln_g", 512),
         ("gmlp_ln_b", 512), ("gmlp_w_s", 131072), ("gmlp_b_s", 1024), ("attn_out_norm_g", 512),
         ("gmlp_out_norm_g", 512), ("b_out", 1024), ("ffn2_norm_g", 1024), ("final_norm_g", 1024), ("loss", 1))


def _small_rows(n):
    return -(-n // 1024) * 8


SMALL_USED_ROWS = sum(_small_rows(n) for _, n in SMALL)
SMALL_ROWS = -(-SMALL_USED_ROWS // 16) * 16


def _pack_small(parts):
    out = []
    for name, n in SMALL:
        flat = parts[name].reshape(-1).astype(F32)
        rows = _small_rows(n)
        out.append(jnp.pad(flat, (0, rows * 128 - n)).reshape(rows, 128))
    if SMALL_ROWS > SMALL_USED_ROWS:
        out.append(jnp.zeros((SMALL_ROWS - SMALL_USED_ROWS, 128), F32))
    return jnp.concatenate(out, axis=0)


def _unpack_small(packed, shapes):
    res, off = {}, 0
    for name, n in SMALL:
        rows = _small_rows(n)
        res[name] = packed[off:off + rows].reshape(-1)[:n].reshape(shapes[name])
        off += rows
    return res


def _ffn_tile(x, g, wg_ref, wu_ref, wd_ref, hb_ref, a_ref, b_ref):
    h, _ = _rms(x, g)
    hb = h.astype(BF16)
    hb_ref[...] = hb
    acc = jnp.zeros(x.shape, F32)
    for j in range(N_CHIPS):
        a = _dot_nt(hb, wg_ref[j])
        b = _dot_nt(hb, wu_ref[j])
        a_ref[j] = a
        b_ref[j] = b
        f = (a * jax.nn.sigmoid(a) * b).astype(BF16)
        acc = acc + _dot(f, wd_ref[j])
    return x + FFN_RES * acc


def _ffn_saved_specs(s, tile):
    ab = pl.BlockSpec((N_CHIPS, tile, FF_SH), lambda i: (0, i, 0))
    shape = jax.ShapeDtypeStruct((N_CHIPS, s, FF_SH), F32)
    return [_rows(tile, D_MODEL), ab, ab], [jax.ShapeDtypeStruct((s, D_MODEL), BF16), shape, shape]


def _ffn_weight_specs(k0):
    one = pl.Buffered(1)
    return [pl.BlockSpec((N_CHIPS, FF_SH, D_MODEL), functools.partial(lambda kk, i: (0, kk, 0), k0 + d),
                         pipeline_mode=one) for d in range(3)]


def _mesh_place():
    x, y, c = lax.axis_index("x"), lax.axis_index("y"), lax.axis_index("c")
    others = [(1 - x, y), (x, 1 - y), (1 - x, 1 - y)]
    return x, y, c, others


def _gather_stages(o_ref, send_sems, recv_sems):
    x, y, c, others = _mesh_place()
    me = 2 * x + y
    sibling = (x, y, 1 - c)
    half_rows = o_ref.shape[1] // 2

    def half(slab, core):
        return o_ref.at[slab, pl.ds(pl.multiple_of(core * half_rows, 16), half_rows)]

    def copy(k, rows, to):
        return pltpu.make_async_remote_copy(src_ref=rows, dst_ref=rows, send_sem=send_sems.at[k],
                                            recv_sem=recv_sems.at[k], device_id=to, device_id_type=MESH)

    first = [copy(j, half(me, c), (px, py, c)) for j, (px, py) in enumerate(others)]
    passed = [copy(3 + j, half(2 * px + py, c), sibling) for j, (px, py) in enumerate(others)]

    def start():
        for cp in first:
            cp.start()

    def forward():
        for j, (px, py) in enumerate(others):
            copy(j, half(2 * px + py, c), (px, py, c)).wait_recv()
            passed[j].start()

    def finish():
        for j, (px, py) in enumerate(others):
            copy(3 + j, half(2 * px + py, 1 - c), sibling).wait_recv()
        for cp in first + passed:
            cp.wait_send()

    return start, forward, finish


def _ffn_fwd(x, g, pack, k0, gather, *, tile, name):
    s = x.shape[0]
    nt = s // tile
    forward_at = max(nt - 6, 0)

    def body(x_ref, g_ref, wg_ref, wu_ref, wd_ref, gin_ref, o_ref, hb_ref, a_ref, b_ref, gat_ref, send_sems, recv_sems):
        i = pl.program_id(0)
        start, forward, finish = _gather_stages(gat_ref, send_sems, recv_sems)
        pl.when(i == 0)(start)
        o_ref[...] = _ffn_tile(x_ref[...], g_ref[...], wg_ref, wu_ref, wd_ref, hb_ref, a_ref, b_ref)
        pl.when(i == forward_at)(forward)
        pl.when(i == nt - 1)(finish)

    saved_specs, saved_shapes = _ffn_saved_specs(s, tile)
    hbm = pl.BlockSpec(memory_space=pl.ANY)
    return pl.pallas_call(
        body, name=name, grid=(nt,),
        in_specs=[_rows(tile, D_MODEL), _const((1, D_MODEL))] + _ffn_weight_specs(k0) + [hbm],
        out_specs=[_rows(tile, D_MODEL)] + saved_specs + [hbm],
        out_shape=[jax.ShapeDtypeStruct(x.shape, F32)] + saved_shapes
                  + [jax.ShapeDtypeStruct(gather.shape, gather.dtype)],
        input_output_aliases={5: 4},
        scratch_shapes=[pltpu.SemaphoreType.DMA((6,)), pltpu.SemaphoreType.DMA((6,))],
        compiler_params=_cparams(("arbitrary",)),
    )(x, g, pack, pack, pack, gather)


def _ffn_fwd_loss(x, g, pack, k0, gf, tgt, *, tile, name):
    s = x.shape[0]

    def body(x_ref, g_ref, wg_ref, wu_ref, wd_ref, gf_ref, t_ref, dx_ref, loss_ref, dgf_ref, hb_ref, a_ref, b_ref,
             do_ref):
        @pl.when(pl.program_id(0) == 0)
        def _():
            loss_ref[...] = jnp.zeros_like(loss_ref)
            dgf_ref[...] = jnp.zeros_like(dgf_ref)

        x3 = _ffn_tile(x_ref[...], g_ref[...], wg_ref, wu_ref, wd_ref, hb_ref, a_ref, b_ref)
        gf_v = gf_ref[...]
        out, r = _rms(x3, gf_v)
        diff = out - t_ref[...]
        part = jnp.sum(jnp.sum(diff * diff, axis=-1, keepdims=True), axis=0, keepdims=True)
        loss_ref[...] += jnp.broadcast_to(part * (0.5 / D_MODEL), loss_ref.shape)
        dx, dg = _rms_bwd(diff * (1.0 / D_MODEL), x3, r, gf_v)
        dx_ref[...] = dx
        do_ref[...] = (FFN_RES * dx).astype(BF16)
        dgf_ref[...] += dg

    saved_specs, saved_shapes = _ffn_saved_specs(s, tile)
    return pl.pallas_call(
        body, name=name, grid=(s // tile,),
        in_specs=[_rows(tile, D_MODEL), _const((1, D_MODEL))] + _ffn_weight_specs(k0)
                 + [_const((1, D_MODEL)), _rows(tile, D_MODEL)],
        out_specs=[_rows(tile, D_MODEL), _const((1, 128)), _const((1, D_MODEL))] + saved_specs
                  + [_rows(tile, D_MODEL)],
        out_shape=[jax.ShapeDtypeStruct(x.shape, F32),
                   jax.ShapeDtypeStruct((1, 128), F32),
                   jax.ShapeDtypeStruct((1, D_MODEL), F32)] + saved_shapes
                  + [jax.ShapeDtypeStruct(x.shape, BF16)],
        compiler_params=_cparams(("arbitrary",)),
    )(x, g, pack, pack, pack, gf, tgt)


def _ffn_bwd(place, hb, a, b, do, pack, region, land, mix_grads, *, tile, name):
    s = hb.shape[0]
    nt = s // tile
    land_rows = pl.ds(region * FFN_HALF, FFN_HALF)
    mix_rows = pl.ds(2 * FFN_HALF, MIX_HALF)
    with_mix = mix_grads is not None
    with_land = land is not None
    n_others = 2 * N_CHIPS - 1

    def body(place_ref, hb_ref, a_ref, b_ref, do_ref, wg_ref, wu_ref, wd_ref, *rest):
        rest = list(rest)
        mix_ref = rest.pop(0) if with_mix else None
        if with_land:
            rest.pop(0)
        dhp_ref, land_ref, acc_ref, stage_ref, send_sems, recv_sem, local_sem = rest[:7]
        t, i = pl.program_id(0), pl.program_id(1)
        xi, yi, c = lax.axis_index("x"), lax.axis_index("y"), lax.axis_index("c")
        dev = 4 * xi + 2 * yi + c
        tt = (t + 1) % N_CHIPS
        tx, ty = jnp.bitwise_xor(xi, tt // 2), jnp.bitwise_xor(yi, tt % 2)

        def remote(src, dst, ssem, rsem, to):
            return pltpu.make_async_remote_copy(src_ref=src, dst_ref=dst, send_sem=ssem, recv_sem=rsem,
                                                device_id=to, device_id_type=MESH)

        def stage_half(h):
            return stage_ref.at[pl.ds(pl.multiple_of(h * FFN_HALF, 16), FFN_HALF)]

        if with_mix:
            mix_send, mix_recv, mix_local = rest[7:10]

            @pl.when(jnp.logical_and(t == 0, i == 0))
            def _():
                for chip in range(N_CHIPS):
                    for h in range(2):
                        src = mix_ref.at[chip, pl.ds(h * MIX_HALF, MIX_HALF)]
                        dst = land_ref.at[dev, mix_rows]
                        mine = jnp.logical_and(2 * xi + yi == chip, c == h)

                        @pl.when(mine)
                        def _():
                            pltpu.make_async_copy(src, dst, mix_local).start()

                        @pl.when(jnp.logical_not(mine))
                        def _():
                            remote(src, dst, mix_send, mix_recv, (chip // 2, chip % 2, h)).start()

        @pl.when(i == 0)
        def _():
            acc_ref[...] = jnp.zeros_like(acc_ref)

        hb = hb_ref[...]
        dob = do_ref[...]
        wg_j, wu_j, wd_j = wg_ref[0], wu_ref[0], wd_ref[0]
        a = a_ref[0]
        b = b_ref[0]
        sg = jax.nn.sigmoid(a)
        sa = a * sg
        fb = (sa * b).astype(BF16)
        df = _dot_nt(dob, wd_j)
        dbb = (df * sa).astype(BF16)
        dab = (df * b * (sg + sa * (1.0 - sg))).astype(BF16)
        dhp_ref[0] = (_dot(dab, wg_j) + _dot(dbb, wu_j)).astype(BF16)
        acc_ref[0:FF_SH, :] += _dot_tn(dab, hb)
        acc_ref[FF_SH:2 * FF_SH, :] += _dot_tn(dbb, hb)
        acc_ref[2 * FF_SH:3 * FF_SH, :] += _dot_tn(fb, dob)

        @pl.when(i == nt - 1)
        def _():
            dst = land_ref.at[dev, land_rows]

            @pl.when(t > 0)
            def _():
                for h in range(2):
                    remote(stage_half(h), dst, send_sems.at[h], recv_sem, (tx, ty, h)).wait_send()

            def cast_rows(r, carry):
                rows = pl.ds(pl.multiple_of(r * MIX_HALF, 16), MIX_HALF)
                stage_ref[rows, :] = acc_ref[rows, :].astype(BF16)
                return carry

            lax.fori_loop(0, 3 * FF_SH // MIX_HALF, cast_rows, 0)

            @pl.when(t < N_CHIPS - 1)
            def _():
                for h in range(2):
                    remote(stage_half(h), dst, send_sems.at[h], recv_sem, (tx, ty, h)).start()

            @pl.when(t == N_CHIPS - 1)
            def _():
                own = pltpu.make_async_copy(stage_half(c), dst, local_sem)
                own.start()
                sib = remote(stage_half(1 - c), dst, send_sems.at[0], recv_sem, (xi, yi, 1 - c))
                sib.start()
                sib.wait_send()
                own.wait()
                arrivals = land_ref.at[pl.ds(0, n_others), land_rows]
                remote(arrivals, arrivals, send_sems.at[0], recv_sem, (xi, yi, 1 - c)).wait_recv()
                if with_mix:
                    seven = land_ref.at[pl.ds(0, n_others), mix_rows]
                    both = remote(seven, seven, mix_send, mix_recv, (xi, yi, 1 - c))
                    both.wait_send()
                    both.wait_recv()
                    pltpu.make_async_copy(mix_ref.at[0, pl.ds(0, MIX_HALF)], land_ref.at[dev, mix_rows],
                                          mix_local).wait()

    def wspec(kk):
        return pl.BlockSpec((1, FF_SH, D_MODEL),
                            lambda t, i, pr: (jnp.bitwise_xor(pr[0], (t + 1) % N_CHIPS), kk, 0))

    xspec = pl.BlockSpec((tile, D_MODEL), lambda t, i, pr: (i, 0))
    abspec = pl.BlockSpec((1, tile, FF_SH), lambda t, i, pr: (jnp.bitwise_xor(pr[0], (t + 1) % N_CHIPS), i, 0))
    hbm = pl.BlockSpec(memory_space=pl.ANY)
    in_specs = [xspec, abspec, abspec, xspec, wspec(0), wspec(1), wspec(2)]
    operands = [place, hb, a, b, do, pack, pack, pack]
    scratch = [pltpu.VMEM((3 * FF_SH, D_MODEL), F32), pltpu.VMEM((3 * FF_SH, D_MODEL), BF16),
               pltpu.SemaphoreType.DMA((2,)), pltpu.SemaphoreType.DMA, pltpu.SemaphoreType.DMA]
    if with_mix:
        in_specs.append(hbm)
        operands.append(mix_grads)
        scratch += [pltpu.SemaphoreType.DMA, pltpu.SemaphoreType.DMA, pltpu.SemaphoreType.DMA]
    aliases = {}
    if with_land:
        in_specs.append(hbm)
        operands.append(land)
        aliases = {len(operands) - 1: 1}
    grid_spec = pltpu.PrefetchScalarGridSpec(
        num_scalar_prefetch=1, grid=(N_CHIPS, nt), in_specs=in_specs,
        out_specs=[pl.BlockSpec((1, tile, D_MODEL), lambda t, i, pr: (t, i, 0)), hbm],
        scratch_shapes=scratch)
    return pl.pallas_call(
        body, name=name, grid_spec=grid_spec,
        out_shape=[jax.ShapeDtypeStruct((N_CHIPS, s, D_MODEL), BF16),
                   jax.ShapeDtypeStruct((2 * N_CHIPS, HALF_ROWS, D_MODEL), BF16)],
        input_output_aliases=aliases,
        compiler_params=_cparams(("arbitrary", "arbitrary")),
    )(*operands)


def _mix_grads_pack(dw_in_t, dw_out, *, name):
    def body(a_ref, b_ref, o_ref):
        o_ref[0, 0:IN_SH, :] = a_ref[0].astype(BF16)
        o_ref[0, IN_SH:FF_SH, :] = b_ref[0].astype(BF16)

    return pl.pallas_call(
        body, name=name, grid=(N_CHIPS,),
        in_specs=[pl.BlockSpec((1, IN_SH, D_MODEL), lambda j: (j, 0, 0)),
                  pl.BlockSpec((1, OUT_SH, D_MODEL), lambda j: (j, 0, 0))],
        out_specs=pl.BlockSpec((1, FF_SH, D_MODEL), lambda j: (j, 0, 0)),
        out_shape=jax.ShapeDtypeStruct((N_CHIPS, FF_SH, D_MODEL), BF16),
        compiler_params=_cparams(("arbitrary",)),
    )(dw_in_t.reshape(N_CHIPS, IN_SH, D_MODEL), dw_out.reshape(N_CHIPS, OUT_SH, D_MODEL))


def _norm_bwd(dhp, x, dy, g, *, tile, name):
    s = x.shape[0]

    def body(dhp_ref, x_ref, dy_ref, g_ref, dx_ref, dg_ref):
        @pl.when(pl.program_id(0) == 0)
        def _():
            dg_ref[...] = jnp.zeros_like(dg_ref)

        dh = ((dhp_ref[0].astype(F32) + dhp_ref[1].astype(F32))
              + (dhp_ref[2].astype(F32) + dhp_ref[3].astype(F32)))
        x_v = x_ref[...]
        r = lax.rsqrt(jnp.mean(x_v * x_v, axis=-1, keepdims=True) + EPS)
        dx, dg = _rms_bwd(dh, x_v, r, g_ref[...])
        dx_ref[...] = dy_ref[...] + dx
        dg_ref[...] += dg

    return pl.pallas_call(
        body, name=name, grid=(s // tile,),
        in_specs=[pl.BlockSpec((N_CHIPS, tile, D_MODEL), lambda i: (0, i, 0)),
                  _rows(tile, D_MODEL), _rows(tile, D_MODEL), _const((1, D_MODEL))],
        out_specs=[_rows(tile, D_MODEL), _const((1, D_MODEL))],
        out_shape=[jax.ShapeDtypeStruct(x.shape, F32), jax.ShapeDtypeStruct((1, D_MODEL), F32)],
        compiler_params=_cparams(("arbitrary",)),
    )(dhp, x, dy, g)


def _mix_in_bwd(x, dy, dq, dk, dv, dz, g, w_in_t, *, tile, name):
    s = x.shape[0]

    def body(x_ref, dy_ref, dq_ref, dk_ref, dv_ref, dz_ref, g_ref, w_ref, dx_ref, dw_ref, db_ref, dg_ref, do_ref):
        @pl.when(pl.program_id(0) == 0)
        def _():
            dw_ref[...] = jnp.zeros_like(dw_ref)
            db_ref[...] = jnp.zeros_like(db_ref)
            dg_ref[...] = jnp.zeros_like(dg_ref)

        dproj = jnp.concatenate([dq_ref[...], dk_ref[...], dv_ref[...], dz_ref[...]], axis=-1)
        db_ref[...] += jnp.sum(dproj, axis=0, keepdims=True)
        dpb = dproj.astype(BF16)
        x_v = x_ref[...]
        g_v = g_ref[...]
        h, r = _rms(x_v, g_v)
        dw_ref[...] += _dot_tn(dpb, h.astype(BF16))
        dh = _dot(dpb, w_ref[...])
        dxn, dg = _rms_bwd(dh, x_v, r, g_v)
        dx = dy_ref[...] + dxn
        dx_ref[...] = dx
        do_ref[...] = (FFN_RES * dx).astype(BF16)
        dg_ref[...] += dg

    return pl.pallas_call(
        body, name=name, grid=(s // tile,),
        in_specs=[_rows(tile, D_MODEL), _rows(tile, D_MODEL), _rows(tile, ATTN_W), _rows(tile, KV_W),
                  _rows(tile, KV_W), _rows(tile, 2 * GMLP_W), _const((1, D_MODEL)), _const((IN_W, D_MODEL))],
        out_specs=[_rows(tile, D_MODEL), _const((IN_W, D_MODEL)), _const((1, IN_W)), _const((1, D_MODEL)),
                   _rows(tile, D_MODEL)],
        out_shape=[jax.ShapeDtypeStruct(x.shape, F32), jax.ShapeDtypeStruct((IN_W, D_MODEL), F32),
                   jax.ShapeDtypeStruct((1, IN_W), F32), jax.ShapeDtypeStruct((1, D_MODEL), F32),
                   jax.ShapeDtypeStruct(x.shape, BF16)],
        compiler_params=_cparams(("arbitrary",)),
    )(x, dy, dq, dk, dv, dz, g, w_in_t)


_GELU_C = 0.7978845608028654
_GELU_A = 0.044715


def _gelu_tanh(x):
    x2 = x * x
    return jnp.tanh(_GELU_C * (x + _GELU_A * (x2 * x))), x2


def _band(ref, i):
    prev = jnp.maximum(i - 1, 0)
    return jnp.concatenate([ref[pl.ds(pl.multiple_of(prev * BLK, BLK), BLK), :],
                            ref[pl.ds(pl.multiple_of(i * BLK, BLK), BLK), :]], axis=0)


def _key_in_block():
    return lax.broadcasted_iota(jnp.int32, (BLK, BLK), 0) <= lax.broadcasted_iota(jnp.int32, (BLK, BLK), 1)


def _fold(band, own):
    return jnp.where(own, band[BLK:], band[:BLK])


def _unfold(a, own):
    zero = jnp.zeros_like(a)
    return jnp.concatenate([jnp.where(own, zero, a), jnp.where(own, a, zero)], axis=0).astype(BF16)


def _attn_fwd(q, kb, vb, i, sink_ref):
    own = _key_in_block()
    outs, saved = [], []
    for h in range(N_Q_HEADS):
        cols = slice((h // REP) * HEAD_DIM, (h // REP + 1) * HEAD_DIM)
        s2 = _dot_nt(kb[:, cols], q[:, h * HEAD_DIM:(h + 1) * HEAD_DIM])
        sc = jnp.where(own, s2[BLK:], jnp.where(i > 0, s2[:BLK], -jnp.inf))
        sink = sink_ref[h]
        m = jnp.maximum(jnp.max(sc, axis=0, keepdims=True), sink)
        p = jnp.exp(sc - m)
        es = jnp.exp(sink - m)
        inv = 1.0 / (jnp.sum(p, axis=0, keepdims=True) + es)
        pn = p * inv
        band = _unfold(pn, own)
        outs.append(_dot_tn(band, vb[:, cols]))
        saved.append((pn, band, es * inv))
    return jnp.concatenate(outs, axis=-1), saved


def _tril_mask():
    t = lax.broadcasted_iota(jnp.int32, (BLK, BLK), 0)
    s_ = lax.broadcasted_iota(jnp.int32, (BLK, BLK), 1)
    return s_ <= t


def _gmlp_fwd_parts(zg, lng, lnb, ws_ref, bs_full):
    th, zg2 = _gelu_tanh(zg)
    z = 0.5 * zg * (1.0 + th)
    u = z[:, :GMLP_W]
    zv = z[:, GMLP_W:]
    mu = jnp.mean(zv, axis=-1, keepdims=True)
    zc = zv - mu
    rstd = lax.rsqrt(jnp.mean(zc * zc, axis=-1, keepdims=True) + EPS)
    xh = zc * rstd
    vvb = (xh * lng + lnb).astype(BF16)
    tril = _tril_mask()
    wms, parts = [], []
    for gi in range(GMLP_GROUPS):
        wm = jnp.where(tril, ws_ref[gi], 0.0).astype(BF16)
        wms.append(wm)
        parts.append(_dot(wm, vvb[:, gi * GROUP_DIM:(gi + 1) * GROUP_DIM]))
    mixed = jnp.concatenate(parts, axis=-1) + bs_full
    gelu_grad = 0.5 * (1.0 + th) + 0.5 * zg * (1.0 - th * th) * (_GELU_C * (1.0 + 3.0 * _GELU_A * zg2))
    return u, xh, rstd, vvb, wms, mixed, gelu_grad


def _mixer_fwd(x1, g, w_in_t, b_in, sinks, lng, lnb, w_s, bs_full, gao, ggo, w_out, b_out, *, name):
    s = x1.shape[0]
    nb = min(MIX_FWD_BLOCKS, s // BLK)
    step_rows = nb * BLK
    last = s // step_rows - 1

    def tile_of(i, lag):
        return jnp.clip(i - lag, 0, last)

    def body(sink_ref, xa_ref, xc_ref, g_ref, wi_ref, bi_ref, lng_ref, lnb_ref, ws_ref, bs_ref, gao_ref, ggo_ref,
             wo_ref, bo_ref, q_ref, k_ref, v_ref, z_ref, y_ref, o_ref, qs_ref, zs_ref, ys_ref):
        i = pl.program_id(0)

        @pl.when(i == 0)
        def _():
            for ref in (k_ref, v_ref, qs_ref, zs_ref, ys_ref):
                ref[...] = jnp.zeros_like(ref)

        slot_a, slot_b, slot_c = i % 2, (i + 1) % 2, i % 2

        o_ref[...] = xc_ref[...] + (_dot(ys_ref[slot_c], wo_ref[...]) + bo_ref[...])

        tile_b = tile_of(i, 1)
        for b in range(nb):
            blk = tile_b * nb + b
            rows = slice(b * BLK, (b + 1) * BLK)
            y_attn, _ = _attn_fwd(qs_ref[slot_b, rows, :], _band(k_ref, blk), _band(v_ref, blk), blk, sink_ref)
            u, _, _, _, _, mixed, _ = _gmlp_fwd_parts(zs_ref[slot_b, rows, :], lng_ref[...], lnb_ref[...], ws_ref,
                                                      bs_ref[...])
            ya, _ = _rms(y_attn, gao_ref[...])
            yg, _ = _rms(u * mixed, ggo_ref[...])
            y_blk = jnp.concatenate([ya, yg], axis=-1).astype(BF16)
            y_ref[rows, :] = y_blk
            ys_ref[slot_b, rows, :] = y_blk

        h, _ = _rms(xa_ref[...], g_ref[...])
        proj = _dot_nt(h.astype(BF16), wi_ref[...]) + bi_ref[...]
        q_t = (proj[:, :ATTN_W] * ATTN_SCALE).astype(BF16)
        z_t = proj[:, ATTN_W + 2 * KV_W:]
        here = pl.ds(pl.multiple_of(tile_of(i, 0) * step_rows, step_rows), step_rows)
        q_ref[...] = q_t
        z_ref[...] = z_t
        qs_ref[slot_a] = q_t
        zs_ref[slot_a] = z_t
        k_ref[here, :] = proj[:, ATTN_W:ATTN_W + KV_W].astype(BF16)
        v_ref[here, :] = proj[:, ATTN_W + KV_W:ATTN_W + 2 * KV_W].astype(BF16)

    def lagged(width, lag):
        return pl.BlockSpec((step_rows, width), lambda i: (tile_of(i, lag), 0))

    return pl.pallas_call(
        body, name=name, grid=(last + 3,),
        in_specs=[pl.BlockSpec(memory_space=pltpu.SMEM),
                  lagged(D_MODEL, 0), lagged(D_MODEL, 2), _const((1, D_MODEL)), _const((IN_W, D_MODEL)),
                  _const((1, IN_W)), _const((1, GMLP_W)), _const((1, GMLP_W)), _const((GMLP_GROUPS, BLK, BLK)),
                  _const((BLK, GMLP_W)), _const((1, ATTN_W)), _const((1, GMLP_W)), _const((D_MODEL, D_MODEL)),
                  _const((1, D_MODEL))],
        out_specs=[lagged(ATTN_W, 0), _const((s, KV_W)), _const((s, KV_W)), lagged(2 * GMLP_W, 0),
                   lagged(D_MODEL, 1), lagged(D_MODEL, 2)],
        out_shape=[jax.ShapeDtypeStruct((s, ATTN_W), BF16), jax.ShapeDtypeStruct((s, KV_W), BF16),
                   jax.ShapeDtypeStruct((s, KV_W), BF16), jax.ShapeDtypeStruct((s, 2 * GMLP_W), F32),
                   jax.ShapeDtypeStruct((s, D_MODEL), BF16), jax.ShapeDtypeStruct((s, D_MODEL), F32)],
        scratch_shapes=[pltpu.VMEM((2, step_rows, ATTN_W), BF16), pltpu.VMEM((2, step_rows, 2 * GMLP_W), F32),
                        pltpu.VMEM((2, step_rows, D_MODEL), BF16)],
        compiler_params=_cparams(("arbitrary",)),
    )(sinks, x1, x1, g, w_in_t, b_in, lng, lnb, w_s, bs_full, gao, ggo, w_out, b_out)


def _norm_bwd_mix_out(dhp, x, dy, g, yb, w_out, *, tile, name):
    s = x.shape[0]

    def body(dhp_ref, x_ref, dy_ref, g_ref, y_ref, w_ref, dx_ref, dg_ref, dyy_ref, dw_ref, db_ref):
        @pl.when(pl.program_id(0) == 0)
        def _():
            dg_ref[...] = jnp.zeros_like(dg_ref)
            dw_ref[...] = jnp.zeros_like(dw_ref)
            db_ref[...] = jnp.zeros_like(db_ref)

        dh = ((dhp_ref[0].astype(F32) + dhp_ref[1].astype(F32))
              + (dhp_ref[2].astype(F32) + dhp_ref[3].astype(F32)))
        x_v = x_ref[...]
        r = lax.rsqrt(jnp.mean(x_v * x_v, axis=-1, keepdims=True) + EPS)
        dxn, dg = _rms_bwd(dh, x_v, r, g_ref[...])
        dx = dy_ref[...] + dxn
        dx_ref[...] = dx
        dg_ref[...] += dg
        dxb = dx.astype(BF16)
        db_ref[...] += jnp.sum(dx, axis=0, keepdims=True)
        dw_ref[...] += _dot_tn(y_ref[...], dxb)
        dyy_ref[...] = _dot_nt(dxb, w_ref[...])

    return pl.pallas_call(
        body, name=name, grid=(s // tile,),
        in_specs=[pl.BlockSpec((N_CHIPS, tile, D_MODEL), lambda i: (0, i, 0)),
                  _rows(tile, D_MODEL), _rows(tile, D_MODEL), _const((1, D_MODEL)), _rows(tile, D_MODEL),
                  _const((D_MODEL, D_MODEL))],
        out_specs=[_rows(tile, D_MODEL), _const((1, D_MODEL)), _rows(tile, D_MODEL), _const((D_MODEL, D_MODEL)),
                   _const((1, D_MODEL))],
        out_shape=[jax.ShapeDtypeStruct(x.shape, F32), jax.ShapeDtypeStruct((1, D_MODEL), F32),
                   jax.ShapeDtypeStruct(x.shape, F32), jax.ShapeDtypeStruct((D_MODEL, D_MODEL), F32),
                   jax.ShapeDtypeStruct((1, D_MODEL), F32)],
        compiler_params=_cparams(("arbitrary",)),
    )(dhp, x, dy, g, yb, w_out)


def _mix_core_bwd(dyy, q, k, v, zg, sinks, lng, lnb, w_s, bs_full, gao, ggo, *, name):
    s = dyy.shape[0]
    nb = min(MIX_BWD_BLOCKS, s // BLK)
    nsteps = s // (nb * BLK)

    def body(*refs):
        accumulators = refs[13:15] + refs[16:]

        @pl.when(pl.program_id(0) == 0)
        def _():
            for ref in accumulators:
                ref[...] = jnp.zeros_like(ref)

        for b in range(nb):
            one_block(pl.program_id(0) * nb + b, slice(b * BLK, (b + 1) * BLK), *refs)

        @pl.when(pl.program_id(0) == nsteps - 1)
        def _():
            tril = _tril_mask()
            for gi in range(GMLP_GROUPS):
                refs[20][gi] = jnp.where(tril, refs[20][gi], 0.0)

    def one_block(i, rows, sink_ref, dyy_ref, q_ref, k_ref, v_ref, z_ref, lng_ref, lnb_ref, ws_ref, bs_ref, gao_ref,
                  ggo_ref, dq_ref, dk_ref, dv_ref, dz_ref, dgao_ref, dggo_ref, dlng_ref, dlnb_ref, dws_ref, dms_ref,
                  dsk_ref):
        q_v = q_ref[rows, :]
        kb = _band(k_ref, i)
        vb = _band(v_ref, i)
        lng_v = lng_ref[...]
        gao_v = gao_ref[...]
        ggo_v = ggo_ref[...]

        y_attn, probs = _attn_fwd(q_v, kb, vb, i, sink_ref)
        u, xh, rstd, vvb, wms, mixed, gelu_grad = _gmlp_fwd_parts(z_ref[rows, :], lng_v, lnb_ref[...], ws_ref,
                                                                  bs_ref[...])
        y_gmlp = u * mixed
        ra = lax.rsqrt(jnp.mean(y_attn * y_attn, axis=-1, keepdims=True) + EPS)
        rg = lax.rsqrt(jnp.mean(y_gmlp * y_gmlp, axis=-1, keepdims=True) + EPS)

        dyy = dyy_ref[rows, :]
        d_attn, dgao = _rms_bwd(dyy[:, :ATTN_W], y_attn, ra, gao_v)
        d_gmlp, dggo = _rms_bwd(dyy[:, ATTN_W:], y_gmlp, rg, ggo_v)
        dgao_ref[...] += dgao
        dggo_ref[...] += dggo

        du = d_gmlp * mixed
        dmixed = d_gmlp * u
        dms_ref[...] += dmixed
        dmb = dmixed.astype(BF16)
        dvv_parts = []
        for gi in range(GMLP_GROUPS):
            sl = slice(gi * GROUP_DIM, (gi + 1) * GROUP_DIM)
            dws_ref[gi] += _dot_nt(dmb[:, sl], vvb[:, sl])
            dvv_parts.append(_dot_tn(wms[gi], dmb[:, sl]))
        dvv = jnp.concatenate(dvv_parts, axis=-1)
        dlng_ref[...] += jnp.sum(dvv * xh, axis=0, keepdims=True)
        dlnb_ref[...] += jnp.sum(dvv, axis=0, keepdims=True)
        dxh = dvv * lng_v
        dzv = rstd * (dxh - jnp.mean(dxh, axis=-1, keepdims=True)
                      - xh * jnp.mean(dxh * xh, axis=-1, keepdims=True))
        dz_ref[rows, :] = jnp.concatenate([du, dzv], axis=-1) * gelu_grad

        dab = d_attn.astype(BF16)
        own = _key_in_block()
        dq_parts = []
        dk_parts = []
        dv_parts = []
        for gi in range(N_KV_HEADS):
            cols = slice(gi * HEAD_DIM, (gi + 1) * HEAD_DIM)
            kg, vg = kb[:, cols], vb[:, cols]
            dkg = jnp.zeros((2 * BLK, HEAD_DIM), F32)
            dvg = jnp.zeros((2 * BLK, HEAD_DIM), F32)
            for rr in range(REP):
                h = gi * REP + rr
                hs = slice(h * HEAD_DIM, (h + 1) * HEAD_DIM)
                qh, doh = q_v[:, hs], dab[:, hs]
                pn, band, psink = probs[h]
                dp = _fold(_dot_nt(vg, doh), own)
                delta = jnp.sum(pn * dp, axis=0, keepdims=True)
                ds2 = _unfold(pn * (dp - delta), own)
                dsink = jnp.sum(-psink * delta, axis=-1, keepdims=True)
                dsk_ref[pl.ds(h, 1), :] += jnp.broadcast_to(dsink, (1, 128))
                dq_parts.append(_dot_tn(ds2, kg) * ATTN_SCALE)
                dkg = dkg + _dot(ds2, qh)
                dvg = dvg + _dot(band, doh)
            dk_parts.append(dkg)
            dv_parts.append(dvg)
        dq_ref[rows, :] = jnp.concatenate(dq_parts, axis=-1)
        dkb = jnp.concatenate(dk_parts, axis=-1)
        dvb = jnp.concatenate(dv_parts, axis=-1)
        prev = pl.ds(pl.multiple_of(jnp.maximum(i - 1, 0) * BLK, BLK), BLK)
        cur = pl.ds(pl.multiple_of(i * BLK, BLK), BLK)
        dk_ref[prev, :] += dkb[:BLK]
        dv_ref[prev, :] += dvb[:BLK]
        dk_ref[cur, :] += dkb[BLK:]
        dv_ref[cur, :] += dvb[BLK:]

    return pl.pallas_call(
        body, name=name, grid=(nsteps,),
        in_specs=[pl.BlockSpec(memory_space=pltpu.SMEM),
                  _rows(nb * BLK, D_MODEL), _rows(nb * BLK, ATTN_W), _const((s, KV_W)), _const((s, KV_W)),
                  _rows(nb * BLK, 2 * GMLP_W), _const((1, GMLP_W)), _const((1, GMLP_W)),
                  _const((GMLP_GROUPS, BLK, BLK)), _const((BLK, GMLP_W)), _const((1, ATTN_W)), _const((1, GMLP_W))],
        out_specs=[_rows(nb * BLK, ATTN_W), _const((s, KV_W)), _const((s, KV_W)), _rows(nb * BLK, 2 * GMLP_W),
                   _const((1, ATTN_W)), _const((1, GMLP_W)),
                   _const((1, GMLP_W)), _const((1, GMLP_W)), _const((GMLP_GROUPS, BLK, BLK)),
                   _const((BLK, GMLP_W)), _const((N_Q_HEADS, 128))],
        out_shape=[jax.ShapeDtypeStruct((s, ATTN_W), F32), jax.ShapeDtypeStruct((s, KV_W), F32),
                   jax.ShapeDtypeStruct((s, KV_W), F32), jax.ShapeDtypeStruct((s, 2 * GMLP_W), F32),
                   jax.ShapeDtypeStruct((1, ATTN_W), F32), jax.ShapeDtypeStruct((1, GMLP_W), F32),
                   jax.ShapeDtypeStruct((1, GMLP_W), F32), jax.ShapeDtypeStruct((1, GMLP_W), F32),
                   jax.ShapeDtypeStruct((GMLP_GROUPS, BLK, BLK), F32), jax.ShapeDtypeStruct((BLK, GMLP_W), F32),
                   jax.ShapeDtypeStruct((N_Q_HEADS, 128), F32)],
        compiler_params=_cparams(("arbitrary",)),
    )(sinks, dyy, q, k, v, zg, lng, lnb, w_s, bs_full, gao, ggo)


def _local_step(place, x, tgt, p, pack_a, pack_b, *, tile=512, fwd_tile=256, bwd_tile=512, norm_tile=512):
    g = {}
    tile, fwd_tile, bwd_tile, norm_tile = (min(t_, x.shape[0]) for t_ in (tile, fwd_tile, bwd_tile, norm_tile))
    x1, hb1, a1, b1, pack_b = _ffn_fwd(x, p["ffn1_norm_g"], pack_a, 0, pack_b, tile=fwd_tile, name="ffn1_fwd")
    mix_rows = pack_b[:, 3 * FF_SH:, :]
    w_in_t = mix_rows[:, :IN_SH, :].reshape(IN_W, D_MODEL)
    w_out = mix_rows[:, IN_SH:, :].reshape(D_MODEL, D_MODEL)
    q, k, v, zg, yb, x2 = _mixer_fwd(
        x1, p["mix_norm_g"], w_in_t, p["b_in"], p["attn_sinks"], p["gmlp_ln_g"], p["gmlp_ln_b"], p["gmlp_w_s"],
        p["bs_full"], p["attn_out_norm_g"], p["gmlp_out_norm_g"], w_out, p["b_out"], name="mixer_fwd")
    mix_args = (q, k, v, zg, p["attn_sinks"], p["gmlp_ln_g"], p["gmlp_ln_b"], p["gmlp_w_s"], p["bs_full"],
                p["attn_out_norm_g"], p["gmlp_out_norm_g"])
    dx3, loss, g["final_norm_g"], hb2, a2, b2, do3 = _ffn_fwd_loss(
        x2, p["ffn2_norm_g"], pack_b, 0, p["final_norm_g"], tgt, tile=fwd_tile, name="ffn2_fwd_loss")

    dhp, land = _ffn_bwd(place, hb2, a2, b2, do3, pack_b, 1, None, None, tile=bwd_tile, name="ffn2_bwd")
    dx2, g["ffn2_norm_g"], dyy, dw_out, g["b_out"] = _norm_bwd_mix_out(
        dhp, x2, dx3, p["ffn2_norm_g"], yb, w_out, tile=norm_tile, name="ffn2_norm_bwd")

    (dq, dk, dv, dz, g["attn_out_norm_g"], g["gmlp_out_norm_g"], g["gmlp_ln_g"],
     g["gmlp_ln_b"], g["gmlp_w_s"], dmix_sum, dsinks) = _mix_core_bwd(dyy, *mix_args, name="mix_core_bwd")
    g["gmlp_b_s"] = dmix_sum
    g["attn_sinks"] = dsinks
    dx1, dw_in_t, g["b_in"], g["mix_norm_g"], do1 = _mix_in_bwd(
        x1, dx2, dq, dk, dv, dz, p["mix_norm_g"], w_in_t, tile=tile, name="mix_in_bwd")
    mix_grads = _mix_grads_pack(dw_in_t, dw_out, name="mix_grads_pack")

    dhp1, land = _ffn_bwd(place, hb1, a1, b1, do1, pack_a, 0, land, mix_grads, tile=bwd_tile, name="ffn1_bwd")
    dx0, g["ffn1_norm_g"] = _norm_bwd(dhp1, x, dx1, p["ffn1_norm_g"], tile=norm_tile, name="ffn1_norm_bwd")
    return loss, dx0, land, g


def _pack_cast(place, parts, *, name):
    def body(place_ref, *refs):
        oa_ref, ob_ref = refs[-2], refs[-1]
        off = 0
        for k, (ref, rows) in enumerate(zip(refs[:-2], BIG_ROWS)):
            if k == 3:
                off = 0
            (oa_ref if k < 3 else ob_ref)[0, off:off + rows, :] = ref[...].astype(BF16)
            off += rows

    one = pl.Buffered(1)

    def slab(rows):
        return pl.BlockSpec((1, rows, D_MODEL), lambda i, pr: (pr[0], 0, 0), pipeline_mode=one)

    grid_spec = pltpu.PrefetchScalarGridSpec(
        num_scalar_prefetch=1, grid=(1,),
        in_specs=[pl.BlockSpec((rows, D_MODEL), lambda i, pr: (0, 0), pipeline_mode=one) for rows in BIG_ROWS],
        out_specs=[slab(PACK_A_ROWS), slab(PACK_B_ROWS)])
    return pl.pallas_call(
        body, name=name, grid_spec=grid_spec,
        out_shape=[jax.ShapeDtypeStruct((N_CHIPS, PACK_A_ROWS, D_MODEL), BF16),
                   jax.ShapeDtypeStruct((N_CHIPS, PACK_B_ROWS, D_MODEL), BF16)],
        compiler_params=_cparams(("arbitrary",)),
    )(place, *parts)


def _all_gather_pack(pack, *, name):
    def body(p_ref, o_ref, send_sems, recv_sems):
        start, forward, finish = _gather_stages(o_ref, send_sems, recv_sems)
        start()
        forward()
        finish()

    return pl.pallas_call(
        body, name=name,
        in_specs=[pl.BlockSpec(memory_space=pl.ANY)],
        out_specs=pl.BlockSpec(memory_space=pl.ANY),
        out_shape=jax.ShapeDtypeStruct(pack.shape, pack.dtype),
        input_output_aliases={0: 0},
        scratch_shapes=[pltpu.SemaphoreType.DMA((6,)), pltpu.SemaphoreType.DMA((6,))],
    )(pack)


def _shard_tile(i, c):
    return jnp.where(i < 3, 3 * c + i, jnp.where(i < 6, 3 + 3 * c + i, 12 + c))


def _rs_reduce(place, land, *, name):
    def body(place_ref, l_ref, o_ref):
        acc = l_ref[0].astype(F32)
        for d in range(1, 2 * N_CHIPS):
            acc = acc + l_ref[d].astype(F32)
        o_ref[...] = acc

    grid_spec = pltpu.PrefetchScalarGridSpec(
        num_scalar_prefetch=1, grid=(HALF_ROWS // MIX_HALF,),
        in_specs=[pl.BlockSpec((2 * N_CHIPS, MIX_HALF, D_MODEL), lambda i, pr: (0, i, 0))],
        out_specs=pl.BlockSpec((MIX_HALF, D_MODEL), lambda i, pr: (_shard_tile(i, pr[1]), 0)))
    return pl.pallas_call(
        body, name=name, grid_spec=grid_spec,
        out_shape=jax.ShapeDtypeStruct((PACK_ROWS, D_MODEL), F32),
        compiler_params=_cparams(("arbitrary",)),
    )(place, land)


def _rs_share(shard, *, name):
    def body(s_ref, o_ref, send_sems, recv_sems):
        x, y, c, _ = _mesh_place()

        def rows(k, core):
            if k < 2:
                return o_ref.at[pl.ds(pl.multiple_of(k * 2 * FFN_HALF + core * FFN_HALF, 8), FFN_HALF)]
            return o_ref.at[pl.ds(pl.multiple_of(4 * FFN_HALF + core * MIX_HALF, 8), MIX_HALF)]

        def copy(k, core):
            return pltpu.make_async_remote_copy(src_ref=rows(k, core), dst_ref=rows(k, core), send_sem=send_sems.at[k],
                                                recv_sem=recv_sems.at[k], device_id=(x, y, 1 - c),
                                                device_id_type=MESH)

        sends = [copy(k, c) for k in range(3)]
        for cp in sends:
            cp.start()
        for k in range(3):
            copy(k, 1 - c).wait_recv()
        for cp in sends:
            cp.wait_send()

    return pl.pallas_call(
        body, name=name,
        in_specs=[pl.BlockSpec(memory_space=pl.ANY)],
        out_specs=pl.BlockSpec(memory_space=pl.ANY),
        out_shape=jax.ShapeDtypeStruct((PACK_ROWS, D_MODEL), F32),
        input_output_aliases={0: 0},
        scratch_shapes=[pltpu.SemaphoreType.DMA((3,)), pltpu.SemaphoreType.DMA((3,))],
    )(shard)


def _small_all_reduce(packed, *, name):
    rows = packed.shape[0]
    half = rows // 2

    def body(p_ref, o_ref, sib_ref, slots_ref, send_sems, recv_sems):
        x, y, c, others = _mesh_place()
        me = 2 * x + y
        sibling = (x, y, 1 - c)

        def half_of(core):
            return pl.ds(pl.multiple_of(core * half, 8), half)

        def remote(k, src, dst, to):
            return pltpu.make_async_remote_copy(src_ref=src, dst_ref=dst, send_sem=send_sems.at[k],
                                                recv_sem=recv_sems.at[k], device_id=to, device_id_type=MESH)

        sib = remote(0, p_ref.at[half_of(1 - c)], sib_ref, sibling)
        sib.start()
        sib.wait()
        slots_ref[me] = p_ref[half_of(c), :] + sib_ref[...]
        sends = [remote(1 + j, slots_ref.at[me], slots_ref.at[me], (px, py, c)) for j, (px, py) in enumerate(others)]
        for cp in sends:
            cp.start()
        for j, (px, py) in enumerate(others):
            slab = slots_ref.at[2 * px + py]
            remote(1 + j, slab, slab, (px, py, c)).wait_recv()
        for cp in sends:
            cp.wait_send()
        o_ref[half_of(c), :] = (slots_ref[0] + slots_ref[1]) + (slots_ref[2] + slots_ref[3])
        back = remote(4, o_ref.at[half_of(c)], o_ref.at[half_of(c)], sibling)
        back.start()
        remote(4, o_ref.at[half_of(1 - c)], o_ref.at[half_of(1 - c)], sibling).wait_recv()
        back.wait_send()

    vm = pl.BlockSpec(memory_space=pltpu.VMEM)
    return pl.pallas_call(
        body, name=name, in_specs=[vm], out_specs=vm,
        out_shape=jax.ShapeDtypeStruct((rows, 128), F32),
        scratch_shapes=[pltpu.VMEM((half, 128), F32), pltpu.VMEM((N_CHIPS, half, 128), F32),
                        pltpu.SemaphoreType.DMA((5,)), pltpu.SemaphoreType.DMA((5,))],
    )(packed)


def _adamw(w, g, m, v, *, g_row0, tile, name):
    rows, cols = w.shape
    assert g_row0 % tile == 0 and rows % tile == 0

    def body(w_ref, g_ref, m_ref, v_ref, go_ref, d_ref, nm_ref, nv_ref):
        g_v = g_ref[...]
        m_n = ADAM_B1 * m_ref[...] + (1.0 - ADAM_B1) * g_v
        v_n = ADAM_B2 * v_ref[...] + (1.0 - ADAM_B2) * (g_v * g_v)
        m_hat = m_n / (1.0 - ADAM_B1 ** ADAM_STEP)
        v_hat = v_n / (1.0 - ADAM_B2 ** ADAM_STEP)
        d_ref[...] = -ADAM_LR * (m_hat / (jnp.sqrt(v_hat) + ADAM_EPS) + ADAM_WD * w_ref[...])
        go_ref[...] = g_v
        nm_ref[...] = m_n
        nv_ref[...] = v_n

    spec = pl.BlockSpec((tile, cols), lambda i: (i, 0))
    gspec = pl.BlockSpec((tile, cols), lambda i: (g_row0 // tile + i, 0))
    shape = jax.ShapeDtypeStruct((rows, cols), F32)
    return pl.pallas_call(
        body, name=name, grid=(rows // tile,),
        in_specs=[spec, gspec, spec, spec], out_specs=[spec] * 4, out_shape=[shape] * 4,
        compiler_params=_cparams(("arbitrary",)),
    )(w, g, m, v)


def kernel(x, ffn1_norm_g, ffn1_w_gate, ffn1_w_up, ffn1_w_down, mix_norm_g, w_in, b_in, attn_sinks, gmlp_ln_g, gmlp_ln_b, gmlp_w_s, gmlp_b_s, attn_out_norm_g, gmlp_out_norm_g, w_out, b_out, ffn2_norm_g, ffn2_w_gate, ffn2_w_up, ffn2_w_down, final_norm_g, loss_target, m_ffn1_norm_g, m_ffn1_w_gate, m_ffn1_w_up, m_ffn1_w_down, m_mix_norm_g, m_w_in, m_b_in, m_attn_sinks, m_gmlp_ln_g, m_gmlp_ln_b, m_gmlp_w_s, m_gmlp_b_s, m_attn_out_norm_g, m_gmlp_out_norm_g, m_w_out, m_b_out, m_ffn2_norm_g, m_ffn2_w_gate, m_ffn2_w_up, m_ffn2_w_down, m_final_norm_g, v_ffn1_norm_g, v_ffn1_w_gate, v_ffn1_w_up, v_ffn1_w_down, v_mix_norm_g, v_w_in, v_b_in, v_attn_sinks, v_gmlp_ln_g, v_gmlp_ln_b, v_gmlp_w_s, v_gmlp_b_s, v_attn_out_norm_g, v_gmlp_out_norm_g, v_w_out, v_b_out, v_ffn2_norm_g, v_ffn2_w_gate, v_ffn2_w_up, v_ffn2_w_down, v_final_norm_g):
    f_args = dict(locals())
    weights = {n: f_args[n] for n in [nm for nm, _ in SMALL if nm != "loss"] + list(BIG)}
    shapes = {n: weights[n].shape for n in weights}
    shapes["loss"] = ()
    place = jnp.stack([2 * lax.axis_index("x") + lax.axis_index("y"), lax.axis_index("c")]).astype(jnp.int32)

    def with_cols(name, a):
        a2 = a.reshape(a.shape[-2], a.shape[-1])
        return a2.T if BIG_TRANSPOSED[BIG.index(name)] else a2

    def natural(name, a2):
        return (a2.T if BIG_TRANSPOSED[BIG.index(name)] else a2).reshape(shapes[name])

    pack_a, pack_b = _pack_cast(place, [with_cols(n, weights[n]) for n in BIG], name="pack_cast")
    pack_a = _all_gather_pack(pack_a, name="ag_weights")
    p = {n: weights[n].reshape(1, -1) for n in ("ffn1_norm_g", "mix_norm_g", "b_in", "gmlp_ln_g", "gmlp_ln_b",
                                                "attn_out_norm_g", "gmlp_out_norm_g", "b_out", "ffn2_norm_g",
                                                "final_norm_g")}
    p["attn_sinks"] = attn_sinks.reshape(N_Q_HEADS)
    p["gmlp_w_s"] = gmlp_w_s.reshape(GMLP_GROUPS, BLK, BLK)
    p["bs_full"] = jnp.broadcast_to(gmlp_b_s.reshape(GMLP_GROUPS, BLK).T[:, :, None],
                                    (BLK, GMLP_GROUPS, GROUP_DIM)).reshape(BLK, GMLP_W)

    loss_part, dx0, land, gs = _local_step(place, x[0], loss_target[0], p, pack_a, pack_b)

    shard = _rs_share(_rs_reduce(place, land, name="rs_reduce"), name="rs_share")
    gs["gmlp_b_s"] = jnp.sum(gs["gmlp_b_s"].reshape(BLK, GMLP_GROUPS, GROUP_DIM), axis=-1).T
    gs["attn_sinks"] = gs["attn_sinks"][:, 0]
    gs["loss"] = loss_part[0, 0]
    small_sum = _small_all_reduce(_pack_small(gs), name="small_all_reduce")

    grad_w, delta, new_m, new_v = {}, {}, {}, {}
    off = 0
    for n, rows in zip(BIG, BIG_ROWS):
        res = _adamw(with_cols(n, weights[n]), shard, with_cols(n, f_args["m_" + n]), with_cols(n, f_args["v_" + n]),
                     g_row0=off, tile=FF_SH // 2 if rows == FF_SH else 64, name="adamw_" + n)
        grad_w[n], delta[n], new_m[n], new_v[n] = [natural(n, a) for a in res]
        off += rows
    sm = {k: {n: f_args[k + n] for n, _ in SMALL if n != "loss"} for k in ("", "m_", "v_")}
    for k in sm:
        sm[k]["loss"] = jnp.zeros((), F32)
    res = _adamw(_pack_small(sm[""]), small_sum, _pack_small(sm["m_"]), _pack_small(sm["v_"]),
                 g_row0=0, tile=SMALL_ROWS, name="adamw_small")
    small = _unpack_small(res[0], shapes)
    for dst, packed in ((grad_w, res[0]), (delta, res[1]), (new_m, res[2]), (new_v, res[3])):
        dst.update({n: a for n, a in _unpack_small(packed, shapes).items() if n != "loss"})

    order = ('ffn1_norm_g', 'ffn1_w_gate', 'ffn1_w_up', 'ffn1_w_down', 'mix_norm_g', 'w_in', 'b_in', 'attn_sinks',
             'gmlp_ln_g', 'gmlp_ln_b', 'gmlp_w_s', 'gmlp_b_s', 'attn_out_norm_g', 'gmlp_out_norm_g', 'w_out', 'b_out',
             'ffn2_norm_g', 'ffn2_w_gate', 'ffn2_w_up', 'ffn2_w_down', 'final_norm_g')
    return (small["loss"], dx0.reshape(x.shape), *[grad_w[n] for n in order], *[delta[n] for n in order],
            *[new_m[n] for n in order], *[new_v[n] for n in order])
```

```python
import functools

import jax
import jax.numpy as jnp
from jax import lax
from jax.experimental import pallas as pl
from jax.experimental.pallas import tpu as pltpu

F32 = jnp.float32
BF16 = jnp.bfloat16

D_MODEL = 1024
D_FF = 2816
N_CHIPS = 4
FF_SH = D_FF // N_CHIPS
N_Q_HEADS = 8
N_KV_HEADS = 2
REP = N_Q_HEADS // N_KV_HEADS
HEAD_DIM = 64
ATTN_W = 512
KV_W = 128
GMLP_W = 512
GMLP_GROUPS = 8
GROUP_DIM = 64
BLK = 128
MIX_FWD_BLOCKS = 2
MIX_BWD_BLOCKS = 4
IN_W = 1792
IN_SH = IN_W // N_CHIPS
OUT_SH = D_MODEL // N_CHIPS
EPS = 1e-6
FFN_RES = 0.5
ATTN_SCALE = HEAD_DIM ** -0.5

ADAM_LR = 0.001
ADAM_B1 = 0.9
ADAM_B2 = 0.999
ADAM_EPS = 1e-08
ADAM_WD = 0.01
ADAM_STEP = 10

V7X_VMEM_LIMIT = 56 * 1024 * 1024
MESH = pl.DeviceIdType.MESH


def _cparams(sem):
    return pltpu.CompilerParams(dimension_semantics=sem, vmem_limit_bytes=V7X_VMEM_LIMIT)


def _dot(a, b):
    return jnp.dot(a, b, preferred_element_type=F32)


def _dot_nt(a, b):
    return lax.dot_general(a, b, (((1,), (1,)), ((), ())), preferred_element_type=F32)


def _dot_tn(a, b):
    return lax.dot_general(a, b, (((0,), (0,)), ((), ())), preferred_element_type=F32)


def _rms(x, g):
    r = lax.rsqrt(jnp.mean(x * x, axis=-1, keepdims=True) + EPS)
    return x * r * g, r


def _rms_bwd(dh, x, r, g):
    gy = dh * g
    dx = r * gy - x * (r * r * r) * jnp.mean(gy * x, axis=-1, keepdims=True)
    dg = jnp.sum(dh * x * r, axis=0, keepdims=True)
    return dx, dg


def _const(shape):
    nd = len(shape)
    return pl.BlockSpec(shape, lambda *_: (0,) * nd)


def _rows(t, w):
    return pl.BlockSpec((t, w), lambda i: (i, 0))


PACK_ROWS = 7 * FF_SH
HALF_ROWS = PACK_ROWS // 2
FFN_HALF = 3 * FF_SH // 2
MIX_HALF = FF_SH // 2
PACK_A_ROWS = 3 * FF_SH
PACK_B_ROWS = 4 * FF_SH
BIG = ("ffn1_w_gate", "ffn1_w_up", "ffn1_w_down", "ffn2_w_gate", "ffn2_w_up", "ffn2_w_down", "w_in", "w_out")
BIG_ROWS = (FF_SH, FF_SH, FF_SH, FF_SH, FF_SH, FF_SH, IN_SH, OUT_SH)
BIG_TRANSPOSED = (True, True, False, True, True, False, True, False)

SMALL = (("ffn1_norm_g", 1024), ("mix_norm_g", 1024), ("b_in", 1792), ("attn_sinks", 8), ("gmlp_ln_g", 512),
         ("gmlp_ln_b", 512), ("gmlp_w_s", 131072), ("gmlp_b_s", 1024), ("attn_out_norm_g", 512),
         ("gmlp_out_norm_g", 512), ("b_out", 1024), ("ffn2_norm_g", 1024), ("final_norm_g", 1024), ("loss", 1))


def _small_rows(n):
    return -(-n // 1024) * 8


SMALL_USED_ROWS = sum(_small_rows(n) for _, n in SMALL)
SMALL_ROWS = -(-SMALL_USED_ROWS // 16) * 16


def _pack_small(parts):
    out = []
    for name, n in SMALL:
        flat = parts[name].reshape(-1).astype(F32)
        rows = _small_rows(n)
        out.append(jnp.pad(flat, (0, rows * 128 - n)).reshape(rows, 128))
    if SMALL_ROWS > SMALL_USED_ROWS:
        out.append(jnp.zeros((SMALL_ROWS - SMALL_USED_ROWS, 128), F32))
    return jnp.concatenate(out, axis=0)


def _unpack_small(packed, shapes):
    res, off = {}, 0
    for name, n in SMALL:
        rows = _small_rows(n)
        res[name] = packed[off:off + rows].reshape(-1)[:n].reshape(shapes[name])
        off += rows
    return res


def _ffn_tile(x, g, wg_ref, wu_ref, wd_ref, hb_ref, a_ref, b_ref):
    h, _ = _rms(x, g)
    hb = h.astype(BF16)
    hb_ref[...] = hb
    acc = jnp.zeros(x.shape, F32)
    for j in range(N_CHIPS):
        a = _dot_nt(hb, wg_ref[j])
        b = _dot_nt(hb, wu_ref[j])
        a_ref[j] = a
        b_ref[j] = b
        f = (a * jax.nn.sigmoid(a) * b).astype(BF16)
        acc = acc + _dot(f, wd_ref[j])
    return x + FFN_RES * acc


def _ffn_saved_specs(s, tile):
    ab = pl.BlockSpec((N_CHIPS, tile, FF_SH), lambda i: (0, i, 0))
    shape = jax.ShapeDtypeStruct((N_CHIPS, s, FF_SH), F32)
    return [_rows(tile, D_MODEL), ab, ab], [jax.ShapeDtypeStruct((s, D_MODEL), BF16), shape, shape]


def _ffn_weight_specs(k0):
    one = pl.Buffered(1)
    return [pl.BlockSpec((N_CHIPS, FF_SH, D_MODEL), functools.partial(lambda kk, i: (0, kk, 0), k0 + d),
                         pipeline_mode=one) for d in range(3)]


def _mesh_place():
    x, y, c = lax.axis_index("x"), lax.axis_index("y"), lax.axis_index("c")
    others = [(1 - x, y), (x, 1 - y), (1 - x, 1 - y)]
    return x, y, c, others


def _gather_stages(o_ref, send_sems, recv_sems):
    x, y, c, others = _mesh_place()
    me = 2 * x + y
    sibling = (x, y, 1 - c)
    half_rows = o_ref.shape[1] // 2

    def half(slab, core):
        return o_ref.at[slab, pl.ds(pl.multiple_of(core * half_rows, 16), half_rows)]

    def copy(k, rows, to):
        return pltpu.make_async_remote_copy(src_ref=rows, dst_ref=rows, send_sem=send_sems.at[k],
                                            recv_sem=recv_sems.at[k], device_id=to, device_id_type=MESH)

    first = [copy(j, half(me, c), (px, py, c)) for j, (px, py) in enumerate(others)]
    passed = [copy(3 + j, half(2 * px + py, c), sibling) for j, (px, py) in enumerate(others)]

    def start():
        for cp in first:
            cp.start()

    def forward():
        for j, (px, py) in enumerate(others):
            copy(j, half(2 * px + py, c), (px, py, c)).wait_recv()
            passed[j].start()

    def finish():
        for j, (px, py) in enumerate(others):
            copy(3 + j, half(2 * px + py, 1 - c), sibling).wait_recv()
        for cp in first + passed:
            cp.wait_send()

    return start, forward, finish


def _ffn_fwd(x, g, pack, k0, gather, *, tile, name):
    s = x.shape[0]
    nt = s // tile
    forward_at = max(nt - 6, 0)

    def body(x_ref, g_ref, wg_ref, wu_ref, wd_ref, gin_ref, o_ref, hb_ref, a_ref, b_ref, gat_ref, send_sems, recv_sems):
        i = pl.program_id(0)
        start, forward, finish = _gather_stages(gat_ref, send_sems, recv_sems)
        pl.when(i == 0)(start)
        o_ref[...] = _ffn_tile(x_ref[...], g_ref[...], wg_ref, wu_ref, wd_ref, hb_ref, a_ref, b_ref)
        pl.when(i == forward_at)(forward)
        pl.when(i == nt - 1)(finish)

    saved_specs, saved_shapes = _ffn_saved_specs(s, tile)
    hbm = pl.BlockSpec(memory_space=pl.ANY)
    return pl.pallas_call(
        body, name=name, grid=(nt,),
        in_specs=[_rows(tile, D_MODEL), _const((1, D_MODEL))] + _ffn_weight_specs(k0) + [hbm],
        out_specs=[_rows(tile, D_MODEL)] + saved_specs + [hbm],
        out_shape=[jax.ShapeDtypeStruct(x.shape, F32)] + saved_shapes
                  + [jax.ShapeDtypeStruct(gather.shape, gather.dtype)],
        input_output_aliases={5: 4},
        scratch_shapes=[pltpu.SemaphoreType.DMA((6,)), pltpu.SemaphoreType.DMA((6,))],
        compiler_params=_cparams(("arbitrary",)),
    )(x, g, pack, pack, pack, gather)


def _ffn_fwd_loss(x, g, pack, k0, gf, tgt, *, tile, name):
    s = x.shape[0]

    def body(x_ref, g_ref, wg_ref, wu_ref, wd_ref, gf_ref, t_ref, dx_ref, loss_ref, dgf_ref, hb_ref, a_ref, b_ref,
             do_ref):
        @pl.when(pl.program_id(0) == 0)
        def _():
            loss_ref[...] = jnp.zeros_like(loss_ref)
            dgf_ref[...] = jnp.zeros_like(dgf_ref)

        x3 = _ffn_tile(x_ref[...], g_ref[...], wg_ref, wu_ref, wd_ref, hb_ref, a_ref, b_ref)
        gf_v = gf_ref[...]
        out, r = _rms(x3, gf_v)
        diff = out - t_ref[...]
        part = jnp.sum(jnp.sum(diff * diff, axis=-1, keepdims=True), axis=0, keepdims=True)
        loss_ref[...] += jnp.broadcast_to(part * (0.5 / D_MODEL), loss_ref.shape)
        dx, dg = _rms_bwd(diff * (1.0 / D_MODEL), x3, r, gf_v)
        dx_ref[...] = dx
        do_ref[...] = (FFN_RES * dx).astype(BF16)
        dgf_ref[...] += dg

    saved_specs, saved_shapes = _ffn_saved_specs(s, tile)
    return pl.pallas_call(
        body, name=name, grid=(s // tile,),
        in_specs=[_rows(tile, D_MODEL), _const((1, D_MODEL))] + _ffn_weight_specs(k0)
                 + [_const((1, D_MODEL)), _rows(tile, D_MODEL)],
        out_specs=[_rows(tile, D_MODEL), _const((1, 128)), _const((1, D_MODEL))] + saved_specs
                  + [_rows(tile, D_MODEL)],
        out_shape=[jax.ShapeDtypeStruct(x.shape, F32),
                   jax.ShapeDtypeStruct((1, 128), F32),
                   jax.ShapeDtypeStruct((1, D_MODEL), F32)] + saved_shapes
                  + [jax.ShapeDtypeStruct(x.shape, BF16)],
        compiler_params=_cparams(("arbitrary",)),
    )(x, g, pack, pack, pack, gf, tgt)


def _ffn_bwd(place, hb, a, b, do, pack, region, land, mix_grads, *, tile, name):
    s = hb.shape[0]
    nt = s // tile
    land_rows = pl.ds(region * FFN_HALF, FFN_HALF)
    mix_rows = pl.ds(2 * FFN_HALF, MIX_HALF)
    with_mix = mix_grads is not None
    with_land = land is not None
    n_others = 2 * N_CHIPS - 1

    def body(place_ref, hb_ref, a_ref, b_ref, do_ref, wg_ref, wu_ref, wd_ref, *rest):
        rest = list(rest)
        mix_ref = rest.pop(0) if with_mix else None
        if with_land:
            rest.pop(0)
        dhp_ref, land_ref, acc_ref, stage_ref, send_sems, recv_sem, local_sem = rest[:7]
        t, i = pl.program_id(0), pl.program_id(1)
        xi, yi, c = lax.axis_index("x"), lax.axis_index("y"), lax.axis_index("c")
        dev = 4 * xi + 2 * yi + c
        tt = (t + 1) % N_CHIPS
        tx, ty = jnp.bitwise_xor(xi, tt // 2), jnp.bitwise_xor(yi, tt % 2)

        def remote(src, dst, ssem, rsem, to):
            return pltpu.make_async_remote_copy(src_ref=src, dst_ref=dst, send_sem=ssem, recv_sem=rsem,
                                                device_id=to, device_id_type=MESH)

        def stage_half(h):
            return stage_ref.at[pl.ds(pl.multiple_of(h * FFN_HALF, 16), FFN_HALF)]

        if with_mix:
            mix_send, mix_recv, mix_local = rest[7:10]

            @pl.when(jnp.logical_and(t == 0, i == 0))
            def _():
                for chip in range(N_CHIPS):
                    for h in range(2):
                        src = mix_ref.at[chip, pl.ds(h * MIX_HALF, MIX_HALF)]
                        dst = land_ref.at[dev, mix_rows]
                        mine = jnp.logical_and(2 * xi + yi == chip, c == h)

                        @pl.when(mine)
                        def _():
                            pltpu.make_async_copy(src, dst, mix_local).start()

                        @pl.when(jnp.logical_not(mine))
                        def _():
                            remote(src, dst, mix_send, mix_recv, (chip // 2, chip % 2, h)).start()

        @pl.when(i == 0)
        def _():
            acc_ref[...] = jnp.zeros_like(acc_ref)

        hb = hb_ref[...]
        dob = do_ref[...]
        wg_j, wu_j, wd_j = wg_ref[0], wu_ref[0], wd_ref[0]
        a = a_ref[0]
        b = b_ref[0]
        sg = jax.nn.sigmoid(a)
        sa = a * sg
        fb = (sa * b).astype(BF16)
        df = _dot_nt(dob, wd_j)
        dbb = (df * sa).astype(BF16)
        dab = (df * b * (sg + sa * (1.0 - sg))).astype(BF16)
        dhp_ref[0] = (_dot(dab, wg_j) + _dot(dbb, wu_j)).astype(BF16)
        acc_ref[0:FF_SH, :] += _dot_tn(dab, hb)
        acc_ref[FF_SH:2 * FF_SH, :] += _dot_tn(dbb, hb)
        acc_ref[2 * FF_SH:3 * FF_SH, :] += _dot_tn(fb, dob)

        @pl.when(i == nt - 1)
        def _():
            dst = land_ref.at[dev, land_rows]

            @pl.when(t > 0)
            def _():
                for h in range(2):
                    remote(stage_half(h), dst, send_sems.at[h], recv_sem, (tx, ty, h)).wait_send()

            def cast_rows(r, carry):
                rows = pl.ds(pl.multiple_of(r * MIX_HALF, 16), MIX_HALF)
                stage_ref[rows, :] = acc_ref[rows, :].astype(BF16)
                return carry

            lax.fori_loop(0, 3 * FF_SH // MIX_HALF, cast_rows, 0)

            @pl.when(t < N_CHIPS - 1)
            def _():
                for h in range(2):
                    remote(stage_half(h), dst, send_sems.at[h], recv_sem, (tx, ty, h)).start()

            @pl.when(t == N_CHIPS - 1)
            def _():
                own = pltpu.make_async_copy(stage_half(c), dst, local_sem)
                own.start()
                sib = remote(stage_half(1 - c), dst, send_sems.at[0], recv_sem, (xi, yi, 1 - c))
                sib.start()
                sib.wait_send()
                own.wait()
                arrivals = land_ref.at[pl.ds(0, n_others), land_rows]
                remote(arrivals, arrivals, send_sems.at[0], recv_sem, (xi, yi, 1 - c)).wait_recv()
                if with_mix:
                    seven = land_ref.at[pl.ds(0, n_others), mix_rows]
                    both = remote(seven, seven, mix_send, mix_recv, (xi, yi, 1 - c))
                    both.wait_send()
                    both.wait_recv()
                    pltpu.make_async_copy(mix_ref.at[0, pl.ds(0, MIX_HALF)], land_ref.at[dev, mix_rows],
                                          mix_local).wait()

    def wspec(kk):
        return pl.BlockSpec((1, FF_SH, D_MODEL),
                            lambda t, i, pr: (jnp.bitwise_xor(pr[0], (t + 1) % N_CHIPS), kk, 0))

    xspec = pl.BlockSpec((tile, D_MODEL), lambda t, i, pr: (i, 0))
    abspec = pl.BlockSpec((1, tile, FF_SH), lambda t, i, pr: (jnp.bitwise_xor(pr[0], (t + 1) % N_CHIPS), i, 0))
    hbm = pl.BlockSpec(memory_space=pl.ANY)
    in_specs = [xspec, abspec, abspec, xspec, wspec(0), wspec(1), wspec(2)]
    operands = [place, hb, a, b, do, pack, pack, pack]
    scratch = [pltpu.VMEM((3 * FF_SH, D_MODEL), F32), pltpu.VMEM((3 * FF_SH, D_MODEL), BF16),
               pltpu.SemaphoreType.DMA((2,)), pltpu.SemaphoreType.DMA, pltpu.SemaphoreType.DMA]
    if with_mix:
        in_specs.append(hbm)
        operands.append(mix_grads)
        scratch += [pltpu.SemaphoreType.DMA, pltpu.SemaphoreType.DMA, pltpu.SemaphoreType.DMA]
    aliases = {}
    if with_land:
        in_specs.append(hbm)
        operands.append(land)
        aliases = {len(operands) - 1: 1}
    grid_spec = pltpu.PrefetchScalarGridSpec(
        num_scalar_prefetch=1, grid=(N_CHIPS, nt), in_specs=in_specs,
        out_specs=[pl.BlockSpec((1, tile, D_MODEL), lambda t, i, pr: (t, i, 0)), hbm],
        scratch_shapes=scratch)
    return pl.pallas_call(
        body, name=name, grid_spec=grid_spec,
        out_shape=[jax.ShapeDtypeStruct((N_CHIPS, s, D_MODEL), BF16),
                   jax.ShapeDtypeStruct((2 * N_CHIPS, HALF_ROWS, D_MODEL), BF16)],
        input_output_aliases=aliases,
        compiler_params=_cparams(("arbitrary", "arbitrary")),
    )(*operands)


def _mix_grads_pack(dw_in_t, dw_out, *, name):
    def body(a_ref, b_ref, o_ref):
        o_ref[0, 0:IN_SH, :] = a_ref[0].astype(BF16)
        o_ref[0, IN_SH:FF_SH, :] = b_ref[0].astype(BF16)

    return pl.pallas_call(
        body, name=name, grid=(N_CHIPS,),
        in_specs=[pl.BlockSpec((1, IN_SH, D_MODEL), lambda j: (j, 0, 0)),
                  pl.BlockSpec((1, OUT_SH, D_MODEL), lambda j: (j, 0, 0))],
        out_specs=pl.BlockSpec((1, FF_SH, D_MODEL), lambda j: (j, 0, 0)),
        out_shape=jax.ShapeDtypeStruct((N_CHIPS, FF_SH, D_MODEL), BF16),
        compiler_params=_cparams(("arbitrary",)),
    )(dw_in_t.reshape(N_CHIPS, IN_SH, D_MODEL), dw_out.reshape(N_CHIPS, OUT_SH, D_MODEL))


def _share_stages(o_ref, send_sems, recv_sems):
    x, y, c, _ = _mesh_place()

    def rows(k, core):
        if k < 2:
            return o_ref.at[pl.ds(pl.multiple_of(k * 2 * FFN_HALF + core * FFN_HALF, 8), FFN_HALF)]
        return o_ref.at[pl.ds(pl.multiple_of(4 * FFN_HALF + core * MIX_HALF, 8), MIX_HALF)]

    def copy(k, core):
        return pltpu.make_async_remote_copy(src_ref=rows(k, core), dst_ref=rows(k, core), send_sem=send_sems.at[k],
                                            recv_sem=recv_sems.at[k], device_id=(x, y, 1 - c), device_id_type=MESH)

    sends = [copy(k, c) for k in range(3)]

    def start():
        for cp in sends:
            cp.start()

    def finish():
        for k in range(3):
            copy(k, 1 - c).wait_recv()
        for cp in sends:
            cp.wait_send()

    return start, finish


def _norm_bwd(dhp, x, dy, g, shard, *, tile, name):
    s = x.shape[0]
    nt = s // tile

    def body(dhp_ref, x_ref, dy_ref, g_ref, sh_in_ref, dx_ref, dg_ref, sh_ref, send_sems, recv_sems):
        i = pl.program_id(0)
        start, finish = _share_stages(sh_ref, send_sems, recv_sems)

        @pl.when(i == 0)
        def _():
            dg_ref[...] = jnp.zeros_like(dg_ref)
            start()

        dh = ((dhp_ref[0].astype(F32) + dhp_ref[1].astype(F32))
              + (dhp_ref[2].astype(F32) + dhp_ref[3].astype(F32)))
        x_v = x_ref[...]
        r = lax.rsqrt(jnp.mean(x_v * x_v, axis=-1, keepdims=True) + EPS)
        dx, dg = _rms_bwd(dh, x_v, r, g_ref[...])
        dx_ref[...] = dy_ref[...] + dx
        dg_ref[...] += dg
        pl.when(i == nt - 1)(finish)

    hbm = pl.BlockSpec(memory_space=pl.ANY)
    return pl.pallas_call(
        body, name=name, grid=(nt,),
        in_specs=[pl.BlockSpec((N_CHIPS, tile, D_MODEL), lambda i: (0, i, 0)),
                  _rows(tile, D_MODEL), _rows(tile, D_MODEL), _const((1, D_MODEL)), hbm],
        out_specs=[_rows(tile, D_MODEL), _const((1, D_MODEL)), hbm],
        out_shape=[jax.ShapeDtypeStruct(x.shape, F32), jax.ShapeDtypeStruct((1, D_MODEL), F32),
                   jax.ShapeDtypeStruct(shard.shape, shard.dtype)],
        input_output_aliases={4: 2},
        scratch_shapes=[pltpu.SemaphoreType.DMA((3,)), pltpu.SemaphoreType.DMA((3,))],
        compiler_params=_cparams(("arbitrary",)),
    )(dhp, x, dy, g, shard)


def _mix_in_bwd(x, dy, dq, dk, dv, dz, g, w_in_t, *, tile, name):
    s = x.shape[0]

    def body(x_ref, dy_ref, dq_ref, dk_ref, dv_ref, dz_ref, g_ref, w_ref, dx_ref, dw_ref, db_ref, dg_ref, do_ref):
        @pl.when(pl.program_id(0) == 0)
        def _():
            dw_ref[...] = jnp.zeros_like(dw_ref)
            db_ref[...] = jnp.zeros_like(db_ref)
            dg_ref[...] = jnp.zeros_like(dg_ref)

        dproj = jnp.concatenate([dq_ref[...], dk_ref[...], dv_ref[...], dz_ref[...]], axis=-1)
        db_ref[...] += jnp.sum(dproj, axis=0, keepdims=True)
        dpb = dproj.astype(BF16)
        x_v = x_ref[...]
        g_v = g_ref[...]
        h, r = _rms(x_v, g_v)
        dw_ref[...] += _dot_tn(dpb, h.astype(BF16))
        dh = _dot(dpb, w_ref[...])
        dxn, dg = _rms_bwd(dh, x_v, r, g_v)
        dx = dy_ref[...] + dxn
        dx_ref[...] = dx
        do_ref[...] = (FFN_RES * dx).astype(BF16)
        dg_ref[...] += dg

    return pl.pallas_call(
        body, name=name, grid=(s // tile,),
        in_specs=[_rows(tile, D_MODEL), _rows(tile, D_MODEL), _rows(tile, ATTN_W), _rows(tile, KV_W),
                  _rows(tile, KV_W), _rows(tile, 2 * GMLP_W), _const((1, D_MODEL)), _const((IN_W, D_MODEL))],
        out_specs=[_rows(tile, D_MODEL), _const((IN_W, D_MODEL)), _const((1, IN_W)), _const((1, D_MODEL)),
                   _rows(tile, D_MODEL)],
        out_shape=[jax.ShapeDtypeStruct(x.shape, F32), jax.ShapeDtypeStruct((IN_W, D_MODEL), F32),
                   jax.ShapeDtypeStruct((1, IN_W), F32), jax.ShapeDtypeStruct((1, D_MODEL), F32),
                   jax.ShapeDtypeStruct(x.shape, BF16)],
        compiler_params=_cparams(("arbitrary",)),
    )(x, dy, dq, dk, dv, dz, g, w_in_t)


_GELU_C = 0.7978845608028654
_GELU_A = 0.044715


def _gelu_tanh(x):
    x2 = x * x
    return jnp.tanh(_GELU_C * (x + _GELU_A * (x2 * x))), x2


def _band(ref, i):
    prev = jnp.maximum(i - 1, 0)
    return jnp.concatenate([ref[pl.ds(pl.multiple_of(prev * BLK, BLK), BLK), :],
                            ref[pl.ds(pl.multiple_of(i * BLK, BLK), BLK), :]], axis=0)


def _key_in_block():
    return lax.broadcasted_iota(jnp.int32, (BLK, BLK), 0) <= lax.broadcasted_iota(jnp.int32, (BLK, BLK), 1)


def _fold(band, own):
    return jnp.where(own, band[BLK:], band[:BLK])


def _unfold(a, own):
    zero = jnp.zeros_like(a)
    return jnp.concatenate([jnp.where(own, zero, a), jnp.where(own, a, zero)], axis=0).astype(BF16)


def _attn_fwd(q, kb, vb, i, sink_ref):
    own = _key_in_block()
    outs, saved = [], []
    for h in range(N_Q_HEADS):
        cols = slice((h // REP) * HEAD_DIM, (h // REP + 1) * HEAD_DIM)
        s2 = _dot_nt(kb[:, cols], q[:, h * HEAD_DIM:(h + 1) * HEAD_DIM])
        sc = jnp.where(own, s2[BLK:], jnp.where(i > 0, s2[:BLK], -jnp.inf))
        sink = sink_ref[h]
        m = jnp.maximum(jnp.max(sc, axis=0, keepdims=True), sink)
        p = jnp.exp(sc - m)
        es = jnp.exp(sink - m)
        inv = 1.0 / (jnp.sum(p, axis=0, keepdims=True) + es)
        pn = p * inv
        band = _unfold(pn, own)
        outs.append(_dot_tn(band, vb[:, cols]))
        saved.append((pn, band, es * inv))
    return jnp.concatenate(outs, axis=-1), saved


def _tril_mask():
    t = lax.broadcasted_iota(jnp.int32, (BLK, BLK), 0)
    s_ = lax.broadcasted_iota(jnp.int32, (BLK, BLK), 1)
    return s_ <= t


def _gmlp_fwd_parts(zg, lng, lnb, ws_ref, bs_full):
    th, zg2 = _gelu_tanh(zg)
    z = 0.5 * zg * (1.0 + th)
    u = z[:, :GMLP_W]
    zv = z[:, GMLP_W:]
    mu = jnp.mean(zv, axis=-1, keepdims=True)
    zc = zv - mu
    rstd = lax.rsqrt(jnp.mean(zc * zc, axis=-1, keepdims=True) + EPS)
    xh = zc * rstd
    vvb = (xh * lng + lnb).astype(BF16)
    tril = _tril_mask()
    wms, parts = [], []
    for gi in range(GMLP_GROUPS):
        wm = jnp.where(tril, ws_ref[gi], 0.0).astype(BF16)
        wms.append(wm)
        parts.append(_dot(wm, vvb[:, gi * GROUP_DIM:(gi + 1) * GROUP_DIM]))
    mixed = jnp.concatenate(parts, axis=-1) + bs_full
    gelu_grad = 0.5 * (1.0 + th) + 0.5 * zg * (1.0 - th * th) * (_GELU_C * (1.0 + 3.0 * _GELU_A * zg2))
    return u, xh, rstd, vvb, wms, mixed, gelu_grad


def _mixer_fwd(x1, g, w_in_t, b_in, sinks, lng, lnb, w_s, bs_full, gao, ggo, w_out, b_out, *, name):
    s = x1.shape[0]
    nb = min(MIX_FWD_BLOCKS, s // BLK)
    step_rows = nb * BLK
    last = s // step_rows - 1

    def tile_of(i, lag):
        return jnp.clip(i - lag, 0, last)

    def body(sink_ref, xa_ref, xc_ref, g_ref, wi_ref, bi_ref, lng_ref, lnb_ref, ws_ref, bs_ref, gao_ref, ggo_ref,
             wo_ref, bo_ref, q_ref, k_ref, v_ref, z_ref, y_ref, o_ref, qs_ref, zs_ref, ys_ref):
        i = pl.program_id(0)

        @pl.when(i == 0)
        def _():
            for ref in (k_ref, v_ref, qs_ref, zs_ref, ys_ref):
                ref[...] = jnp.zeros_like(ref)

        slot_a, slot_b, slot_c = i % 2, (i + 1) % 2, i % 2

        o_ref[...] = xc_ref[...] + (_dot(ys_ref[slot_c], wo_ref[...]) + bo_ref[...])

        tile_b = tile_of(i, 1)
        for b in range(nb):
            blk = tile_b * nb + b
            rows = slice(b * BLK, (b + 1) * BLK)
            y_attn, _ = _attn_fwd(qs_ref[slot_b, rows, :], _band(k_ref, blk), _band(v_ref, blk), blk, sink_ref)
            u, _, _, _, _, mixed, _ = _gmlp_fwd_parts(zs_ref[slot_b, rows, :], lng_ref[...], lnb_ref[...], ws_ref,
                                                      bs_ref[...])
            ya, _ = _rms(y_attn, gao_ref[...])
            yg, _ = _rms(u * mixed, ggo_ref[...])
            y_blk = jnp.concatenate([ya, yg], axis=-1).astype(BF16)
            y_ref[rows, :] = y_blk
            ys_ref[slot_b, rows, :] = y_blk

        h, _ = _rms(xa_ref[...], g_ref[...])
        proj = _dot_nt(h.astype(BF16), wi_ref[...]) + bi_ref[...]
        q_t = (proj[:, :ATTN_W] * ATTN_SCALE).astype(BF16)
        z_t = proj[:, ATTN_W + 2 * KV_W:]
        here = pl.ds(pl.multiple_of(tile_of(i, 0) * step_rows, step_rows), step_rows)
        q_ref[...] = q_t
        z_ref[...] = z_t
        qs_ref[slot_a] = q_t
        zs_ref[slot_a] = z_t
        k_ref[here, :] = proj[:, ATTN_W:ATTN_W + KV_W].astype(BF16)
        v_ref[here, :] = proj[:, ATTN_W + KV_W:ATTN_W + 2 * KV_W].astype(BF16)

    def lagged(width, lag):
        return pl.BlockSpec((step_rows, width), lambda i: (tile_of(i, lag), 0))

    return pl.pallas_call(
        body, name=name, grid=(last + 3,),
        in_specs=[pl.BlockSpec(memory_space=pltpu.SMEM),
                  lagged(D_MODEL, 0), lagged(D_MODEL, 2), _const((1, D_MODEL)), _const((IN_W, D_MODEL)),
                  _const((1, IN_W)), _const((1, GMLP_W)), _const((1, GMLP_W)), _const((GMLP_GROUPS, BLK, BLK)),
                  _const((BLK, GMLP_W)), _const((1, ATTN_W)), _const((1, GMLP_W)), _const((D_MODEL, D_MODEL)),
                  _const((1, D_MODEL))],
        out_specs=[lagged(ATTN_W, 0), _const((s, KV_W)), _const((s, KV_W)), lagged(2 * GMLP_W, 0),
                   lagged(D_MODEL, 1), lagged(D_MODEL, 2)],
        out_shape=[jax.ShapeDtypeStruct((s, ATTN_W), BF16), jax.ShapeDtypeStruct((s, KV_W), BF16),
                   jax.ShapeDtypeStruct((s, KV_W), BF16), jax.ShapeDtypeStruct((s, 2 * GMLP_W), F32),
                   jax.ShapeDtypeStruct((s, D_MODEL), BF16), jax.ShapeDtypeStruct((s, D_MODEL), F32)],
        scratch_shapes=[pltpu.VMEM((2, step_rows, ATTN_W), BF16), pltpu.VMEM((2, step_rows, 2 * GMLP_W), F32),
                        pltpu.VMEM((2, step_rows, D_MODEL), BF16)],
        compiler_params=_cparams(("arbitrary",)),
    )(sinks, x1, x1, g, w_in_t, b_in, lng, lnb, w_s, bs_full, gao, ggo, w_out, b_out)


def _norm_bwd_mix_out(dhp, x, dy, g, yb, w_out, *, tile, name):
    s = x.shape[0]

    def body(dhp_ref, x_ref, dy_ref, g_ref, y_ref, w_ref, dx_ref, dg_ref, dyy_ref, dw_ref, db_ref):
        @pl.when(pl.program_id(0) == 0)
        def _():
            dg_ref[...] = jnp.zeros_like(dg_ref)
            dw_ref[...] = jnp.zeros_like(dw_ref)
            db_ref[...] = jnp.zeros_like(db_ref)

        dh = ((dhp_ref[0].astype(F32) + dhp_ref[1].astype(F32))
              + (dhp_ref[2].astype(F32) + dhp_ref[3].astype(F32)))
        x_v = x_ref[...]
        r = lax.rsqrt(jnp.mean(x_v * x_v, axis=-1, keepdims=True) + EPS)
        dxn, dg = _rms_bwd(dh, x_v, r, g_ref[...])
        dx = dy_ref[...] + dxn
        dx_ref[...] = dx
        dg_ref[...] += dg
        dxb = dx.astype(BF16)
        db_ref[...] += jnp.sum(dx, axis=0, keepdims=True)
        dw_ref[...] += _dot_tn(y_ref[...], dxb)
        dyy_ref[...] = _dot_nt(dxb, w_ref[...])

    return pl.pallas_call(
        body, name=name, grid=(s // tile,),
        in_specs=[pl.BlockSpec((N_CHIPS, tile, D_MODEL), lambda i: (0, i, 0)),
                  _rows(tile, D_MODEL), _rows(tile, D_MODEL), _const((1, D_MODEL)), _rows(tile, D_MODEL),
                  _const((D_MODEL, D_MODEL))],
        out_specs=[_rows(tile, D_MODEL), _const((1, D_MODEL)), _rows(tile, D_MODEL), _const((D_MODEL, D_MODEL)),
                   _const((1, D_MODEL))],
        out_shape=[jax.ShapeDtypeStruct(x.shape, F32), jax.ShapeDtypeStruct((1, D_MODEL), F32),
                   jax.ShapeDtypeStruct(x.shape, F32), jax.ShapeDtypeStruct((D_MODEL, D_MODEL), F32),
                   jax.ShapeDtypeStruct((1, D_MODEL), F32)],
        compiler_params=_cparams(("arbitrary",)),
    )(dhp, x, dy, g, yb, w_out)


def _mix_core_bwd(dyy, q, k, v, zg, sinks, lng, lnb, w_s, bs_full, gao, ggo, *, name):
    s = dyy.shape[0]
    nb = min(MIX_BWD_BLOCKS, s // BLK)
    nsteps = s // (nb * BLK)

    def body(*refs):
        accumulators = refs[13:15] + refs[16:]

        @pl.when(pl.program_id(0) == 0)
        def _():
            for ref in accumulators:
                ref[...] = jnp.zeros_like(ref)

        for b in range(nb):
            one_block(pl.program_id(0) * nb + b, slice(b * BLK, (b + 1) * BLK), *refs)

        @pl.when(pl.program_id(0) == nsteps - 1)
        def _():
            tril = _tril_mask()
            for gi in range(GMLP_GROUPS):
                refs[20][gi] = jnp.where(tril, refs[20][gi], 0.0)

    def one_block(i, rows, sink_ref, dyy_ref, q_ref, k_ref, v_ref, z_ref, lng_ref, lnb_ref, ws_ref, bs_ref, gao_ref,
                  ggo_ref, dq_ref, dk_ref, dv_ref, dz_ref, dgao_ref, dggo_ref, dlng_ref, dlnb_ref, dws_ref, dms_ref,
                  dsk_ref):
        q_v = q_ref[rows, :]
        kb = _band(k_ref, i)
        vb = _band(v_ref, i)
        lng_v = lng_ref[...]
        gao_v = gao_ref[...]
        ggo_v = ggo_ref[...]

        y_attn, probs = _attn_fwd(q_v, kb, vb, i, sink_ref)
        u, xh, rstd, vvb, wms, mixed, gelu_grad = _gmlp_fwd_parts(z_ref[rows, :], lng_v, lnb_ref[...], ws_ref,
                                                                  bs_ref[...])
        y_gmlp = u * mixed
        ra = lax.rsqrt(jnp.mean(y_attn * y_attn, axis=-1, keepdims=True) + EPS)
        rg = lax.rsqrt(jnp.mean(y_gmlp * y_gmlp, axis=-1, keepdims=True) + EPS)

        dyy = dyy_ref[rows, :]
        d_attn, dgao = _rms_bwd(dyy[:, :ATTN_W], y_attn, ra, gao_v)
        d_gmlp, dggo = _rms_bwd(dyy[:, ATTN_W:], y_gmlp, rg, ggo_v)
        dgao_ref[...] += dgao
        dggo_ref[...] += dggo

        du = d_gmlp * mixed
        dmixed = d_gmlp * u
        dms_ref[...] += dmixed
        dmb = dmixed.astype(BF16)
        dvv_parts = []
        for gi in range(GMLP_GROUPS):
            sl = slice(gi * GROUP_DIM, (gi + 1) * GROUP_DIM)
            dws_ref[gi] += _dot_nt(dmb[:, sl], vvb[:, sl])
            dvv_parts.append(_dot_tn(wms[gi], dmb[:, sl]))
        dvv = jnp.concatenate(dvv_parts, axis=-1)
        dlng_ref[...] += jnp.sum(dvv * xh, axis=0, keepdims=True)
        dlnb_ref[...] += jnp.sum(dvv, axis=0, keepdims=True)
        dxh = dvv * lng_v
        dzv = rstd * (dxh - jnp.mean(dxh, axis=-1, keepdims=True)
                      - xh * jnp.mean(dxh * xh, axis=-1, keepdims=True))
        dz_ref[rows, :] = jnp.concatenate([du, dzv], axis=-1) * gelu_grad

        dab = d_attn.astype(BF16)
        own = _key_in_block()
        dq_parts = []
        dk_parts = []
        dv_parts = []
        for gi in range(N_KV_HEADS):
            cols = slice(gi * HEAD_DIM, (gi + 1) * HEAD_DIM)
            kg, vg = kb[:, cols], vb[:, cols]
            dkg = jnp.zeros((2 * BLK, HEAD_DIM), F32)
            dvg = jnp.zeros((2 * BLK, HEAD_DIM), F32)
            for rr in range(REP):
                h = gi * REP + rr
                hs = slice(h * HEAD_DIM, (h + 1) * HEAD_DIM)
                qh, doh = q_v[:, hs], dab[:, hs]
                pn, band, psink = probs[h]
                dp = _fold(_dot_nt(vg, doh), own)
                delta = jnp.sum(pn * dp, axis=0, keepdims=True)
                ds2 = _unfold(pn * (dp - delta), own)
                dsink = jnp.sum(-psink * delta, axis=-1, keepdims=True)
                dsk_ref[pl.ds(h, 1), :] += jnp.broadcast_to(dsink, (1, 128))
                dq_parts.append(_dot_tn(ds2, kg) * ATTN_SCALE)
                dkg = dkg + _dot(ds2, qh)
                dvg = dvg + _dot(band, doh)
            dk_parts.append(dkg)
            dv_parts.append(dvg)
        dq_ref[rows, :] = jnp.concatenate(dq_parts, axis=-1)
        dkb = jnp.concatenate(dk_parts, axis=-1)
        dvb = jnp.concatenate(dv_parts, axis=-1)
        prev = pl.ds(pl.multiple_of(jnp.maximum(i - 1, 0) * BLK, BLK), BLK)
        cur = pl.ds(pl.multiple_of(i * BLK, BLK), BLK)
        dk_ref[prev, :] += dkb[:BLK]
        dv_ref[prev, :] += dvb[:BLK]
        dk_ref[cur, :] += dkb[BLK:]
        dv_ref[cur, :] += dvb[BLK:]

    return pl.pallas_call(
        body, name=name, grid=(nsteps,),
        in_specs=[pl.BlockSpec(memory_space=pltpu.SMEM),
                  _rows(nb * BLK, D_MODEL), _rows(nb * BLK, ATTN_W), _const((s, KV_W)), _const((s, KV_W)),
                  _rows(nb * BLK, 2 * GMLP_W), _const((1, GMLP_W)), _const((1, GMLP_W)),
                  _const((GMLP_GROUPS, BLK, BLK)), _const((BLK, GMLP_W)), _const((1, ATTN_W)), _const((1, GMLP_W))],
        out_specs=[_rows(nb * BLK, ATTN_W), _const((s, KV_W)), _const((s, KV_W)), _rows(nb * BLK, 2 * GMLP_W),
                   _const((1, ATTN_W)), _const((1, GMLP_W)),
                   _const((1, GMLP_W)), _const((1, GMLP_W)), _const((GMLP_GROUPS, BLK, BLK)),
                   _const((BLK, GMLP_W)), _const((N_Q_HEADS, 128))],
        out_shape=[jax.ShapeDtypeStruct((s, ATTN_W), F32), jax.ShapeDtypeStruct((s, KV_W), F32),
                   jax.ShapeDtypeStruct((s, KV_W), F32), jax.ShapeDtypeStruct((s, 2 * GMLP_W), F32),
                   jax.ShapeDtypeStruct((1, ATTN_W), F32), jax.ShapeDtypeStruct((1, GMLP_W), F32),
                   jax.ShapeDtypeStruct((1, GMLP_W), F32), jax.ShapeDtypeStruct((1, GMLP_W), F32),
                   jax.ShapeDtypeStruct((GMLP_GROUPS, BLK, BLK), F32), jax.ShapeDtypeStruct((BLK, GMLP_W), F32),
                   jax.ShapeDtypeStruct((N_Q_HEADS, 128), F32)],
        compiler_params=_cparams(("arbitrary",)),
    )(sinks, dyy, q, k, v, zg, lng, lnb, w_s, bs_full, gao, ggo)


def _local_step(place, x, tgt, p, pack_a, pack_b, *, tile=512, fwd_tile=256, bwd_tile=512, norm_tile=512):
    g = {}
    tile, fwd_tile, bwd_tile, norm_tile = (min(t_, x.shape[0]) for t_ in (tile, fwd_tile, bwd_tile, norm_tile))
    x1, hb1, a1, b1, pack_b = _ffn_fwd(x, p["ffn1_norm_g"], pack_a, 0, pack_b, tile=fwd_tile, name="ffn1_fwd")
    mix_rows = pack_b[:, 3 * FF_SH:, :]
    w_in_t = mix_rows[:, :IN_SH, :].reshape(IN_W, D_MODEL)
    w_out = mix_rows[:, IN_SH:, :].reshape(D_MODEL, D_MODEL)
    q, k, v, zg, yb, x2 = _mixer_fwd(
        x1, p["mix_norm_g"], w_in_t, p["b_in"], p["attn_sinks"], p["gmlp_ln_g"], p["gmlp_ln_b"], p["gmlp_w_s"],
        p["bs_full"], p["attn_out_norm_g"], p["gmlp_out_norm_g"], w_out, p["b_out"], name="mixer_fwd")
    mix_args = (q, k, v, zg, p["attn_sinks"], p["gmlp_ln_g"], p["gmlp_ln_b"], p["gmlp_w_s"], p["bs_full"],
                p["attn_out_norm_g"], p["gmlp_out_norm_g"])
    dx3, loss, g["final_norm_g"], hb2, a2, b2, do3 = _ffn_fwd_loss(
        x2, p["ffn2_norm_g"], pack_b, 0, p["final_norm_g"], tgt, tile=fwd_tile, name="ffn2_fwd_loss")

    dhp, land = _ffn_bwd(place, hb2, a2, b2, do3, pack_b, 1, None, None, tile=bwd_tile, name="ffn2_bwd")
    dx2, g["ffn2_norm_g"], dyy, dw_out, g["b_out"] = _norm_bwd_mix_out(
        dhp, x2, dx3, p["ffn2_norm_g"], yb, w_out, tile=norm_tile, name="ffn2_norm_bwd")

    (dq, dk, dv, dz, g["attn_out_norm_g"], g["gmlp_out_norm_g"], g["gmlp_ln_g"],
     g["gmlp_ln_b"], g["gmlp_w_s"], dmix_sum, dsinks) = _mix_core_bwd(dyy, *mix_args, name="mix_core_bwd")
    g["gmlp_b_s"] = dmix_sum
    g["attn_sinks"] = dsinks
    dx1, dw_in_t, g["b_in"], g["mix_norm_g"], do1 = _mix_in_bwd(
        x1, dx2, dq, dk, dv, dz, p["mix_norm_g"], w_in_t, tile=tile, name="mix_in_bwd")
    mix_grads = _mix_grads_pack(dw_in_t, dw_out, name="mix_grads_pack")

    dhp1, land = _ffn_bwd(place, hb1, a1, b1, do1, pack_a, 0, land, mix_grads, tile=bwd_tile, name="ffn1_bwd")
    shard = _rs_reduce(place, land, name="rs_reduce")
    dx0, g["ffn1_norm_g"], shard = _norm_bwd(dhp1, x, dx1, p["ffn1_norm_g"], shard, tile=norm_tile,
                                             name="ffn1_norm_bwd")
    return loss, dx0, shard, g


def _pack_cast(place, parts, *, name):
    def body(place_ref, *refs):
        oa_ref, ob_ref = refs[-2], refs[-1]
        off = 0
        for k, (ref, rows) in enumerate(zip(refs[:-2], BIG_ROWS)):
            if k == 3:
                off = 0
            (oa_ref if k < 3 else ob_ref)[0, off:off + rows, :] = ref[...].astype(BF16)
            off += rows

    one = pl.Buffered(1)

    def slab(rows):
        return pl.BlockSpec((1, rows, D_MODEL), lambda i, pr: (pr[0], 0, 0), pipeline_mode=one)

    grid_spec = pltpu.PrefetchScalarGridSpec(
        num_scalar_prefetch=1, grid=(1,),
        in_specs=[pl.BlockSpec((rows, D_MODEL), lambda i, pr: (0, 0), pipeline_mode=one) for rows in BIG_ROWS],
        out_specs=[slab(PACK_A_ROWS), slab(PACK_B_ROWS)])
    return pl.pallas_call(
        body, name=name, grid_spec=grid_spec,
        out_shape=[jax.ShapeDtypeStruct((N_CHIPS, PACK_A_ROWS, D_MODEL), BF16),
                   jax.ShapeDtypeStruct((N_CHIPS, PACK_B_ROWS, D_MODEL), BF16)],
        compiler_params=_cparams(("arbitrary",)),
    )(place, *parts)


def _all_gather_pack(pack, *, name):
    def body(p_ref, o_ref, send_sems, recv_sems):
        start, forward, finish = _gather_stages(o_ref, send_sems, recv_sems)
        start()
        forward()
        finish()

    return pl.pallas_call(
        body, name=name,
        in_specs=[pl.BlockSpec(memory_space=pl.ANY)],
        out_specs=pl.BlockSpec(memory_space=pl.ANY),
        out_shape=jax.ShapeDtypeStruct(pack.shape, pack.dtype),
        input_output_aliases={0: 0},
        scratch_shapes=[pltpu.SemaphoreType.DMA((6,)), pltpu.SemaphoreType.DMA((6,))],
    )(pack)


def _shard_tile(i, c):
    return jnp.where(i < 3, 3 * c + i, jnp.where(i < 6, 3 + 3 * c + i, 12 + c))


def _rs_reduce(place, land, *, name):
    def body(place_ref, l_ref, o_ref):
        acc = l_ref[0].astype(F32)
        for d in range(1, 2 * N_CHIPS):
            acc = acc + l_ref[d].astype(F32)
        o_ref[...] = acc

    grid_spec = pltpu.PrefetchScalarGridSpec(
        num_scalar_prefetch=1, grid=(HALF_ROWS // MIX_HALF,),
        in_specs=[pl.BlockSpec((2 * N_CHIPS, MIX_HALF, D_MODEL), lambda i, pr: (0, i, 0))],
        out_specs=pl.BlockSpec((MIX_HALF, D_MODEL), lambda i, pr: (_shard_tile(i, pr[1]), 0)))
    return pl.pallas_call(
        body, name=name, grid_spec=grid_spec,
        out_shape=jax.ShapeDtypeStruct((PACK_ROWS, D_MODEL), F32),
        compiler_params=_cparams(("arbitrary",)),
    )(place, land)


def _small_all_reduce(packed, *, name):
    rows = packed.shape[0]
    half = rows // 2

    def body(p_ref, o_ref, sib_ref, slots_ref, send_sems, recv_sems):
        x, y, c, others = _mesh_place()
        me = 2 * x + y
        sibling = (x, y, 1 - c)

        def half_of(core):
            return pl.ds(pl.multiple_of(core * half, 8), half)

        def remote(k, src, dst, to):
            return pltpu.make_async_remote_copy(src_ref=src, dst_ref=dst, send_sem=send_sems.at[k],
                                                recv_sem=recv_sems.at[k], device_id=to, device_id_type=MESH)

        sib = remote(0, p_ref.at[half_of(1 - c)], sib_ref, sibling)
        sib.start()
        sib.wait()
        slots_ref[me] = p_ref[half_of(c), :] + sib_ref[...]
        sends = [remote(1 + j, slots_ref.at[me], slots_ref.at[me], (px, py, c)) for j, (px, py) in enumerate(others)]
        for cp in sends:
            cp.start()
        for j, (px, py) in enumerate(others):
            slab = slots_ref.at[2 * px + py]
            remote(1 + j, slab, slab, (px, py, c)).wait_recv()
        for cp in sends:
            cp.wait_send()
        o_ref[half_of(c), :] = (slots_ref[0] + slots_ref[1]) + (slots_ref[2] + slots_ref[3])
        back = remote(4, o_ref.at[half_of(c)], o_ref.at[half_of(c)], sibling)
        back.start()
        remote(4, o_ref.at[half_of(1 - c)], o_ref.at[half_of(1 - c)], sibling).wait_recv()
        back.wait_send()

    vm = pl.BlockSpec(memory_space=pltpu.VMEM)
    return pl.pallas_call(
        body, name=name, in_specs=[vm], out_specs=vm,
        out_shape=jax.ShapeDtypeStruct((rows, 128), F32),
        scratch_shapes=[pltpu.VMEM((half, 128), F32), pltpu.VMEM((N_CHIPS, half, 128), F32),
                        pltpu.SemaphoreType.DMA((5,)), pltpu.SemaphoreType.DMA((5,))],
    )(packed)


def _adamw(w, g, m, v, *, g_row0, tile, name):
    rows, cols = w.shape
    assert g_row0 % tile == 0 and rows % tile == 0

    def body(w_ref, g_ref, m_ref, v_ref, go_ref, d_ref, nm_ref, nv_ref):
        g_v = g_ref[...]
        m_n = ADAM_B1 * m_ref[...] + (1.0 - ADAM_B1) * g_v
        v_n = ADAM_B2 * v_ref[...] + (1.0 - ADAM_B2) * (g_v * g_v)
        m_hat = m_n / (1.0 - ADAM_B1 ** ADAM_STEP)
        v_hat = v_n / (1.0 - ADAM_B2 ** ADAM_STEP)
        d_ref[...] = -ADAM_LR * (m_hat / (jnp.sqrt(v_hat) + ADAM_EPS) + ADAM_WD * w_ref[...])
        go_ref[...] = g_v
        nm_ref[...] = m_n
        nv_ref[...] = v_n

    spec = pl.BlockSpec((tile, cols), lambda i: (i, 0))
    gspec = pl.BlockSpec((tile, cols), lambda i: (g_row0 // tile + i, 0))
    shape = jax.ShapeDtypeStruct((rows, cols), F32)
    return pl.pallas_call(
        body, name=name, grid=(rows // tile,),
        in_specs=[spec, gspec, spec, spec], out_specs=[spec] * 4, out_shape=[shape] * 4,
        compiler_params=_cparams(("arbitrary",)),
    )(w, g, m, v)


def kernel(x, ffn1_norm_g, ffn1_w_gate, ffn1_w_up, ffn1_w_down, mix_norm_g, w_in, b_in, attn_sinks, gmlp_ln_g, gmlp_ln_b, gmlp_w_s, gmlp_b_s, attn_out_norm_g, gmlp_out_norm_g, w_out, b_out, ffn2_norm_g, ffn2_w_gate, ffn2_w_up, ffn2_w_down, final_norm_g, loss_target, m_ffn1_norm_g, m_ffn1_w_gate, m_ffn1_w_up, m_ffn1_w_down, m_mix_norm_g, m_w_in, m_b_in, m_attn_sinks, m_gmlp_ln_g, m_gmlp_ln_b, m_gmlp_w_s, m_gmlp_b_s, m_attn_out_norm_g, m_gmlp_out_norm_g, m_w_out, m_b_out, m_ffn2_norm_g, m_ffn2_w_gate, m_ffn2_w_up, m_ffn2_w_down, m_final_norm_g, v_ffn1_norm_g, v_ffn1_w_gate, v_ffn1_w_up, v_ffn1_w_down, v_mix_norm_g, v_w_in, v_b_in, v_attn_sinks, v_gmlp_ln_g, v_gmlp_ln_b, v_gmlp_w_s, v_gmlp_b_s, v_attn_out_norm_g, v_gmlp_out_norm_g, v_w_out, v_b_out, v_ffn2_norm_g, v_ffn2_w_gate, v_ffn2_w_up, v_ffn2_w_down, v_final_norm_g):
    f_args = dict(locals())
    weights = {n: f_args[n] for n in [nm for nm, _ in SMALL if nm != "loss"] + list(BIG)}
    shapes = {n: weights[n].shape for n in weights}
    shapes["loss"] = ()
    place = jnp.stack([2 * lax.axis_index("x") + lax.axis_index("y"), lax.axis_index("c")]).astype(jnp.int32)

    def with_cols(name, a):
        a2 = a.reshape(a.shape[-2], a.shape[-1])
        return a2.T if BIG_TRANSPOSED[BIG.index(name)] else a2

    def natural(name, a2):
        return (a2.T if BIG_TRANSPOSED[BIG.index(name)] else a2).reshape(shapes[name])

    pack_a, pack_b = _pack_cast(place, [with_cols(n, weights[n]) for n in BIG], name="pack_cast")
    pack_a = _all_gather_pack(pack_a, name="ag_weights")
    p = {n: weights[n].reshape(1, -1) for n in ("ffn1_norm_g", "mix_norm_g", "b_in", "gmlp_ln_g", "gmlp_ln_b",
                                                "attn_out_norm_g", "gmlp_out_norm_g", "b_out", "ffn2_norm_g",
                                                "final_norm_g")}
    p["attn_sinks"] = attn_sinks.reshape(N_Q_HEADS)
    p["gmlp_w_s"] = gmlp_w_s.reshape(GMLP_GROUPS, BLK, BLK)
    p["bs_full"] = jnp.broadcast_to(gmlp_b_s.reshape(GMLP_GROUPS, BLK).T[:, :, None],
                                    (BLK, GMLP_GROUPS, GROUP_DIM)).reshape(BLK, GMLP_W)

    loss_part, dx0, shard, gs = _local_step(place, x[0], loss_target[0], p, pack_a, pack_b)

    gs["gmlp_b_s"] = jnp.sum(gs["gmlp_b_s"].reshape(BLK, GMLP_GROUPS, GROUP_DIM), axis=-1).T
    gs["attn_sinks"] = gs["attn_sinks"][:, 0]
    gs["loss"] = loss_part[0, 0]
    small_sum = _small_all_reduce(_pack_small(gs), name="small_all_reduce")

    grad_w, delta, new_m, new_v = {}, {}, {}, {}
    off = 0
    for n, rows in zip(BIG, BIG_ROWS):
        res = _adamw(with_cols(n, weights[n]), shard, with_cols(n, f_args["m_" + n]), with_cols(n, f_args["v_" + n]),
                     g_row0=off, tile=FF_SH // 2 if rows == FF_SH else 64, name="adamw_" + n)
        grad_w[n], delta[n], new_m[n], new_v[n] = [natural(n, a) for a in res]
        off += rows
    sm = {k: {n: f_args[k + n] for n, _ in SMALL if n != "loss"} for k in ("", "m_", "v_")}
    for k in sm:
        sm[k]["loss"] = jnp.zeros((), F32)
    res = _adamw(_pack_small(sm[""]), small_sum, _pack_small(sm["m_"]), _pack_small(sm["v_"]),
                 g_row0=0, tile=SMALL_ROWS, name="adamw_small")
    small = _unpack_small(res[0], shapes)
    for dst, packed in ((grad_w, res[0]), (delta, res[1]), (new_m, res[2]), (new_v, res[3])):
        dst.update({n: a for n, a in _unpack_small(packed, shapes).items() if n != "loss"})

    order = ('ffn1_norm_g', 'ffn1_w_gate', 'ffn1_w_up', 'ffn1_w_down', 'mix_norm_g', 'w_in', 'b_in', 'attn_sinks',
             'gmlp_ln_g', 'gmlp_ln_b', 'gmlp_w_s', 'gmlp_b_s', 'attn_out_norm_g', 'gmlp_out_norm_g', 'w_out', 'b_out',
             'ffn2_norm_g', 'ffn2_w_gate', 'ffn2_w_up', 'ffn2_w_down', 'final_norm_g')
    return (small["loss"], dx0.reshape(x.shape), *[grad_w[n] for n in order], *[delta[n] for n in order],
            *[new_m[n] for n in order], *[new_v[n] for n in order])
```

```python
import functools

import jax
import jax.numpy as jnp
from jax import lax
from jax.experimental import pallas as pl
from jax.experimental.pallas import tpu as pltpu

F32 = jnp.float32
BF16 = jnp.bfloat16

D_MODEL = 1024
D_FF = 2816
N_CHIPS = 4
FF_SH = D_FF // N_CHIPS
N_Q_HEADS = 8
N_KV_HEADS = 2
REP = N_Q_HEADS // N_KV_HEADS
HEAD_DIM = 64
ATTN_W = 512
KV_W = 128
GMLP_W = 512
GMLP_GROUPS = 8
GROUP_DIM = 64
BLK = 128
MIX_FWD_BLOCKS = 2
MIX_BWD_BLOCKS = 4
IN_W = 1792
IN_SH = IN_W // N_CHIPS
OUT_SH = D_MODEL // N_CHIPS
EPS = 1e-6
FFN_RES = 0.5
ATTN_SCALE = HEAD_DIM ** -0.5

ADAM_LR = 0.001
ADAM_B1 = 0.9
ADAM_B2 = 0.999
ADAM_EPS = 1e-08
ADAM_WD = 0.01
ADAM_STEP = 10

V7X_VMEM_LIMIT = 56 * 1024 * 1024
MESH = pl.DeviceIdType.MESH


def _cparams(sem):
    return pltpu.CompilerParams(dimension_semantics=sem, vmem_limit_bytes=V7X_VMEM_LIMIT)


def _dot(a, b):
    return jnp.dot(a, b, preferred_element_type=F32)


def _dot_nt(a, b):
    return lax.dot_general(a, b, (((1,), (1,)), ((), ())), preferred_element_type=F32)


def _dot_tn(a, b):
    return lax.dot_general(a, b, (((0,), (0,)), ((), ())), preferred_element_type=F32)


def _rms(x, g):
    r = lax.rsqrt(jnp.mean(x * x, axis=-1, keepdims=True) + EPS)
    return x * r * g, r


def _rms_bwd(dh, x, r, g):
    gy = dh * g
    dx = r * gy - x * (r * r * r) * jnp.mean(gy * x, axis=-1, keepdims=True)
    dg = jnp.sum(dh * x * r, axis=0, keepdims=True)
    return dx, dg


def _const(shape):
    nd = len(shape)
    return pl.BlockSpec(shape, lambda *_: (0,) * nd)


def _rows(t, w):
    return pl.BlockSpec((t, w), lambda i: (i, 0))


PACK_ROWS = 7 * FF_SH
HALF_ROWS = PACK_ROWS // 2
FFN_HALF = 3 * FF_SH // 2
MIX_HALF = FF_SH // 2
PACK_A_ROWS = 3 * FF_SH
PACK_B_ROWS = 4 * FF_SH
BIG = ("ffn1_w_gate", "ffn1_w_up", "ffn1_w_down", "ffn2_w_gate", "ffn2_w_up", "ffn2_w_down", "w_in", "w_out")
BIG_ROWS = (FF_SH, FF_SH, FF_SH, FF_SH, FF_SH, FF_SH, IN_SH, OUT_SH)
BIG_TRANSPOSED = (True, True, False, True, True, False, True, False)

SMALL = (("ffn1_norm_g", 1024), ("mix_norm_g", 1024), ("b_in", 1792), ("attn_sinks", 8), ("gmlp_ln_g", 512),
         ("gmlp_ln_b", 512), ("gmlp_w_s", 131072), ("gmlp_b_s", 1024), ("attn_out_norm_g", 512),
         ("gmlp_out_norm_g", 512), ("b_out", 1024), ("ffn2_norm_g", 1024), ("final_norm_g", 1024), ("loss", 1))


def _small_rows(n):
    return -(-n // 1024) * 8


SMALL_USED_ROWS = sum(_small_rows(n) for _, n in SMALL)
SMALL_ROWS = -(-SMALL_USED_ROWS // 16) * 16


def _pack_small(parts):
    out = []
    for name, n in SMALL:
        flat = parts[name].reshape(-1).astype(F32)
        rows = _small_rows(n)
        out.append(jnp.pad(flat, (0, rows * 128 - n)).reshape(rows, 128))
    if SMALL_ROWS > SMALL_USED_ROWS:
        out.append(jnp.zeros((SMALL_ROWS - SMALL_USED_ROWS, 128), F32))
    return jnp.concatenate(out, axis=0)


def _unpack_small(packed, shapes):
    res, off = {}, 0
    for name, n in SMALL:
        rows = _small_rows(n)
        res[name] = packed[off:off + rows].reshape(-1)[:n].reshape(shapes[name])
        off += rows
    return res


def _ffn_tile(x, g, wg_ref, wu_ref, wd_ref, hb_ref, a_ref, b_ref):
    h, _ = _rms(x, g)
    hb = h.astype(BF16)
    hb_ref[...] = hb
    acc = jnp.zeros(x.shape, F32)
    for j in range(N_CHIPS):
        a = _dot_nt(hb, wg_ref[j])
        b = _dot_nt(hb, wu_ref[j])
        a_ref[j] = a
        b_ref[j] = b
        f = (a * jax.nn.sigmoid(a) * b).astype(BF16)
        acc = acc + _dot(f, wd_ref[j])
    return x + FFN_RES * acc


def _ffn_saved_specs(s, tile):
    ab = pl.BlockSpec((N_CHIPS, tile, FF_SH), lambda i: (0, i, 0))
    shape = jax.ShapeDtypeStruct((N_CHIPS, s, FF_SH), F32)
    return [_rows(tile, D_MODEL), ab, ab], [jax.ShapeDtypeStruct((s, D_MODEL), BF16), shape, shape]


def _ffn_weight_specs(k0):
    one = pl.Buffered(1)
    return [pl.BlockSpec((N_CHIPS, FF_SH, D_MODEL), functools.partial(lambda kk, i: (0, kk, 0), k0 + d),
                         pipeline_mode=one) for d in range(3)]


def _mesh_place():
    x, y, c = lax.axis_index("x"), lax.axis_index("y"), lax.axis_index("c")
    others = [(1 - x, y), (x, 1 - y), (1 - x, 1 - y)]
    return x, y, c, others


def _gather_stages(o_ref, send_sems, recv_sems):
    x, y, c, others = _mesh_place()
    me = 2 * x + y
    sibling = (x, y, 1 - c)
    half_rows = o_ref.shape[1] // 2

    def half(slab, core):
        return o_ref.at[slab, pl.ds(pl.multiple_of(core * half_rows, 16), half_rows)]

    def copy(k, rows, to):
        return pltpu.make_async_remote_copy(src_ref=rows, dst_ref=rows, send_sem=send_sems.at[k],
                                            recv_sem=recv_sems.at[k], device_id=to, device_id_type=MESH)

    first = [copy(j, half(me, c), (px, py, c)) for j, (px, py) in enumerate(others)]
    passed = [copy(3 + j, half(2 * px + py, c), sibling) for j, (px, py) in enumerate(others)]

    def start():
        for cp in first:
            cp.start()

    def forward():
        for j, (px, py) in enumerate(others):
            copy(j, half(2 * px + py, c), (px, py, c)).wait_recv()
            passed[j].start()

    def finish():
        for j, (px, py) in enumerate(others):
            copy(3 + j, half(2 * px + py, 1 - c), sibling).wait_recv()
        for cp in first + passed:
            cp.wait_send()

    return start, forward, finish


def _ffn_fwd(x, g, pack, k0, gather, *, tile, name):
    s = x.shape[0]
    nt = s // tile
    forward_at = max(nt - 6, 0)

    def body(x_ref, g_ref, wg_ref, wu_ref, wd_ref, gin_ref, o_ref, hb_ref, a_ref, b_ref, gat_ref, send_sems, recv_sems):
        i = pl.program_id(0)
        start, forward, finish = _gather_stages(gat_ref, send_sems, recv_sems)
        pl.when(i == 0)(start)
        o_ref[...] = _ffn_tile(x_ref[...], g_ref[...], wg_ref, wu_ref, wd_ref, hb_ref, a_ref, b_ref)
        pl.when(i == forward_at)(forward)
        pl.when(i == nt - 1)(finish)

    saved_specs, saved_shapes = _ffn_saved_specs(s, tile)
    hbm = pl.BlockSpec(memory_space=pl.ANY)
    return pl.pallas_call(
        body, name=name, grid=(nt,),
        in_specs=[_rows(tile, D_MODEL), _const((1, D_MODEL))] + _ffn_weight_specs(k0) + [hbm],
        out_specs=[_rows(tile, D_MODEL)] + saved_specs + [hbm],
        out_shape=[jax.ShapeDtypeStruct(x.shape, F32)] + saved_shapes
                  + [jax.ShapeDtypeStruct(gather.shape, gather.dtype)],
        input_output_aliases={5: 4},
        scratch_shapes=[pltpu.SemaphoreType.DMA((6,)), pltpu.SemaphoreType.DMA((6,))],
        compiler_params=_cparams(("arbitrary",)),
    )(x, g, pack, pack, pack, gather)


def _ffn_fwd_loss(x, g, pack, k0, gf, tgt, *, tile, name):
    s = x.shape[0]

    def body(x_ref, g_ref, wg_ref, wu_ref, wd_ref, gf_ref, t_ref, dx_ref, loss_ref, dgf_ref, hb_ref, a_ref, b_ref,
             do_ref):
        @pl.when(pl.program_id(0) == 0)
        def _():
            loss_ref[...] = jnp.zeros_like(loss_ref)
            dgf_ref[...] = jnp.zeros_like(dgf_ref)

        x3 = _ffn_tile(x_ref[...], g_ref[...], wg_ref, wu_ref, wd_ref, hb_ref, a_ref, b_ref)
        gf_v = gf_ref[...]
        out, r = _rms(x3, gf_v)
        diff = out - t_ref[...]
        part = jnp.sum(jnp.sum(diff * diff, axis=-1, keepdims=True), axis=0, keepdims=True)
        loss_ref[...] += jnp.broadcast_to(part * (0.5 / D_MODEL), loss_ref.shape)
        dx, dg = _rms_bwd(diff * (1.0 / D_MODEL), x3, r, gf_v)
        dx_ref[...] = dx
        do_ref[...] = (FFN_RES * dx).astype(BF16)
        dgf_ref[...] += dg

    saved_specs, saved_shapes = _ffn_saved_specs(s, tile)
    return pl.pallas_call(
        body, name=name, grid=(s // tile,),
        in_specs=[_rows(tile, D_MODEL), _const((1, D_MODEL))] + _ffn_weight_specs(k0)
                 + [_const((1, D_MODEL)), _rows(tile, D_MODEL)],
        out_specs=[_rows(tile, D_MODEL), _const((1, 128)), _const((1, D_MODEL))] + saved_specs
                  + [_rows(tile, D_MODEL)],
        out_shape=[jax.ShapeDtypeStruct(x.shape, F32),
                   jax.ShapeDtypeStruct((1, 128), F32),
                   jax.ShapeDtypeStruct((1, D_MODEL), F32)] + saved_shapes
                  + [jax.ShapeDtypeStruct(x.shape, BF16)],
        compiler_params=_cparams(("arbitrary",)),
    )(x, g, pack, pack, pack, gf, tgt)


def _ffn_bwd(place, hb, a, b, do, pack, region, land, mix_grads, *, tile, name):
    s = hb.shape[0]
    nt = s // tile
    land_rows = pl.ds(region * FFN_HALF, FFN_HALF)
    mix_rows = pl.ds(2 * FFN_HALF, MIX_HALF)
    with_mix = mix_grads is not None
    with_land = land is not None
    n_others = 2 * N_CHIPS - 1

    def body(place_ref, hb_ref, a_ref, b_ref, do_ref, wg_ref, wu_ref, wd_ref, *rest):
        rest = list(rest)
        mix_ref = rest.pop(0) if with_mix else None
        if with_land:
            rest.pop(0)
        dhp_ref, land_ref, acc_ref, stage_ref, send_sems, recv_sem, local_sem = rest[:7]
        t, i = pl.program_id(0), pl.program_id(1)
        xi, yi, c = lax.axis_index("x"), lax.axis_index("y"), lax.axis_index("c")
        dev = 4 * xi + 2 * yi + c
        tt = (t + 1) % N_CHIPS
        tx, ty = jnp.bitwise_xor(xi, tt // 2), jnp.bitwise_xor(yi, tt % 2)

        def remote(src, dst, ssem, rsem, to):
            return pltpu.make_async_remote_copy(src_ref=src, dst_ref=dst, send_sem=ssem, recv_sem=rsem,
                                                device_id=to, device_id_type=MESH)

        def stage_half(h):
            return stage_ref.at[pl.ds(pl.multiple_of(h * FFN_HALF, 16), FFN_HALF)]

        if with_mix:
            mix_send, mix_recv, mix_local = rest[7:10]

            @pl.when(jnp.logical_and(t == 0, i == 0))
            def _():
                for chip in range(N_CHIPS):
                    for h in range(2):
                        src = mix_ref.at[chip, pl.ds(h * MIX_HALF, MIX_HALF)]
                        dst = land_ref.at[dev, mix_rows]
                        mine = jnp.logical_and(2 * xi + yi == chip, c == h)

                        @pl.when(mine)
                        def _():
                            pltpu.make_async_copy(src, dst, mix_local).start()

                        @pl.when(jnp.logical_not(mine))
                        def _():
                            remote(src, dst, mix_send, mix_recv, (chip // 2, chip % 2, h)).start()

        @pl.when(i == 0)
        def _():
            acc_ref[...] = jnp.zeros_like(acc_ref)

        hb = hb_ref[...]
        dob = do_ref[...]
        wg_j, wu_j, wd_j = wg_ref[0], wu_ref[0], wd_ref[0]
        a = a_ref[0]
        b = b_ref[0]
        sg = jax.nn.sigmoid(a)
        sa = a * sg
        fb = (sa * b).astype(BF16)
        df = _dot_nt(dob, wd_j)
        dbb = (df * sa).astype(BF16)
        dab = (df * b * (sg + sa * (1.0 - sg))).astype(BF16)
        dhp_ref[0] = (_dot(dab, wg_j) + _dot(dbb, wu_j)).astype(BF16)
        acc_ref[0:FF_SH, :] += _dot_tn(dab, hb)
        acc_ref[FF_SH:2 * FF_SH, :] += _dot_tn(dbb, hb)
        acc_ref[2 * FF_SH:3 * FF_SH, :] += _dot_tn(fb, dob)

        @pl.when(i == nt - 1)
        def _():
            dst = land_ref.at[dev, land_rows]

            @pl.when(t > 0)
            def _():
                for h in range(2):
                    remote(stage_half(h), dst, send_sems.at[h], recv_sem, (tx, ty, h)).wait_send()

            def cast_rows(r, carry):
                rows = pl.ds(pl.multiple_of(r * MIX_HALF, 16), MIX_HALF)
                stage_ref[rows, :] = acc_ref[rows, :].astype(BF16)
                return carry

            lax.fori_loop(0, 3 * FF_SH // MIX_HALF, cast_rows, 0)

            @pl.when(t < N_CHIPS - 1)
            def _():
                for h in range(2):
                    remote(stage_half(h), dst, send_sems.at[h], recv_sem, (tx, ty, h)).start()

            @pl.when(t == N_CHIPS - 1)
            def _():
                own = pltpu.make_async_copy(stage_half(c), dst, local_sem)
                own.start()
                sib = remote(stage_half(1 - c), dst, send_sems.at[0], recv_sem, (xi, yi, 1 - c))
                sib.start()
                sib.wait_send()
                own.wait()
                arrivals = land_ref.at[pl.ds(0, n_others), land_rows]
                remote(arrivals, arrivals, send_sems.at[0], recv_sem, (xi, yi, 1 - c)).wait_recv()
                if with_mix:
                    seven = land_ref.at[pl.ds(0, n_others), mix_rows]
                    both = remote(seven, seven, mix_send, mix_recv, (xi, yi, 1 - c))
                    both.wait_send()
                    both.wait_recv()
                    pltpu.make_async_copy(mix_ref.at[0, pl.ds(0, MIX_HALF)], land_ref.at[dev, mix_rows],
                                          mix_local).wait()

    def wspec(kk):
        return pl.BlockSpec((1, FF_SH, D_MODEL),
                            lambda t, i, pr: (jnp.bitwise_xor(pr[0], (t + 1) % N_CHIPS), kk, 0))

    xspec = pl.BlockSpec((tile, D_MODEL), lambda t, i, pr: (i, 0))
    abspec = pl.BlockSpec((1, tile, FF_SH), lambda t, i, pr: (jnp.bitwise_xor(pr[0], (t + 1) % N_CHIPS), i, 0))
    hbm = pl.BlockSpec(memory_space=pl.ANY)
    in_specs = [xspec, abspec, abspec, xspec, wspec(0), wspec(1), wspec(2)]
    operands = [place, hb, a, b, do, pack, pack, pack]
    scratch = [pltpu.VMEM((3 * FF_SH, D_MODEL), F32), pltpu.VMEM((3 * FF_SH, D_MODEL), BF16),
               pltpu.SemaphoreType.DMA((2,)), pltpu.SemaphoreType.DMA, pltpu.SemaphoreType.DMA]
    if with_mix:
        in_specs.append(hbm)
        operands.append(mix_grads)
        scratch += [pltpu.SemaphoreType.DMA, pltpu.SemaphoreType.DMA, pltpu.SemaphoreType.DMA]
    aliases = {}
    if with_land:
        in_specs.append(hbm)
        operands.append(land)
        aliases = {len(operands) - 1: 1}
    grid_spec = pltpu.PrefetchScalarGridSpec(
        num_scalar_prefetch=1, grid=(N_CHIPS, nt), in_specs=in_specs,
        out_specs=[pl.BlockSpec((1, tile, D_MODEL), lambda t, i, pr: (t, i, 0)), hbm],
        scratch_shapes=scratch)
    return pl.pallas_call(
        body, name=name, grid_spec=grid_spec,
        out_shape=[jax.ShapeDtypeStruct((N_CHIPS, s, D_MODEL), BF16),
                   jax.ShapeDtypeStruct((2 * N_CHIPS, HALF_ROWS, D_MODEL), BF16)],
        input_output_aliases=aliases,
        compiler_params=_cparams(("arbitrary", "arbitrary")),
    )(*operands)


def _mix_grads_pack(dw_in_t, dw_out, *, name):
    def body(a_ref, b_ref, o_ref):
        o_ref[0, 0:IN_SH, :] = a_ref[0].astype(BF16)
        o_ref[0, IN_SH:FF_SH, :] = b_ref[0].astype(BF16)

    return pl.pallas_call(
        body, name=name, grid=(N_CHIPS,),
        in_specs=[pl.BlockSpec((1, IN_SH, D_MODEL), lambda j: (j, 0, 0)),
                  pl.BlockSpec((1, OUT_SH, D_MODEL), lambda j: (j, 0, 0))],
        out_specs=pl.BlockSpec((1, FF_SH, D_MODEL), lambda j: (j, 0, 0)),
        out_shape=jax.ShapeDtypeStruct((N_CHIPS, FF_SH, D_MODEL), BF16),
        compiler_params=_cparams(("arbitrary",)),
    )(dw_in_t.reshape(N_CHIPS, IN_SH, D_MODEL), dw_out.reshape(N_CHIPS, OUT_SH, D_MODEL))


def _share_stages(o_ref, send_sems, recv_sems):
    x, y, c, _ = _mesh_place()

    def rows(k, core):
        if k < 2:
            return o_ref.at[pl.ds(pl.multiple_of(k * 2 * FFN_HALF + core * FFN_HALF, 8), FFN_HALF)]
        return o_ref.at[pl.ds(pl.multiple_of(4 * FFN_HALF + core * MIX_HALF, 8), MIX_HALF)]

    def copy(k, core):
        return pltpu.make_async_remote_copy(src_ref=rows(k, core), dst_ref=rows(k, core), send_sem=send_sems.at[k],
                                            recv_sem=recv_sems.at[k], device_id=(x, y, 1 - c), device_id_type=MESH)

    sends = [copy(k, c) for k in range(3)]

    def start():
        for cp in sends:
            cp.start()

    def finish():
        for k in range(3):
            copy(k, 1 - c).wait_recv()
        for cp in sends:
            cp.wait_send()

    return start, finish


def _norm_bwd(dhp, x, dy, g, *, tile, name):
    s = x.shape[0]

    def body(dhp_ref, x_ref, dy_ref, g_ref, dx_ref, dg_ref):
        @pl.when(pl.program_id(0) == 0)
        def _():
            dg_ref[...] = jnp.zeros_like(dg_ref)

        dh = ((dhp_ref[0].astype(F32) + dhp_ref[1].astype(F32))
              + (dhp_ref[2].astype(F32) + dhp_ref[3].astype(F32)))
        x_v = x_ref[...]
        r = lax.rsqrt(jnp.mean(x_v * x_v, axis=-1, keepdims=True) + EPS)
        dx, dg = _rms_bwd(dh, x_v, r, g_ref[...])
        dx_ref[...] = dy_ref[...] + dx
        dg_ref[...] += dg

    return pl.pallas_call(
        body, name=name, grid=(s // tile,),
        in_specs=[pl.BlockSpec((N_CHIPS, tile, D_MODEL), lambda i: (0, i, 0)),
                  _rows(tile, D_MODEL), _rows(tile, D_MODEL), _const((1, D_MODEL))],
        out_specs=[_rows(tile, D_MODEL), _const((1, D_MODEL))],
        out_shape=[jax.ShapeDtypeStruct(x.shape, F32), jax.ShapeDtypeStruct((1, D_MODEL), F32)],
        compiler_params=_cparams(("arbitrary",)),
    )(dhp, x, dy, g)


def _mix_in_bwd(x, dy, dq, dk, dv, dz, g, w_in_t, *, tile, name):
    s = x.shape[0]

    def body(x_ref, dy_ref, dq_ref, dk_ref, dv_ref, dz_ref, g_ref, w_ref, dx_ref, dw_ref, db_ref, dg_ref, do_ref):
        @pl.when(pl.program_id(0) == 0)
        def _():
            dw_ref[...] = jnp.zeros_like(dw_ref)
            db_ref[...] = jnp.zeros_like(db_ref)
            dg_ref[...] = jnp.zeros_like(dg_ref)

        dproj = jnp.concatenate([dq_ref[...], dk_ref[...], dv_ref[...], dz_ref[...]], axis=-1)
        db_ref[...] += jnp.sum(dproj, axis=0, keepdims=True)
        dpb = dproj.astype(BF16)
        x_v = x_ref[...]
        g_v = g_ref[...]
        h, r = _rms(x_v, g_v)
        dw_ref[...] += _dot_tn(dpb, h.astype(BF16))
        dh = _dot(dpb, w_ref[...])
        dxn, dg = _rms_bwd(dh, x_v, r, g_v)
        dx = dy_ref[...] + dxn
        dx_ref[...] = dx
        do_ref[...] = (FFN_RES * dx).astype(BF16)
        dg_ref[...] += dg

    return pl.pallas_call(
        body, name=name, grid=(s // tile,),
        in_specs=[_rows(tile, D_MODEL), _rows(tile, D_MODEL), _rows(tile, ATTN_W), _rows(tile, KV_W),
                  _rows(tile, KV_W), _rows(tile, 2 * GMLP_W), _const((1, D_MODEL)), _const((IN_W, D_MODEL))],
        out_specs=[_rows(tile, D_MODEL), _const((IN_W, D_MODEL)), _const((1, IN_W)), _const((1, D_MODEL)),
                   _rows(tile, D_MODEL)],
        out_shape=[jax.ShapeDtypeStruct(x.shape, F32), jax.ShapeDtypeStruct((IN_W, D_MODEL), F32),
                   jax.ShapeDtypeStruct((1, IN_W), F32), jax.ShapeDtypeStruct((1, D_MODEL), F32),
                   jax.ShapeDtypeStruct(x.shape, BF16)],
        compiler_params=_cparams(("arbitrary",)),
    )(x, dy, dq, dk, dv, dz, g, w_in_t)


_GELU_C = 0.7978845608028654
_GELU_A = 0.044715


def _gelu_tanh(x):
    x2 = x * x
    return jnp.tanh(_GELU_C * (x + _GELU_A * (x2 * x))), x2


def _band(ref, i):
    prev = jnp.maximum(i - 1, 0)
    return jnp.concatenate([ref[pl.ds(pl.multiple_of(prev * BLK, BLK), BLK), :],
                            ref[pl.ds(pl.multiple_of(i * BLK, BLK), BLK), :]], axis=0)


def _key_in_block():
    return lax.broadcasted_iota(jnp.int32, (BLK, BLK), 0) <= lax.broadcasted_iota(jnp.int32, (BLK, BLK), 1)


def _fold(band, own):
    return jnp.where(own, band[BLK:], band[:BLK])


def _unfold(a, own):
    zero = jnp.zeros_like(a)
    return jnp.concatenate([jnp.where(own, zero, a), jnp.where(own, a, zero)], axis=0).astype(BF16)


def _attn_fwd(q, kb, vb, i, sink_ref):
    own = _key_in_block()
    outs, saved = [], []
    for h in range(N_Q_HEADS):
        cols = slice((h // REP) * HEAD_DIM, (h // REP + 1) * HEAD_DIM)
        s2 = _dot_nt(kb[:, cols], q[:, h * HEAD_DIM:(h + 1) * HEAD_DIM])
        sc = jnp.where(own, s2[BLK:], jnp.where(i > 0, s2[:BLK], -jnp.inf))
        sink = sink_ref[h]
        m = jnp.maximum(jnp.max(sc, axis=0, keepdims=True), sink)
        p = jnp.exp(sc - m)
        es = jnp.exp(sink - m)
        inv = 1.0 / (jnp.sum(p, axis=0, keepdims=True) + es)
        pn = p * inv
        band = _unfold(pn, own)
        outs.append(_dot_tn(band, vb[:, cols]))
        saved.append((pn, band, es * inv))
    return jnp.concatenate(outs, axis=-1), saved


def _tril_mask():
    t = lax.broadcasted_iota(jnp.int32, (BLK, BLK), 0)
    s_ = lax.broadcasted_iota(jnp.int32, (BLK, BLK), 1)
    return s_ <= t


def _gmlp_fwd_parts(zg, lng, lnb, ws_ref, bs_full):
    th, zg2 = _gelu_tanh(zg)
    z = 0.5 * zg * (1.0 + th)
    u = z[:, :GMLP_W]
    zv = z[:, GMLP_W:]
    mu = jnp.mean(zv, axis=-1, keepdims=True)
    zc = zv - mu
    rstd = lax.rsqrt(jnp.mean(zc * zc, axis=-1, keepdims=True) + EPS)
    xh = zc * rstd
    vvb = (xh * lng + lnb).astype(BF16)
    tril = _tril_mask()
    wms, parts = [], []
    for gi in range(GMLP_GROUPS):
        wm = jnp.where(tril, ws_ref[gi], 0.0).astype(BF16)
        wms.append(wm)
        parts.append(_dot(wm, vvb[:, gi * GROUP_DIM:(gi + 1) * GROUP_DIM]))
    mixed = jnp.concatenate(parts, axis=-1) + bs_full
    gelu_grad = 0.5 * (1.0 + th) + 0.5 * zg * (1.0 - th * th) * (_GELU_C * (1.0 + 3.0 * _GELU_A * zg2))
    return u, xh, rstd, vvb, wms, mixed, gelu_grad


def _mixer_fwd(x1, g, w_in_t, b_in, sinks, lng, lnb, w_s, bs_full, gao, ggo, w_out, b_out, *, name):
    s = x1.shape[0]
    nb = min(MIX_FWD_BLOCKS, s // BLK)
    step_rows = nb * BLK
    last = s // step_rows - 1

    def tile_of(i, lag):
        return jnp.clip(i - lag, 0, last)

    def body(sink_ref, xa_ref, xc_ref, g_ref, wi_ref, bi_ref, lng_ref, lnb_ref, ws_ref, bs_ref, gao_ref, ggo_ref,
             wo_ref, bo_ref, q_ref, k_ref, v_ref, z_ref, y_ref, o_ref, qs_ref, zs_ref, ys_ref):
        i = pl.program_id(0)

        @pl.when(i == 0)
        def _():
            for ref in (k_ref, v_ref, qs_ref, zs_ref, ys_ref):
                ref[...] = jnp.zeros_like(ref)

        slot_a, slot_b, slot_c = i % 2, (i + 1) % 2, i % 2

        o_ref[...] = xc_ref[...] + (_dot(ys_ref[slot_c], wo_ref[...]) + bo_ref[...])

        tile_b = tile_of(i, 1)
        for b in range(nb):
            blk = tile_b * nb + b
            rows = slice(b * BLK, (b + 1) * BLK)
            y_attn, _ = _attn_fwd(qs_ref[slot_b, rows, :], _band(k_ref, blk), _band(v_ref, blk), blk, sink_ref)
            u, _, _, _, _, mixed, _ = _gmlp_fwd_parts(zs_ref[slot_b, rows, :], lng_ref[...], lnb_ref[...], ws_ref,
                                                      bs_ref[...])
            ya, _ = _rms(y_attn, gao_ref[...])
            yg, _ = _rms(u * mixed, ggo_ref[...])
            y_blk = jnp.concatenate([ya, yg], axis=-1).astype(BF16)
            y_ref[rows, :] = y_blk
            ys_ref[slot_b, rows, :] = y_blk

        h, _ = _rms(xa_ref[...], g_ref[...])
        proj = _dot_nt(h.astype(BF16), wi_ref[...]) + bi_ref[...]
        q_t = (proj[:, :ATTN_W] * ATTN_SCALE).astype(BF16)
        z_t = proj[:, ATTN_W + 2 * KV_W:]
        here = pl.ds(pl.multiple_of(tile_of(i, 0) * step_rows, step_rows), step_rows)
        q_ref[...] = q_t
        z_ref[...] = z_t
        qs_ref[slot_a] = q_t
        zs_ref[slot_a] = z_t
        k_ref[here, :] = proj[:, ATTN_W:ATTN_W + KV_W].astype(BF16)
        v_ref[here, :] = proj[:, ATTN_W + KV_W:ATTN_W + 2 * KV_W].astype(BF16)

    def lagged(width, lag):
        return pl.BlockSpec((step_rows, width), lambda i: (tile_of(i, lag), 0))

    return pl.pallas_call(
        body, name=name, grid=(last + 3,),
        in_specs=[pl.BlockSpec(memory_space=pltpu.SMEM),
                  lagged(D_MODEL, 0), lagged(D_MODEL, 2), _const((1, D_MODEL)), _const((IN_W, D_MODEL)),
                  _const((1, IN_W)), _const((1, GMLP_W)), _const((1, GMLP_W)), _const((GMLP_GROUPS, BLK, BLK)),
                  _const((BLK, GMLP_W)), _const((1, ATTN_W)), _const((1, GMLP_W)), _const((D_MODEL, D_MODEL)),
                  _const((1, D_MODEL))],
        out_specs=[lagged(ATTN_W, 0), _const((s, KV_W)), _const((s, KV_W)), lagged(2 * GMLP_W, 0),
                   lagged(D_MODEL, 1), lagged(D_MODEL, 2)],
        out_shape=[jax.ShapeDtypeStruct((s, ATTN_W), BF16), jax.ShapeDtypeStruct((s, KV_W), BF16),
                   jax.ShapeDtypeStruct((s, KV_W), BF16), jax.ShapeDtypeStruct((s, 2 * GMLP_W), F32),
                   jax.ShapeDtypeStruct((s, D_MODEL), BF16), jax.ShapeDtypeStruct((s, D_MODEL), F32)],
        scratch_shapes=[pltpu.VMEM((2, step_rows, ATTN_W), BF16), pltpu.VMEM((2, step_rows, 2 * GMLP_W), F32),
                        pltpu.VMEM((2, step_rows, D_MODEL), BF16)],
        compiler_params=_cparams(("arbitrary",)),
    )(sinks, x1, x1, g, w_in_t, b_in, lng, lnb, w_s, bs_full, gao, ggo, w_out, b_out)


def _norm_bwd_mix_out(dhp, x, dy, g, yb, w_out, *, tile, name):
    s = x.shape[0]

    def body(dhp_ref, x_ref, dy_ref, g_ref, y_ref, w_ref, dx_ref, dg_ref, dyy_ref, dw_ref, db_ref):
        @pl.when(pl.program_id(0) == 0)
        def _():
            dg_ref[...] = jnp.zeros_like(dg_ref)
            dw_ref[...] = jnp.zeros_like(dw_ref)
            db_ref[...] = jnp.zeros_like(db_ref)

        dh = ((dhp_ref[0].astype(F32) + dhp_ref[1].astype(F32))
              + (dhp_ref[2].astype(F32) + dhp_ref[3].astype(F32)))
        x_v = x_ref[...]
        r = lax.rsqrt(jnp.mean(x_v * x_v, axis=-1, keepdims=True) + EPS)
        dxn, dg = _rms_bwd(dh, x_v, r, g_ref[...])
        dx = dy_ref[...] + dxn
        dx_ref[...] = dx
        dg_ref[...] += dg
        dxb = dx.astype(BF16)
        db_ref[...] += jnp.sum(dx, axis=0, keepdims=True)
        dw_ref[...] += _dot_tn(y_ref[...], dxb)
        dyy_ref[...] = _dot_nt(dxb, w_ref[...])

    return pl.pallas_call(
        body, name=name, grid=(s // tile,),
        in_specs=[pl.BlockSpec((N_CHIPS, tile, D_MODEL), lambda i: (0, i, 0)),
                  _rows(tile, D_MODEL), _rows(tile, D_MODEL), _const((1, D_MODEL)), _rows(tile, D_MODEL),
                  _const((D_MODEL, D_MODEL))],
        out_specs=[_rows(tile, D_MODEL), _const((1, D_MODEL)), _rows(tile, D_MODEL), _const((D_MODEL, D_MODEL)),
                   _const((1, D_MODEL))],
        out_shape=[jax.ShapeDtypeStruct(x.shape, F32), jax.ShapeDtypeStruct((1, D_MODEL), F32),
                   jax.ShapeDtypeStruct(x.shape, F32), jax.ShapeDtypeStruct((D_MODEL, D_MODEL), F32),
                   jax.ShapeDtypeStruct((1, D_MODEL), F32)],
        compiler_params=_cparams(("arbitrary",)),
    )(dhp, x, dy, g, yb, w_out)


def _mix_core_bwd(dyy, q, k, v, zg, sinks, lng, lnb, w_s, bs_full, gao, ggo, *, name):
    s = dyy.shape[0]
    nb = min(MIX_BWD_BLOCKS, s // BLK)
    nsteps = s // (nb * BLK)

    def body(*refs):
        accumulators = refs[13:15] + refs[16:]

        @pl.when(pl.program_id(0) == 0)
        def _():
            for ref in accumulators:
                ref[...] = jnp.zeros_like(ref)

        for b in range(nb):
            one_block(pl.program_id(0) * nb + b, slice(b * BLK, (b + 1) * BLK), *refs)

        @pl.when(pl.program_id(0) == nsteps - 1)
        def _():
            tril = _tril_mask()
            for gi in range(GMLP_GROUPS):
                refs[20][gi] = jnp.where(tril, refs[20][gi], 0.0)

    def one_block(i, rows, sink_ref, dyy_ref, q_ref, k_ref, v_ref, z_ref, lng_ref, lnb_ref, ws_ref, bs_ref, gao_ref,
                  ggo_ref, dq_ref, dk_ref, dv_ref, dz_ref, dgao_ref, dggo_ref, dlng_ref, dlnb_ref, dws_ref, dms_ref,
                  dsk_ref):
        q_v = q_ref[rows, :]
        kb = _band(k_ref, i)
        vb = _band(v_ref, i)
        lng_v = lng_ref[...]
        gao_v = gao_ref[...]
        ggo_v = ggo_ref[...]

        y_attn, probs = _attn_fwd(q_v, kb, vb, i, sink_ref)
        u, xh, rstd, vvb, wms, mixed, gelu_grad = _gmlp_fwd_parts(z_ref[rows, :], lng_v, lnb_ref[...], ws_ref,
                                                                  bs_ref[...])
        y_gmlp = u * mixed
        ra = lax.rsqrt(jnp.mean(y_attn * y_attn, axis=-1, keepdims=True) + EPS)
        rg = lax.rsqrt(jnp.mean(y_gmlp * y_gmlp, axis=-1, keepdims=True) + EPS)

        dyy = dyy_ref[rows, :]
        d_attn, dgao = _rms_bwd(dyy[:, :ATTN_W], y_attn, ra, gao_v)
        d_gmlp, dggo = _rms_bwd(dyy[:, ATTN_W:], y_gmlp, rg, ggo_v)
        dgao_ref[...] += dgao
        dggo_ref[...] += dggo

        du = d_gmlp * mixed
        dmixed = d_gmlp * u
        dms_ref[...] += dmixed
        dmb = dmixed.astype(BF16)
        dvv_parts = []
        for gi in range(GMLP_GROUPS):
            sl = slice(gi * GROUP_DIM, (gi + 1) * GROUP_DIM)
            dws_ref[gi] += _dot_nt(dmb[:, sl], vvb[:, sl])
            dvv_parts.append(_dot_tn(wms[gi], dmb[:, sl]))
        dvv = jnp.concatenate(dvv_parts, axis=-1)
        dlng_ref[...] += jnp.sum(dvv * xh, axis=0, keepdims=True)
        dlnb_ref[...] += jnp.sum(dvv, axis=0, keepdims=True)
        dxh = dvv * lng_v
        dzv = rstd * (dxh - jnp.mean(dxh, axis=-1, keepdims=True)
                      - xh * jnp.mean(dxh * xh, axis=-1, keepdims=True))
        dz_ref[rows, :] = jnp.concatenate([du, dzv], axis=-1) * gelu_grad

        dab = d_attn.astype(BF16)
        own = _key_in_block()
        dq_parts = []
        dk_parts = []
        dv_parts = []
        for gi in range(N_KV_HEADS):
            cols = slice(gi * HEAD_DIM, (gi + 1) * HEAD_DIM)
            kg, vg = kb[:, cols], vb[:, cols]
            dkg = jnp.zeros((2 * BLK, HEAD_DIM), F32)
            dvg = jnp.zeros((2 * BLK, HEAD_DIM), F32)
            for rr in range(REP):
                h = gi * REP + rr
                hs = slice(h * HEAD_DIM, (h + 1) * HEAD_DIM)
                qh, doh = q_v[:, hs], dab[:, hs]
                pn, band, psink = probs[h]
                dp = _fold(_dot_nt(vg, doh), own)
                delta = jnp.sum(pn * dp, axis=0, keepdims=True)
                ds2 = _unfold(pn * (dp - delta), own)
                dsink = jnp.sum(-psink * delta, axis=-1, keepdims=True)
                dsk_ref[pl.ds(h, 1), :] += jnp.broadcast_to(dsink, (1, 128))
                dq_parts.append(_dot_tn(ds2, kg) * ATTN_SCALE)
                dkg = dkg + _dot(ds2, qh)
                dvg = dvg + _dot(band, doh)
            dk_parts.append(dkg)
            dv_parts.append(dvg)
        dq_ref[rows, :] = jnp.concatenate(dq_parts, axis=-1)
        dkb = jnp.concatenate(dk_parts, axis=-1)
        dvb = jnp.concatenate(dv_parts, axis=-1)
        prev = pl.ds(pl.multiple_of(jnp.maximum(i - 1, 0) * BLK, BLK), BLK)
        cur = pl.ds(pl.multiple_of(i * BLK, BLK), BLK)
        dk_ref[prev, :] += dkb[:BLK]
        dv_ref[prev, :] += dvb[:BLK]
        dk_ref[cur, :] += dkb[BLK:]
        dv_ref[cur, :] += dvb[BLK:]

    return pl.pallas_call(
        body, name=name, grid=(nsteps,),
        in_specs=[pl.BlockSpec(memory_space=pltpu.SMEM),
                  _rows(nb * BLK, D_MODEL), _rows(nb * BLK, ATTN_W), _const((s, KV_W)), _const((s, KV_W)),
                  _rows(nb * BLK, 2 * GMLP_W), _const((1, GMLP_W)), _const((1, GMLP_W)),
                  _const((GMLP_GROUPS, BLK, BLK)), _const((BLK, GMLP_W)), _const((1, ATTN_W)), _const((1, GMLP_W))],
        out_specs=[_rows(nb * BLK, ATTN_W), _const((s, KV_W)), _const((s, KV_W)), _rows(nb * BLK, 2 * GMLP_W),
                   _const((1, ATTN_W)), _const((1, GMLP_W)),
                   _const((1, GMLP_W)), _const((1, GMLP_W)), _const((GMLP_GROUPS, BLK, BLK)),
                   _const((BLK, GMLP_W)), _const((N_Q_HEADS, 128))],
        out_shape=[jax.ShapeDtypeStruct((s, ATTN_W), F32), jax.ShapeDtypeStruct((s, KV_W), F32),
                   jax.ShapeDtypeStruct((s, KV_W), F32), jax.ShapeDtypeStruct((s, 2 * GMLP_W), F32),
                   jax.ShapeDtypeStruct((1, ATTN_W), F32), jax.ShapeDtypeStruct((1, GMLP_W), F32),
                   jax.ShapeDtypeStruct((1, GMLP_W), F32), jax.ShapeDtypeStruct((1, GMLP_W), F32),
                   jax.ShapeDtypeStruct((GMLP_GROUPS, BLK, BLK), F32), jax.ShapeDtypeStruct((BLK, GMLP_W), F32),
                   jax.ShapeDtypeStruct((N_Q_HEADS, 128), F32)],
        compiler_params=_cparams(("arbitrary",)),
    )(sinks, dyy, q, k, v, zg, lng, lnb, w_s, bs_full, gao, ggo)


def _local_step(place, x, tgt, p, pack_a, pack_b, *, tile=512, fwd_tile=256, bwd_tile=512, norm_tile=512):
    g = {}
    tile, fwd_tile, bwd_tile, norm_tile = (min(t_, x.shape[0]) for t_ in (tile, fwd_tile, bwd_tile, norm_tile))
    x1, hb1, a1, b1, pack_b = _ffn_fwd(x, p["ffn1_norm_g"], pack_a, 0, pack_b, tile=fwd_tile, name="ffn1_fwd")
    mix_rows = pack_b[:, 3 * FF_SH:, :]
    w_in_t = mix_rows[:, :IN_SH, :].reshape(IN_W, D_MODEL)
    w_out = mix_rows[:, IN_SH:, :].reshape(D_MODEL, D_MODEL)
    q, k, v, zg, yb, x2 = _mixer_fwd(
        x1, p["mix_norm_g"], w_in_t, p["b_in"], p["attn_sinks"], p["gmlp_ln_g"], p["gmlp_ln_b"], p["gmlp_w_s"],
        p["bs_full"], p["attn_out_norm_g"], p["gmlp_out_norm_g"], w_out, p["b_out"], name="mixer_fwd")
    mix_args = (q, k, v, zg, p["attn_sinks"], p["gmlp_ln_g"], p["gmlp_ln_b"], p["gmlp_w_s"], p["bs_full"],
                p["attn_out_norm_g"], p["gmlp_out_norm_g"])
    dx3, loss, g["final_norm_g"], hb2, a2, b2, do3 = _ffn_fwd_loss(
        x2, p["ffn2_norm_g"], pack_b, 0, p["final_norm_g"], tgt, tile=fwd_tile, name="ffn2_fwd_loss")

    dhp, land = _ffn_bwd(place, hb2, a2, b2, do3, pack_b, 1, None, None, tile=bwd_tile, name="ffn2_bwd")
    dx2, g["ffn2_norm_g"], dyy, dw_out, g["b_out"] = _norm_bwd_mix_out(
        dhp, x2, dx3, p["ffn2_norm_g"], yb, w_out, tile=norm_tile, name="ffn2_norm_bwd")

    (dq, dk, dv, dz, g["attn_out_norm_g"], g["gmlp_out_norm_g"], g["gmlp_ln_g"],
     g["gmlp_ln_b"], g["gmlp_w_s"], dmix_sum, dsinks) = _mix_core_bwd(dyy, *mix_args, name="mix_core_bwd")
    g["gmlp_b_s"] = dmix_sum
    g["attn_sinks"] = dsinks
    dx1, dw_in_t, g["b_in"], g["mix_norm_g"], do1 = _mix_in_bwd(
        x1, dx2, dq, dk, dv, dz, p["mix_norm_g"], w_in_t, tile=tile, name="mix_in_bwd")
    mix_grads = _mix_grads_pack(dw_in_t, dw_out, name="mix_grads_pack")

    dhp1, land = _ffn_bwd(place, hb1, a1, b1, do1, pack_a, 0, land, mix_grads, tile=bwd_tile, name="ffn1_bwd")
    dx0, g["ffn1_norm_g"] = _norm_bwd(dhp1, x, dx1, p["ffn1_norm_g"], tile=norm_tile, name="ffn1_norm_bwd")
    return loss, dx0, land, g


def _pack_cast(place, parts, *, name):
    def body(place_ref, *refs):
        oa_ref, ob_ref = refs[-2], refs[-1]
        off = 0
        for k, (ref, rows) in enumerate(zip(refs[:-2], BIG_ROWS)):
            if k == 3:
                off = 0
            (oa_ref if k < 3 else ob_ref)[0, off:off + rows, :] = ref[...].astype(BF16)
            off += rows

    one = pl.Buffered(1)

    def slab(rows):
        return pl.BlockSpec((1, rows, D_MODEL), lambda i, pr: (pr[0], 0, 0), pipeline_mode=one)

    grid_spec = pltpu.PrefetchScalarGridSpec(
        num_scalar_prefetch=1, grid=(1,),
        in_specs=[pl.BlockSpec((rows, D_MODEL), lambda i, pr: (0, 0), pipeline_mode=one) for rows in BIG_ROWS],
        out_specs=[slab(PACK_A_ROWS), slab(PACK_B_ROWS)])
    return pl.pallas_call(
        body, name=name, grid_spec=grid_spec,
        out_shape=[jax.ShapeDtypeStruct((N_CHIPS, PACK_A_ROWS, D_MODEL), BF16),
                   jax.ShapeDtypeStruct((N_CHIPS, PACK_B_ROWS, D_MODEL), BF16)],
        compiler_params=_cparams(("arbitrary",)),
    )(place, *parts)


def _all_gather_pack(pack, *, name):
    def body(p_ref, o_ref, send_sems, recv_sems):
        start, forward, finish = _gather_stages(o_ref, send_sems, recv_sems)
        start()
        forward()
        finish()

    return pl.pallas_call(
        body, name=name,
        in_specs=[pl.BlockSpec(memory_space=pl.ANY)],
        out_specs=pl.BlockSpec(memory_space=pl.ANY),
        out_shape=jax.ShapeDtypeStruct(pack.shape, pack.dtype),
        input_output_aliases={0: 0},
        scratch_shapes=[pltpu.SemaphoreType.DMA((6,)), pltpu.SemaphoreType.DMA((6,))],
    )(pack)


def _shard_tile(i, c):
    return jnp.where(i < 3, 3 * c + i, jnp.where(i < 6, 3 + 3 * c + i, 12 + c))


def _rs_reduce(place, land, *, name):
    def body(place_ref, l_ref, o_ref):
        acc = l_ref[0].astype(F32)
        for d in range(1, 2 * N_CHIPS):
            acc = acc + l_ref[d].astype(F32)
        o_ref[...] = acc

    grid_spec = pltpu.PrefetchScalarGridSpec(
        num_scalar_prefetch=1, grid=(HALF_ROWS // MIX_HALF,),
        in_specs=[pl.BlockSpec((2 * N_CHIPS, MIX_HALF, D_MODEL), lambda i, pr: (0, i, 0))],
        out_specs=pl.BlockSpec((MIX_HALF, D_MODEL), lambda i, pr: (_shard_tile(i, pr[1]), 0)))
    return pl.pallas_call(
        body, name=name, grid_spec=grid_spec,
        out_shape=jax.ShapeDtypeStruct((PACK_ROWS, D_MODEL), F32),
        compiler_params=_cparams(("arbitrary",)),
    )(place, land)


def _small_all_reduce(packed, shard, *, name):
    rows = packed.shape[0]
    half = rows // 2

    def body(p_ref, sh_in_ref, o_ref, sh_ref, sib_ref, slots_ref, send_sems, recv_sems, share_send, share_recv):
        x, y, c, others = _mesh_place()
        me = 2 * x + y
        sibling = (x, y, 1 - c)
        share_start, share_finish = _share_stages(sh_ref, share_send, share_recv)
        share_start()

        def half_of(core):
            return pl.ds(pl.multiple_of(core * half, 8), half)

        def remote(k, src, dst, to):
            return pltpu.make_async_remote_copy(src_ref=src, dst_ref=dst, send_sem=send_sems.at[k],
                                                recv_sem=recv_sems.at[k], device_id=to, device_id_type=MESH)

        sib = remote(0, p_ref.at[half_of(1 - c)], sib_ref, sibling)
        sib.start()
        sib.wait()
        slots_ref[me] = p_ref[half_of(c), :] + sib_ref[...]
        sends = [remote(1 + j, slots_ref.at[me], slots_ref.at[me], (px, py, c)) for j, (px, py) in enumerate(others)]
        for cp in sends:
            cp.start()
        for j, (px, py) in enumerate(others):
            slab = slots_ref.at[2 * px + py]
            remote(1 + j, slab, slab, (px, py, c)).wait_recv()
        for cp in sends:
            cp.wait_send()
        o_ref[half_of(c), :] = (slots_ref[0] + slots_ref[1]) + (slots_ref[2] + slots_ref[3])
        back = remote(4, o_ref.at[half_of(c)], o_ref.at[half_of(c)], sibling)
        back.start()
        remote(4, o_ref.at[half_of(1 - c)], o_ref.at[half_of(1 - c)], sibling).wait_recv()
        back.wait_send()
        share_finish()

    vm = pl.BlockSpec(memory_space=pltpu.VMEM)
    hbm = pl.BlockSpec(memory_space=pl.ANY)
    return pl.pallas_call(
        body, name=name, in_specs=[vm, hbm], out_specs=[vm, hbm],
        out_shape=[jax.ShapeDtypeStruct((rows, 128), F32), jax.ShapeDtypeStruct(shard.shape, shard.dtype)],
        input_output_aliases={1: 1},
        scratch_shapes=[pltpu.VMEM((half, 128), F32), pltpu.VMEM((N_CHIPS, half, 128), F32),
                        pltpu.SemaphoreType.DMA((5,)), pltpu.SemaphoreType.DMA((5,)),
                        pltpu.SemaphoreType.DMA((3,)), pltpu.SemaphoreType.DMA((3,))],
    )(packed, shard)


def _adamw(w, g, m, v, *, g_row0, tile, name):
    rows, cols = w.shape
    assert g_row0 % tile == 0 and rows % tile == 0

    def body(w_ref, g_ref, m_ref, v_ref, go_ref, d_ref, nm_ref, nv_ref):
        g_v = g_ref[...]
        m_n = ADAM_B1 * m_ref[...] + (1.0 - ADAM_B1) * g_v
        v_n = ADAM_B2 * v_ref[...] + (1.0 - ADAM_B2) * (g_v * g_v)
        m_hat = m_n / (1.0 - ADAM_B1 ** ADAM_STEP)
        v_hat = v_n / (1.0 - ADAM_B2 ** ADAM_STEP)
        d_ref[...] = -ADAM_LR * (m_hat / (jnp.sqrt(v_hat) + ADAM_EPS) + ADAM_WD * w_ref[...])
        go_ref[...] = g_v
        nm_ref[...] = m_n
        nv_ref[...] = v_n

    spec = pl.BlockSpec((tile, cols), lambda i: (i, 0))
    gspec = pl.BlockSpec((tile, cols), lambda i: (g_row0 // tile + i, 0))
    shape = jax.ShapeDtypeStruct((rows, cols), F32)
    return pl.pallas_call(
        body, name=name, grid=(rows // tile,),
        in_specs=[spec, gspec, spec, spec], out_specs=[spec] * 4, out_shape=[shape] * 4,
        compiler_params=_cparams(("arbitrary",)),
    )(w, g, m, v)


def kernel(x, ffn1_norm_g, ffn1_w_gate, ffn1_w_up, ffn1_w_down, mix_norm_g, w_in, b_in, attn_sinks, gmlp_ln_g, gmlp_ln_b, gmlp_w_s, gmlp_b_s, attn_out_norm_g, gmlp_out_norm_g, w_out, b_out, ffn2_norm_g, ffn2_w_gate, ffn2_w_up, ffn2_w_down, final_norm_g, loss_target, m_ffn1_norm_g, m_ffn1_w_gate, m_ffn1_w_up, m_ffn1_w_down, m_mix_norm_g, m_w_in, m_b_in, m_attn_sinks, m_gmlp_ln_g, m_gmlp_ln_b, m_gmlp_w_s, m_gmlp_b_s, m_attn_out_norm_g, m_gmlp_out_norm_g, m_w_out, m_b_out, m_ffn2_norm_g, m_ffn2_w_gate, m_ffn2_w_up, m_ffn2_w_down, m_final_norm_g, v_ffn1_norm_g, v_ffn1_w_gate, v_ffn1_w_up, v_ffn1_w_down, v_mix_norm_g, v_w_in, v_b_in, v_attn_sinks, v_gmlp_ln_g, v_gmlp_ln_b, v_gmlp_w_s, v_gmlp_b_s, v_attn_out_norm_g, v_gmlp_out_norm_g, v_w_out, v_b_out, v_ffn2_norm_g, v_ffn2_w_gate, v_ffn2_w_up, v_ffn2_w_down, v_final_norm_g):
    f_args = dict(locals())
    weights = {n: f_args[n] for n in [nm for nm, _ in SMALL if nm != "loss"] + list(BIG)}
    shapes = {n: weights[n].shape for n in weights}
    shapes["loss"] = ()
    place = jnp.stack([2 * lax.axis_index("x") + lax.axis_index("y"), lax.axis_index("c")]).astype(jnp.int32)

    def with_cols(name, a):
        a2 = a.reshape(a.shape[-2], a.shape[-1])
        return a2.T if BIG_TRANSPOSED[BIG.index(name)] else a2

    def natural(name, a2):
        return (a2.T if BIG_TRANSPOSED[BIG.index(name)] else a2).reshape(shapes[name])

    pack_a, pack_b = _pack_cast(place, [with_cols(n, weights[n]) for n in BIG], name="pack_cast")
    pack_a = _all_gather_pack(pack_a, name="ag_weights")
    p = {n: weights[n].reshape(1, -1) for n in ("ffn1_norm_g", "mix_norm_g", "b_in", "gmlp_ln_g", "gmlp_ln_b",
                                                "attn_out_norm_g", "gmlp_out_norm_g", "b_out", "ffn2_norm_g",
                                                "final_norm_g")}
    p["attn_sinks"] = attn_sinks.reshape(N_Q_HEADS)
    p["gmlp_w_s"] = gmlp_w_s.reshape(GMLP_GROUPS, BLK, BLK)
    p["bs_full"] = jnp.broadcast_to(gmlp_b_s.reshape(GMLP_GROUPS, BLK).T[:, :, None],
                                    (BLK, GMLP_GROUPS, GROUP_DIM)).reshape(BLK, GMLP_W)

    loss_part, dx0, land, gs = _local_step(place, x[0], loss_target[0], p, pack_a, pack_b)

    gs["gmlp_b_s"] = jnp.sum(gs["gmlp_b_s"].reshape(BLK, GMLP_GROUPS, GROUP_DIM), axis=-1).T
    gs["attn_sinks"] = gs["attn_sinks"][:, 0]
    gs["loss"] = loss_part[0, 0]
    small_sum, shard = _small_all_reduce(_pack_small(gs), _rs_reduce(place, land, name="rs_reduce"),
                                         name="small_all_reduce")

    grad_w, delta, new_m, new_v = {}, {}, {}, {}
    off = 0
    for n, rows in zip(BIG, BIG_ROWS):
        res = _adamw(with_cols(n, weights[n]), shard, with_cols(n, f_args["m_" + n]), with_cols(n, f_args["v_" + n]),
                     g_row0=off, tile=FF_SH // 2 if rows == FF_SH else 64, name="adamw_" + n)
        grad_w[n], delta[n], new_m[n], new_v[n] = [natural(n, a) for a in res]
        off += rows
    sm = {k: {n: f_args[k + n] for n, _ in SMALL if n != "loss"} for k in ("", "m_", "v_")}
    for k in sm:
        sm[k]["loss"] = jnp.zeros((), F32)
    res = _adamw(_pack_small(sm[""]), small_sum, _pack_small(sm["m_"]), _pack_small(sm["v_"]),
                 g_row0=0, tile=SMALL_ROWS, name="adamw_small")
    small = _unpack_small(res[0], shapes)
    for dst, packed in ((grad_w, res[0]), (delta, res[1]), (new_m, res[2]), (new_v, res[3])):
        dst.update({n: a for n, a in _unpack_small(packed, shapes).items() if n != "loss"})

    order = ('ffn1_norm_g', 'ffn1_w_gate', 'ffn1_w_up', 'ffn1_w_down', 'mix_norm_g', 'w_in', 'b_in', 'attn_sinks',
             'gmlp_ln_g', 'gmlp_ln_b', 'gmlp_w_s', 'gmlp_b_s', 'attn_out_norm_g', 'gmlp_out_norm_g', 'w_out', 'b_out',
             'ffn2_norm_g', 'ffn2_w_gate', 'ffn2_w_up', 'ffn2_w_down', 'final_norm_g')
    return (small["loss"], dx0.reshape(x.shape), *[grad_w[n] for n in order], *[delta[n] for n in order],
            *[new_m[n] for n in order], *[new_v[n] for n in order])
```

```python
import functools

import jax
import jax.numpy as jnp
from jax import lax
from jax.experimental import pallas as pl
from jax.experimental.pallas import tpu as pltpu

F32 = jnp.float32
BF16 = jnp.bfloat16

D_MODEL = 1024
D_FF = 2816
N_CHIPS = 4
FF_SH = D_FF // N_CHIPS
N_Q_HEADS = 8
N_KV_HEADS = 2
REP = N_Q_HEADS // N_KV_HEADS
HEAD_DIM = 64
ATTN_W = 512
KV_W = 128
GMLP_W = 512
GMLP_GROUPS = 8
GROUP_DIM = 64
BLK = 128
MIX_FWD_BLOCKS = 2
MIX_BWD_BLOCKS = 4
IN_W = 1792
IN_SH = IN_W // N_CHIPS
OUT_SH = D_MODEL // N_CHIPS
EPS = 1e-6
FFN_RES = 0.5
ATTN_SCALE = HEAD_DIM ** -0.5

ADAM_LR = 0.001
ADAM_B1 = 0.9
ADAM_B2 = 0.999
ADAM_EPS = 1e-08
ADAM_WD = 0.01
ADAM_STEP = 10

V7X_VMEM_LIMIT = 56 * 1024 * 1024
MESH = pl.DeviceIdType.MESH


def _cparams(sem):
    return pltpu.CompilerParams(dimension_semantics=sem, vmem_limit_bytes=V7X_VMEM_LIMIT)


def _dot(a, b):
    return jnp.dot(a, b, preferred_element_type=F32)


def _dot_nt(a, b):
    return lax.dot_general(a, b, (((1,), (1,)), ((), ())), preferred_element_type=F32)


def _dot_tn(a, b):
    return lax.dot_general(a, b, (((0,), (0,)), ((), ())), preferred_element_type=F32)


def _rms(x, g):
    r = lax.rsqrt(jnp.mean(x * x, axis=-1, keepdims=True) + EPS)
    return x * r * g, r


def _rms_bwd(dh, x, r, g):
    gy = dh * g
    dx = r * gy - x * (r * r * r) * jnp.mean(gy * x, axis=-1, keepdims=True)
    dg = jnp.sum(dh * x * r, axis=0, keepdims=True)
    return dx, dg


def _const(shape):
    nd = len(shape)
    return pl.BlockSpec(shape, lambda *_: (0,) * nd)


def _rows(t, w):
    return pl.BlockSpec((t, w), lambda i: (i, 0))


PACK_ROWS = 7 * FF_SH
HALF_ROWS = PACK_ROWS // 2
FFN_HALF = 3 * FF_SH // 2
MIX_HALF = FF_SH // 2
PACK_A_ROWS = 3 * FF_SH
PACK_B_ROWS = 4 * FF_SH
BIG = ("ffn1_w_gate", "ffn1_w_up", "ffn1_w_down", "ffn2_w_gate", "ffn2_w_up", "ffn2_w_down", "w_in", "w_out")
BIG_ROWS = (FF_SH, FF_SH, FF_SH, FF_SH, FF_SH, FF_SH, IN_SH, OUT_SH)
BIG_TRANSPOSED = (True, True, False, True, True, False, True, False)

SMALL = (("ffn1_norm_g", 1024), ("mix_norm_g", 1024), ("b_in", 1792), ("attn_sinks", 8), ("gmlp_ln_g", 512),
         ("gmlp_ln_b", 512), ("gmlp_w_s", 131072), ("gmlp_b_s", 1024), ("attn_out_norm_g", 512),
         ("gmlp_out_norm_g", 512), ("b_out", 1024), ("ffn2_norm_g", 1024), ("final_norm_g", 1024), ("loss", 1))


def _small_rows(n):
    return -(-n // 1024) * 8


SMALL_USED_ROWS = sum(_small_rows(n) for _, n in SMALL)
SMALL_ROWS = -(-SMALL_USED_ROWS // 16) * 16


def _pack_small(parts):
    out = []
    for name, n in SMALL:
        flat = parts[name].reshape(-1).astype(F32)
        rows = _small_rows(n)
        out.append(jnp.pad(flat, (0, rows * 128 - n)).reshape(rows, 128))
    if SMALL_ROWS > SMALL_USED_ROWS:
        out.append(jnp.zeros((SMALL_ROWS - SMALL_USED_ROWS, 128), F32))
    return jnp.concatenate(out, axis=0)


def _unpack_small(packed, shapes):
    res, off = {}, 0
    for name, n in SMALL:
        rows = _small_rows(n)
        res[name] = packed[off:off + rows].reshape(-1)[:n].reshape(shapes[name])
        off += rows
    return res


def _ffn_tile(x, g, wg_ref, wu_ref, wd_ref, hb_ref, a_ref, b_ref):
    h, _ = _rms(x, g)
    hb = h.astype(BF16)
    hb_ref[...] = hb
    acc = jnp.zeros(x.shape, F32)
    for j in range(N_CHIPS):
        a = _dot_nt(hb, wg_ref[j])
        b = _dot_nt(hb, wu_ref[j])
        a_ref[j] = a
        b_ref[j] = b
        f = (a * jax.nn.sigmoid(a) * b).astype(BF16)
        acc = acc + _dot(f, wd_ref[j])
    return x + FFN_RES * acc


def _ffn_saved_specs(s, tile):
    ab = pl.BlockSpec((N_CHIPS, tile, FF_SH), lambda i: (0, i, 0))
    shape = jax.ShapeDtypeStruct((N_CHIPS, s, FF_SH), F32)
    return [_rows(tile, D_MODEL), ab, ab], [jax.ShapeDtypeStruct((s, D_MODEL), BF16), shape, shape]


def _ffn_weight_specs(k0):
    one = pl.Buffered(1)
    return [pl.BlockSpec((N_CHIPS, FF_SH, D_MODEL), functools.partial(lambda kk, i: (0, kk, 0), k0 + d),
                         pipeline_mode=one) for d in range(3)]


def _mesh_place():
    x, y, c = lax.axis_index("x"), lax.axis_index("y"), lax.axis_index("c")
    others = [(1 - x, y), (x, 1 - y), (1 - x, 1 - y)]
    return x, y, c, others


def _gather_stages(o_ref, send_sems, recv_sems):
    x, y, c, others = _mesh_place()
    me = 2 * x + y
    sibling = (x, y, 1 - c)
    half_rows = o_ref.shape[1] // 2

    def half(slab, core):
        return o_ref.at[slab, pl.ds(pl.multiple_of(core * half_rows, 16), half_rows)]

    def copy(k, rows, to):
        return pltpu.make_async_remote_copy(src_ref=rows, dst_ref=rows, send_sem=send_sems.at[k],
                                            recv_sem=recv_sems.at[k], device_id=to, device_id_type=MESH)

    first = [copy(j, half(me, c), (px, py, c)) for j, (px, py) in enumerate(others)]
    passed = [copy(3 + j, half(2 * px + py, c), sibling) for j, (px, py) in enumerate(others)]

    def start():
        for cp in first:
            cp.start()

    def forward():
        for j, (px, py) in enumerate(others):
            copy(j, half(2 * px + py, c), (px, py, c)).wait_recv()
            passed[j].start()

    def finish():
        for j, (px, py) in enumerate(others):
            copy(3 + j, half(2 * px + py, 1 - c), sibling).wait_recv()
        for cp in first + passed:
            cp.wait_send()

    return start, forward, finish


def _swiglu_slab(hb, wg, wu, wd):
    a = _dot_nt(hb, wg)
    b = _dot_nt(hb, wu)
    return a, b, _dot((a * jax.nn.sigmoid(a) * b).astype(BF16), wd)


def _ffn1_own(x, g, own, gather, *, tile, name):
    s = x.shape[0]
    nt = s // tile

    def body(x_ref, g_ref, wg_ref, wu_ref, wd_ref, gin_ref, hb_ref, a_ref, b_ref, p_ref, gat_ref, send_sems, recv_sems):
        i = pl.program_id(0)
        start, forward, finish = _gather_stages(gat_ref, send_sems, recv_sems)
        pl.when(i == 0)(start)
        h, _ = _rms(x_ref[...], g_ref[...])
        hb = h.astype(BF16)
        hb_ref[...] = hb
        a_ref[...], b_ref[...], p_ref[...] = _swiglu_slab(hb, wg_ref[...], wu_ref[...], wd_ref[...])

        @pl.when(i == nt - 1)
        def _():
            forward()
            finish()

    one = pl.Buffered(1)
    wspecs = [pl.BlockSpec((FF_SH, D_MODEL), functools.partial(lambda kk, i: (kk, 0), k), pipeline_mode=one)
              for k in range(3)]
    hbm = pl.BlockSpec(memory_space=pl.ANY)
    return pl.pallas_call(
        body, name=name, grid=(nt,),
        in_specs=[_rows(tile, D_MODEL), _const((1, D_MODEL))] + wspecs + [hbm],
        out_specs=[_rows(tile, D_MODEL), _rows(tile, FF_SH), _rows(tile, FF_SH), _rows(tile, D_MODEL), hbm],
        out_shape=[jax.ShapeDtypeStruct((s, D_MODEL), BF16), jax.ShapeDtypeStruct((s, FF_SH), F32),
                   jax.ShapeDtypeStruct((s, FF_SH), F32), jax.ShapeDtypeStruct((s, D_MODEL), F32),
                   jax.ShapeDtypeStruct(gather.shape, gather.dtype)],
        input_output_aliases={5: 4},
        scratch_shapes=[pltpu.SemaphoreType.DMA((6,)), pltpu.SemaphoreType.DMA((6,))],
        compiler_params=_cparams(("arbitrary",)),
    )(x, g, own, own, own, gather)


def _ffn1_others(place, x, hb, p_own, pack, gather, *, tile, name):
    s = x.shape[0]
    nt = s // tile
    forward_at = max(nt - 6, 0)

    def body(place_ref, x_ref, hb_ref, p_ref, *rest):
        w_refs = rest[:9]
        gin_ref, o_ref, a_ref, b_ref, gat_ref, send_sems, recv_sems = rest[9:]
        i = pl.program_id(0)
        start, forward, finish = _gather_stages(gat_ref, send_sems, recv_sems)
        pl.when(i == 0)(start)
        hb = hb_ref[...]
        acc = p_ref[...]
        for t in range(N_CHIPS - 1):
            a_ref[t], b_ref[t], part = _swiglu_slab(hb, w_refs[3 * t][0], w_refs[3 * t + 1][0], w_refs[3 * t + 2][0])
            acc = acc + part
        o_ref[...] = x_ref[...] + FFN_RES * acc
        pl.when(i == forward_at)(forward)
        pl.when(i == nt - 1)(finish)

    one = pl.Buffered(1)

    def wspec(t, kk):
        return pl.BlockSpec((1, FF_SH, D_MODEL), lambda i, pr: (jnp.bitwise_xor(pr[0], t + 1), kk, 0),
                            pipeline_mode=one)

    rows = lambda w: pl.BlockSpec((tile, w), lambda i, pr: (i, 0))
    ab = pl.BlockSpec((N_CHIPS - 1, tile, FF_SH), lambda i, pr: (0, i, 0))
    ab_shape = jax.ShapeDtypeStruct((N_CHIPS - 1, s, FF_SH), F32)
    hbm = pl.BlockSpec(memory_space=pl.ANY)
    grid_spec = pltpu.PrefetchScalarGridSpec(
        num_scalar_prefetch=1, grid=(nt,),
        in_specs=[rows(D_MODEL), rows(D_MODEL), rows(D_MODEL)]
                 + [wspec(t, kk) for t in range(N_CHIPS - 1) for kk in range(3)] + [hbm],
        out_specs=[rows(D_MODEL), ab, ab, hbm],
        scratch_shapes=[pltpu.SemaphoreType.DMA((6,)), pltpu.SemaphoreType.DMA((6,))])
    return pl.pallas_call(
        body, name=name, grid_spec=grid_spec,
        out_shape=[jax.ShapeDtypeStruct(x.shape, F32), ab_shape, ab_shape,
                   jax.ShapeDtypeStruct(gather.shape, gather.dtype)],
        input_output_aliases={13: 3},
        compiler_params=_cparams(("arbitrary",)),
    )(place, x, hb, p_own, *([pack] * 9), gather)


def _ffn_fwd_loss(x, g, pack, k0, gf, tgt, *, tile, name):
    s = x.shape[0]

    def body(x_ref, g_ref, wg_ref, wu_ref, wd_ref, gf_ref, t_ref, dx_ref, loss_ref, dgf_ref, hb_ref, a_ref, b_ref,
             do_ref):
        @pl.when(pl.program_id(0) == 0)
        def _():
            loss_ref[...] = jnp.zeros_like(loss_ref)
            dgf_ref[...] = jnp.zeros_like(dgf_ref)

        x3 = _ffn_tile(x_ref[...], g_ref[...], wg_ref, wu_ref, wd_ref, hb_ref, a_ref, b_ref)
        gf_v = gf_ref[...]
        out, r = _rms(x3, gf_v)
        diff = out - t_ref[...]
        part = jnp.sum(jnp.sum(diff * diff, axis=-1, keepdims=True), axis=0, keepdims=True)
        loss_ref[...] += jnp.broadcast_to(part * (0.5 / D_MODEL), loss_ref.shape)
        dx, dg = _rms_bwd(diff * (1.0 / D_MODEL), x3, r, gf_v)
        dx_ref[...] = dx
        do_ref[...] = (FFN_RES * dx).astype(BF16)
        dgf_ref[...] += dg

    saved_specs, saved_shapes = _ffn_saved_specs(s, tile)
    return pl.pallas_call(
        body, name=name, grid=(s // tile,),
        in_specs=[_rows(tile, D_MODEL), _const((1, D_MODEL))] + _ffn_weight_specs(k0)
                 + [_const((1, D_MODEL)), _rows(tile, D_MODEL)],
        out_specs=[_rows(tile, D_MODEL), _const((1, 128)), _const((1, D_MODEL))] + saved_specs
                  + [_rows(tile, D_MODEL)],
        out_shape=[jax.ShapeDtypeStruct(x.shape, F32),
                   jax.ShapeDtypeStruct((1, 128), F32),
                   jax.ShapeDtypeStruct((1, D_MODEL), F32)] + saved_shapes
                  + [jax.ShapeDtypeStruct(x.shape, BF16)],
        compiler_params=_cparams(("arbitrary",)),
    )(x, g, pack, pack, pack, gf, tgt)


def _ffn_bwd(place, hb, a, b, do, pack, region, land, mix_grads, own_ab=None, *, tile, name):
    s = hb.shape[0]
    nt = s // tile
    land_rows = pl.ds(region * FFN_HALF, FFN_HALF)
    mix_rows = pl.ds(2 * FFN_HALF, MIX_HALF)
    with_mix = mix_grads is not None
    with_land = land is not None
    with_own = own_ab is not None
    n_others = 2 * N_CHIPS - 1

    def body(place_ref, hb_ref, a_ref, b_ref, do_ref, wg_ref, wu_ref, wd_ref, *rest):
        rest = list(rest)
        own_a_ref, own_b_ref = (rest.pop(0), rest.pop(0)) if with_own else (None, None)
        mix_ref = rest.pop(0) if with_mix else None
        if with_land:
            rest.pop(0)
        dhp_ref, land_ref, acc_ref, stage_ref, send_sems, recv_sem, local_sem = rest[:7]
        t, i = pl.program_id(0), pl.program_id(1)
        xi, yi, c = lax.axis_index("x"), lax.axis_index("y"), lax.axis_index("c")
        dev = 4 * xi + 2 * yi + c
        tt = (t + 1) % N_CHIPS
        tx, ty = jnp.bitwise_xor(xi, tt // 2), jnp.bitwise_xor(yi, tt % 2)

        def remote(src, dst, ssem, rsem, to):
            return pltpu.make_async_remote_copy(src_ref=src, dst_ref=dst, send_sem=ssem, recv_sem=rsem,
                                                device_id=to, device_id_type=MESH)

        def stage_half(h):
            return stage_ref.at[pl.ds(pl.multiple_of(h * FFN_HALF, 16), FFN_HALF)]

        if with_mix:
            mix_send, mix_recv, mix_local = rest[7:10]

            @pl.when(jnp.logical_and(t == 0, i == 0))
            def _():
                for chip in range(N_CHIPS):
                    for h in range(2):
                        src = mix_ref.at[chip, pl.ds(h * MIX_HALF, MIX_HALF)]
                        dst = land_ref.at[dev, mix_rows]
                        mine = jnp.logical_and(2 * xi + yi == chip, c == h)

                        @pl.when(mine)
                        def _():
                            pltpu.make_async_copy(src, dst, mix_local).start()

                        @pl.when(jnp.logical_not(mine))
                        def _():
                            remote(src, dst, mix_send, mix_recv, (chip // 2, chip % 2, h)).start()

        @pl.when(i == 0)
        def _():
            acc_ref[...] = jnp.zeros_like(acc_ref)

        hb = hb_ref[...]
        dob = do_ref[...]
        wg_j, wu_j, wd_j = wg_ref[0], wu_ref[0], wd_ref[0]
        a = a_ref[0]
        b = b_ref[0]
        if with_own:
            a = jnp.where(t == N_CHIPS - 1, own_a_ref[...], a)
            b = jnp.where(t == N_CHIPS - 1, own_b_ref[...], b)
        sg = jax.nn.sigmoid(a)
        sa = a * sg
        fb = (sa * b).astype(BF16)
        df = _dot_nt(dob, wd_j)
        dbb = (df * sa).astype(BF16)
        dab = (df * b * (sg + sa * (1.0 - sg))).astype(BF16)
        dhp_ref[0] = (_dot(dab, wg_j) + _dot(dbb, wu_j)).astype(BF16)
        acc_ref[0:FF_SH, :] += _dot_tn(dab, hb)
        acc_ref[FF_SH:2 * FF_SH, :] += _dot_tn(dbb, hb)
        acc_ref[2 * FF_SH:3 * FF_SH, :] += _dot_tn(fb, dob)

        @pl.when(i == nt - 1)
        def _():
            dst = land_ref.at[dev, land_rows]

            @pl.when(t > 0)
            def _():
                for h in range(2):
                    remote(stage_half(h), dst, send_sems.at[h], recv_sem, (tx, ty, h)).wait_send()

            def cast_rows(r, carry):
                rows = pl.ds(pl.multiple_of(r * MIX_HALF, 16), MIX_HALF)
                stage_ref[rows, :] = acc_ref[rows, :].astype(BF16)
                return carry

            lax.fori_loop(0, 3 * FF_SH // MIX_HALF, cast_rows, 0)

            @pl.when(t < N_CHIPS - 1)
            def _():
                for h in range(2):
                    remote(stage_half(h), dst, send_sems.at[h], recv_sem, (tx, ty, h)).start()

            @pl.when(t == N_CHIPS - 1)
            def _():
                own = pltpu.make_async_copy(stage_half(c), dst, local_sem)
                own.start()
                sib = remote(stage_half(1 - c), dst, send_sems.at[0], recv_sem, (xi, yi, 1 - c))
                sib.start()
                sib.wait_send()
                own.wait()
                arrivals = land_ref.at[pl.ds(0, n_others), land_rows]
                remote(arrivals, arrivals, send_sems.at[0], recv_sem, (xi, yi, 1 - c)).wait_recv()
                if with_mix:
                    seven = land_ref.at[pl.ds(0, n_others), mix_rows]
                    both = remote(seven, seven, mix_send, mix_recv, (xi, yi, 1 - c))
                    both.wait_send()
                    both.wait_recv()
                    pltpu.make_async_copy(mix_ref.at[0, pl.ds(0, MIX_HALF)], land_ref.at[dev, mix_rows],
                                          mix_local).wait()

    def wspec(kk):
        return pl.BlockSpec((1, FF_SH, D_MODEL),
                            lambda t, i, pr: (jnp.bitwise_xor(pr[0], (t + 1) % N_CHIPS), kk, 0))

    xspec = pl.BlockSpec((tile, D_MODEL), lambda t, i, pr: (i, 0))
    if with_own:
        abspec = pl.BlockSpec((1, tile, FF_SH), lambda t, i, pr: (jnp.minimum(t, N_CHIPS - 2), i, 0))
    else:
        abspec = pl.BlockSpec((1, tile, FF_SH), lambda t, i, pr: (jnp.bitwise_xor(pr[0], (t + 1) % N_CHIPS), i, 0))
    hbm = pl.BlockSpec(memory_space=pl.ANY)
    in_specs = [xspec, abspec, abspec, xspec, wspec(0), wspec(1), wspec(2)]
    operands = [place, hb, a, b, do, pack, pack, pack]
    if with_own:
        own_spec = pl.BlockSpec((tile, FF_SH), lambda t, i, pr: (jnp.where(t == N_CHIPS - 1, i, 0), 0))
        in_specs += [own_spec, own_spec]
        operands += list(own_ab)
    scratch = [pltpu.VMEM((3 * FF_SH, D_MODEL), F32), pltpu.VMEM((3 * FF_SH, D_MODEL), BF16),
               pltpu.SemaphoreType.DMA((2,)), pltpu.SemaphoreType.DMA, pltpu.SemaphoreType.DMA]
    if with_mix:
        in_specs.append(hbm)
        operands.append(mix_grads)
        scratch += [pltpu.SemaphoreType.DMA, pltpu.SemaphoreType.DMA, pltpu.SemaphoreType.DMA]
    aliases = {}
    if with_land:
        in_specs.append(hbm)
        operands.append(land)
        aliases = {len(operands) - 1: 1}
    grid_spec = pltpu.PrefetchScalarGridSpec(
        num_scalar_prefetch=1, grid=(N_CHIPS, nt), in_specs=in_specs,
        out_specs=[pl.BlockSpec((1, tile, D_MODEL), lambda t, i, pr: (t, i, 0)), hbm],
        scratch_shapes=scratch)
    return pl.pallas_call(
        body, name=name, grid_spec=grid_spec,
        out_shape=[jax.ShapeDtypeStruct((N_CHIPS, s, D_MODEL), BF16),
                   jax.ShapeDtypeStruct((2 * N_CHIPS, HALF_ROWS, D_MODEL), BF16)],
        input_output_aliases=aliases,
        compiler_params=_cparams(("arbitrary", "arbitrary")),
    )(*operands)


def _mix_grads_pack(dw_in_t, dw_out, *, name):
    def body(a_ref, b_ref, o_ref):
        o_ref[0, 0:IN_SH, :] = a_ref[0].astype(BF16)
        o_ref[0, IN_SH:FF_SH, :] = b_ref[0].astype(BF16)

    return pl.pallas_call(
        body, name=name, grid=(N_CHIPS,),
        in_specs=[pl.BlockSpec((1, IN_SH, D_MODEL), lambda j: (j, 0, 0)),
                  pl.BlockSpec((1, OUT_SH, D_MODEL), lambda j: (j, 0, 0))],
        out_specs=pl.BlockSpec((1, FF_SH, D_MODEL), lambda j: (j, 0, 0)),
        out_shape=jax.ShapeDtypeStruct((N_CHIPS, FF_SH, D_MODEL), BF16),
        compiler_params=_cparams(("arbitrary",)),
    )(dw_in_t.reshape(N_CHIPS, IN_SH, D_MODEL), dw_out.reshape(N_CHIPS, OUT_SH, D_MODEL))


def _share_stages(o_ref, send_sems, recv_sems):
    x, y, c, _ = _mesh_place()

    def rows(k, core):
        if k < 2:
            return o_ref.at[pl.ds(pl.multiple_of(k * 2 * FFN_HALF + core * FFN_HALF, 8), FFN_HALF)]
        return o_ref.at[pl.ds(pl.multiple_of(4 * FFN_HALF + core * MIX_HALF, 8), MIX_HALF)]

    def copy(k, core):
        return pltpu.make_async_remote_copy(src_ref=rows(k, core), dst_ref=rows(k, core), send_sem=send_sems.at[k],
                                            recv_sem=recv_sems.at[k], device_id=(x, y, 1 - c), device_id_type=MESH)

    sends = [copy(k, c) for k in range(3)]

    def start():
        for cp in sends:
            cp.start()

    def finish():
        for k in range(3):
            copy(k, 1 - c).wait_recv()
        for cp in sends:
            cp.wait_send()

    return start, finish


def _norm_bwd(dhp, x, dy, g, *, tile, name):
    s = x.shape[0]

    def body(dhp_ref, x_ref, dy_ref, g_ref, dx_ref, dg_ref):
        @pl.when(pl.program_id(0) == 0)
        def _():
            dg_ref[...] = jnp.zeros_like(dg_ref)

        dh = ((dhp_ref[0].astype(F32) + dhp_ref[1].astype(F32))
              + (dhp_ref[2].astype(F32) + dhp_ref[3].astype(F32)))
        x_v = x_ref[...]
        r = lax.rsqrt(jnp.mean(x_v * x_v, axis=-1, keepdims=True) + EPS)
        dx, dg = _rms_bwd(dh, x_v, r, g_ref[...])
        dx_ref[...] = dy_ref[...] + dx
        dg_ref[...] += dg

    return pl.pallas_call(
        body, name=name, grid=(s // tile,),
        in_specs=[pl.BlockSpec((N_CHIPS, tile, D_MODEL), lambda i: (0, i, 0)),
                  _rows(tile, D_MODEL), _rows(tile, D_MODEL), _const((1, D_MODEL))],
        out_specs=[_rows(tile, D_MODEL), _const((1, D_MODEL))],
        out_shape=[jax.ShapeDtypeStruct(x.shape, F32), jax.ShapeDtypeStruct((1, D_MODEL), F32)],
        compiler_params=_cparams(("arbitrary",)),
    )(dhp, x, dy, g)


def _mix_in_bwd(x, dy, dq, dk, dv, dz, g, w_in_t, *, tile, name):
    s = x.shape[0]

    def body(x_ref, dy_ref, dq_ref, dk_ref, dv_ref, dz_ref, g_ref, w_ref, dx_ref, dw_ref, db_ref, dg_ref, do_ref):
        @pl.when(pl.program_id(0) == 0)
        def _():
            dw_ref[...] = jnp.zeros_like(dw_ref)
            db_ref[...] = jnp.zeros_like(db_ref)
            dg_ref[...] = jnp.zeros_like(dg_ref)

        dproj = jnp.concatenate([dq_ref[...], dk_ref[...], dv_ref[...], dz_ref[...]], axis=-1)
        db_ref[...] += jnp.sum(dproj, axis=0, keepdims=True)
        dpb = dproj.astype(BF16)
        x_v = x_ref[...]
        g_v = g_ref[...]
        h, r = _rms(x_v, g_v)
        dw_ref[...] += _dot_tn(dpb, h.astype(BF16))
        dh = _dot(dpb, w_ref[...])
        dxn, dg = _rms_bwd(dh, x_v, r, g_v)
        dx = dy_ref[...] + dxn
        dx_ref[...] = dx
        do_ref[...] = (FFN_RES * dx).astype(BF16)
        dg_ref[...] += dg

    return pl.pallas_call(
        body, name=name, grid=(s // tile,),
        in_specs=[_rows(tile, D_MODEL), _rows(tile, D_MODEL), _rows(tile, ATTN_W), _rows(tile, KV_W),
                  _rows(tile, KV_W), _rows(tile, 2 * GMLP_W), _const((1, D_MODEL)), _const((IN_W, D_MODEL))],
        out_specs=[_rows(tile, D_MODEL), _const((IN_W, D_MODEL)), _const((1, IN_W)), _const((1, D_MODEL)),
                   _rows(tile, D_MODEL)],
        out_shape=[jax.ShapeDtypeStruct(x.shape, F32), jax.ShapeDtypeStruct((IN_W, D_MODEL), F32),
                   jax.ShapeDtypeStruct((1, IN_W), F32), jax.ShapeDtypeStruct((1, D_MODEL), F32),
                   jax.ShapeDtypeStruct(x.shape, BF16)],
        compiler_params=_cparams(("arbitrary",)),
    )(x, dy, dq, dk, dv, dz, g, w_in_t)


_GELU_C = 0.7978845608028654
_GELU_A = 0.044715


def _gelu_tanh(x):
    x2 = x * x
    return jnp.tanh(_GELU_C * (x + _GELU_A * (x2 * x))), x2


def _band(ref, i):
    prev = jnp.maximum(i - 1, 0)
    return jnp.concatenate([ref[pl.ds(pl.multiple_of(prev * BLK, BLK), BLK), :],
                            ref[pl.ds(pl.multiple_of(i * BLK, BLK), BLK), :]], axis=0)


def _key_in_block():
    return lax.broadcasted_iota(jnp.int32, (BLK, BLK), 0) <= lax.broadcasted_iota(jnp.int32, (BLK, BLK), 1)


def _fold(band, own):
    return jnp.where(own, band[BLK:], band[:BLK])


def _unfold(a, own):
    zero = jnp.zeros_like(a)
    return jnp.concatenate([jnp.where(own, zero, a), jnp.where(own, a, zero)], axis=0).astype(BF16)


def _attn_fwd(q, kb, vb, i, sink_ref):
    own = _key_in_block()
    outs, saved = [], []
    for h in range(N_Q_HEADS):
        cols = slice((h // REP) * HEAD_DIM, (h // REP + 1) * HEAD_DIM)
        s2 = _dot_nt(kb[:, cols], q[:, h * HEAD_DIM:(h + 1) * HEAD_DIM])
        sc = jnp.where(own, s2[BLK:], jnp.where(i > 0, s2[:BLK], -jnp.inf))
        sink = sink_ref[h]
        m = jnp.maximum(jnp.max(sc, axis=0, keepdims=True), sink)
        p = jnp.exp(sc - m)
        es = jnp.exp(sink - m)
        inv = 1.0 / (jnp.sum(p, axis=0, keepdims=True) + es)
        pn = p * inv
        band = _unfold(pn, own)
        outs.append(_dot_tn(band, vb[:, cols]))
        saved.append((pn, band, es * inv))
    return jnp.concatenate(outs, axis=-1), saved


def _tril_mask():
    t = lax.broadcasted_iota(jnp.int32, (BLK, BLK), 0)
    s_ = lax.broadcasted_iota(jnp.int32, (BLK, BLK), 1)
    return s_ <= t


def _gmlp_fwd_parts(zg, lng, lnb, ws_ref, bs_full):
    th, zg2 = _gelu_tanh(zg)
    z = 0.5 * zg * (1.0 + th)
    u = z[:, :GMLP_W]
    zv = z[:, GMLP_W:]
    mu = jnp.mean(zv, axis=-1, keepdims=True)
    zc = zv - mu
    rstd = lax.rsqrt(jnp.mean(zc * zc, axis=-1, keepdims=True) + EPS)
    xh = zc * rstd
    vvb = (xh * lng + lnb).astype(BF16)
    tril = _tril_mask()
    wms, parts = [], []
    for gi in range(GMLP_GROUPS):
        wm = jnp.where(tril, ws_ref[gi], 0.0).astype(BF16)
        wms.append(wm)
        parts.append(_dot(wm, vvb[:, gi * GROUP_DIM:(gi + 1) * GROUP_DIM]))
    mixed = jnp.concatenate(parts, axis=-1) + bs_full
    gelu_grad = 0.5 * (1.0 + th) + 0.5 * zg * (1.0 - th * th) * (_GELU_C * (1.0 + 3.0 * _GELU_A * zg2))
    return u, xh, rstd, vvb, wms, mixed, gelu_grad


def _mixer_fwd(x1, g, w_in_t, b_in, sinks, lng, lnb, w_s, bs_full, gao, ggo, w_out, b_out, *, name):
    s = x1.shape[0]
    nb = min(MIX_FWD_BLOCKS, s // BLK)
    step_rows = nb * BLK
    last = s // step_rows - 1

    def tile_of(i, lag):
        return jnp.clip(i - lag, 0, last)

    def body(sink_ref, xa_ref, xc_ref, g_ref, wi_ref, bi_ref, lng_ref, lnb_ref, ws_ref, bs_ref, gao_ref, ggo_ref,
             wo_ref, bo_ref, q_ref, k_ref, v_ref, z_ref, y_ref, o_ref, qs_ref, zs_ref, ys_ref):
        i = pl.program_id(0)

        @pl.when(i == 0)
        def _():
            for ref in (k_ref, v_ref, qs_ref, zs_ref, ys_ref):
                ref[...] = jnp.zeros_like(ref)

        slot_a, slot_b, slot_c = i % 2, (i + 1) % 2, i % 2

        o_ref[...] = xc_ref[...] + (_dot(ys_ref[slot_c], wo_ref[...]) + bo_ref[...])

        tile_b = tile_of(i, 1)
        for b in range(nb):
            blk = tile_b * nb + b
            rows = slice(b * BLK, (b + 1) * BLK)
            y_attn, _ = _attn_fwd(qs_ref[slot_b, rows, :], _band(k_ref, blk), _band(v_ref, blk), blk, sink_ref)
            u, _, _, _, _, mixed, _ = _gmlp_fwd_parts(zs_ref[slot_b, rows, :], lng_ref[...], lnb_ref[...], ws_ref,
                                                      bs_ref[...])
            ya, _ = _rms(y_attn, gao_ref[...])
            yg, _ = _rms(u * mixed, ggo_ref[...])
            y_blk = jnp.concatenate([ya, yg], axis=-1).astype(BF16)
            y_ref[rows, :] = y_blk
            ys_ref[slot_b, rows, :] = y_blk

        h, _ = _rms(xa_ref[...], g_ref[...])
        proj = _dot_nt(h.astype(BF16), wi_ref[...]) + bi_ref[...]
        q_t = (proj[:, :ATTN_W] * ATTN_SCALE).astype(BF16)
        z_t = proj[:, ATTN_W + 2 * KV_W:]
        here = pl.ds(pl.multiple_of(tile_of(i, 0) * step_rows, step_rows), step_rows)
        q_ref[...] = q_t
        z_ref[...] = z_t
        qs_ref[slot_a] = q_t
        zs_ref[slot_a] = z_t
        k_ref[here, :] = proj[:, ATTN_W:ATTN_W + KV_W].astype(BF16)
        v_ref[here, :] = proj[:, ATTN_W + KV_W:ATTN_W + 2 * KV_W].astype(BF16)

    def lagged(width, lag):
        return pl.BlockSpec((step_rows, width), lambda i: (tile_of(i, lag), 0))

    return pl.pallas_call(
        body, name=name, grid=(last + 3,),
        in_specs=[pl.BlockSpec(memory_space=pltpu.SMEM),
                  lagged(D_MODEL, 0), lagged(D_MODEL, 2), _const((1, D_MODEL)), _const((IN_W, D_MODEL)),
                  _const((1, IN_W)), _const((1, GMLP_W)), _const((1, GMLP_W)), _const((GMLP_GROUPS, BLK, BLK)),
                  _const((BLK, GMLP_W)), _const((1, ATTN_W)), _const((1, GMLP_W)), _const((D_MODEL, D_MODEL)),
                  _const((1, D_MODEL))],
        out_specs=[lagged(ATTN_W, 0), _const((s, KV_W)), _const((s, KV_W)), lagged(2 * GMLP_W, 0),
                   lagged(D_MODEL, 1), lagged(D_MODEL, 2)],
        out_shape=[jax.ShapeDtypeStruct((s, ATTN_W), BF16), jax.ShapeDtypeStruct((s, KV_W), BF16),
                   jax.ShapeDtypeStruct((s, KV_W), BF16), jax.ShapeDtypeStruct((s, 2 * GMLP_W), F32),
                   jax.ShapeDtypeStruct((s, D_MODEL), BF16), jax.ShapeDtypeStruct((s, D_MODEL), F32)],
        scratch_shapes=[pltpu.VMEM((2, step_rows, ATTN_W), BF16), pltpu.VMEM((2, step_rows, 2 * GMLP_W), F32),
                        pltpu.VMEM((2, step_rows, D_MODEL), BF16)],
        compiler_params=_cparams(("arbitrary",)),
    )(sinks, x1, x1, g, w_in_t, b_in, lng, lnb, w_s, bs_full, gao, ggo, w_out, b_out)


def _norm_bwd_mix_out(dhp, x, dy, g, yb, w_out, *, tile, name):
    s = x.shape[0]

    def body(dhp_ref, x_ref, dy_ref, g_ref, y_ref, w_ref, dx_ref, dg_ref, dyy_ref, dw_ref, db_ref):
        @pl.when(pl.program_id(0) == 0)
        def _():
            dg_ref[...] = jnp.zeros_like(dg_ref)
            dw_ref[...] = jnp.zeros_like(dw_ref)
            db_ref[...] = jnp.zeros_like(db_ref)

        dh = ((dhp_ref[0].astype(F32) + dhp_ref[1].astype(F32))
              + (dhp_ref[2].astype(F32) + dhp_ref[3].astype(F32)))
        x_v = x_ref[...]
        r = lax.rsqrt(jnp.mean(x_v * x_v, axis=-1, keepdims=True) + EPS)
        dxn, dg = _rms_bwd(dh, x_v, r, g_ref[...])
        dx = dy_ref[...] + dxn
        dx_ref[...] = dx
        dg_ref[...] += dg
        dxb = dx.astype(BF16)
        db_ref[...] += jnp.sum(dx, axis=0, keepdims=True)
        dw_ref[...] += _dot_tn(y_ref[...], dxb)
        dyy_ref[...] = _dot_nt(dxb, w_ref[...])

    return pl.pallas_call(
        body, name=name, grid=(s // tile,),
        in_specs=[pl.BlockSpec((N_CHIPS, tile, D_MODEL), lambda i: (0, i, 0)),
                  _rows(tile, D_MODEL), _rows(tile, D_MODEL), _const((1, D_MODEL)), _rows(tile, D_MODEL),
                  _const((D_MODEL, D_MODEL))],
        out_specs=[_rows(tile, D_MODEL), _const((1, D_MODEL)), _rows(tile, D_MODEL), _const((D_MODEL, D_MODEL)),
                   _const((1, D_MODEL))],
        out_shape=[jax.ShapeDtypeStruct(x.shape, F32), jax.ShapeDtypeStruct((1, D_MODEL), F32),
                   jax.ShapeDtypeStruct(x.shape, F32), jax.ShapeDtypeStruct((D_MODEL, D_MODEL), F32),
                   jax.ShapeDtypeStruct((1, D_MODEL), F32)],
        compiler_params=_cparams(("arbitrary",)),
    )(dhp, x, dy, g, yb, w_out)


def _mix_core_bwd(dyy, q, k, v, zg, sinks, lng, lnb, w_s, bs_full, gao, ggo, *, name):
    s = dyy.shape[0]
    nb = min(MIX_BWD_BLOCKS, s // BLK)
    nsteps = s // (nb * BLK)

    def body(*refs):
        accumulators = refs[13:15] + refs[16:]

        @pl.when(pl.program_id(0) == 0)
        def _():
            for ref in accumulators:
                ref[...] = jnp.zeros_like(ref)

        for b in range(nb):
            one_block(pl.program_id(0) * nb + b, slice(b * BLK, (b + 1) * BLK), *refs)

        @pl.when(pl.program_id(0) == nsteps - 1)
        def _():
            tril = _tril_mask()
            for gi in range(GMLP_GROUPS):
                refs[20][gi] = jnp.where(tril, refs[20][gi], 0.0)

    def one_block(i, rows, sink_ref, dyy_ref, q_ref, k_ref, v_ref, z_ref, lng_ref, lnb_ref, ws_ref, bs_ref, gao_ref,
                  ggo_ref, dq_ref, dk_ref, dv_ref, dz_ref, dgao_ref, dggo_ref, dlng_ref, dlnb_ref, dws_ref, dms_ref,
                  dsk_ref):
        q_v = q_ref[rows, :]
        kb = _band(k_ref, i)
        vb = _band(v_ref, i)
        lng_v = lng_ref[...]
        gao_v = gao_ref[...]
        ggo_v = ggo_ref[...]

        y_attn, probs = _attn_fwd(q_v, kb, vb, i, sink_ref)
        u, xh, rstd, vvb, wms, mixed, gelu_grad = _gmlp_fwd_parts(z_ref[rows, :], lng_v, lnb_ref[...], ws_ref,
                                                                  bs_ref[...])
        y_gmlp = u * mixed
        ra = lax.rsqrt(jnp.mean(y_attn * y_attn, axis=-1, keepdims=True) + EPS)
        rg = lax.rsqrt(jnp.mean(y_gmlp * y_gmlp, axis=-1, keepdims=True) + EPS)

        dyy = dyy_ref[rows, :]
        d_attn, dgao = _rms_bwd(dyy[:, :ATTN_W], y_attn, ra, gao_v)
        d_gmlp, dggo = _rms_bwd(dyy[:, ATTN_W:], y_gmlp, rg, ggo_v)
        dgao_ref[...] += dgao
        dggo_ref[...] += dggo

        du = d_gmlp * mixed
        dmixed = d_gmlp * u
        dms_ref[...] += dmixed
        dmb = dmixed.astype(BF16)
        dvv_parts = []
        for gi in range(GMLP_GROUPS):
            sl = slice(gi * GROUP_DIM, (gi + 1) * GROUP_DIM)
            dws_ref[gi] += _dot_nt(dmb[:, sl], vvb[:, sl])
            dvv_parts.append(_dot_tn(wms[gi], dmb[:, sl]))
        dvv = jnp.concatenate(dvv_parts, axis=-1)
        dlng_ref[...] += jnp.sum(dvv * xh, axis=0, keepdims=True)
        dlnb_ref[...] += jnp.sum(dvv, axis=0, keepdims=True)
        dxh = dvv * lng_v
        dzv = rstd * (dxh - jnp.mean(dxh, axis=-1, keepdims=True)
                      - xh * jnp.mean(dxh * xh, axis=-1, keepdims=True))
        dz_ref[rows, :] = jnp.concatenate([du, dzv], axis=-1) * gelu_grad

        dab = d_attn.astype(BF16)
        own = _key_in_block()
        dq_parts = []
        dk_parts = []
        dv_parts = []
        for gi in range(N_KV_HEADS):
            cols = slice(gi * HEAD_DIM, (gi + 1) * HEAD_DIM)
            kg, vg = kb[:, cols], vb[:, cols]
            dkg = jnp.zeros((2 * BLK, HEAD_DIM), F32)
            dvg = jnp.zeros((2 * BLK, HEAD_DIM), F32)
            for rr in range(REP):
                h = gi * REP + rr
                hs = slice(h * HEAD_DIM, (h + 1) * HEAD_DIM)
                qh, doh = q_v[:, hs], dab[:, hs]
                pn, band, psink = probs[h]
                dp = _fold(_dot_nt(vg, doh), own)
                delta = jnp.sum(pn * dp, axis=0, keepdims=True)
                ds2 = _unfold(pn * (dp - delta), own)
                dsink = jnp.sum(-psink * delta, axis=-1, keepdims=True)
                dsk_ref[pl.ds(h, 1), :] += jnp.broadcast_to(dsink, (1, 128))
                dq_parts.append(_dot_tn(ds2, kg) * ATTN_SCALE)
                dkg = dkg + _dot(ds2, qh)
                dvg = dvg + _dot(band, doh)
            dk_parts.append(dkg)
            dv_parts.append(dvg)
        dq_ref[rows, :] = jnp.concatenate(dq_parts, axis=-1)
        dkb = jnp.concatenate(dk_parts, axis=-1)
        dvb = jnp.concatenate(dv_parts, axis=-1)
        prev = pl.ds(pl.multiple_of(jnp.maximum(i - 1, 0) * BLK, BLK), BLK)
        cur = pl.ds(pl.multiple_of(i * BLK, BLK), BLK)
        dk_ref[prev, :] += dkb[:BLK]
        dv_ref[prev, :] += dvb[:BLK]
        dk_ref[cur, :] += dkb[BLK:]
        dv_ref[cur, :] += dvb[BLK:]

    return pl.pallas_call(
        body, name=name, grid=(nsteps,),
        in_specs=[pl.BlockSpec(memory_space=pltpu.SMEM),
                  _rows(nb * BLK, D_MODEL), _rows(nb * BLK, ATTN_W), _const((s, KV_W)), _const((s, KV_W)),
                  _rows(nb * BLK, 2 * GMLP_W), _const((1, GMLP_W)), _const((1, GMLP_W)),
                  _const((GMLP_GROUPS, BLK, BLK)), _const((BLK, GMLP_W)), _const((1, ATTN_W)), _const((1, GMLP_W))],
        out_specs=[_rows(nb * BLK, ATTN_W), _const((s, KV_W)), _const((s, KV_W)), _rows(nb * BLK, 2 * GMLP_W),
                   _const((1, ATTN_W)), _const((1, GMLP_W)),
                   _const((1, GMLP_W)), _const((1, GMLP_W)), _const((GMLP_GROUPS, BLK, BLK)),
                   _const((BLK, GMLP_W)), _const((N_Q_HEADS, 128))],
        out_shape=[jax.ShapeDtypeStruct((s, ATTN_W), F32), jax.ShapeDtypeStruct((s, KV_W), F32),
                   jax.ShapeDtypeStruct((s, KV_W), F32), jax.ShapeDtypeStruct((s, 2 * GMLP_W), F32),
                   jax.ShapeDtypeStruct((1, ATTN_W), F32), jax.ShapeDtypeStruct((1, GMLP_W), F32),
                   jax.ShapeDtypeStruct((1, GMLP_W), F32), jax.ShapeDtypeStruct((1, GMLP_W), F32),
                   jax.ShapeDtypeStruct((GMLP_GROUPS, BLK, BLK), F32), jax.ShapeDtypeStruct((BLK, GMLP_W), F32),
                   jax.ShapeDtypeStruct((N_Q_HEADS, 128), F32)],
        compiler_params=_cparams(("arbitrary",)),
    )(sinks, dyy, q, k, v, zg, lng, lnb, w_s, bs_full, gao, ggo)


def _local_step(place, x, tgt, p, own_a, pack_a, pack_b, *, tile=512, fwd_tile=256, bwd_tile=512, norm_tile=512):
    g = {}
    tile, fwd_tile, bwd_tile, norm_tile = (min(t_, x.shape[0]) for t_ in (tile, fwd_tile, bwd_tile, norm_tile))
    hb1, a1_own, b1_own, part1, pack_a = _ffn1_own(x, p["ffn1_norm_g"], own_a, pack_a, tile=tile, name="ffn1_own")
    x1, a1, b1, pack_b = _ffn1_others(place, x, hb1, part1, pack_a, pack_b, tile=fwd_tile, name="ffn1_fwd")
    mix_rows = pack_b[:, 3 * FF_SH:, :]
    w_in_t = mix_rows[:, :IN_SH, :].reshape(IN_W, D_MODEL)
    w_out = mix_rows[:, IN_SH:, :].reshape(D_MODEL, D_MODEL)
    q, k, v, zg, yb, x2 = _mixer_fwd(
        x1, p["mix_norm_g"], w_in_t, p["b_in"], p["attn_sinks"], p["gmlp_ln_g"], p["gmlp_ln_b"], p["gmlp_w_s"],
        p["bs_full"], p["attn_out_norm_g"], p["gmlp_out_norm_g"], w_out, p["b_out"], name="mixer_fwd")
    mix_args = (q, k, v, zg, p["attn_sinks"], p["gmlp_ln_g"], p["gmlp_ln_b"], p["gmlp_w_s"], p["bs_full"],
                p["attn_out_norm_g"], p["gmlp_out_norm_g"])
    dx3, loss, g["final_norm_g"], hb2, a2, b2, do3 = _ffn_fwd_loss(
        x2, p["ffn2_norm_g"], pack_b, 0, p["final_norm_g"], tgt, tile=fwd_tile, name="ffn2_fwd_loss")

    dhp, land = _ffn_bwd(place, hb2, a2, b2, do3, pack_b, 1, None, None, tile=bwd_tile, name="ffn2_bwd")
    dx2, g["ffn2_norm_g"], dyy, dw_out, g["b_out"] = _norm_bwd_mix_out(
        dhp, x2, dx3, p["ffn2_norm_g"], yb, w_out, tile=norm_tile, name="ffn2_norm_bwd")

    (dq, dk, dv, dz, g["attn_out_norm_g"], g["gmlp_out_norm_g"], g["gmlp_ln_g"],
     g["gmlp_ln_b"], g["gmlp_w_s"], dmix_sum, dsinks) = _mix_core_bwd(dyy, *mix_args, name="mix_core_bwd")
    g["gmlp_b_s"] = dmix_sum
    g["attn_sinks"] = dsinks
    dx1, dw_in_t, g["b_in"], g["mix_norm_g"], do1 = _mix_in_bwd(
        x1, dx2, dq, dk, dv, dz, p["mix_norm_g"], w_in_t, tile=tile, name="mix_in_bwd")
    mix_grads = _mix_grads_pack(dw_in_t, dw_out, name="mix_grads_pack")

    dhp1, land = _ffn_bwd(place, hb1, a1, b1, do1, pack_a, 0, land, mix_grads, (a1_own, b1_own), tile=bwd_tile,
                          name="ffn1_bwd")
    dx0, g["ffn1_norm_g"] = _norm_bwd(dhp1, x, dx1, p["ffn1_norm_g"], tile=norm_tile, name="ffn1_norm_bwd")
    return loss, dx0, land, g


def _pack_cast(place, parts, *, name):
    def body(place_ref, *refs):
        oa_ref, ob_ref, own_ref = refs[-3:]
        off = 0
        for k, (ref, rows) in enumerate(zip(refs[:-3], BIG_ROWS)):
            if k == 3:
                off = 0
            cast = ref[...].astype(BF16)
            (oa_ref if k < 3 else ob_ref)[0, off:off + rows, :] = cast
            if k < 3:
                own_ref[off:off + rows, :] = cast
            off += rows

    one = pl.Buffered(1)

    def slab(rows):
        return pl.BlockSpec((1, rows, D_MODEL), lambda i, pr: (pr[0], 0, 0), pipeline_mode=one)

    grid_spec = pltpu.PrefetchScalarGridSpec(
        num_scalar_prefetch=1, grid=(1,),
        in_specs=[pl.BlockSpec((rows, D_MODEL), lambda i, pr: (0, 0), pipeline_mode=one) for rows in BIG_ROWS],
        out_specs=[slab(PACK_A_ROWS), slab(PACK_B_ROWS),
                   pl.BlockSpec((PACK_A_ROWS, D_MODEL), lambda i, pr: (0, 0), pipeline_mode=one)])
    return pl.pallas_call(
        body, name=name, grid_spec=grid_spec,
        out_shape=[jax.ShapeDtypeStruct((N_CHIPS, PACK_A_ROWS, D_MODEL), BF16),
                   jax.ShapeDtypeStruct((N_CHIPS, PACK_B_ROWS, D_MODEL), BF16),
                   jax.ShapeDtypeStruct((PACK_A_ROWS, D_MODEL), BF16)],
        compiler_params=_cparams(("arbitrary",)),
    )(place, *parts)


def _shard_tile(i, c):
    return jnp.where(i < 3, 3 * c + i, jnp.where(i < 6, 3 + 3 * c + i, 12 + c))


def _rs_reduce(place, land, *, name):
    def body(place_ref, l_ref, o_ref):
        acc = l_ref[0].astype(F32)
        for d in range(1, 2 * N_CHIPS):
            acc = acc + l_ref[d].astype(F32)
        o_ref[...] = acc

    grid_spec = pltpu.PrefetchScalarGridSpec(
        num_scalar_prefetch=1, grid=(HALF_ROWS // MIX_HALF,),
        in_specs=[pl.BlockSpec((2 * N_CHIPS, MIX_HALF, D_MODEL), lambda i, pr: (0, i, 0))],
        out_specs=pl.BlockSpec((MIX_HALF, D_MODEL), lambda i, pr: (_shard_tile(i, pr[1]), 0)))
    return pl.pallas_call(
        body, name=name, grid_spec=grid_spec,
        out_shape=jax.ShapeDtypeStruct((PACK_ROWS, D_MODEL), F32),
        compiler_params=_cparams(("arbitrary",)),
    )(place, land)


def _small_all_reduce(packed, shard, *, name):
    rows = packed.shape[0]
    half = rows // 2

    def body(p_ref, sh_in_ref, o_ref, sh_ref, sib_ref, slots_ref, send_sems, recv_sems, share_send, share_recv):
        x, y, c, others = _mesh_place()
        me = 2 * x + y
        sibling = (x, y, 1 - c)
        share_start, share_finish = _share_stages(sh_ref, share_send, share_recv)
        share_start()

        def half_of(core):
            return pl.ds(pl.multiple_of(core * half, 8), half)

        def remote(k, src, dst, to):
            return pltpu.make_async_remote_copy(src_ref=src, dst_ref=dst, send_sem=send_sems.at[k],
                                                recv_sem=recv_sems.at[k], device_id=to, device_id_type=MESH)

        sib = remote(0, p_ref.at[half_of(1 - c)], sib_ref, sibling)
        sib.start()
        sib.wait()
        slots_ref[me] = p_ref[half_of(c), :] + sib_ref[...]
        sends = [remote(1 + j, slots_ref.at[me], slots_ref.at[me], (px, py, c)) for j, (px, py) in enumerate(others)]
        for cp in sends:
            cp.start()
        for j, (px, py) in enumerate(others):
            slab = slots_ref.at[2 * px + py]
            remote(1 + j, slab, slab, (px, py, c)).wait_recv()
        for cp in sends:
            cp.wait_send()
        o_ref[half_of(c), :] = (slots_ref[0] + slots_ref[1]) + (slots_ref[2] + slots_ref[3])
        back = remote(4, o_ref.at[half_of(c)], o_ref.at[half_of(c)], sibling)
        back.start()
        remote(4, o_ref.at[half_of(1 - c)], o_ref.at[half_of(1 - c)], sibling).wait_recv()
        back.wait_send()
        share_finish()

    vm = pl.BlockSpec(memory_space=pltpu.VMEM)
    hbm = pl.BlockSpec(memory_space=pl.ANY)
    return pl.pallas_call(
        body, name=name, in_specs=[vm, hbm], out_specs=[vm, hbm],
        out_shape=[jax.ShapeDtypeStruct((rows, 128), F32), jax.ShapeDtypeStruct(shard.shape, shard.dtype)],
        input_output_aliases={1: 1},
        scratch_shapes=[pltpu.VMEM((half, 128), F32), pltpu.VMEM((N_CHIPS, half, 128), F32),
                        pltpu.SemaphoreType.DMA((5,)), pltpu.SemaphoreType.DMA((5,)),
                        pltpu.SemaphoreType.DMA((3,)), pltpu.SemaphoreType.DMA((3,))],
    )(packed, shard)


def _adamw(w, g, m, v, *, g_row0, tile, name):
    rows, cols = w.shape
    assert g_row0 % tile == 0 and rows % tile == 0

    def body(w_ref, g_ref, m_ref, v_ref, go_ref, d_ref, nm_ref, nv_ref):
        g_v = g_ref[...]
        m_n = ADAM_B1 * m_ref[...] + (1.0 - ADAM_B1) * g_v
        v_n = ADAM_B2 * v_ref[...] + (1.0 - ADAM_B2) * (g_v * g_v)
        m_hat = m_n / (1.0 - ADAM_B1 ** ADAM_STEP)
        v_hat = v_n / (1.0 - ADAM_B2 ** ADAM_STEP)
        d_ref[...] = -ADAM_LR * (m_hat / (jnp.sqrt(v_hat) + ADAM_EPS) + ADAM_WD * w_ref[...])
        go_ref[...] = g_v
        nm_ref[...] = m_n
        nv_ref[...] = v_n

    spec = pl.BlockSpec((tile, cols), lambda i: (i, 0))
    gspec = pl.BlockSpec((tile, cols), lambda i: (g_row0 // tile + i, 0))
    shape = jax.ShapeDtypeStruct((rows, cols), F32)
    return pl.pallas_call(
        body, name=name, grid=(rows // tile,),
        in_specs=[spec, gspec, spec, spec], out_specs=[spec] * 4, out_shape=[shape] * 4,
        compiler_params=_cparams(("arbitrary",)),
    )(w, g, m, v)


def kernel(x, ffn1_norm_g, ffn1_w_gate, ffn1_w_up, ffn1_w_down, mix_norm_g, w_in, b_in, attn_sinks, gmlp_ln_g, gmlp_ln_b, gmlp_w_s, gmlp_b_s, attn_out_norm_g, gmlp_out_norm_g, w_out, b_out, ffn2_norm_g, ffn2_w_gate, ffn2_w_up, ffn2_w_down, final_norm_g, loss_target, m_ffn1_norm_g, m_ffn1_w_gate, m_ffn1_w_up, m_ffn1_w_down, m_mix_norm_g, m_w_in, m_b_in, m_attn_sinks, m_gmlp_ln_g, m_gmlp_ln_b, m_gmlp_w_s, m_gmlp_b_s, m_attn_out_norm_g, m_gmlp_out_norm_g, m_w_out, m_b_out, m_ffn2_norm_g, m_ffn2_w_gate, m_ffn2_w_up, m_ffn2_w_down, m_final_norm_g, v_ffn1_norm_g, v_ffn1_w_gate, v_ffn1_w_up, v_ffn1_w_down, v_mix_norm_g, v_w_in, v_b_in, v_attn_sinks, v_gmlp_ln_g, v_gmlp_ln_b, v_gmlp_w_s, v_gmlp_b_s, v_attn_out_norm_g, v_gmlp_out_norm_g, v_w_out, v_b_out, v_ffn2_norm_g, v_ffn2_w_gate, v_ffn2_w_up, v_ffn2_w_down, v_final_norm_g):
    f_args = dict(locals())
    weights = {n: f_args[n] for n in [nm for nm, _ in SMALL if nm != "loss"] + list(BIG)}
    shapes = {n: weights[n].shape for n in weights}
    shapes["loss"] = ()
    place = jnp.stack([2 * lax.axis_index("x") + lax.axis_index("y"), lax.axis_index("c")]).astype(jnp.int32)

    def with_cols(name, a):
        a2 = a.reshape(a.shape[-2], a.shape[-1])
        return a2.T if BIG_TRANSPOSED[BIG.index(name)] else a2

    def natural(name, a2):
        return (a2.T if BIG_TRANSPOSED[BIG.index(name)] else a2).reshape(shapes[name])

    pack_a, pack_b, own_a = _pack_cast(place, [with_cols(n, weights[n]) for n in BIG], name="pack_cast")
    p = {n: weights[n].reshape(1, -1) for n in ("ffn1_norm_g", "mix_norm_g", "b_in", "gmlp_ln_g", "gmlp_ln_b",
                                                "attn_out_norm_g", "gmlp_out_norm_g", "b_out", "ffn2_norm_g",
                                                "final_norm_g")}
    p["attn_sinks"] = attn_sinks.reshape(N_Q_HEADS)
    p["gmlp_w_s"] = gmlp_w_s.reshape(GMLP_GROUPS, BLK, BLK)
    p["bs_full"] = jnp.broadcast_to(gmlp_b_s.reshape(GMLP_GROUPS, BLK).T[:, :, None],
                                    (BLK, GMLP_GROUPS, GROUP_DIM)).reshape(BLK, GMLP_W)

    loss_part, dx0, land, gs = _local_step(place, x[0], loss_target[0], p, own_a, pack_a, pack_b)

    gs["gmlp_b_s"] = jnp.sum(gs["gmlp_b_s"].reshape(BLK, GMLP_GROUPS, GROUP_DIM), axis=-1).T
    gs["attn_sinks"] = gs["attn_sinks"][:, 0]
    gs["loss"] = loss_part[0, 0]
    small_sum, shard = _small_all_reduce(_pack_small(gs), _rs_reduce(place, land, name="rs_reduce"),
                                         name="small_all_reduce")

    grad_w, delta, new_m, new_v = {}, {}, {}, {}
    off = 0
    for n, rows in zip(BIG, BIG_ROWS):
        res = _adamw(with_cols(n, weights[n]), shard, with_cols(n, f_args["m_" + n]), with_cols(n, f_args["v_" + n]),
                     g_row0=off, tile=FF_SH // 2 if rows == FF_SH else 64, name="adamw_" + n)
        grad_w[n], delta[n], new_m[n], new_v[n] = [natural(n, a) for a in res]
        off += rows
    sm = {k: {n: f_args[k + n] for n, _ in SMALL if n != "loss"} for k in ("", "m_", "v_")}
    for k in sm:
        sm[k]["loss"] = jnp.zeros((), F32)
    res = _adamw(_pack_small(sm[""]), small_sum, _pack_small(sm["m_"]), _pack_small(sm["v_"]),
                 g_row0=0, tile=SMALL_ROWS, name="adamw_small")
    small = _unpack_small(res[0], shapes)
    for dst, packed in ((grad_w, res[0]), (delta, res[1]), (new_m, res[2]), (new_v, res[3])):
        dst.update({n: a for n, a in _unpack_small(packed, shapes).items() if n != "loss"})

    order = ('ffn1_norm_g', 'ffn1_w_gate', 'ffn1_w_up', 'ffn1_w_down', 'mix_norm_g', 'w_in', 'b_in', 'attn_sinks',
             'gmlp_ln_g', 'gmlp_ln_b', 'gmlp_w_s', 'gmlp_b_s', 'attn_out_norm_g', 'gmlp_out_norm_g', 'w_out', 'b_out',
             'ffn2_norm_g', 'ffn2_w_gate', 'ffn2_w_up', 'ffn2_w_down', 'final_norm_g')
    return (small["loss"], dx0.reshape(x.shape), *[grad_w[n] for n in order], *[delta[n] for n in order],
            *[new_m[n] for n in order], *[new_v[n] for n in order])
```

```python
import functools

import jax
import jax.numpy as jnp
from jax import lax
from jax.experimental import pallas as pl
from jax.experimental.pallas import tpu as pltpu

F32 = jnp.float32
BF16 = jnp.bfloat16

D_MODEL = 1024
D_FF = 2816
N_CHIPS = 4
FF_SH = D_FF // N_CHIPS
N_Q_HEADS = 8
N_KV_HEADS = 2
REP = N_Q_HEADS // N_KV_HEADS
HEAD_DIM = 64
ATTN_W = 512
KV_W = 128
GMLP_W = 512
GMLP_GROUPS = 8
GROUP_DIM = 64
BLK = 128
MIX_FWD_BLOCKS = 2
MIX_BWD_BLOCKS = 4
IN_W = 1792
IN_SH = IN_W // N_CHIPS
OUT_SH = D_MODEL // N_CHIPS
EPS = 1e-6
FFN_RES = 0.5
ATTN_SCALE = HEAD_DIM ** -0.5

ADAM_LR = 0.001
ADAM_B1 = 0.9
ADAM_B2 = 0.999
ADAM_EPS = 1e-08
ADAM_WD = 0.01
ADAM_STEP = 10

V7X_VMEM_LIMIT = 56 * 1024 * 1024
MESH = pl.DeviceIdType.MESH


def _cparams(sem):
    return pltpu.CompilerParams(dimension_semantics=sem, vmem_limit_bytes=V7X_VMEM_LIMIT)


def _dot(a, b):
    return jnp.dot(a, b, preferred_element_type=F32)


def _dot_nt(a, b):
    return lax.dot_general(a, b, (((1,), (1,)), ((), ())), preferred_element_type=F32)


def _dot_tn(a, b):
    return lax.dot_general(a, b, (((0,), (0,)), ((), ())), preferred_element_type=F32)


def _rms(x, g):
    r = lax.rsqrt(jnp.mean(x * x, axis=-1, keepdims=True) + EPS)
    return x * r * g, r


def _rms_bwd(dh, x, r, g):
    gy = dh * g
    dx = r * gy - x * (r * r * r) * jnp.mean(gy * x, axis=-1, keepdims=True)
    dg = jnp.sum(dh * x * r, axis=0, keepdims=True)
    return dx, dg


def _const(shape):
    nd = len(shape)
    return pl.BlockSpec(shape, lambda *_: (0,) * nd)


def _rows(t, w):
    return pl.BlockSpec((t, w), lambda i: (i, 0))


PACK_ROWS = 7 * FF_SH
HALF_ROWS = PACK_ROWS // 2
FFN_HALF = 3 * FF_SH // 2
MIX_HALF = FF_SH // 2
PACK_A_ROWS = 3 * FF_SH
PACK_B_ROWS = 3 * FF_SH
PACK_M_ROWS = FF_SH
BIG = ("ffn1_w_gate", "ffn1_w_up", "ffn1_w_down", "ffn2_w_gate", "ffn2_w_up", "ffn2_w_down", "w_in", "w_out")
BIG_ROWS = (FF_SH, FF_SH, FF_SH, FF_SH, FF_SH, FF_SH, IN_SH, OUT_SH)
BIG_TRANSPOSED = (True, True, False, True, True, False, True, False)

SMALL = (("ffn1_norm_g", 1024), ("mix_norm_g", 1024), ("b_in", 1792), ("attn_sinks", 8), ("gmlp_ln_g", 512),
         ("gmlp_ln_b", 512), ("gmlp_w_s", 131072), ("gmlp_b_s", 1024), ("attn_out_norm_g", 512),
         ("gmlp_out_norm_g", 512), ("b_out", 1024), ("ffn2_norm_g", 1024), ("final_norm_g", 1024), ("loss", 1))


def _small_rows(n):
    return -(-n // 1024) * 8


SMALL_USED_ROWS = sum(_small_rows(n) for _, n in SMALL)
SMALL_ROWS = -(-SMALL_USED_ROWS // 16) * 16


def _pack_small(parts):
    out = []
    for name, n in SMALL:
        flat = parts[name].reshape(-1).astype(F32)
        rows = _small_rows(n)
        out.append(jnp.pad(flat, (0, rows * 128 - n)).reshape(rows, 128))
    if SMALL_ROWS > SMALL_USED_ROWS:
        out.append(jnp.zeros((SMALL_ROWS - SMALL_USED_ROWS, 128), F32))
    return jnp.concatenate(out, axis=0)


def _unpack_small(packed, shapes):
    res, off = {}, 0
    for name, n in SMALL:
        rows = _small_rows(n)
        res[name] = packed[off:off + rows].reshape(-1)[:n].reshape(shapes[name])
        off += rows
    return res


def _ffn_tile(x, g, wg_ref, wu_ref, wd_ref, hb_ref, a_ref, b_ref):
    h, _ = _rms(x, g)
    hb = h.astype(BF16)
    hb_ref[...] = hb
    acc = jnp.zeros(x.shape, F32)
    for j in range(N_CHIPS):
        a = _dot_nt(hb, wg_ref[j])
        b = _dot_nt(hb, wu_ref[j])
        a_ref[j] = a
        b_ref[j] = b
        f = (a * jax.nn.sigmoid(a) * b).astype(BF16)
        acc = acc + _dot(f, wd_ref[j])
    return x + FFN_RES * acc


def _ffn_saved_specs(s, tile):
    ab = pl.BlockSpec((N_CHIPS, tile, FF_SH), lambda i: (0, i, 0))
    shape = jax.ShapeDtypeStruct((N_CHIPS, s, FF_SH), F32)
    return [_rows(tile, D_MODEL), ab, ab], [jax.ShapeDtypeStruct((s, D_MODEL), BF16), shape, shape]


def _ffn_weight_specs(k0):
    one = pl.Buffered(1)
    return [pl.BlockSpec((N_CHIPS, FF_SH, D_MODEL), functools.partial(lambda kk, i: (0, kk, 0), k0 + d),
                         pipeline_mode=one) for d in range(3)]


def _mesh_place():
    x, y, c = lax.axis_index("x"), lax.axis_index("y"), lax.axis_index("c")
    others = [(1 - x, y), (x, 1 - y), (1 - x, 1 - y)]
    return x, y, c, others


def _gather_stages(o_ref, send_sems, recv_sems):
    x, y, c, others = _mesh_place()
    me = 2 * x + y
    sibling = (x, y, 1 - c)
    half_rows = o_ref.shape[1] // 2

    def half(slab, core):
        return o_ref.at[slab, pl.ds(pl.multiple_of(core * half_rows, 16), half_rows)]

    def copy(k, rows, to):
        return pltpu.make_async_remote_copy(src_ref=rows, dst_ref=rows, send_sem=send_sems.at[k],
                                            recv_sem=recv_sems.at[k], device_id=to, device_id_type=MESH)

    first = [copy(j, half(me, c), (px, py, c)) for j, (px, py) in enumerate(others)]
    passed = [copy(3 + j, half(2 * px + py, c), sibling) for j, (px, py) in enumerate(others)]

    def start():
        for cp in first:
            cp.start()

    def forward():
        for j, (px, py) in enumerate(others):
            copy(j, half(2 * px + py, c), (px, py, c)).wait_recv()
            passed[j].start()

    def finish():
        for j, (px, py) in enumerate(others):
            copy(3 + j, half(2 * px + py, 1 - c), sibling).wait_recv()
        for cp in first + passed:
            cp.wait_send()

    return start, forward, finish


def _swiglu_slab(hb, wg, wu, wd):
    a = _dot_nt(hb, wg)
    b = _dot_nt(hb, wu)
    return a, b, _dot((a * jax.nn.sigmoid(a) * b).astype(BF16), wd)


def _ffn1_own(x, g, own, gather, *, tile, name):
    s = x.shape[0]
    nt = s // tile

    def body(x_ref, g_ref, wg_ref, wu_ref, wd_ref, gin_ref, hb_ref, a_ref, b_ref, p_ref, gat_ref, send_sems, recv_sems):
        i = pl.program_id(0)
        start, forward, finish = _gather_stages(gat_ref, send_sems, recv_sems)
        pl.when(i == 0)(start)
        h, _ = _rms(x_ref[...], g_ref[...])
        hb = h.astype(BF16)
        hb_ref[...] = hb
        a_ref[0], b_ref[0], p_ref[...] = _swiglu_slab(hb, wg_ref[...], wu_ref[...], wd_ref[...])

        @pl.when(i == nt - 1)
        def _():
            forward()
            finish()

    one = pl.Buffered(1)
    wspecs = [pl.BlockSpec((FF_SH, D_MODEL), functools.partial(lambda kk, i: (kk, 0), k), pipeline_mode=one)
              for k in range(3)]
    hbm = pl.BlockSpec(memory_space=pl.ANY)
    own_ab = pl.BlockSpec((1, tile, FF_SH), lambda i: (N_CHIPS - 1, i, 0))
    return pl.pallas_call(
        body, name=name, grid=(nt,),
        in_specs=[_rows(tile, D_MODEL), _const((1, D_MODEL))] + wspecs + [hbm],
        out_specs=[_rows(tile, D_MODEL), own_ab, own_ab, _rows(tile, D_MODEL), hbm],
        out_shape=[jax.ShapeDtypeStruct((s, D_MODEL), BF16), jax.ShapeDtypeStruct((N_CHIPS, s, FF_SH), F32),
                   jax.ShapeDtypeStruct((N_CHIPS, s, FF_SH), F32), jax.ShapeDtypeStruct((s, D_MODEL), F32),
                   jax.ShapeDtypeStruct(gather.shape, gather.dtype)],
        input_output_aliases={5: 4},
        scratch_shapes=[pltpu.SemaphoreType.DMA((6,)), pltpu.SemaphoreType.DMA((6,))],
        compiler_params=_cparams(("arbitrary",)),
    )(x, g, own, own, own, gather)


def _ffn1_others(place, x, hb, p_own, a_all, b_all, pack, gather, *, tile, name):
    s = x.shape[0]
    nt = s // tile
    forward_at = max(nt - 6, 0)

    def body(place_ref, x_ref, hb_ref, p_ref, *rest):
        w_refs = rest[:9]
        o_ref, a_ref, b_ref, gat_ref, send_sems, recv_sems = rest[12:]
        i = pl.program_id(0)
        start, forward, finish = _gather_stages(gat_ref, send_sems, recv_sems)
        pl.when(i == 0)(start)
        hb = hb_ref[...]
        acc = p_ref[...]
        for t in range(N_CHIPS - 1):
            a_ref[t], b_ref[t], part = _swiglu_slab(hb, w_refs[3 * t][0], w_refs[3 * t + 1][0], w_refs[3 * t + 2][0])
            acc = acc + part
        o_ref[...] = x_ref[...] + FFN_RES * acc
        pl.when(i == forward_at)(forward)
        pl.when(i == nt - 1)(finish)

    one = pl.Buffered(1)

    def wspec(t, kk):
        return pl.BlockSpec((1, FF_SH, D_MODEL), lambda i, pr: (jnp.bitwise_xor(pr[0], t + 1), kk, 0),
                            pipeline_mode=one)

    rows = lambda w: pl.BlockSpec((tile, w), lambda i, pr: (i, 0))
    ab = pl.BlockSpec((N_CHIPS - 1, tile, FF_SH), lambda i, pr: (0, i, 0))
    ab_shape = jax.ShapeDtypeStruct((N_CHIPS, s, FF_SH), F32)
    hbm = pl.BlockSpec(memory_space=pl.ANY)
    grid_spec = pltpu.PrefetchScalarGridSpec(
        num_scalar_prefetch=1, grid=(nt,),
        in_specs=[rows(D_MODEL), rows(D_MODEL), rows(D_MODEL)]
                 + [wspec(t, kk) for t in range(N_CHIPS - 1) for kk in range(3)] + [hbm, hbm, hbm],
        out_specs=[rows(D_MODEL), ab, ab, hbm],
        scratch_shapes=[pltpu.SemaphoreType.DMA((6,)), pltpu.SemaphoreType.DMA((6,))])
    return pl.pallas_call(
        body, name=name, grid_spec=grid_spec,
        out_shape=[jax.ShapeDtypeStruct(x.shape, F32), ab_shape, ab_shape,
                   jax.ShapeDtypeStruct(gather.shape, gather.dtype)],
        input_output_aliases={13: 1, 14: 2, 15: 3},
        compiler_params=_cparams(("arbitrary",)),
    )(place, x, hb, p_own, *([pack] * 9), a_all, b_all, gather)


def _ffn_fwd_loss(x, g, pack, k0, gf, tgt, *, tile, name):
    s = x.shape[0]

    def body(x_ref, g_ref, wg_ref, wu_ref, wd_ref, gf_ref, t_ref, dx_ref, loss_ref, dgf_ref, hb_ref, a_ref, b_ref,
             do_ref):
        @pl.when(pl.program_id(0) == 0)
        def _():
            loss_ref[...] = jnp.zeros_like(loss_ref)
            dgf_ref[...] = jnp.zeros_like(dgf_ref)

        x3 = _ffn_tile(x_ref[...], g_ref[...], wg_ref, wu_ref, wd_ref, hb_ref, a_ref, b_ref)
        gf_v = gf_ref[...]
        out, r = _rms(x3, gf_v)
        diff = out - t_ref[...]
        part = jnp.sum(jnp.sum(diff * diff, axis=-1, keepdims=True), axis=0, keepdims=True)
        loss_ref[...] += jnp.broadcast_to(part * (0.5 / D_MODEL), loss_ref.shape)
        dx, dg = _rms_bwd(diff * (1.0 / D_MODEL), x3, r, gf_v)
        dx_ref[...] = dx
        do_ref[...] = (FFN_RES * dx).astype(BF16)
        dgf_ref[...] += dg

    saved_specs, saved_shapes = _ffn_saved_specs(s, tile)
    return pl.pallas_call(
        body, name=name, grid=(s // tile,),
        in_specs=[_rows(tile, D_MODEL), _const((1, D_MODEL))] + _ffn_weight_specs(k0)
                 + [_const((1, D_MODEL)), _rows(tile, D_MODEL)],
        out_specs=[_rows(tile, D_MODEL), _const((1, 128)), _const((1, D_MODEL))] + saved_specs
                  + [_rows(tile, D_MODEL)],
        out_shape=[jax.ShapeDtypeStruct(x.shape, F32),
                   jax.ShapeDtypeStruct((1, 128), F32),
                   jax.ShapeDtypeStruct((1, D_MODEL), F32)] + saved_shapes
                  + [jax.ShapeDtypeStruct(x.shape, BF16)],
        compiler_params=_cparams(("arbitrary",)),
    )(x, g, pack, pack, pack, gf, tgt)


def _ffn_bwd(place, hb, a, b, do, pack, region, land, mix_grads, ab_by_pass=False, *, tile, name):
    s = hb.shape[0]
    nt = s // tile
    land_rows = pl.ds(region * FFN_HALF, FFN_HALF)
    mix_rows = pl.ds(2 * FFN_HALF, MIX_HALF)
    with_mix = mix_grads is not None
    with_land = land is not None
    n_others = 2 * N_CHIPS - 1

    def body(place_ref, hb_ref, a_ref, b_ref, do_ref, wg_ref, wu_ref, wd_ref, *rest):
        rest = list(rest)
        mix_ref = rest.pop(0) if with_mix else None
        if with_land:
            rest.pop(0)
        dhp_ref, land_ref, acc_ref, stage_ref, send_sems, recv_sem, local_sem = rest[:7]
        t, i = pl.program_id(0), pl.program_id(1)
        xi, yi, c = lax.axis_index("x"), lax.axis_index("y"), lax.axis_index("c")
        dev = 4 * xi + 2 * yi + c
        tt = (t + 1) % N_CHIPS
        tx, ty = jnp.bitwise_xor(xi, tt // 2), jnp.bitwise_xor(yi, tt % 2)

        def remote(src, dst, ssem, rsem, to):
            return pltpu.make_async_remote_copy(src_ref=src, dst_ref=dst, send_sem=ssem, recv_sem=rsem,
                                                device_id=to, device_id_type=MESH)

        def stage_half(h):
            return stage_ref.at[pl.ds(pl.multiple_of(h * FFN_HALF, 16), FFN_HALF)]

        if with_mix:
            mix_send, mix_recv, mix_local = rest[7:10]

            @pl.when(jnp.logical_and(t == 0, i == 0))
            def _():
                for chip in range(N_CHIPS):
                    for h in range(2):
                        src = mix_ref.at[chip, pl.ds(h * MIX_HALF, MIX_HALF)]
                        dst = land_ref.at[dev, mix_rows]
                        mine = jnp.logical_and(2 * xi + yi == chip, c == h)

                        @pl.when(mine)
                        def _():
                            pltpu.make_async_copy(src, dst, mix_local).start()

                        @pl.when(jnp.logical_not(mine))
                        def _():
                            remote(src, dst, mix_send, mix_recv, (chip // 2, chip % 2, h)).start()

        @pl.when(i == 0)
        def _():
            acc_ref[...] = jnp.zeros_like(acc_ref)

        hb = hb_ref[...]
        dob = do_ref[...]
        wg_j, wu_j, wd_j = wg_ref[0], wu_ref[0], wd_ref[0]
        a = a_ref[0]
        b = b_ref[0]
        sg = jax.nn.sigmoid(a)
        sa = a * sg
        fb = (sa * b).astype(BF16)
        df = _dot_nt(dob, wd_j)
        dbb = (df * sa).astype(BF16)
        dab = (df * b * (sg + sa * (1.0 - sg))).astype(BF16)
        dhp_ref[0] = (_dot(dab, wg_j) + _dot(dbb, wu_j)).astype(BF16)
        acc_ref[0:FF_SH, :] += _dot_tn(dab, hb)
        acc_ref[FF_SH:2 * FF_SH, :] += _dot_tn(dbb, hb)
        acc_ref[2 * FF_SH:3 * FF_SH, :] += _dot_tn(fb, dob)

        @pl.when(i == nt - 1)
        def _():
            dst = land_ref.at[dev, land_rows]

            @pl.when(t > 0)
            def _():
                for h in range(2):
                    remote(stage_half(h), dst, send_sems.at[h], recv_sem, (tx, ty, h)).wait_send()

            def cast_rows(r, carry):
                rows = pl.ds(pl.multiple_of(r * MIX_HALF, 16), MIX_HALF)
                stage_ref[rows, :] = acc_ref[rows, :].astype(BF16)
                return carry

            lax.fori_loop(0, 3 * FF_SH // MIX_HALF, cast_rows, 0)

            @pl.when(t < N_CHIPS - 1)
            def _():
                for h in range(2):
                    remote(stage_half(h), dst, send_sems.at[h], recv_sem, (tx, ty, h)).start()

            @pl.when(t == N_CHIPS - 1)
            def _():
                own = pltpu.make_async_copy(stage_half(c), dst, local_sem)
                own.start()
                sib = remote(stage_half(1 - c), dst, send_sems.at[0], recv_sem, (xi, yi, 1 - c))
                sib.start()
                sib.wait_send()
                own.wait()
                arrivals = land_ref.at[pl.ds(0, n_others), land_rows]
                remote(arrivals, arrivals, send_sems.at[0], recv_sem, (xi, yi, 1 - c)).wait_recv()
                if with_mix:
                    seven = land_ref.at[pl.ds(0, n_others), mix_rows]
                    both = remote(seven, seven, mix_send, mix_recv, (xi, yi, 1 - c))
                    both.wait_send()
                    both.wait_recv()
                    pltpu.make_async_copy(mix_ref.at[0, pl.ds(0, MIX_HALF)], land_ref.at[dev, mix_rows],
                                          mix_local).wait()

    def wspec(kk):
        return pl.BlockSpec((1, FF_SH, D_MODEL),
                            lambda t, i, pr: (jnp.bitwise_xor(pr[0], (t + 1) % N_CHIPS), kk, 0))

    xspec = pl.BlockSpec((tile, D_MODEL), lambda t, i, pr: (i, 0))
    if ab_by_pass:
        abspec = pl.BlockSpec((1, tile, FF_SH), lambda t, i, pr: (t, i, 0))
    else:
        abspec = pl.BlockSpec((1, tile, FF_SH), lambda t, i, pr: (jnp.bitwise_xor(pr[0], (t + 1) % N_CHIPS), i, 0))
    hbm = pl.BlockSpec(memory_space=pl.ANY)
    in_specs = [xspec, abspec, abspec, xspec, wspec(0), wspec(1), wspec(2)]
    operands = [place, hb, a, b, do, pack, pack, pack]
    scratch = [pltpu.VMEM((3 * FF_SH, D_MODEL), F32), pltpu.VMEM((3 * FF_SH, D_MODEL), BF16),
               pltpu.SemaphoreType.DMA((2,)), pltpu.SemaphoreType.DMA, pltpu.SemaphoreType.DMA]
    if with_mix:
        in_specs.append(hbm)
        operands.append(mix_grads)
        scratch += [pltpu.SemaphoreType.DMA, pltpu.SemaphoreType.DMA, pltpu.SemaphoreType.DMA]
    aliases = {}
    if with_land:
        in_specs.append(hbm)
        operands.append(land)
        aliases = {len(operands) - 1: 1}
    grid_spec = pltpu.PrefetchScalarGridSpec(
        num_scalar_prefetch=1, grid=(N_CHIPS, nt), in_specs=in_specs,
        out_specs=[pl.BlockSpec((1, tile, D_MODEL), lambda t, i, pr: (t, i, 0)), hbm],
        scratch_shapes=scratch)
    return pl.pallas_call(
        body, name=name, grid_spec=grid_spec,
        out_shape=[jax.ShapeDtypeStruct((N_CHIPS, s, D_MODEL), BF16),
                   jax.ShapeDtypeStruct((2 * N_CHIPS, HALF_ROWS, D_MODEL), BF16)],
        input_output_aliases=aliases,
        compiler_params=_cparams(("arbitrary", "arbitrary")),
    )(*operands)


def _mix_grads_pack(dw_in_t, dw_out, *, name):
    def body(a_ref, b_ref, o_ref):
        o_ref[0, 0:IN_SH, :] = a_ref[0].astype(BF16)
        o_ref[0, IN_SH:FF_SH, :] = b_ref[0].astype(BF16)

    return pl.pallas_call(
        body, name=name, grid=(N_CHIPS,),
        in_specs=[pl.BlockSpec((1, IN_SH, D_MODEL), lambda j: (j, 0, 0)),
                  pl.BlockSpec((1, OUT_SH, D_MODEL), lambda j: (j, 0, 0))],
        out_specs=pl.BlockSpec((1, FF_SH, D_MODEL), lambda j: (j, 0, 0)),
        out_shape=jax.ShapeDtypeStruct((N_CHIPS, FF_SH, D_MODEL), BF16),
        compiler_params=_cparams(("arbitrary",)),
    )(dw_in_t.reshape(N_CHIPS, IN_SH, D_MODEL), dw_out.reshape(N_CHIPS, OUT_SH, D_MODEL))


def _share_stages(o_ref, send_sems, recv_sems):
    x, y, c, _ = _mesh_place()

    def rows(k, core):
        if k < 2:
            return o_ref.at[pl.ds(pl.multiple_of(k * 2 * FFN_HALF + core * FFN_HALF, 8), FFN_HALF)]
        return o_ref.at[pl.ds(pl.multiple_of(4 * FFN_HALF + core * MIX_HALF, 8), MIX_HALF)]

    def copy(k, core):
        return pltpu.make_async_remote_copy(src_ref=rows(k, core), dst_ref=rows(k, core), send_sem=send_sems.at[k],
                                            recv_sem=recv_sems.at[k], device_id=(x, y, 1 - c), device_id_type=MESH)

    sends = [copy(k, c) for k in range(3)]

    def start():
        for cp in sends:
            cp.start()

    def finish():
        for k in range(3):
            copy(k, 1 - c).wait_recv()
        for cp in sends:
            cp.wait_send()

    return start, finish


def _norm_bwd(dhp, x, dy, g, *, tile, name):
    s = x.shape[0]

    def body(dhp_ref, x_ref, dy_ref, g_ref, dx_ref, dg_ref):
        @pl.when(pl.program_id(0) == 0)
        def _():
            dg_ref[...] = jnp.zeros_like(dg_ref)

        dh = ((dhp_ref[0].astype(F32) + dhp_ref[1].astype(F32))
              + (dhp_ref[2].astype(F32) + dhp_ref[3].astype(F32)))
        x_v = x_ref[...]
        r = lax.rsqrt(jnp.mean(x_v * x_v, axis=-1, keepdims=True) + EPS)
        dx, dg = _rms_bwd(dh, x_v, r, g_ref[...])
        dx_ref[...] = dy_ref[...] + dx
        dg_ref[...] += dg

    return pl.pallas_call(
        body, name=name, grid=(s // tile,),
        in_specs=[pl.BlockSpec((N_CHIPS, tile, D_MODEL), lambda i: (0, i, 0)),
                  _rows(tile, D_MODEL), _rows(tile, D_MODEL), _const((1, D_MODEL))],
        out_specs=[_rows(tile, D_MODEL), _const((1, D_MODEL))],
        out_shape=[jax.ShapeDtypeStruct(x.shape, F32), jax.ShapeDtypeStruct((1, D_MODEL), F32)],
        compiler_params=_cparams(("arbitrary",)),
    )(dhp, x, dy, g)


def _mix_in_bwd(x, dy, dq, dk, dv, dz, g, w_in_t, *, tile, name):
    s = x.shape[0]

    def body(x_ref, dy_ref, dq_ref, dk_ref, dv_ref, dz_ref, g_ref, w_ref, dx_ref, dw_ref, db_ref, dg_ref, do_ref):
        @pl.when(pl.program_id(0) == 0)
        def _():
            dw_ref[...] = jnp.zeros_like(dw_ref)
            db_ref[...] = jnp.zeros_like(db_ref)
            dg_ref[...] = jnp.zeros_like(dg_ref)

        dproj = jnp.concatenate([dq_ref[...], dk_ref[...], dv_ref[...], dz_ref[...]], axis=-1)
        db_ref[...] += jnp.sum(dproj, axis=0, keepdims=True)
        dpb = dproj.astype(BF16)
        x_v = x_ref[...]
        g_v = g_ref[...]
        h, r = _rms(x_v, g_v)
        dw_ref[...] += _dot_tn(dpb, h.astype(BF16))
        dh = _dot(dpb, w_ref[...])
        dxn, dg = _rms_bwd(dh, x_v, r, g_v)
        dx = dy_ref[...] + dxn
        dx_ref[...] = dx
        do_ref[...] = (FFN_RES * dx).astype(BF16)
        dg_ref[...] += dg

    return pl.pallas_call(
        body, name=name, grid=(s // tile,),
        in_specs=[_rows(tile, D_MODEL), _rows(tile, D_MODEL), _rows(tile, ATTN_W), _rows(tile, KV_W),
                  _rows(tile, KV_W), _rows(tile, 2 * GMLP_W), _const((1, D_MODEL)), _const((IN_W, D_MODEL))],
        out_specs=[_rows(tile, D_MODEL), _const((IN_W, D_MODEL)), _const((1, IN_W)), _const((1, D_MODEL)),
                   _rows(tile, D_MODEL)],
        out_shape=[jax.ShapeDtypeStruct(x.shape, F32), jax.ShapeDtypeStruct((IN_W, D_MODEL), F32),
                   jax.ShapeDtypeStruct((1, IN_W), F32), jax.ShapeDtypeStruct((1, D_MODEL), F32),
                   jax.ShapeDtypeStruct(x.shape, BF16)],
        compiler_params=_cparams(("arbitrary",)),
    )(x, dy, dq, dk, dv, dz, g, w_in_t)


_GELU_C = 0.7978845608028654
_GELU_A = 0.044715


def _gelu_tanh(x):
    x2 = x * x
    return jnp.tanh(_GELU_C * (x + _GELU_A * (x2 * x))), x2


def _band(ref, i):
    prev = jnp.maximum(i - 1, 0)
    return jnp.concatenate([ref[pl.ds(pl.multiple_of(prev * BLK, BLK), BLK), :],
                            ref[pl.ds(pl.multiple_of(i * BLK, BLK), BLK), :]], axis=0)


def _key_in_block():
    return lax.broadcasted_iota(jnp.int32, (BLK, BLK), 0) <= lax.broadcasted_iota(jnp.int32, (BLK, BLK), 1)


def _fold(band, own):
    return jnp.where(own, band[BLK:], band[:BLK])


def _unfold(a, own):
    zero = jnp.zeros_like(a)
    return jnp.concatenate([jnp.where(own, zero, a), jnp.where(own, a, zero)], axis=0).astype(BF16)


def _attn_fwd(q, kb, vb, i, sink_ref):
    own = _key_in_block()
    outs, saved = [], []
    for h in range(N_Q_HEADS):
        cols = slice((h // REP) * HEAD_DIM, (h // REP + 1) * HEAD_DIM)
        s2 = _dot_nt(kb[:, cols], q[:, h * HEAD_DIM:(h + 1) * HEAD_DIM])
        sc = jnp.where(own, s2[BLK:], jnp.where(i > 0, s2[:BLK], -jnp.inf))
        sink = sink_ref[h]
        m = jnp.maximum(jnp.max(sc, axis=0, keepdims=True), sink)
        p = jnp.exp(sc - m)
        es = jnp.exp(sink - m)
        inv = 1.0 / (jnp.sum(p, axis=0, keepdims=True) + es)
        pn = p * inv
        band = _unfold(pn, own)
        outs.append(_dot_tn(band, vb[:, cols]))
        saved.append((pn, band, es * inv))
    return jnp.concatenate(outs, axis=-1), saved


def _tril_mask():
    t = lax.broadcasted_iota(jnp.int32, (BLK, BLK), 0)
    s_ = lax.broadcasted_iota(jnp.int32, (BLK, BLK), 1)
    return s_ <= t


def _gmlp_fwd_parts(zg, lng, lnb, ws_ref, bs_full):
    th, zg2 = _gelu_tanh(zg)
    z = 0.5 * zg * (1.0 + th)
    u = z[:, :GMLP_W]
    zv = z[:, GMLP_W:]
    mu = jnp.mean(zv, axis=-1, keepdims=True)
    zc = zv - mu
    rstd = lax.rsqrt(jnp.mean(zc * zc, axis=-1, keepdims=True) + EPS)
    xh = zc * rstd
    vvb = (xh * lng + lnb).astype(BF16)
    tril = _tril_mask()
    wms, parts = [], []
    for gi in range(GMLP_GROUPS):
        wm = jnp.where(tril, ws_ref[gi], 0.0).astype(BF16)
        wms.append(wm)
        parts.append(_dot(wm, vvb[:, gi * GROUP_DIM:(gi + 1) * GROUP_DIM]))
    mixed = jnp.concatenate(parts, axis=-1) + bs_full
    gelu_grad = 0.5 * (1.0 + th) + 0.5 * zg * (1.0 - th * th) * (_GELU_C * (1.0 + 3.0 * _GELU_A * zg2))
    return u, xh, rstd, vvb, wms, mixed, gelu_grad


def _mixer_fwd(x1, g, w_in_t, b_in, sinks, lng, lnb, w_s, bs_full, gao, ggo, w_out, b_out, gather, *, name):
    s = x1.shape[0]
    nb = min(MIX_FWD_BLOCKS, s // BLK)
    step_rows = nb * BLK
    last = s // step_rows - 1

    def tile_of(i, lag):
        return jnp.clip(i - lag, 0, last)

    def body(sink_ref, xa_ref, xc_ref, g_ref, wi_ref, bi_ref, lng_ref, lnb_ref, ws_ref, bs_ref, gao_ref, ggo_ref,
             wo_ref, bo_ref, gin_ref, q_ref, k_ref, v_ref, z_ref, y_ref, o_ref, gat_ref, qs_ref, zs_ref, ys_ref,
             send_sems, recv_sems):
        i = pl.program_id(0)
        start, forward, finish = _gather_stages(gat_ref, send_sems, recv_sems)

        @pl.when(i == 0)
        def _():
            for ref in (k_ref, v_ref, qs_ref, zs_ref, ys_ref):
                ref[...] = jnp.zeros_like(ref)
            start()

        slot_a, slot_b, slot_c = i % 2, (i + 1) % 2, i % 2

        o_ref[...] = xc_ref[...] + (_dot(ys_ref[slot_c], wo_ref[...]) + bo_ref[...])

        tile_b = tile_of(i, 1)
        for b in range(nb):
            blk = tile_b * nb + b
            rows = slice(b * BLK, (b + 1) * BLK)
            y_attn, _ = _attn_fwd(qs_ref[slot_b, rows, :], _band(k_ref, blk), _band(v_ref, blk), blk, sink_ref)
            u, _, _, _, _, mixed, _ = _gmlp_fwd_parts(zs_ref[slot_b, rows, :], lng_ref[...], lnb_ref[...], ws_ref,
                                                      bs_ref[...])
            ya, _ = _rms(y_attn, gao_ref[...])
            yg, _ = _rms(u * mixed, ggo_ref[...])
            y_blk = jnp.concatenate([ya, yg], axis=-1).astype(BF16)
            y_ref[rows, :] = y_blk
            ys_ref[slot_b, rows, :] = y_blk

        h, _ = _rms(xa_ref[...], g_ref[...])
        proj = _dot_nt(h.astype(BF16), wi_ref[...]) + bi_ref[...]
        q_t = (proj[:, :ATTN_W] * ATTN_SCALE).astype(BF16)
        z_t = proj[:, ATTN_W + 2 * KV_W:]
        here = pl.ds(pl.multiple_of(tile_of(i, 0) * step_rows, step_rows), step_rows)
        q_ref[...] = q_t
        z_ref[...] = z_t
        qs_ref[slot_a] = q_t
        zs_ref[slot_a] = z_t
        k_ref[here, :] = proj[:, ATTN_W:ATTN_W + KV_W].astype(BF16)
        v_ref[here, :] = proj[:, ATTN_W + KV_W:ATTN_W + 2 * KV_W].astype(BF16)
        pl.when(i == max(last - 3, 0))(forward)
        pl.when(i == last + 2)(finish)

    def lagged(width, lag):
        return pl.BlockSpec((step_rows, width), lambda i: (tile_of(i, lag), 0))

    return pl.pallas_call(
        body, name=name, grid=(last + 3,),
        in_specs=[pl.BlockSpec(memory_space=pltpu.SMEM),
                  lagged(D_MODEL, 0), lagged(D_MODEL, 2), _const((1, D_MODEL)), _const((IN_W, D_MODEL)),
                  _const((1, IN_W)), _const((1, GMLP_W)), _const((1, GMLP_W)), _const((GMLP_GROUPS, BLK, BLK)),
                  _const((BLK, GMLP_W)), _const((1, ATTN_W)), _const((1, GMLP_W)), _const((D_MODEL, D_MODEL)),
                  _const((1, D_MODEL)), pl.BlockSpec(memory_space=pl.ANY)],
        out_specs=[lagged(ATTN_W, 0), _const((s, KV_W)), _const((s, KV_W)), lagged(2 * GMLP_W, 0),
                   lagged(D_MODEL, 1), lagged(D_MODEL, 2), pl.BlockSpec(memory_space=pl.ANY)],
        out_shape=[jax.ShapeDtypeStruct((s, ATTN_W), BF16), jax.ShapeDtypeStruct((s, KV_W), BF16),
                   jax.ShapeDtypeStruct((s, KV_W), BF16), jax.ShapeDtypeStruct((s, 2 * GMLP_W), F32),
                   jax.ShapeDtypeStruct((s, D_MODEL), BF16), jax.ShapeDtypeStruct((s, D_MODEL), F32),
                   jax.ShapeDtypeStruct(gather.shape, gather.dtype)],
        input_output_aliases={14: 6},
        scratch_shapes=[pltpu.VMEM((2, step_rows, ATTN_W), BF16), pltpu.VMEM((2, step_rows, 2 * GMLP_W), F32),
                        pltpu.VMEM((2, step_rows, D_MODEL), BF16),
                        pltpu.SemaphoreType.DMA((6,)), pltpu.SemaphoreType.DMA((6,))],
        compiler_params=_cparams(("arbitrary",)),
    )(sinks, x1, x1, g, w_in_t, b_in, lng, lnb, w_s, bs_full, gao, ggo, w_out, b_out, gather)


def _norm_bwd_mix_out(dhp, x, dy, g, yb, w_out, *, tile, name):
    s = x.shape[0]

    def body(dhp_ref, x_ref, dy_ref, g_ref, y_ref, w_ref, dx_ref, dg_ref, dyy_ref, dw_ref, db_ref):
        @pl.when(pl.program_id(0) == 0)
        def _():
            dg_ref[...] = jnp.zeros_like(dg_ref)
            dw_ref[...] = jnp.zeros_like(dw_ref)
            db_ref[...] = jnp.zeros_like(db_ref)

        dh = ((dhp_ref[0].astype(F32) + dhp_ref[1].astype(F32))
              + (dhp_ref[2].astype(F32) + dhp_ref[3].astype(F32)))
        x_v = x_ref[...]
        r = lax.rsqrt(jnp.mean(x_v * x_v, axis=-1, keepdims=True) + EPS)
        dxn, dg = _rms_bwd(dh, x_v, r, g_ref[...])
        dx = dy_ref[...] + dxn
        dx_ref[...] = dx
        dg_ref[...] += dg
        dxb = dx.astype(BF16)
        db_ref[...] += jnp.sum(dx, axis=0, keepdims=True)
        dw_ref[...] += _dot_tn(y_ref[...], dxb)
        dyy_ref[...] = _dot_nt(dxb, w_ref[...])

    return pl.pallas_call(
        body, name=name, grid=(s // tile,),
        in_specs=[pl.BlockSpec((N_CHIPS, tile, D_MODEL), lambda i: (0, i, 0)),
                  _rows(tile, D_MODEL), _rows(tile, D_MODEL), _const((1, D_MODEL)), _rows(tile, D_MODEL),
                  _const((D_MODEL, D_MODEL))],
        out_specs=[_rows(tile, D_MODEL), _const((1, D_MODEL)), _rows(tile, D_MODEL), _const((D_MODEL, D_MODEL)),
                   _const((1, D_MODEL))],
        out_shape=[jax.ShapeDtypeStruct(x.shape, F32), jax.ShapeDtypeStruct((1, D_MODEL), F32),
                   jax.ShapeDtypeStruct(x.shape, F32), jax.ShapeDtypeStruct((D_MODEL, D_MODEL), F32),
                   jax.ShapeDtypeStruct((1, D_MODEL), F32)],
        compiler_params=_cparams(("arbitrary",)),
    )(dhp, x, dy, g, yb, w_out)


def _mix_core_bwd(dyy, q, k, v, zg, sinks, lng, lnb, w_s, bs_full, gao, ggo, *, name):
    s = dyy.shape[0]
    nb = min(MIX_BWD_BLOCKS, s // BLK)
    nsteps = s // (nb * BLK)

    def body(*refs):
        accumulators = refs[13:15] + refs[16:]

        @pl.when(pl.program_id(0) == 0)
        def _():
            for ref in accumulators:
                ref[...] = jnp.zeros_like(ref)

        for b in range(nb):
            one_block(pl.program_id(0) * nb + b, slice(b * BLK, (b + 1) * BLK), *refs)

        @pl.when(pl.program_id(0) == nsteps - 1)
        def _():
            tril = _tril_mask()
            for gi in range(GMLP_GROUPS):
                refs[20][gi] = jnp.where(tril, refs[20][gi], 0.0)

    def one_block(i, rows, sink_ref, dyy_ref, q_ref, k_ref, v_ref, z_ref, lng_ref, lnb_ref, ws_ref, bs_ref, gao_ref,
                  ggo_ref, dq_ref, dk_ref, dv_ref, dz_ref, dgao_ref, dggo_ref, dlng_ref, dlnb_ref, dws_ref, dms_ref,
                  dsk_ref):
        q_v = q_ref[rows, :]
        kb = _band(k_ref, i)
        vb = _band(v_ref, i)
        lng_v = lng_ref[...]
        gao_v = gao_ref[...]
        ggo_v = ggo_ref[...]

        y_attn, probs = _attn_fwd(q_v, kb, vb, i, sink_ref)
        u, xh, rstd, vvb, wms, mixed, gelu_grad = _gmlp_fwd_parts(z_ref[rows, :], lng_v, lnb_ref[...], ws_ref,
                                                                  bs_ref[...])
        y_gmlp = u * mixed
        ra = lax.rsqrt(jnp.mean(y_attn * y_attn, axis=-1, keepdims=True) + EPS)
        rg = lax.rsqrt(jnp.mean(y_gmlp * y_gmlp, axis=-1, keepdims=True) + EPS)

        dyy = dyy_ref[rows, :]
        d_attn, dgao = _rms_bwd(dyy[:, :ATTN_W], y_attn, ra, gao_v)
        d_gmlp, dggo = _rms_bwd(dyy[:, ATTN_W:], y_gmlp, rg, ggo_v)
        dgao_ref[...] += dgao
        dggo_ref[...] += dggo

        du = d_gmlp * mixed
        dmixed = d_gmlp * u
        dms_ref[...] += dmixed
        dmb = dmixed.astype(BF16)
        dvv_parts = []
        for gi in range(GMLP_GROUPS):
            sl = slice(gi * GROUP_DIM, (gi + 1) * GROUP_DIM)
            dws_ref[gi] += _dot_nt(dmb[:, sl], vvb[:, sl])
            dvv_parts.append(_dot_tn(wms[gi], dmb[:, sl]))
        dvv = jnp.concatenate(dvv_parts, axis=-1)
        dlng_ref[...] += jnp.sum(dvv * xh, axis=0, keepdims=True)
        dlnb_ref[...] += jnp.sum(dvv, axis=0, keepdims=True)
        dxh = dvv * lng_v
        dzv = rstd * (dxh - jnp.mean(dxh, axis=-1, keepdims=True)
                      - xh * jnp.mean(dxh * xh, axis=-1, keepdims=True))
        dz_ref[rows, :] = jnp.concatenate([du, dzv], axis=-1) * gelu_grad

        dab = d_attn.astype(BF16)
        own = _key_in_block()
        dq_parts = []
        dk_parts = []
        dv_parts = []
        for gi in range(N_KV_HEADS):
            cols = slice(gi * HEAD_DIM, (gi + 1) * HEAD_DIM)
            kg, vg = kb[:, cols], vb[:, cols]
            dkg = jnp.zeros((2 * BLK, HEAD_DIM), F32)
            dvg = jnp.zeros((2 * BLK, HEAD_DIM), F32)
            for rr in range(REP):
                h = gi * REP + rr
                hs = slice(h * HEAD_DIM, (h + 1) * HEAD_DIM)
                qh, doh = q_v[:, hs], dab[:, hs]
                pn, band, psink = probs[h]
                dp = _fold(_dot_nt(vg, doh), own)
                delta = jnp.sum(pn * dp, axis=0, keepdims=True)
                ds2 = _unfold(pn * (dp - delta), own)
                dsink = jnp.sum(-psink * delta, axis=-1, keepdims=True)
                dsk_ref[pl.ds(h, 1), :] += jnp.broadcast_to(dsink, (1, 128))
                dq_parts.append(_dot_tn(ds2, kg) * ATTN_SCALE)
                dkg = dkg + _dot(ds2, qh)
                dvg = dvg + _dot(band, doh)
            dk_parts.append(dkg)
            dv_parts.append(dvg)
        dq_ref[rows, :] = jnp.concatenate(dq_parts, axis=-1)
        dkb = jnp.concatenate(dk_parts, axis=-1)
        dvb = jnp.concatenate(dv_parts, axis=-1)
        prev = pl.ds(pl.multiple_of(jnp.maximum(i - 1, 0) * BLK, BLK), BLK)
        cur = pl.ds(pl.multiple_of(i * BLK, BLK), BLK)
        dk_ref[prev, :] += dkb[:BLK]
        dv_ref[prev, :] += dvb[:BLK]
        dk_ref[cur, :] += dkb[BLK:]
        dv_ref[cur, :] += dvb[BLK:]

    return pl.pallas_call(
        body, name=name, grid=(nsteps,),
        in_specs=[pl.BlockSpec(memory_space=pltpu.SMEM),
                  _rows(nb * BLK, D_MODEL), _rows(nb * BLK, ATTN_W), _const((s, KV_W)), _const((s, KV_W)),
                  _rows(nb * BLK, 2 * GMLP_W), _const((1, GMLP_W)), _const((1, GMLP_W)),
                  _const((GMLP_GROUPS, BLK, BLK)), _const((BLK, GMLP_W)), _const((1, ATTN_W)), _const((1, GMLP_W))],
        out_specs=[_rows(nb * BLK, ATTN_W), _const((s, KV_W)), _const((s, KV_W)), _rows(nb * BLK, 2 * GMLP_W),
                   _const((1, ATTN_W)), _const((1, GMLP_W)),
                   _const((1, GMLP_W)), _const((1, GMLP_W)), _const((GMLP_GROUPS, BLK, BLK)),
                   _const((BLK, GMLP_W)), _const((N_Q_HEADS, 128))],
        out_shape=[jax.ShapeDtypeStruct((s, ATTN_W), F32), jax.ShapeDtypeStruct((s, KV_W), F32),
                   jax.ShapeDtypeStruct((s, KV_W), F32), jax.ShapeDtypeStruct((s, 2 * GMLP_W), F32),
                   jax.ShapeDtypeStruct((1, ATTN_W), F32), jax.ShapeDtypeStruct((1, GMLP_W), F32),
                   jax.ShapeDtypeStruct((1, GMLP_W), F32), jax.ShapeDtypeStruct((1, GMLP_W), F32),
                   jax.ShapeDtypeStruct((GMLP_GROUPS, BLK, BLK), F32), jax.ShapeDtypeStruct((BLK, GMLP_W), F32),
                   jax.ShapeDtypeStruct((N_Q_HEADS, 128), F32)],
        compiler_params=_cparams(("arbitrary",)),
    )(sinks, dyy, q, k, v, zg, lng, lnb, w_s, bs_full, gao, ggo)


def _local_step(place, x, tgt, p, own_a, pack_a, pack_b, pack_m, *, tile=512, fwd_tile=256, bwd_tile=512,
                norm_tile=512):
    g = {}
    tile, fwd_tile, bwd_tile, norm_tile = (min(t_, x.shape[0]) for t_ in (tile, fwd_tile, bwd_tile, norm_tile))
    hb1, a1, b1, part1, pack_a = _ffn1_own(x, p["ffn1_norm_g"], own_a, pack_a, tile=tile, name="ffn1_own")
    x1, a1, b1, pack_m = _ffn1_others(place, x, hb1, part1, a1, b1, pack_a, pack_m, tile=fwd_tile, name="ffn1_fwd")
    w_in_t = pack_m[:, :IN_SH, :].reshape(IN_W, D_MODEL)
    w_out = pack_m[:, IN_SH:, :].reshape(D_MODEL, D_MODEL)
    q, k, v, zg, yb, x2, pack_b = _mixer_fwd(
        x1, p["mix_norm_g"], w_in_t, p["b_in"], p["attn_sinks"], p["gmlp_ln_g"], p["gmlp_ln_b"], p["gmlp_w_s"],
        p["bs_full"], p["attn_out_norm_g"], p["gmlp_out_norm_g"], w_out, p["b_out"], pack_b, name="mixer_fwd")
    mix_args = (q, k, v, zg, p["attn_sinks"], p["gmlp_ln_g"], p["gmlp_ln_b"], p["gmlp_w_s"], p["bs_full"],
                p["attn_out_norm_g"], p["gmlp_out_norm_g"])
    dx3, loss, g["final_norm_g"], hb2, a2, b2, do3 = _ffn_fwd_loss(
        x2, p["ffn2_norm_g"], pack_b, 0, p["final_norm_g"], tgt, tile=fwd_tile, name="ffn2_fwd_loss")

    dhp, land = _ffn_bwd(place, hb2, a2, b2, do3, pack_b, 1, None, None, tile=bwd_tile, name="ffn2_bwd")
    dx2, g["ffn2_norm_g"], dyy, dw_out, g["b_out"] = _norm_bwd_mix_out(
        dhp, x2, dx3, p["ffn2_norm_g"], yb, w_out, tile=norm_tile, name="ffn2_norm_bwd")

    (dq, dk, dv, dz, g["attn_out_norm_g"], g["gmlp_out_norm_g"], g["gmlp_ln_g"],
     g["gmlp_ln_b"], g["gmlp_w_s"], dmix_sum, dsinks) = _mix_core_bwd(dyy, *mix_args, name="mix_core_bwd")
    g["gmlp_b_s"] = dmix_sum
    g["attn_sinks"] = dsinks
    dx1, dw_in_t, g["b_in"], g["mix_norm_g"], do1 = _mix_in_bwd(
        x1, dx2, dq, dk, dv, dz, p["mix_norm_g"], w_in_t, tile=tile, name="mix_in_bwd")
    mix_grads = _mix_grads_pack(dw_in_t, dw_out, name="mix_grads_pack")

    dhp1, land = _ffn_bwd(place, hb1, a1, b1, do1, pack_a, 0, land, mix_grads, True, tile=bwd_tile, name="ffn1_bwd")
    dx0, g["ffn1_norm_g"] = _norm_bwd(dhp1, x, dx1, p["ffn1_norm_g"], tile=norm_tile, name="ffn1_norm_bwd")
    return loss, dx0, land, g


def _pack_cast(place, parts, *, name):
    def body(place_ref, *refs):
        oa_ref, ob_ref, om_ref, own_ref = refs[-4:]
        off = 0
        for k, (ref, rows) in enumerate(zip(refs[:-4], BIG_ROWS)):
            if k in (3, 6):
                off = 0
            cast = ref[...].astype(BF16)
            (oa_ref if k < 3 else ob_ref if k < 6 else om_ref)[0, off:off + rows, :] = cast
            if k < 3:
                own_ref[off:off + rows, :] = cast
            off += rows

    one = pl.Buffered(1)

    def slab(rows):
        return pl.BlockSpec((1, rows, D_MODEL), lambda i, pr: (pr[0], 0, 0), pipeline_mode=one)

    grid_spec = pltpu.PrefetchScalarGridSpec(
        num_scalar_prefetch=1, grid=(1,),
        in_specs=[pl.BlockSpec((rows, D_MODEL), lambda i, pr: (0, 0), pipeline_mode=one) for rows in BIG_ROWS],
        out_specs=[slab(PACK_A_ROWS), slab(PACK_B_ROWS), slab(PACK_M_ROWS),
                   pl.BlockSpec((PACK_A_ROWS, D_MODEL), lambda i, pr: (0, 0), pipeline_mode=one)])
    return pl.pallas_call(
        body, name=name, grid_spec=grid_spec,
        out_shape=[jax.ShapeDtypeStruct((N_CHIPS, PACK_A_ROWS, D_MODEL), BF16),
                   jax.ShapeDtypeStruct((N_CHIPS, PACK_B_ROWS, D_MODEL), BF16),
                   jax.ShapeDtypeStruct((N_CHIPS, PACK_M_ROWS, D_MODEL), BF16),
                   jax.ShapeDtypeStruct((PACK_A_ROWS, D_MODEL), BF16)],
        compiler_params=_cparams(("arbitrary",)),
    )(place, *parts)


def _shard_tile(i, c):
    return jnp.where(i < 3, 3 * c + i, jnp.where(i < 6, 3 + 3 * c + i, 12 + c))


def _rs_reduce(place, land, *, name):
    def body(place_ref, l_ref, o_ref):
        acc = l_ref[0].astype(F32)
        for d in range(1, 2 * N_CHIPS):
            acc = acc + l_ref[d].astype(F32)
        o_ref[...] = acc

    grid_spec = pltpu.PrefetchScalarGridSpec(
        num_scalar_prefetch=1, grid=(HALF_ROWS // MIX_HALF,),
        in_specs=[pl.BlockSpec((2 * N_CHIPS, MIX_HALF, D_MODEL), lambda i, pr: (0, i, 0))],
        out_specs=pl.BlockSpec((MIX_HALF, D_MODEL), lambda i, pr: (_shard_tile(i, pr[1]), 0)))
    return pl.pallas_call(
        body, name=name, grid_spec=grid_spec,
        out_shape=jax.ShapeDtypeStruct((PACK_ROWS, D_MODEL), F32),
        compiler_params=_cparams(("arbitrary",)),
    )(place, land)


def _small_all_reduce(packed, shard, *, name):
    rows = packed.shape[0]
    half = rows // 2

    def body(p_ref, sh_in_ref, o_ref, sh_ref, sib_ref, slots_ref, send_sems, recv_sems, share_send, share_recv):
        x, y, c, others = _mesh_place()
        me = 2 * x + y
        sibling = (x, y, 1 - c)
        share_start, share_finish = _share_stages(sh_ref, share_send, share_recv)
        share_start()

        def half_of(core):
            return pl.ds(pl.multiple_of(core * half, 8), half)

        def remote(k, src, dst, to):
            return pltpu.make_async_remote_copy(src_ref=src, dst_ref=dst, send_sem=send_sems.at[k],
                                                recv_sem=recv_sems.at[k], device_id=to, device_id_type=MESH)

        sib = remote(0, p_ref.at[half_of(1 - c)], sib_ref, sibling)
        sib.start()
        sib.wait()
        slots_ref[me] = p_ref[half_of(c), :] + sib_ref[...]
        sends = [remote(1 + j, slots_ref.at[me], slots_ref.at[me], (px, py, c)) for j, (px, py) in enumerate(others)]
        for cp in sends:
            cp.start()
        for j, (px, py) in enumerate(others):
            slab = slots_ref.at[2 * px + py]
            remote(1 + j, slab, slab, (px, py, c)).wait_recv()
        for cp in sends:
            cp.wait_send()
        o_ref[half_of(c), :] = (slots_ref[0] + slots_ref[1]) + (slots_ref[2] + slots_ref[3])
        back = remote(4, o_ref.at[half_of(c)], o_ref.at[half_of(c)], sibling)
        back.start()
        remote(4, o_ref.at[half_of(1 - c)], o_ref.at[half_of(1 - c)], sibling).wait_recv()
        back.wait_send()
        share_finish()

    vm = pl.BlockSpec(memory_space=pltpu.VMEM)
    hbm = pl.BlockSpec(memory_space=pl.ANY)
    return pl.pallas_call(
        body, name=name, in_specs=[vm, hbm], out_specs=[vm, hbm],
        out_shape=[jax.ShapeDtypeStruct((rows, 128), F32), jax.ShapeDtypeStruct(shard.shape, shard.dtype)],
        input_output_aliases={1: 1},
        scratch_shapes=[pltpu.VMEM((half, 128), F32), pltpu.VMEM((N_CHIPS, half, 128), F32),
                        pltpu.SemaphoreType.DMA((5,)), pltpu.SemaphoreType.DMA((5,)),
                        pltpu.SemaphoreType.DMA((3,)), pltpu.SemaphoreType.DMA((3,))],
    )(packed, shard)


def _adamw(w, g, m, v, *, g_row0, tile, name):
    rows, cols = w.shape
    assert g_row0 % tile == 0 and rows % tile == 0

    def body(w_ref, g_ref, m_ref, v_ref, go_ref, d_ref, nm_ref, nv_ref):
        g_v = g_ref[...]
        m_n = ADAM_B1 * m_ref[...] + (1.0 - ADAM_B1) * g_v
        v_n = ADAM_B2 * v_ref[...] + (1.0 - ADAM_B2) * (g_v * g_v)
        m_hat = m_n / (1.0 - ADAM_B1 ** ADAM_STEP)
        v_hat = v_n / (1.0 - ADAM_B2 ** ADAM_STEP)
        d_ref[...] = -ADAM_LR * (m_hat / (jnp.sqrt(v_hat) + ADAM_EPS) + ADAM_WD * w_ref[...])
        go_ref[...] = g_v
        nm_ref[...] = m_n
        nv_ref[...] = v_n

    spec = pl.BlockSpec((tile, cols), lambda i: (i, 0))
    gspec = pl.BlockSpec((tile, cols), lambda i: (g_row0 // tile + i, 0))
    shape = jax.ShapeDtypeStruct((rows, cols), F32)
    return pl.pallas_call(
        body, name=name, grid=(rows // tile,),
        in_specs=[spec, gspec, spec, spec], out_specs=[spec] * 4, out_shape=[shape] * 4,
        compiler_params=_cparams(("arbitrary",)),
    )(w, g, m, v)


def kernel(x, ffn1_norm_g, ffn1_w_gate, ffn1_w_up, ffn1_w_down, mix_norm_g, w_in, b_in, attn_sinks, gmlp_ln_g, gmlp_ln_b, gmlp_w_s, gmlp_b_s, attn_out_norm_g, gmlp_out_norm_g, w_out, b_out, ffn2_norm_g, ffn2_w_gate, ffn2_w_up, ffn2_w_down, final_norm_g, loss_target, m_ffn1_norm_g, m_ffn1_w_gate, m_ffn1_w_up, m_ffn1_w_down, m_mix_norm_g, m_w_in, m_b_in, m_attn_sinks, m_gmlp_ln_g, m_gmlp_ln_b, m_gmlp_w_s, m_gmlp_b_s, m_attn_out_norm_g, m_gmlp_out_norm_g, m_w_out, m_b_out, m_ffn2_norm_g, m_ffn2_w_gate, m_ffn2_w_up, m_ffn2_w_down, m_final_norm_g, v_ffn1_norm_g, v_ffn1_w_gate, v_ffn1_w_up, v_ffn1_w_down, v_mix_norm_g, v_w_in, v_b_in, v_attn_sinks, v_gmlp_ln_g, v_gmlp_ln_b, v_gmlp_w_s, v_gmlp_b_s, v_attn_out_norm_g, v_gmlp_out_norm_g, v_w_out, v_b_out, v_ffn2_norm_g, v_ffn2_w_gate, v_ffn2_w_up, v_ffn2_w_down, v_final_norm_g):
    f_args = dict(locals())
    weights = {n: f_args[n] for n in [nm for nm, _ in SMALL if nm != "loss"] + list(BIG)}
    shapes = {n: weights[n].shape for n in weights}
    shapes["loss"] = ()
    place = jnp.stack([2 * lax.axis_index("x") + lax.axis_index("y"), lax.axis_index("c")]).astype(jnp.int32)

    def with_cols(name, a):
        a2 = a.reshape(a.shape[-2], a.shape[-1])
        return a2.T if BIG_TRANSPOSED[BIG.index(name)] else a2

    def natural(name, a2):
        return (a2.T if BIG_TRANSPOSED[BIG.index(name)] else a2).reshape(shapes[name])

    pack_a, pack_b, pack_m, own_a = _pack_cast(place, [with_cols(n, weights[n]) for n in BIG], name="pack_cast")
    p = {n: weights[n].reshape(1, -1) for n in ("ffn1_norm_g", "mix_norm_g", "b_in", "gmlp_ln_g", "gmlp_ln_b",
                                                "attn_out_norm_g", "gmlp_out_norm_g", "b_out", "ffn2_norm_g",
                                                "final_norm_g")}
    p["attn_sinks"] = attn_sinks.reshape(N_Q_HEADS)
    p["gmlp_w_s"] = gmlp_w_s.reshape(GMLP_GROUPS, BLK, BLK)
    p["bs_full"] = jnp.broadcast_to(gmlp_b_s.reshape(GMLP_GROUPS, BLK).T[:, :, None],
                                    (BLK, GMLP_GROUPS, GROUP_DIM)).reshape(BLK, GMLP_W)

    loss_part, dx0, land, gs = _local_step(place, x[0], loss_target[0], p, own_a, pack_a, pack_b, pack_m)

    gs["gmlp_b_s"] = jnp.sum(gs["gmlp_b_s"].reshape(BLK, GMLP_GROUPS, GROUP_DIM), axis=-1).T
    gs["attn_sinks"] = gs["attn_sinks"][:, 0]
    gs["loss"] = loss_part[0, 0]
    small_sum, shard = _small_all_reduce(_pack_small(gs), _rs_reduce(place, land, name="rs_reduce"),
                                         name="small_all_reduce")

    grad_w, delta, new_m, new_v = {}, {}, {}, {}
    off = 0
    for n, rows in zip(BIG, BIG_ROWS):
        res = _adamw(with_cols(n, weights[n]), shard, with_cols(n, f_args["m_" + n]), with_cols(n, f_args["v_" + n]),
                     g_row0=off, tile=FF_SH // 2 if rows == FF_SH else 64, name="adamw_" + n)
        grad_w[n], delta[n], new_m[n], new_v[n] = [natural(n, a) for a in res]
        off += rows
    sm = {k: {n: f_args[k + n] for n, _ in SMALL if n != "loss"} for k in ("", "m_", "v_")}
    for k in sm:
        sm[k]["loss"] = jnp.zeros((), F32)
    res = _adamw(_pack_small(sm[""]), small_sum, _pack_small(sm["m_"]), _pack_small(sm["v_"]),
                 g_row0=0, tile=SMALL_ROWS, name="adamw_small")
    small = _unpack_small(res[0], shapes)
    for dst, packed in ((grad_w, res[0]), (delta, res[1]), (new_m, res[2]), (new_v, res[3])):
        dst.update({n: a for n, a in _unpack_small(packed, shapes).items() if n != "loss"})

    order = ('ffn1_norm_g', 'ffn1_w_gate', 'ffn1_w_up', 'ffn1_w_down', 'mix_norm_g', 'w_in', 'b_in', 'attn_sinks',
             'gmlp_ln_g', 'gmlp_ln_b', 'gmlp_w_s', 'gmlp_b_s', 'attn_out_norm_g', 'gmlp_out_norm_g', 'w_out', 'b_out',
             'ffn2_norm_g', 'ffn2_w_gate', 'ffn2_w_up', 'ffn2_w_down', 'final_norm_g')
    return (small["loss"], dx0.reshape(x.shape), *[grad_w[n] for n in order], *[delta[n] for n in order],
            *[new_m[n] for n in order], *[new_v[n] for n in order])
```

```python
import functools

import jax
import jax.numpy as jnp
from jax import lax
from jax.experimental import pallas as pl
from jax.experimental.pallas import tpu as pltpu

F32 = jnp.float32
BF16 = jnp.bfloat16

D_MODEL = 1024
D_FF = 2816
N_CHIPS = 4
FF_SH = D_FF // N_CHIPS
N_Q_HEADS = 8
N_KV_HEADS = 2
REP = N_Q_HEADS // N_KV_HEADS
HEAD_DIM = 64
ATTN_W = 512
KV_W = 128
GMLP_W = 512
GMLP_GROUPS = 8
GROUP_DIM = 64
BLK = 128
MIX_FWD_BLOCKS = 2
MIX_BWD_BLOCKS = 4
IN_W = 1792
IN_SH = IN_W // N_CHIPS
OUT_SH = D_MODEL // N_CHIPS
EPS = 1e-6
FFN_RES = 0.5
ATTN_SCALE = HEAD_DIM ** -0.5

ADAM_LR = 0.001
ADAM_B1 = 0.9
ADAM_B2 = 0.999
ADAM_EPS = 1e-08
ADAM_WD = 0.01
ADAM_STEP = 10

V7X_VMEM_LIMIT = 56 * 1024 * 1024
MESH = pl.DeviceIdType.MESH


def _cparams(sem):
    return pltpu.CompilerParams(dimension_semantics=sem, vmem_limit_bytes=V7X_VMEM_LIMIT)


def _dot(a, b):
    return jnp.dot(a, b, preferred_element_type=F32)


def _dot_nt(a, b):
    return lax.dot_general(a, b, (((1,), (1,)), ((), ())), preferred_element_type=F32)


def _dot_tn(a, b):
    return lax.dot_general(a, b, (((0,), (0,)), ((), ())), preferred_element_type=F32)


def _rms(x, g):
    r = lax.rsqrt(jnp.mean(x * x, axis=-1, keepdims=True) + EPS)
    return x * r * g, r


def _rms_bwd(dh, x, r, g):
    gy = dh * g
    dx = r * gy - x * (r * r * r) * jnp.mean(gy * x, axis=-1, keepdims=True)
    dg = jnp.sum(dh * x * r, axis=0, keepdims=True)
    return dx, dg


def _const(shape):
    nd = len(shape)
    return pl.BlockSpec(shape, lambda *_: (0,) * nd)


def _rows(t, w):
    return pl.BlockSpec((t, w), lambda i: (i, 0))


PACK_ROWS = 7 * FF_SH
HALF_ROWS = PACK_ROWS // 2
FFN_HALF = 3 * FF_SH // 2
MIX_HALF = FF_SH // 2
PACK_A_ROWS = 3 * FF_SH
PACK_B_ROWS = 3 * FF_SH
PACK_M_ROWS = FF_SH
BIG = ("ffn1_w_gate", "ffn1_w_up", "ffn1_w_down", "ffn2_w_gate", "ffn2_w_up", "ffn2_w_down", "w_in", "w_out")
BIG_ROWS = (FF_SH, FF_SH, FF_SH, FF_SH, FF_SH, FF_SH, IN_SH, OUT_SH)
BIG_TRANSPOSED = (True, True, False, True, True, False, True, False)

SMALL = (("ffn1_norm_g", 1024), ("mix_norm_g", 1024), ("b_in", 1792), ("attn_sinks", 8), ("gmlp_ln_g", 512),
         ("gmlp_ln_b", 512), ("gmlp_w_s", 131072), ("gmlp_b_s", 1024), ("attn_out_norm_g", 512),
         ("gmlp_out_norm_g", 512), ("b_out", 1024), ("ffn2_norm_g", 1024), ("final_norm_g", 1024), ("loss", 1))


def _small_rows(n):
    return -(-n // 1024) * 8


SMALL_USED_ROWS = sum(_small_rows(n) for _, n in SMALL)
SMALL_ROWS = -(-SMALL_USED_ROWS // 16) * 16


def _pack_small(parts):
    out = []
    for name, n in SMALL:
        flat = parts[name].reshape(-1).astype(F32)
        rows = _small_rows(n)
        out.append(jnp.pad(flat, (0, rows * 128 - n)).reshape(rows, 128))
    if SMALL_ROWS > SMALL_USED_ROWS:
        out.append(jnp.zeros((SMALL_ROWS - SMALL_USED_ROWS, 128), F32))
    return jnp.concatenate(out, axis=0)


def _unpack_small(packed, shapes):
    res, off = {}, 0
    for name, n in SMALL:
        rows = _small_rows(n)
        res[name] = packed[off:off + rows].reshape(-1)[:n].reshape(shapes[name])
        off += rows
    return res


def _ffn_tile(x, g, wg_ref, wu_ref, wd_ref, hb_ref, a_ref, b_ref):
    h, _ = _rms(x, g)
    hb = h.astype(BF16)
    hb_ref[...] = hb
    acc = jnp.zeros(x.shape, F32)
    for j in range(N_CHIPS):
        a = _dot_nt(hb, wg_ref[j])
        b = _dot_nt(hb, wu_ref[j])
        a_ref[j] = a
        b_ref[j] = b
        f = (a * jax.nn.sigmoid(a) * b).astype(BF16)
        acc = acc + _dot(f, wd_ref[j])
    return x + FFN_RES * acc


def _ffn_saved_specs(s, tile):
    ab = pl.BlockSpec((N_CHIPS, tile, FF_SH), lambda i: (0, i, 0))
    shape = jax.ShapeDtypeStruct((N_CHIPS, s, FF_SH), F32)
    return [_rows(tile, D_MODEL), ab, ab], [jax.ShapeDtypeStruct((s, D_MODEL), BF16), shape, shape]


def _ffn_weight_specs(k0):
    one = pl.Buffered(1)
    return [pl.BlockSpec((N_CHIPS, FF_SH, D_MODEL), functools.partial(lambda kk, i: (0, kk, 0), k0 + d),
                         pipeline_mode=one) for d in range(3)]


def _mesh_place():
    x, y, c = lax.axis_index("x"), lax.axis_index("y"), lax.axis_index("c")
    others = [(1 - x, y), (x, 1 - y), (1 - x, 1 - y)]
    return x, y, c, others


def _gather_stages(o_ref, send_sems, recv_sems):
    x, y, c, others = _mesh_place()
    me = 2 * x + y
    sibling = (x, y, 1 - c)
    half_rows = o_ref.shape[1] // 2

    def half(slab, core):
        return o_ref.at[slab, pl.ds(pl.multiple_of(core * half_rows, 16), half_rows)]

    def copy(k, rows, to):
        return pltpu.make_async_remote_copy(src_ref=rows, dst_ref=rows, send_sem=send_sems.at[k],
                                            recv_sem=recv_sems.at[k], device_id=to, device_id_type=MESH)

    first = [copy(j, half(me, c), (px, py, c)) for j, (px, py) in enumerate(others)]
    passed = [copy(3 + j, half(2 * px + py, c), sibling) for j, (px, py) in enumerate(others)]

    def landed(j):
        px, py = others[j]
        copy(j, half(2 * px + py, c), (px, py, c)).wait_recv()
        passed[j].start()

    def sibling_landed(j):
        px, py = others[j]
        copy(3 + j, half(2 * px + py, 1 - c), sibling).wait_recv()

    def start():
        for cp in first:
            cp.start()

    def forward():
        for j in range(len(others)):
            landed(j)

    def finish():
        for j in range(len(others)):
            sibling_landed(j)
        for cp in first + passed:
            cp.wait_send()

    return start, forward, finish, (first, passed, landed, sibling_landed)


def _swiglu_slab(hb, wg, wu, wd):
    a = _dot_nt(hb, wg)
    b = _dot_nt(hb, wu)
    return a, b, _dot((a * jax.nn.sigmoid(a) * b).astype(BF16), wd)


def _ffn1_own(x, g, own, gather, *, tile, name):
    s = x.shape[0]
    nt = s // tile
    y_neighbour = 1

    def body(x_ref, g_ref, wg_ref, wu_ref, wd_ref, gin_ref, hb_ref, a_ref, b_ref, p_ref, gat_ref, w2_ref, send_sems,
             recv_sems, w2_sem):
        ps, i = pl.program_id(0), pl.program_id(1)
        xi, yi, _, _ = _mesh_place()
        _, _, _, (first, passed, landed, sibling_landed) = _gather_stages(gat_ref, send_sems, recv_sems)

        @pl.when(jnp.logical_and(ps == 0, i == 0))
        def _():
            first[0].start()
            first[1].start()

        @pl.when(jnp.logical_and(ps == 1, i == 0))
        def _():
            first[0].wait_send()
            first[1].wait_send()
            first[2].start()
            landed(y_neighbour)
            sibling_landed(y_neighbour)
            load = pltpu.make_async_copy(gat_ref.at[2 * xi + (1 - yi)], w2_ref, w2_sem)
            load.start()
            load.wait()

        h, _ = _rms(x_ref[...], g_ref[...])
        hb = h.astype(BF16)

        @pl.when(ps == 0)
        def _():
            hb_ref[...] = hb
            a_ref[0], b_ref[0], p_ref[0] = _swiglu_slab(hb, wg_ref[...], wu_ref[...], wd_ref[...])

        @pl.when(ps == 1)
        def _():
            a_ref[0], b_ref[0], p_ref[0] = _swiglu_slab(hb, w2_ref[0:FF_SH, :], w2_ref[FF_SH:2 * FF_SH, :],
                                                        w2_ref[2 * FF_SH:3 * FF_SH, :])

        @pl.when(jnp.logical_and(ps == 1, i == nt - 1))
        def _():
            for j in (0, 2):
                landed(j)
            for j in (0, 2):
                sibling_landed(j)
            for cp in [first[2]] + passed:
                cp.wait_send()

    one = pl.Buffered(1)
    wspecs = [pl.BlockSpec((FF_SH, D_MODEL), functools.partial(lambda kk, ps, i: (kk, 0), k), pipeline_mode=one)
              for k in range(3)]
    hbm = pl.BlockSpec(memory_space=pl.ANY)
    tiles = pl.BlockSpec((tile, D_MODEL), lambda ps, i: (i, 0))
    first_pass_tiles = pl.BlockSpec((tile, D_MODEL), lambda ps, i: (jnp.where(ps == 0, i, nt - 1), 0))
    by_pass = lambda w: pl.BlockSpec((1, tile, w), lambda ps, i: (ps, i, 0))
    return pl.pallas_call(
        body, name=name, grid=(2, nt),
        in_specs=[tiles, pl.BlockSpec((1, D_MODEL), lambda ps, i: (0, 0))] + wspecs + [hbm],
        out_specs=[first_pass_tiles, by_pass(FF_SH), by_pass(FF_SH), by_pass(D_MODEL), hbm],
        out_shape=[jax.ShapeDtypeStruct((s, D_MODEL), BF16), jax.ShapeDtypeStruct((N_CHIPS, s, FF_SH), F32),
                   jax.ShapeDtypeStruct((N_CHIPS, s, FF_SH), F32), jax.ShapeDtypeStruct((2, s, D_MODEL), F32),
                   jax.ShapeDtypeStruct(gather.shape, gather.dtype)],
        input_output_aliases={5: 4},
        scratch_shapes=[pltpu.VMEM((PACK_A_ROWS, D_MODEL), BF16), pltpu.SemaphoreType.DMA((6,)),
                        pltpu.SemaphoreType.DMA((6,)), pltpu.SemaphoreType.DMA],
        compiler_params=_cparams(("arbitrary", "arbitrary")),
    )(x, g, own, own, own, gather)


def _ffn1_others(place, x, hb, p_own, a_all, b_all, pack, gather, *, tile, name):
    s = x.shape[0]
    nt = s // tile
    forward_at = max(nt - 6, 0)

    def body(place_ref, x_ref, hb_ref, p_ref, *rest):
        w_refs = rest[:6]
        o_ref, a_ref, b_ref, gat_ref, send_sems, recv_sems = rest[9:]
        i = pl.program_id(0)
        start, forward, finish, _ = _gather_stages(gat_ref, send_sems, recv_sems)
        pl.when(i == 0)(start)
        hb = hb_ref[...]
        acc = p_ref[0] + p_ref[1]
        for t in range(2):
            a_ref[t], b_ref[t], part = _swiglu_slab(hb, w_refs[3 * t][0], w_refs[3 * t + 1][0], w_refs[3 * t + 2][0])
            acc = acc + part
        o_ref[...] = x_ref[...] + FFN_RES * acc
        pl.when(i == forward_at)(forward)
        pl.when(i == nt - 1)(finish)

    one = pl.Buffered(1)

    def wspec(t, kk):
        return pl.BlockSpec((1, FF_SH, D_MODEL), lambda i, pr: (jnp.bitwise_xor(pr[0], t + 2), kk, 0),
                            pipeline_mode=one)

    rows = lambda w: pl.BlockSpec((tile, w), lambda i, pr: (i, 0))
    ab = pl.BlockSpec((2, tile, FF_SH), lambda i, pr: (1, i, 0))
    ab_shape = jax.ShapeDtypeStruct((N_CHIPS, s, FF_SH), F32)
    hbm = pl.BlockSpec(memory_space=pl.ANY)
    grid_spec = pltpu.PrefetchScalarGridSpec(
        num_scalar_prefetch=1, grid=(nt,),
        in_specs=[rows(D_MODEL), rows(D_MODEL), pl.BlockSpec((2, tile, D_MODEL), lambda i, pr: (0, i, 0))]
                 + [wspec(t, kk) for t in range(2) for kk in range(3)] + [hbm, hbm, hbm],
        out_specs=[rows(D_MODEL), ab, ab, hbm],
        scratch_shapes=[pltpu.SemaphoreType.DMA((6,)), pltpu.SemaphoreType.DMA((6,))])
    return pl.pallas_call(
        body, name=name, grid_spec=grid_spec,
        out_shape=[jax.ShapeDtypeStruct(x.shape, F32), ab_shape, ab_shape,
                   jax.ShapeDtypeStruct(gather.shape, gather.dtype)],
        input_output_aliases={10: 1, 11: 2, 12: 3},
        compiler_params=_cparams(("arbitrary",)),
    )(place, x, hb, p_own, *([pack] * 6), a_all, b_all, gather)


def _ffn_fwd_loss(x, g, pack, k0, gf, tgt, *, tile, name):
    s = x.shape[0]

    def body(x_ref, g_ref, wg_ref, wu_ref, wd_ref, gf_ref, t_ref, dx_ref, loss_ref, dgf_ref, hb_ref, a_ref, b_ref,
             do_ref):
        @pl.when(pl.program_id(0) == 0)
        def _():
            loss_ref[...] = jnp.zeros_like(loss_ref)
            dgf_ref[...] = jnp.zeros_like(dgf_ref)

        x3 = _ffn_tile(x_ref[...], g_ref[...], wg_ref, wu_ref, wd_ref, hb_ref, a_ref, b_ref)
        gf_v = gf_ref[...]
        out, r = _rms(x3, gf_v)
        diff = out - t_ref[...]
        part = jnp.sum(jnp.sum(diff * diff, axis=-1, keepdims=True), axis=0, keepdims=True)
        loss_ref[...] += jnp.broadcast_to(part * (0.5 / D_MODEL), loss_ref.shape)
        dx, dg = _rms_bwd(diff * (1.0 / D_MODEL), x3, r, gf_v)
        dx_ref[...] = dx
        do_ref[...] = (FFN_RES * dx).astype(BF16)
        dgf_ref[...] += dg

    saved_specs, saved_shapes = _ffn_saved_specs(s, tile)
    return pl.pallas_call(
        body, name=name, grid=(s // tile,),
        in_specs=[_rows(tile, D_MODEL), _const((1, D_MODEL))] + _ffn_weight_specs(k0)
                 + [_const((1, D_MODEL)), _rows(tile, D_MODEL)],
        out_specs=[_rows(tile, D_MODEL), _const((1, 128)), _const((1, D_MODEL))] + saved_specs
                  + [_rows(tile, D_MODEL)],
        out_shape=[jax.ShapeDtypeStruct(x.shape, F32),
                   jax.ShapeDtypeStruct((1, 128), F32),
                   jax.ShapeDtypeStruct((1, D_MODEL), F32)] + saved_shapes
                  + [jax.ShapeDtypeStruct(x.shape, BF16)],
        compiler_params=_cparams(("arbitrary",)),
    )(x, g, pack, pack, pack, gf, tgt)


def _ffn_bwd(place, hb, a, b, do, pack, region, land, mix_grads, ab_by_pass=False, *, tile, name):
    s = hb.shape[0]
    nt = s // tile
    land_rows = pl.ds(region * FFN_HALF, FFN_HALF)
    mix_rows = pl.ds(2 * FFN_HALF, MIX_HALF)
    with_mix = mix_grads is not None
    with_land = land is not None
    n_others = 2 * N_CHIPS - 1

    def body(place_ref, hb_ref, a_ref, b_ref, do_ref, wg_ref, wu_ref, wd_ref, *rest):
        rest = list(rest)
        mix_ref = rest.pop(0) if with_mix else None
        if with_land:
            rest.pop(0)
        dhp_ref, land_ref, acc_ref, stage_ref, send_sems, recv_sem, local_sem = rest[:7]
        t, i = pl.program_id(0), pl.program_id(1)
        xi, yi, c = lax.axis_index("x"), lax.axis_index("y"), lax.axis_index("c")
        dev = 4 * xi + 2 * yi + c
        tt = (t + 1) % N_CHIPS
        tx, ty = jnp.bitwise_xor(xi, tt // 2), jnp.bitwise_xor(yi, tt % 2)

        def remote(src, dst, ssem, rsem, to):
            return pltpu.make_async_remote_copy(src_ref=src, dst_ref=dst, send_sem=ssem, recv_sem=rsem,
                                                device_id=to, device_id_type=MESH)

        def stage_half(h):
            return stage_ref.at[pl.ds(pl.multiple_of(h * FFN_HALF, 16), FFN_HALF)]

        if with_mix:
            mix_send, mix_recv, mix_local = rest[7:10]

            @pl.when(jnp.logical_and(t == 0, i == 0))
            def _():
                for chip in range(N_CHIPS):
                    for h in range(2):
                        src = mix_ref.at[chip, pl.ds(h * MIX_HALF, MIX_HALF)]
                        dst = land_ref.at[dev, mix_rows]
                        mine = jnp.logical_and(2 * xi + yi == chip, c == h)

                        @pl.when(mine)
                        def _():
                            pltpu.make_async_copy(src, dst, mix_local).start()

                        @pl.when(jnp.logical_not(mine))
                        def _():
                            remote(src, dst, mix_send, mix_recv, (chip // 2, chip % 2, h)).start()

        @pl.when(i == 0)
        def _():
            acc_ref[...] = jnp.zeros_like(acc_ref)

        hb = hb_ref[...]
        dob = do_ref[...]
        wg_j, wu_j, wd_j = wg_ref[0], wu_ref[0], wd_ref[0]
        a = a_ref[0]
        b = b_ref[0]
        sg = jax.nn.sigmoid(a)
        sa = a * sg
        fb = (sa * b).astype(BF16)
        df = _dot_nt(dob, wd_j)
        dbb = (df * sa).astype(BF16)
        dab = (df * b * (sg + sa * (1.0 - sg))).astype(BF16)
        dhp_ref[0] = (_dot(dab, wg_j) + _dot(dbb, wu_j)).astype(BF16)
        acc_ref[0:FF_SH, :] += _dot_tn(dab, hb)
        acc_ref[FF_SH:2 * FF_SH, :] += _dot_tn(dbb, hb)
        acc_ref[2 * FF_SH:3 * FF_SH, :] += _dot_tn(fb, dob)

        @pl.when(i == nt - 1)
        def _():
            dst = land_ref.at[dev, land_rows]

            @pl.when(t > 0)
            def _():
                for h in range(2):
                    remote(stage_half(h), dst, send_sems.at[h], recv_sem, (tx, ty, h)).wait_send()

            def cast_rows(r, carry):
                rows = pl.ds(pl.multiple_of(r * MIX_HALF, 16), MIX_HALF)
                stage_ref[rows, :] = acc_ref[rows, :].astype(BF16)
                return carry

            lax.fori_loop(0, 3 * FF_SH // MIX_HALF, cast_rows, 0)

            @pl.when(t < N_CHIPS - 1)
            def _():
                for h in range(2):
                    remote(stage_half(h), dst, send_sems.at[h], recv_sem, (tx, ty, h)).start()

            @pl.when(t == N_CHIPS - 1)
            def _():
                own = pltpu.make_async_copy(stage_half(c), dst, local_sem)
                own.start()
                sib = remote(stage_half(1 - c), dst, send_sems.at[0], recv_sem, (xi, yi, 1 - c))
                sib.start()
                sib.wait_send()
                own.wait()
                arrivals = land_ref.at[pl.ds(0, n_others), land_rows]
                remote(arrivals, arrivals, send_sems.at[0], recv_sem, (xi, yi, 1 - c)).wait_recv()
                if with_mix:
                    seven = land_ref.at[pl.ds(0, n_others), mix_rows]
                    both = remote(seven, seven, mix_send, mix_recv, (xi, yi, 1 - c))
                    both.wait_send()
                    both.wait_recv()
                    pltpu.make_async_copy(mix_ref.at[0, pl.ds(0, MIX_HALF)], land_ref.at[dev, mix_rows],
                                          mix_local).wait()

    def wspec(kk):
        return pl.BlockSpec((1, FF_SH, D_MODEL),
                            lambda t, i, pr: (jnp.bitwise_xor(pr[0], (t + 1) % N_CHIPS), kk, 0))

    xspec = pl.BlockSpec((tile, D_MODEL), lambda t, i, pr: (i, 0))
    if ab_by_pass:
        abspec = pl.BlockSpec((1, tile, FF_SH), lambda t, i, pr: ((t + 1) % N_CHIPS, i, 0))
    else:
        abspec = pl.BlockSpec((1, tile, FF_SH), lambda t, i, pr: (jnp.bitwise_xor(pr[0], (t + 1) % N_CHIPS), i, 0))
    hbm = pl.BlockSpec(memory_space=pl.ANY)
    in_specs = [xspec, abspec, abspec, xspec, wspec(0), wspec(1), wspec(2)]
    operands = [place, hb, a, b, do, pack, pack, pack]
    scratch = [pltpu.VMEM((3 * FF_SH, D_MODEL), F32), pltpu.VMEM((3 * FF_SH, D_MODEL), BF16),
               pltpu.SemaphoreType.DMA((2,)), pltpu.SemaphoreType.DMA, pltpu.SemaphoreType.DMA]
    if with_mix:
        in_specs.append(hbm)
        operands.append(mix_grads)
        scratch += [pltpu.SemaphoreType.DMA, pltpu.SemaphoreType.DMA, pltpu.SemaphoreType.DMA]
    aliases = {}
    if with_land:
        in_specs.append(hbm)
        operands.append(land)
        aliases = {len(operands) - 1: 1}
    grid_spec = pltpu.PrefetchScalarGridSpec(
        num_scalar_prefetch=1, grid=(N_CHIPS, nt), in_specs=in_specs,
        out_specs=[pl.BlockSpec((1, tile, D_MODEL), lambda t, i, pr: (t, i, 0)), hbm],
        scratch_shapes=scratch)
    return pl.pallas_call(
        body, name=name, grid_spec=grid_spec,
        out_shape=[jax.ShapeDtypeStruct((N_CHIPS, s, D_MODEL), BF16),
                   jax.ShapeDtypeStruct((2 * N_CHIPS, HALF_ROWS, D_MODEL), BF16)],
        input_output_aliases=aliases,
        compiler_params=_cparams(("arbitrary", "arbitrary")),
    )(*operands)


def _mix_grads_pack(dw_in_t, dw_out, *, name):
    def body(a_ref, b_ref, o_ref):
        o_ref[0, 0:IN_SH, :] = a_ref[0].astype(BF16)
        o_ref[0, IN_SH:FF_SH, :] = b_ref[0].astype(BF16)

    return pl.pallas_call(
        body, name=name, grid=(N_CHIPS,),
        in_specs=[pl.BlockSpec((1, IN_SH, D_MODEL), lambda j: (j, 0, 0)),
                  pl.BlockSpec((1, OUT_SH, D_MODEL), lambda j: (j, 0, 0))],
        out_specs=pl.BlockSpec((1, FF_SH, D_MODEL), lambda j: (j, 0, 0)),
        out_shape=jax.ShapeDtypeStruct((N_CHIPS, FF_SH, D_MODEL), BF16),
        compiler_params=_cparams(("arbitrary",)),
    )(dw_in_t.reshape(N_CHIPS, IN_SH, D_MODEL), dw_out.reshape(N_CHIPS, OUT_SH, D_MODEL))


def _share_stages(o_ref, send_sems, recv_sems):
    x, y, c, _ = _mesh_place()

    def rows(k, core):
        if k < 2:
            return o_ref.at[pl.ds(pl.multiple_of(k * 2 * FFN_HALF + core * FFN_HALF, 8), FFN_HALF)]
        return o_ref.at[pl.ds(pl.multiple_of(4 * FFN_HALF + core * MIX_HALF, 8), MIX_HALF)]

    def copy(k, core):
        return pltpu.make_async_remote_copy(src_ref=rows(k, core), dst_ref=rows(k, core), send_sem=send_sems.at[k],
                                            recv_sem=recv_sems.at[k], device_id=(x, y, 1 - c), device_id_type=MESH)

    sends = [copy(k, c) for k in range(3)]

    def start():
        for cp in sends:
            cp.start()

    def finish():
        for k in range(3):
            copy(k, 1 - c).wait_recv()
        for cp in sends:
            cp.wait_send()

    return start, finish


def _norm_bwd(dhp, x, dy, g, *, tile, name):
    s = x.shape[0]

    def body(dhp_ref, x_ref, dy_ref, g_ref, dx_ref, dg_ref):
        @pl.when(pl.program_id(0) == 0)
        def _():
            dg_ref[...] = jnp.zeros_like(dg_ref)

        dh = ((dhp_ref[0].astype(F32) + dhp_ref[1].astype(F32))
              + (dhp_ref[2].astype(F32) + dhp_ref[3].astype(F32)))
        x_v = x_ref[...]
        r = lax.rsqrt(jnp.mean(x_v * x_v, axis=-1, keepdims=True) + EPS)
        dx, dg = _rms_bwd(dh, x_v, r, g_ref[...])
        dx_ref[...] = dy_ref[...] + dx
        dg_ref[...] += dg

    return pl.pallas_call(
        body, name=name, grid=(s // tile,),
        in_specs=[pl.BlockSpec((N_CHIPS, tile, D_MODEL), lambda i: (0, i, 0)),
                  _rows(tile, D_MODEL), _rows(tile, D_MODEL), _const((1, D_MODEL))],
        out_specs=[_rows(tile, D_MODEL), _const((1, D_MODEL))],
        out_shape=[jax.ShapeDtypeStruct(x.shape, F32), jax.ShapeDtypeStruct((1, D_MODEL), F32)],
        compiler_params=_cparams(("arbitrary",)),
    )(dhp, x, dy, g)


def _mix_in_bwd(x, dy, dq, dk, dv, dz, g, w_in_t, *, tile, name):
    s = x.shape[0]

    def body(x_ref, dy_ref, dq_ref, dk_ref, dv_ref, dz_ref, g_ref, w_ref, dx_ref, dw_ref, db_ref, dg_ref, do_ref):
        @pl.when(pl.program_id(0) == 0)
        def _():
            dw_ref[...] = jnp.zeros_like(dw_ref)
            db_ref[...] = jnp.zeros_like(db_ref)
            dg_ref[...] = jnp.zeros_like(dg_ref)

        dproj = jnp.concatenate([dq_ref[...], dk_ref[...], dv_ref[...], dz_ref[...]], axis=-1)
        db_ref[...] += jnp.sum(dproj, axis=0, keepdims=True)
        dpb = dproj.astype(BF16)
        x_v = x_ref[...]
        g_v = g_ref[...]
        h, r = _rms(x_v, g_v)
        dw_ref[...] += _dot_tn(dpb, h.astype(BF16))
        dh = _dot(dpb, w_ref[...])
        dxn, dg = _rms_bwd(dh, x_v, r, g_v)
        dx = dy_ref[...] + dxn
        dx_ref[...] = dx
        do_ref[...] = (FFN_RES * dx).astype(BF16)
        dg_ref[...] += dg

    return pl.pallas_call(
        body, name=name, grid=(s // tile,),
        in_specs=[_rows(tile, D_MODEL), _rows(tile, D_MODEL), _rows(tile, ATTN_W), _rows(tile, KV_W),
                  _rows(tile, KV_W), _rows(tile, 2 * GMLP_W), _const((1, D_MODEL)), _const((IN_W, D_MODEL))],
        out_specs=[_rows(tile, D_MODEL), _const((IN_W, D_MODEL)), _const((1, IN_W)), _const((1, D_MODEL)),
                   _rows(tile, D_MODEL)],
        out_shape=[jax.ShapeDtypeStruct(x.shape, F32), jax.ShapeDtypeStruct((IN_W, D_MODEL), F32),
                   jax.ShapeDtypeStruct((1, IN_W), F32), jax.ShapeDtypeStruct((1, D_MODEL), F32),
                   jax.ShapeDtypeStruct(x.shape, BF16)],
        compiler_params=_cparams(("arbitrary",)),
    )(x, dy, dq, dk, dv, dz, g, w_in_t)


_GELU_C = 0.7978845608028654
_GELU_A = 0.044715


def _gelu_tanh(x):
    x2 = x * x
    return jnp.tanh(_GELU_C * (x + _GELU_A * (x2 * x))), x2


def _band(ref, i):
    prev = jnp.maximum(i - 1, 0)
    return jnp.concatenate([ref[pl.ds(pl.multiple_of(prev * BLK, BLK), BLK), :],
                            ref[pl.ds(pl.multiple_of(i * BLK, BLK), BLK), :]], axis=0)


def _key_in_block():
    return lax.broadcasted_iota(jnp.int32, (BLK, BLK), 0) <= lax.broadcasted_iota(jnp.int32, (BLK, BLK), 1)


def _fold(band, own):
    return jnp.where(own, band[BLK:], band[:BLK])


def _unfold(a, own):
    zero = jnp.zeros_like(a)
    return jnp.concatenate([jnp.where(own, zero, a), jnp.where(own, a, zero)], axis=0).astype(BF16)


def _attn_fwd(q, kb, vb, i, sink_ref):
    own = _key_in_block()
    outs, saved = [], []
    for h in range(N_Q_HEADS):
        cols = slice((h // REP) * HEAD_DIM, (h // REP + 1) * HEAD_DIM)
        s2 = _dot_nt(kb[:, cols], q[:, h * HEAD_DIM:(h + 1) * HEAD_DIM])
        sc = jnp.where(own, s2[BLK:], jnp.where(i > 0, s2[:BLK], -jnp.inf))
        sink = sink_ref[h]
        m = jnp.maximum(jnp.max(sc, axis=0, keepdims=True), sink)
        p = jnp.exp(sc - m)
        es = jnp.exp(sink - m)
        inv = 1.0 / (jnp.sum(p, axis=0, keepdims=True) + es)
        pn = p * inv
        band = _unfold(pn, own)
        outs.append(_dot_tn(band, vb[:, cols]))
        saved.append((pn, band, es * inv))
    return jnp.concatenate(outs, axis=-1), saved


def _tril_mask():
    t = lax.broadcasted_iota(jnp.int32, (BLK, BLK), 0)
    s_ = lax.broadcasted_iota(jnp.int32, (BLK, BLK), 1)
    return s_ <= t


def _gmlp_fwd_parts(zg, lng, lnb, ws_ref, bs_full):
    th, zg2 = _gelu_tanh(zg)
    z = 0.5 * zg * (1.0 + th)
    u = z[:, :GMLP_W]
    zv = z[:, GMLP_W:]
    mu = jnp.mean(zv, axis=-1, keepdims=True)
    zc = zv - mu
    rstd = lax.rsqrt(jnp.mean(zc * zc, axis=-1, keepdims=True) + EPS)
    xh = zc * rstd
    vvb = (xh * lng + lnb).astype(BF16)
    tril = _tril_mask()
    wms, parts = [], []
    for gi in range(GMLP_GROUPS):
        wm = jnp.where(tril, ws_ref[gi], 0.0).astype(BF16)
        wms.append(wm)
        parts.append(_dot(wm, vvb[:, gi * GROUP_DIM:(gi + 1) * GROUP_DIM]))
    mixed = jnp.concatenate(parts, axis=-1) + bs_full
    gelu_grad = 0.5 * (1.0 + th) + 0.5 * zg * (1.0 - th * th) * (_GELU_C * (1.0 + 3.0 * _GELU_A * zg2))
    return u, xh, rstd, vvb, wms, mixed, gelu_grad


def _mixer_fwd(x1, g, w_in_t, b_in, sinks, lng, lnb, w_s, bs_full, gao, ggo, w_out, b_out, gather, *, name):
    s = x1.shape[0]
    nb = min(MIX_FWD_BLOCKS, s // BLK)
    step_rows = nb * BLK
    last = s // step_rows - 1

    def tile_of(i, lag):
        return jnp.clip(i - lag, 0, last)

    def body(sink_ref, xa_ref, xc_ref, g_ref, wi_ref, bi_ref, lng_ref, lnb_ref, ws_ref, bs_ref, gao_ref, ggo_ref,
             wo_ref, bo_ref, gin_ref, q_ref, k_ref, v_ref, z_ref, y_ref, o_ref, gat_ref, qs_ref, zs_ref, ys_ref,
             send_sems, recv_sems):
        i = pl.program_id(0)
        start, forward, finish, _ = _gather_stages(gat_ref, send_sems, recv_sems)

        @pl.when(i == 0)
        def _():
            for ref in (k_ref, v_ref, qs_ref, zs_ref, ys_ref):
                ref[...] = jnp.zeros_like(ref)
            start()

        slot_a, slot_b, slot_c = i % 2, (i + 1) % 2, i % 2

        o_ref[...] = xc_ref[...] + (_dot(ys_ref[slot_c], wo_ref[...]) + bo_ref[...])

        tile_b = tile_of(i, 1)
        for b in range(nb):
            blk = tile_b * nb + b
            rows = slice(b * BLK, (b + 1) * BLK)
            y_attn, _ = _attn_fwd(qs_ref[slot_b, rows, :], _band(k_ref, blk), _band(v_ref, blk), blk, sink_ref)
            u, _, _, _, _, mixed, _ = _gmlp_fwd_parts(zs_ref[slot_b, rows, :], lng_ref[...], lnb_ref[...], ws_ref,
                                                      bs_ref[...])
            ya, _ = _rms(y_attn, gao_ref[...])
            yg, _ = _rms(u * mixed, ggo_ref[...])
            y_blk = jnp.concatenate([ya, yg], axis=-1).astype(BF16)
            y_ref[rows, :] = y_blk
            ys_ref[slot_b, rows, :] = y_blk

        h, _ = _rms(xa_ref[...], g_ref[...])
        proj = _dot_nt(h.astype(BF16), wi_ref[...]) + bi_ref[...]
        q_t = (proj[:, :ATTN_W] * ATTN_SCALE).astype(BF16)
        z_t = proj[:, ATTN_W + 2 * KV_W:]
        here = pl.ds(pl.multiple_of(tile_of(i, 0) * step_rows, step_rows), step_rows)
        q_ref[...] = q_t
        z_ref[...] = z_t
        qs_ref[slot_a] = q_t
        zs_ref[slot_a] = z_t
        k_ref[here, :] = proj[:, ATTN_W:ATTN_W + KV_W].astype(BF16)
        v_ref[here, :] = proj[:, ATTN_W + KV_W:ATTN_W + 2 * KV_W].astype(BF16)
        pl.when(i == max(last - 3, 0))(forward)
        pl.when(i == last + 2)(finish)

    def lagged(width, lag):
        return pl.BlockSpec((step_rows, width), lambda i: (tile_of(i, lag), 0))

    return pl.pallas_call(
        body, name=name, grid=(last + 3,),
        in_specs=[pl.BlockSpec(memory_space=pltpu.SMEM),
                  lagged(D_MODEL, 0), lagged(D_MODEL, 2), _const((1, D_MODEL)), _const((IN_W, D_MODEL)),
                  _const((1, IN_W)), _const((1, GMLP_W)), _const((1, GMLP_W)), _const((GMLP_GROUPS, BLK, BLK)),
                  _const((BLK, GMLP_W)), _const((1, ATTN_W)), _const((1, GMLP_W)), _const((D_MODEL, D_MODEL)),
                  _const((1, D_MODEL)), pl.BlockSpec(memory_space=pl.ANY)],
        out_specs=[lagged(ATTN_W, 0), _const((s, KV_W)), _const((s, KV_W)), lagged(2 * GMLP_W, 0),
                   lagged(D_MODEL, 1), lagged(D_MODEL, 2), pl.BlockSpec(memory_space=pl.ANY)],
        out_shape=[jax.ShapeDtypeStruct((s, ATTN_W), BF16), jax.ShapeDtypeStruct((s, KV_W), BF16),
                   jax.ShapeDtypeStruct((s, KV_W), BF16), jax.ShapeDtypeStruct((s, 2 * GMLP_W), F32),
                   jax.ShapeDtypeStruct((s, D_MODEL), BF16), jax.ShapeDtypeStruct((s, D_MODEL), F32),
                   jax.ShapeDtypeStruct(gather.shape, gather.dtype)],
        input_output_aliases={14: 6},
        scratch_shapes=[pltpu.VMEM((2, step_rows, ATTN_W), BF16), pltpu.VMEM((2, step_rows, 2 * GMLP_W), F32),
                        pltpu.VMEM((2, step_rows, D_MODEL), BF16),
                        pltpu.SemaphoreType.DMA((6,)), pltpu.SemaphoreType.DMA((6,))],
        compiler_params=_cparams(("arbitrary",)),
    )(sinks, x1, x1, g, w_in_t, b_in, lng, lnb, w_s, bs_full, gao, ggo, w_out, b_out, gather)


def _norm_bwd_mix_out(dhp, x, dy, g, yb, w_out, *, tile, name):
    s = x.shape[0]

    def body(dhp_ref, x_ref, dy_ref, g_ref, y_ref, w_ref, dx_ref, dg_ref, dyy_ref, dw_ref, db_ref):
        @pl.when(pl.program_id(0) == 0)
        def _():
            dg_ref[...] = jnp.zeros_like(dg_ref)
            dw_ref[...] = jnp.zeros_like(dw_ref)
            db_ref[...] = jnp.zeros_like(db_ref)

        dh = ((dhp_ref[0].astype(F32) + dhp_ref[1].astype(F32))
              + (dhp_ref[2].astype(F32) + dhp_ref[3].astype(F32)))
        x_v = x_ref[...]
        r = lax.rsqrt(jnp.mean(x_v * x_v, axis=-1, keepdims=True) + EPS)
        dxn, dg = _rms_bwd(dh, x_v, r, g_ref[...])
        dx = dy_ref[...] + dxn
        dx_ref[...] = dx
        dg_ref[...] += dg
        dxb = dx.astype(BF16)
        db_ref[...] += jnp.sum(dx, axis=0, keepdims=True)
        dw_ref[...] += _dot_tn(y_ref[...], dxb)
        dyy_ref[...] = _dot_nt(dxb, w_ref[...])

    return pl.pallas_call(
        body, name=name, grid=(s // tile,),
        in_specs=[pl.BlockSpec((N_CHIPS, tile, D_MODEL), lambda i: (0, i, 0)),
                  _rows(tile, D_MODEL), _rows(tile, D_MODEL), _const((1, D_MODEL)), _rows(tile, D_MODEL),
                  _const((D_MODEL, D_MODEL))],
        out_specs=[_rows(tile, D_MODEL), _const((1, D_MODEL)), _rows(tile, D_MODEL), _const((D_MODEL, D_MODEL)),
                   _const((1, D_MODEL))],
        out_shape=[jax.ShapeDtypeStruct(x.shape, F32), jax.ShapeDtypeStruct((1, D_MODEL), F32),
                   jax.ShapeDtypeStruct(x.shape, F32), jax.ShapeDtypeStruct((D_MODEL, D_MODEL), F32),
                   jax.ShapeDtypeStruct((1, D_MODEL), F32)],
        compiler_params=_cparams(("arbitrary",)),
    )(dhp, x, dy, g, yb, w_out)


def _mix_core_bwd(dyy, q, k, v, zg, sinks, lng, lnb, w_s, bs_full, gao, ggo, *, name):
    s = dyy.shape[0]
    nb = min(MIX_BWD_BLOCKS, s // BLK)
    nsteps = s // (nb * BLK)

    def body(*refs):
        accumulators = refs[13:15] + refs[16:]

        @pl.when(pl.program_id(0) == 0)
        def _():
            for ref in accumulators:
                ref[...] = jnp.zeros_like(ref)

        for b in range(nb):
            one_block(pl.program_id(0) * nb + b, slice(b * BLK, (b + 1) * BLK), *refs)

        @pl.when(pl.program_id(0) == nsteps - 1)
        def _():
            tril = _tril_mask()
            for gi in range(GMLP_GROUPS):
                refs[20][gi] = jnp.where(tril, refs[20][gi], 0.0)

    def one_block(i, rows, sink_ref, dyy_ref, q_ref, k_ref, v_ref, z_ref, lng_ref, lnb_ref, ws_ref, bs_ref, gao_ref,
                  ggo_ref, dq_ref, dk_ref, dv_ref, dz_ref, dgao_ref, dggo_ref, dlng_ref, dlnb_ref, dws_ref, dms_ref,
                  dsk_ref):
        q_v = q_ref[rows, :]
        kb = _band(k_ref, i)
        vb = _band(v_ref, i)
        lng_v = lng_ref[...]
        gao_v = gao_ref[...]
        ggo_v = ggo_ref[...]

        y_attn, probs = _attn_fwd(q_v, kb, vb, i, sink_ref)
        u, xh, rstd, vvb, wms, mixed, gelu_grad = _gmlp_fwd_parts(z_ref[rows, :], lng_v, lnb_ref[...], ws_ref,
                                                                  bs_ref[...])
        y_gmlp = u * mixed
        ra = lax.rsqrt(jnp.mean(y_attn * y_attn, axis=-1, keepdims=True) + EPS)
        rg = lax.rsqrt(jnp.mean(y_gmlp * y_gmlp, axis=-1, keepdims=True) + EPS)

        dyy = dyy_ref[rows, :]
        d_attn, dgao = _rms_bwd(dyy[:, :ATTN_W], y_attn, ra, gao_v)
        d_gmlp, dggo = _rms_bwd(dyy[:, ATTN_W:], y_gmlp, rg, ggo_v)
        dgao_ref[...] += dgao
        dggo_ref[...] += dggo

        du = d_gmlp * mixed
        dmixed = d_gmlp * u
        dms_ref[...] += dmixed
        dmb = dmixed.astype(BF16)
        dvv_parts = []
        for gi in range(GMLP_GROUPS):
            sl = slice(gi * GROUP_DIM, (gi + 1) * GROUP_DIM)
            dws_ref[gi] += _dot_nt(dmb[:, sl], vvb[:, sl])
            dvv_parts.append(_dot_tn(wms[gi], dmb[:, sl]))
        dvv = jnp.concatenate(dvv_parts, axis=-1)
        dlng_ref[...] += jnp.sum(dvv * xh, axis=0, keepdims=True)
        dlnb_ref[...] += jnp.sum(dvv, axis=0, keepdims=True)
        dxh = dvv * lng_v
        dzv = rstd * (dxh - jnp.mean(dxh, axis=-1, keepdims=True)
                      - xh * jnp.mean(dxh * xh, axis=-1, keepdims=True))
        dz_ref[rows, :] = jnp.concatenate([du, dzv], axis=-1) * gelu_grad

        dab = d_attn.astype(BF16)
        own = _key_in_block()
        dq_parts = []
        dk_parts = []
        dv_parts = []
        for gi in range(N_KV_HEADS):
            cols = slice(gi * HEAD_DIM, (gi + 1) * HEAD_DIM)
            kg, vg = kb[:, cols], vb[:, cols]
            dkg = jnp.zeros((2 * BLK, HEAD_DIM), F32)
            dvg = jnp.zeros((2 * BLK, HEAD_DIM), F32)
            for rr in range(REP):
                h = gi * REP + rr
                hs = slice(h * HEAD_DIM, (h + 1) * HEAD_DIM)
                qh, doh = q_v[:, hs], dab[:, hs]
                pn, band, psink = probs[h]
                dp = _fold(_dot_nt(vg, doh), own)
                delta = jnp.sum(pn * dp, axis=0, keepdims=True)
                ds2 = _unfold(pn * (dp - delta), own)
                dsink = jnp.sum(-psink * delta, axis=-1, keepdims=True)
                dsk_ref[pl.ds(h, 1), :] += jnp.broadcast_to(dsink, (1, 128))
                dq_parts.append(_dot_tn(ds2, kg) * ATTN_SCALE)
                dkg = dkg + _dot(ds2, qh)
                dvg = dvg + _dot(band, doh)
            dk_parts.append(dkg)
            dv_parts.append(dvg)
        dq_ref[rows, :] = jnp.concatenate(dq_parts, axis=-1)
        dkb = jnp.concatenate(dk_parts, axis=-1)
        dvb = jnp.concatenate(dv_parts, axis=-1)
        prev = pl.ds(pl.multiple_of(jnp.maximum(i - 1, 0) * BLK, BLK), BLK)
        cur = pl.ds(pl.multiple_of(i * BLK, BLK), BLK)
        dk_ref[prev, :] += dkb[:BLK]
        dv_ref[prev, :] += dvb[:BLK]
        dk_ref[cur, :] += dkb[BLK:]
        dv_ref[cur, :] += dvb[BLK:]

    return pl.pallas_call(
        body, name=name, grid=(nsteps,),
        in_specs=[pl.BlockSpec(memory_space=pltpu.SMEM),
                  _rows(nb * BLK, D_MODEL), _rows(nb * BLK, ATTN_W), _const((s, KV_W)), _const((s, KV_W)),
                  _rows(nb * BLK, 2 * GMLP_W), _const((1, GMLP_W)), _const((1, GMLP_W)),
                  _const((GMLP_GROUPS, BLK, BLK)), _const((BLK, GMLP_W)), _const((1, ATTN_W)), _const((1, GMLP_W))],
        out_specs=[_rows(nb * BLK, ATTN_W), _const((s, KV_W)), _const((s, KV_W)), _rows(nb * BLK, 2 * GMLP_W),
                   _const((1, ATTN_W)), _const((1, GMLP_W)),
                   _const((1, GMLP_W)), _const((1, GMLP_W)), _const((GMLP_GROUPS, BLK, BLK)),
                   _const((BLK, GMLP_W)), _const((N_Q_HEADS, 128))],
        out_shape=[jax.ShapeDtypeStruct((s, ATTN_W), F32), jax.ShapeDtypeStruct((s, KV_W), F32),
                   jax.ShapeDtypeStruct((s, KV_W), F32), jax.ShapeDtypeStruct((s, 2 * GMLP_W), F32),
                   jax.ShapeDtypeStruct((1, ATTN_W), F32), jax.ShapeDtypeStruct((1, GMLP_W), F32),
                   jax.ShapeDtypeStruct((1, GMLP_W), F32), jax.ShapeDtypeStruct((1, GMLP_W), F32),
                   jax.ShapeDtypeStruct((GMLP_GROUPS, BLK, BLK), F32), jax.ShapeDtypeStruct((BLK, GMLP_W), F32),
                   jax.ShapeDtypeStruct((N_Q_HEADS, 128), F32)],
        compiler_params=_cparams(("arbitrary",)),
    )(sinks, dyy, q, k, v, zg, lng, lnb, w_s, bs_full, gao, ggo)


def _local_step(place, x, tgt, p, own_a, pack_a, pack_b, pack_m, *, tile=512, fwd_tile=256, bwd_tile=512,
                norm_tile=512):
    g = {}
    tile, fwd_tile, bwd_tile, norm_tile = (min(t_, x.shape[0]) for t_ in (tile, fwd_tile, bwd_tile, norm_tile))
    hb1, a1, b1, part1, pack_a = _ffn1_own(x, p["ffn1_norm_g"], own_a, pack_a, tile=tile, name="ffn1_own")
    x1, a1, b1, pack_m = _ffn1_others(place, x, hb1, part1, a1, b1, pack_a, pack_m, tile=fwd_tile, name="ffn1_fwd")
    w_in_t = pack_m[:, :IN_SH, :].reshape(IN_W, D_MODEL)
    w_out = pack_m[:, IN_SH:, :].reshape(D_MODEL, D_MODEL)
    q, k, v, zg, yb, x2, pack_b = _mixer_fwd(
        x1, p["mix_norm_g"], w_in_t, p["b_in"], p["attn_sinks"], p["gmlp_ln_g"], p["gmlp_ln_b"], p["gmlp_w_s"],
        p["bs_full"], p["attn_out_norm_g"], p["gmlp_out_norm_g"], w_out, p["b_out"], pack_b, name="mixer_fwd")
    mix_args = (q, k, v, zg, p["attn_sinks"], p["gmlp_ln_g"], p["gmlp_ln_b"], p["gmlp_w_s"], p["bs_full"],
                p["attn_out_norm_g"], p["gmlp_out_norm_g"])
    dx3, loss, g["final_norm_g"], hb2, a2, b2, do3 = _ffn_fwd_loss(
        x2, p["ffn2_norm_g"], pack_b, 0, p["final_norm_g"], tgt, tile=fwd_tile, name="ffn2_fwd_loss")

    dhp, land = _ffn_bwd(place, hb2, a2, b2, do3, pack_b, 1, None, None, tile=bwd_tile, name="ffn2_bwd")
    dx2, g["ffn2_norm_g"], dyy, dw_out, g["b_out"] = _norm_bwd_mix_out(
        dhp, x2, dx3, p["ffn2_norm_g"], yb, w_out, tile=norm_tile, name="ffn2_norm_bwd")

    (dq, dk, dv, dz, g["attn_out_norm_g"], g["gmlp_out_norm_g"], g["gmlp_ln_g"],
     g["gmlp_ln_b"], g["gmlp_w_s"], dmix_sum, dsinks) = _mix_core_bwd(dyy, *mix_args, name="mix_core_bwd")
    g["gmlp_b_s"] = dmix_sum
    g["attn_sinks"] = dsinks
    dx1, dw_in_t, g["b_in"], g["mix_norm_g"], do1 = _mix_in_bwd(
        x1, dx2, dq, dk, dv, dz, p["mix_norm_g"], w_in_t, tile=tile, name="mix_in_bwd")
    mix_grads = _mix_grads_pack(dw_in_t, dw_out, name="mix_grads_pack")

    dhp1, land = _ffn_bwd(place, hb1, a1, b1, do1, pack_a, 0, land, mix_grads, True, tile=bwd_tile, name="ffn1_bwd")
    dx0, g["ffn1_norm_g"] = _norm_bwd(dhp1, x, dx1, p["ffn1_norm_g"], tile=norm_tile, name="ffn1_norm_bwd")
    return loss, dx0, land, g


def _pack_cast(place, parts, *, name):
    def body(place_ref, *refs):
        oa_ref, ob_ref, om_ref, own_ref = refs[-4:]
        off = 0
        for k, (ref, rows) in enumerate(zip(refs[:-4], BIG_ROWS)):
            if k in (3, 6):
                off = 0
            cast = ref[...].astype(BF16)
            (oa_ref if k < 3 else ob_ref if k < 6 else om_ref)[0, off:off + rows, :] = cast
            if k < 3:
                own_ref[off:off + rows, :] = cast
            off += rows

    one = pl.Buffered(1)

    def slab(rows):
        return pl.BlockSpec((1, rows, D_MODEL), lambda i, pr: (pr[0], 0, 0), pipeline_mode=one)

    grid_spec = pltpu.PrefetchScalarGridSpec(
        num_scalar_prefetch=1, grid=(1,),
        in_specs=[pl.BlockSpec((rows, D_MODEL), lambda i, pr: (0, 0), pipeline_mode=one) for rows in BIG_ROWS],
        out_specs=[slab(PACK_A_ROWS), slab(PACK_B_ROWS), slab(PACK_M_ROWS),
                   pl.BlockSpec((PACK_A_ROWS, D_MODEL), lambda i, pr: (0, 0), pipeline_mode=one)])
    return pl.pallas_call(
        body, name=name, grid_spec=grid_spec,
        out_shape=[jax.ShapeDtypeStruct((N_CHIPS, PACK_A_ROWS, D_MODEL), BF16),
                   jax.ShapeDtypeStruct((N_CHIPS, PACK_B_ROWS, D_MODEL), BF16),
                   jax.ShapeDtypeStruct((N_CHIPS, PACK_M_ROWS, D_MODEL), BF16),
                   jax.ShapeDtypeStruct((PACK_A_ROWS, D_MODEL), BF16)],
        compiler_params=_cparams(("arbitrary",)),
    )(place, *parts)


def _shard_tile(i, c):
    return jnp.where(i < 3, 3 * c + i, jnp.where(i < 6, 3 + 3 * c + i, 12 + c))


def _rs_reduce(place, land, *, name):
    def body(place_ref, l_ref, o_ref):
        acc = l_ref[0].astype(F32)
        for d in range(1, 2 * N_CHIPS):
            acc = acc + l_ref[d].astype(F32)
        o_ref[...] = acc

    grid_spec = pltpu.PrefetchScalarGridSpec(
        num_scalar_prefetch=1, grid=(HALF_ROWS // MIX_HALF,),
        in_specs=[pl.BlockSpec((2 * N_CHIPS, MIX_HALF, D_MODEL), lambda i, pr: (0, i, 0))],
        out_specs=pl.BlockSpec((MIX_HALF, D_MODEL), lambda i, pr: (_shard_tile(i, pr[1]), 0)))
    return pl.pallas_call(
        body, name=name, grid_spec=grid_spec,
        out_shape=jax.ShapeDtypeStruct((PACK_ROWS, D_MODEL), F32),
        compiler_params=_cparams(("arbitrary",)),
    )(place, land)


def _small_all_reduce(packed, shard, *, name):
    rows = packed.shape[0]
    half = rows // 2

    def body(p_ref, sh_in_ref, o_ref, sh_ref, sib_ref, slots_ref, send_sems, recv_sems, share_send, share_recv):
        x, y, c, others = _mesh_place()
        me = 2 * x + y
        sibling = (x, y, 1 - c)
        share_start, share_finish = _share_stages(sh_ref, share_send, share_recv)
        share_start()

        def half_of(core):
            return pl.ds(pl.multiple_of(core * half, 8), half)

        def remote(k, src, dst, to):
            return pltpu.make_async_remote_copy(src_ref=src, dst_ref=dst, send_sem=send_sems.at[k],
                                                recv_sem=recv_sems.at[k], device_id=to, device_id_type=MESH)

        sib = remote(0, p_ref.at[half_of(1 - c)], sib_ref, sibling)
        sib.start()
        sib.wait()
        slots_ref[me] = p_ref[half_of(c), :] + sib_ref[...]
        sends = [remote(1 + j, slots_ref.at[me], slots_ref.at[me], (px, py, c)) for j, (px, py) in enumerate(others)]
        for cp in sends:
            cp.start()
        for j, (px, py) in enumerate(others):
            slab = slots_ref.at[2 * px + py]
            remote(1 + j, slab, slab, (px, py, c)).wait_recv()
        for cp in sends:
            cp.wait_send()
        o_ref[half_of(c), :] = (slots_ref[0] + slots_ref[1]) + (slots_ref[2] + slots_ref[3])
        back = remote(4, o_ref.at[half_of(c)], o_ref.at[half_of(c)], sibling)
        back.start()
        remote(4, o_ref.at[half_of(1 - c)], o_ref.at[half_of(1 - c)], sibling).wait_recv()
        back.wait_send()
        share_finish()

    vm = pl.BlockSpec(memory_space=pltpu.VMEM)
    hbm = pl.BlockSpec(memory_space=pl.ANY)
    return pl.pallas_call(
        body, name=name, in_specs=[vm, hbm], out_specs=[vm, hbm],
        out_shape=[jax.ShapeDtypeStruct((rows, 128), F32), jax.ShapeDtypeStruct(shard.shape, shard.dtype)],
        input_output_aliases={1: 1},
        scratch_shapes=[pltpu.VMEM((half, 128), F32), pltpu.VMEM((N_CHIPS, half, 128), F32),
                        pltpu.SemaphoreType.DMA((5,)), pltpu.SemaphoreType.DMA((5,)),
                        pltpu.SemaphoreType.DMA((3,)), pltpu.SemaphoreType.DMA((3,))],
    )(packed, shard)


def _adamw(w, g, m, v, *, g_row0, tile, name):
    rows, cols = w.shape
    assert g_row0 % tile == 0 and rows % tile == 0

    def body(w_ref, g_ref, m_ref, v_ref, go_ref, d_ref, nm_ref, nv_ref):
        g_v = g_ref[...]
        m_n = ADAM_B1 * m_ref[...] + (1.0 - ADAM_B1) * g_v
        v_n = ADAM_B2 * v_ref[...] + (1.0 - ADAM_B2) * (g_v * g_v)
        m_hat = m_n / (1.0 - ADAM_B1 ** ADAM_STEP)
        v_hat = v_n / (1.0 - ADAM_B2 ** ADAM_STEP)
        d_ref[...] = -ADAM_LR * (m_hat / (jnp.sqrt(v_hat) + ADAM_EPS) + ADAM_WD * w_ref[...])
        go_ref[...] = g_v
        nm_ref[...] = m_n
        nv_ref[...] = v_n

    spec = pl.BlockSpec((tile, cols), lambda i: (i, 0))
    gspec = pl.BlockSpec((tile, cols), lambda i: (g_row0 // tile + i, 0))
    shape = jax.ShapeDtypeStruct((rows, cols), F32)
    return pl.pallas_call(
        body, name=name, grid=(rows // tile,),
        in_specs=[spec, gspec, spec, spec], out_specs=[spec] * 4, out_shape=[shape] * 4,
        compiler_params=_cparams(("arbitrary",)),
    )(w, g, m, v)


def kernel(x, ffn1_norm_g, ffn1_w_gate, ffn1_w_up, ffn1_w_down, mix_norm_g, w_in, b_in, attn_sinks, gmlp_ln_g, gmlp_ln_b, gmlp_w_s, gmlp_b_s, attn_out_norm_g, gmlp_out_norm_g, w_out, b_out, ffn2_norm_g, ffn2_w_gate, ffn2_w_up, ffn2_w_down, final_norm_g, loss_target, m_ffn1_norm_g, m_ffn1_w_gate, m_ffn1_w_up, m_ffn1_w_down, m_mix_norm_g, m_w_in, m_b_in, m_attn_sinks, m_gmlp_ln_g, m_gmlp_ln_b, m_gmlp_w_s, m_gmlp_b_s, m_attn_out_norm_g, m_gmlp_out_norm_g, m_w_out, m_b_out, m_ffn2_norm_g, m_ffn2_w_gate, m_ffn2_w_up, m_ffn2_w_down, m_final_norm_g, v_ffn1_norm_g, v_ffn1_w_gate, v_ffn1_w_up, v_ffn1_w_down, v_mix_norm_g, v_w_in, v_b_in, v_attn_sinks, v_gmlp_ln_g, v_gmlp_ln_b, v_gmlp_w_s, v_gmlp_b_s, v_attn_out_norm_g, v_gmlp_out_norm_g, v_w_out, v_b_out, v_ffn2_norm_g, v_ffn2_w_gate, v_ffn2_w_up, v_ffn2_w_down, v_final_norm_g):
    f_args = dict(locals())
    weights = {n: f_args[n] for n in [nm for nm, _ in SMALL if nm != "loss"] + list(BIG)}
    shapes = {n: weights[n].shape for n in weights}
    shapes["loss"] = ()
    place = jnp.stack([2 * lax.axis_index("x") + lax.axis_index("y"), lax.axis_index("c")]).astype(jnp.int32)

    def with_cols(name, a):
        a2 = a.reshape(a.shape[-2], a.shape[-1])
        return a2.T if BIG_TRANSPOSED[BIG.index(name)] else a2

    def natural(name, a2):
        return (a2.T if BIG_TRANSPOSED[BIG.index(name)] else a2).reshape(shapes[name])

    pack_a, pack_b, pack_m, own_a = _pack_cast(place, [with_cols(n, weights[n]) for n in BIG], name="pack_cast")
    p = {n: weights[n].reshape(1, -1) for n in ("ffn1_norm_g", "mix_norm_g", "b_in", "gmlp_ln_g", "gmlp_ln_b",
                                                "attn_out_norm_g", "gmlp_out_norm_g", "b_out", "ffn2_norm_g",
                                                "final_norm_g")}
    p["attn_sinks"] = attn_sinks.reshape(N_Q_HEADS)
    p["gmlp_w_s"] = gmlp_w_s.reshape(GMLP_GROUPS, BLK, BLK)
    p["bs_full"] = jnp.broadcast_to(gmlp_b_s.reshape(GMLP_GROUPS, BLK).T[:, :, None],
                                    (BLK, GMLP_GROUPS, GROUP_DIM)).reshape(BLK, GMLP_W)

    loss_part, dx0, land, gs = _local_step(place, x[0], loss_target[0], p, own_a, pack_a, pack_b, pack_m)

    gs["gmlp_b_s"] = jnp.sum(gs["gmlp_b_s"].reshape(BLK, GMLP_GROUPS, GROUP_DIM), axis=-1).T
    gs["attn_sinks"] = gs["attn_sinks"][:, 0]
    gs["loss"] = loss_part[0, 0]
    small_sum, shard = _small_all_reduce(_pack_small(gs), _rs_reduce(place, land, name="rs_reduce"),
                                         name="small_all_reduce")

    grad_w, delta, new_m, new_v = {}, {}, {}, {}
    off = 0
    for n, rows in zip(BIG, BIG_ROWS):
        res = _adamw(with_cols(n, weights[n]), shard, with_cols(n, f_args["m_" + n]), with_cols(n, f_args["v_" + n]),
                     g_row0=off, tile=FF_SH // 2 if rows == FF_SH else 64, name="adamw_" + n)
        grad_w[n], delta[n], new_m[n], new_v[n] = [natural(n, a) for a in res]
        off += rows
    sm = {k: {n: f_args[k + n] for n, _ in SMALL if n != "loss"} for k in ("", "m_", "v_")}
    for k in sm:
        sm[k]["loss"] = jnp.zeros((), F32)
    res = _adamw(_pack_small(sm[""]), small_sum, _pack_small(sm["m_"]), _pack_small(sm["v_"]),
                 g_row0=0, tile=SMALL_ROWS, name="adamw_small")
    small = _unpack_small(res[0], shapes)
    for dst, packed in ((grad_w, res[0]), (delta, res[1]), (new_m, res[2]), (new_v, res[3])):
        dst.update({n: a for n, a in _unpack_small(packed, shapes).items() if n != "loss"})

    order = ('ffn1_norm_g', 'ffn1_w_gate', 'ffn1_w_up', 'ffn1_w_down', 'mix_norm_g', 'w_in', 'b_in', 'attn_sinks',
             'gmlp_ln_g', 'gmlp_ln_b', 'gmlp_w_s', 'gmlp_b_s', 'attn_out_norm_g', 'gmlp_out_norm_g', 'w_out', 'b_out',
             'ffn2_norm_g', 'ffn2_w_gate', 'ffn2_w_up', 'ffn2_w_down', 'final_norm_g')
    return (small["loss"], dx0.reshape(x.shape), *[grad_w[n] for n in order], *[delta[n] for n in order],
            *[new_m[n] for n in order], *[new_v[n] for n in order])
```

```python
import functools

import jax
import jax.numpy as jnp
from jax import lax
from jax.experimental import pallas as pl
from jax.experimental.pallas import tpu as pltpu

F32 = jnp.float32
BF16 = jnp.bfloat16

D_MODEL = 1024
D_FF = 2816
N_CHIPS = 4
FF_SH = D_FF // N_CHIPS
N_Q_HEADS = 8
N_KV_HEADS = 2
REP = N_Q_HEADS // N_KV_HEADS
HEAD_DIM = 64
ATTN_W = 512
KV_W = 128
GMLP_W = 512
GMLP_GROUPS = 8
GROUP_DIM = 64
BLK = 128
MIX_FWD_BLOCKS = 2
MIX_BWD_BLOCKS = 4
IN_W = 1792
IN_SH = IN_W // N_CHIPS
OUT_SH = D_MODEL // N_CHIPS
EPS = 1e-6
FFN_RES = 0.5
ATTN_SCALE = HEAD_DIM ** -0.5

ADAM_LR = 0.001
ADAM_B1 = 0.9
ADAM_B2 = 0.999
ADAM_EPS = 1e-08
ADAM_WD = 0.01
ADAM_STEP = 10

V7X_VMEM_LIMIT = 56 * 1024 * 1024
MESH = pl.DeviceIdType.MESH


def _cparams(sem):
    return pltpu.CompilerParams(dimension_semantics=sem, vmem_limit_bytes=V7X_VMEM_LIMIT)


def _dot(a, b):
    return jnp.dot(a, b, preferred_element_type=F32)


def _dot_nt(a, b):
    return lax.dot_general(a, b, (((1,), (1,)), ((), ())), preferred_element_type=F32)


def _dot_tn(a, b):
    return lax.dot_general(a, b, (((0,), (0,)), ((), ())), preferred_element_type=F32)


def _rms(x, g):
    r = lax.rsqrt(jnp.mean(x * x, axis=-1, keepdims=True) + EPS)
    return x * r * g, r


def _rms_bwd(dh, x, r, g):
    gy = dh * g
    dx = r * gy - x * (r * r * r) * jnp.mean(gy * x, axis=-1, keepdims=True)
    dg = jnp.sum(dh * x * r, axis=0, keepdims=True)
    return dx, dg


def _const(shape):
    nd = len(shape)
    return pl.BlockSpec(shape, lambda *_: (0,) * nd)


def _rows(t, w):
    return pl.BlockSpec((t, w), lambda i: (i, 0))


PACK_ROWS = 7 * FF_SH
HALF_ROWS = PACK_ROWS // 2
FFN_HALF = 3 * FF_SH // 2
MIX_HALF = FF_SH // 2
PACK_A_ROWS = 3 * FF_SH
PACK_B_ROWS = 3 * FF_SH
PACK_M_ROWS = FF_SH
BIG = ("ffn1_w_gate", "ffn1_w_up", "ffn1_w_down", "ffn2_w_gate", "ffn2_w_up", "ffn2_w_down", "w_in", "w_out")
BIG_ROWS = (FF_SH, FF_SH, FF_SH, FF_SH, FF_SH, FF_SH, IN_SH, OUT_SH)
BIG_TRANSPOSED = (True, True, False, True, True, False, True, False)

SMALL = (("ffn1_norm_g", 1024), ("mix_norm_g", 1024), ("b_in", 1792), ("attn_sinks", 8), ("gmlp_ln_g", 512),
         ("gmlp_ln_b", 512), ("gmlp_w_s", 131072), ("gmlp_b_s", 1024), ("attn_out_norm_g", 512),
         ("gmlp_out_norm_g", 512), ("b_out", 1024), ("ffn2_norm_g", 1024), ("final_norm_g", 1024), ("loss", 1))


def _small_rows(n):
    return -(-n // 1024) * 8


SMALL_USED_ROWS = sum(_small_rows(n) for _, n in SMALL)
SMALL_ROWS = -(-SMALL_USED_ROWS // 16) * 16


def _pack_small(parts):
    out = []
    for name, n in SMALL:
        flat = parts[name].reshape(-1).astype(F32)
        rows = _small_rows(n)
        out.append(jnp.pad(flat, (0, rows * 128 - n)).reshape(rows, 128))
    if SMALL_ROWS > SMALL_USED_ROWS:
        out.append(jnp.zeros((SMALL_ROWS - SMALL_USED_ROWS, 128), F32))
    return jnp.concatenate(out, axis=0)


def _unpack_small(packed, shapes):
    res, off = {}, 0
    for name, n in SMALL:
        rows = _small_rows(n)
        res[name] = packed[off:off + rows].reshape(-1)[:n].reshape(shapes[name])
        off += rows
    return res


def _ffn_tile(x, g, wg_ref, wu_ref, wd_ref, hb_ref, a_ref, b_ref):
    h, _ = _rms(x, g)
    hb = h.astype(BF16)
    hb_ref[...] = hb
    acc = jnp.zeros(x.shape, F32)
    for j in range(N_CHIPS):
        a = _dot_nt(hb, wg_ref[j])
        b = _dot_nt(hb, wu_ref[j])
        a_ref[j] = a
        b_ref[j] = b
        f = (a * jax.nn.sigmoid(a) * b).astype(BF16)
        acc = acc + _dot(f, wd_ref[j])
    return x + FFN_RES * acc


def _ffn_saved_specs(s, tile):
    ab = pl.BlockSpec((N_CHIPS, tile, FF_SH), lambda i: (0, i, 0))
    shape = jax.ShapeDtypeStruct((N_CHIPS, s, FF_SH), F32)
    return [_rows(tile, D_MODEL), ab, ab], [jax.ShapeDtypeStruct((s, D_MODEL), BF16), shape, shape]


def _ffn_weight_specs(k0):
    one = pl.Buffered(1)
    return [pl.BlockSpec((N_CHIPS, FF_SH, D_MODEL), functools.partial(lambda kk, i: (0, kk, 0), k0 + d),
                         pipeline_mode=one) for d in range(3)]


def _mesh_place():
    x, y, c = lax.axis_index("x"), lax.axis_index("y"), lax.axis_index("c")
    others = [(1 - x, y), (x, 1 - y), (1 - x, 1 - y)]
    return x, y, c, others


def _gather_stages(o_ref, send_sems, recv_sems):
    x, y, c, others = _mesh_place()
    me = 2 * x + y
    sibling = (x, y, 1 - c)
    half_rows = o_ref.shape[1] // 2

    def half(slab, core):
        return o_ref.at[slab, pl.ds(pl.multiple_of(core * half_rows, 16), half_rows)]

    def copy(k, rows, to):
        return pltpu.make_async_remote_copy(src_ref=rows, dst_ref=rows, send_sem=send_sems.at[k],
                                            recv_sem=recv_sems.at[k], device_id=to, device_id_type=MESH)

    first = [copy(j, half(me, c), (px, py, c)) for j, (px, py) in enumerate(others)]
    passed = [copy(3 + j, half(2 * px + py, c), sibling) for j, (px, py) in enumerate(others)]

    def landed(j):
        px, py = others[j]
        copy(j, half(2 * px + py, c), (px, py, c)).wait_recv()
        passed[j].start()

    def sibling_landed(j):
        px, py = others[j]
        copy(3 + j, half(2 * px + py, 1 - c), sibling).wait_recv()

    def start():
        for cp in first:
            cp.start()

    def forward():
        for j in range(len(others)):
            landed(j)

    def finish():
        for j in range(len(others)):
            sibling_landed(j)
        for cp in first + passed:
            cp.wait_send()

    return start, forward, finish, (first, passed, landed, sibling_landed)


def _swiglu_slab(hb, wg, wu, wd):
    a = _dot_nt(hb, wg)
    b = _dot_nt(hb, wu)
    return a, b, _dot((a * jax.nn.sigmoid(a) * b).astype(BF16), wd)


def _ffn1_own(x, g, own, gather, *, tile, name):
    s = x.shape[0]
    nt = s // tile
    neighbour_of_pass = {1: 1, 2: 0}

    def body(x_ref, g_ref, wg_ref, wu_ref, wd_ref, gin_ref, hb_ref, a_ref, b_ref, p_ref, gat_ref, w2_ref, send_sems,
             recv_sems, w2_sem):
        ps, i = pl.program_id(0), pl.program_id(1)
        xi, yi, _, _ = _mesh_place()
        _, _, _, (first, passed, landed, sibling_landed) = _gather_stages(gat_ref, send_sems, recv_sems)

        @pl.when(jnp.logical_and(ps == 0, i == 0))
        def _():
            first[0].start()
            first[1].start()

        def take(j, slab):
            landed(j)
            sibling_landed(j)
            load = pltpu.make_async_copy(gat_ref.at[slab], w2_ref, w2_sem)
            load.start()
            load.wait()

        @pl.when(jnp.logical_and(ps == 1, i == 0))
        def _():
            first[0].wait_send()
            first[1].wait_send()
            first[2].start()
            take(neighbour_of_pass[1], 2 * xi + (1 - yi))

        @pl.when(jnp.logical_and(ps == 2, i == 0))
        def _():
            take(neighbour_of_pass[2], 2 * (1 - xi) + yi)

        h, _ = _rms(x_ref[...], g_ref[...])
        hb = h.astype(BF16)

        @pl.when(ps == 0)
        def _():
            hb_ref[...] = hb
            a_ref[0], b_ref[0], p_ref[0] = _swiglu_slab(hb, wg_ref[...], wu_ref[...], wd_ref[...])

        @pl.when(ps > 0)
        def _():
            a_ref[0], b_ref[0], p_ref[0] = _swiglu_slab(hb, w2_ref[0:FF_SH, :], w2_ref[FF_SH:2 * FF_SH, :],
                                                        w2_ref[2 * FF_SH:3 * FF_SH, :])

        @pl.when(jnp.logical_and(ps == 2, i == nt - 1))
        def _():
            landed(2)
            sibling_landed(2)
            for cp in [first[2]] + passed:
                cp.wait_send()

    one = pl.Buffered(1)
    wspecs = [pl.BlockSpec((FF_SH, D_MODEL), functools.partial(lambda kk, ps, i: (kk, 0), k), pipeline_mode=one)
              for k in range(3)]
    hbm = pl.BlockSpec(memory_space=pl.ANY)
    tiles = pl.BlockSpec((tile, D_MODEL), lambda ps, i: (i, 0))
    first_pass_tiles = pl.BlockSpec((tile, D_MODEL), lambda ps, i: (jnp.where(ps == 0, i, nt - 1), 0))
    by_pass = lambda w: pl.BlockSpec((1, tile, w), lambda ps, i: (ps, i, 0))
    return pl.pallas_call(
        body, name=name, grid=(3, nt),
        in_specs=[tiles, pl.BlockSpec((1, D_MODEL), lambda ps, i: (0, 0))] + wspecs + [hbm],
        out_specs=[first_pass_tiles, by_pass(FF_SH), by_pass(FF_SH), by_pass(D_MODEL), hbm],
        out_shape=[jax.ShapeDtypeStruct((s, D_MODEL), BF16), jax.ShapeDtypeStruct((N_CHIPS, s, FF_SH), F32),
                   jax.ShapeDtypeStruct((N_CHIPS, s, FF_SH), F32), jax.ShapeDtypeStruct((3, s, D_MODEL), F32),
                   jax.ShapeDtypeStruct(gather.shape, gather.dtype)],
        input_output_aliases={5: 4},
        scratch_shapes=[pltpu.VMEM((PACK_A_ROWS, D_MODEL), BF16), pltpu.SemaphoreType.DMA((6,)),
                        pltpu.SemaphoreType.DMA((6,)), pltpu.SemaphoreType.DMA],
        compiler_params=_cparams(("arbitrary", "arbitrary")),
    )(x, g, own, own, own, gather)


def _ffn1_others(place, x, hb, p_own, a_all, b_all, pack, gather, *, tile, name):
    s = x.shape[0]
    nt = s // tile
    forward_at = max(nt - 6, 0)

    def body(place_ref, x_ref, hb_ref, p_ref, *rest):
        w_refs = rest[:3]
        o_ref, a_ref, b_ref, gat_ref, send_sems, recv_sems = rest[6:]
        i = pl.program_id(0)
        start, forward, finish, _ = _gather_stages(gat_ref, send_sems, recv_sems)
        pl.when(i == 0)(start)
        hb = hb_ref[...]
        a_ref[0], b_ref[0], part = _swiglu_slab(hb, w_refs[0][0], w_refs[1][0], w_refs[2][0])
        o_ref[...] = x_ref[...] + FFN_RES * ((p_ref[0] + p_ref[1]) + (p_ref[2] + part))
        pl.when(i == forward_at)(forward)
        pl.when(i == nt - 1)(finish)

    one = pl.Buffered(1)

    def wspec(kk):
        return pl.BlockSpec((1, FF_SH, D_MODEL), lambda i, pr: (jnp.bitwise_xor(pr[0], N_CHIPS - 1), kk, 0),
                            pipeline_mode=one)

    rows = lambda w: pl.BlockSpec((tile, w), lambda i, pr: (i, 0))
    ab = pl.BlockSpec((1, tile, FF_SH), lambda i, pr: (N_CHIPS - 1, i, 0))
    ab_shape = jax.ShapeDtypeStruct((N_CHIPS, s, FF_SH), F32)
    hbm = pl.BlockSpec(memory_space=pl.ANY)
    grid_spec = pltpu.PrefetchScalarGridSpec(
        num_scalar_prefetch=1, grid=(nt,),
        in_specs=[rows(D_MODEL), rows(D_MODEL), pl.BlockSpec((3, tile, D_MODEL), lambda i, pr: (0, i, 0))]
                 + [wspec(kk) for kk in range(3)] + [hbm, hbm, hbm],
        out_specs=[rows(D_MODEL), ab, ab, hbm],
        scratch_shapes=[pltpu.SemaphoreType.DMA((6,)), pltpu.SemaphoreType.DMA((6,))])
    return pl.pallas_call(
        body, name=name, grid_spec=grid_spec,
        out_shape=[jax.ShapeDtypeStruct(x.shape, F32), ab_shape, ab_shape,
                   jax.ShapeDtypeStruct(gather.shape, gather.dtype)],
        input_output_aliases={7: 1, 8: 2, 9: 3},
        compiler_params=_cparams(("arbitrary",)),
    )(place, x, hb, p_own, *([pack] * 3), a_all, b_all, gather)


def _ffn_fwd_loss(x, g, pack, k0, gf, tgt, *, tile, name):
    s = x.shape[0]

    def body(x_ref, g_ref, wg_ref, wu_ref, wd_ref, gf_ref, t_ref, dx_ref, loss_ref, dgf_ref, hb_ref, a_ref, b_ref,
             do_ref):
        @pl.when(pl.program_id(0) == 0)
        def _():
            loss_ref[...] = jnp.zeros_like(loss_ref)
            dgf_ref[...] = jnp.zeros_like(dgf_ref)

        x3 = _ffn_tile(x_ref[...], g_ref[...], wg_ref, wu_ref, wd_ref, hb_ref, a_ref, b_ref)
        gf_v = gf_ref[...]
        out, r = _rms(x3, gf_v)
        diff = out - t_ref[...]
        part = jnp.sum(jnp.sum(diff * diff, axis=-1, keepdims=True), axis=0, keepdims=True)
        loss_ref[...] += jnp.broadcast_to(part * (0.5 / D_MODEL), loss_ref.shape)
        dx, dg = _rms_bwd(diff * (1.0 / D_MODEL), x3, r, gf_v)
        dx_ref[...] = dx
        do_ref[...] = (FFN_RES * dx).astype(BF16)
        dgf_ref[...] += dg

    saved_specs, saved_shapes = _ffn_saved_specs(s, tile)
    return pl.pallas_call(
        body, name=name, grid=(s // tile,),
        in_specs=[_rows(tile, D_MODEL), _const((1, D_MODEL))] + _ffn_weight_specs(k0)
                 + [_const((1, D_MODEL)), _rows(tile, D_MODEL)],
        out_specs=[_rows(tile, D_MODEL), _const((1, 128)), _const((1, D_MODEL))] + saved_specs
                  + [_rows(tile, D_MODEL)],
        out_shape=[jax.ShapeDtypeStruct(x.shape, F32),
                   jax.ShapeDtypeStruct((1, 128), F32),
                   jax.ShapeDtypeStruct((1, D_MODEL), F32)] + saved_shapes
                  + [jax.ShapeDtypeStruct(x.shape, BF16)],
        compiler_params=_cparams(("arbitrary",)),
    )(x, g, pack, pack, pack, gf, tgt)


def _ffn_bwd(place, hb, a, b, do, pack, region, land, mix_grads, ab_by_pass=False, *, tile, name):
    s = hb.shape[0]
    nt = s // tile
    land_rows = pl.ds(region * FFN_HALF, FFN_HALF)
    mix_rows = pl.ds(2 * FFN_HALF, MIX_HALF)
    with_mix = mix_grads is not None
    with_land = land is not None
    n_others = 2 * N_CHIPS - 1

    def body(place_ref, hb_ref, a_ref, b_ref, do_ref, wg_ref, wu_ref, wd_ref, *rest):
        rest = list(rest)
        mix_ref = rest.pop(0) if with_mix else None
        if with_land:
            rest.pop(0)
        dhp_ref, land_ref, acc_ref, stage_ref, send_sems, recv_sem, local_sem = rest[:7]
        t, i = pl.program_id(0), pl.program_id(1)
        xi, yi, c = lax.axis_index("x"), lax.axis_index("y"), lax.axis_index("c")
        dev = 4 * xi + 2 * yi + c
        tt = (t + 1) % N_CHIPS
        tx, ty = jnp.bitwise_xor(xi, tt // 2), jnp.bitwise_xor(yi, tt % 2)

        def remote(src, dst, ssem, rsem, to):
            return pltpu.make_async_remote_copy(src_ref=src, dst_ref=dst, send_sem=ssem, recv_sem=rsem,
                                                device_id=to, device_id_type=MESH)

        def stage_half(h):
            return stage_ref.at[pl.ds(pl.multiple_of(h * FFN_HALF, 16), FFN_HALF)]

        if with_mix:
            mix_send, mix_recv, mix_local = rest[7:10]

            @pl.when(jnp.logical_and(t == 0, i == 0))
            def _():
                for chip in range(N_CHIPS):
                    for h in range(2):
                        src = mix_ref.at[chip, pl.ds(h * MIX_HALF, MIX_HALF)]
                        dst = land_ref.at[dev, mix_rows]
                        mine = jnp.logical_and(2 * xi + yi == chip, c == h)

                        @pl.when(mine)
                        def _():
                            pltpu.make_async_copy(src, dst, mix_local).start()

                        @pl.when(jnp.logical_not(mine))
                        def _():
                            remote(src, dst, mix_send, mix_recv, (chip // 2, chip % 2, h)).start()

        @pl.when(i == 0)
        def _():
            acc_ref[...] = jnp.zeros_like(acc_ref)

        hb = hb_ref[...]
        dob = do_ref[...]
        wg_j, wu_j, wd_j = wg_ref[0], wu_ref[0], wd_ref[0]
        a = a_ref[0]
        b = b_ref[0]
        sg = jax.nn.sigmoid(a)
        sa = a * sg
        fb = (sa * b).astype(BF16)
        df = _dot_nt(dob, wd_j)
        dbb = (df * sa).astype(BF16)
        dab = (df * b * (sg + sa * (1.0 - sg))).astype(BF16)
        dhp_ref[0] = (_dot(dab, wg_j) + _dot(dbb, wu_j)).astype(BF16)
        acc_ref[0:FF_SH, :] += _dot_tn(dab, hb)
        acc_ref[FF_SH:2 * FF_SH, :] += _dot_tn(dbb, hb)
        acc_ref[2 * FF_SH:3 * FF_SH, :] += _dot_tn(fb, dob)

        @pl.when(i == nt - 1)
        def _():
            dst = land_ref.at[dev, land_rows]

            @pl.when(t > 0)
            def _():
                for h in range(2):
                    remote(stage_half(h), dst, send_sems.at[h], recv_sem, (tx, ty, h)).wait_send()

            def cast_rows(r, carry):
                rows = pl.ds(pl.multiple_of(r * MIX_HALF, 16), MIX_HALF)
                stage_ref[rows, :] = acc_ref[rows, :].astype(BF16)
                return carry

            lax.fori_loop(0, 3 * FF_SH // MIX_HALF, cast_rows, 0)

            @pl.when(t < N_CHIPS - 1)
            def _():
                for h in range(2):
                    remote(stage_half(h), dst, send_sems.at[h], recv_sem, (tx, ty, h)).start()

            @pl.when(t == N_CHIPS - 1)
            def _():
                own = pltpu.make_async_copy(stage_half(c), dst, local_sem)
                own.start()
                sib = remote(stage_half(1 - c), dst, send_sems.at[0], recv_sem, (xi, yi, 1 - c))
                sib.start()
                sib.wait_send()
                own.wait()
                arrivals = land_ref.at[pl.ds(0, n_others), land_rows]
                remote(arrivals, arrivals, send_sems.at[0], recv_sem, (xi, yi, 1 - c)).wait_recv()
                if with_mix:
                    seven = land_ref.at[pl.ds(0, n_others), mix_rows]
                    both = remote(seven, seven, mix_send, mix_recv, (xi, yi, 1 - c))
                    both.wait_send()
                    both.wait_recv()
                    pltpu.make_async_copy(mix_ref.at[0, pl.ds(0, MIX_HALF)], land_ref.at[dev, mix_rows],
                                          mix_local).wait()

    def wspec(kk):
        return pl.BlockSpec((1, FF_SH, D_MODEL),
                            lambda t, i, pr: (jnp.bitwise_xor(pr[0], (t + 1) % N_CHIPS), kk, 0))

    xspec = pl.BlockSpec((tile, D_MODEL), lambda t, i, pr: (i, 0))
    if ab_by_pass:
        abspec = pl.BlockSpec((1, tile, FF_SH), lambda t, i, pr: ((t + 1) % N_CHIPS, i, 0))
    else:
        abspec = pl.BlockSpec((1, tile, FF_SH), lambda t, i, pr: (jnp.bitwise_xor(pr[0], (t + 1) % N_CHIPS), i, 0))
    hbm = pl.BlockSpec(memory_space=pl.ANY)
    in_specs = [xspec, abspec, abspec, xspec, wspec(0), wspec(1), wspec(2)]
    operands = [place, hb, a, b, do, pack, pack, pack]
    scratch = [pltpu.VMEM((3 * FF_SH, D_MODEL), F32), pltpu.VMEM((3 * FF_SH, D_MODEL), BF16),
               pltpu.SemaphoreType.DMA((2,)), pltpu.SemaphoreType.DMA, pltpu.SemaphoreType.DMA]
    if with_mix:
        in_specs.append(hbm)
        operands.append(mix_grads)
        scratch += [pltpu.SemaphoreType.DMA, pltpu.SemaphoreType.DMA, pltpu.SemaphoreType.DMA]
    aliases = {}
    if with_land:
        in_specs.append(hbm)
        operands.append(land)
        aliases = {len(operands) - 1: 1}
    grid_spec = pltpu.PrefetchScalarGridSpec(
        num_scalar_prefetch=1, grid=(N_CHIPS, nt), in_specs=in_specs,
        out_specs=[pl.BlockSpec((1, tile, D_MODEL), lambda t, i, pr: (t, i, 0)), hbm],
        scratch_shapes=scratch)
    return pl.pallas_call(
        body, name=name, grid_spec=grid_spec,
        out_shape=[jax.ShapeDtypeStruct((N_CHIPS, s, D_MODEL), BF16),
                   jax.ShapeDtypeStruct((2 * N_CHIPS, HALF_ROWS, D_MODEL), BF16)],
        input_output_aliases=aliases,
        compiler_params=_cparams(("arbitrary", "arbitrary")),
    )(*operands)


def _mix_grads_pack(dw_in_t, dw_out, *, name):
    def body(a_ref, b_ref, o_ref):
        o_ref[0, 0:IN_SH, :] = a_ref[0].astype(BF16)
        o_ref[0, IN_SH:FF_SH, :] = b_ref[0].astype(BF16)

    return pl.pallas_call(
        body, name=name, grid=(N_CHIPS,),
        in_specs=[pl.BlockSpec((1, IN_SH, D_MODEL), lambda j: (j, 0, 0)),
                  pl.BlockSpec((1, OUT_SH, D_MODEL), lambda j: (j, 0, 0))],
        out_specs=pl.BlockSpec((1, FF_SH, D_MODEL), lambda j: (j, 0, 0)),
        out_shape=jax.ShapeDtypeStruct((N_CHIPS, FF_SH, D_MODEL), BF16),
        compiler_params=_cparams(("arbitrary",)),
    )(dw_in_t.reshape(N_CHIPS, IN_SH, D_MODEL), dw_out.reshape(N_CHIPS, OUT_SH, D_MODEL))


def _share_stages(o_ref, send_sems, recv_sems):
    x, y, c, _ = _mesh_place()

    def rows(k, core):
        if k < 2:
            return o_ref.at[pl.ds(pl.multiple_of(k * 2 * FFN_HALF + core * FFN_HALF, 8), FFN_HALF)]
        return o_ref.at[pl.ds(pl.multiple_of(4 * FFN_HALF + core * MIX_HALF, 8), MIX_HALF)]

    def copy(k, core):
        return pltpu.make_async_remote_copy(src_ref=rows(k, core), dst_ref=rows(k, core), send_sem=send_sems.at[k],
                                            recv_sem=recv_sems.at[k], device_id=(x, y, 1 - c), device_id_type=MESH)

    sends = [copy(k, c) for k in range(3)]

    def start():
        for cp in sends:
            cp.start()

    def finish():
        for k in range(3):
            copy(k, 1 - c).wait_recv()
        for cp in sends:
            cp.wait_send()

    return start, finish


def _norm_bwd(dhp, x, dy, g, *, tile, name):
    s = x.shape[0]

    def body(dhp_ref, x_ref, dy_ref, g_ref, dx_ref, dg_ref):
        @pl.when(pl.program_id(0) == 0)
        def _():
            dg_ref[...] = jnp.zeros_like(dg_ref)

        dh = ((dhp_ref[0].astype(F32) + dhp_ref[1].astype(F32))
              + (dhp_ref[2].astype(F32) + dhp_ref[3].astype(F32)))
        x_v = x_ref[...]
        r = lax.rsqrt(jnp.mean(x_v * x_v, axis=-1, keepdims=True) + EPS)
        dx, dg = _rms_bwd(dh, x_v, r, g_ref[...])
        dx_ref[...] = dy_ref[...] + dx
        dg_ref[...] += dg

    return pl.pallas_call(
        body, name=name, grid=(s // tile,),
        in_specs=[pl.BlockSpec((N_CHIPS, tile, D_MODEL), lambda i: (0, i, 0)),
                  _rows(tile, D_MODEL), _rows(tile, D_MODEL), _const((1, D_MODEL))],
        out_specs=[_rows(tile, D_MODEL), _const((1, D_MODEL))],
        out_shape=[jax.ShapeDtypeStruct(x.shape, F32), jax.ShapeDtypeStruct((1, D_MODEL), F32)],
        compiler_params=_cparams(("arbitrary",)),
    )(dhp, x, dy, g)


def _mix_in_bwd(x, dy, dq, dk, dv, dz, g, w_in_t, *, tile, name):
    s = x.shape[0]

    def body(x_ref, dy_ref, dq_ref, dk_ref, dv_ref, dz_ref, g_ref, w_ref, dx_ref, dw_ref, db_ref, dg_ref, do_ref):
        @pl.when(pl.program_id(0) == 0)
        def _():
            dw_ref[...] = jnp.zeros_like(dw_ref)
            db_ref[...] = jnp.zeros_like(db_ref)
            dg_ref[...] = jnp.zeros_like(dg_ref)

        dproj = jnp.concatenate([dq_ref[...], dk_ref[...], dv_ref[...], dz_ref[...]], axis=-1)
        db_ref[...] += jnp.sum(dproj, axis=0, keepdims=True)
        dpb = dproj.astype(BF16)
        x_v = x_ref[...]
        g_v = g_ref[...]
        h, r = _rms(x_v, g_v)
        dw_ref[...] += _dot_tn(dpb, h.astype(BF16))
        dh = _dot(dpb, w_ref[...])
        dxn, dg = _rms_bwd(dh, x_v, r, g_v)
        dx = dy_ref[...] + dxn
        dx_ref[...] = dx
        do_ref[...] = (FFN_RES * dx).astype(BF16)
        dg_ref[...] += dg

    return pl.pallas_call(
        body, name=name, grid=(s // tile,),
        in_specs=[_rows(tile, D_MODEL), _rows(tile, D_MODEL), _rows(tile, ATTN_W), _rows(tile, KV_W),
                  _rows(tile, KV_W), _rows(tile, 2 * GMLP_W), _const((1, D_MODEL)), _const((IN_W, D_MODEL))],
        out_specs=[_rows(tile, D_MODEL), _const((IN_W, D_MODEL)), _const((1, IN_W)), _const((1, D_MODEL)),
                   _rows(tile, D_MODEL)],
        out_shape=[jax.ShapeDtypeStruct(x.shape, F32), jax.ShapeDtypeStruct((IN_W, D_MODEL), F32),
                   jax.ShapeDtypeStruct((1, IN_W), F32), jax.ShapeDtypeStruct((1, D_MODEL), F32),
                   jax.ShapeDtypeStruct(x.shape, BF16)],
        compiler_params=_cparams(("arbitrary",)),
    )(x, dy, dq, dk, dv, dz, g, w_in_t)


_GELU_C = 0.7978845608028654
_GELU_A = 0.044715


def _gelu_tanh(x):
    x2 = x * x
    return jnp.tanh(_GELU_C * (x + _GELU_A * (x2 * x))), x2


def _band(ref, i):
    prev = jnp.maximum(i - 1, 0)
    return jnp.concatenate([ref[pl.ds(pl.multiple_of(prev * BLK, BLK), BLK), :],
                            ref[pl.ds(pl.multiple_of(i * BLK, BLK), BLK), :]], axis=0)


def _key_in_block():
    return lax.broadcasted_iota(jnp.int32, (BLK, BLK), 0) <= lax.broadcasted_iota(jnp.int32, (BLK, BLK), 1)


def _fold(band, own):
    return jnp.where(own, band[BLK:], band[:BLK])


def _unfold(a, own):
    zero = jnp.zeros_like(a)
    return jnp.concatenate([jnp.where(own, zero, a), jnp.where(own, a, zero)], axis=0).astype(BF16)


def _attn_fwd(q, kb, vb, i, sink_ref):
    own = _key_in_block()
    outs, saved = [], []
    for h in range(N_Q_HEADS):
        cols = slice((h // REP) * HEAD_DIM, (h // REP + 1) * HEAD_DIM)
        s2 = _dot_nt(kb[:, cols], q[:, h * HEAD_DIM:(h + 1) * HEAD_DIM])
        sc = jnp.where(own, s2[BLK:], jnp.where(i > 0, s2[:BLK], -jnp.inf))
        sink = sink_ref[h]
        m = jnp.maximum(jnp.max(sc, axis=0, keepdims=True), sink)
        p = jnp.exp(sc - m)
        es = jnp.exp(sink - m)
        inv = 1.0 / (jnp.sum(p, axis=0, keepdims=True) + es)
        pn = p * inv
        band = _unfold(pn, own)
        outs.append(_dot_tn(band, vb[:, cols]))
        saved.append((pn, band, es * inv))
    return jnp.concatenate(outs, axis=-1), saved


def _tril_mask():
    t = lax.broadcasted_iota(jnp.int32, (BLK, BLK), 0)
    s_ = lax.broadcasted_iota(jnp.int32, (BLK, BLK), 1)
    return s_ <= t


def _gmlp_fwd_parts(zg, lng, lnb, ws_ref, bs_full):
    th, zg2 = _gelu_tanh(zg)
    z = 0.5 * zg * (1.0 + th)
    u = z[:, :GMLP_W]
    zv = z[:, GMLP_W:]
    mu = jnp.mean(zv, axis=-1, keepdims=True)
    zc = zv - mu
    rstd = lax.rsqrt(jnp.mean(zc * zc, axis=-1, keepdims=True) + EPS)
    xh = zc * rstd
    vvb = (xh * lng + lnb).astype(BF16)
    tril = _tril_mask()
    wms, parts = [], []
    for gi in range(GMLP_GROUPS):
        wm = jnp.where(tril, ws_ref[gi], 0.0).astype(BF16)
        wms.append(wm)
        parts.append(_dot(wm, vvb[:, gi * GROUP_DIM:(gi + 1) * GROUP_DIM]))
    mixed = jnp.concatenate(parts, axis=-1) + bs_full
    gelu_grad = 0.5 * (1.0 + th) + 0.5 * zg * (1.0 - th * th) * (_GELU_C * (1.0 + 3.0 * _GELU_A * zg2))
    return u, xh, rstd, vvb, wms, mixed, gelu_grad


def _mixer_fwd(x1, g, w_in_t, b_in, sinks, lng, lnb, w_s, bs_full, gao, ggo, w_out, b_out, gather, *, name):
    s = x1.shape[0]
    nb = min(MIX_FWD_BLOCKS, s // BLK)
    step_rows = nb * BLK
    last = s // step_rows - 1

    def tile_of(i, lag):
        return jnp.clip(i - lag, 0, last)

    def body(sink_ref, xa_ref, xc_ref, g_ref, wi_ref, bi_ref, lng_ref, lnb_ref, ws_ref, bs_ref, gao_ref, ggo_ref,
             wo_ref, bo_ref, gin_ref, q_ref, k_ref, v_ref, z_ref, y_ref, o_ref, gat_ref, qs_ref, zs_ref, ys_ref,
             send_sems, recv_sems):
        i = pl.program_id(0)
        start, forward, finish, _ = _gather_stages(gat_ref, send_sems, recv_sems)

        @pl.when(i == 0)
        def _():
            for ref in (k_ref, v_ref, qs_ref, zs_ref, ys_ref):
                ref[...] = jnp.zeros_like(ref)
            start()

        slot_a, slot_b, slot_c = i % 2, (i + 1) % 2, i % 2

        o_ref[...] = xc_ref[...] + (_dot(ys_ref[slot_c], wo_ref[...]) + bo_ref[...])

        tile_b = tile_of(i, 1)
        for b in range(nb):
            blk = tile_b * nb + b
            rows = slice(b * BLK, (b + 1) * BLK)
            y_attn, _ = _attn_fwd(qs_ref[slot_b, rows, :], _band(k_ref, blk), _band(v_ref, blk), blk, sink_ref)
            u, _, _, _, _, mixed, _ = _gmlp_fwd_parts(zs_ref[slot_b, rows, :], lng_ref[...], lnb_ref[...], ws_ref,
                                                      bs_ref[...])
            ya, _ = _rms(y_attn, gao_ref[...])
            yg, _ = _rms(u * mixed, ggo_ref[...])
            y_blk = jnp.concatenate([ya, yg], axis=-1).astype(BF16)
            y_ref[rows, :] = y_blk
            ys_ref[slot_b, rows, :] = y_blk

        h, _ = _rms(xa_ref[...], g_ref[...])
        proj = _dot_nt(h.astype(BF16), wi_ref[...]) + bi_ref[...]
        q_t = (proj[:, :ATTN_W] * ATTN_SCALE).astype(BF16)
        z_t = proj[:, ATTN_W + 2 * KV_W:]
        here = pl.ds(pl.multiple_of(tile_of(i, 0) * step_rows, step_rows), step_rows)
        q_ref[...] = q_t
        z_ref[...] = z_t
        qs_ref[slot_a] = q_t
        zs_ref[slot_a] = z_t
        k_ref[here, :] = proj[:, ATTN_W:ATTN_W + KV_W].astype(BF16)
        v_ref[here, :] = proj[:, ATTN_W + KV_W:ATTN_W + 2 * KV_W].astype(BF16)
        pl.when(i == max(last - 3, 0))(forward)
        pl.when(i == last + 2)(finish)

    def lagged(width, lag):
        return pl.BlockSpec((step_rows, width), lambda i: (tile_of(i, lag), 0))

    return pl.pallas_call(
        body, name=name, grid=(last + 3,),
        in_specs=[pl.BlockSpec(memory_space=pltpu.SMEM),
                  lagged(D_MODEL, 0), lagged(D_MODEL, 2), _const((1, D_MODEL)), _const((IN_W, D_MODEL)),
                  _const((1, IN_W)), _const((1, GMLP_W)), _const((1, GMLP_W)), _const((GMLP_GROUPS, BLK, BLK)),
                  _const((BLK, GMLP_W)), _const((1, ATTN_W)), _const((1, GMLP_W)), _const((D_MODEL, D_MODEL)),
                  _const((1, D_MODEL)), pl.BlockSpec(memory_space=pl.ANY)],
        out_specs=[lagged(ATTN_W, 0), _const((s, KV_W)), _const((s, KV_W)), lagged(2 * GMLP_W, 0),
                   lagged(D_MODEL, 1), lagged(D_MODEL, 2), pl.BlockSpec(memory_space=pl.ANY)],
        out_shape=[jax.ShapeDtypeStruct((s, ATTN_W), BF16), jax.ShapeDtypeStruct((s, KV_W), BF16),
                   jax.ShapeDtypeStruct((s, KV_W), BF16), jax.ShapeDtypeStruct((s, 2 * GMLP_W), F32),
                   jax.ShapeDtypeStruct((s, D_MODEL), BF16), jax.ShapeDtypeStruct((s, D_MODEL), F32),
                   jax.ShapeDtypeStruct(gather.shape, gather.dtype)],
        input_output_aliases={14: 6},
        scratch_shapes=[pltpu.VMEM((2, step_rows, ATTN_W), BF16), pltpu.VMEM((2, step_rows, 2 * GMLP_W), F32),
                        pltpu.VMEM((2, step_rows, D_MODEL), BF16),
                        pltpu.SemaphoreType.DMA((6,)), pltpu.SemaphoreType.DMA((6,))],
        compiler_params=_cparams(("arbitrary",)),
    )(sinks, x1, x1, g, w_in_t, b_in, lng, lnb, w_s, bs_full, gao, ggo, w_out, b_out, gather)


def _norm_bwd_mix_out(dhp, x, dy, g, yb, w_out, *, tile, name):
    s = x.shape[0]

    def body(dhp_ref, x_ref, dy_ref, g_ref, y_ref, w_ref, dx_ref, dg_ref, dyy_ref, dw_ref, db_ref):
        @pl.when(pl.program_id(0) == 0)
        def _():
            dg_ref[...] = jnp.zeros_like(dg_ref)
            dw_ref[...] = jnp.zeros_like(dw_ref)
            db_ref[...] = jnp.zeros_like(db_ref)

        dh = ((dhp_ref[0].astype(F32) + dhp_ref[1].astype(F32))
              + (dhp_ref[2].astype(F32) + dhp_ref[3].astype(F32)))
        x_v = x_ref[...]
        r = lax.rsqrt(jnp.mean(x_v * x_v, axis=-1, keepdims=True) + EPS)
        dxn, dg = _rms_bwd(dh, x_v, r, g_ref[...])
        dx = dy_ref[...] + dxn
        dx_ref[...] = dx
        dg_ref[...] += dg
        dxb = dx.astype(BF16)
        db_ref[...] += jnp.sum(dx, axis=0, keepdims=True)
        dw_ref[...] += _dot_tn(y_ref[...], dxb)
        dyy_ref[...] = _dot_nt(dxb, w_ref[...])

    return pl.pallas_call(
        body, name=name, grid=(s // tile,),
        in_specs=[pl.BlockSpec((N_CHIPS, tile, D_MODEL), lambda i: (0, i, 0)),
                  _rows(tile, D_MODEL), _rows(tile, D_MODEL), _const((1, D_MODEL)), _rows(tile, D_MODEL),
                  _const((D_MODEL, D_MODEL))],
        out_specs=[_rows(tile, D_MODEL), _const((1, D_MODEL)), _rows(tile, D_MODEL), _const((D_MODEL, D_MODEL)),
                   _const((1, D_MODEL))],
        out_shape=[jax.ShapeDtypeStruct(x.shape, F32), jax.ShapeDtypeStruct((1, D_MODEL), F32),
                   jax.ShapeDtypeStruct(x.shape, F32), jax.ShapeDtypeStruct((D_MODEL, D_MODEL), F32),
                   jax.ShapeDtypeStruct((1, D_MODEL), F32)],
        compiler_params=_cparams(("arbitrary",)),
    )(dhp, x, dy, g, yb, w_out)


def _mix_core_bwd(dyy, q, k, v, zg, sinks, lng, lnb, w_s, bs_full, gao, ggo, *, name):
    s = dyy.shape[0]
    nb = min(MIX_BWD_BLOCKS, s // BLK)
    nsteps = s // (nb * BLK)

    def body(*refs):
        accumulators = refs[13:15] + refs[16:]

        @pl.when(pl.program_id(0) == 0)
        def _():
            for ref in accumulators:
                ref[...] = jnp.zeros_like(ref)

        for b in range(nb):
            one_block(pl.program_id(0) * nb + b, slice(b * BLK, (b + 1) * BLK), *refs)

        @pl.when(pl.program_id(0) == nsteps - 1)
        def _():
            tril = _tril_mask()
            for gi in range(GMLP_GROUPS):
                refs[20][gi] = jnp.where(tril, refs[20][gi], 0.0)

    def one_block(i, rows, sink_ref, dyy_ref, q_ref, k_ref, v_ref, z_ref, lng_ref, lnb_ref, ws_ref, bs_ref, gao_ref,
                  ggo_ref, dq_ref, dk_ref, dv_ref, dz_ref, dgao_ref, dggo_ref, dlng_ref, dlnb_ref, dws_ref, dms_ref,
                  dsk_ref):
        q_v = q_ref[rows, :]
        kb = _band(k_ref, i)
        vb = _band(v_ref, i)
        lng_v = lng_ref[...]
        gao_v = gao_ref[...]
        ggo_v = ggo_ref[...]

        y_attn, probs = _attn_fwd(q_v, kb, vb, i, sink_ref)
        u, xh, rstd, vvb, wms, mixed, gelu_grad = _gmlp_fwd_parts(z_ref[rows, :], lng_v, lnb_ref[...], ws_ref,
                                                                  bs_ref[...])
        y_gmlp = u * mixed
        ra = lax.rsqrt(jnp.mean(y_attn * y_attn, axis=-1, keepdims=True) + EPS)
        rg = lax.rsqrt(jnp.mean(y_gmlp * y_gmlp, axis=-1, keepdims=True) + EPS)

        dyy = dyy_ref[rows, :]
        d_attn, dgao = _rms_bwd(dyy[:, :ATTN_W], y_attn, ra, gao_v)
        d_gmlp, dggo = _rms_bwd(dyy[:, ATTN_W:], y_gmlp, rg, ggo_v)
        dgao_ref[...] += dgao
        dggo_ref[...] += dggo

        du = d_gmlp * mixed
        dmixed = d_gmlp * u
        dms_ref[...] += dmixed
        dmb = dmixed.astype(BF16)
        dvv_parts = []
        for gi in range(GMLP_GROUPS):
            sl = slice(gi * GROUP_DIM, (gi + 1) * GROUP_DIM)
            dws_ref[gi] += _dot_nt(dmb[:, sl], vvb[:, sl])
            dvv_parts.append(_dot_tn(wms[gi], dmb[:, sl]))
        dvv = jnp.concatenate(dvv_parts, axis=-1)
        dlng_ref[...] += jnp.sum(dvv * xh, axis=0, keepdims=True)
        dlnb_ref[...] += jnp.sum(dvv, axis=0, keepdims=True)
        dxh = dvv * lng_v
        dzv = rstd * (dxh - jnp.mean(dxh, axis=-1, keepdims=True)
                      - xh * jnp.mean(dxh * xh, axis=-1, keepdims=True))
        dz_ref[rows, :] = jnp.concatenate([du, dzv], axis=-1) * gelu_grad

        dab = d_attn.astype(BF16)
        own = _key_in_block()
        dq_parts = []
        dk_parts = []
        dv_parts = []
        for gi in range(N_KV_HEADS):
            cols = slice(gi * HEAD_DIM, (gi + 1) * HEAD_DIM)
            kg, vg = kb[:, cols], vb[:, cols]
            dkg = jnp.zeros((2 * BLK, HEAD_DIM), F32)
            dvg = jnp.zeros((2 * BLK, HEAD_DIM), F32)
            for rr in range(REP):
                h = gi * REP + rr
                hs = slice(h * HEAD_DIM, (h + 1) * HEAD_DIM)
                qh, doh = q_v[:, hs], dab[:, hs]
                pn, band, psink = probs[h]
                dp = _fold(_dot_nt(vg, doh), own)
                delta = jnp.sum(pn * dp, axis=0, keepdims=True)
                ds2 = _unfold(pn * (dp - delta), own)
                dsink = jnp.sum(-psink * delta, axis=-1, keepdims=True)
                dsk_ref[pl.ds(h, 1), :] += jnp.broadcast_to(dsink, (1, 128))
                dq_parts.append(_dot_tn(ds2, kg) * ATTN_SCALE)
                dkg = dkg + _dot(ds2, qh)
                dvg = dvg + _dot(band, doh)
            dk_parts.append(dkg)
            dv_parts.append(dvg)
        dq_ref[rows, :] = jnp.concatenate(dq_parts, axis=-1)
        dkb = jnp.concatenate(dk_parts, axis=-1)
        dvb = jnp.concatenate(dv_parts, axis=-1)
        prev = pl.ds(pl.multiple_of(jnp.maximum(i - 1, 0) * BLK, BLK), BLK)
        cur = pl.ds(pl.multiple_of(i * BLK, BLK), BLK)
        dk_ref[prev, :] += dkb[:BLK]
        dv_ref[prev, :] += dvb[:BLK]
        dk_ref[cur, :] += dkb[BLK:]
        dv_ref[cur, :] += dvb[BLK:]

    return pl.pallas_call(
        body, name=name, grid=(nsteps,),
        in_specs=[pl.BlockSpec(memory_space=pltpu.SMEM),
                  _rows(nb * BLK, D_MODEL), _rows(nb * BLK, ATTN_W), _const((s, KV_W)), _const((s, KV_W)),
                  _rows(nb * BLK, 2 * GMLP_W), _const((1, GMLP_W)), _const((1, GMLP_W)),
                  _const((GMLP_GROUPS, BLK, BLK)), _const((BLK, GMLP_W)), _const((1, ATTN_W)), _const((1, GMLP_W))],
        out_specs=[_rows(nb * BLK, ATTN_W), _const((s, KV_W)), _const((s, KV_W)), _rows(nb * BLK, 2 * GMLP_W),
                   _const((1, ATTN_W)), _const((1, GMLP_W)),
                   _const((1, GMLP_W)), _const((1, GMLP_W)), _const((GMLP_GROUPS, BLK, BLK)),
                   _const((BLK, GMLP_W)), _const((N_Q_HEADS, 128))],
        out_shape=[jax.ShapeDtypeStruct((s, ATTN_W), F32), jax.ShapeDtypeStruct((s, KV_W), F32),
                   jax.ShapeDtypeStruct((s, KV_W), F32), jax.ShapeDtypeStruct((s, 2 * GMLP_W), F32),
                   jax.ShapeDtypeStruct((1, ATTN_W), F32), jax.ShapeDtypeStruct((1, GMLP_W), F32),
                   jax.ShapeDtypeStruct((1, GMLP_W), F32), jax.ShapeDtypeStruct((1, GMLP_W), F32),
                   jax.ShapeDtypeStruct((GMLP_GROUPS, BLK, BLK), F32), jax.ShapeDtypeStruct((BLK, GMLP_W), F32),
                   jax.ShapeDtypeStruct((N_Q_HEADS, 128), F32)],
        compiler_params=_cparams(("arbitrary",)),
    )(sinks, dyy, q, k, v, zg, lng, lnb, w_s, bs_full, gao, ggo)


def _local_step(place, x, tgt, p, own_a, pack_a, pack_b, pack_m, *, tile=512, fwd_tile=256, bwd_tile=512,
                norm_tile=512):
    g = {}
    tile, fwd_tile, bwd_tile, norm_tile = (min(t_, x.shape[0]) for t_ in (tile, fwd_tile, bwd_tile, norm_tile))
    hb1, a1, b1, part1, pack_a = _ffn1_own(x, p["ffn1_norm_g"], own_a, pack_a, tile=tile, name="ffn1_own")
    x1, a1, b1, pack_m = _ffn1_others(place, x, hb1, part1, a1, b1, pack_a, pack_m, tile=fwd_tile, name="ffn1_fwd")
    w_in_t = pack_m[:, :IN_SH, :].reshape(IN_W, D_MODEL)
    w_out = pack_m[:, IN_SH:, :].reshape(D_MODEL, D_MODEL)
    q, k, v, zg, yb, x2, pack_b = _mixer_fwd(
        x1, p["mix_norm_g"], w_in_t, p["b_in"], p["attn_sinks"], p["gmlp_ln_g"], p["gmlp_ln_b"], p["gmlp_w_s"],
        p["bs_full"], p["attn_out_norm_g"], p["gmlp_out_norm_g"], w_out, p["b_out"], pack_b, name="mixer_fwd")
    mix_args = (q, k, v, zg, p["attn_sinks"], p["gmlp_ln_g"], p["gmlp_ln_b"], p["gmlp_w_s"], p["bs_full"],
                p["attn_out_norm_g"], p["gmlp_out_norm_g"])
    dx3, loss, g["final_norm_g"], hb2, a2, b2, do3 = _ffn_fwd_loss(
        x2, p["ffn2_norm_g"], pack_b, 0, p["final_norm_g"], tgt, tile=fwd_tile, name="ffn2_fwd_loss")

    dhp, land = _ffn_bwd(place, hb2, a2, b2, do3, pack_b, 1, None, None, tile=bwd_tile, name="ffn2_bwd")
    dx2, g["ffn2_norm_g"], dyy, dw_out, g["b_out"] = _norm_bwd_mix_out(
        dhp, x2, dx3, p["ffn2_norm_g"], yb, w_out, tile=norm_tile, name="ffn2_norm_bwd")

    (dq, dk, dv, dz, g["attn_out_norm_g"], g["gmlp_out_norm_g"], g["gmlp_ln_g"],
     g["gmlp_ln_b"], g["gmlp_w_s"], dmix_sum, dsinks) = _mix_core_bwd(dyy, *mix_args, name="mix_core_bwd")
    g["gmlp_b_s"] = dmix_sum
    g["attn_sinks"] = dsinks
    dx1, dw_in_t, g["b_in"], g["mix_norm_g"], do1 = _mix_in_bwd(
        x1, dx2, dq, dk, dv, dz, p["mix_norm_g"], w_in_t, tile=tile, name="mix_in_bwd")
    mix_grads = _mix_grads_pack(dw_in_t, dw_out, name="mix_grads_pack")

    dhp1, land = _ffn_bwd(place, hb1, a1, b1, do1, pack_a, 0, land, mix_grads, True, tile=bwd_tile, name="ffn1_bwd")
    dx0, g["ffn1_norm_g"] = _norm_bwd(dhp1, x, dx1, p["ffn1_norm_g"], tile=norm_tile, name="ffn1_norm_bwd")
    return loss, dx0, land, g


def _pack_cast(place, parts, *, name):
    def body(place_ref, *refs):
        oa_ref, ob_ref, om_ref, own_ref = refs[-4:]
        off = 0
        for k, (ref, rows) in enumerate(zip(refs[:-4], BIG_ROWS)):
            if k in (3, 6):
                off = 0
            cast = ref[...].astype(BF16)
            (oa_ref if k < 3 else ob_ref if k < 6 else om_ref)[0, off:off + rows, :] = cast
            if k < 3:
                own_ref[off:off + rows, :] = cast
            off += rows

    one = pl.Buffered(1)

    def slab(rows):
        return pl.BlockSpec((1, rows, D_MODEL), lambda i, pr: (pr[0], 0, 0), pipeline_mode=one)

    grid_spec = pltpu.PrefetchScalarGridSpec(
        num_scalar_prefetch=1, grid=(1,),
        in_specs=[pl.BlockSpec((rows, D_MODEL), lambda i, pr: (0, 0), pipeline_mode=one) for rows in BIG_ROWS],
        out_specs=[slab(PACK_A_ROWS), slab(PACK_B_ROWS), slab(PACK_M_ROWS),
                   pl.BlockSpec((PACK_A_ROWS, D_MODEL), lambda i, pr: (0, 0), pipeline_mode=one)])
    return pl.pallas_call(
        body, name=name, grid_spec=grid_spec,
        out_shape=[jax.ShapeDtypeStruct((N_CHIPS, PACK_A_ROWS, D_MODEL), BF16),
                   jax.ShapeDtypeStruct((N_CHIPS, PACK_B_ROWS, D_MODEL), BF16),
                   jax.ShapeDtypeStruct((N_CHIPS, PACK_M_ROWS, D_MODEL), BF16),
                   jax.ShapeDtypeStruct((PACK_A_ROWS, D_MODEL), BF16)],
        compiler_params=_cparams(("arbitrary",)),
    )(place, *parts)


def _shard_tile(i, c):
    return jnp.where(i < 3, 3 * c + i, jnp.where(i < 6, 3 + 3 * c + i, 12 + c))


def _rs_reduce(place, land, *, name):
    def body(place_ref, l_ref, o_ref):
        acc = l_ref[0].astype(F32)
        for d in range(1, 2 * N_CHIPS):
            acc = acc + l_ref[d].astype(F32)
        o_ref[...] = acc

    grid_spec = pltpu.PrefetchScalarGridSpec(
        num_scalar_prefetch=1, grid=(HALF_ROWS // MIX_HALF,),
        in_specs=[pl.BlockSpec((2 * N_CHIPS, MIX_HALF, D_MODEL), lambda i, pr: (0, i, 0))],
        out_specs=pl.BlockSpec((MIX_HALF, D_MODEL), lambda i, pr: (_shard_tile(i, pr[1]), 0)))
    return pl.pallas_call(
        body, name=name, grid_spec=grid_spec,
        out_shape=jax.ShapeDtypeStruct((PACK_ROWS, D_MODEL), F32),
        compiler_params=_cparams(("arbitrary",)),
    )(place, land)


def _small_all_reduce(packed, shard, *, name):
    rows = packed.shape[0]
    half = rows // 2

    def body(p_ref, sh_in_ref, o_ref, sh_ref, sib_ref, slots_ref, send_sems, recv_sems, share_send, share_recv):
        x, y, c, others = _mesh_place()
        me = 2 * x + y
        sibling = (x, y, 1 - c)
        share_start, share_finish = _share_stages(sh_ref, share_send, share_recv)
        share_start()

        def half_of(core):
            return pl.ds(pl.multiple_of(core * half, 8), half)

        def remote(k, src, dst, to):
            return pltpu.make_async_remote_copy(src_ref=src, dst_ref=dst, send_sem=send_sems.at[k],
                                                recv_sem=recv_sems.at[k], device_id=to, device_id_type=MESH)

        sib = remote(0, p_ref.at[half_of(1 - c)], sib_ref, sibling)
        sib.start()
        sib.wait()
        slots_ref[me] = p_ref[half_of(c), :] + sib_ref[...]
        sends = [remote(1 + j, slots_ref.at[me], slots_ref.at[me], (px, py, c)) for j, (px, py) in enumerate(others)]
        for cp in sends:
            cp.start()
        for j, (px, py) in enumerate(others):
            slab = slots_ref.at[2 * px + py]
            remote(1 + j, slab, slab, (px, py, c)).wait_recv()
        for cp in sends:
            cp.wait_send()
        o_ref[half_of(c), :] = (slots_ref[0] + slots_ref[1]) + (slots_ref[2] + slots_ref[3])
        back = remote(4, o_ref.at[half_of(c)], o_ref.at[half_of(c)], sibling)
        back.start()
        remote(4, o_ref.at[half_of(1 - c)], o_ref.at[half_of(1 - c)], sibling).wait_recv()
        back.wait_send()
        share_finish()

    vm = pl.BlockSpec(memory_space=pltpu.VMEM)
    hbm = pl.BlockSpec(memory_space=pl.ANY)
    return pl.pallas_call(
        body, name=name, in_specs=[vm, hbm], out_specs=[vm, hbm],
        out_shape=[jax.ShapeDtypeStruct((rows, 128), F32), jax.ShapeDtypeStruct(shard.shape, shard.dtype)],
        input_output_aliases={1: 1},
        scratch_shapes=[pltpu.VMEM((half, 128), F32), pltpu.VMEM((N_CHIPS, half, 128), F32),
                        pltpu.SemaphoreType.DMA((5,)), pltpu.SemaphoreType.DMA((5,)),
                        pltpu.SemaphoreType.DMA((3,)), pltpu.SemaphoreType.DMA((3,))],
    )(packed, shard)


def _adamw(w, g, m, v, *, g_row0, tile, name):
    rows, cols = w.shape
    assert g_row0 % tile == 0 and rows % tile == 0

    def body(w_ref, g_ref, m_ref, v_ref, go_ref, d_ref, nm_ref, nv_ref):
        g_v = g_ref[...]
        m_n = ADAM_B1 * m_ref[...] + (1.0 - ADAM_B1) * g_v
        v_n = ADAM_B2 * v_ref[...] + (1.0 - ADAM_B2) * (g_v * g_v)
        m_hat = m_n / (1.0 - ADAM_B1 ** ADAM_STEP)
        v_hat = v_n / (1.0 - ADAM_B2 ** ADAM_STEP)
        d_ref[...] = -ADAM_LR * (m_hat / (jnp.sqrt(v_hat) + ADAM_EPS) + ADAM_WD * w_ref[...])
        go_ref[...] = g_v
        nm_ref[...] = m_n
        nv_ref[...] = v_n

    spec = pl.BlockSpec((tile, cols), lambda i: (i, 0))
    gspec = pl.BlockSpec((tile, cols), lambda i: (g_row0 // tile + i, 0))
    shape = jax.ShapeDtypeStruct((rows, cols), F32)
    return pl.pallas_call(
        body, name=name, grid=(rows // tile,),
        in_specs=[spec, gspec, spec, spec], out_specs=[spec] * 4, out_shape=[shape] * 4,
        compiler_params=_cparams(("arbitrary",)),
    )(w, g, m, v)


def kernel(x, ffn1_norm_g, ffn1_w_gate, ffn1_w_up, ffn1_w_down, mix_norm_g, w_in, b_in, attn_sinks, gmlp_ln_g, gmlp_ln_b, gmlp_w_s, gmlp_b_s, attn_out_norm_g, gmlp_out_norm_g, w_out, b_out, ffn2_norm_g, ffn2_w_gate, ffn2_w_up, ffn2_w_down, final_norm_g, loss_target, m_ffn1_norm_g, m_ffn1_w_gate, m_ffn1_w_up, m_ffn1_w_down, m_mix_norm_g, m_w_in, m_b_in, m_attn_sinks, m_gmlp_ln_g, m_gmlp_ln_b, m_gmlp_w_s, m_gmlp_b_s, m_attn_out_norm_g, m_gmlp_out_norm_g, m_w_out, m_b_out, m_ffn2_norm_g, m_ffn2_w_gate, m_ffn2_w_up, m_ffn2_w_down, m_final_norm_g, v_ffn1_norm_g, v_ffn1_w_gate, v_ffn1_w_up, v_ffn1_w_down, v_mix_norm_g, v_w_in, v_b_in, v_attn_sinks, v_gmlp_ln_g, v_gmlp_ln_b, v_gmlp_w_s, v_gmlp_b_s, v_attn_out_norm_g, v_gmlp_out_norm_g, v_w_out, v_b_out, v_ffn2_norm_g, v_ffn2_w_gate, v_ffn2_w_up, v_ffn2_w_down, v_final_norm_g):
    f_args = dict(locals())
    weights = {n: f_args[n] for n in [nm for nm, _ in SMALL if nm != "loss"] + list(BIG)}
    shapes = {n: weights[n].shape for n in weights}
    shapes["loss"] = ()
    place = jnp.stack([2 * lax.axis_index("x") + lax.axis_index("y"), lax.axis_index("c")]).astype(jnp.int32)

    def with_cols(name, a):
        a2 = a.reshape(a.shape[-2], a.shape[-1])
        return a2.T if BIG_TRANSPOSED[BIG.index(name)] else a2

    def natural(name, a2):
        return (a2.T if BIG_TRANSPOSED[BIG.index(name)] else a2).reshape(shapes[name])

    pack_a, pack_b, pack_m, own_a = _pack_cast(place, [with_cols(n, weights[n]) for n in BIG], name="pack_cast")
    p = {n: weights[n].reshape(1, -1) for n in ("ffn1_norm_g", "mix_norm_g", "b_in", "gmlp_ln_g", "gmlp_ln_b",
                                                "attn_out_norm_g", "gmlp_out_norm_g", "b_out", "ffn2_norm_g",
                                                "final_norm_g")}
    p["attn_sinks"] = attn_sinks.reshape(N_Q_HEADS)
    p["gmlp_w_s"] = gmlp_w_s.reshape(GMLP_GROUPS, BLK, BLK)
    p["bs_full"] = jnp.broadcast_to(gmlp_b_s.reshape(GMLP_GROUPS, BLK).T[:, :, None],
                                    (BLK, GMLP_GROUPS, GROUP_DIM)).reshape(BLK, GMLP_W)

    loss_part, dx0, land, gs = _local_step(place, x[0], loss_target[0], p, own_a, pack_a, pack_b, pack_m)

    gs["gmlp_b_s"] = jnp.sum(gs["gmlp_b_s"].reshape(BLK, GMLP_GROUPS, GROUP_DIM), axis=-1).T
    gs["attn_sinks"] = gs["attn_sinks"][:, 0]
    gs["loss"] = loss_part[0, 0]
    small_sum, shard = _small_all_reduce(_pack_small(gs), _rs_reduce(place, land, name="rs_reduce"),
                                         name="small_all_reduce")

    grad_w, delta, new_m, new_v = {}, {}, {}, {}
    off = 0
    for n, rows in zip(BIG, BIG_ROWS):
        res = _adamw(with_cols(n, weights[n]), shard, with_cols(n, f_args["m_" + n]), with_cols(n, f_args["v_" + n]),
                     g_row0=off, tile=FF_SH // 2 if rows == FF_SH else 64, name="adamw_" + n)
        grad_w[n], delta[n], new_m[n], new_v[n] = [natural(n, a) for a in res]
        off += rows
    sm = {k: {n: f_args[k + n] for n, _ in SMALL if n != "loss"} for k in ("", "m_", "v_")}
    for k in sm:
        sm[k]["loss"] = jnp.zeros((), F32)
    res = _adamw(_pack_small(sm[""]), small_sum, _pack_small(sm["m_"]), _pack_small(sm["v_"]),
                 g_row0=0, tile=SMALL_ROWS, name="adamw_small")
    small = _unpack_small(res[0], shapes)
    for dst, packed in ((grad_w, res[0]), (delta, res[1]), (new_m, res[2]), (new_v, res[3])):
        dst.update({n: a for n, a in _unpack_small(packed, shapes).items() if n != "loss"})

    order = ('ffn1_norm_g', 'ffn1_w_gate', 'ffn1_w_up', 'ffn1_w_down', 'mix_norm_g', 'w_in', 'b_in', 'attn_sinks',
             'gmlp_ln_g', 'gmlp_ln_b', 'gmlp_w_s', 'gmlp_b_s', 'attn_out_norm_g', 'gmlp_out_norm_g', 'w_out', 'b_out',
             'ffn2_norm_g', 'ffn2_w_gate', 'ffn2_w_up', 'ffn2_w_down', 'final_norm_g')
    return (small["loss"], dx0.reshape(x.shape), *[grad_w[n] for n in order], *[delta[n] for n in order],
            *[new_m[n] for n in order], *[new_v[n] for n in order])
```

```python
import functools

import jax
import jax.numpy as jnp
from jax import lax
from jax.experimental import pallas as pl
from jax.experimental.pallas import tpu as pltpu

F32 = jnp.float32
BF16 = jnp.bfloat16

D_MODEL = 1024
D_FF = 2816
N_CHIPS = 4
FF_SH = D_FF // N_CHIPS
N_Q_HEADS = 8
N_KV_HEADS = 2
REP = N_Q_HEADS // N_KV_HEADS
HEAD_DIM = 64
ATTN_W = 512
KV_W = 128
GMLP_W = 512
GMLP_GROUPS = 8
GROUP_DIM = 64
BLK = 128
MIX_FWD_BLOCKS = 2
MIX_BWD_BLOCKS = 4
IN_W = 1792
IN_SH = IN_W // N_CHIPS
OUT_SH = D_MODEL // N_CHIPS
EPS = 1e-6
FFN_RES = 0.5
ATTN_SCALE = HEAD_DIM ** -0.5

ADAM_LR = 0.001
ADAM_B1 = 0.9
ADAM_B2 = 0.999
ADAM_EPS = 1e-08
ADAM_WD = 0.01
ADAM_STEP = 10

V7X_VMEM_LIMIT = 56 * 1024 * 1024
MESH = pl.DeviceIdType.MESH


def _cparams(sem):
    return pltpu.CompilerParams(dimension_semantics=sem, vmem_limit_bytes=V7X_VMEM_LIMIT)


def _dot(a, b):
    return jnp.dot(a, b, preferred_element_type=F32)


def _dot_nt(a, b):
    return lax.dot_general(a, b, (((1,), (1,)), ((), ())), preferred_element_type=F32)


def _dot_tn(a, b):
    return lax.dot_general(a, b, (((0,), (0,)), ((), ())), preferred_element_type=F32)


def _rms(x, g):
    r = lax.rsqrt(jnp.mean(x * x, axis=-1, keepdims=True) + EPS)
    return x * r * g, r


def _rms_bwd(dh, x, r, g):
    gy = dh * g
    dx = r * gy - x * (r * r * r) * jnp.mean(gy * x, axis=-1, keepdims=True)
    dg = jnp.sum(dh * x * r, axis=0, keepdims=True)
    return dx, dg


def _const(shape):
    nd = len(shape)
    return pl.BlockSpec(shape, lambda *_: (0,) * nd)


def _rows(t, w):
    return pl.BlockSpec((t, w), lambda i: (i, 0))


PACK_ROWS = 7 * FF_SH
HALF_ROWS = PACK_ROWS // 2
FFN_HALF = 3 * FF_SH // 2
MIX_HALF = FF_SH // 2
PACK_A_ROWS = 3 * FF_SH
PACK_B_ROWS = 3 * FF_SH
PACK_M_ROWS = FF_SH
BIG = ("ffn1_w_gate", "ffn1_w_up", "ffn1_w_down", "ffn2_w_gate", "ffn2_w_up", "ffn2_w_down", "w_in", "w_out")
BIG_ROWS = (FF_SH, FF_SH, FF_SH, FF_SH, FF_SH, FF_SH, IN_SH, OUT_SH)
BIG_TRANSPOSED = (True, True, False, True, True, False, True, False)

SMALL = (("ffn1_norm_g", 1024), ("mix_norm_g", 1024), ("b_in", 1792), ("attn_sinks", 8), ("gmlp_ln_g", 512),
         ("gmlp_ln_b", 512), ("gmlp_w_s", 131072), ("gmlp_b_s", 1024), ("attn_out_norm_g", 512),
         ("gmlp_out_norm_g", 512), ("b_out", 1024), ("ffn2_norm_g", 1024), ("final_norm_g", 1024), ("loss", 1))


def _small_rows(n):
    return -(-n // 1024) * 8


SMALL_USED_ROWS = sum(_small_rows(n) for _, n in SMALL)
SMALL_ROWS = -(-SMALL_USED_ROWS // 16) * 16


def _pack_small(parts):
    out = []
    for name, n in SMALL:
        flat = parts[name].reshape(-1).astype(F32)
        rows = _small_rows(n)
        out.append(jnp.pad(flat, (0, rows * 128 - n)).reshape(rows, 128))
    if SMALL_ROWS > SMALL_USED_ROWS:
        out.append(jnp.zeros((SMALL_ROWS - SMALL_USED_ROWS, 128), F32))
    return jnp.concatenate(out, axis=0)


def _unpack_small(packed, shapes):
    res, off = {}, 0
    for name, n in SMALL:
        rows = _small_rows(n)
        res[name] = packed[off:off + rows].reshape(-1)[:n].reshape(shapes[name])
        off += rows
    return res


def _ffn_tile(x, g, wg_ref, wu_ref, wd_ref, hb_ref, a_ref, b_ref):
    h, _ = _rms(x, g)
    hb = h.astype(BF16)
    hb_ref[...] = hb
    acc = jnp.zeros(x.shape, F32)
    for j in range(N_CHIPS):
        a = _dot_nt(hb, wg_ref[j])
        b = _dot_nt(hb, wu_ref[j])
        a_ref[j] = a
        b_ref[j] = b
        f = (a * jax.nn.sigmoid(a) * b).astype(BF16)
        acc = acc + _dot(f, wd_ref[j])
    return x + FFN_RES * acc


def _ffn_saved_specs(s, tile):
    ab = pl.BlockSpec((N_CHIPS, tile, FF_SH), lambda i: (0, i, 0))
    shape = jax.ShapeDtypeStruct((N_CHIPS, s, FF_SH), F32)
    return [_rows(tile, D_MODEL), ab, ab], [jax.ShapeDtypeStruct((s, D_MODEL), BF16), shape, shape]


def _ffn_weight_specs(k0):
    one = pl.Buffered(1)
    return [pl.BlockSpec((N_CHIPS, FF_SH, D_MODEL), functools.partial(lambda kk, i: (0, kk, 0), k0 + d),
                         pipeline_mode=one) for d in range(3)]


def _mesh_place():
    x, y, c = lax.axis_index("x"), lax.axis_index("y"), lax.axis_index("c")
    others = [(1 - x, y), (x, 1 - y), (1 - x, 1 - y)]
    return x, y, c, others


def _gather_stages(o_ref, send_sems, recv_sems):
    x, y, c, others = _mesh_place()
    me = 2 * x + y
    sibling = (x, y, 1 - c)
    half_rows = o_ref.shape[1] // 2

    def half(slab, core):
        return o_ref.at[slab, pl.ds(pl.multiple_of(core * half_rows, 16), half_rows)]

    def copy(k, rows, to):
        return pltpu.make_async_remote_copy(src_ref=rows, dst_ref=rows, send_sem=send_sems.at[k],
                                            recv_sem=recv_sems.at[k], device_id=to, device_id_type=MESH)

    first = [copy(j, half(me, c), (px, py, c)) for j, (px, py) in enumerate(others)]
    passed = [copy(3 + j, half(2 * px + py, c), sibling) for j, (px, py) in enumerate(others)]

    def landed(j):
        px, py = others[j]
        copy(j, half(2 * px + py, c), (px, py, c)).wait_recv()
        passed[j].start()

    def sibling_landed(j):
        px, py = others[j]
        copy(3 + j, half(2 * px + py, 1 - c), sibling).wait_recv()

    def start():
        for cp in first:
            cp.start()

    def forward():
        for j in range(len(others)):
            landed(j)

    def finish():
        for j in range(len(others)):
            sibling_landed(j)
        for cp in first + passed:
            cp.wait_send()

    return start, forward, finish, (first, passed, landed, sibling_landed)


def _swiglu_slab(hb, wg, wu, wd):
    a = _dot_nt(hb, wg)
    b = _dot_nt(hb, wu)
    return a, b, _dot((a * jax.nn.sigmoid(a) * b).astype(BF16), wd)


def _ffn1_own(x, g, own, gather, *, tile, name):
    s = x.shape[0]
    nt = s // tile
    y_neighbour = 1

    def body(x_ref, g_ref, wg_ref, wu_ref, wd_ref, gin_ref, hb_ref, a_ref, b_ref, p_ref, gat_ref, w2_ref, hb_all_ref,
             send_sems, recv_sems, w2_sem):
        ps, i = pl.program_id(0), pl.program_id(1)
        rows = pl.ds(pl.multiple_of(i * tile, tile), tile)
        xi, yi, _, _ = _mesh_place()
        _, _, _, (first, passed, landed, sibling_landed) = _gather_stages(gat_ref, send_sems, recv_sems)

        @pl.when(jnp.logical_and(ps == 0, i == 0))
        def _():
            first[0].start()
            first[1].start()

        @pl.when(jnp.logical_and(ps == 1, i == 0))
        def _():
            first[0].wait_send()
            first[1].wait_send()
            first[2].start()
            landed(y_neighbour)
            sibling_landed(y_neighbour)
            load = pltpu.make_async_copy(gat_ref.at[2 * xi + (1 - yi)], w2_ref, w2_sem)
            load.start()
            load.wait()

        @pl.when(ps == 0)
        def _():
            h, _ = _rms(x_ref[...], g_ref[...])
            hb = h.astype(BF16)
            hb_ref[...] = hb
            hb_all_ref[rows, :] = hb
            a_ref[0], b_ref[0], p_ref[0] = _swiglu_slab(hb, wg_ref[...], wu_ref[...], wd_ref[...])

        @pl.when(ps == 1)
        def _():
            a_ref[0], b_ref[0], p_ref[0] = _swiglu_slab(hb_all_ref[rows, :], w2_ref[0:FF_SH, :],
                                                        w2_ref[FF_SH:2 * FF_SH, :], w2_ref[2 * FF_SH:3 * FF_SH, :])

        @pl.when(jnp.logical_and(ps == 1, i == nt - 1))
        def _():
            for j in (0, 2):
                landed(j)
            for j in (0, 2):
                sibling_landed(j)
            for cp in [first[2]] + passed:
                cp.wait_send()

    one = pl.Buffered(1)
    wspecs = [pl.BlockSpec((FF_SH, D_MODEL), functools.partial(lambda kk, ps, i: (kk, 0), k), pipeline_mode=one)
              for k in range(3)]
    hbm = pl.BlockSpec(memory_space=pl.ANY)
    first_pass_tiles = pl.BlockSpec((tile, D_MODEL), lambda ps, i: (jnp.where(ps == 0, i, nt - 1), 0))
    by_pass = lambda w: pl.BlockSpec((1, tile, w), lambda ps, i: (ps, i, 0))
    return pl.pallas_call(
        body, name=name, grid=(2, nt),
        in_specs=[first_pass_tiles, pl.BlockSpec((1, D_MODEL), lambda ps, i: (0, 0))] + wspecs + [hbm],
        out_specs=[first_pass_tiles, by_pass(FF_SH), by_pass(FF_SH), by_pass(D_MODEL), hbm],
        out_shape=[jax.ShapeDtypeStruct((s, D_MODEL), BF16), jax.ShapeDtypeStruct((N_CHIPS, s, FF_SH), F32),
                   jax.ShapeDtypeStruct((N_CHIPS, s, FF_SH), F32), jax.ShapeDtypeStruct((2, s, D_MODEL), F32),
                   jax.ShapeDtypeStruct(gather.shape, gather.dtype)],
        input_output_aliases={5: 4},
        scratch_shapes=[pltpu.VMEM((PACK_A_ROWS, D_MODEL), BF16), pltpu.VMEM((s, D_MODEL), BF16),
                        pltpu.SemaphoreType.DMA((6,)), pltpu.SemaphoreType.DMA((6,)), pltpu.SemaphoreType.DMA],
        compiler_params=_cparams(("arbitrary", "arbitrary")),
    )(x, g, own, own, own, gather)


def _ffn1_others(place, x, hb, p_own, a_all, b_all, pack, gather, *, tile, name):
    s = x.shape[0]
    nt = s // tile
    forward_at = max(nt - 6, 0)

    def body(place_ref, x_ref, hb_ref, p_ref, *rest):
        w_refs = rest[:6]
        o_ref, a_ref, b_ref, gat_ref, send_sems, recv_sems = rest[9:]
        i = pl.program_id(0)
        start, forward, finish, _ = _gather_stages(gat_ref, send_sems, recv_sems)
        pl.when(i == 0)(start)
        hb = hb_ref[...]
        acc = p_ref[0] + p_ref[1]
        for t in range(2):
            a_ref[t], b_ref[t], part = _swiglu_slab(hb, w_refs[3 * t][0], w_refs[3 * t + 1][0], w_refs[3 * t + 2][0])
            acc = acc + part
        o_ref[...] = x_ref[...] + FFN_RES * acc
        pl.when(i == forward_at)(forward)
        pl.when(i == nt - 1)(finish)

    one = pl.Buffered(1)

    def wspec(t, kk):
        return pl.BlockSpec((1, FF_SH, D_MODEL), lambda i, pr: (jnp.bitwise_xor(pr[0], t + 2), kk, 0),
                            pipeline_mode=one)

    rows = lambda w: pl.BlockSpec((tile, w), lambda i, pr: (i, 0))
    ab = pl.BlockSpec((2, tile, FF_SH), lambda i, pr: (1, i, 0))
    ab_shape = jax.ShapeDtypeStruct((N_CHIPS, s, FF_SH), F32)
    hbm = pl.BlockSpec(memory_space=pl.ANY)
    grid_spec = pltpu.PrefetchScalarGridSpec(
        num_scalar_prefetch=1, grid=(nt,),
        in_specs=[rows(D_MODEL), rows(D_MODEL), pl.BlockSpec((2, tile, D_MODEL), lambda i, pr: (0, i, 0))]
                 + [wspec(t, kk) for t in range(2) for kk in range(3)] + [hbm, hbm, hbm],
        out_specs=[rows(D_MODEL), ab, ab, hbm],
        scratch_shapes=[pltpu.SemaphoreType.DMA((6,)), pltpu.SemaphoreType.DMA((6,))])
    return pl.pallas_call(
        body, name=name, grid_spec=grid_spec,
        out_shape=[jax.ShapeDtypeStruct(x.shape, F32), ab_shape, ab_shape,
                   jax.ShapeDtypeStruct(gather.shape, gather.dtype)],
        input_output_aliases={10: 1, 11: 2, 12: 3},
        compiler_params=_cparams(("arbitrary",)),
    )(place, x, hb, p_own, *([pack] * 6), a_all, b_all, gather)


def _ffn_fwd_loss(x, g, pack, k0, gf, tgt, *, tile, name):
    s = x.shape[0]

    def body(x_ref, g_ref, wg_ref, wu_ref, wd_ref, gf_ref, t_ref, dx_ref, loss_ref, dgf_ref, hb_ref, a_ref, b_ref,
             do_ref):
        @pl.when(pl.program_id(0) == 0)
        def _():
            loss_ref[...] = jnp.zeros_like(loss_ref)
            dgf_ref[...] = jnp.zeros_like(dgf_ref)

        x3 = _ffn_tile(x_ref[...], g_ref[...], wg_ref, wu_ref, wd_ref, hb_ref, a_ref, b_ref)
        gf_v = gf_ref[...]
        out, r = _rms(x3, gf_v)
        diff = out - t_ref[...]
        part = jnp.sum(jnp.sum(diff * diff, axis=-1, keepdims=True), axis=0, keepdims=True)
        loss_ref[...] += jnp.broadcast_to(part * (0.5 / D_MODEL), loss_ref.shape)
        dx, dg = _rms_bwd(diff * (1.0 / D_MODEL), x3, r, gf_v)
        dx_ref[...] = dx
        do_ref[...] = (FFN_RES * dx).astype(BF16)
        dgf_ref[...] += dg

    saved_specs, saved_shapes = _ffn_saved_specs(s, tile)
    return pl.pallas_call(
        body, name=name, grid=(s // tile,),
        in_specs=[_rows(tile, D_MODEL), _const((1, D_MODEL))] + _ffn_weight_specs(k0)
                 + [_const((1, D_MODEL)), _rows(tile, D_MODEL)],
        out_specs=[_rows(tile, D_MODEL), _const((1, 128)), _const((1, D_MODEL))] + saved_specs
                  + [_rows(tile, D_MODEL)],
        out_shape=[jax.ShapeDtypeStruct(x.shape, F32),
                   jax.ShapeDtypeStruct((1, 128), F32),
                   jax.ShapeDtypeStruct((1, D_MODEL), F32)] + saved_shapes
                  + [jax.ShapeDtypeStruct(x.shape, BF16)],
        compiler_params=_cparams(("arbitrary",)),
    )(x, g, pack, pack, pack, gf, tgt)


def _ffn_bwd(place, hb, a, b, do, pack, region, land, mix_grads, ab_by_pass=False, *, tile, name):
    s = hb.shape[0]
    nt = s // tile
    land_rows = pl.ds(region * FFN_HALF, FFN_HALF)
    mix_rows = pl.ds(2 * FFN_HALF, MIX_HALF)
    with_mix = mix_grads is not None
    with_land = land is not None
    n_others = 2 * N_CHIPS - 1

    def body(place_ref, hb_ref, a_ref, b_ref, do_ref, wg_ref, wu_ref, wd_ref, *rest):
        rest = list(rest)
        mix_ref = rest.pop(0) if with_mix else None
        if with_land:
            rest.pop(0)
        dhp_ref, land_ref, acc_ref, stage_ref, send_sems, recv_sem, local_sem = rest[:7]
        t, i = pl.program_id(0), pl.program_id(1)
        xi, yi, c = lax.axis_index("x"), lax.axis_index("y"), lax.axis_index("c")
        dev = 4 * xi + 2 * yi + c
        tt = (t + 1) % N_CHIPS
        tx, ty = jnp.bitwise_xor(xi, tt // 2), jnp.bitwise_xor(yi, tt % 2)

        def remote(src, dst, ssem, rsem, to):
            return pltpu.make_async_remote_copy(src_ref=src, dst_ref=dst, send_sem=ssem, recv_sem=rsem,
                                                device_id=to, device_id_type=MESH)

        def stage_half(h):
            return stage_ref.at[pl.ds(pl.multiple_of(h * FFN_HALF, 16), FFN_HALF)]

        if with_mix:
            mix_send, mix_recv, mix_local = rest[7:10]

            @pl.when(jnp.logical_and(t == 0, i == 0))
            def _():
                for chip in range(N_CHIPS):
                    for h in range(2):
                        src = mix_ref.at[chip, pl.ds(h * MIX_HALF, MIX_HALF)]
                        dst = land_ref.at[dev, mix_rows]
                        mine = jnp.logical_and(2 * xi + yi == chip, c == h)

                        @pl.when(mine)
                        def _():
                            pltpu.make_async_copy(src, dst, mix_local).start()

                        @pl.when(jnp.logical_not(mine))
                        def _():
                            remote(src, dst, mix_send, mix_recv, (chip // 2, chip % 2, h)).start()

        @pl.when(i == 0)
        def _():
            acc_ref[...] = jnp.zeros_like(acc_ref)

        hb = hb_ref[...]
        dob = do_ref[...]
        wg_j, wu_j, wd_j = wg_ref[0], wu_ref[0], wd_ref[0]
        a = a_ref[0]
        b = b_ref[0]
        sg = jax.nn.sigmoid(a)
        sa = a * sg
        fb = (sa * b).astype(BF16)
        df = _dot_nt(dob, wd_j)
        dbb = (df * sa).astype(BF16)
        dab = (df * b * (sg + sa * (1.0 - sg))).astype(BF16)
        dhp_ref[0] = (_dot(dab, wg_j) + _dot(dbb, wu_j)).astype(BF16)
        acc_ref[0:FF_SH, :] += _dot_tn(dab, hb)
        acc_ref[FF_SH:2 * FF_SH, :] += _dot_tn(dbb, hb)
        acc_ref[2 * FF_SH:3 * FF_SH, :] += _dot_tn(fb, dob)

        @pl.when(i == nt - 1)
        def _():
            dst = land_ref.at[dev, land_rows]

            @pl.when(t > 0)
            def _():
                for h in range(2):
                    remote(stage_half(h), dst, send_sems.at[h], recv_sem, (tx, ty, h)).wait_send()

            def cast_rows(r, carry):
                rows = pl.ds(pl.multiple_of(r * MIX_HALF, 16), MIX_HALF)
                stage_ref[rows, :] = acc_ref[rows, :].astype(BF16)
                return carry

            lax.fori_loop(0, 3 * FF_SH // MIX_HALF, cast_rows, 0)

            @pl.when(t < N_CHIPS - 1)
            def _():
                for h in range(2):
                    remote(stage_half(h), dst, send_sems.at[h], recv_sem, (tx, ty, h)).start()

            @pl.when(t == N_CHIPS - 1)
            def _():
                own = pltpu.make_async_copy(stage_half(c), dst, local_sem)
                own.start()
                sib = remote(stage_half(1 - c), dst, send_sems.at[0], recv_sem, (xi, yi, 1 - c))
                sib.start()
                sib.wait_send()
                own.wait()
                arrivals = land_ref.at[pl.ds(0, n_others), land_rows]
                remote(arrivals, arrivals, send_sems.at[0], recv_sem, (xi, yi, 1 - c)).wait_recv()
                if with_mix:
                    seven = land_ref.at[pl.ds(0, n_others), mix_rows]
                    both = remote(seven, seven, mix_send, mix_recv, (xi, yi, 1 - c))
                    both.wait_send()
                    both.wait_recv()
                    pltpu.make_async_copy(mix_ref.at[0, pl.ds(0, MIX_HALF)], land_ref.at[dev, mix_rows],
                                          mix_local).wait()

    def wspec(kk):
        return pl.BlockSpec((1, FF_SH, D_MODEL),
                            lambda t, i, pr: (jnp.bitwise_xor(pr[0], (t + 1) % N_CHIPS), kk, 0))

    xspec = pl.BlockSpec((tile, D_MODEL), lambda t, i, pr: (i, 0))
    if ab_by_pass:
        abspec = pl.BlockSpec((1, tile, FF_SH), lambda t, i, pr: ((t + 1) % N_CHIPS, i, 0))
    else:
        abspec = pl.BlockSpec((1, tile, FF_SH), lambda t, i, pr: (jnp.bitwise_xor(pr[0], (t + 1) % N_CHIPS), i, 0))
    hbm = pl.BlockSpec(memory_space=pl.ANY)
    in_specs = [xspec, abspec, abspec, xspec, wspec(0), wspec(1), wspec(2)]
    operands = [place, hb, a, b, do, pack, pack, pack]
    scratch = [pltpu.VMEM((3 * FF_SH, D_MODEL), F32), pltpu.VMEM((3 * FF_SH, D_MODEL), BF16),
               pltpu.SemaphoreType.DMA((2,)), pltpu.SemaphoreType.DMA, pltpu.SemaphoreType.DMA]
    if with_mix:
        in_specs.append(hbm)
        operands.append(mix_grads)
        scratch += [pltpu.SemaphoreType.DMA, pltpu.SemaphoreType.DMA, pltpu.SemaphoreType.DMA]
    aliases = {}
    if with_land:
        in_specs.append(hbm)
        operands.append(land)
        aliases = {len(operands) - 1: 1}
    grid_spec = pltpu.PrefetchScalarGridSpec(
        num_scalar_prefetch=1, grid=(N_CHIPS, nt), in_specs=in_specs,
        out_specs=[pl.BlockSpec((1, tile, D_MODEL), lambda t, i, pr: (t, i, 0)), hbm],
        scratch_shapes=scratch)
    return pl.pallas_call(
        body, name=name, grid_spec=grid_spec,
        out_shape=[jax.ShapeDtypeStruct((N_CHIPS, s, D_MODEL), BF16),
                   jax.ShapeDtypeStruct((2 * N_CHIPS, HALF_ROWS, D_MODEL), BF16)],
        input_output_aliases=aliases,
        compiler_params=_cparams(("arbitrary", "arbitrary")),
    )(*operands)


def _mix_grads_pack(dw_in_t, dw_out, *, name):
    def body(a_ref, b_ref, o_ref):
        o_ref[0, 0:IN_SH, :] = a_ref[0].astype(BF16)
        o_ref[0, IN_SH:FF_SH, :] = b_ref[0].astype(BF16)

    return pl.pallas_call(
        body, name=name, grid=(N_CHIPS,),
        in_specs=[pl.BlockSpec((1, IN_SH, D_MODEL), lambda j: (j, 0, 0)),
                  pl.BlockSpec((1, OUT_SH, D_MODEL), lambda j: (j, 0, 0))],
        out_specs=pl.BlockSpec((1, FF_SH, D_MODEL), lambda j: (j, 0, 0)),
        out_shape=jax.ShapeDtypeStruct((N_CHIPS, FF_SH, D_MODEL), BF16),
        compiler_params=_cparams(("arbitrary",)),
    )(dw_in_t.reshape(N_CHIPS, IN_SH, D_MODEL), dw_out.reshape(N_CHIPS, OUT_SH, D_MODEL))


def _share_stages(o_ref, send_sems, recv_sems):
    x, y, c, _ = _mesh_place()

    def rows(k, core):
        if k < 2:
            return o_ref.at[pl.ds(pl.multiple_of(k * 2 * FFN_HALF + core * FFN_HALF, 8), FFN_HALF)]
        return o_ref.at[pl.ds(pl.multiple_of(4 * FFN_HALF + core * MIX_HALF, 8), MIX_HALF)]

    def copy(k, core):
        return pltpu.make_async_remote_copy(src_ref=rows(k, core), dst_ref=rows(k, core), send_sem=send_sems.at[k],
                                            recv_sem=recv_sems.at[k], device_id=(x, y, 1 - c), device_id_type=MESH)

    sends = [copy(k, c) for k in range(3)]

    def start():
        for cp in sends:
            cp.start()

    def finish():
        for k in range(3):
            copy(k, 1 - c).wait_recv()
        for cp in sends:
            cp.wait_send()

    return start, finish


def _norm_bwd(dhp, x, dy, g, *, tile, name):
    s = x.shape[0]

    def body(dhp_ref, x_ref, dy_ref, g_ref, dx_ref, dg_ref):
        @pl.when(pl.program_id(0) == 0)
        def _():
            dg_ref[...] = jnp.zeros_like(dg_ref)

        dh = ((dhp_ref[0].astype(F32) + dhp_ref[1].astype(F32))
              + (dhp_ref[2].astype(F32) + dhp_ref[3].astype(F32)))
        x_v = x_ref[...]
        r = lax.rsqrt(jnp.mean(x_v * x_v, axis=-1, keepdims=True) + EPS)
        dx, dg = _rms_bwd(dh, x_v, r, g_ref[...])
        dx_ref[...] = dy_ref[...] + dx
        dg_ref[...] += dg

    return pl.pallas_call(
        body, name=name, grid=(s // tile,),
        in_specs=[pl.BlockSpec((N_CHIPS, tile, D_MODEL), lambda i: (0, i, 0)),
                  _rows(tile, D_MODEL), _rows(tile, D_MODEL), _const((1, D_MODEL))],
        out_specs=[_rows(tile, D_MODEL), _const((1, D_MODEL))],
        out_shape=[jax.ShapeDtypeStruct(x.shape, F32), jax.ShapeDtypeStruct((1, D_MODEL), F32)],
        compiler_params=_cparams(("arbitrary",)),
    )(dhp, x, dy, g)


def _mix_in_bwd(x, dy, dq, dk, dv, dz, g, w_in_t, *, tile, name):
    s = x.shape[0]

    def body(x_ref, dy_ref, dq_ref, dk_ref, dv_ref, dz_ref, g_ref, w_ref, dx_ref, dw_ref, db_ref, dg_ref, do_ref):
        @pl.when(pl.program_id(0) == 0)
        def _():
            dw_ref[...] = jnp.zeros_like(dw_ref)
            db_ref[...] = jnp.zeros_like(db_ref)
            dg_ref[...] = jnp.zeros_like(dg_ref)

        dproj = jnp.concatenate([dq_ref[...], dk_ref[...], dv_ref[...], dz_ref[...]], axis=-1)
        db_ref[...] += jnp.sum(dproj, axis=0, keepdims=True)
        dpb = dproj.astype(BF16)
        x_v = x_ref[...]
        g_v = g_ref[...]
        h, r = _rms(x_v, g_v)
        dw_ref[...] += _dot_tn(dpb, h.astype(BF16))
        dh = _dot(dpb, w_ref[...])
        dxn, dg = _rms_bwd(dh, x_v, r, g_v)
        dx = dy_ref[...] + dxn
        dx_ref[...] = dx
        do_ref[...] = (FFN_RES * dx).astype(BF16)
        dg_ref[...] += dg

    return pl.pallas_call(
        body, name=name, grid=(s // tile,),
        in_specs=[_rows(tile, D_MODEL), _rows(tile, D_MODEL), _rows(tile, ATTN_W), _rows(tile, KV_W),
                  _rows(tile, KV_W), _rows(tile, 2 * GMLP_W), _const((1, D_MODEL)), _const((IN_W, D_MODEL))],
        out_specs=[_rows(tile, D_MODEL), _const((IN_W, D_MODEL)), _const((1, IN_W)), _const((1, D_MODEL)),
                   _rows(tile, D_MODEL)],
        out_shape=[jax.ShapeDtypeStruct(x.shape, F32), jax.ShapeDtypeStruct((IN_W, D_MODEL), F32),
                   jax.ShapeDtypeStruct((1, IN_W), F32), jax.ShapeDtypeStruct((1, D_MODEL), F32),
                   jax.ShapeDtypeStruct(x.shape, BF16)],
        compiler_params=_cparams(("arbitrary",)),
    )(x, dy, dq, dk, dv, dz, g, w_in_t)


_GELU_C = 0.7978845608028654
_GELU_A = 0.044715


def _gelu_tanh(x):
    x2 = x * x
    return jnp.tanh(_GELU_C * (x + _GELU_A * (x2 * x))), x2


def _band(ref, i):
    prev = jnp.maximum(i - 1, 0)
    return jnp.concatenate([ref[pl.ds(pl.multiple_of(prev * BLK, BLK), BLK), :],
                            ref[pl.ds(pl.multiple_of(i * BLK, BLK), BLK), :]], axis=0)


def _key_in_block():
    return lax.broadcasted_iota(jnp.int32, (BLK, BLK), 0) <= lax.broadcasted_iota(jnp.int32, (BLK, BLK), 1)


def _fold(band, own):
    return jnp.where(own, band[BLK:], band[:BLK])


def _unfold(a, own):
    zero = jnp.zeros_like(a)
    return jnp.concatenate([jnp.where(own, zero, a), jnp.where(own, a, zero)], axis=0).astype(BF16)


def _attn_fwd(q, kb, vb, i, sink_ref):
    own = _key_in_block()
    outs, saved = [], []
    for h in range(N_Q_HEADS):
        cols = slice((h // REP) * HEAD_DIM, (h // REP + 1) * HEAD_DIM)
        s2 = _dot_nt(kb[:, cols], q[:, h * HEAD_DIM:(h + 1) * HEAD_DIM])
        sc = jnp.where(own, s2[BLK:], jnp.where(i > 0, s2[:BLK], -jnp.inf))
        sink = sink_ref[h]
        m = jnp.maximum(jnp.max(sc, axis=0, keepdims=True), sink)
        p = jnp.exp(sc - m)
        es = jnp.exp(sink - m)
        inv = 1.0 / (jnp.sum(p, axis=0, keepdims=True) + es)
        pn = p * inv
        band = _unfold(pn, own)
        outs.append(_dot_tn(band, vb[:, cols]))
        saved.append((pn, band, es * inv))
    return jnp.concatenate(outs, axis=-1), saved


def _tril_mask():
    t = lax.broadcasted_iota(jnp.int32, (BLK, BLK), 0)
    s_ = lax.broadcasted_iota(jnp.int32, (BLK, BLK), 1)
    return s_ <= t


def _gmlp_fwd_parts(zg, lng, lnb, ws_ref, bs_full):
    th, zg2 = _gelu_tanh(zg)
    z = 0.5 * zg * (1.0 + th)
    u = z[:, :GMLP_W]
    zv = z[:, GMLP_W:]
    mu = jnp.mean(zv, axis=-1, keepdims=True)
    zc = zv - mu
    rstd = lax.rsqrt(jnp.mean(zc * zc, axis=-1, keepdims=True) + EPS)
    xh = zc * rstd
    vvb = (xh * lng + lnb).astype(BF16)
    tril = _tril_mask()
    wms, parts = [], []
    for gi in range(GMLP_GROUPS):
        wm = jnp.where(tril, ws_ref[gi], 0.0).astype(BF16)
        wms.append(wm)
        parts.append(_dot(wm, vvb[:, gi * GROUP_DIM:(gi + 1) * GROUP_DIM]))
    mixed = jnp.concatenate(parts, axis=-1) + bs_full
    gelu_grad = 0.5 * (1.0 + th) + 0.5 * zg * (1.0 - th * th) * (_GELU_C * (1.0 + 3.0 * _GELU_A * zg2))
    return u, xh, rstd, vvb, wms, mixed, gelu_grad


def _mixer_fwd(x1, g, w_in_t, b_in, sinks, lng, lnb, w_s, bs_full, gao, ggo, w_out, b_out, gather, *, name):
    s = x1.shape[0]
    nb = min(MIX_FWD_BLOCKS, s // BLK)
    step_rows = nb * BLK
    last = s // step_rows - 1

    def tile_of(i, lag):
        return jnp.clip(i - lag, 0, last)

    def body(sink_ref, xa_ref, xc_ref, g_ref, wi_ref, bi_ref, lng_ref, lnb_ref, ws_ref, bs_ref, gao_ref, ggo_ref,
             wo_ref, bo_ref, gin_ref, q_ref, k_ref, v_ref, z_ref, y_ref, o_ref, gat_ref, qs_ref, zs_ref, ys_ref,
             send_sems, recv_sems):
        i = pl.program_id(0)
        start, forward, finish, _ = _gather_stages(gat_ref, send_sems, recv_sems)

        @pl.when(i == 0)
        def _():
            for ref in (k_ref, v_ref, qs_ref, zs_ref, ys_ref):
                ref[...] = jnp.zeros_like(ref)
            start()

        slot_a, slot_b, slot_c = i % 2, (i + 1) % 2, i % 2

        o_ref[...] = xc_ref[...] + (_dot(ys_ref[slot_c], wo_ref[...]) + bo_ref[...])

        tile_b = tile_of(i, 1)
        for b in range(nb):
            blk = tile_b * nb + b
            rows = slice(b * BLK, (b + 1) * BLK)
            y_attn, _ = _attn_fwd(qs_ref[slot_b, rows, :], _band(k_ref, blk), _band(v_ref, blk), blk, sink_ref)
            u, _, _, _, _, mixed, _ = _gmlp_fwd_parts(zs_ref[slot_b, rows, :], lng_ref[...], lnb_ref[...], ws_ref,
                                                      bs_ref[...])
            ya, _ = _rms(y_attn, gao_ref[...])
            yg, _ = _rms(u * mixed, ggo_ref[...])
            y_blk = jnp.concatenate([ya, yg], axis=-1).astype(BF16)
            y_ref[rows, :] = y_blk
            ys_ref[slot_b, rows, :] = y_blk

        h, _ = _rms(xa_ref[...], g_ref[...])
        proj = _dot_nt(h.astype(BF16), wi_ref[...]) + bi_ref[...]
        q_t = (proj[:, :ATTN_W] * ATTN_SCALE).astype(BF16)
        z_t = proj[:, ATTN_W + 2 * KV_W:]
        here = pl.ds(pl.multiple_of(tile_of(i, 0) * step_rows, step_rows), step_rows)
        q_ref[...] = q_t
        z_ref[...] = z_t
        qs_ref[slot_a] = q_t
        zs_ref[slot_a] = z_t
        k_ref[here, :] = proj[:, ATTN_W:ATTN_W + KV_W].astype(BF16)
        v_ref[here, :] = proj[:, ATTN_W + KV_W:ATTN_W + 2 * KV_W].astype(BF16)
        pl.when(i == max(last - 3, 0))(forward)
        pl.when(i == last + 2)(finish)

    def lagged(width, lag):
        return pl.BlockSpec((step_rows, width), lambda i: (tile_of(i, lag), 0))

    return pl.pallas_call(
        body, name=name, grid=(last + 3,),
        in_specs=[pl.BlockSpec(memory_space=pltpu.SMEM),
                  lagged(D_MODEL, 0), lagged(D_MODEL, 2), _const((1, D_MODEL)), _const((IN_W, D_MODEL)),
                  _const((1, IN_W)), _const((1, GMLP_W)), _const((1, GMLP_W)), _const((GMLP_GROUPS, BLK, BLK)),
                  _const((BLK, GMLP_W)), _const((1, ATTN_W)), _const((1, GMLP_W)), _const((D_MODEL, D_MODEL)),
                  _const((1, D_MODEL)), pl.BlockSpec(memory_space=pl.ANY)],
        out_specs=[lagged(ATTN_W, 0), _const((s, KV_W)), _const((s, KV_W)), lagged(2 * GMLP_W, 0),
                   lagged(D_MODEL, 1), lagged(D_MODEL, 2), pl.BlockSpec(memory_space=pl.ANY)],
        out_shape=[jax.ShapeDtypeStruct((s, ATTN_W), BF16), jax.ShapeDtypeStruct((s, KV_W), BF16),
                   jax.ShapeDtypeStruct((s, KV_W), BF16), jax.ShapeDtypeStruct((s, 2 * GMLP_W), F32),
                   jax.ShapeDtypeStruct((s, D_MODEL), BF16), jax.ShapeDtypeStruct((s, D_MODEL), F32),
                   jax.ShapeDtypeStruct(gather.shape, gather.dtype)],
        input_output_aliases={14: 6},
        scratch_shapes=[pltpu.VMEM((2, step_rows, ATTN_W), BF16), pltpu.VMEM((2, step_rows, 2 * GMLP_W), F32),
                        pltpu.VMEM((2, step_rows, D_MODEL), BF16),
                        pltpu.SemaphoreType.DMA((6,)), pltpu.SemaphoreType.DMA((6,))],
        compiler_params=_cparams(("arbitrary",)),
    )(sinks, x1, x1, g, w_in_t, b_in, lng, lnb, w_s, bs_full, gao, ggo, w_out, b_out, gather)


def _norm_bwd_mix_out(dhp, x, dy, g, yb, w_out, *, tile, name):
    s = x.shape[0]

    def body(dhp_ref, x_ref, dy_ref, g_ref, y_ref, w_ref, dx_ref, dg_ref, dyy_ref, dw_ref, db_ref):
        @pl.when(pl.program_id(0) == 0)
        def _():
            dg_ref[...] = jnp.zeros_like(dg_ref)
            dw_ref[...] = jnp.zeros_like(dw_ref)
            db_ref[...] = jnp.zeros_like(db_ref)

        dh = ((dhp_ref[0].astype(F32) + dhp_ref[1].astype(F32))
              + (dhp_ref[2].astype(F32) + dhp_ref[3].astype(F32)))
        x_v = x_ref[...]
        r = lax.rsqrt(jnp.mean(x_v * x_v, axis=-1, keepdims=True) + EPS)
        dxn, dg = _rms_bwd(dh, x_v, r, g_ref[...])
        dx = dy_ref[...] + dxn
        dx_ref[...] = dx
        dg_ref[...] += dg
        dxb = dx.astype(BF16)
        db_ref[...] += jnp.sum(dx, axis=0, keepdims=True)
        dw_ref[...] += _dot_tn(y_ref[...], dxb)
        dyy_ref[...] = _dot_nt(dxb, w_ref[...])

    return pl.pallas_call(
        body, name=name, grid=(s // tile,),
        in_specs=[pl.BlockSpec((N_CHIPS, tile, D_MODEL), lambda i: (0, i, 0)),
                  _rows(tile, D_MODEL), _rows(tile, D_MODEL), _const((1, D_MODEL)), _rows(tile, D_MODEL),
                  _const((D_MODEL, D_MODEL))],
        out_specs=[_rows(tile, D_MODEL), _const((1, D_MODEL)), _rows(tile, D_MODEL), _const((D_MODEL, D_MODEL)),
                   _const((1, D_MODEL))],
        out_shape=[jax.ShapeDtypeStruct(x.shape, F32), jax.ShapeDtypeStruct((1, D_MODEL), F32),
                   jax.ShapeDtypeStruct(x.shape, F32), jax.ShapeDtypeStruct((D_MODEL, D_MODEL), F32),
                   jax.ShapeDtypeStruct((1, D_MODEL), F32)],
        compiler_params=_cparams(("arbitrary",)),
    )(dhp, x, dy, g, yb, w_out)


def _mix_core_bwd(dyy, q, k, v, zg, sinks, lng, lnb, w_s, bs_full, gao, ggo, *, name):
    s = dyy.shape[0]
    nb = min(MIX_BWD_BLOCKS, s // BLK)
    nsteps = s // (nb * BLK)

    def body(*refs):
        accumulators = refs[13:15] + refs[16:]

        @pl.when(pl.program_id(0) == 0)
        def _():
            for ref in accumulators:
                ref[...] = jnp.zeros_like(ref)

        for b in range(nb):
            one_block(pl.program_id(0) * nb + b, slice(b * BLK, (b + 1) * BLK), *refs)

        @pl.when(pl.program_id(0) == nsteps - 1)
        def _():
            tril = _tril_mask()
            for gi in range(GMLP_GROUPS):
                refs[20][gi] = jnp.where(tril, refs[20][gi], 0.0)

    def one_block(i, rows, sink_ref, dyy_ref, q_ref, k_ref, v_ref, z_ref, lng_ref, lnb_ref, ws_ref, bs_ref, gao_ref,
                  ggo_ref, dq_ref, dk_ref, dv_ref, dz_ref, dgao_ref, dggo_ref, dlng_ref, dlnb_ref, dws_ref, dms_ref,
                  dsk_ref):
        q_v = q_ref[rows, :]
        kb = _band(k_ref, i)
        vb = _band(v_ref, i)
        lng_v = lng_ref[...]
        gao_v = gao_ref[...]
        ggo_v = ggo_ref[...]

        y_attn, probs = _attn_fwd(q_v, kb, vb, i, sink_ref)
        u, xh, rstd, vvb, wms, mixed, gelu_grad = _gmlp_fwd_parts(z_ref[rows, :], lng_v, lnb_ref[...], ws_ref,
                                                                  bs_ref[...])
        y_gmlp = u * mixed
        ra = lax.rsqrt(jnp.mean(y_attn * y_attn, axis=-1, keepdims=True) + EPS)
        rg = lax.rsqrt(jnp.mean(y_gmlp * y_gmlp, axis=-1, keepdims=True) + EPS)

        dyy = dyy_ref[rows, :]
        d_attn, dgao = _rms_bwd(dyy[:, :ATTN_W], y_attn, ra, gao_v)
        d_gmlp, dggo = _rms_bwd(dyy[:, ATTN_W:], y_gmlp, rg, ggo_v)
        dgao_ref[...] += dgao
        dggo_ref[...] += dggo

        du = d_gmlp * mixed
        dmixed = d_gmlp * u
        dms_ref[...] += dmixed
        dmb = dmixed.astype(BF16)
        dvv_parts = []
        for gi in range(GMLP_GROUPS):
            sl = slice(gi * GROUP_DIM, (gi + 1) * GROUP_DIM)
            dws_ref[gi] += _dot_nt(dmb[:, sl], vvb[:, sl])
            dvv_parts.append(_dot_tn(wms[gi], dmb[:, sl]))
        dvv = jnp.concatenate(dvv_parts, axis=-1)
        dlng_ref[...] += jnp.sum(dvv * xh, axis=0, keepdims=True)
        dlnb_ref[...] += jnp.sum(dvv, axis=0, keepdims=True)
        dxh = dvv * lng_v
        dzv = rstd * (dxh - jnp.mean(dxh, axis=-1, keepdims=True)
                      - xh * jnp.mean(dxh * xh, axis=-1, keepdims=True))
        dz_ref[rows, :] = jnp.concatenate([du, dzv], axis=-1) * gelu_grad

        dab = d_attn.astype(BF16)
        own = _key_in_block()
        dq_parts = []
        dk_parts = []
        dv_parts = []
        for gi in range(N_KV_HEADS):
            cols = slice(gi * HEAD_DIM, (gi + 1) * HEAD_DIM)
            kg, vg = kb[:, cols], vb[:, cols]
            dkg = jnp.zeros((2 * BLK, HEAD_DIM), F32)
            dvg = jnp.zeros((2 * BLK, HEAD_DIM), F32)
            for rr in range(REP):
                h = gi * REP + rr
                hs = slice(h * HEAD_DIM, (h + 1) * HEAD_DIM)
                qh, doh = q_v[:, hs], dab[:, hs]
                pn, band, psink = probs[h]
                dp = _fold(_dot_nt(vg, doh), own)
                delta = jnp.sum(pn * dp, axis=0, keepdims=True)
                ds2 = _unfold(pn * (dp - delta), own)
                dsink = jnp.sum(-psink * delta, axis=-1, keepdims=True)
                dsk_ref[pl.ds(h, 1), :] += jnp.broadcast_to(dsink, (1, 128))
                dq_parts.append(_dot_tn(ds2, kg) * ATTN_SCALE)
                dkg = dkg + _dot(ds2, qh)
                dvg = dvg + _dot(band, doh)
            dk_parts.append(dkg)
            dv_parts.append(dvg)
        dq_ref[rows, :] = jnp.concatenate(dq_parts, axis=-1)
        dkb = jnp.concatenate(dk_parts, axis=-1)
        dvb = jnp.concatenate(dv_parts, axis=-1)
        prev = pl.ds(pl.multiple_of(jnp.maximum(i - 1, 0) * BLK, BLK), BLK)
        cur = pl.ds(pl.multiple_of(i * BLK, BLK), BLK)
        dk_ref[prev, :] += dkb[:BLK]
        dv_ref[prev, :] += dvb[:BLK]
        dk_ref[cur, :] += dkb[BLK:]
        dv_ref[cur, :] += dvb[BLK:]

    return pl.pallas_call(
        body, name=name, grid=(nsteps,),
        in_specs=[pl.BlockSpec(memory_space=pltpu.SMEM),
                  _rows(nb * BLK, D_MODEL), _rows(nb * BLK, ATTN_W), _const((s, KV_W)), _const((s, KV_W)),
                  _rows(nb * BLK, 2 * GMLP_W), _const((1, GMLP_W)), _const((1, GMLP_W)),
                  _const((GMLP_GROUPS, BLK, BLK)), _const((BLK, GMLP_W)), _const((1, ATTN_W)), _const((1, GMLP_W))],
        out_specs=[_rows(nb * BLK, ATTN_W), _const((s, KV_W)), _const((s, KV_W)), _rows(nb * BLK, 2 * GMLP_W),
                   _const((1, ATTN_W)), _const((1, GMLP_W)),
                   _const((1, GMLP_W)), _const((1, GMLP_W)), _const((GMLP_GROUPS, BLK, BLK)),
                   _const((BLK, GMLP_W)), _const((N_Q_HEADS, 128))],
        out_shape=[jax.ShapeDtypeStruct((s, ATTN_W), F32), jax.ShapeDtypeStruct((s, KV_W), F32),
                   jax.ShapeDtypeStruct((s, KV_W), F32), jax.ShapeDtypeStruct((s, 2 * GMLP_W), F32),
                   jax.ShapeDtypeStruct((1, ATTN_W), F32), jax.ShapeDtypeStruct((1, GMLP_W), F32),
                   jax.ShapeDtypeStruct((1, GMLP_W), F32), jax.ShapeDtypeStruct((1, GMLP_W), F32),
                   jax.ShapeDtypeStruct((GMLP_GROUPS, BLK, BLK), F32), jax.ShapeDtypeStruct((BLK, GMLP_W), F32),
                   jax.ShapeDtypeStruct((N_Q_HEADS, 128), F32)],
        compiler_params=_cparams(("arbitrary",)),
    )(sinks, dyy, q, k, v, zg, lng, lnb, w_s, bs_full, gao, ggo)


def _local_step(place, x, tgt, p, own_a, pack_a, pack_b, pack_m, *, tile=512, fwd_tile=256, bwd_tile=512,
                norm_tile=512):
    g = {}
    tile, fwd_tile, bwd_tile, norm_tile = (min(t_, x.shape[0]) for t_ in (tile, fwd_tile, bwd_tile, norm_tile))
    hb1, a1, b1, part1, pack_a = _ffn1_own(x, p["ffn1_norm_g"], own_a, pack_a, tile=tile, name="ffn1_own")
    x1, a1, b1, pack_m = _ffn1_others(place, x, hb1, part1, a1, b1, pack_a, pack_m, tile=fwd_tile, name="ffn1_fwd")
    w_in_t = pack_m[:, :IN_SH, :].reshape(IN_W, D_MODEL)
    w_out = pack_m[:, IN_SH:, :].reshape(D_MODEL, D_MODEL)
    q, k, v, zg, yb, x2, pack_b = _mixer_fwd(
        x1, p["mix_norm_g"], w_in_t, p["b_in"], p["attn_sinks"], p["gmlp_ln_g"], p["gmlp_ln_b"], p["gmlp_w_s"],
        p["bs_full"], p["attn_out_norm_g"], p["gmlp_out_norm_g"], w_out, p["b_out"], pack_b, name="mixer_fwd")
    mix_args = (q, k, v, zg, p["attn_sinks"], p["gmlp_ln_g"], p["gmlp_ln_b"], p["gmlp_w_s"], p["bs_full"],
                p["attn_out_norm_g"], p["gmlp_out_norm_g"])
    dx3, loss, g["final_norm_g"], hb2, a2, b2, do3 = _ffn_fwd_loss(
        x2, p["ffn2_norm_g"], pack_b, 0, p["final_norm_g"], tgt, tile=fwd_tile, name="ffn2_fwd_loss")

    dhp, land = _ffn_bwd(place, hb2, a2, b2, do3, pack_b, 1, None, None, tile=bwd_tile, name="ffn2_bwd")
    dx2, g["ffn2_norm_g"], dyy, dw_out, g["b_out"] = _norm_bwd_mix_out(
        dhp, x2, dx3, p["ffn2_norm_g"], yb, w_out, tile=norm_tile, name="ffn2_norm_bwd")

    (dq, dk, dv, dz, g["attn_out_norm_g"], g["gmlp_out_norm_g"], g["gmlp_ln_g"],
     g["gmlp_ln_b"], g["gmlp_w_s"], dmix_sum, dsinks) = _mix_core_bwd(dyy, *mix_args, name="mix_core_bwd")
    g["gmlp_b_s"] = dmix_sum
    g["attn_sinks"] = dsinks
    dx1, dw_in_t, g["b_in"], g["mix_norm_g"], do1 = _mix_in_bwd(
        x1, dx2, dq, dk, dv, dz, p["mix_norm_g"], w_in_t, tile=tile, name="mix_in_bwd")
    mix_grads = _mix_grads_pack(dw_in_t, dw_out, name="mix_grads_pack")

    dhp1, land = _ffn_bwd(place, hb1, a1, b1, do1, pack_a, 0, land, mix_grads, True, tile=bwd_tile, name="ffn1_bwd")
    dx0, g["ffn1_norm_g"] = _norm_bwd(dhp1, x, dx1, p["ffn1_norm_g"], tile=norm_tile, name="ffn1_norm_bwd")
    return loss, dx0, land, g


def _pack_cast(place, parts, *, name):
    def body(place_ref, *refs):
        oa_ref, ob_ref, om_ref, own_ref = refs[-4:]
        off = 0
        for k, (ref, rows) in enumerate(zip(refs[:-4], BIG_ROWS)):
            if k in (3, 6):
                off = 0
            cast = ref[...].astype(BF16)
            (oa_ref if k < 3 else ob_ref if k < 6 else om_ref)[0, off:off + rows, :] = cast
            if k < 3:
                own_ref[off:off + rows, :] = cast
            off += rows

    one = pl.Buffered(1)

    def slab(rows):
        return pl.BlockSpec((1, rows, D_MODEL), lambda i, pr: (pr[0], 0, 0), pipeline_mode=one)

    grid_spec = pltpu.PrefetchScalarGridSpec(
        num_scalar_prefetch=1, grid=(1,),
        in_specs=[pl.BlockSpec((rows, D_MODEL), lambda i, pr: (0, 0), pipeline_mode=one) for rows in BIG_ROWS],
        out_specs=[slab(PACK_A_ROWS), slab(PACK_B_ROWS), slab(PACK_M_ROWS),
                   pl.BlockSpec((PACK_A_ROWS, D_MODEL), lambda i, pr: (0, 0), pipeline_mode=one)])
    return pl.pallas_call(
        body, name=name, grid_spec=grid_spec,
        out_shape=[jax.ShapeDtypeStruct((N_CHIPS, PACK_A_ROWS, D_MODEL), BF16),
                   jax.ShapeDtypeStruct((N_CHIPS, PACK_B_ROWS, D_MODEL), BF16),
                   jax.ShapeDtypeStruct((N_CHIPS, PACK_M_ROWS, D_MODEL), BF16),
                   jax.ShapeDtypeStruct((PACK_A_ROWS, D_MODEL), BF16)],
        compiler_params=_cparams(("arbitrary",)),
    )(place, *parts)


def _shard_tile(i, c):
    return jnp.where(i < 3, 3 * c + i, jnp.where(i < 6, 3 + 3 * c + i, 12 + c))


def _rs_reduce(place, land, *, name):
    def body(place_ref, l_ref, o_ref):
        acc = l_ref[0].astype(F32)
        for d in range(1, 2 * N_CHIPS):
            acc = acc + l_ref[d].astype(F32)
        o_ref[...] = acc

    grid_spec = pltpu.PrefetchScalarGridSpec(
        num_scalar_prefetch=1, grid=(HALF_ROWS // MIX_HALF,),
        in_specs=[pl.BlockSpec((2 * N_CHIPS, MIX_HALF, D_MODEL), lambda i, pr: (0, i, 0))],
        out_specs=pl.BlockSpec((MIX_HALF, D_MODEL), lambda i, pr: (_shard_tile(i, pr[1]), 0)))
    return pl.pallas_call(
        body, name=name, grid_spec=grid_spec,
        out_shape=jax.ShapeDtypeStruct((PACK_ROWS, D_MODEL), F32),
        compiler_params=_cparams(("arbitrary",)),
    )(place, land)


def _small_all_reduce(packed, shard, *, name):
    rows = packed.shape[0]
    half = rows // 2

    def body(p_ref, sh_in_ref, o_ref, sh_ref, sib_ref, slots_ref, send_sems, recv_sems, share_send, share_recv):
        x, y, c, others = _mesh_place()
        me = 2 * x + y
        sibling = (x, y, 1 - c)
        share_start, share_finish = _share_stages(sh_ref, share_send, share_recv)
        share_start()

        def half_of(core):
            return pl.ds(pl.multiple_of(core * half, 8), half)

        def remote(k, src, dst, to):
            return pltpu.make_async_remote_copy(src_ref=src, dst_ref=dst, send_sem=send_sems.at[k],
                                                recv_sem=recv_sems.at[k], device_id=to, device_id_type=MESH)

        sib = remote(0, p_ref.at[half_of(1 - c)], sib_ref, sibling)
        sib.start()
        sib.wait()
        slots_ref[me] = p_ref[half_of(c), :] + sib_ref[...]
        sends = [remote(1 + j, slots_ref.at[me], slots_ref.at[me], (px, py, c)) for j, (px, py) in enumerate(others)]
        for cp in sends:
            cp.start()
        for j, (px, py) in enumerate(others):
            slab = slots_ref.at[2 * px + py]
            remote(1 + j, slab, slab, (px, py, c)).wait_recv()
        for cp in sends:
            cp.wait_send()
        o_ref[half_of(c), :] = (slots_ref[0] + slots_ref[1]) + (slots_ref[2] + slots_ref[3])
        back = remote(4, o_ref.at[half_of(c)], o_ref.at[half_of(c)], sibling)
        back.start()
        remote(4, o_ref.at[half_of(1 - c)], o_ref.at[half_of(1 - c)], sibling).wait_recv()
        back.wait_send()
        share_finish()

    vm = pl.BlockSpec(memory_space=pltpu.VMEM)
    hbm = pl.BlockSpec(memory_space=pl.ANY)
    return pl.pallas_call(
        body, name=name, in_specs=[vm, hbm], out_specs=[vm, hbm],
        out_shape=[jax.ShapeDtypeStruct((rows, 128), F32), jax.ShapeDtypeStruct(shard.shape, shard.dtype)],
        input_output_aliases={1: 1},
        scratch_shapes=[pltpu.VMEM((half, 128), F32), pltpu.VMEM((N_CHIPS, half, 128), F32),
                        pltpu.SemaphoreType.DMA((5,)), pltpu.SemaphoreType.DMA((5,)),
                        pltpu.SemaphoreType.DMA((3,)), pltpu.SemaphoreType.DMA((3,))],
    )(packed, shard)


def _adamw(w, g, m, v, *, g_row0, tile, name):
    rows, cols = w.shape
    assert g_row0 % tile == 0 and rows % tile == 0

    def body(w_ref, g_ref, m_ref, v_ref, go_ref, d_ref, nm_ref, nv_ref):
        g_v = g_ref[...]
        m_n = ADAM_B1 * m_ref[...] + (1.0 - ADAM_B1) * g_v
        v_n = ADAM_B2 * v_ref[...] + (1.0 - ADAM_B2) * (g_v * g_v)
        m_hat = m_n / (1.0 - ADAM_B1 ** ADAM_STEP)
        v_hat = v_n / (1.0 - ADAM_B2 ** ADAM_STEP)
        d_ref[...] = -ADAM_LR * (m_hat / (jnp.sqrt(v_hat) + ADAM_EPS) + ADAM_WD * w_ref[...])
        go_ref[...] = g_v
        nm_ref[...] = m_n
        nv_ref[...] = v_n

    spec = pl.BlockSpec((tile, cols), lambda i: (i, 0))
    gspec = pl.BlockSpec((tile, cols), lambda i: (g_row0 // tile + i, 0))
    shape = jax.ShapeDtypeStruct((rows, cols), F32)
    return pl.pallas_call(
        body, name=name, grid=(rows // tile,),
        in_specs=[spec, gspec, spec, spec], out_specs=[spec] * 4, out_shape=[shape] * 4,
        compiler_params=_cparams(("arbitrary",)),
    )(w, g, m, v)


def kernel(x, ffn1_norm_g, ffn1_w_gate, ffn1_w_up, ffn1_w_down, mix_norm_g, w_in, b_in, attn_sinks, gmlp_ln_g, gmlp_ln_b, gmlp_w_s, gmlp_b_s, attn_out_norm_g, gmlp_out_norm_g, w_out, b_out, ffn2_norm_g, ffn2_w_gate, ffn2_w_up, ffn2_w_down, final_norm_g, loss_target, m_ffn1_norm_g, m_ffn1_w_gate, m_ffn1_w_up, m_ffn1_w_down, m_mix_norm_g, m_w_in, m_b_in, m_attn_sinks, m_gmlp_ln_g, m_gmlp_ln_b, m_gmlp_w_s, m_gmlp_b_s, m_attn_out_norm_g, m_gmlp_out_norm_g, m_w_out, m_b_out, m_ffn2_norm_g, m_ffn2_w_gate, m_ffn2_w_up, m_ffn2_w_down, m_final_norm_g, v_ffn1_norm_g, v_ffn1_w_gate, v_ffn1_w_up, v_ffn1_w_down, v_mix_norm_g, v_w_in, v_b_in, v_attn_sinks, v_gmlp_ln_g, v_gmlp_ln_b, v_gmlp_w_s, v_gmlp_b_s, v_attn_out_norm_g, v_gmlp_out_norm_g, v_w_out, v_b_out, v_ffn2_norm_g, v_ffn2_w_gate, v_ffn2_w_up, v_ffn2_w_down, v_final_norm_g):
    f_args = dict(locals())
    weights = {n: f_args[n] for n in [nm for nm, _ in SMALL if nm != "loss"] + list(BIG)}
    shapes = {n: weights[n].shape for n in weights}
    shapes["loss"] = ()
    place = jnp.stack([2 * lax.axis_index("x") + lax.axis_index("y"), lax.axis_index("c")]).astype(jnp.int32)

    def with_cols(name, a):
        a2 = a.reshape(a.shape[-2], a.shape[-1])
        return a2.T if BIG_TRANSPOSED[BIG.index(name)] else a2

    def natural(name, a2):
        return (a2.T if BIG_TRANSPOSED[BIG.index(name)] else a2).reshape(shapes[name])

    pack_a, pack_b, pack_m, own_a = _pack_cast(place, [with_cols(n, weights[n]) for n in BIG], name="pack_cast")
    p = {n: weights[n].reshape(1, -1) for n in ("ffn1_norm_g", "mix_norm_g", "b_in", "gmlp_ln_g", "gmlp_ln_b",
                                                "attn_out_norm_g", "gmlp_out_norm_g", "b_out", "ffn2_norm_g",
                                                "final_norm_g")}
    p["attn_sinks"] = attn_sinks.reshape(N_Q_HEADS)
    p["gmlp_w_s"] = gmlp_w_s.reshape(GMLP_GROUPS, BLK, BLK)
    p["bs_full"] = jnp.broadcast_to(gmlp_b_s.reshape(GMLP_GROUPS, BLK).T[:, :, None],
                                    (BLK, GMLP_GROUPS, GROUP_DIM)).reshape(BLK, GMLP_W)

    loss_part, dx0, land, gs = _local_step(place, x[0], loss_target[0], p, own_a, pack_a, pack_b, pack_m)

    gs["gmlp_b_s"] = jnp.sum(gs["gmlp_b_s"].reshape(BLK, GMLP_GROUPS, GROUP_DIM), axis=-1).T
    gs["attn_sinks"] = gs["attn_sinks"][:, 0]
    gs["loss"] = loss_part[0, 0]
    small_sum, shard = _small_all_reduce(_pack_small(gs), _rs_reduce(place, land, name="rs_reduce"),
                                         name="small_all_reduce")

    grad_w, delta, new_m, new_v = {}, {}, {}, {}
    off = 0
    for n, rows in zip(BIG, BIG_ROWS):
        res = _adamw(with_cols(n, weights[n]), shard, with_cols(n, f_args["m_" + n]), with_cols(n, f_args["v_" + n]),
                     g_row0=off, tile=FF_SH // 2 if rows == FF_SH else 64, name="adamw_" + n)
        grad_w[n], delta[n], new_m[n], new_v[n] = [natural(n, a) for a in res]
        off += rows
    sm = {k: {n: f_args[k + n] for n, _ in SMALL if n != "loss"} for k in ("", "m_", "v_")}
    for k in sm:
        sm[k]["loss"] = jnp.zeros((), F32)
    res = _adamw(_pack_small(sm[""]), small_sum, _pack_small(sm["m_"]), _pack_small(sm["v_"]),
                 g_row0=0, tile=SMALL_ROWS, name="adamw_small")
    small = _unpack_small(res[0], shapes)
    for dst, packed in ((grad_w, res[0]), (delta, res[1]), (new_m, res[2]), (new_v, res[3])):
        dst.update({n: a for n, a in _unpack_small(packed, shapes).items() if n != "loss"})

    order = ('ffn1_norm_g', 'ffn1_w_gate', 'ffn1_w_up', 'ffn1_w_down', 'mix_norm_g', 'w_in', 'b_in', 'attn_sinks',
             'gmlp_ln_g', 'gmlp_ln_b', 'gmlp_w_s', 'gmlp_b_s', 'attn_out_norm_g', 'gmlp_out_norm_g', 'w_out', 'b_out',
             'ffn2_norm_g', 'ffn2_w_gate', 'ffn2_w_up', 'ffn2_w_down', 'final_norm_g')
    return (small["loss"], dx0.reshape(x.shape), *[grad_w[n] for n in order], *[delta[n] for n in order],
            *[new_m[n] for n in order], *[new_v[n] for n in order])
```

```python
import functools

import jax
import jax.numpy as jnp
from jax import lax
from jax.experimental import pallas as pl
from jax.experimental.pallas import tpu as pltpu

F32 = jnp.float32
BF16 = jnp.bfloat16

D_MODEL = 1024
D_FF = 2816
N_CHIPS = 4
FF_SH = D_FF // N_CHIPS
N_Q_HEADS = 8
N_KV_HEADS = 2
REP = N_Q_HEADS // N_KV_HEADS
HEAD_DIM = 64
ATTN_W = 512
KV_W = 128
GMLP_W = 512
GMLP_GROUPS = 8
GROUP_DIM = 64
BLK = 128
MIX_FWD_BLOCKS = 2
MIX_BWD_BLOCKS = 4
IN_W = 1792
IN_SH = IN_W // N_CHIPS
OUT_SH = D_MODEL // N_CHIPS
EPS = 1e-6
FFN_RES = 0.5
ATTN_SCALE = HEAD_DIM ** -0.5

ADAM_LR = 0.001
ADAM_B1 = 0.9
ADAM_B2 = 0.999
ADAM_EPS = 1e-08
ADAM_WD = 0.01
ADAM_STEP = 10

V7X_VMEM_LIMIT = 56 * 1024 * 1024
MESH = pl.DeviceIdType.MESH


def _cparams(sem):
    return pltpu.CompilerParams(dimension_semantics=sem, vmem_limit_bytes=V7X_VMEM_LIMIT)


def _dot(a, b):
    return jnp.dot(a, b, preferred_element_type=F32)


def _dot_nt(a, b):
    return lax.dot_general(a, b, (((1,), (1,)), ((), ())), preferred_element_type=F32)


def _dot_tn(a, b):
    return lax.dot_general(a, b, (((0,), (0,)), ((), ())), preferred_element_type=F32)


def _rms(x, g):
    r = lax.rsqrt(jnp.mean(x * x, axis=-1, keepdims=True) + EPS)
    return x * r * g, r


def _rms_bwd(dh, x, r, g):
    gy = dh * g
    dx = r * gy - x * (r * r * r) * jnp.mean(gy * x, axis=-1, keepdims=True)
    dg = jnp.sum(dh * x * r, axis=0, keepdims=True)
    return dx, dg


def _const(shape):
    nd = len(shape)
    return pl.BlockSpec(shape, lambda *_: (0,) * nd)


def _rows(t, w):
    return pl.BlockSpec((t, w), lambda i: (i, 0))


PACK_ROWS = 7 * FF_SH
HALF_ROWS = PACK_ROWS // 2
FFN_HALF = 3 * FF_SH // 2
MIX_HALF = FF_SH // 2
PACK_A_ROWS = 3 * FF_SH
PACK_B_ROWS = 3 * FF_SH
PACK_M_ROWS = FF_SH
BIG = ("ffn1_w_gate", "ffn1_w_up", "ffn1_w_down", "ffn2_w_gate", "ffn2_w_up", "ffn2_w_down", "w_in", "w_out")
BIG_ROWS = (FF_SH, FF_SH, FF_SH, FF_SH, FF_SH, FF_SH, IN_SH, OUT_SH)
BIG_TRANSPOSED = (True, True, False, True, True, False, True, False)

SMALL = (("ffn1_norm_g", 1024), ("mix_norm_g", 1024), ("b_in", 1792), ("attn_sinks", 8), ("gmlp_ln_g", 512),
         ("gmlp_ln_b", 512), ("gmlp_w_s", 131072), ("gmlp_b_s", 1024), ("attn_out_norm_g", 512),
         ("gmlp_out_norm_g", 512), ("b_out", 1024), ("ffn2_norm_g", 1024), ("final_norm_g", 1024), ("loss", 1))


def _small_rows(n):
    return -(-n // 1024) * 8


SMALL_USED_ROWS = sum(_small_rows(n) for _, n in SMALL)
SMALL_ROWS = -(-SMALL_USED_ROWS // 16) * 16


def _pack_small(parts):
    out = []
    for name, n in SMALL:
        flat = parts[name].reshape(-1).astype(F32)
        rows = _small_rows(n)
        out.append(jnp.pad(flat, (0, rows * 128 - n)).reshape(rows, 128))
    if SMALL_ROWS > SMALL_USED_ROWS:
        out.append(jnp.zeros((SMALL_ROWS - SMALL_USED_ROWS, 128), F32))
    return jnp.concatenate(out, axis=0)


def _unpack_small(packed, shapes):
    res, off = {}, 0
    for name, n in SMALL:
        rows = _small_rows(n)
        res[name] = packed[off:off + rows].reshape(-1)[:n].reshape(shapes[name])
        off += rows
    return res


def _ffn_tile(x, g, wg_ref, wu_ref, wd_ref, hb_ref, a_ref, b_ref):
    h, _ = _rms(x, g)
    hb = h.astype(BF16)
    hb_ref[...] = hb
    acc = jnp.zeros(x.shape, F32)
    for j in range(N_CHIPS):
        a = _dot_nt(hb, wg_ref[j])
        b = _dot_nt(hb, wu_ref[j])
        a_ref[j] = a
        b_ref[j] = b
        f = (a * jax.nn.sigmoid(a) * b).astype(BF16)
        acc = acc + _dot(f, wd_ref[j])
    return x + FFN_RES * acc


def _ffn_saved_specs(s, tile):
    ab = pl.BlockSpec((N_CHIPS, tile, FF_SH), lambda i: (0, i, 0))
    shape = jax.ShapeDtypeStruct((N_CHIPS, s, FF_SH), F32)
    return [_rows(tile, D_MODEL), ab, ab], [jax.ShapeDtypeStruct((s, D_MODEL), BF16), shape, shape]


def _ffn_weight_specs(k0):
    one = pl.Buffered(1)
    return [pl.BlockSpec((N_CHIPS, FF_SH, D_MODEL), functools.partial(lambda kk, i: (0, kk, 0), k0 + d),
                         pipeline_mode=one) for d in range(3)]


def _mesh_place():
    x, y, c = lax.axis_index("x"), lax.axis_index("y"), lax.axis_index("c")
    others = [(1 - x, y), (x, 1 - y), (1 - x, 1 - y)]
    return x, y, c, others


def _gather_stages(o_ref, send_sems, recv_sems):
    x, y, c, others = _mesh_place()
    me = 2 * x + y
    sibling = (x, y, 1 - c)
    half_rows = o_ref.shape[1] // 2

    def half(slab, core):
        return o_ref.at[slab, pl.ds(pl.multiple_of(core * half_rows, 16), half_rows)]

    def copy(k, rows, to):
        return pltpu.make_async_remote_copy(src_ref=rows, dst_ref=rows, send_sem=send_sems.at[k],
                                            recv_sem=recv_sems.at[k], device_id=to, device_id_type=MESH)

    first = [copy(j, half(me, c), (px, py, c)) for j, (px, py) in enumerate(others)]
    passed = [copy(3 + j, half(2 * px + py, c), sibling) for j, (px, py) in enumerate(others)]

    def landed(j):
        px, py = others[j]
        copy(j, half(2 * px + py, c), (px, py, c)).wait_recv()
        passed[j].start()

    def sibling_landed(j):
        px, py = others[j]
        copy(3 + j, half(2 * px + py, 1 - c), sibling).wait_recv()

    def start():
        for cp in first:
            cp.start()

    def forward():
        for j in range(len(others)):
            landed(j)

    def finish():
        for j in range(len(others)):
            sibling_landed(j)
        for cp in first + passed:
            cp.wait_send()

    return start, forward, finish, (first, passed, landed, sibling_landed)


def _swiglu_slab(hb, wg, wu, wd):
    a = _dot_nt(hb, wg)
    b = _dot_nt(hb, wu)
    return a, b, _dot((a * jax.nn.sigmoid(a) * b).astype(BF16), wd)


def _ffn1_own(x, g, own, gather, *, tile, name):
    s = x.shape[0]
    nt = s // tile
    y_neighbour = 1

    def body(x_ref, g_ref, wg_ref, wu_ref, wd_ref, gin_ref, hb_ref, a_ref, b_ref, p_ref, gat_ref, w2_ref, hb_all_ref,
             send_sems, recv_sems, w2_sem):
        ps, i = pl.program_id(0), pl.program_id(1)
        rows = pl.ds(pl.multiple_of(i * tile, tile), tile)
        xi, yi, _, _ = _mesh_place()
        _, _, _, (first, passed, landed, sibling_landed) = _gather_stages(gat_ref, send_sems, recv_sems)

        @pl.when(jnp.logical_and(ps == 0, i == 0))
        def _():
            first[0].start()
            first[1].start()

        @pl.when(jnp.logical_and(ps == 1, i == 0))
        def _():
            first[0].wait_send()
            first[1].wait_send()
            first[2].start()
            landed(y_neighbour)
            sibling_landed(y_neighbour)
            load = pltpu.make_async_copy(gat_ref.at[2 * xi + (1 - yi)], w2_ref, w2_sem)
            load.start()
            load.wait()

        @pl.when(ps == 0)
        def _():
            h, _ = _rms(x_ref[...], g_ref[...])
            hb = h.astype(BF16)
            hb_ref[...] = hb
            hb_all_ref[rows, :] = hb
            a_ref[0], b_ref[0], part = _swiglu_slab(hb, wg_ref[...], wu_ref[...], wd_ref[...])
            p_ref[0] = x_ref[...] + FFN_RES * part

        @pl.when(ps == 1)
        def _():
            a_ref[0], b_ref[0], part = _swiglu_slab(hb_all_ref[rows, :], w2_ref[0:FF_SH, :],
                                                    w2_ref[FF_SH:2 * FF_SH, :], w2_ref[2 * FF_SH:3 * FF_SH, :])
            p_ref[0] = FFN_RES * part

        @pl.when(jnp.logical_and(ps == 1, i == nt - 1))
        def _():
            for j in (0, 2):
                landed(j)
            for j in (0, 2):
                sibling_landed(j)
            for cp in [first[2]] + passed:
                cp.wait_send()

    one = pl.Buffered(1)
    wspecs = [pl.BlockSpec((FF_SH, D_MODEL), functools.partial(lambda kk, ps, i: (kk, 0), k), pipeline_mode=one)
              for k in range(3)]
    hbm = pl.BlockSpec(memory_space=pl.ANY)
    first_pass_tiles = pl.BlockSpec((tile, D_MODEL), lambda ps, i: (jnp.where(ps == 0, i, nt - 1), 0))
    by_pass = lambda w: pl.BlockSpec((1, tile, w), lambda ps, i: (ps, i, 0))
    return pl.pallas_call(
        body, name=name, grid=(2, nt),
        in_specs=[first_pass_tiles, pl.BlockSpec((1, D_MODEL), lambda ps, i: (0, 0))] + wspecs + [hbm],
        out_specs=[first_pass_tiles, by_pass(FF_SH), by_pass(FF_SH), by_pass(D_MODEL), hbm],
        out_shape=[jax.ShapeDtypeStruct((s, D_MODEL), BF16), jax.ShapeDtypeStruct((N_CHIPS, s, FF_SH), F32),
                   jax.ShapeDtypeStruct((N_CHIPS, s, FF_SH), F32), jax.ShapeDtypeStruct((2, s, D_MODEL), F32),
                   jax.ShapeDtypeStruct(gather.shape, gather.dtype)],
        input_output_aliases={5: 4},
        scratch_shapes=[pltpu.VMEM((PACK_A_ROWS, D_MODEL), BF16), pltpu.VMEM((s, D_MODEL), BF16),
                        pltpu.SemaphoreType.DMA((6,)), pltpu.SemaphoreType.DMA((6,)), pltpu.SemaphoreType.DMA],
        compiler_params=_cparams(("arbitrary", "arbitrary")),
    )(x, g, own, own, own, gather)


def _ffn1_others(place, hb, p_own, a_all, b_all, pack, gather, *, tile, name):
    s = hb.shape[0]
    nt = s // tile
    forward_at = max(nt - 6, 0)

    def body(place_ref, hb_ref, p_ref, *rest):
        w_refs = rest[:6]
        o_ref, a_ref, b_ref, gat_ref, send_sems, recv_sems = rest[9:]
        i = pl.program_id(0)
        start, forward, finish, _ = _gather_stages(gat_ref, send_sems, recv_sems)
        pl.when(i == 0)(start)
        hb = hb_ref[...]
        a_ref[0], b_ref[0], part2 = _swiglu_slab(hb, w_refs[0][0], w_refs[1][0], w_refs[2][0])
        a_ref[1], b_ref[1], part3 = _swiglu_slab(hb, w_refs[3][0], w_refs[4][0], w_refs[5][0])
        o_ref[...] = (p_ref[0] + p_ref[1]) + FFN_RES * (part2 + part3)
        pl.when(i == forward_at)(forward)
        pl.when(i == nt - 1)(finish)

    one = pl.Buffered(1)

    def wspec(t, kk):
        return pl.BlockSpec((1, FF_SH, D_MODEL), lambda i, pr: (jnp.bitwise_xor(pr[0], t + 2), kk, 0),
                            pipeline_mode=one)

    rows = lambda w: pl.BlockSpec((tile, w), lambda i, pr: (i, 0))
    ab = pl.BlockSpec((2, tile, FF_SH), lambda i, pr: (1, i, 0))
    ab_shape = jax.ShapeDtypeStruct((N_CHIPS, s, FF_SH), F32)
    hbm = pl.BlockSpec(memory_space=pl.ANY)
    grid_spec = pltpu.PrefetchScalarGridSpec(
        num_scalar_prefetch=1, grid=(nt,),
        in_specs=[rows(D_MODEL), pl.BlockSpec((2, tile, D_MODEL), lambda i, pr: (0, i, 0))]
                 + [wspec(t, kk) for t in range(2) for kk in range(3)] + [hbm, hbm, hbm],
        out_specs=[rows(D_MODEL), ab, ab, hbm],
        scratch_shapes=[pltpu.SemaphoreType.DMA((6,)), pltpu.SemaphoreType.DMA((6,))])
    return pl.pallas_call(
        body, name=name, grid_spec=grid_spec,
        out_shape=[jax.ShapeDtypeStruct(hb.shape, F32), ab_shape, ab_shape,
                   jax.ShapeDtypeStruct(gather.shape, gather.dtype)],
        input_output_aliases={9: 1, 10: 2, 11: 3},
        compiler_params=_cparams(("arbitrary",)),
    )(place, hb, p_own, *([pack] * 6), a_all, b_all, gather)


def _ffn_fwd_loss(x, g, pack, k0, gf, tgt, *, tile, name):
    s = x.shape[0]

    def body(x_ref, g_ref, wg_ref, wu_ref, wd_ref, gf_ref, t_ref, dx_ref, loss_ref, dgf_ref, hb_ref, a_ref, b_ref,
             do_ref):
        @pl.when(pl.program_id(0) == 0)
        def _():
            loss_ref[...] = jnp.zeros_like(loss_ref)
            dgf_ref[...] = jnp.zeros_like(dgf_ref)

        x3 = _ffn_tile(x_ref[...], g_ref[...], wg_ref, wu_ref, wd_ref, hb_ref, a_ref, b_ref)
        gf_v = gf_ref[...]
        out, r = _rms(x3, gf_v)
        diff = out - t_ref[...]
        part = jnp.sum(jnp.sum(diff * diff, axis=-1, keepdims=True), axis=0, keepdims=True)
        loss_ref[...] += jnp.broadcast_to(part * (0.5 / D_MODEL), loss_ref.shape)
        dx, dg = _rms_bwd(diff * (1.0 / D_MODEL), x3, r, gf_v)
        dx_ref[...] = dx
        do_ref[...] = (FFN_RES * dx).astype(BF16)
        dgf_ref[...] += dg

    saved_specs, saved_shapes = _ffn_saved_specs(s, tile)
    return pl.pallas_call(
        body, name=name, grid=(s // tile,),
        in_specs=[_rows(tile, D_MODEL), _const((1, D_MODEL))] + _ffn_weight_specs(k0)
                 + [_const((1, D_MODEL)), _rows(tile, D_MODEL)],
        out_specs=[_rows(tile, D_MODEL), _const((1, 128)), _const((1, D_MODEL))] + saved_specs
                  + [_rows(tile, D_MODEL)],
        out_shape=[jax.ShapeDtypeStruct(x.shape, F32),
                   jax.ShapeDtypeStruct((1, 128), F32),
                   jax.ShapeDtypeStruct((1, D_MODEL), F32)] + saved_shapes
                  + [jax.ShapeDtypeStruct(x.shape, BF16)],
        compiler_params=_cparams(("arbitrary",)),
    )(x, g, pack, pack, pack, gf, tgt)


def _ffn_bwd(place, hb, a, b, do, pack, region, land, mix_grads, ab_by_pass=False, *, tile, name):
    s = hb.shape[0]
    nt = s // tile
    land_rows = pl.ds(region * FFN_HALF, FFN_HALF)
    mix_rows = pl.ds(2 * FFN_HALF, MIX_HALF)
    with_mix = mix_grads is not None
    with_land = land is not None
    n_others = 2 * N_CHIPS - 1

    def body(place_ref, hb_ref, a_ref, b_ref, do_ref, wg_ref, wu_ref, wd_ref, *rest):
        rest = list(rest)
        mix_ref = rest.pop(0) if with_mix else None
        if with_land:
            rest.pop(0)
        dhp_ref, land_ref, acc_ref, stage_ref, send_sems, recv_sem, local_sem = rest[:7]
        t, i = pl.program_id(0), pl.program_id(1)
        xi, yi, c = lax.axis_index("x"), lax.axis_index("y"), lax.axis_index("c")
        dev = 4 * xi + 2 * yi + c
        tt = (t + 1) % N_CHIPS
        tx, ty = jnp.bitwise_xor(xi, tt // 2), jnp.bitwise_xor(yi, tt % 2)

        def remote(src, dst, ssem, rsem, to):
            return pltpu.make_async_remote_copy(src_ref=src, dst_ref=dst, send_sem=ssem, recv_sem=rsem,
                                                device_id=to, device_id_type=MESH)

        def stage_half(h):
            return stage_ref.at[pl.ds(pl.multiple_of(h * FFN_HALF, 16), FFN_HALF)]

        if with_mix:
            mix_send, mix_recv, mix_local = rest[7:10]

            @pl.when(jnp.logical_and(t == 0, i == 0))
            def _():
                for chip in range(N_CHIPS):
                    for h in range(2):
                        src = mix_ref.at[chip, pl.ds(h * MIX_HALF, MIX_HALF)]
                        dst = land_ref.at[dev, mix_rows]
                        mine = jnp.logical_and(2 * xi + yi == chip, c == h)

                        @pl.when(mine)
                        def _():
                            pltpu.make_async_copy(src, dst, mix_local).start()

                        @pl.when(jnp.logical_not(mine))
                        def _():
                            remote(src, dst, mix_send, mix_recv, (chip // 2, chip % 2, h)).start()

        @pl.when(i == 0)
        def _():
            acc_ref[...] = jnp.zeros_like(acc_ref)

        hb = hb_ref[...]
        dob = do_ref[...]
        wg_j, wu_j, wd_j = wg_ref[0], wu_ref[0], wd_ref[0]
        a = a_ref[0]
        b = b_ref[0]
        sg = jax.nn.sigmoid(a)
        sa = a * sg
        fb = (sa * b).astype(BF16)
        df = _dot_nt(dob, wd_j)
        dbb = (df * sa).astype(BF16)
        dab = (df * b * (sg + sa * (1.0 - sg))).astype(BF16)
        dhp_ref[0] = (_dot(dab, wg_j) + _dot(dbb, wu_j)).astype(BF16)
        acc_ref[0:FF_SH, :] += _dot_tn(dab, hb)
        acc_ref[FF_SH:2 * FF_SH, :] += _dot_tn(dbb, hb)
        acc_ref[2 * FF_SH:3 * FF_SH, :] += _dot_tn(fb, dob)

        @pl.when(i == nt - 1)
        def _():
            dst = land_ref.at[dev, land_rows]

            @pl.when(t > 0)
            def _():
                for h in range(2):
                    remote(stage_half(h), dst, send_sems.at[h], recv_sem, (tx, ty, h)).wait_send()

            def cast_rows(r, carry):
                rows = pl.ds(pl.multiple_of(r * MIX_HALF, 16), MIX_HALF)
                stage_ref[rows, :] = acc_ref[rows, :].astype(BF16)
                return carry

            lax.fori_loop(0, 3 * FF_SH // MIX_HALF, cast_rows, 0)

            @pl.when(t < N_CHIPS - 1)
            def _():
                for h in range(2):
                    remote(stage_half(h), dst, send_sems.at[h], recv_sem, (tx, ty, h)).start()

            @pl.when(t == N_CHIPS - 1)
            def _():
                own = pltpu.make_async_copy(stage_half(c), dst, local_sem)
                own.start()
                sib = remote(stage_half(1 - c), dst, send_sems.at[0], recv_sem, (xi, yi, 1 - c))
                sib.start()
                sib.wait_send()
                own.wait()
                arrivals = land_ref.at[pl.ds(0, n_others), land_rows]
                remote(arrivals, arrivals, send_sems.at[0], recv_sem, (xi, yi, 1 - c)).wait_recv()
                if with_mix:
                    seven = land_ref.at[pl.ds(0, n_others), mix_rows]
                    both = remote(seven, seven, mix_send, mix_recv, (xi, yi, 1 - c))
                    both.wait_send()
                    both.wait_recv()
                    pltpu.make_async_copy(mix_ref.at[0, pl.ds(0, MIX_HALF)], land_ref.at[dev, mix_rows],
                                          mix_local).wait()

    def wspec(kk):
        return pl.BlockSpec((1, FF_SH, D_MODEL),
                            lambda t, i, pr: (jnp.bitwise_xor(pr[0], (t + 1) % N_CHIPS), kk, 0))

    xspec = pl.BlockSpec((tile, D_MODEL), lambda t, i, pr: (i, 0))
    if ab_by_pass:
        abspec = pl.BlockSpec((1, tile, FF_SH), lambda t, i, pr: ((t + 1) % N_CHIPS, i, 0))
    else:
        abspec = pl.BlockSpec((1, tile, FF_SH), lambda t, i, pr: (jnp.bitwise_xor(pr[0], (t + 1) % N_CHIPS), i, 0))
    hbm = pl.BlockSpec(memory_space=pl.ANY)
    in_specs = [xspec, abspec, abspec, xspec, wspec(0), wspec(1), wspec(2)]
    operands = [place, hb, a, b, do, pack, pack, pack]
    scratch = [pltpu.VMEM((3 * FF_SH, D_MODEL), F32), pltpu.VMEM((3 * FF_SH, D_MODEL), BF16),
               pltpu.SemaphoreType.DMA((2,)), pltpu.SemaphoreType.DMA, pltpu.SemaphoreType.DMA]
    if with_mix:
        in_specs.append(hbm)
        operands.append(mix_grads)
        scratch += [pltpu.SemaphoreType.DMA, pltpu.SemaphoreType.DMA, pltpu.SemaphoreType.DMA]
    aliases = {}
    if with_land:
        in_specs.append(hbm)
        operands.append(land)
        aliases = {len(operands) - 1: 1}
    grid_spec = pltpu.PrefetchScalarGridSpec(
        num_scalar_prefetch=1, grid=(N_CHIPS, nt), in_specs=in_specs,
        out_specs=[pl.BlockSpec((1, tile, D_MODEL), lambda t, i, pr: (t, i, 0)), hbm],
        scratch_shapes=scratch)
    return pl.pallas_call(
        body, name=name, grid_spec=grid_spec,
        out_shape=[jax.ShapeDtypeStruct((N_CHIPS, s, D_MODEL), BF16),
                   jax.ShapeDtypeStruct((2 * N_CHIPS, HALF_ROWS, D_MODEL), BF16)],
        input_output_aliases=aliases,
        compiler_params=_cparams(("arbitrary", "arbitrary")),
    )(*operands)


def _mix_grads_pack(dw_in_t, dw_out, *, name):
    def body(a_ref, b_ref, o_ref):
        o_ref[0, 0:IN_SH, :] = a_ref[0].astype(BF16)
        o_ref[0, IN_SH:FF_SH, :] = b_ref[0].astype(BF16)

    return pl.pallas_call(
        body, name=name, grid=(N_CHIPS,),
        in_specs=[pl.BlockSpec((1, IN_SH, D_MODEL), lambda j: (j, 0, 0)),
                  pl.BlockSpec((1, OUT_SH, D_MODEL), lambda j: (j, 0, 0))],
        out_specs=pl.BlockSpec((1, FF_SH, D_MODEL), lambda j: (j, 0, 0)),
        out_shape=jax.ShapeDtypeStruct((N_CHIPS, FF_SH, D_MODEL), BF16),
        compiler_params=_cparams(("arbitrary",)),
    )(dw_in_t.reshape(N_CHIPS, IN_SH, D_MODEL), dw_out.reshape(N_CHIPS, OUT_SH, D_MODEL))


def _share_stages(o_ref, send_sems, recv_sems):
    x, y, c, _ = _mesh_place()

    def rows(k, core):
        if k < 2:
            return o_ref.at[pl.ds(pl.multiple_of(k * 2 * FFN_HALF + core * FFN_HALF, 8), FFN_HALF)]
        return o_ref.at[pl.ds(pl.multiple_of(4 * FFN_HALF + core * MIX_HALF, 8), MIX_HALF)]

    def copy(k, core):
        return pltpu.make_async_remote_copy(src_ref=rows(k, core), dst_ref=rows(k, core), send_sem=send_sems.at[k],
                                            recv_sem=recv_sems.at[k], device_id=(x, y, 1 - c), device_id_type=MESH)

    sends = [copy(k, c) for k in range(3)]

    def start():
        for cp in sends:
            cp.start()

    def finish():
        for k in range(3):
            copy(k, 1 - c).wait_recv()
        for cp in sends:
            cp.wait_send()

    return start, finish


def _norm_bwd(dhp, x, dy, g, *, tile, name):
    s = x.shape[0]

    def body(dhp_ref, x_ref, dy_ref, g_ref, dx_ref, dg_ref):
        @pl.when(pl.program_id(0) == 0)
        def _():
            dg_ref[...] = jnp.zeros_like(dg_ref)

        dh = ((dhp_ref[0].astype(F32) + dhp_ref[1].astype(F32))
              + (dhp_ref[2].astype(F32) + dhp_ref[3].astype(F32)))
        x_v = x_ref[...]
        r = lax.rsqrt(jnp.mean(x_v * x_v, axis=-1, keepdims=True) + EPS)
        dx, dg = _rms_bwd(dh, x_v, r, g_ref[...])
        dx_ref[...] = dy_ref[...] + dx
        dg_ref[...] += dg

    return pl.pallas_call(
        body, name=name, grid=(s // tile,),
        in_specs=[pl.BlockSpec((N_CHIPS, tile, D_MODEL), lambda i: (0, i, 0)),
                  _rows(tile, D_MODEL), _rows(tile, D_MODEL), _const((1, D_MODEL))],
        out_specs=[_rows(tile, D_MODEL), _const((1, D_MODEL))],
        out_shape=[jax.ShapeDtypeStruct(x.shape, F32), jax.ShapeDtypeStruct((1, D_MODEL), F32)],
        compiler_params=_cparams(("arbitrary",)),
    )(dhp, x, dy, g)


def _mix_in_bwd(x, dy, dq, dk, dv, dz, g, w_in_t, *, tile, name):
    s = x.shape[0]

    def body(x_ref, dy_ref, dq_ref, dk_ref, dv_ref, dz_ref, g_ref, w_ref, dx_ref, dw_ref, db_ref, dg_ref, do_ref):
        @pl.when(pl.program_id(0) == 0)
        def _():
            dw_ref[...] = jnp.zeros_like(dw_ref)
            db_ref[...] = jnp.zeros_like(db_ref)
            dg_ref[...] = jnp.zeros_like(dg_ref)

        dproj = jnp.concatenate([dq_ref[...], dk_ref[...], dv_ref[...], dz_ref[...]], axis=-1)
        db_ref[...] += jnp.sum(dproj, axis=0, keepdims=True)
        dpb = dproj.astype(BF16)
        x_v = x_ref[...]
        g_v = g_ref[...]
        h, r = _rms(x_v, g_v)
        dw_ref[...] += _dot_tn(dpb, h.astype(BF16))
        dh = _dot(dpb, w_ref[...])
        dxn, dg = _rms_bwd(dh, x_v, r, g_v)
        dx = dy_ref[...] + dxn
        dx_ref[...] = dx
        do_ref[...] = (FFN_RES * dx).astype(BF16)
        dg_ref[...] += dg

    return pl.pallas_call(
        body, name=name, grid=(s // tile,),
        in_specs=[_rows(tile, D_MODEL), _rows(tile, D_MODEL), _rows(tile, ATTN_W), _rows(tile, KV_W),
                  _rows(tile, KV_W), _rows(tile, 2 * GMLP_W), _const((1, D_MODEL)), _const((IN_W, D_MODEL))],
        out_specs=[_rows(tile, D_MODEL), _const((IN_W, D_MODEL)), _const((1, IN_W)), _const((1, D_MODEL)),
                   _rows(tile, D_MODEL)],
        out_shape=[jax.ShapeDtypeStruct(x.shape, F32), jax.ShapeDtypeStruct((IN_W, D_MODEL), F32),
                   jax.ShapeDtypeStruct((1, IN_W), F32), jax.ShapeDtypeStruct((1, D_MODEL), F32),
                   jax.ShapeDtypeStruct(x.shape, BF16)],
        compiler_params=_cparams(("arbitrary",)),
    )(x, dy, dq, dk, dv, dz, g, w_in_t)


_GELU_C = 0.7978845608028654
_GELU_A = 0.044715


def _gelu_tanh(x):
    x2 = x * x
    return jnp.tanh(_GELU_C * (x + _GELU_A * (x2 * x))), x2


def _band(ref, i):
    prev = jnp.maximum(i - 1, 0)
    return jnp.concatenate([ref[pl.ds(pl.multiple_of(prev * BLK, BLK), BLK), :],
                            ref[pl.ds(pl.multiple_of(i * BLK, BLK), BLK), :]], axis=0)


def _key_in_block():
    return lax.broadcasted_iota(jnp.int32, (BLK, BLK), 0) <= lax.broadcasted_iota(jnp.int32, (BLK, BLK), 1)


def _fold(band, own):
    return jnp.where(own, band[BLK:], band[:BLK])


def _unfold(a, own):
    zero = jnp.zeros_like(a)
    return jnp.concatenate([jnp.where(own, zero, a), jnp.where(own, a, zero)], axis=0).astype(BF16)


def _attn_fwd(q, kb, vb, i, sink_ref):
    own = _key_in_block()
    outs, saved = [], []
    for h in range(N_Q_HEADS):
        cols = slice((h // REP) * HEAD_DIM, (h // REP + 1) * HEAD_DIM)
        s2 = _dot_nt(kb[:, cols], q[:, h * HEAD_DIM:(h + 1) * HEAD_DIM])
        sc = jnp.where(own, s2[BLK:], jnp.where(i > 0, s2[:BLK], -jnp.inf))
        sink = sink_ref[h]
        m = jnp.maximum(jnp.max(sc, axis=0, keepdims=True), sink)
        p = jnp.exp(sc - m)
        es = jnp.exp(sink - m)
        inv = 1.0 / (jnp.sum(p, axis=0, keepdims=True) + es)
        pn = p * inv
        band = _unfold(pn, own)
        outs.append(_dot_tn(band, vb[:, cols]))
        saved.append((pn, band, es * inv))
    return jnp.concatenate(outs, axis=-1), saved


def _tril_mask():
    t = lax.broadcasted_iota(jnp.int32, (BLK, BLK), 0)
    s_ = lax.broadcasted_iota(jnp.int32, (BLK, BLK), 1)
    return s_ <= t


def _gmlp_fwd_parts(zg, lng, lnb, ws_ref, bs_full):
    th, zg2 = _gelu_tanh(zg)
    z = 0.5 * zg * (1.0 + th)
    u = z[:, :GMLP_W]
    zv = z[:, GMLP_W:]
    mu = jnp.mean(zv, axis=-1, keepdims=True)
    zc = zv - mu
    rstd = lax.rsqrt(jnp.mean(zc * zc, axis=-1, keepdims=True) + EPS)
    xh = zc * rstd
    vvb = (xh * lng + lnb).astype(BF16)
    tril = _tril_mask()
    wms, parts = [], []
    for gi in range(GMLP_GROUPS):
        wm = jnp.where(tril, ws_ref[gi], 0.0).astype(BF16)
        wms.append(wm)
        parts.append(_dot(wm, vvb[:, gi * GROUP_DIM:(gi + 1) * GROUP_DIM]))
    mixed = jnp.concatenate(parts, axis=-1) + bs_full
    gelu_grad = 0.5 * (1.0 + th) + 0.5 * zg * (1.0 - th * th) * (_GELU_C * (1.0 + 3.0 * _GELU_A * zg2))
    return u, xh, rstd, vvb, wms, mixed, gelu_grad


def _mixer_fwd(x1, g, w_in_t, b_in, sinks, lng, lnb, w_s, bs_full, gao, ggo, w_out, b_out, gather, *, name):
    s = x1.shape[0]
    nb = min(MIX_FWD_BLOCKS, s // BLK)
    step_rows = nb * BLK
    last = s // step_rows - 1

    def tile_of(i, lag):
        return jnp.clip(i - lag, 0, last)

    def body(sink_ref, xa_ref, xc_ref, g_ref, wi_ref, bi_ref, lng_ref, lnb_ref, ws_ref, bs_ref, gao_ref, ggo_ref,
             wo_ref, bo_ref, gin_ref, q_ref, k_ref, v_ref, z_ref, y_ref, o_ref, gat_ref, qs_ref, zs_ref, ys_ref,
             send_sems, recv_sems):
        i = pl.program_id(0)
        start, forward, finish, _ = _gather_stages(gat_ref, send_sems, recv_sems)

        @pl.when(i == 0)
        def _():
            for ref in (k_ref, v_ref, qs_ref, zs_ref, ys_ref):
                ref[...] = jnp.zeros_like(ref)
            start()

        slot_a, slot_b, slot_c = i % 2, (i + 1) % 2, i % 2

        o_ref[...] = xc_ref[...] + (_dot(ys_ref[slot_c], wo_ref[...]) + bo_ref[...])

        tile_b = tile_of(i, 1)
        for b in range(nb):
            blk = tile_b * nb + b
            rows = slice(b * BLK, (b + 1) * BLK)
            y_attn, _ = _attn_fwd(qs_ref[slot_b, rows, :], _band(k_ref, blk), _band(v_ref, blk), blk, sink_ref)
            u, _, _, _, _, mixed, _ = _gmlp_fwd_parts(zs_ref[slot_b, rows, :], lng_ref[...], lnb_ref[...], ws_ref,
                                                      bs_ref[...])
            ya, _ = _rms(y_attn, gao_ref[...])
            yg, _ = _rms(u * mixed, ggo_ref[...])
            y_blk = jnp.concatenate([ya, yg], axis=-1).astype(BF16)
            y_ref[rows, :] = y_blk
            ys_ref[slot_b, rows, :] = y_blk

        h, _ = _rms(xa_ref[...], g_ref[...])
        proj = _dot_nt(h.astype(BF16), wi_ref[...]) + bi_ref[...]
        q_t = (proj[:, :ATTN_W] * ATTN_SCALE).astype(BF16)
        z_t = proj[:, ATTN_W + 2 * KV_W:]
        here = pl.ds(pl.multiple_of(tile_of(i, 0) * step_rows, step_rows), step_rows)
        q_ref[...] = q_t
        z_ref[...] = z_t
        qs_ref[slot_a] = q_t
        zs_ref[slot_a] = z_t
        k_ref[here, :] = proj[:, ATTN_W:ATTN_W + KV_W].astype(BF16)
        v_ref[here, :] = proj[:, ATTN_W + KV_W:ATTN_W + 2 * KV_W].astype(BF16)
        pl.when(i == max(last - 3, 0))(forward)
        pl.when(i == last + 2)(finish)

    def lagged(width, lag):
        return pl.BlockSpec((step_rows, width), lambda i: (tile_of(i, lag), 0))

    return pl.pallas_call(
        body, name=name, grid=(last + 3,),
        in_specs=[pl.BlockSpec(memory_space=pltpu.SMEM),
                  lagged(D_MODEL, 0), lagged(D_MODEL, 2), _const((1, D_MODEL)), _const((IN_W, D_MODEL)),
                  _const((1, IN_W)), _const((1, GMLP_W)), _const((1, GMLP_W)), _const((GMLP_GROUPS, BLK, BLK)),
                  _const((BLK, GMLP_W)), _const((1, ATTN_W)), _const((1, GMLP_W)), _const((D_MODEL, D_MODEL)),
                  _const((1, D_MODEL)), pl.BlockSpec(memory_space=pl.ANY)],
        out_specs=[lagged(ATTN_W, 0), _const((s, KV_W)), _const((s, KV_W)), lagged(2 * GMLP_W, 0),
                   lagged(D_MODEL, 1), lagged(D_MODEL, 2), pl.BlockSpec(memory_space=pl.ANY)],
        out_shape=[jax.ShapeDtypeStruct((s, ATTN_W), BF16), jax.ShapeDtypeStruct((s, KV_W), BF16),
                   jax.ShapeDtypeStruct((s, KV_W), BF16), jax.ShapeDtypeStruct((s, 2 * GMLP_W), F32),
                   jax.ShapeDtypeStruct((s, D_MODEL), BF16), jax.ShapeDtypeStruct((s, D_MODEL), F32),
                   jax.ShapeDtypeStruct(gather.shape, gather.dtype)],
        input_output_aliases={14: 6},
        scratch_shapes=[pltpu.VMEM((2, step_rows, ATTN_W), BF16), pltpu.VMEM((2, step_rows, 2 * GMLP_W), F32),
                        pltpu.VMEM((2, step_rows, D_MODEL), BF16),
                        pltpu.SemaphoreType.DMA((6,)), pltpu.SemaphoreType.DMA((6,))],
        compiler_params=_cparams(("arbitrary",)),
    )(sinks, x1, x1, g, w_in_t, b_in, lng, lnb, w_s, bs_full, gao, ggo, w_out, b_out, gather)


def _norm_bwd_mix_out(dhp, x, dy, g, yb, w_out, *, tile, name):
    s = x.shape[0]

    def body(dhp_ref, x_ref, dy_ref, g_ref, y_ref, w_ref, dx_ref, dg_ref, dyy_ref, dw_ref, db_ref):
        @pl.when(pl.program_id(0) == 0)
        def _():
            dg_ref[...] = jnp.zeros_like(dg_ref)
            dw_ref[...] = jnp.zeros_like(dw_ref)
            db_ref[...] = jnp.zeros_like(db_ref)

        dh = ((dhp_ref[0].astype(F32) + dhp_ref[1].astype(F32))
              + (dhp_ref[2].astype(F32) + dhp_ref[3].astype(F32)))
        x_v = x_ref[...]
        r = lax.rsqrt(jnp.mean(x_v * x_v, axis=-1, keepdims=True) + EPS)
        dxn, dg = _rms_bwd(dh, x_v, r, g_ref[...])
        dx = dy_ref[...] + dxn
        dx_ref[...] = dx
        dg_ref[...] += dg
        dxb = dx.astype(BF16)
        db_ref[...] += jnp.sum(dx, axis=0, keepdims=True)
        dw_ref[...] += _dot_tn(y_ref[...], dxb)
        dyy_ref[...] = _dot_nt(dxb, w_ref[...])

    return pl.pallas_call(
        body, name=name, grid=(s // tile,),
        in_specs=[pl.BlockSpec((N_CHIPS, tile, D_MODEL), lambda i: (0, i, 0)),
                  _rows(tile, D_MODEL), _rows(tile, D_MODEL), _const((1, D_MODEL)), _rows(tile, D_MODEL),
                  _const((D_MODEL, D_MODEL))],
        out_specs=[_rows(tile, D_MODEL), _const((1, D_MODEL)), _rows(tile, D_MODEL), _const((D_MODEL, D_MODEL)),
                   _const((1, D_MODEL))],
        out_shape=[jax.ShapeDtypeStruct(x.shape, F32), jax.ShapeDtypeStruct((1, D_MODEL), F32),
                   jax.ShapeDtypeStruct(x.shape, F32), jax.ShapeDtypeStruct((D_MODEL, D_MODEL), F32),
                   jax.ShapeDtypeStruct((1, D_MODEL), F32)],
        compiler_params=_cparams(("arbitrary",)),
    )(dhp, x, dy, g, yb, w_out)


def _mix_core_bwd(dyy, q, k, v, zg, sinks, lng, lnb, w_s, bs_full, gao, ggo, *, name):
    s = dyy.shape[0]
    nb = min(MIX_BWD_BLOCKS, s // BLK)
    nsteps = s // (nb * BLK)

    def body(*refs):
        accumulators = refs[13:15] + refs[16:]

        @pl.when(pl.program_id(0) == 0)
        def _():
            for ref in accumulators:
                ref[...] = jnp.zeros_like(ref)

        for b in range(nb):
            one_block(pl.program_id(0) * nb + b, slice(b * BLK, (b + 1) * BLK), *refs)

        @pl.when(pl.program_id(0) == nsteps - 1)
        def _():
            tril = _tril_mask()
            for gi in range(GMLP_GROUPS):
                refs[20][gi] = jnp.where(tril, refs[20][gi], 0.0)

    def one_block(i, rows, sink_ref, dyy_ref, q_ref, k_ref, v_ref, z_ref, lng_ref, lnb_ref, ws_ref, bs_ref, gao_ref,
                  ggo_ref, dq_ref, dk_ref, dv_ref, dz_ref, dgao_ref, dggo_ref, dlng_ref, dlnb_ref, dws_ref, dms_ref,
                  dsk_ref):
        q_v = q_ref[rows, :]
        kb = _band(k_ref, i)
        vb = _band(v_ref, i)
        lng_v = lng_ref[...]
        gao_v = gao_ref[...]
        ggo_v = ggo_ref[...]

        y_attn, probs = _attn_fwd(q_v, kb, vb, i, sink_ref)
        u, xh, rstd, vvb, wms, mixed, gelu_grad = _gmlp_fwd_parts(z_ref[rows, :], lng_v, lnb_ref[...], ws_ref,
                                                                  bs_ref[...])
        y_gmlp = u * mixed
        ra = lax.rsqrt(jnp.mean(y_attn * y_attn, axis=-1, keepdims=True) + EPS)
        rg = lax.rsqrt(jnp.mean(y_gmlp * y_gmlp, axis=-1, keepdims=True) + EPS)

        dyy = dyy_ref[rows, :]
        d_attn, dgao = _rms_bwd(dyy[:, :ATTN_W], y_attn, ra, gao_v)
        d_gmlp, dggo = _rms_bwd(dyy[:, ATTN_W:], y_gmlp, rg, ggo_v)
        dgao_ref[...] += dgao
        dggo_ref[...] += dggo

        du = d_gmlp * mixed
        dmixed = d_gmlp * u
        dms_ref[...] += dmixed
        dmb = dmixed.astype(BF16)
        dvv_parts = []
        for gi in range(GMLP_GROUPS):
            sl = slice(gi * GROUP_DIM, (gi + 1) * GROUP_DIM)
            dws_ref[gi] += _dot_nt(dmb[:, sl], vvb[:, sl])
            dvv_parts.append(_dot_tn(wms[gi], dmb[:, sl]))
        dvv = jnp.concatenate(dvv_parts, axis=-1)
        dlng_ref[...] += jnp.sum(dvv * xh, axis=0, keepdims=True)
        dlnb_ref[...] += jnp.sum(dvv, axis=0, keepdims=True)
        dxh = dvv * lng_v
        dzv = rstd * (dxh - jnp.mean(dxh, axis=-1, keepdims=True)
                      - xh * jnp.mean(dxh * xh, axis=-1, keepdims=True))
        dz_ref[rows, :] = jnp.concatenate([du, dzv], axis=-1) * gelu_grad

        dab = d_attn.astype(BF16)
        own = _key_in_block()
        dq_parts = []
        dk_parts = []
        dv_parts = []
        for gi in range(N_KV_HEADS):
            cols = slice(gi * HEAD_DIM, (gi + 1) * HEAD_DIM)
            kg, vg = kb[:, cols], vb[:, cols]
            dkg = jnp.zeros((2 * BLK, HEAD_DIM), F32)
            dvg = jnp.zeros((2 * BLK, HEAD_DIM), F32)
            for rr in range(REP):
                h = gi * REP + rr
                hs = slice(h * HEAD_DIM, (h + 1) * HEAD_DIM)
                qh, doh = q_v[:, hs], dab[:, hs]
                pn, band, psink = probs[h]
                dp = _fold(_dot_nt(vg, doh), own)
                delta = jnp.sum(pn * dp, axis=0, keepdims=True)
                ds2 = _unfold(pn * (dp - delta), own)
                dsink = jnp.sum(-psink * delta, axis=-1, keepdims=True)
                dsk_ref[pl.ds(h, 1), :] += jnp.broadcast_to(dsink, (1, 128))
                dq_parts.append(_dot_tn(ds2, kg) * ATTN_SCALE)
                dkg = dkg + _dot(ds2, qh)
                dvg = dvg + _dot(band, doh)
            dk_parts.append(dkg)
            dv_parts.append(dvg)
        dq_ref[rows, :] = jnp.concatenate(dq_parts, axis=-1)
        dkb = jnp.concatenate(dk_parts, axis=-1)
        dvb = jnp.concatenate(dv_parts, axis=-1)
        prev = pl.ds(pl.multiple_of(jnp.maximum(i - 1, 0) * BLK, BLK), BLK)
        cur = pl.ds(pl.multiple_of(i * BLK, BLK), BLK)
        dk_ref[prev, :] += dkb[:BLK]
        dv_ref[prev, :] += dvb[:BLK]
        dk_ref[cur, :] += dkb[BLK:]
        dv_ref[cur, :] += dvb[BLK:]

    return pl.pallas_call(
        body, name=name, grid=(nsteps,),
        in_specs=[pl.BlockSpec(memory_space=pltpu.SMEM),
                  _rows(nb * BLK, D_MODEL), _rows(nb * BLK, ATTN_W), _const((s, KV_W)), _const((s, KV_W)),
                  _rows(nb * BLK, 2 * GMLP_W), _const((1, GMLP_W)), _const((1, GMLP_W)),
                  _const((GMLP_GROUPS, BLK, BLK)), _const((BLK, GMLP_W)), _const((1, ATTN_W)), _const((1, GMLP_W))],
        out_specs=[_rows(nb * BLK, ATTN_W), _const((s, KV_W)), _const((s, KV_W)), _rows(nb * BLK, 2 * GMLP_W),
                   _const((1, ATTN_W)), _const((1, GMLP_W)),
                   _const((1, GMLP_W)), _const((1, GMLP_W)), _const((GMLP_GROUPS, BLK, BLK)),
                   _const((BLK, GMLP_W)), _const((N_Q_HEADS, 128))],
        out_shape=[jax.ShapeDtypeStruct((s, ATTN_W), F32), jax.ShapeDtypeStruct((s, KV_W), F32),
                   jax.ShapeDtypeStruct((s, KV_W), F32), jax.ShapeDtypeStruct((s, 2 * GMLP_W), F32),
                   jax.ShapeDtypeStruct((1, ATTN_W), F32), jax.ShapeDtypeStruct((1, GMLP_W), F32),
                   jax.ShapeDtypeStruct((1, GMLP_W), F32), jax.ShapeDtypeStruct((1, GMLP_W), F32),
                   jax.ShapeDtypeStruct((GMLP_GROUPS, BLK, BLK), F32), jax.ShapeDtypeStruct((BLK, GMLP_W), F32),
                   jax.ShapeDtypeStruct((N_Q_HEADS, 128), F32)],
        compiler_params=_cparams(("arbitrary",)),
    )(sinks, dyy, q, k, v, zg, lng, lnb, w_s, bs_full, gao, ggo)


def _local_step(place, x, tgt, p, own_a, pack_a, pack_b, pack_m, *, tile=512, fwd_tile=256, bwd_tile=512,
                norm_tile=512):
    g = {}
    tile, fwd_tile, bwd_tile, norm_tile = (min(t_, x.shape[0]) for t_ in (tile, fwd_tile, bwd_tile, norm_tile))
    hb1, a1, b1, part1, pack_a = _ffn1_own(x, p["ffn1_norm_g"], own_a, pack_a, tile=tile, name="ffn1_own")
    x1, a1, b1, pack_m = _ffn1_others(place, hb1, part1, a1, b1, pack_a, pack_m, tile=fwd_tile, name="ffn1_fwd")
    w_in_t = pack_m[:, :IN_SH, :].reshape(IN_W, D_MODEL)
    w_out = pack_m[:, IN_SH:, :].reshape(D_MODEL, D_MODEL)
    q, k, v, zg, yb, x2, pack_b = _mixer_fwd(
        x1, p["mix_norm_g"], w_in_t, p["b_in"], p["attn_sinks"], p["gmlp_ln_g"], p["gmlp_ln_b"], p["gmlp_w_s"],
        p["bs_full"], p["attn_out_norm_g"], p["gmlp_out_norm_g"], w_out, p["b_out"], pack_b, name="mixer_fwd")
    mix_args = (q, k, v, zg, p["attn_sinks"], p["gmlp_ln_g"], p["gmlp_ln_b"], p["gmlp_w_s"], p["bs_full"],
                p["attn_out_norm_g"], p["gmlp_out_norm_g"])
    dx3, loss, g["final_norm_g"], hb2, a2, b2, do3 = _ffn_fwd_loss(
        x2, p["ffn2_norm_g"], pack_b, 0, p["final_norm_g"], tgt, tile=fwd_tile, name="ffn2_fwd_loss")

    dhp, land = _ffn_bwd(place, hb2, a2, b2, do3, pack_b, 1, None, None, tile=bwd_tile, name="ffn2_bwd")
    dx2, g["ffn2_norm_g"], dyy, dw_out, g["b_out"] = _norm_bwd_mix_out(
        dhp, x2, dx3, p["ffn2_norm_g"], yb, w_out, tile=norm_tile, name="ffn2_norm_bwd")

    (dq, dk, dv, dz, g["attn_out_norm_g"], g["gmlp_out_norm_g"], g["gmlp_ln_g"],
     g["gmlp_ln_b"], g["gmlp_w_s"], dmix_sum, dsinks) = _mix_core_bwd(dyy, *mix_args, name="mix_core_bwd")
    g["gmlp_b_s"] = dmix_sum
    g["attn_sinks"] = dsinks
    dx1, dw_in_t, g["b_in"], g["mix_norm_g"], do1 = _mix_in_bwd(
        x1, dx2, dq, dk, dv, dz, p["mix_norm_g"], w_in_t, tile=tile, name="mix_in_bwd")
    mix_grads = _mix_grads_pack(dw_in_t, dw_out, name="mix_grads_pack")

    dhp1, land = _ffn_bwd(place, hb1, a1, b1, do1, pack_a, 0, land, mix_grads, True, tile=bwd_tile, name="ffn1_bwd")
    dx0, g["ffn1_norm_g"] = _norm_bwd(dhp1, x, dx1, p["ffn1_norm_g"], tile=norm_tile, name="ffn1_norm_bwd")
    return loss, dx0, land, g


def _pack_cast(place, parts, *, name):
    def body(place_ref, *refs):
        oa_ref, ob_ref, om_ref, own_ref = refs[-4:]
        off = 0
        for k, (ref, rows) in enumerate(zip(refs[:-4], BIG_ROWS)):
            if k in (3, 6):
                off = 0
            cast = ref[...].astype(BF16)
            (oa_ref if k < 3 else ob_ref if k < 6 else om_ref)[0, off:off + rows, :] = cast
            if k < 3:
                own_ref[off:off + rows, :] = cast
            off += rows

    one = pl.Buffered(1)

    def slab(rows):
        return pl.BlockSpec((1, rows, D_MODEL), lambda i, pr: (pr[0], 0, 0), pipeline_mode=one)

    grid_spec = pltpu.PrefetchScalarGridSpec(
        num_scalar_prefetch=1, grid=(1,),
        in_specs=[pl.BlockSpec((rows, D_MODEL), lambda i, pr: (0, 0), pipeline_mode=one) for rows in BIG_ROWS],
        out_specs=[slab(PACK_A_ROWS), slab(PACK_B_ROWS), slab(PACK_M_ROWS),
                   pl.BlockSpec((PACK_A_ROWS, D_MODEL), lambda i, pr: (0, 0), pipeline_mode=one)])
    return pl.pallas_call(
        body, name=name, grid_spec=grid_spec,
        out_shape=[jax.ShapeDtypeStruct((N_CHIPS, PACK_A_ROWS, D_MODEL), BF16),
                   jax.ShapeDtypeStruct((N_CHIPS, PACK_B_ROWS, D_MODEL), BF16),
                   jax.ShapeDtypeStruct((N_CHIPS, PACK_M_ROWS, D_MODEL), BF16),
                   jax.ShapeDtypeStruct((PACK_A_ROWS, D_MODEL), BF16)],
        compiler_params=_cparams(("arbitrary",)),
    )(place, *parts)


def _shard_tile(i, c):
    return jnp.where(i < 3, 3 * c + i, jnp.where(i < 6, 3 + 3 * c + i, 12 + c))


def _rs_reduce(place, land, *, name):
    def body(place_ref, l_ref, o_ref):
        acc = l_ref[0].astype(F32)
        for d in range(1, 2 * N_CHIPS):
            acc = acc + l_ref[d].astype(F32)
        o_ref[...] = acc

    grid_spec = pltpu.PrefetchScalarGridSpec(
        num_scalar_prefetch=1, grid=(HALF_ROWS // MIX_HALF,),
        in_specs=[pl.BlockSpec((2 * N_CHIPS, MIX_HALF, D_MODEL), lambda i, pr: (0, i, 0))],
        out_specs=pl.BlockSpec((MIX_HALF, D_MODEL), lambda i, pr: (_shard_tile(i, pr[1]), 0)))
    return pl.pallas_call(
        body, name=name, grid_spec=grid_spec,
        out_shape=jax.ShapeDtypeStruct((PACK_ROWS, D_MODEL), F32),
        compiler_params=_cparams(("arbitrary",)),
    )(place, land)


def _small_all_reduce(packed, shard, *, name):
    rows = packed.shape[0]
    half = rows // 2

    def body(p_ref, sh_in_ref, o_ref, sh_ref, sib_ref, slots_ref, send_sems, recv_sems, share_send, share_recv):
        x, y, c, others = _mesh_place()
        me = 2 * x + y
        sibling = (x, y, 1 - c)
        share_start, share_finish = _share_stages(sh_ref, share_send, share_recv)
        share_start()

        def half_of(core):
            return pl.ds(pl.multiple_of(core * half, 8), half)

        def remote(k, src, dst, to):
            return pltpu.make_async_remote_copy(src_ref=src, dst_ref=dst, send_sem=send_sems.at[k],
                                                recv_sem=recv_sems.at[k], device_id=to, device_id_type=MESH)

        sib = remote(0, p_ref.at[half_of(1 - c)], sib_ref, sibling)
        sib.start()
        sib.wait()
        slots_ref[me] = p_ref[half_of(c), :] + sib_ref[...]
        sends = [remote(1 + j, slots_ref.at[me], slots_ref.at[me], (px, py, c)) for j, (px, py) in enumerate(others)]
        for cp in sends:
            cp.start()
        for j, (px, py) in enumerate(others):
            slab = slots_ref.at[2 * px + py]
            remote(1 + j, slab, slab, (px, py, c)).wait_recv()
        for cp in sends:
            cp.wait_send()
        o_ref[half_of(c), :] = (slots_ref[0] + slots_ref[1]) + (slots_ref[2] + slots_ref[3])
        back = remote(4, o_ref.at[half_of(c)], o_ref.at[half_of(c)], sibling)
        back.start()
        remote(4, o_ref.at[half_of(1 - c)], o_ref.at[half_of(1 - c)], sibling).wait_recv()
        back.wait_send()
        share_finish()

    vm = pl.BlockSpec(memory_space=pltpu.VMEM)
    hbm = pl.BlockSpec(memory_space=pl.ANY)
    return pl.pallas_call(
        body, name=name, in_specs=[vm, hbm], out_specs=[vm, hbm],
        out_shape=[jax.ShapeDtypeStruct((rows, 128), F32), jax.ShapeDtypeStruct(shard.shape, shard.dtype)],
        input_output_aliases={1: 1},
        scratch_shapes=[pltpu.VMEM((half, 128), F32), pltpu.VMEM((N_CHIPS, half, 128), F32),
                        pltpu.SemaphoreType.DMA((5,)), pltpu.SemaphoreType.DMA((5,)),
                        pltpu.SemaphoreType.DMA((3,)), pltpu.SemaphoreType.DMA((3,))],
    )(packed, shard)


def _adamw(w, g, m, v, *, g_row0, tile, name):
    rows, cols = w.shape
    assert g_row0 % tile == 0 and rows % tile == 0

    def body(w_ref, g_ref, m_ref, v_ref, go_ref, d_ref, nm_ref, nv_ref):
        g_v = g_ref[...]
        m_n = ADAM_B1 * m_ref[...] + (1.0 - ADAM_B1) * g_v
        v_n = ADAM_B2 * v_ref[...] + (1.0 - ADAM_B2) * (g_v * g_v)
        m_hat = m_n / (1.0 - ADAM_B1 ** ADAM_STEP)
        v_hat = v_n / (1.0 - ADAM_B2 ** ADAM_STEP)
        d_ref[...] = -ADAM_LR * (m_hat / (jnp.sqrt(v_hat) + ADAM_EPS) + ADAM_WD * w_ref[...])
        go_ref[...] = g_v
        nm_ref[...] = m_n
        nv_ref[...] = v_n

    spec = pl.BlockSpec((tile, cols), lambda i: (i, 0))
    gspec = pl.BlockSpec((tile, cols), lambda i: (g_row0 // tile + i, 0))
    shape = jax.ShapeDtypeStruct((rows, cols), F32)
    return pl.pallas_call(
        body, name=name, grid=(rows // tile,),
        in_specs=[spec, gspec, spec, spec], out_specs=[spec] * 4, out_shape=[shape] * 4,
        compiler_params=_cparams(("arbitrary",)),
    )(w, g, m, v)


def kernel(x, ffn1_norm_g, ffn1_w_gate, ffn1_w_up, ffn1_w_down, mix_norm_g, w_in, b_in, attn_sinks, gmlp_ln_g, gmlp_ln_b, gmlp_w_s, gmlp_b_s, attn_out_norm_g, gmlp_out_norm_g, w_out, b_out, ffn2_norm_g, ffn2_w_gate, ffn2_w_up, ffn2_w_down, final_norm_g, loss_target, m_ffn1_norm_g, m_ffn1_w_gate, m_ffn1_w_up, m_ffn1_w_down, m_mix_norm_g, m_w_in, m_b_in, m_attn_sinks, m_gmlp_ln_g, m_gmlp_ln_b, m_gmlp_w_s, m_gmlp_b_s, m_attn_out_norm_g, m_gmlp_out_norm_g, m_w_out, m_b_out, m_ffn2_norm_g, m_ffn2_w_gate, m_ffn2_w_up, m_ffn2_w_down, m_final_norm_g, v_ffn1_norm_g, v_ffn1_w_gate, v_ffn1_w_up, v_ffn1_w_down, v_mix_norm_g, v_w_in, v_b_in, v_attn_sinks, v_gmlp_ln_g, v_gmlp_ln_b, v_gmlp_w_s, v_gmlp_b_s, v_attn_out_norm_g, v_gmlp_out_norm_g, v_w_out, v_b_out, v_ffn2_norm_g, v_ffn2_w_gate, v_ffn2_w_up, v_ffn2_w_down, v_final_norm_g):
    f_args = dict(locals())
    weights = {n: f_args[n] for n in [nm for nm, _ in SMALL if nm != "loss"] + list(BIG)}
    shapes = {n: weights[n].shape for n in weights}
    shapes["loss"] = ()
    place = jnp.stack([2 * lax.axis_index("x") + lax.axis_index("y"), lax.axis_index("c")]).astype(jnp.int32)

    def with_cols(name, a):
        a2 = a.reshape(a.shape[-2], a.shape[-1])
        return a2.T if BIG_TRANSPOSED[BIG.index(name)] else a2

    def natural(name, a2):
        return (a2.T if BIG_TRANSPOSED[BIG.index(name)] else a2).reshape(shapes[name])

    pack_a, pack_b, pack_m, own_a = _pack_cast(place, [with_cols(n, weights[n]) for n in BIG], name="pack_cast")
    p = {n: weights[n].reshape(1, -1) for n in ("ffn1_norm_g", "mix_norm_g", "b_in", "gmlp_ln_g", "gmlp_ln_b",
                                                "attn_out_norm_g", "gmlp_out_norm_g", "b_out", "ffn2_norm_g",
                                                "final_norm_g")}
    p["attn_sinks"] = attn_sinks.reshape(N_Q_HEADS)
    p["gmlp_w_s"] = gmlp_w_s.reshape(GMLP_GROUPS, BLK, BLK)
    p["bs_full"] = jnp.broadcast_to(gmlp_b_s.reshape(GMLP_GROUPS, BLK).T[:, :, None],
                                    (BLK, GMLP_GROUPS, GROUP_DIM)).reshape(BLK, GMLP_W)

    loss_part, dx0, land, gs = _local_step(place, x[0], loss_target[0], p, own_a, pack_a, pack_b, pack_m)

    gs["gmlp_b_s"] = jnp.sum(gs["gmlp_b_s"].reshape(BLK, GMLP_GROUPS, GROUP_DIM), axis=-1).T
    gs["attn_sinks"] = gs["attn_sinks"][:, 0]
    gs["loss"] = loss_part[0, 0]
    small_sum, shard = _small_all_reduce(_pack_small(gs), _rs_reduce(place, land, name="rs_reduce"),
                                         name="small_all_reduce")

    grad_w, delta, new_m, new_v = {}, {}, {}, {}
    off = 0
    for n, rows in zip(BIG, BIG_ROWS):
        res = _adamw(with_cols(n, weights[n]), shard, with_cols(n, f_args["m_" + n]), with_cols(n, f_args["v_" + n]),
                     g_row0=off, tile=FF_SH // 2 if rows == FF_SH else 64, name="adamw_" + n)
        grad_w[n], delta[n], new_m[n], new_v[n] = [natural(n, a) for a in res]
        off += rows
    sm = {k: {n: f_args[k + n] for n, _ in SMALL if n != "loss"} for k in ("", "m_", "v_")}
    for k in sm:
        sm[k]["loss"] = jnp.zeros((), F32)
    res = _adamw(_pack_small(sm[""]), small_sum, _pack_small(sm["m_"]), _pack_small(sm["v_"]),
                 g_row0=0, tile=SMALL_ROWS, name="adamw_small")
    small = _unpack_small(res[0], shapes)
    for dst, packed in ((grad_w, res[0]), (delta, res[1]), (new_m, res[2]), (new_v, res[3])):
        dst.update({n: a for n, a in _unpack_small(packed, shapes).items() if n != "loss"})

    order = ('ffn1_norm_g', 'ffn1_w_gate', 'ffn1_w_up', 'ffn1_w_down', 'mix_norm_g', 'w_in', 'b_in', 'attn_sinks',
             'gmlp_ln_g', 'gmlp_ln_b', 'gmlp_w_s', 'gmlp_b_s', 'attn_out_norm_g', 'gmlp_out_norm_g', 'w_out', 'b_out',
             'ffn2_norm_g', 'ffn2_w_gate', 'ffn2_w_up', 'ffn2_w_down', 'final_norm_g')
    return (small["loss"], dx0.reshape(x.shape), *[grad_w[n] for n in order], *[delta[n] for n in order],
            *[new_m[n] for n in order], *[new_v[n] for n in order])
```

```python
import functools

import jax
import jax.numpy as jnp
from jax import lax
from jax.experimental import pallas as pl
from jax.experimental.pallas import tpu as pltpu

F32 = jnp.float32
BF16 = jnp.bfloat16

D_MODEL = 1024
D_FF = 2816
N_CHIPS = 4
FF_SH = D_FF // N_CHIPS
N_Q_HEADS = 8
N_KV_HEADS = 2
REP = N_Q_HEADS // N_KV_HEADS
HEAD_DIM = 64
ATTN_W = 512
KV_W = 128
GMLP_W = 512
GMLP_GROUPS = 8
GROUP_DIM = 64
BLK = 128
MIX_FWD_BLOCKS = 2
MIX_BWD_BLOCKS = 4
IN_W = 1792
IN_SH = IN_W // N_CHIPS
OUT_SH = D_MODEL // N_CHIPS
EPS = 1e-6
FFN_RES = 0.5
ATTN_SCALE = HEAD_DIM ** -0.5

ADAM_LR = 0.001
ADAM_B1 = 0.9
ADAM_B2 = 0.999
ADAM_EPS = 1e-08
ADAM_WD = 0.01
ADAM_STEP = 10

V7X_VMEM_LIMIT = 56 * 1024 * 1024
MESH = pl.DeviceIdType.MESH


def _cparams(sem):
    return pltpu.CompilerParams(dimension_semantics=sem, vmem_limit_bytes=V7X_VMEM_LIMIT)


def _dot(a, b):
    return jnp.dot(a, b, preferred_element_type=F32)


def _dot_nt(a, b):
    return lax.dot_general(a, b, (((1,), (1,)), ((), ())), preferred_element_type=F32)


def _dot_tn(a, b):
    return lax.dot_general(a, b, (((0,), (0,)), ((), ())), preferred_element_type=F32)


def _rms(x, g):
    r = lax.rsqrt(jnp.mean(x * x, axis=-1, keepdims=True) + EPS)
    return x * r * g, r


def _rms_bwd(dh, x, r, g):
    gy = dh * g
    dx = r * gy - x * (r * r * r) * jnp.mean(gy * x, axis=-1, keepdims=True)
    dg = jnp.sum(dh * x * r, axis=0, keepdims=True)
    return dx, dg


def _const(shape):
    nd = len(shape)
    return pl.BlockSpec(shape, lambda *_: (0,) * nd)


def _rows(t, w):
    return pl.BlockSpec((t, w), lambda i: (i, 0))


PACK_ROWS = 7 * FF_SH
HALF_ROWS = PACK_ROWS // 2
FFN_HALF = 3 * FF_SH // 2
MIX_HALF = FF_SH // 2
PACK_A_ROWS = 3 * FF_SH
PACK_B_ROWS = 3 * FF_SH
PACK_M_ROWS = FF_SH
BIG = ("ffn1_w_gate", "ffn1_w_up", "ffn1_w_down", "ffn2_w_gate", "ffn2_w_up", "ffn2_w_down", "w_in", "w_out")
BIG_ROWS = (FF_SH, FF_SH, FF_SH, FF_SH, FF_SH, FF_SH, IN_SH, OUT_SH)
BIG_TRANSPOSED = (True, True, False, True, True, False, True, False)

SMALL = (("ffn1_norm_g", 1024), ("mix_norm_g", 1024), ("b_in", 1792), ("attn_sinks", 8), ("gmlp_ln_g", 512),
         ("gmlp_ln_b", 512), ("gmlp_w_s", 131072), ("gmlp_b_s", 1024), ("attn_out_norm_g", 512),
         ("gmlp_out_norm_g", 512), ("b_out", 1024), ("ffn2_norm_g", 1024), ("final_norm_g", 1024), ("loss", 1))


def _small_rows(n):
    return -(-n // 1024) * 8


SMALL_USED_ROWS = sum(_small_rows(n) for _, n in SMALL)
SMALL_ROWS = -(-SMALL_USED_ROWS // 16) * 16


def _pack_small(parts):
    out = []
    for name, n in SMALL:
        flat = parts[name].reshape(-1).astype(F32)
        rows = _small_rows(n)
        out.append(jnp.pad(flat, (0, rows * 128 - n)).reshape(rows, 128))
    if SMALL_ROWS > SMALL_USED_ROWS:
        out.append(jnp.zeros((SMALL_ROWS - SMALL_USED_ROWS, 128), F32))
    return jnp.concatenate(out, axis=0)


def _unpack_small(packed, shapes):
    res, off = {}, 0
    for name, n in SMALL:
        rows = _small_rows(n)
        res[name] = packed[off:off + rows].reshape(-1)[:n].reshape(shapes[name])
        off += rows
    return res


def _ffn_tile(x, g, wg_ref, wu_ref, wd_ref, hb_ref, a_ref, b_ref):
    h, _ = _rms(x, g)
    hb = h.astype(BF16)
    hb_ref[...] = hb
    acc = jnp.zeros(x.shape, F32)
    for j in range(N_CHIPS):
        a = _dot_nt(hb, wg_ref[j])
        b = _dot_nt(hb, wu_ref[j])
        a_ref[j] = a
        b_ref[j] = b
        f = (a * jax.nn.sigmoid(a) * b).astype(BF16)
        acc = acc + _dot(f, wd_ref[j])
    return x + FFN_RES * acc


def _ffn_saved_specs(s, tile):
    ab = pl.BlockSpec((N_CHIPS, tile, FF_SH), lambda i: (0, i, 0))
    shape = jax.ShapeDtypeStruct((N_CHIPS, s, FF_SH), F32)
    return [_rows(tile, D_MODEL), ab, ab], [jax.ShapeDtypeStruct((s, D_MODEL), BF16), shape, shape]


def _ffn_weight_specs(k0):
    one = pl.Buffered(1)
    return [pl.BlockSpec((N_CHIPS, FF_SH, D_MODEL), functools.partial(lambda kk, i: (0, kk, 0), k0 + d),
                         pipeline_mode=one) for d in range(3)]


def _mesh_place():
    x, y, c = lax.axis_index("x"), lax.axis_index("y"), lax.axis_index("c")
    others = [(1 - x, y), (x, 1 - y), (1 - x, 1 - y)]
    return x, y, c, others


def _gather_stages(o_ref, send_sems, recv_sems):
    x, y, c, others = _mesh_place()
    me = 2 * x + y
    sibling = (x, y, 1 - c)
    half_rows = o_ref.shape[1] // 2

    def half(slab, core):
        return o_ref.at[slab, pl.ds(pl.multiple_of(core * half_rows, 16), half_rows)]

    def copy(k, rows, to):
        return pltpu.make_async_remote_copy(src_ref=rows, dst_ref=rows, send_sem=send_sems.at[k],
                                            recv_sem=recv_sems.at[k], device_id=to, device_id_type=MESH)

    first = [copy(j, half(me, c), (px, py, c)) for j, (px, py) in enumerate(others)]
    passed = [copy(3 + j, half(2 * px + py, c), sibling) for j, (px, py) in enumerate(others)]

    def landed(j):
        px, py = others[j]
        copy(j, half(2 * px + py, c), (px, py, c)).wait_recv()
        passed[j].start()

    def sibling_landed(j):
        px, py = others[j]
        copy(3 + j, half(2 * px + py, 1 - c), sibling).wait_recv()

    def start():
        for cp in first:
            cp.start()

    def forward():
        for j in range(len(others)):
            landed(j)

    def finish():
        for j in range(len(others)):
            sibling_landed(j)
        for cp in first + passed:
            cp.wait_send()

    return start, forward, finish, (first, passed, landed, sibling_landed)


def _swiglu_slab(hb, wg, wu, wd):
    a = _dot_nt(hb, wg)
    b = _dot_nt(hb, wu)
    return a, b, _dot((a * jax.nn.sigmoid(a) * b).astype(BF16), wd)


def _ffn1_own(x, g, own, gather, *, tile, name):
    s = x.shape[0]
    nt = s // tile
    y_neighbour = 1

    def body(x_ref, g_ref, wg_ref, wu_ref, wd_ref, gin_ref, hb_ref, a_ref, b_ref, p_ref, gat_ref, w2_ref, hb_all_ref,
             send_sems, recv_sems, w2_sem):
        ps, i = pl.program_id(0), pl.program_id(1)
        rows = pl.ds(pl.multiple_of(i * tile, tile), tile)
        xi, yi, _, _ = _mesh_place()
        _, _, _, (first, passed, landed, sibling_landed) = _gather_stages(gat_ref, send_sems, recv_sems)

        @pl.when(jnp.logical_and(ps == 0, i == 0))
        def _():
            first[0].start()
            first[1].start()

        @pl.when(jnp.logical_and(ps == 1, i == 0))
        def _():
            first[0].wait_send()
            first[1].wait_send()
            first[2].start()
            landed(y_neighbour)
            sibling_landed(y_neighbour)
            load = pltpu.make_async_copy(gat_ref.at[2 * xi + (1 - yi)], w2_ref, w2_sem)
            load.start()
            load.wait()

        @pl.when(ps == 0)
        def _():
            h, _ = _rms(x_ref[...], g_ref[...])
            hb = h.astype(BF16)
            hb_ref[...] = hb
            hb_all_ref[rows, :] = hb
            a_ref[0], b_ref[0], part = _swiglu_slab(hb, wg_ref[...], wu_ref[...], wd_ref[...])
            p_ref[0] = x_ref[...] + FFN_RES * part

        @pl.when(ps == 1)
        def _():
            a_ref[0], b_ref[0], part = _swiglu_slab(hb_all_ref[rows, :], w2_ref[0:FF_SH, :],
                                                    w2_ref[FF_SH:2 * FF_SH, :], w2_ref[2 * FF_SH:3 * FF_SH, :])
            p_ref[0] = FFN_RES * part

        @pl.when(jnp.logical_and(ps == 1, i == nt - 1))
        def _():
            for j in (0, 2):
                landed(j)
            for j in (0, 2):
                sibling_landed(j)
            for cp in [first[2]] + passed:
                cp.wait_send()

    one = pl.Buffered(1)
    wspecs = [pl.BlockSpec((FF_SH, D_MODEL), functools.partial(lambda kk, ps, i: (kk, 0), k), pipeline_mode=one)
              for k in range(3)]
    hbm = pl.BlockSpec(memory_space=pl.ANY)
    first_pass_tiles = pl.BlockSpec((tile, D_MODEL), lambda ps, i: (jnp.where(ps == 0, i, nt - 1), 0))
    by_pass = lambda w: pl.BlockSpec((1, tile, w), lambda ps, i: (ps, i, 0))
    return pl.pallas_call(
        body, name=name, grid=(2, nt),
        in_specs=[first_pass_tiles, pl.BlockSpec((1, D_MODEL), lambda ps, i: (0, 0))] + wspecs + [hbm],
        out_specs=[first_pass_tiles, by_pass(FF_SH), by_pass(FF_SH), by_pass(D_MODEL), hbm],
        out_shape=[jax.ShapeDtypeStruct((s, D_MODEL), BF16), jax.ShapeDtypeStruct((N_CHIPS, s, FF_SH), F32),
                   jax.ShapeDtypeStruct((N_CHIPS, s, FF_SH), F32), jax.ShapeDtypeStruct((2, s, D_MODEL), F32),
                   jax.ShapeDtypeStruct(gather.shape, gather.dtype)],
        input_output_aliases={5: 4},
        scratch_shapes=[pltpu.VMEM((PACK_A_ROWS, D_MODEL), BF16), pltpu.VMEM((s, D_MODEL), BF16),
                        pltpu.SemaphoreType.DMA((6,)), pltpu.SemaphoreType.DMA((6,)), pltpu.SemaphoreType.DMA],
        compiler_params=_cparams(("arbitrary", "arbitrary")),
    )(x, g, own, own, own, gather)


def _ffn1_others(place, hb, p_own, a_all, b_all, pack, gather, *, tile, name):
    s = hb.shape[0]
    nt = s // tile
    forward_at = max(nt - 6, 0)

    def body(place_ref, hb_ref, p_ref, *rest):
        w_refs = rest[:6]
        o_ref, a_ref, b_ref, gat_ref, send_sems, recv_sems = rest[9:]
        i = pl.program_id(0)
        start, forward, finish, _ = _gather_stages(gat_ref, send_sems, recv_sems)
        pl.when(i == 0)(start)
        hb = hb_ref[...]
        a_ref[0], b_ref[0], part2 = _swiglu_slab(hb, w_refs[0][0], w_refs[1][0], w_refs[2][0])
        a_ref[1], b_ref[1], part3 = _swiglu_slab(hb, w_refs[3][0], w_refs[4][0], w_refs[5][0])
        o_ref[...] = (p_ref[0] + p_ref[1]) + FFN_RES * (part2 + part3)
        pl.when(i == forward_at)(forward)
        pl.when(i == nt - 1)(finish)

    one = pl.Buffered(1)

    def wspec(t, kk):
        return pl.BlockSpec((1, FF_SH, D_MODEL), lambda i, pr: (jnp.bitwise_xor(pr[0], t + 2), kk, 0),
                            pipeline_mode=one)

    rows = lambda w: pl.BlockSpec((tile, w), lambda i, pr: (i, 0))
    ab = pl.BlockSpec((2, tile, FF_SH), lambda i, pr: (1, i, 0))
    ab_shape = jax.ShapeDtypeStruct((N_CHIPS, s, FF_SH), F32)
    hbm = pl.BlockSpec(memory_space=pl.ANY)
    grid_spec = pltpu.PrefetchScalarGridSpec(
        num_scalar_prefetch=1, grid=(nt,),
        in_specs=[rows(D_MODEL), pl.BlockSpec((2, tile, D_MODEL), lambda i, pr: (0, i, 0))]
                 + [wspec(t, kk) for t in range(2) for kk in range(3)] + [hbm, hbm, hbm],
        out_specs=[rows(D_MODEL), ab, ab, hbm],
        scratch_shapes=[pltpu.SemaphoreType.DMA((6,)), pltpu.SemaphoreType.DMA((6,))])
    return pl.pallas_call(
        body, name=name, grid_spec=grid_spec,
        out_shape=[jax.ShapeDtypeStruct(hb.shape, F32), ab_shape, ab_shape,
                   jax.ShapeDtypeStruct(gather.shape, gather.dtype)],
        input_output_aliases={9: 1, 10: 2, 11: 3},
        compiler_params=_cparams(("arbitrary",)),
    )(place, hb, p_own, *([pack] * 6), a_all, b_all, gather)


def _ffn_fwd_loss(x, g, pack, k0, gf, tgt, *, tile, name):
    s = x.shape[0]

    def body(x_ref, g_ref, wg_ref, wu_ref, wd_ref, gf_ref, t_ref, dx_ref, loss_ref, dgf_ref, hb_ref, a_ref, b_ref,
             do_ref):
        @pl.when(pl.program_id(0) == 0)
        def _():
            loss_ref[...] = jnp.zeros_like(loss_ref)
            dgf_ref[...] = jnp.zeros_like(dgf_ref)

        x3 = _ffn_tile(x_ref[...], g_ref[...], wg_ref, wu_ref, wd_ref, hb_ref, a_ref, b_ref)
        gf_v = gf_ref[...]
        out, r = _rms(x3, gf_v)
        diff = out - t_ref[...]
        part = jnp.sum(jnp.sum(diff * diff, axis=-1, keepdims=True), axis=0, keepdims=True)
        loss_ref[...] += jnp.broadcast_to(part * (0.5 / D_MODEL), loss_ref.shape)
        dx, dg = _rms_bwd(diff * (1.0 / D_MODEL), x3, r, gf_v)
        dx_ref[...] = dx
        do_ref[...] = (FFN_RES * dx).astype(BF16)
        dgf_ref[...] += dg

    saved_specs, saved_shapes = _ffn_saved_specs(s, tile)
    return pl.pallas_call(
        body, name=name, grid=(s // tile,),
        in_specs=[_rows(tile, D_MODEL), _const((1, D_MODEL))] + _ffn_weight_specs(k0)
                 + [_const((1, D_MODEL)), _rows(tile, D_MODEL)],
        out_specs=[_rows(tile, D_MODEL), _const((1, 128)), _const((1, D_MODEL))] + saved_specs
                  + [_rows(tile, D_MODEL)],
        out_shape=[jax.ShapeDtypeStruct(x.shape, F32),
                   jax.ShapeDtypeStruct((1, 128), F32),
                   jax.ShapeDtypeStruct((1, D_MODEL), F32)] + saved_shapes
                  + [jax.ShapeDtypeStruct(x.shape, BF16)],
        compiler_params=_cparams(("arbitrary",)),
    )(x, g, pack, pack, pack, gf, tgt)


def _ffn_bwd(place, hb, a, b, do, pack, region, land, mix_grads, ab_by_pass=False, *, tile, name):
    s = hb.shape[0]
    nt = s // tile
    land_rows = pl.ds(region * FFN_HALF, FFN_HALF)
    mix_rows = pl.ds(2 * FFN_HALF, MIX_HALF)
    with_mix = mix_grads is not None
    with_land = land is not None
    n_others = 2 * N_CHIPS - 1

    def body(place_ref, hb_ref, a_ref, b_ref, do_ref, wg_ref, wu_ref, wd_ref, *rest):
        rest = list(rest)
        mix_ref = rest.pop(0) if with_mix else None
        if with_land:
            rest.pop(0)
        dhp_ref, land_ref, acc_ref, stage_ref, send_sems, recv_sem, local_sem = rest[:7]
        t, i = pl.program_id(0), pl.program_id(1)
        xi, yi, c = lax.axis_index("x"), lax.axis_index("y"), lax.axis_index("c")
        dev = 4 * xi + 2 * yi + c
        tt = (t + 1) % N_CHIPS
        tx, ty = jnp.bitwise_xor(xi, tt // 2), jnp.bitwise_xor(yi, tt % 2)

        def remote(src, dst, ssem, rsem, to):
            return pltpu.make_async_remote_copy(src_ref=src, dst_ref=dst, send_sem=ssem, recv_sem=rsem,
                                                device_id=to, device_id_type=MESH)

        def stage_half(h):
            return stage_ref.at[pl.ds(pl.multiple_of(h * FFN_HALF, 16), FFN_HALF)]

        if with_mix:
            mix_send, mix_recv, mix_local = rest[7:10]

            @pl.when(jnp.logical_and(t == 0, i == 0))
            def _():
                for chip in range(N_CHIPS):
                    for h in range(2):
                        src = mix_ref.at[chip, pl.ds(h * MIX_HALF, MIX_HALF)]
                        dst = land_ref.at[dev, mix_rows]
                        mine = jnp.logical_and(2 * xi + yi == chip, c == h)

                        @pl.when(mine)
                        def _():
                            pltpu.make_async_copy(src, dst, mix_local).start()

                        @pl.when(jnp.logical_not(mine))
                        def _():
                            remote(src, dst, mix_send, mix_recv, (chip // 2, chip % 2, h)).start()

        @pl.when(i == 0)
        def _():
            acc_ref[...] = jnp.zeros_like(acc_ref)

        hb = hb_ref[...]
        dob = do_ref[...]
        wg_j, wu_j, wd_j = wg_ref[0], wu_ref[0], wd_ref[0]
        a = a_ref[0]
        b = b_ref[0]
        sg = jax.nn.sigmoid(a)
        sa = a * sg
        fb = (sa * b).astype(BF16)
        df = _dot_nt(dob, wd_j)
        dbb = (df * sa).astype(BF16)
        dab = (df * b * (sg + sa * (1.0 - sg))).astype(BF16)
        dhp_ref[0] = (_dot(dab, wg_j) + _dot(dbb, wu_j)).astype(BF16)
        acc_ref[0:FF_SH, :] += _dot_tn(dab, hb)
        acc_ref[FF_SH:2 * FF_SH, :] += _dot_tn(dbb, hb)
        acc_ref[2 * FF_SH:3 * FF_SH, :] += _dot_tn(fb, dob)

        @pl.when(i == nt - 1)
        def _():
            dst = land_ref.at[dev, land_rows]

            @pl.when(t > 0)
            def _():
                for h in range(2):
                    remote(stage_half(h), dst, send_sems.at[h], recv_sem, (tx, ty, h)).wait_send()

            def cast_rows(r, carry):
                rows = pl.ds(pl.multiple_of(r * MIX_HALF, 16), MIX_HALF)
                stage_ref[rows, :] = acc_ref[rows, :].astype(BF16)
                return carry

            lax.fori_loop(0, 3 * FF_SH // MIX_HALF, cast_rows, 0)

            @pl.when(t < N_CHIPS - 1)
            def _():
                for h in range(2):
                    remote(stage_half(h), dst, send_sems.at[h], recv_sem, (tx, ty, h)).start()

            @pl.when(t == N_CHIPS - 1)
            def _():
                own = pltpu.make_async_copy(stage_half(c), dst, local_sem)
                own.start()
                sib = remote(stage_half(1 - c), dst, send_sems.at[0], recv_sem, (xi, yi, 1 - c))
                sib.start()
                sib.wait_send()
                own.wait()
                arrivals = land_ref.at[pl.ds(0, n_others), land_rows]
                remote(arrivals, arrivals, send_sems.at[0], recv_sem, (xi, yi, 1 - c)).wait_recv()
                if with_mix:
                    seven = land_ref.at[pl.ds(0, n_others), mix_rows]
                    both = remote(seven, seven, mix_send, mix_recv, (xi, yi, 1 - c))
                    both.wait_send()
                    both.wait_recv()
                    pltpu.make_async_copy(mix_ref.at[0, pl.ds(0, MIX_HALF)], land_ref.at[dev, mix_rows],
                                          mix_local).wait()

    def wspec(kk):
        return pl.BlockSpec((1, FF_SH, D_MODEL),
                            lambda t, i, pr: (jnp.bitwise_xor(pr[0], (t + 1) % N_CHIPS), kk, 0))

    xspec = pl.BlockSpec((tile, D_MODEL), lambda t, i, pr: (i, 0))
    if ab_by_pass:
        abspec = pl.BlockSpec((1, tile, FF_SH), lambda t, i, pr: ((t + 1) % N_CHIPS, i, 0))
    else:
        abspec = pl.BlockSpec((1, tile, FF_SH), lambda t, i, pr: (jnp.bitwise_xor(pr[0], (t + 1) % N_CHIPS), i, 0))
    hbm = pl.BlockSpec(memory_space=pl.ANY)
    in_specs = [xspec, abspec, abspec, xspec, wspec(0), wspec(1), wspec(2)]
    operands = [place, hb, a, b, do, pack, pack, pack]
    scratch = [pltpu.VMEM((3 * FF_SH, D_MODEL), F32), pltpu.VMEM((3 * FF_SH, D_MODEL), BF16),
               pltpu.SemaphoreType.DMA((2,)), pltpu.SemaphoreType.DMA, pltpu.SemaphoreType.DMA]
    if with_mix:
        in_specs.append(hbm)
        operands.append(mix_grads)
        scratch += [pltpu.SemaphoreType.DMA, pltpu.SemaphoreType.DMA, pltpu.SemaphoreType.DMA]
    aliases = {}
    if with_land:
        in_specs.append(hbm)
        operands.append(land)
        aliases = {len(operands) - 1: 1}
    grid_spec = pltpu.PrefetchScalarGridSpec(
        num_scalar_prefetch=1, grid=(N_CHIPS, nt), in_specs=in_specs,
        out_specs=[pl.BlockSpec((1, tile, D_MODEL), lambda t, i, pr: (t, i, 0)), hbm],
        scratch_shapes=scratch)
    return pl.pallas_call(
        body, name=name, grid_spec=grid_spec,
        out_shape=[jax.ShapeDtypeStruct((N_CHIPS, s, D_MODEL), BF16),
                   jax.ShapeDtypeStruct((2 * N_CHIPS, HALF_ROWS, D_MODEL), BF16)],
        input_output_aliases=aliases,
        compiler_params=_cparams(("arbitrary", "arbitrary")),
    )(*operands)


def _mix_grads_pack(dw_in_t, dw_out, *, name):
    def body(a_ref, b_ref, o_ref):
        o_ref[0, 0:IN_SH, :] = a_ref[0].astype(BF16)
        o_ref[0, IN_SH:FF_SH, :] = b_ref[0].astype(BF16)

    return pl.pallas_call(
        body, name=name, grid=(N_CHIPS,),
        in_specs=[pl.BlockSpec((1, IN_SH, D_MODEL), lambda j: (j, 0, 0)),
                  pl.BlockSpec((1, OUT_SH, D_MODEL), lambda j: (j, 0, 0))],
        out_specs=pl.BlockSpec((1, FF_SH, D_MODEL), lambda j: (j, 0, 0)),
        out_shape=jax.ShapeDtypeStruct((N_CHIPS, FF_SH, D_MODEL), BF16),
        compiler_params=_cparams(("arbitrary",)),
    )(dw_in_t.reshape(N_CHIPS, IN_SH, D_MODEL), dw_out.reshape(N_CHIPS, OUT_SH, D_MODEL))


def _share_stages(o_ref, send_sems, recv_sems):
    x, y, c, _ = _mesh_place()

    def rows(k, core):
        if k < 2:
            return o_ref.at[pl.ds(pl.multiple_of(k * 2 * FFN_HALF + core * FFN_HALF, 8), FFN_HALF)]
        return o_ref.at[pl.ds(pl.multiple_of(4 * FFN_HALF + core * MIX_HALF, 8), MIX_HALF)]

    def copy(k, core):
        return pltpu.make_async_remote_copy(src_ref=rows(k, core), dst_ref=rows(k, core), send_sem=send_sems.at[k],
                                            recv_sem=recv_sems.at[k], device_id=(x, y, 1 - c), device_id_type=MESH)

    sends = [copy(k, c) for k in range(3)]

    def start():
        for cp in sends:
            cp.start()

    def finish():
        for k in range(3):
            copy(k, 1 - c).wait_recv()
        for cp in sends:
            cp.wait_send()

    return start, finish


def _norm_bwd(dhp, x, dy, g, *, tile, name):
    s = x.shape[0]

    def body(dhp_ref, x_ref, dy_ref, g_ref, dx_ref, dg_ref):
        @pl.when(pl.program_id(0) == 0)
        def _():
            dg_ref[...] = jnp.zeros_like(dg_ref)

        dh = ((dhp_ref[0].astype(F32) + dhp_ref[1].astype(F32))
              + (dhp_ref[2].astype(F32) + dhp_ref[3].astype(F32)))
        x_v = x_ref[...]
        r = lax.rsqrt(jnp.mean(x_v * x_v, axis=-1, keepdims=True) + EPS)
        dx, dg = _rms_bwd(dh, x_v, r, g_ref[...])
        dx_ref[...] = dy_ref[...] + dx
        dg_ref[...] += dg

    return pl.pallas_call(
        body, name=name, grid=(s // tile,),
        in_specs=[pl.BlockSpec((N_CHIPS, tile, D_MODEL), lambda i: (0, i, 0)),
                  _rows(tile, D_MODEL), _rows(tile, D_MODEL), _const((1, D_MODEL))],
        out_specs=[_rows(tile, D_MODEL), _const((1, D_MODEL))],
        out_shape=[jax.ShapeDtypeStruct(x.shape, F32), jax.ShapeDtypeStruct((1, D_MODEL), F32)],
        compiler_params=_cparams(("arbitrary",)),
    )(dhp, x, dy, g)


def _mix_in_bwd(x, dy, dq, dk, dv, dz, g, w_in_t, *, tile, name):
    s = x.shape[0]

    def body(x_ref, dy_ref, dq_ref, dk_ref, dv_ref, dz_ref, g_ref, w_ref, dx_ref, dw_ref, db_ref, dg_ref, do_ref):
        @pl.when(pl.program_id(0) == 0)
        def _():
            dw_ref[...] = jnp.zeros_like(dw_ref)
            db_ref[...] = jnp.zeros_like(db_ref)
            dg_ref[...] = jnp.zeros_like(dg_ref)

        dproj = jnp.concatenate([dq_ref[...], dk_ref[...], dv_ref[...], dz_ref[...]], axis=-1)
        db_ref[...] += jnp.sum(dproj, axis=0, keepdims=True)
        dpb = dproj.astype(BF16)
        x_v = x_ref[...]
        g_v = g_ref[...]
        h, r = _rms(x_v, g_v)
        dw_ref[...] += _dot_tn(dpb, h.astype(BF16))
        dh = _dot(dpb, w_ref[...])
        dxn, dg = _rms_bwd(dh, x_v, r, g_v)
        dx = dy_ref[...] + dxn
        dx_ref[...] = dx
        do_ref[...] = (FFN_RES * dx).astype(BF16)
        dg_ref[...] += dg

    return pl.pallas_call(
        body, name=name, grid=(s // tile,),
        in_specs=[_rows(tile, D_MODEL), _rows(tile, D_MODEL), _rows(tile, ATTN_W), _rows(tile, KV_W),
                  _rows(tile, KV_W), _rows(tile, 2 * GMLP_W), _const((1, D_MODEL)), _const((IN_W, D_MODEL))],
        out_specs=[_rows(tile, D_MODEL), _const((IN_W, D_MODEL)), _const((1, IN_W)), _const((1, D_MODEL)),
                   _rows(tile, D_MODEL)],
        out_shape=[jax.ShapeDtypeStruct(x.shape, F32), jax.ShapeDtypeStruct((IN_W, D_MODEL), F32),
                   jax.ShapeDtypeStruct((1, IN_W), F32), jax.ShapeDtypeStruct((1, D_MODEL), F32),
                   jax.ShapeDtypeStruct(x.shape, BF16)],
        compiler_params=_cparams(("arbitrary",)),
    )(x, dy, dq, dk, dv, dz, g, w_in_t)


_GELU_C = 0.7978845608028654
_GELU_A = 0.044715


def _gelu_tanh(x):
    x2 = x * x
    return jnp.tanh(_GELU_C * (x + _GELU_A * (x2 * x))), x2


def _band(ref, i):
    prev = jnp.maximum(i - 1, 0)
    return jnp.concatenate([ref[pl.ds(pl.multiple_of(prev * BLK, BLK), BLK), :],
                            ref[pl.ds(pl.multiple_of(i * BLK, BLK), BLK), :]], axis=0)


def _key_in_block():
    return lax.broadcasted_iota(jnp.int32, (BLK, BLK), 0) <= lax.broadcasted_iota(jnp.int32, (BLK, BLK), 1)


def _fold(band, own):
    return jnp.where(own, band[BLK:], band[:BLK])


def _unfold(a, own):
    zero = jnp.zeros_like(a)
    return jnp.concatenate([jnp.where(own, zero, a), jnp.where(own, a, zero)], axis=0).astype(BF16)


def _attn_fwd(q, kb, vb, i, sink_ref):
    own = _key_in_block()
    outs, saved = [], []
    for h in range(N_Q_HEADS):
        cols = slice((h // REP) * HEAD_DIM, (h // REP + 1) * HEAD_DIM)
        s2 = _dot_nt(kb[:, cols], q[:, h * HEAD_DIM:(h + 1) * HEAD_DIM])
        sc = jnp.where(own, s2[BLK:], jnp.where(i > 0, s2[:BLK], -jnp.inf))
        sink = sink_ref[h]
        m = jnp.maximum(jnp.max(sc, axis=0, keepdims=True), sink)
        p = jnp.exp(sc - m)
        es = jnp.exp(sink - m)
        inv = 1.0 / (jnp.sum(p, axis=0, keepdims=True) + es)
        pn = p * inv
        band = _unfold(pn, own)
        outs.append(_dot_tn(band, vb[:, cols]))
        saved.append((pn, band, es * inv))
    return jnp.concatenate(outs, axis=-1), saved


def _tril_mask():
    t = lax.broadcasted_iota(jnp.int32, (BLK, BLK), 0)
    s_ = lax.broadcasted_iota(jnp.int32, (BLK, BLK), 1)
    return s_ <= t


def _gmlp_fwd_parts(zg, lng, lnb, ws_ref, bs_full):
    th, zg2 = _gelu_tanh(zg)
    z = 0.5 * zg * (1.0 + th)
    u = z[:, :GMLP_W]
    zv = z[:, GMLP_W:]
    mu = jnp.mean(zv, axis=-1, keepdims=True)
    zc = zv - mu
    rstd = lax.rsqrt(jnp.mean(zc * zc, axis=-1, keepdims=True) + EPS)
    xh = zc * rstd
    vvb = (xh * lng + lnb).astype(BF16)
    tril = _tril_mask()
    wms, parts = [], []
    for gi in range(GMLP_GROUPS):
        wm = jnp.where(tril, ws_ref[gi], 0.0).astype(BF16)
        wms.append(wm)
        parts.append(_dot(wm, vvb[:, gi * GROUP_DIM:(gi + 1) * GROUP_DIM]))
    mixed = jnp.concatenate(parts, axis=-1) + bs_full
    gelu_grad = 0.5 * (1.0 + th) + 0.5 * zg * (1.0 - th * th) * (_GELU_C * (1.0 + 3.0 * _GELU_A * zg2))
    return u, xh, rstd, vvb, wms, mixed, gelu_grad


def _mixer_fwd(x1, g, w_in_t, b_in, sinks, lng, lnb, w_s, bs_full, gao, ggo, w_out, b_out, gather, *, name):
    s = x1.shape[0]
    nb = min(MIX_FWD_BLOCKS, s // BLK)
    step_rows = nb * BLK
    last = s // step_rows - 1

    def tile_of(i, lag):
        return jnp.clip(i - lag, 0, last)

    def body(sink_ref, xa_ref, xc_ref, g_ref, wi_ref, bi_ref, lng_ref, lnb_ref, ws_ref, bs_ref, gao_ref, ggo_ref,
             wo_ref, bo_ref, gin_ref, q_ref, k_ref, v_ref, z_ref, y_ref, o_ref, gat_ref, qs_ref, zs_ref, ys_ref,
             send_sems, recv_sems):
        i = pl.program_id(0)
        start, forward, finish, _ = _gather_stages(gat_ref, send_sems, recv_sems)

        @pl.when(i == 0)
        def _():
            for ref in (k_ref, v_ref, qs_ref, zs_ref, ys_ref):
                ref[...] = jnp.zeros_like(ref)
            start()

        slot_a, slot_b, slot_c = i % 2, (i + 1) % 2, i % 2

        o_ref[...] = xc_ref[...] + (_dot(ys_ref[slot_c], wo_ref[...]) + bo_ref[...])

        tile_b = tile_of(i, 1)
        for b in range(nb):
            blk = tile_b * nb + b
            rows = slice(b * BLK, (b + 1) * BLK)
            y_attn, _ = _attn_fwd(qs_ref[slot_b, rows, :], _band(k_ref, blk), _band(v_ref, blk), blk, sink_ref)
            u, _, _, _, _, mixed, _ = _gmlp_fwd_parts(zs_ref[slot_b, rows, :], lng_ref[...], lnb_ref[...], ws_ref,
                                                      bs_ref[...])
            ya, _ = _rms(y_attn, gao_ref[...])
            yg, _ = _rms(u * mixed, ggo_ref[...])
            y_blk = jnp.concatenate([ya, yg], axis=-1).astype(BF16)
            y_ref[rows, :] = y_blk
            ys_ref[slot_b, rows, :] = y_blk

        h, _ = _rms(xa_ref[...], g_ref[...])
        proj = _dot_nt(h.astype(BF16), wi_ref[...]) + bi_ref[...]
        q_t = (proj[:, :ATTN_W] * ATTN_SCALE).astype(BF16)
        z_t = proj[:, ATTN_W + 2 * KV_W:]
        here = pl.ds(pl.multiple_of(tile_of(i, 0) * step_rows, step_rows), step_rows)
        q_ref[...] = q_t
        z_ref[...] = z_t
        qs_ref[slot_a] = q_t
        zs_ref[slot_a] = z_t
        k_ref[here, :] = proj[:, ATTN_W:ATTN_W + KV_W].astype(BF16)
        v_ref[here, :] = proj[:, ATTN_W + KV_W:ATTN_W + 2 * KV_W].astype(BF16)
        pl.when(i == max(last - 3, 0))(forward)
        pl.when(i == last + 2)(finish)

    def lagged(width, lag):
        return pl.BlockSpec((step_rows, width), lambda i: (tile_of(i, lag), 0))

    return pl.pallas_call(
        body, name=name, grid=(last + 3,),
        in_specs=[pl.BlockSpec(memory_space=pltpu.SMEM),
                  lagged(D_MODEL, 0), lagged(D_MODEL, 2), _const((1, D_MODEL)), _const((IN_W, D_MODEL)),
                  _const((1, IN_W)), _const((1, GMLP_W)), _const((1, GMLP_W)), _const((GMLP_GROUPS, BLK, BLK)),
                  _const((BLK, GMLP_W)), _const((1, ATTN_W)), _const((1, GMLP_W)), _const((D_MODEL, D_MODEL)),
                  _const((1, D_MODEL)), pl.BlockSpec(memory_space=pl.ANY)],
        out_specs=[lagged(ATTN_W, 0), _const((s, KV_W)), _const((s, KV_W)), lagged(2 * GMLP_W, 0),
                   lagged(D_MODEL, 1), lagged(D_MODEL, 2), pl.BlockSpec(memory_space=pl.ANY)],
        out_shape=[jax.ShapeDtypeStruct((s, ATTN_W), BF16), jax.ShapeDtypeStruct((s, KV_W), BF16),
                   jax.ShapeDtypeStruct((s, KV_W), BF16), jax.ShapeDtypeStruct((s, 2 * GMLP_W), F32),
                   jax.ShapeDtypeStruct((s, D_MODEL), BF16), jax.ShapeDtypeStruct((s, D_MODEL), F32),
                   jax.ShapeDtypeStruct(gather.shape, gather.dtype)],
        input_output_aliases={14: 6},
        scratch_shapes=[pltpu.VMEM((2, step_rows, ATTN_W), BF16), pltpu.VMEM((2, step_rows, 2 * GMLP_W), F32),
                        pltpu.VMEM((2, step_rows, D_MODEL), BF16),
                        pltpu.SemaphoreType.DMA((6,)), pltpu.SemaphoreType.DMA((6,))],
        compiler_params=_cparams(("arbitrary",)),
    )(sinks, x1, x1, g, w_in_t, b_in, lng, lnb, w_s, bs_full, gao, ggo, w_out, b_out, gather)


def _norm_bwd_mix_out(dhp, x, dy, g, yb, w_out, *, tile, name):
    s = x.shape[0]

    def body(dhp_ref, x_ref, dy_ref, g_ref, y_ref, w_ref, dx_ref, dg_ref, dyy_ref, dw_ref, db_ref):
        @pl.when(pl.program_id(0) == 0)
        def _():
            dg_ref[...] = jnp.zeros_like(dg_ref)
            dw_ref[...] = jnp.zeros_like(dw_ref)
            db_ref[...] = jnp.zeros_like(db_ref)

        dh = ((dhp_ref[0].astype(F32) + dhp_ref[1].astype(F32))
              + (dhp_ref[2].astype(F32) + dhp_ref[3].astype(F32)))
        x_v = x_ref[...]
        r = lax.rsqrt(jnp.mean(x_v * x_v, axis=-1, keepdims=True) + EPS)
        dxn, dg = _rms_bwd(dh, x_v, r, g_ref[...])
        dx = dy_ref[...] + dxn
        dx_ref[...] = dx
        dg_ref[...] += dg
        dxb = dx.astype(BF16)
        db_ref[...] += jnp.sum(dx, axis=0, keepdims=True)
        dw_ref[...] += _dot_tn(y_ref[...], dxb)
        dyy_ref[...] = _dot_nt(dxb, w_ref[...])

    return pl.pallas_call(
        body, name=name, grid=(s // tile,),
        in_specs=[pl.BlockSpec((N_CHIPS, tile, D_MODEL), lambda i: (0, i, 0)),
                  _rows(tile, D_MODEL), _rows(tile, D_MODEL), _const((1, D_MODEL)), _rows(tile, D_MODEL),
                  _const((D_MODEL, D_MODEL))],
        out_specs=[_rows(tile, D_MODEL), _const((1, D_MODEL)), _rows(tile, D_MODEL), _const((D_MODEL, D_MODEL)),
                   _const((1, D_MODEL))],
        out_shape=[jax.ShapeDtypeStruct(x.shape, F32), jax.ShapeDtypeStruct((1, D_MODEL), F32),
                   jax.ShapeDtypeStruct(x.shape, F32), jax.ShapeDtypeStruct((D_MODEL, D_MODEL), F32),
                   jax.ShapeDtypeStruct((1, D_MODEL), F32)],
        compiler_params=_cparams(("arbitrary",)),
    )(dhp, x, dy, g, yb, w_out)


def _mix_core_bwd(dyy, q, k, v, zg, sinks, lng, lnb, w_s, bs_full, gao, ggo, *, name):
    s = dyy.shape[0]
    nb = min(MIX_BWD_BLOCKS, s // BLK)
    nsteps = s // (nb * BLK)

    def body(*refs):
        accumulators = refs[13:15] + refs[16:]

        @pl.when(pl.program_id(0) == 0)
        def _():
            for ref in accumulators:
                ref[...] = jnp.zeros_like(ref)

        for b in range(nb):
            one_block(pl.program_id(0) * nb + b, slice(b * BLK, (b + 1) * BLK), *refs)

        @pl.when(pl.program_id(0) == nsteps - 1)
        def _():
            tril = _tril_mask()
            for gi in range(GMLP_GROUPS):
                refs[20][gi] = jnp.where(tril, refs[20][gi], 0.0)

    def one_block(i, rows, sink_ref, dyy_ref, q_ref, k_ref, v_ref, z_ref, lng_ref, lnb_ref, ws_ref, bs_ref, gao_ref,
                  ggo_ref, dq_ref, dk_ref, dv_ref, dz_ref, dgao_ref, dggo_ref, dlng_ref, dlnb_ref, dws_ref, dms_ref,
                  dsk_ref):
        q_v = q_ref[rows, :]
        kb = _band(k_ref, i)
        vb = _band(v_ref, i)
        lng_v = lng_ref[...]
        gao_v = gao_ref[...]
        ggo_v = ggo_ref[...]

        y_attn, probs = _attn_fwd(q_v, kb, vb, i, sink_ref)
        u, xh, rstd, vvb, wms, mixed, gelu_grad = _gmlp_fwd_parts(z_ref[rows, :], lng_v, lnb_ref[...], ws_ref,
                                                                  bs_ref[...])
        y_gmlp = u * mixed
        ra = lax.rsqrt(jnp.mean(y_attn * y_attn, axis=-1, keepdims=True) + EPS)
        rg = lax.rsqrt(jnp.mean(y_gmlp * y_gmlp, axis=-1, keepdims=True) + EPS)

        dyy = dyy_ref[rows, :]
        d_attn, dgao = _rms_bwd(dyy[:, :ATTN_W], y_attn, ra, gao_v)
        d_gmlp, dggo = _rms_bwd(dyy[:, ATTN_W:], y_gmlp, rg, ggo_v)
        dgao_ref[...] += dgao
        dggo_ref[...] += dggo

        du = d_gmlp * mixed
        dmixed = d_gmlp * u
        dms_ref[...] += dmixed
        dmb = dmixed.astype(BF16)
        dvv_parts = []
        for gi in range(GMLP_GROUPS):
            sl = slice(gi * GROUP_DIM, (gi + 1) * GROUP_DIM)
            dws_ref[gi] += _dot_nt(dmb[:, sl], vvb[:, sl])
            dvv_parts.append(_dot_tn(wms[gi], dmb[:, sl]))
        dvv = jnp.concatenate(dvv_parts, axis=-1)
        dlng_ref[...] += jnp.sum(dvv * xh, axis=0, keepdims=True)
        dlnb_ref[...] += jnp.sum(dvv, axis=0, keepdims=True)
        dxh = dvv * lng_v
        dzv = rstd * (dxh - jnp.mean(dxh, axis=-1, keepdims=True)
                      - xh * jnp.mean(dxh * xh, axis=-1, keepdims=True))
        dz_ref[rows, :] = jnp.concatenate([du, dzv], axis=-1) * gelu_grad

        dab = d_attn.astype(BF16)
        own = _key_in_block()
        dq_parts = []
        dk_parts = []
        dv_parts = []
        for gi in range(N_KV_HEADS):
            cols = slice(gi * HEAD_DIM, (gi + 1) * HEAD_DIM)
            kg, vg = kb[:, cols], vb[:, cols]
            dkg = jnp.zeros((2 * BLK, HEAD_DIM), F32)
            dvg = jnp.zeros((2 * BLK, HEAD_DIM), F32)
            for rr in range(REP):
                h = gi * REP + rr
                hs = slice(h * HEAD_DIM, (h + 1) * HEAD_DIM)
                qh, doh = q_v[:, hs], dab[:, hs]
                pn, band, psink = probs[h]
                dp = _fold(_dot_nt(vg, doh), own)
                delta = jnp.sum(pn * dp, axis=0, keepdims=True)
                ds2 = _unfold(pn * (dp - delta), own)
                dsink = jnp.sum(-psink * delta, axis=-1, keepdims=True)
                dsk_ref[pl.ds(h, 1), :] += jnp.broadcast_to(dsink, (1, 128))
                dq_parts.append(_dot_tn(ds2, kg) * ATTN_SCALE)
                dkg = dkg + _dot(ds2, qh)
                dvg = dvg + _dot(band, doh)
            dk_parts.append(dkg)
            dv_parts.append(dvg)
        dq_ref[rows, :] = jnp.concatenate(dq_parts, axis=-1)
        dkb = jnp.concatenate(dk_parts, axis=-1)
        dvb = jnp.concatenate(dv_parts, axis=-1)
        prev = pl.ds(pl.multiple_of(jnp.maximum(i - 1, 0) * BLK, BLK), BLK)
        cur = pl.ds(pl.multiple_of(i * BLK, BLK), BLK)
        dk_ref[prev, :] += dkb[:BLK]
        dv_ref[prev, :] += dvb[:BLK]
        dk_ref[cur, :] += dkb[BLK:]
        dv_ref[cur, :] += dvb[BLK:]

    return pl.pallas_call(
        body, name=name, grid=(nsteps,),
        in_specs=[pl.BlockSpec(memory_space=pltpu.SMEM),
                  _rows(nb * BLK, D_MODEL), _rows(nb * BLK, ATTN_W), _const((s, KV_W)), _const((s, KV_W)),
                  _rows(nb * BLK, 2 * GMLP_W), _const((1, GMLP_W)), _const((1, GMLP_W)),
                  _const((GMLP_GROUPS, BLK, BLK)), _const((BLK, GMLP_W)), _const((1, ATTN_W)), _const((1, GMLP_W))],
        out_specs=[_rows(nb * BLK, ATTN_W), _const((s, KV_W)), _const((s, KV_W)), _rows(nb * BLK, 2 * GMLP_W),
                   _const((1, ATTN_W)), _const((1, GMLP_W)),
                   _const((1, GMLP_W)), _const((1, GMLP_W)), _const((GMLP_GROUPS, BLK, BLK)),
                   _const((BLK, GMLP_W)), _const((N_Q_HEADS, 128))],
        out_shape=[jax.ShapeDtypeStruct((s, ATTN_W), F32), jax.ShapeDtypeStruct((s, KV_W), F32),
                   jax.ShapeDtypeStruct((s, KV_W), F32), jax.ShapeDtypeStruct((s, 2 * GMLP_W), F32),
                   jax.ShapeDtypeStruct((1, ATTN_W), F32), jax.ShapeDtypeStruct((1, GMLP_W), F32),
                   jax.ShapeDtypeStruct((1, GMLP_W), F32), jax.ShapeDtypeStruct((1, GMLP_W), F32),
                   jax.ShapeDtypeStruct((GMLP_GROUPS, BLK, BLK), F32), jax.ShapeDtypeStruct((BLK, GMLP_W), F32),
                   jax.ShapeDtypeStruct((N_Q_HEADS, 128), F32)],
        compiler_params=_cparams(("arbitrary",)),
    )(sinks, dyy, q, k, v, zg, lng, lnb, w_s, bs_full, gao, ggo)


def _local_step(place, x, tgt, p, own_a, pack_a, pack_b, pack_m, *, tile=512, fwd_tile=256, bwd_tile=512,
                norm_tile=512):
    g = {}
    tile, fwd_tile, bwd_tile, norm_tile = (min(t_, x.shape[0]) for t_ in (tile, fwd_tile, bwd_tile, norm_tile))
    hb1, a1, b1, part1, pack_a = _ffn1_own(x, p["ffn1_norm_g"], own_a, pack_a, tile=tile, name="ffn1_own")
    x1, a1, b1, pack_m = _ffn1_others(place, hb1, part1, a1, b1, pack_a, pack_m, tile=tile, name="ffn1_fwd")
    w_in_t = pack_m[:, :IN_SH, :].reshape(IN_W, D_MODEL)
    w_out = pack_m[:, IN_SH:, :].reshape(D_MODEL, D_MODEL)
    q, k, v, zg, yb, x2, pack_b = _mixer_fwd(
        x1, p["mix_norm_g"], w_in_t, p["b_in"], p["attn_sinks"], p["gmlp_ln_g"], p["gmlp_ln_b"], p["gmlp_w_s"],
        p["bs_full"], p["attn_out_norm_g"], p["gmlp_out_norm_g"], w_out, p["b_out"], pack_b, name="mixer_fwd")
    mix_args = (q, k, v, zg, p["attn_sinks"], p["gmlp_ln_g"], p["gmlp_ln_b"], p["gmlp_w_s"], p["bs_full"],
                p["attn_out_norm_g"], p["gmlp_out_norm_g"])
    dx3, loss, g["final_norm_g"], hb2, a2, b2, do3 = _ffn_fwd_loss(
        x2, p["ffn2_norm_g"], pack_b, 0, p["final_norm_g"], tgt, tile=fwd_tile, name="ffn2_fwd_loss")

    dhp, land = _ffn_bwd(place, hb2, a2, b2, do3, pack_b, 1, None, None, tile=bwd_tile, name="ffn2_bwd")
    dx2, g["ffn2_norm_g"], dyy, dw_out, g["b_out"] = _norm_bwd_mix_out(
        dhp, x2, dx3, p["ffn2_norm_g"], yb, w_out, tile=norm_tile, name="ffn2_norm_bwd")

    (dq, dk, dv, dz, g["attn_out_norm_g"], g["gmlp_out_norm_g"], g["gmlp_ln_g"],
     g["gmlp_ln_b"], g["gmlp_w_s"], dmix_sum, dsinks) = _mix_core_bwd(dyy, *mix_args, name="mix_core_bwd")
    g["gmlp_b_s"] = dmix_sum
    g["attn_sinks"] = dsinks
    dx1, dw_in_t, g["b_in"], g["mix_norm_g"], do1 = _mix_in_bwd(
        x1, dx2, dq, dk, dv, dz, p["mix_norm_g"], w_in_t, tile=tile, name="mix_in_bwd")
    mix_grads = _mix_grads_pack(dw_in_t, dw_out, name="mix_grads_pack")

    dhp1, land = _ffn_bwd(place, hb1, a1, b1, do1, pack_a, 0, land, mix_grads, True, tile=bwd_tile, name="ffn1_bwd")
    dx0, g["ffn1_norm_g"] = _norm_bwd(dhp1, x, dx1, p["ffn1_norm_g"], tile=norm_tile, name="ffn1_norm_bwd")
    return loss, dx0, land, g


def _pack_cast(place, parts, *, name):
    def body(place_ref, *refs):
        oa_ref, ob_ref, om_ref, own_ref = refs[-4:]
        off = 0
        for k, (ref, rows) in enumerate(zip(refs[:-4], BIG_ROWS)):
            if k in (3, 6):
                off = 0
            cast = ref[...].astype(BF16)
            (oa_ref if k < 3 else ob_ref if k < 6 else om_ref)[0, off:off + rows, :] = cast
            if k < 3:
                own_ref[off:off + rows, :] = cast
            off += rows

    one = pl.Buffered(1)

    def slab(rows):
        return pl.BlockSpec((1, rows, D_MODEL), lambda i, pr: (pr[0], 0, 0), pipeline_mode=one)

    grid_spec = pltpu.PrefetchScalarGridSpec(
        num_scalar_prefetch=1, grid=(1,),
        in_specs=[pl.BlockSpec((rows, D_MODEL), lambda i, pr: (0, 0), pipeline_mode=one) for rows in BIG_ROWS],
        out_specs=[slab(PACK_A_ROWS), slab(PACK_B_ROWS), slab(PACK_M_ROWS),
                   pl.BlockSpec((PACK_A_ROWS, D_MODEL), lambda i, pr: (0, 0), pipeline_mode=one)])
    return pl.pallas_call(
        body, name=name, grid_spec=grid_spec,
        out_shape=[jax.ShapeDtypeStruct((N_CHIPS, PACK_A_ROWS, D_MODEL), BF16),
                   jax.ShapeDtypeStruct((N_CHIPS, PACK_B_ROWS, D_MODEL), BF16),
                   jax.ShapeDtypeStruct((N_CHIPS, PACK_M_ROWS, D_MODEL), BF16),
                   jax.ShapeDtypeStruct((PACK_A_ROWS, D_MODEL), BF16)],
        compiler_params=_cparams(("arbitrary",)),
    )(place, *parts)


def _shard_tile(i, c):
    return jnp.where(i < 3, 3 * c + i, jnp.where(i < 6, 3 + 3 * c + i, 12 + c))


def _rs_reduce(place, land, *, name):
    def body(place_ref, l_ref, o_ref):
        acc = l_ref[0].astype(F32)
        for d in range(1, 2 * N_CHIPS):
            acc = acc + l_ref[d].astype(F32)
        o_ref[...] = acc

    grid_spec = pltpu.PrefetchScalarGridSpec(
        num_scalar_prefetch=1, grid=(HALF_ROWS // MIX_HALF,),
        in_specs=[pl.BlockSpec((2 * N_CHIPS, MIX_HALF, D_MODEL), lambda i, pr: (0, i, 0))],
        out_specs=pl.BlockSpec((MIX_HALF, D_MODEL), lambda i, pr: (_shard_tile(i, pr[1]), 0)))
    return pl.pallas_call(
        body, name=name, grid_spec=grid_spec,
        out_shape=jax.ShapeDtypeStruct((PACK_ROWS, D_MODEL), F32),
        compiler_params=_cparams(("arbitrary",)),
    )(place, land)


def _small_all_reduce(packed, shard, *, name):
    rows = packed.shape[0]
    half = rows // 2

    def body(p_ref, sh_in_ref, o_ref, sh_ref, sib_ref, slots_ref, send_sems, recv_sems, share_send, share_recv):
        x, y, c, others = _mesh_place()
        me = 2 * x + y
        sibling = (x, y, 1 - c)
        share_start, share_finish = _share_stages(sh_ref, share_send, share_recv)
        share_start()

        def half_of(core):
            return pl.ds(pl.multiple_of(core * half, 8), half)

        def remote(k, src, dst, to):
            return pltpu.make_async_remote_copy(src_ref=src, dst_ref=dst, send_sem=send_sems.at[k],
                                                recv_sem=recv_sems.at[k], device_id=to, device_id_type=MESH)

        sib = remote(0, p_ref.at[half_of(1 - c)], sib_ref, sibling)
        sib.start()
        sib.wait()
        slots_ref[me] = p_ref[half_of(c), :] + sib_ref[...]
        sends = [remote(1 + j, slots_ref.at[me], slots_ref.at[me], (px, py, c)) for j, (px, py) in enumerate(others)]
        for cp in sends:
            cp.start()
        for j, (px, py) in enumerate(others):
            slab = slots_ref.at[2 * px + py]
            remote(1 + j, slab, slab, (px, py, c)).wait_recv()
        for cp in sends:
            cp.wait_send()
        o_ref[half_of(c), :] = (slots_ref[0] + slots_ref[1]) + (slots_ref[2] + slots_ref[3])
        back = remote(4, o_ref.at[half_of(c)], o_ref.at[half_of(c)], sibling)
        back.start()
        remote(4, o_ref.at[half_of(1 - c)], o_ref.at[half_of(1 - c)], sibling).wait_recv()
        back.wait_send()
        share_finish()

    vm = pl.BlockSpec(memory_space=pltpu.VMEM)
    hbm = pl.BlockSpec(memory_space=pl.ANY)
    return pl.pallas_call(
        body, name=name, in_specs=[vm, hbm], out_specs=[vm, hbm],
        out_shape=[jax.ShapeDtypeStruct((rows, 128), F32), jax.ShapeDtypeStruct(shard.shape, shard.dtype)],
        input_output_aliases={1: 1},
        scratch_shapes=[pltpu.VMEM((half, 128), F32), pltpu.VMEM((N_CHIPS, half, 128), F32),
                        pltpu.SemaphoreType.DMA((5,)), pltpu.SemaphoreType.DMA((5,)),
                        pltpu.SemaphoreType.DMA((3,)), pltpu.SemaphoreType.DMA((3,))],
    )(packed, shard)


def _adamw(w, g, m, v, *, g_row0, tile, name):
    rows, cols = w.shape
    assert g_row0 % tile == 0 and rows % tile == 0

    def body(w_ref, g_ref, m_ref, v_ref, go_ref, d_ref, nm_ref, nv_ref):
        g_v = g_ref[...]
        m_n = ADAM_B1 * m_ref[...] + (1.0 - ADAM_B1) * g_v
        v_n = ADAM_B2 * v_ref[...] + (1.0 - ADAM_B2) * (g_v * g_v)
        m_hat = m_n / (1.0 - ADAM_B1 ** ADAM_STEP)
        v_hat = v_n / (1.0 - ADAM_B2 ** ADAM_STEP)
        d_ref[...] = -ADAM_LR * (m_hat / (jnp.sqrt(v_hat) + ADAM_EPS) + ADAM_WD * w_ref[...])
        go_ref[...] = g_v
        nm_ref[...] = m_n
        nv_ref[...] = v_n

    spec = pl.BlockSpec((tile, cols), lambda i: (i, 0))
    gspec = pl.BlockSpec((tile, cols), lambda i: (g_row0 // tile + i, 0))
    shape = jax.ShapeDtypeStruct((rows, cols), F32)
    return pl.pallas_call(
        body, name=name, grid=(rows // tile,),
        in_specs=[spec, gspec, spec, spec], out_specs=[spec] * 4, out_shape=[shape] * 4,
        compiler_params=_cparams(("arbitrary",)),
    )(w, g, m, v)


def kernel(x, ffn1_norm_g, ffn1_w_gate, ffn1_w_up, ffn1_w_down, mix_norm_g, w_in, b_in, attn_sinks, gmlp_ln_g, gmlp_ln_b, gmlp_w_s, gmlp_b_s, attn_out_norm_g, gmlp_out_norm_g, w_out, b_out, ffn2_norm_g, ffn2_w_gate, ffn2_w_up, ffn2_w_down, final_norm_g, loss_target, m_ffn1_norm_g, m_ffn1_w_gate, m_ffn1_w_up, m_ffn1_w_down, m_mix_norm_g, m_w_in, m_b_in, m_attn_sinks, m_gmlp_ln_g, m_gmlp_ln_b, m_gmlp_w_s, m_gmlp_b_s, m_attn_out_norm_g, m_gmlp_out_norm_g, m_w_out, m_b_out, m_ffn2_norm_g, m_ffn2_w_gate, m_ffn2_w_up, m_ffn2_w_down, m_final_norm_g, v_ffn1_norm_g, v_ffn1_w_gate, v_ffn1_w_up, v_ffn1_w_down, v_mix_norm_g, v_w_in, v_b_in, v_attn_sinks, v_gmlp_ln_g, v_gmlp_ln_b, v_gmlp_w_s, v_gmlp_b_s, v_attn_out_norm_g, v_gmlp_out_norm_g, v_w_out, v_b_out, v_ffn2_norm_g, v_ffn2_w_gate, v_ffn2_w_up, v_ffn2_w_down, v_final_norm_g):
    f_args = dict(locals())
    weights = {n: f_args[n] for n in [nm for nm, _ in SMALL if nm != "loss"] + list(BIG)}
    shapes = {n: weights[n].shape for n in weights}
    shapes["loss"] = ()
    place = jnp.stack([2 * lax.axis_index("x") + lax.axis_index("y"), lax.axis_index("c")]).astype(jnp.int32)

    def with_cols(name, a):
        a2 = a.reshape(a.shape[-2], a.shape[-1])
        return a2.T if BIG_TRANSPOSED[BIG.index(name)] else a2

    def natural(name, a2):
        return (a2.T if BIG_TRANSPOSED[BIG.index(name)] else a2).reshape(shapes[name])

    pack_a, pack_b, pack_m, own_a = _pack_cast(place, [with_cols(n, weights[n]) for n in BIG], name="pack_cast")
    p = {n: weights[n].reshape(1, -1) for n in ("ffn1_norm_g", "mix_norm_g", "b_in", "gmlp_ln_g", "gmlp_ln_b",
                                                "attn_out_norm_g", "gmlp_out_norm_g", "b_out", "ffn2_norm_g",
                                                "final_norm_g")}
    p["attn_sinks"] = attn_sinks.reshape(N_Q_HEADS)
    p["gmlp_w_s"] = gmlp_w_s.reshape(GMLP_GROUPS, BLK, BLK)
    p["bs_full"] = jnp.broadcast_to(gmlp_b_s.reshape(GMLP_GROUPS, BLK).T[:, :, None],
                                    (BLK, GMLP_GROUPS, GROUP_DIM)).reshape(BLK, GMLP_W)

    loss_part, dx0, land, gs = _local_step(place, x[0], loss_target[0], p, own_a, pack_a, pack_b, pack_m)

    gs["gmlp_b_s"] = jnp.sum(gs["gmlp_b_s"].reshape(BLK, GMLP_GROUPS, GROUP_DIM), axis=-1).T
    gs["attn_sinks"] = gs["attn_sinks"][:, 0]
    gs["loss"] = loss_part[0, 0]
    small_sum, shard = _small_all_reduce(_pack_small(gs), _rs_reduce(place, land, name="rs_reduce"),
                                         name="small_all_reduce")

    grad_w, delta, new_m, new_v = {}, {}, {}, {}
    off = 0
    for n, rows in zip(BIG, BIG_ROWS):
        res = _adamw(with_cols(n, weights[n]), shard, with_cols(n, f_args["m_" + n]), with_cols(n, f_args["v_" + n]),
                     g_row0=off, tile=FF_SH // 2 if rows == FF_SH else 64, name="adamw_" + n)
        grad_w[n], delta[n], new_m[n], new_v[n] = [natural(n, a) for a in res]
        off += rows
    sm = {k: {n: f_args[k + n] for n, _ in SMALL if n != "loss"} for k in ("", "m_", "v_")}
    for k in sm:
        sm[k]["loss"] = jnp.zeros((), F32)
    res = _adamw(_pack_small(sm[""]), small_sum, _pack_small(sm["m_"]), _pack_small(sm["v_"]),
                 g_row0=0, tile=SMALL_ROWS, name="adamw_small")
    small = _unpack_small(res[0], shapes)
    for dst, packed in ((grad_w, res[0]), (delta, res[1]), (new_m, res[2]), (new_v, res[3])):
        dst.update({n: a for n, a in _unpack_small(packed, shapes).items() if n != "loss"})

    order = ('ffn1_norm_g', 'ffn1_w_gate', 'ffn1_w_up', 'ffn1_w_down', 'mix_norm_g', 'w_in', 'b_in', 'attn_sinks',
             'gmlp_ln_g', 'gmlp_ln_b', 'gmlp_w_s', 'gmlp_b_s', 'attn_out_norm_g', 'gmlp_out_norm_g', 'w_out', 'b_out',
             'ffn2_norm_g', 'ffn2_w_gate', 'ffn2_w_up', 'ffn2_w_down', 'final_norm_g')
    return (small["loss"], dx0.reshape(x.shape), *[grad_w[n] for n in order], *[delta[n] for n in order],
            *[new_m[n] for n in order], *[new_v[n] for n in order])
```

```python
import functools

import jax
import jax.numpy as jnp
from jax import lax
from jax.experimental import pallas as pl
from jax.experimental.pallas import tpu as pltpu

F32 = jnp.float32
BF16 = jnp.bfloat16

D_MODEL = 1024
D_FF = 2816
N_CHIPS = 4
FF_SH = D_FF // N_CHIPS
N_Q_HEADS = 8
N_KV_HEADS = 2
REP = N_Q_HEADS // N_KV_HEADS
HEAD_DIM = 64
ATTN_W = 512
KV_W = 128
GMLP_W = 512
GMLP_GROUPS = 8
GROUP_DIM = 64
BLK = 128
MIX_FWD_BLOCKS = 2
MIX_BWD_BLOCKS = 4
IN_W = 1792
IN_SH = IN_W // N_CHIPS
OUT_SH = D_MODEL // N_CHIPS
EPS = 1e-6
FFN_RES = 0.5
ATTN_SCALE = HEAD_DIM ** -0.5

ADAM_LR = 0.001
ADAM_B1 = 0.9
ADAM_B2 = 0.999
ADAM_EPS = 1e-08
ADAM_WD = 0.01
ADAM_STEP = 10

V7X_VMEM_LIMIT = 56 * 1024 * 1024
MESH = pl.DeviceIdType.MESH


def _cparams(sem):
    return pltpu.CompilerParams(dimension_semantics=sem, vmem_limit_bytes=V7X_VMEM_LIMIT)


def _dot(a, b):
    return jnp.dot(a, b, preferred_element_type=F32)


def _dot_nt(a, b):
    return lax.dot_general(a, b, (((1,), (1,)), ((), ())), preferred_element_type=F32)


def _dot_tn(a, b):
    return lax.dot_general(a, b, (((0,), (0,)), ((), ())), preferred_element_type=F32)


def _rms(x, g):
    r = lax.rsqrt(jnp.mean(x * x, axis=-1, keepdims=True) + EPS)
    return x * r * g, r


def _rms_bwd(dh, x, r, g):
    gy = dh * g
    dx = r * gy - x * (r * r * r) * jnp.mean(gy * x, axis=-1, keepdims=True)
    dg = jnp.sum(dh * x * r, axis=0, keepdims=True)
    return dx, dg


def _const(shape):
    nd = len(shape)
    return pl.BlockSpec(shape, lambda *_: (0,) * nd)


def _rows(t, w):
    return pl.BlockSpec((t, w), lambda i: (i, 0))


PACK_ROWS = 7 * FF_SH
HALF_ROWS = PACK_ROWS // 2
FFN_HALF = 3 * FF_SH // 2
MIX_HALF = FF_SH // 2
PACK_A_ROWS = 3 * FF_SH
PACK_B_ROWS = 3 * FF_SH
PACK_M_ROWS = FF_SH
BIG = ("ffn1_w_gate", "ffn1_w_up", "ffn1_w_down", "ffn2_w_gate", "ffn2_w_up", "ffn2_w_down", "w_in", "w_out")
BIG_ROWS = (FF_SH, FF_SH, FF_SH, FF_SH, FF_SH, FF_SH, IN_SH, OUT_SH)
BIG_TRANSPOSED = (True, True, False, True, True, False, True, False)

SMALL = (("ffn1_norm_g", 1024), ("mix_norm_g", 1024), ("b_in", 1792), ("attn_sinks", 8), ("gmlp_ln_g", 512),
         ("gmlp_ln_b", 512), ("gmlp_w_s", 131072), ("gmlp_b_s", 1024), ("attn_out_norm_g", 512),
         ("gmlp_out_norm_g", 512), ("b_out", 1024), ("ffn2_norm_g", 1024), ("final_norm_g", 1024), ("loss", 1))


def _small_rows(n):
    return -(-n // 1024) * 8


SMALL_USED_ROWS = sum(_small_rows(n) for _, n in SMALL)
SMALL_ROWS = -(-SMALL_USED_ROWS // 16) * 16


def _pack_small(parts):
    out = []
    for name, n in SMALL:
        flat = parts[name].reshape(-1).astype(F32)
        rows = _small_rows(n)
        out.append(jnp.pad(flat, (0, rows * 128 - n)).reshape(rows, 128))
    if SMALL_ROWS > SMALL_USED_ROWS:
        out.append(jnp.zeros((SMALL_ROWS - SMALL_USED_ROWS, 128), F32))
    return jnp.concatenate(out, axis=0)


def _unpack_small(packed, shapes):
    res, off = {}, 0
    for name, n in SMALL:
        rows = _small_rows(n)
        res[name] = packed[off:off + rows].reshape(-1)[:n].reshape(shapes[name])
        off += rows
    return res


def _ffn_tile(x, g, wg_ref, wu_ref, wd_ref, hb_ref, a_ref, b_ref):
    h, _ = _rms(x, g)
    hb = h.astype(BF16)
    hb_ref[...] = hb
    acc = jnp.zeros(x.shape, F32)
    for j in range(N_CHIPS):
        a = _dot_nt(hb, wg_ref[j])
        b = _dot_nt(hb, wu_ref[j])
        a_ref[j] = a
        b_ref[j] = b
        f = (a * jax.nn.sigmoid(a) * b).astype(BF16)
        acc = acc + _dot(f, wd_ref[j])
    return x + FFN_RES * acc


def _ffn_saved_specs(s, tile):
    ab = pl.BlockSpec((N_CHIPS, tile, FF_SH), lambda i: (0, i, 0))
    shape = jax.ShapeDtypeStruct((N_CHIPS, s, FF_SH), F32)
    return [_rows(tile, D_MODEL), ab, ab], [jax.ShapeDtypeStruct((s, D_MODEL), BF16), shape, shape]


def _ffn_weight_specs(k0):
    one = pl.Buffered(1)
    return [pl.BlockSpec((N_CHIPS, FF_SH, D_MODEL), functools.partial(lambda kk, i: (0, kk, 0), k0 + d),
                         pipeline_mode=one) for d in range(3)]


def _mesh_place():
    x, y, c = lax.axis_index("x"), lax.axis_index("y"), lax.axis_index("c")
    others = [(1 - x, y), (x, 1 - y), (1 - x, 1 - y)]
    return x, y, c, others


def _gather_stages(o_ref, send_sems, recv_sems):
    x, y, c, others = _mesh_place()
    me = 2 * x + y
    sibling = (x, y, 1 - c)
    half_rows = o_ref.shape[1] // 2

    def half(slab, core):
        return o_ref.at[slab, pl.ds(pl.multiple_of(core * half_rows, 16), half_rows)]

    def copy(k, rows, to):
        return pltpu.make_async_remote_copy(src_ref=rows, dst_ref=rows, send_sem=send_sems.at[k],
                                            recv_sem=recv_sems.at[k], device_id=to, device_id_type=MESH)

    first = [copy(j, half(me, c), (px, py, c)) for j, (px, py) in enumerate(others)]
    passed = [copy(3 + j, half(2 * px + py, c), sibling) for j, (px, py) in enumerate(others)]

    def landed(j):
        px, py = others[j]
        copy(j, half(2 * px + py, c), (px, py, c)).wait_recv()
        passed[j].start()

    def sibling_landed(j):
        px, py = others[j]
        copy(3 + j, half(2 * px + py, 1 - c), sibling).wait_recv()

    def start():
        for cp in first:
            cp.start()

    def forward():
        for j in range(len(others)):
            landed(j)

    def finish():
        for j in range(len(others)):
            sibling_landed(j)
        for cp in first + passed:
            cp.wait_send()

    return start, forward, finish, (first, passed, landed, sibling_landed)


def _swiglu_slab(hb, wg, wu, wd):
    a = _dot_nt(hb, wg)
    b = _dot_nt(hb, wu)
    return a, b, _dot((a * jax.nn.sigmoid(a) * b).astype(BF16), wd)


def _ffn1_own(x, g, own, gather, *, tile, name):
    s = x.shape[0]
    nt = s // tile
    x_neighbour, y_neighbour, two_hop = 0, 1, 2
    x_starts_at = (2 * nt) // 3
    two_hop_starts_at = nt // 3
    x_passed_at = nt // 2

    def body(x_ref, g_ref, wg_ref, wu_ref, wd_ref, gin_ref, hb_ref, a_ref, b_ref, p_ref, gat_ref, w2_ref, hb_all_ref,
             send_sems, recv_sems, w2_sem):
        ps, i = pl.program_id(0), pl.program_id(1)
        rows = pl.ds(pl.multiple_of(i * tile, tile), tile)
        xi, yi, _, _ = _mesh_place()
        _, _, _, (first, passed, landed, sibling_landed) = _gather_stages(gat_ref, send_sems, recv_sems)

        @pl.when(jnp.logical_and(ps == 0, i == 0))
        def _():
            first[y_neighbour].start()

        @pl.when(jnp.logical_and(ps == 0, i == x_starts_at))
        def _():
            first[x_neighbour].start()

        @pl.when(jnp.logical_and(ps == 1, i == 0))
        def _():
            landed(y_neighbour)
            sibling_landed(y_neighbour)
            load = pltpu.make_async_copy(gat_ref.at[2 * xi + (1 - yi)], w2_ref, w2_sem)
            load.start()
            load.wait()

        @pl.when(jnp.logical_and(ps == 1, i == two_hop_starts_at))
        def _():
            first[two_hop].start()

        @pl.when(jnp.logical_and(ps == 1, i == x_passed_at))
        def _():
            landed(x_neighbour)

        @pl.when(ps == 0)
        def _():
            h, _ = _rms(x_ref[...], g_ref[...])
            hb = h.astype(BF16)
            hb_ref[...] = hb
            hb_all_ref[rows, :] = hb
            a_ref[0], b_ref[0], part = _swiglu_slab(hb, wg_ref[...], wu_ref[...], wd_ref[...])
            p_ref[0] = x_ref[...] + FFN_RES * part

        @pl.when(ps == 1)
        def _():
            a_ref[0], b_ref[0], part = _swiglu_slab(hb_all_ref[rows, :], w2_ref[0:FF_SH, :],
                                                    w2_ref[FF_SH:2 * FF_SH, :], w2_ref[2 * FF_SH:3 * FF_SH, :])
            p_ref[0] = FFN_RES * part

        @pl.when(jnp.logical_and(ps == 1, i == nt - 1))
        def _():
            landed(two_hop)
            for j in (x_neighbour, two_hop):
                sibling_landed(j)
            for cp in first + passed:
                cp.wait_send()

    one = pl.Buffered(1)
    wspecs = [pl.BlockSpec((FF_SH, D_MODEL), functools.partial(lambda kk, ps, i: (kk, 0), k), pipeline_mode=one)
              for k in range(3)]
    hbm = pl.BlockSpec(memory_space=pl.ANY)
    first_pass_tiles = pl.BlockSpec((tile, D_MODEL), lambda ps, i: (jnp.where(ps == 0, i, nt - 1), 0))
    by_pass = lambda w: pl.BlockSpec((1, tile, w), lambda ps, i: (ps, i, 0))
    return pl.pallas_call(
        body, name=name, grid=(2, nt),
        in_specs=[first_pass_tiles, pl.BlockSpec((1, D_MODEL), lambda ps, i: (0, 0))] + wspecs + [hbm],
        out_specs=[first_pass_tiles, by_pass(FF_SH), by_pass(FF_SH), by_pass(D_MODEL), hbm],
        out_shape=[jax.ShapeDtypeStruct((s, D_MODEL), BF16), jax.ShapeDtypeStruct((N_CHIPS, s, FF_SH), F32),
                   jax.ShapeDtypeStruct((N_CHIPS, s, FF_SH), F32), jax.ShapeDtypeStruct((2, s, D_MODEL), F32),
                   jax.ShapeDtypeStruct(gather.shape, gather.dtype)],
        input_output_aliases={5: 4},
        scratch_shapes=[pltpu.VMEM((PACK_A_ROWS, D_MODEL), BF16), pltpu.VMEM((s, D_MODEL), BF16),
                        pltpu.SemaphoreType.DMA((6,)), pltpu.SemaphoreType.DMA((6,)), pltpu.SemaphoreType.DMA],
        compiler_params=_cparams(("arbitrary", "arbitrary")),
    )(x, g, own, own, own, gather)


def _ffn1_others(place, hb, p_own, a_all, b_all, pack, gather, *, tile, name):
    s = hb.shape[0]
    nt = s // tile
    forward_at = max(nt - 6, 0)

    def body(place_ref, hb_ref, p_ref, *rest):
        w_refs = rest[:6]
        o_ref, a_ref, b_ref, gat_ref, send_sems, recv_sems = rest[9:]
        i = pl.program_id(0)
        start, forward, finish, _ = _gather_stages(gat_ref, send_sems, recv_sems)
        pl.when(i == 0)(start)
        hb = hb_ref[...]
        a_ref[0], b_ref[0], part2 = _swiglu_slab(hb, w_refs[0][0], w_refs[1][0], w_refs[2][0])
        a_ref[1], b_ref[1], part3 = _swiglu_slab(hb, w_refs[3][0], w_refs[4][0], w_refs[5][0])
        o_ref[...] = (p_ref[0] + p_ref[1]) + FFN_RES * (part2 + part3)
        pl.when(i == forward_at)(forward)
        pl.when(i == nt - 1)(finish)

    one = pl.Buffered(1)

    def wspec(t, kk):
        return pl.BlockSpec((1, FF_SH, D_MODEL), lambda i, pr: (jnp.bitwise_xor(pr[0], t + 2), kk, 0),
                            pipeline_mode=one)

    rows = lambda w: pl.BlockSpec((tile, w), lambda i, pr: (i, 0))
    ab = pl.BlockSpec((2, tile, FF_SH), lambda i, pr: (1, i, 0))
    ab_shape = jax.ShapeDtypeStruct((N_CHIPS, s, FF_SH), F32)
    hbm = pl.BlockSpec(memory_space=pl.ANY)
    grid_spec = pltpu.PrefetchScalarGridSpec(
        num_scalar_prefetch=1, grid=(nt,),
        in_specs=[rows(D_MODEL), pl.BlockSpec((2, tile, D_MODEL), lambda i, pr: (0, i, 0))]
                 + [wspec(t, kk) for t in range(2) for kk in range(3)] + [hbm, hbm, hbm],
        out_specs=[rows(D_MODEL), ab, ab, hbm],
        scratch_shapes=[pltpu.SemaphoreType.DMA((6,)), pltpu.SemaphoreType.DMA((6,))])
    return pl.pallas_call(
        body, name=name, grid_spec=grid_spec,
        out_shape=[jax.ShapeDtypeStruct(hb.shape, F32), ab_shape, ab_shape,
                   jax.ShapeDtypeStruct(gather.shape, gather.dtype)],
        input_output_aliases={9: 1, 10: 2, 11: 3},
        compiler_params=_cparams(("arbitrary",)),
    )(place, hb, p_own, *([pack] * 6), a_all, b_all, gather)


def _ffn_fwd_loss(x, g, pack, k0, gf, tgt, *, tile, name):
    s = x.shape[0]

    def body(x_ref, g_ref, wg_ref, wu_ref, wd_ref, gf_ref, t_ref, dx_ref, loss_ref, dgf_ref, hb_ref, a_ref, b_ref,
             do_ref):
        @pl.when(pl.program_id(0) == 0)
        def _():
            loss_ref[...] = jnp.zeros_like(loss_ref)
            dgf_ref[...] = jnp.zeros_like(dgf_ref)

        x3 = _ffn_tile(x_ref[...], g_ref[...], wg_ref, wu_ref, wd_ref, hb_ref, a_ref, b_ref)
        gf_v = gf_ref[...]
        out, r = _rms(x3, gf_v)
        diff = out - t_ref[...]
        part = jnp.sum(jnp.sum(diff * diff, axis=-1, keepdims=True), axis=0, keepdims=True)
        loss_ref[...] += jnp.broadcast_to(part * (0.5 / D_MODEL), loss_ref.shape)
        dx, dg = _rms_bwd(diff * (1.0 / D_MODEL), x3, r, gf_v)
        dx_ref[...] = dx
        do_ref[...] = (FFN_RES * dx).astype(BF16)
        dgf_ref[...] += dg

    saved_specs, saved_shapes = _ffn_saved_specs(s, tile)
    return pl.pallas_call(
        body, name=name, grid=(s // tile,),
        in_specs=[_rows(tile, D_MODEL), _const((1, D_MODEL))] + _ffn_weight_specs(k0)
                 + [_const((1, D_MODEL)), _rows(tile, D_MODEL)],
        out_specs=[_rows(tile, D_MODEL), _const((1, 128)), _const((1, D_MODEL))] + saved_specs
                  + [_rows(tile, D_MODEL)],
        out_shape=[jax.ShapeDtypeStruct(x.shape, F32),
                   jax.ShapeDtypeStruct((1, 128), F32),
                   jax.ShapeDtypeStruct((1, D_MODEL), F32)] + saved_shapes
                  + [jax.ShapeDtypeStruct(x.shape, BF16)],
        compiler_params=_cparams(("arbitrary",)),
    )(x, g, pack, pack, pack, gf, tgt)


def _ffn_bwd(place, hb, a, b, do, pack, region, land, mix_grads, ab_by_pass=False, *, tile, name):
    s = hb.shape[0]
    nt = s // tile
    land_rows = pl.ds(region * FFN_HALF, FFN_HALF)
    mix_rows = pl.ds(2 * FFN_HALF, MIX_HALF)
    with_mix = mix_grads is not None
    with_land = land is not None
    n_others = 2 * N_CHIPS - 1

    def body(place_ref, hb_ref, a_ref, b_ref, do_ref, wg_ref, wu_ref, wd_ref, *rest):
        rest = list(rest)
        mix_ref = rest.pop(0) if with_mix else None
        if with_land:
            rest.pop(0)
        dhp_ref, land_ref, acc_ref, stage_ref, send_sems, recv_sem, local_sem = rest[:7]
        t, i = pl.program_id(0), pl.program_id(1)
        xi, yi, c = lax.axis_index("x"), lax.axis_index("y"), lax.axis_index("c")
        dev = 4 * xi + 2 * yi + c
        tt = (t + 1) % N_CHIPS
        tx, ty = jnp.bitwise_xor(xi, tt // 2), jnp.bitwise_xor(yi, tt % 2)

        def remote(src, dst, ssem, rsem, to):
            return pltpu.make_async_remote_copy(src_ref=src, dst_ref=dst, send_sem=ssem, recv_sem=rsem,
                                                device_id=to, device_id_type=MESH)

        def stage_half(h):
            return stage_ref.at[pl.ds(pl.multiple_of(h * FFN_HALF, 16), FFN_HALF)]

        if with_mix:
            mix_send, mix_recv, mix_local = rest[7:10]

            @pl.when(jnp.logical_and(t == 0, i == 0))
            def _():
                for chip in range(N_CHIPS):
                    for h in range(2):
                        src = mix_ref.at[chip, pl.ds(h * MIX_HALF, MIX_HALF)]
                        dst = land_ref.at[dev, mix_rows]
                        mine = jnp.logical_and(2 * xi + yi == chip, c == h)

                        @pl.when(mine)
                        def _():
                            pltpu.make_async_copy(src, dst, mix_local).start()

                        @pl.when(jnp.logical_not(mine))
                        def _():
                            remote(src, dst, mix_send, mix_recv, (chip // 2, chip % 2, h)).start()

        @pl.when(i == 0)
        def _():
            acc_ref[...] = jnp.zeros_like(acc_ref)

        hb = hb_ref[...]
        dob = do_ref[...]
        wg_j, wu_j, wd_j = wg_ref[0], wu_ref[0], wd_ref[0]
        a = a_ref[0]
        b = b_ref[0]
        sg = jax.nn.sigmoid(a)
        sa = a * sg
        fb = (sa * b).astype(BF16)
        df = _dot_nt(dob, wd_j)
        dbb = (df * sa).astype(BF16)
        dab = (df * b * (sg + sa * (1.0 - sg))).astype(BF16)
        dhp_ref[0] = (_dot(dab, wg_j) + _dot(dbb, wu_j)).astype(BF16)
        acc_ref[0:FF_SH, :] += _dot_tn(dab, hb)
        acc_ref[FF_SH:2 * FF_SH, :] += _dot_tn(dbb, hb)
        acc_ref[2 * FF_SH:3 * FF_SH, :] += _dot_tn(fb, dob)

        @pl.when(i == nt - 1)
        def _():
            dst = land_ref.at[dev, land_rows]

            @pl.when(t > 0)
            def _():
                for h in range(2):
                    remote(stage_half(h), dst, send_sems.at[h], recv_sem, (tx, ty, h)).wait_send()

            def cast_rows(r, carry):
                rows = pl.ds(pl.multiple_of(r * MIX_HALF, 16), MIX_HALF)
                stage_ref[rows, :] = acc_ref[rows, :].astype(BF16)
                return carry

            lax.fori_loop(0, 3 * FF_SH // MIX_HALF, cast_rows, 0)

            @pl.when(t < N_CHIPS - 1)
            def _():
                for h in range(2):
                    remote(stage_half(h), dst, send_sems.at[h], recv_sem, (tx, ty, h)).start()

            @pl.when(t == N_CHIPS - 1)
            def _():
                own = pltpu.make_async_copy(stage_half(c), dst, local_sem)
                own.start()
                sib = remote(stage_half(1 - c), dst, send_sems.at[0], recv_sem, (xi, yi, 1 - c))
                sib.start()
                sib.wait_send()
                own.wait()
                arrivals = land_ref.at[pl.ds(0, n_others), land_rows]
                remote(arrivals, arrivals, send_sems.at[0], recv_sem, (xi, yi, 1 - c)).wait_recv()
                if with_mix:
                    seven = land_ref.at[pl.ds(0, n_others), mix_rows]
                    both = remote(seven, seven, mix_send, mix_recv, (xi, yi, 1 - c))
                    both.wait_send()
                    both.wait_recv()
                    pltpu.make_async_copy(mix_ref.at[0, pl.ds(0, MIX_HALF)], land_ref.at[dev, mix_rows],
                                          mix_local).wait()

    def wspec(kk):
        return pl.BlockSpec((1, FF_SH, D_MODEL),
                            lambda t, i, pr: (jnp.bitwise_xor(pr[0], (t + 1) % N_CHIPS), kk, 0))

    xspec = pl.BlockSpec((tile, D_MODEL), lambda t, i, pr: (i, 0))
    if ab_by_pass:
        abspec = pl.BlockSpec((1, tile, FF_SH), lambda t, i, pr: ((t + 1) % N_CHIPS, i, 0))
    else:
        abspec = pl.BlockSpec((1, tile, FF_SH), lambda t, i, pr: (jnp.bitwise_xor(pr[0], (t + 1) % N_CHIPS), i, 0))
    hbm = pl.BlockSpec(memory_space=pl.ANY)
    in_specs = [xspec, abspec, abspec, xspec, wspec(0), wspec(1), wspec(2)]
    operands = [place, hb, a, b, do, pack, pack, pack]
    scratch = [pltpu.VMEM((3 * FF_SH, D_MODEL), F32), pltpu.VMEM((3 * FF_SH, D_MODEL), BF16),
               pltpu.SemaphoreType.DMA((2,)), pltpu.SemaphoreType.DMA, pltpu.SemaphoreType.DMA]
    if with_mix:
        in_specs.append(hbm)
        operands.append(mix_grads)
        scratch += [pltpu.SemaphoreType.DMA, pltpu.SemaphoreType.DMA, pltpu.SemaphoreType.DMA]
    aliases = {}
    if with_land:
        in_specs.append(hbm)
        operands.append(land)
        aliases = {len(operands) - 1: 1}
    grid_spec = pltpu.PrefetchScalarGridSpec(
        num_scalar_prefetch=1, grid=(N_CHIPS, nt), in_specs=in_specs,
        out_specs=[pl.BlockSpec((1, tile, D_MODEL), lambda t, i, pr: (t, i, 0)), hbm],
        scratch_shapes=scratch)
    return pl.pallas_call(
        body, name=name, grid_spec=grid_spec,
        out_shape=[jax.ShapeDtypeStruct((N_CHIPS, s, D_MODEL), BF16),
                   jax.ShapeDtypeStruct((2 * N_CHIPS, HALF_ROWS, D_MODEL), BF16)],
        input_output_aliases=aliases,
        compiler_params=_cparams(("arbitrary", "arbitrary")),
    )(*operands)


def _mix_grads_pack(dw_in_t, dw_out, *, name):
    def body(a_ref, b_ref, o_ref):
        o_ref[0, 0:IN_SH, :] = a_ref[0].astype(BF16)
        o_ref[0, IN_SH:FF_SH, :] = b_ref[0].astype(BF16)

    return pl.pallas_call(
        body, name=name, grid=(N_CHIPS,),
        in_specs=[pl.BlockSpec((1, IN_SH, D_MODEL), lambda j: (j, 0, 0)),
                  pl.BlockSpec((1, OUT_SH, D_MODEL), lambda j: (j, 0, 0))],
        out_specs=pl.BlockSpec((1, FF_SH, D_MODEL), lambda j: (j, 0, 0)),
        out_shape=jax.ShapeDtypeStruct((N_CHIPS, FF_SH, D_MODEL), BF16),
        compiler_params=_cparams(("arbitrary",)),
    )(dw_in_t.reshape(N_CHIPS, IN_SH, D_MODEL), dw_out.reshape(N_CHIPS, OUT_SH, D_MODEL))


def _share_stages(o_ref, send_sems, recv_sems):
    x, y, c, _ = _mesh_place()

    def rows(k, core):
        if k < 2:
            return o_ref.at[pl.ds(pl.multiple_of(k * 2 * FFN_HALF + core * FFN_HALF, 8), FFN_HALF)]
        return o_ref.at[pl.ds(pl.multiple_of(4 * FFN_HALF + core * MIX_HALF, 8), MIX_HALF)]

    def copy(k, core):
        return pltpu.make_async_remote_copy(src_ref=rows(k, core), dst_ref=rows(k, core), send_sem=send_sems.at[k],
                                            recv_sem=recv_sems.at[k], device_id=(x, y, 1 - c), device_id_type=MESH)

    sends = [copy(k, c) for k in range(3)]

    def start():
        for cp in sends:
            cp.start()

    def finish():
        for k in range(3):
            copy(k, 1 - c).wait_recv()
        for cp in sends:
            cp.wait_send()

    return start, finish


def _norm_bwd(dhp, x, dy, g, *, tile, name):
    s = x.shape[0]

    def body(dhp_ref, x_ref, dy_ref, g_ref, dx_ref, dg_ref):
        @pl.when(pl.program_id(0) == 0)
        def _():
            dg_ref[...] = jnp.zeros_like(dg_ref)

        dh = ((dhp_ref[0].astype(F32) + dhp_ref[1].astype(F32))
              + (dhp_ref[2].astype(F32) + dhp_ref[3].astype(F32)))
        x_v = x_ref[...]
        r = lax.rsqrt(jnp.mean(x_v * x_v, axis=-1, keepdims=True) + EPS)
        dx, dg = _rms_bwd(dh, x_v, r, g_ref[...])
        dx_ref[...] = dy_ref[...] + dx
        dg_ref[...] += dg

    return pl.pallas_call(
        body, name=name, grid=(s // tile,),
        in_specs=[pl.BlockSpec((N_CHIPS, tile, D_MODEL), lambda i: (0, i, 0)),
                  _rows(tile, D_MODEL), _rows(tile, D_MODEL), _const((1, D_MODEL))],
        out_specs=[_rows(tile, D_MODEL), _const((1, D_MODEL))],
        out_shape=[jax.ShapeDtypeStruct(x.shape, F32), jax.ShapeDtypeStruct((1, D_MODEL), F32)],
        compiler_params=_cparams(("arbitrary",)),
    )(dhp, x, dy, g)


def _mix_in_bwd(x, dy, dq, dk, dv, dz, g, w_in_t, *, tile, name):
    s = x.shape[0]

    def body(x_ref, dy_ref, dq_ref, dk_ref, dv_ref, dz_ref, g_ref, w_ref, dx_ref, dw_ref, db_ref, dg_ref, do_ref):
        @pl.when(pl.program_id(0) == 0)
        def _():
            dw_ref[...] = jnp.zeros_like(dw_ref)
            db_ref[...] = jnp.zeros_like(db_ref)
            dg_ref[...] = jnp.zeros_like(dg_ref)

        dproj = jnp.concatenate([dq_ref[...], dk_ref[...], dv_ref[...], dz_ref[...]], axis=-1)
        db_ref[...] += jnp.sum(dproj, axis=0, keepdims=True)
        dpb = dproj.astype(BF16)
        x_v = x_ref[...]
        g_v = g_ref[...]
        h, r = _rms(x_v, g_v)
        dw_ref[...] += _dot_tn(dpb, h.astype(BF16))
        dh = _dot(dpb, w_ref[...])
        dxn, dg = _rms_bwd(dh, x_v, r, g_v)
        dx = dy_ref[...] + dxn
        dx_ref[...] = dx
        do_ref[...] = (FFN_RES * dx).astype(BF16)
        dg_ref[...] += dg

    return pl.pallas_call(
        body, name=name, grid=(s // tile,),
        in_specs=[_rows(tile, D_MODEL), _rows(tile, D_MODEL), _rows(tile, ATTN_W), _rows(tile, KV_W),
                  _rows(tile, KV_W), _rows(tile, 2 * GMLP_W), _const((1, D_MODEL)), _const((IN_W, D_MODEL))],
        out_specs=[_rows(tile, D_MODEL), _const((IN_W, D_MODEL)), _const((1, IN_W)), _const((1, D_MODEL)),
                   _rows(tile, D_MODEL)],
        out_shape=[jax.ShapeDtypeStruct(x.shape, F32), jax.ShapeDtypeStruct((IN_W, D_MODEL), F32),
                   jax.ShapeDtypeStruct((1, IN_W), F32), jax.ShapeDtypeStruct((1, D_MODEL), F32),
                   jax.ShapeDtypeStruct(x.shape, BF16)],
        compiler_params=_cparams(("arbitrary",)),
    )(x, dy, dq, dk, dv, dz, g, w_in_t)


_GELU_C = 0.7978845608028654
_GELU_A = 0.044715


def _gelu_tanh(x):
    x2 = x * x
    return jnp.tanh(_GELU_C * (x + _GELU_A * (x2 * x))), x2


def _band(ref, i):
    prev = jnp.maximum(i - 1, 0)
    return jnp.concatenate([ref[pl.ds(pl.multiple_of(prev * BLK, BLK), BLK), :],
                            ref[pl.ds(pl.multiple_of(i * BLK, BLK), BLK), :]], axis=0)


def _key_in_block():
    return lax.broadcasted_iota(jnp.int32, (BLK, BLK), 0) <= lax.broadcasted_iota(jnp.int32, (BLK, BLK), 1)


def _fold(band, own):
    return jnp.where(own, band[BLK:], band[:BLK])


def _unfold(a, own):
    zero = jnp.zeros_like(a)
    return jnp.concatenate([jnp.where(own, zero, a), jnp.where(own, a, zero)], axis=0).astype(BF16)


def _attn_fwd(q, kb, vb, i, sink_ref):
    own = _key_in_block()
    outs, saved = [], []
    for h in range(N_Q_HEADS):
        cols = slice((h // REP) * HEAD_DIM, (h // REP + 1) * HEAD_DIM)
        s2 = _dot_nt(kb[:, cols], q[:, h * HEAD_DIM:(h + 1) * HEAD_DIM])
        sc = jnp.where(own, s2[BLK:], jnp.where(i > 0, s2[:BLK], -jnp.inf))
        sink = sink_ref[h]
        m = jnp.maximum(jnp.max(sc, axis=0, keepdims=True), sink)
        p = jnp.exp(sc - m)
        es = jnp.exp(sink - m)
        inv = 1.0 / (jnp.sum(p, axis=0, keepdims=True) + es)
        pn = p * inv
        band = _unfold(pn, own)
        outs.append(_dot_tn(band, vb[:, cols]))
        saved.append((pn, band, es * inv))
    return jnp.concatenate(outs, axis=-1), saved


def _tril_mask():
    t = lax.broadcasted_iota(jnp.int32, (BLK, BLK), 0)
    s_ = lax.broadcasted_iota(jnp.int32, (BLK, BLK), 1)
    return s_ <= t


def _gmlp_fwd_parts(zg, lng, lnb, ws_ref, bs_full):
    th, zg2 = _gelu_tanh(zg)
    z = 0.5 * zg * (1.0 + th)
    u = z[:, :GMLP_W]
    zv = z[:, GMLP_W:]
    mu = jnp.mean(zv, axis=-1, keepdims=True)
    zc = zv - mu
    rstd = lax.rsqrt(jnp.mean(zc * zc, axis=-1, keepdims=True) + EPS)
    xh = zc * rstd
    vvb = (xh * lng + lnb).astype(BF16)
    tril = _tril_mask()
    wms, parts = [], []
    for gi in range(GMLP_GROUPS):
        wm = jnp.where(tril, ws_ref[gi], 0.0).astype(BF16)
        wms.append(wm)
        parts.append(_dot(wm, vvb[:, gi * GROUP_DIM:(gi + 1) * GROUP_DIM]))
    mixed = jnp.concatenate(parts, axis=-1) + bs_full
    gelu_grad = 0.5 * (1.0 + th) + 0.5 * zg * (1.0 - th * th) * (_GELU_C * (1.0 + 3.0 * _GELU_A * zg2))
    return u, xh, rstd, vvb, wms, mixed, gelu_grad


def _mixer_fwd(x1, g, w_in_t, b_in, sinks, lng, lnb, w_s, bs_full, gao, ggo, w_out, b_out, gather, *, name):
    s = x1.shape[0]
    nb = min(MIX_FWD_BLOCKS, s // BLK)
    step_rows = nb * BLK
    last = s // step_rows - 1

    def tile_of(i, lag):
        return jnp.clip(i - lag, 0, last)

    def body(sink_ref, xa_ref, xc_ref, g_ref, wi_ref, bi_ref, lng_ref, lnb_ref, ws_ref, bs_ref, gao_ref, ggo_ref,
             wo_ref, bo_ref, gin_ref, q_ref, k_ref, v_ref, z_ref, y_ref, o_ref, gat_ref, qs_ref, zs_ref, ys_ref,
             send_sems, recv_sems):
        i = pl.program_id(0)
        start, forward, finish, _ = _gather_stages(gat_ref, send_sems, recv_sems)

        @pl.when(i == 0)
        def _():
            for ref in (k_ref, v_ref, qs_ref, zs_ref, ys_ref):
                ref[...] = jnp.zeros_like(ref)
            start()

        slot_a, slot_b, slot_c = i % 2, (i + 1) % 2, i % 2

        o_ref[...] = xc_ref[...] + (_dot(ys_ref[slot_c], wo_ref[...]) + bo_ref[...])

        tile_b = tile_of(i, 1)
        for b in range(nb):
            blk = tile_b * nb + b
            rows = slice(b * BLK, (b + 1) * BLK)
            y_attn, _ = _attn_fwd(qs_ref[slot_b, rows, :], _band(k_ref, blk), _band(v_ref, blk), blk, sink_ref)
            u, _, _, _, _, mixed, _ = _gmlp_fwd_parts(zs_ref[slot_b, rows, :], lng_ref[...], lnb_ref[...], ws_ref,
                                                      bs_ref[...])
            ya, _ = _rms(y_attn, gao_ref[...])
            yg, _ = _rms(u * mixed, ggo_ref[...])
            y_blk = jnp.concatenate([ya, yg], axis=-1).astype(BF16)
            y_ref[rows, :] = y_blk
            ys_ref[slot_b, rows, :] = y_blk

        h, _ = _rms(xa_ref[...], g_ref[...])
        proj = _dot_nt(h.astype(BF16), wi_ref[...]) + bi_ref[...]
        q_t = (proj[:, :ATTN_W] * ATTN_SCALE).astype(BF16)
        z_t = proj[:, ATTN_W + 2 * KV_W:]
        here = pl.ds(pl.multiple_of(tile_of(i, 0) * step_rows, step_rows), step_rows)
        q_ref[...] = q_t
        z_ref[...] = z_t
        qs_ref[slot_a] = q_t
        zs_ref[slot_a] = z_t
        k_ref[here, :] = proj[:, ATTN_W:ATTN_W + KV_W].astype(BF16)
        v_ref[here, :] = proj[:, ATTN_W + KV_W:ATTN_W + 2 * KV_W].astype(BF16)
        pl.when(i == max(last - 3, 0))(forward)
        pl.when(i == last + 2)(finish)

    def lagged(width, lag):
        return pl.BlockSpec((step_rows, width), lambda i: (tile_of(i, lag), 0))

    return pl.pallas_call(
        body, name=name, grid=(last + 3,),
        in_specs=[pl.BlockSpec(memory_space=pltpu.SMEM),
                  lagged(D_MODEL, 0), lagged(D_MODEL, 2), _const((1, D_MODEL)), _const((IN_W, D_MODEL)),
                  _const((1, IN_W)), _const((1, GMLP_W)), _const((1, GMLP_W)), _const((GMLP_GROUPS, BLK, BLK)),
                  _const((BLK, GMLP_W)), _const((1, ATTN_W)), _const((1, GMLP_W)), _const((D_MODEL, D_MODEL)),
                  _const((1, D_MODEL)), pl.BlockSpec(memory_space=pl.ANY)],
        out_specs=[lagged(ATTN_W, 0), _const((s, KV_W)), _const((s, KV_W)), lagged(2 * GMLP_W, 0),
                   lagged(D_MODEL, 1), lagged(D_MODEL, 2), pl.BlockSpec(memory_space=pl.ANY)],
        out_shape=[jax.ShapeDtypeStruct((s, ATTN_W), BF16), jax.ShapeDtypeStruct((s, KV_W), BF16),
                   jax.ShapeDtypeStruct((s, KV_W), BF16), jax.ShapeDtypeStruct((s, 2 * GMLP_W), F32),
                   jax.ShapeDtypeStruct((s, D_MODEL), BF16), jax.ShapeDtypeStruct((s, D_MODEL), F32),
                   jax.ShapeDtypeStruct(gather.shape, gather.dtype)],
        input_output_aliases={14: 6},
        scratch_shapes=[pltpu.VMEM((2, step_rows, ATTN_W), BF16), pltpu.VMEM((2, step_rows, 2 * GMLP_W), F32),
                        pltpu.VMEM((2, step_rows, D_MODEL), BF16),
                        pltpu.SemaphoreType.DMA((6,)), pltpu.SemaphoreType.DMA((6,))],
        compiler_params=_cparams(("arbitrary",)),
    )(sinks, x1, x1, g, w_in_t, b_in, lng, lnb, w_s, bs_full, gao, ggo, w_out, b_out, gather)


def _norm_bwd_mix_out(dhp, x, dy, g, yb, w_out, *, tile, name):
    s = x.shape[0]

    def body(dhp_ref, x_ref, dy_ref, g_ref, y_ref, w_ref, dx_ref, dg_ref, dyy_ref, dw_ref, db_ref):
        @pl.when(pl.program_id(0) == 0)
        def _():
            dg_ref[...] = jnp.zeros_like(dg_ref)
            dw_ref[...] = jnp.zeros_like(dw_ref)
            db_ref[...] = jnp.zeros_like(db_ref)

        dh = ((dhp_ref[0].astype(F32) + dhp_ref[1].astype(F32))
              + (dhp_ref[2].astype(F32) + dhp_ref[3].astype(F32)))
        x_v = x_ref[...]
        r = lax.rsqrt(jnp.mean(x_v * x_v, axis=-1, keepdims=True) + EPS)
        dxn, dg = _rms_bwd(dh, x_v, r, g_ref[...])
        dx = dy_ref[...] + dxn
        dx_ref[...] = dx
        dg_ref[...] += dg
        dxb = dx.astype(BF16)
        db_ref[...] += jnp.sum(dx, axis=0, keepdims=True)
        dw_ref[...] += _dot_tn(y_ref[...], dxb)
        dyy_ref[...] = _dot_nt(dxb, w_ref[...])

    return pl.pallas_call(
        body, name=name, grid=(s // tile,),
        in_specs=[pl.BlockSpec((N_CHIPS, tile, D_MODEL), lambda i: (0, i, 0)),
                  _rows(tile, D_MODEL), _rows(tile, D_MODEL), _const((1, D_MODEL)), _rows(tile, D_MODEL),
                  _const((D_MODEL, D_MODEL))],
        out_specs=[_rows(tile, D_MODEL), _const((1, D_MODEL)), _rows(tile, D_MODEL), _const((D_MODEL, D_MODEL)),
                   _const((1, D_MODEL))],
        out_shape=[jax.ShapeDtypeStruct(x.shape, F32), jax.ShapeDtypeStruct((1, D_MODEL), F32),
                   jax.ShapeDtypeStruct(x.shape, F32), jax.ShapeDtypeStruct((D_MODEL, D_MODEL), F32),
                   jax.ShapeDtypeStruct((1, D_MODEL), F32)],
        compiler_params=_cparams(("arbitrary",)),
    )(dhp, x, dy, g, yb, w_out)


def _mix_core_bwd(dyy, q, k, v, zg, sinks, lng, lnb, w_s, bs_full, gao, ggo, *, name):
    s = dyy.shape[0]
    nb = min(MIX_BWD_BLOCKS, s // BLK)
    nsteps = s // (nb * BLK)

    def body(*refs):
        accumulators = refs[13:15] + refs[16:]

        @pl.when(pl.program_id(0) == 0)
        def _():
            for ref in accumulators:
                ref[...] = jnp.zeros_like(ref)

        for b in range(nb):
            one_block(pl.program_id(0) * nb + b, slice(b * BLK, (b + 1) * BLK), *refs)

        @pl.when(pl.program_id(0) == nsteps - 1)
        def _():
            tril = _tril_mask()
            for gi in range(GMLP_GROUPS):
                refs[20][gi] = jnp.where(tril, refs[20][gi], 0.0)

    def one_block(i, rows, sink_ref, dyy_ref, q_ref, k_ref, v_ref, z_ref, lng_ref, lnb_ref, ws_ref, bs_ref, gao_ref,
                  ggo_ref, dq_ref, dk_ref, dv_ref, dz_ref, dgao_ref, dggo_ref, dlng_ref, dlnb_ref, dws_ref, dms_ref,
                  dsk_ref):
        q_v = q_ref[rows, :]
        kb = _band(k_ref, i)
        vb = _band(v_ref, i)
        lng_v = lng_ref[...]
        gao_v = gao_ref[...]
        ggo_v = ggo_ref[...]

        y_attn, probs = _attn_fwd(q_v, kb, vb, i, sink_ref)
        u, xh, rstd, vvb, wms, mixed, gelu_grad = _gmlp_fwd_parts(z_ref[rows, :], lng_v, lnb_ref[...], ws_ref,
                                                                  bs_ref[...])
        y_gmlp = u * mixed
        ra = lax.rsqrt(jnp.mean(y_attn * y_attn, axis=-1, keepdims=True) + EPS)
        rg = lax.rsqrt(jnp.mean(y_gmlp * y_gmlp, axis=-1, keepdims=True) + EPS)

        dyy = dyy_ref[rows, :]
        d_attn, dgao = _rms_bwd(dyy[:, :ATTN_W], y_attn, ra, gao_v)
        d_gmlp, dggo = _rms_bwd(dyy[:, ATTN_W:], y_gmlp, rg, ggo_v)
        dgao_ref[...] += dgao
        dggo_ref[...] += dggo

        du = d_gmlp * mixed
        dmixed = d_gmlp * u
        dms_ref[...] += dmixed
        dmb = dmixed.astype(BF16)
        dvv_parts = []
        for gi in range(GMLP_GROUPS):
            sl = slice(gi * GROUP_DIM, (gi + 1) * GROUP_DIM)
            dws_ref[gi] += _dot_nt(dmb[:, sl], vvb[:, sl])
            dvv_parts.append(_dot_tn(wms[gi], dmb[:, sl]))
        dvv = jnp.concatenate(dvv_parts, axis=-1)
        dlng_ref[...] += jnp.sum(dvv * xh, axis=0, keepdims=True)
        dlnb_ref[...] += jnp.sum(dvv, axis=0, keepdims=True)
        dxh = dvv * lng_v
        dzv = rstd * (dxh - jnp.mean(dxh, axis=-1, keepdims=True)
                      - xh * jnp.mean(dxh * xh, axis=-1, keepdims=True))
        dz_ref[rows, :] = jnp.concatenate([du, dzv], axis=-1) * gelu_grad

        dab = d_attn.astype(BF16)
        own = _key_in_block()
        dq_parts = []
        dk_parts = []
        dv_parts = []
        for gi in range(N_KV_HEADS):
            cols = slice(gi * HEAD_DIM, (gi + 1) * HEAD_DIM)
            kg, vg = kb[:, cols], vb[:, cols]
            dkg = jnp.zeros((2 * BLK, HEAD_DIM), F32)
            dvg = jnp.zeros((2 * BLK, HEAD_DIM), F32)
            for rr in range(REP):
                h = gi * REP + rr
                hs = slice(h * HEAD_DIM, (h + 1) * HEAD_DIM)
                qh, doh = q_v[:, hs], dab[:, hs]
                pn, band, psink = probs[h]
                dp = _fold(_dot_nt(vg, doh), own)
                delta = jnp.sum(pn * dp, axis=0, keepdims=True)
                ds2 = _unfold(pn * (dp - delta), own)
                dsink = jnp.sum(-psink * delta, axis=-1, keepdims=True)
                dsk_ref[pl.ds(h, 1), :] += jnp.broadcast_to(dsink, (1, 128))
                dq_parts.append(_dot_tn(ds2, kg) * ATTN_SCALE)
                dkg = dkg + _dot(ds2, qh)
                dvg = dvg + _dot(band, doh)
            dk_parts.append(dkg)
            dv_parts.append(dvg)
        dq_ref[rows, :] = jnp.concatenate(dq_parts, axis=-1)
        dkb = jnp.concatenate(dk_parts, axis=-1)
        dvb = jnp.concatenate(dv_parts, axis=-1)
        prev = pl.ds(pl.multiple_of(jnp.maximum(i - 1, 0) * BLK, BLK), BLK)
        cur = pl.ds(pl.multiple_of(i * BLK, BLK), BLK)
        dk_ref[prev, :] += dkb[:BLK]
        dv_ref[prev, :] += dvb[:BLK]
        dk_ref[cur, :] += dkb[BLK:]
        dv_ref[cur, :] += dvb[BLK:]

    return pl.pallas_call(
        body, name=name, grid=(nsteps,),
        in_specs=[pl.BlockSpec(memory_space=pltpu.SMEM),
                  _rows(nb * BLK, D_MODEL), _rows(nb * BLK, ATTN_W), _const((s, KV_W)), _const((s, KV_W)),
                  _rows(nb * BLK, 2 * GMLP_W), _const((1, GMLP_W)), _const((1, GMLP_W)),
                  _const((GMLP_GROUPS, BLK, BLK)), _const((BLK, GMLP_W)), _const((1, ATTN_W)), _const((1, GMLP_W))],
        out_specs=[_rows(nb * BLK, ATTN_W), _const((s, KV_W)), _const((s, KV_W)), _rows(nb * BLK, 2 * GMLP_W),
                   _const((1, ATTN_W)), _const((1, GMLP_W)),
                   _const((1, GMLP_W)), _const((1, GMLP_W)), _const((GMLP_GROUPS, BLK, BLK)),
                   _const((BLK, GMLP_W)), _const((N_Q_HEADS, 128))],
        out_shape=[jax.ShapeDtypeStruct((s, ATTN_W), F32), jax.ShapeDtypeStruct((s, KV_W), F32),
                   jax.ShapeDtypeStruct((s, KV_W), F32), jax.ShapeDtypeStruct((s, 2 * GMLP_W), F32),
                   jax.ShapeDtypeStruct((1, ATTN_W), F32), jax.ShapeDtypeStruct((1, GMLP_W), F32),
                   jax.ShapeDtypeStruct((1, GMLP_W), F32), jax.ShapeDtypeStruct((1, GMLP_W), F32),
                   jax.ShapeDtypeStruct((GMLP_GROUPS, BLK, BLK), F32), jax.ShapeDtypeStruct((BLK, GMLP_W), F32),
                   jax.ShapeDtypeStruct((N_Q_HEADS, 128), F32)],
        compiler_params=_cparams(("arbitrary",)),
    )(sinks, dyy, q, k, v, zg, lng, lnb, w_s, bs_full, gao, ggo)


def _local_step(place, x, tgt, p, own_a, pack_a, pack_b, pack_m, *, tile=512, fwd_tile=256, bwd_tile=512,
                norm_tile=512):
    g = {}
    tile, fwd_tile, bwd_tile, norm_tile = (min(t_, x.shape[0]) for t_ in (tile, fwd_tile, bwd_tile, norm_tile))
    hb1, a1, b1, part1, pack_a = _ffn1_own(x, p["ffn1_norm_g"], own_a, pack_a, tile=tile, name="ffn1_own")
    x1, a1, b1, pack_m = _ffn1_others(place, hb1, part1, a1, b1, pack_a, pack_m, tile=tile, name="ffn1_fwd")
    w_in_t = pack_m[:, :IN_SH, :].reshape(IN_W, D_MODEL)
    w_out = pack_m[:, IN_SH:, :].reshape(D_MODEL, D_MODEL)
    q, k, v, zg, yb, x2, pack_b = _mixer_fwd(
        x1, p["mix_norm_g"], w_in_t, p["b_in"], p["attn_sinks"], p["gmlp_ln_g"], p["gmlp_ln_b"], p["gmlp_w_s"],
        p["bs_full"], p["attn_out_norm_g"], p["gmlp_out_norm_g"], w_out, p["b_out"], pack_b, name="mixer_fwd")
    mix_args = (q, k, v, zg, p["attn_sinks"], p["gmlp_ln_g"], p["gmlp_ln_b"], p["gmlp_w_s"], p["bs_full"],
                p["attn_out_norm_g"], p["gmlp_out_norm_g"])
    dx3, loss, g["final_norm_g"], hb2, a2, b2, do3 = _ffn_fwd_loss(
        x2, p["ffn2_norm_g"], pack_b, 0, p["final_norm_g"], tgt, tile=fwd_tile, name="ffn2_fwd_loss")

    dhp, land = _ffn_bwd(place, hb2, a2, b2, do3, pack_b, 1, None, None, tile=bwd_tile, name="ffn2_bwd")
    dx2, g["ffn2_norm_g"], dyy, dw_out, g["b_out"] = _norm_bwd_mix_out(
        dhp, x2, dx3, p["ffn2_norm_g"], yb, w_out, tile=norm_tile, name="ffn2_norm_bwd")

    (dq, dk, dv, dz, g["attn_out_norm_g"], g["gmlp_out_norm_g"], g["gmlp_ln_g"],
     g["gmlp_ln_b"], g["gmlp_w_s"], dmix_sum, dsinks) = _mix_core_bwd(dyy, *mix_args, name="mix_core_bwd")
    g["gmlp_b_s"] = dmix_sum
    g["attn_sinks"] = dsinks
    dx1, dw_in_t, g["b_in"], g["mix_norm_g"], do1 = _mix_in_bwd(
        x1, dx2, dq, dk, dv, dz, p["mix_norm_g"], w_in_t, tile=tile, name="mix_in_bwd")
    mix_grads = _mix_grads_pack(dw_in_t, dw_out, name="mix_grads_pack")

    dhp1, land = _ffn_bwd(place, hb1, a1, b1, do1, pack_a, 0, land, mix_grads, True, tile=bwd_tile, name="ffn1_bwd")
    dx0, g["ffn1_norm_g"] = _norm_bwd(dhp1, x, dx1, p["ffn1_norm_g"], tile=norm_tile, name="ffn1_norm_bwd")
    return loss, dx0, land, g


def _pack_cast(place, parts, *, name):
    def body(place_ref, *refs):
        oa_ref, ob_ref, om_ref, own_ref = refs[-4:]
        off = 0
        for k, (ref, rows) in enumerate(zip(refs[:-4], BIG_ROWS)):
            if k in (3, 6):
                off = 0
            cast = ref[...].astype(BF16)
            (oa_ref if k < 3 else ob_ref if k < 6 else om_ref)[0, off:off + rows, :] = cast
            if k < 3:
                own_ref[off:off + rows, :] = cast
            off += rows

    one = pl.Buffered(1)

    def slab(rows):
        return pl.BlockSpec((1, rows, D_MODEL), lambda i, pr: (pr[0], 0, 0), pipeline_mode=one)

    grid_spec = pltpu.PrefetchScalarGridSpec(
        num_scalar_prefetch=1, grid=(1,),
        in_specs=[pl.BlockSpec((rows, D_MODEL), lambda i, pr: (0, 0), pipeline_mode=one) for rows in BIG_ROWS],
        out_specs=[slab(PACK_A_ROWS), slab(PACK_B_ROWS), slab(PACK_M_ROWS),
                   pl.BlockSpec((PACK_A_ROWS, D_MODEL), lambda i, pr: (0, 0), pipeline_mode=one)])
    return pl.pallas_call(
        body, name=name, grid_spec=grid_spec,
        out_shape=[jax.ShapeDtypeStruct((N_CHIPS, PACK_A_ROWS, D_MODEL), BF16),
                   jax.ShapeDtypeStruct((N_CHIPS, PACK_B_ROWS, D_MODEL), BF16),
                   jax.ShapeDtypeStruct((N_CHIPS, PACK_M_ROWS, D_MODEL), BF16),
                   jax.ShapeDtypeStruct((PACK_A_ROWS, D_MODEL), BF16)],
        compiler_params=_cparams(("arbitrary",)),
    )(place, *parts)


def _shard_tile(i, c):
    return jnp.where(i < 3, 3 * c + i, jnp.where(i < 6, 3 + 3 * c + i, 12 + c))


def _rs_reduce(place, land, *, name):
    def body(place_ref, l_ref, o_ref):
        acc = l_ref[0].astype(F32)
        for d in range(1, 2 * N_CHIPS):
            acc = acc + l_ref[d].astype(F32)
        o_ref[...] = acc

    grid_spec = pltpu.PrefetchScalarGridSpec(
        num_scalar_prefetch=1, grid=(HALF_ROWS // MIX_HALF,),
        in_specs=[pl.BlockSpec((2 * N_CHIPS, MIX_HALF, D_MODEL), lambda i, pr: (0, i, 0))],
        out_specs=pl.BlockSpec((MIX_HALF, D_MODEL), lambda i, pr: (_shard_tile(i, pr[1]), 0)))
    return pl.pallas_call(
        body, name=name, grid_spec=grid_spec,
        out_shape=jax.ShapeDtypeStruct((PACK_ROWS, D_MODEL), F32),
        compiler_params=_cparams(("arbitrary",)),
    )(place, land)


def _small_all_reduce(packed, shard, *, name):
    rows = packed.shape[0]
    half = rows // 2

    def body(p_ref, sh_in_ref, o_ref, sh_ref, sib_ref, slots_ref, send_sems, recv_sems, share_send, share_recv):
        x, y, c, others = _mesh_place()
        me = 2 * x + y
        sibling = (x, y, 1 - c)
        share_start, share_finish = _share_stages(sh_ref, share_send, share_recv)
        share_start()

        def half_of(core):
            return pl.ds(pl.multiple_of(core * half, 8), half)

        def remote(k, src, dst, to):
            return pltpu.make_async_remote_copy(src_ref=src, dst_ref=dst, send_sem=send_sems.at[k],
                                                recv_sem=recv_sems.at[k], device_id=to, device_id_type=MESH)

        sib = remote(0, p_ref.at[half_of(1 - c)], sib_ref, sibling)
        sib.start()
        sib.wait()
        slots_ref[me] = p_ref[half_of(c), :] + sib_ref[...]
        sends = [remote(1 + j, slots_ref.at[me], slots_ref.at[me], (px, py, c)) for j, (px, py) in enumerate(others)]
        for cp in sends:
            cp.start()
        for j, (px, py) in enumerate(others):
            slab = slots_ref.at[2 * px + py]
            remote(1 + j, slab, slab, (px, py, c)).wait_recv()
        for cp in sends:
            cp.wait_send()
        o_ref[half_of(c), :] = (slots_ref[0] + slots_ref[1]) + (slots_ref[2] + slots_ref[3])
        back = remote(4, o_ref.at[half_of(c)], o_ref.at[half_of(c)], sibling)
        back.start()
        remote(4, o_ref.at[half_of(1 - c)], o_ref.at[half_of(1 - c)], sibling).wait_recv()
        back.wait_send()
        share_finish()

    vm = pl.BlockSpec(memory_space=pltpu.VMEM)
    hbm = pl.BlockSpec(memory_space=pl.ANY)
    return pl.pallas_call(
        body, name=name, in_specs=[vm, hbm], out_specs=[vm, hbm],
        out_shape=[jax.ShapeDtypeStruct((rows, 128), F32), jax.ShapeDtypeStruct(shard.shape, shard.dtype)],
        input_output_aliases={1: 1},
        scratch_shapes=[pltpu.VMEM((half, 128), F32), pltpu.VMEM((N_CHIPS, half, 128), F32),
                        pltpu.SemaphoreType.DMA((5,)), pltpu.SemaphoreType.DMA((5,)),
                        pltpu.SemaphoreType.DMA((3,)), pltpu.SemaphoreType.DMA((3,))],
    )(packed, shard)


def _adamw(w, g, m, v, *, g_row0, tile, name):
    rows, cols = w.shape
    assert g_row0 % tile == 0 and rows % tile == 0

    def body(w_ref, g_ref, m_ref, v_ref, go_ref, d_ref, nm_ref, nv_ref):
        g_v = g_ref[...]
        m_n = ADAM_B1 * m_ref[...] + (1.0 - ADAM_B1) * g_v
        v_n = ADAM_B2 * v_ref[...] + (1.0 - ADAM_B2) * (g_v * g_v)
        m_hat = m_n / (1.0 - ADAM_B1 ** ADAM_STEP)
        v_hat = v_n / (1.0 - ADAM_B2 ** ADAM_STEP)
        d_ref[...] = -ADAM_LR * (m_hat / (jnp.sqrt(v_hat) + ADAM_EPS) + ADAM_WD * w_ref[...])
        go_ref[...] = g_v
        nm_ref[...] = m_n
        nv_ref[...] = v_n

    spec = pl.BlockSpec((tile, cols), lambda i: (i, 0))
    gspec = pl.BlockSpec((tile, cols), lambda i: (g_row0 // tile + i, 0))
    shape = jax.ShapeDtypeStruct((rows, cols), F32)
    return pl.pallas_call(
        body, name=name, grid=(rows // tile,),
        in_specs=[spec, gspec, spec, spec], out_specs=[spec] * 4, out_shape=[shape] * 4,
        compiler_params=_cparams(("arbitrary",)),
    )(w, g, m, v)


def kernel(x, ffn1_norm_g, ffn1_w_gate, ffn1_w_up, ffn1_w_down, mix_norm_g, w_in, b_in, attn_sinks, gmlp_ln_g, gmlp_ln_b, gmlp_w_s, gmlp_b_s, attn_out_norm_g, gmlp_out_norm_g, w_out, b_out, ffn2_norm_g, ffn2_w_gate, ffn2_w_up, ffn2_w_down, final_norm_g, loss_target, m_ffn1_norm_g, m_ffn1_w_gate, m_ffn1_w_up, m_ffn1_w_down, m_mix_norm_g, m_w_in, m_b_in, m_attn_sinks, m_gmlp_ln_g, m_gmlp_ln_b, m_gmlp_w_s, m_gmlp_b_s, m_attn_out_norm_g, m_gmlp_out_norm_g, m_w_out, m_b_out, m_ffn2_norm_g, m_ffn2_w_gate, m_ffn2_w_up, m_ffn2_w_down, m_final_norm_g, v_ffn1_norm_g, v_ffn1_w_gate, v_ffn1_w_up, v_ffn1_w_down, v_mix_norm_g, v_w_in, v_b_in, v_attn_sinks, v_gmlp_ln_g, v_gmlp_ln_b, v_gmlp_w_s, v_gmlp_b_s, v_attn_out_norm_g, v_gmlp_out_norm_g, v_w_out, v_b_out, v_ffn2_norm_g, v_ffn2_w_gate, v_ffn2_w_up, v_ffn2_w_down, v_final_norm_g):
    f_args = dict(locals())
    weights = {n: f_args[n] for n in [nm for nm, _ in SMALL if nm != "loss"] + list(BIG)}
    shapes = {n: weights[n].shape for n in weights}
    shapes["loss"] = ()
    place = jnp.stack([2 * lax.axis_index("x") + lax.axis_index("y"), lax.axis_index("c")]).astype(jnp.int32)

    def with_cols(name, a):
        a2 = a.reshape(a.shape[-2], a.shape[-1])
        return a2.T if BIG_TRANSPOSED[BIG.index(name)] else a2

    def natural(name, a2):
        return (a2.T if BIG_TRANSPOSED[BIG.index(name)] else a2).reshape(shapes[name])

    pack_a, pack_b, pack_m, own_a = _pack_cast(place, [with_cols(n, weights[n]) for n in BIG], name="pack_cast")
    p = {n: weights[n].reshape(1, -1) for n in ("ffn1_norm_g", "mix_norm_g", "b_in", "gmlp_ln_g", "gmlp_ln_b",
                                                "attn_out_norm_g", "gmlp_out_norm_g", "b_out", "ffn2_norm_g",
                                                "final_norm_g")}
    p["attn_sinks"] = attn_sinks.reshape(N_Q_HEADS)
    p["gmlp_w_s"] = gmlp_w_s.reshape(GMLP_GROUPS, BLK, BLK)
    p["bs_full"] = jnp.broadcast_to(gmlp_b_s.reshape(GMLP_GROUPS, BLK).T[:, :, None],
                                    (BLK, GMLP_GROUPS, GROUP_DIM)).reshape(BLK, GMLP_W)

    loss_part, dx0, land, gs = _local_step(place, x[0], loss_target[0], p, own_a, pack_a, pack_b, pack_m)

    gs["gmlp_b_s"] = jnp.sum(gs["gmlp_b_s"].reshape(BLK, GMLP_GROUPS, GROUP_DIM), axis=-1).T
    gs["attn_sinks"] = gs["attn_sinks"][:, 0]
    gs["loss"] = loss_part[0, 0]
    small_sum, shard = _small_all_reduce(_pack_small(gs), _rs_reduce(place, land, name="rs_reduce"),
                                         name="small_all_reduce")

    grad_w, delta, new_m, new_v = {}, {}, {}, {}
    off = 0
    for n, rows in zip(BIG, BIG_ROWS):
        res = _adamw(with_cols(n, weights[n]), shard, with_cols(n, f_args["m_" + n]), with_cols(n, f_args["v_" + n]),
                     g_row0=off, tile=FF_SH // 2 if rows == FF_SH else 64, name="adamw_" + n)
        grad_w[n], delta[n], new_m[n], new_v[n] = [natural(n, a) for a in res]
        off += rows
    sm = {k: {n: f_args[k + n] for n, _ in SMALL if n != "loss"} for k in ("", "m_", "v_")}
    for k in sm:
        sm[k]["loss"] = jnp.zeros((), F32)
    res = _adamw(_pack_small(sm[""]), small_sum, _pack_small(sm["m_"]), _pack_small(sm["v_"]),
                 g_row0=0, tile=SMALL_ROWS, name="adamw_small")
    small = _unpack_small(res[0], shapes)
    for dst, packed in ((grad_w, res[0]), (delta, res[1]), (new_m, res[2]), (new_v, res[3])):
        dst.update({n: a for n, a in _unpack_small(packed, shapes).items() if n != "loss"})

    order = ('ffn1_norm_g', 'ffn1_w_gate', 'ffn1_w_up', 'ffn1_w_down', 'mix_norm_g', 'w_in', 'b_in', 'attn_sinks',
             'gmlp_ln_g', 'gmlp_ln_b', 'gmlp_w_s', 'gmlp_b_s', 'attn_out_norm_g', 'gmlp_out_norm_g', 'w_out', 'b_out',
             'ffn2_norm_g', 'ffn2_w_gate', 'ffn2_w_up', 'ffn2_w_down', 'final_norm_g')
    return (small["loss"], dx0.reshape(x.shape), *[grad_w[n] for n in order], *[delta[n] for n in order],
            *[new_m[n] for n in order], *[new_v[n] for n in order])
```

```python
import functools

import jax
import jax.numpy as jnp
from jax import lax
from jax.experimental import pallas as pl
from jax.experimental.pallas import tpu as pltpu

F32 = jnp.float32
BF16 = jnp.bfloat16

D_MODEL = 1024
D_FF = 2816
N_CHIPS = 4
FF_SH = D_FF // N_CHIPS
N_Q_HEADS = 8
N_KV_HEADS = 2
REP = N_Q_HEADS // N_KV_HEADS
HEAD_DIM = 64
ATTN_W = 512
KV_W = 128
GMLP_W = 512
GMLP_GROUPS = 8
GROUP_DIM = 64
BLK = 128
MIX_FWD_BLOCKS = 2
MIX_BWD_BLOCKS = 4
IN_W = 1792
IN_SH = IN_W // N_CHIPS
OUT_SH = D_MODEL // N_CHIPS
EPS = 1e-6
FFN_RES = 0.5
ATTN_SCALE = HEAD_DIM ** -0.5

ADAM_LR = 0.001
ADAM_B1 = 0.9
ADAM_B2 = 0.999
ADAM_EPS = 1e-08
ADAM_WD = 0.01
ADAM_STEP = 10

V7X_VMEM_LIMIT = 56 * 1024 * 1024
MESH = pl.DeviceIdType.MESH


def _cparams(sem):
    return pltpu.CompilerParams(dimension_semantics=sem, vmem_limit_bytes=V7X_VMEM_LIMIT)


def _dot(a, b):
    return jnp.dot(a, b, preferred_element_type=F32)


def _dot_nt(a, b):
    return lax.dot_general(a, b, (((1,), (1,)), ((), ())), preferred_element_type=F32)


def _dot_tn(a, b):
    return lax.dot_general(a, b, (((0,), (0,)), ((), ())), preferred_element_type=F32)


def _rms(x, g):
    r = lax.rsqrt(jnp.mean(x * x, axis=-1, keepdims=True) + EPS)
    return x * r * g, r


def _rms_bwd(dh, x, r, g):
    gy = dh * g
    dx = r * gy - x * (r * r * r) * jnp.mean(gy * x, axis=-1, keepdims=True)
    dg = jnp.sum(dh * x * r, axis=0, keepdims=True)
    return dx, dg


def _const(shape):
    nd = len(shape)
    return pl.BlockSpec(shape, lambda *_: (0,) * nd)


def _rows(t, w):
    return pl.BlockSpec((t, w), lambda i: (i, 0))


PACK_ROWS = 7 * FF_SH
HALF_ROWS = PACK_ROWS // 2
FFN_HALF = 3 * FF_SH // 2
MIX_HALF = FF_SH // 2
PACK_A_ROWS = 3 * FF_SH
PACK_B_ROWS = 3 * FF_SH
PACK_M_ROWS = FF_SH
BIG = ("ffn1_w_gate", "ffn1_w_up", "ffn1_w_down", "ffn2_w_gate", "ffn2_w_up", "ffn2_w_down", "w_in", "w_out")
BIG_ROWS = (FF_SH, FF_SH, FF_SH, FF_SH, FF_SH, FF_SH, IN_SH, OUT_SH)
BIG_TRANSPOSED = (True, True, False, True, True, False, True, False)

SMALL = (("ffn1_norm_g", 1024), ("mix_norm_g", 1024), ("b_in", 1792), ("attn_sinks", 8), ("gmlp_ln_g", 512),
         ("gmlp_ln_b", 512), ("gmlp_w_s", 131072), ("gmlp_b_s", 1024), ("attn_out_norm_g", 512),
         ("gmlp_out_norm_g", 512), ("b_out", 1024), ("ffn2_norm_g", 1024), ("final_norm_g", 1024), ("loss", 1))


def _small_rows(n):
    return -(-n // 1024) * 8


SMALL_USED_ROWS = sum(_small_rows(n) for _, n in SMALL)
SMALL_ROWS = -(-SMALL_USED_ROWS // 16) * 16


def _pack_small(parts):
    out = []
    for name, n in SMALL:
        flat = parts[name].reshape(-1).astype(F32)
        rows = _small_rows(n)
        out.append(jnp.pad(flat, (0, rows * 128 - n)).reshape(rows, 128))
    if SMALL_ROWS > SMALL_USED_ROWS:
        out.append(jnp.zeros((SMALL_ROWS - SMALL_USED_ROWS, 128), F32))
    return jnp.concatenate(out, axis=0)


def _unpack_small(packed, shapes):
    res, off = {}, 0
    for name, n in SMALL:
        rows = _small_rows(n)
        res[name] = packed[off:off + rows].reshape(-1)[:n].reshape(shapes[name])
        off += rows
    return res


def _ffn_tile(x, g, wg_ref, wu_ref, wd_ref, hb_ref, a_ref, b_ref):
    h, _ = _rms(x, g)
    hb = h.astype(BF16)
    hb_ref[...] = hb
    acc = jnp.zeros(x.shape, F32)
    for j in range(N_CHIPS):
        a = _dot_nt(hb, wg_ref[j])
        b = _dot_nt(hb, wu_ref[j])
        a_ref[j] = a
        b_ref[j] = b
        f = (a * jax.nn.sigmoid(a) * b).astype(BF16)
        acc = acc + _dot(f, wd_ref[j])
    return x + FFN_RES * acc


def _ffn_saved_specs(s, tile):
    ab = pl.BlockSpec((N_CHIPS, tile, FF_SH), lambda i: (0, i, 0))
    shape = jax.ShapeDtypeStruct((N_CHIPS, s, FF_SH), F32)
    return [_rows(tile, D_MODEL), ab, ab], [jax.ShapeDtypeStruct((s, D_MODEL), BF16), shape, shape]


def _ffn_weight_specs(k0):
    one = pl.Buffered(1)
    return [pl.BlockSpec((N_CHIPS, FF_SH, D_MODEL), functools.partial(lambda kk, i: (0, kk, 0), k0 + d),
                         pipeline_mode=one) for d in range(3)]


def _mesh_place():
    x, y, c = lax.axis_index("x"), lax.axis_index("y"), lax.axis_index("c")
    others = [(1 - x, y), (x, 1 - y), (1 - x, 1 - y)]
    return x, y, c, others


def _gather_stages(o_ref, send_sems, recv_sems):
    x, y, c, others = _mesh_place()
    me = 2 * x + y
    sibling = (x, y, 1 - c)
    half_rows = o_ref.shape[1] // 2

    def half(slab, core):
        return o_ref.at[slab, pl.ds(pl.multiple_of(core * half_rows, 16), half_rows)]

    def copy(k, rows, to):
        return pltpu.make_async_remote_copy(src_ref=rows, dst_ref=rows, send_sem=send_sems.at[k],
                                            recv_sem=recv_sems.at[k], device_id=to, device_id_type=MESH)

    first = [copy(j, half(me, c), (px, py, c)) for j, (px, py) in enumerate(others)]
    passed = [copy(3 + j, half(2 * px + py, c), sibling) for j, (px, py) in enumerate(others)]

    def landed(j):
        px, py = others[j]
        copy(j, half(2 * px + py, c), (px, py, c)).wait_recv()
        passed[j].start()

    def sibling_landed(j):
        px, py = others[j]
        copy(3 + j, half(2 * px + py, 1 - c), sibling).wait_recv()

    def start():
        for cp in first:
            cp.start()

    def forward():
        for j in range(len(others)):
            landed(j)

    def finish():
        for j in range(len(others)):
            sibling_landed(j)
        for cp in first + passed:
            cp.wait_send()

    return start, forward, finish, (first, passed, landed, sibling_landed)


def _swiglu_slab(hb, wg, wu, wd):
    a = _dot_nt(hb, wg)
    b = _dot_nt(hb, wu)
    return a, b, _dot((a * jax.nn.sigmoid(a) * b).astype(BF16), wd)


def _ffn1_own(x, g, own, gather, *, tile, name):
    s = x.shape[0]
    nt = s // tile
    x_neighbour, y_neighbour, two_hop = 0, 1, 2
    others_start_at = (2 * nt) // 3
    quarter = PACK_A_ROWS // 4

    def body(x_ref, g_ref, wg_ref, wu_ref, wd_ref, gin_ref, hb_ref, a_ref, b_ref, p_ref, gat_ref, w2_ref, hb_all_ref,
             send_sems, recv_sems, w2_sem):
        ps, i = pl.program_id(0), pl.program_id(1)
        rows = pl.ds(pl.multiple_of(i * tile, tile), tile)
        xi, yi, c, _ = _mesh_place()
        _, _, _, (first, passed, landed, sibling_landed) = _gather_stages(gat_ref, send_sems, recv_sems)

        def y_chunk(k, slab):
            part = gat_ref.at[slab, pl.ds(pl.multiple_of((2 * c + k) * quarter, 16), quarter)]
            sem = (y_neighbour, 6)[k]
            return pltpu.make_async_remote_copy(src_ref=part, dst_ref=part, send_sem=send_sems.at[sem],
                                                recv_sem=recv_sems.at[sem], device_id=(xi, 1 - yi, c),
                                                device_id_type=MESH)

        @pl.when(jnp.logical_and(ps == 0, i == 0))
        def _():
            for k in range(2):
                y_chunk(k, 2 * xi + yi).start()

        @pl.when(jnp.logical_and(ps == 0, i == others_start_at))
        def _():
            first[x_neighbour].start()
            first[two_hop].start()

        @pl.when(jnp.logical_and(ps == 1, i == 0))
        def _():
            for k in range(2):
                y_chunk(k, 2 * xi + (1 - yi)).wait_recv()
            passed[y_neighbour].start()
            sibling_landed(y_neighbour)
            load = pltpu.make_async_copy(gat_ref.at[2 * xi + (1 - yi)], w2_ref, w2_sem)
            load.start()
            load.wait()

        @pl.when(ps == 0)
        def _():
            h, _ = _rms(x_ref[...], g_ref[...])
            hb = h.astype(BF16)
            hb_ref[...] = hb
            hb_all_ref[rows, :] = hb
            a_ref[0], b_ref[0], part = _swiglu_slab(hb, wg_ref[...], wu_ref[...], wd_ref[...])
            p_ref[0] = x_ref[...] + FFN_RES * part

        @pl.when(ps == 1)
        def _():
            a_ref[0], b_ref[0], part = _swiglu_slab(hb_all_ref[rows, :], w2_ref[0:FF_SH, :],
                                                    w2_ref[FF_SH:2 * FF_SH, :], w2_ref[2 * FF_SH:3 * FF_SH, :])
            p_ref[0] = FFN_RES * part

        @pl.when(jnp.logical_and(ps == 1, i == nt - 1))
        def _():
            for j in (x_neighbour, two_hop):
                landed(j)
            for j in (x_neighbour, two_hop):
                sibling_landed(j)
            for cp in [first[x_neighbour], first[two_hop], y_chunk(0, 2 * xi + yi), y_chunk(1, 2 * xi + yi)] + passed:
                cp.wait_send()

    one = pl.Buffered(1)
    wspecs = [pl.BlockSpec((FF_SH, D_MODEL), functools.partial(lambda kk, ps, i: (kk, 0), k), pipeline_mode=one)
              for k in range(3)]
    hbm = pl.BlockSpec(memory_space=pl.ANY)
    first_pass_tiles = pl.BlockSpec((tile, D_MODEL), lambda ps, i: (jnp.where(ps == 0, i, nt - 1), 0))
    by_pass = lambda w: pl.BlockSpec((1, tile, w), lambda ps, i: (ps, i, 0))
    return pl.pallas_call(
        body, name=name, grid=(2, nt),
        in_specs=[first_pass_tiles, pl.BlockSpec((1, D_MODEL), lambda ps, i: (0, 0))] + wspecs + [hbm],
        out_specs=[first_pass_tiles, by_pass(FF_SH), by_pass(FF_SH), by_pass(D_MODEL), hbm],
        out_shape=[jax.ShapeDtypeStruct((s, D_MODEL), BF16), jax.ShapeDtypeStruct((N_CHIPS, s, FF_SH), F32),
                   jax.ShapeDtypeStruct((N_CHIPS, s, FF_SH), F32), jax.ShapeDtypeStruct((2, s, D_MODEL), F32),
                   jax.ShapeDtypeStruct(gather.shape, gather.dtype)],
        input_output_aliases={5: 4},
        scratch_shapes=[pltpu.VMEM((PACK_A_ROWS, D_MODEL), BF16), pltpu.VMEM((s, D_MODEL), BF16),
                        pltpu.SemaphoreType.DMA((7,)), pltpu.SemaphoreType.DMA((7,)), pltpu.SemaphoreType.DMA],
        compiler_params=_cparams(("arbitrary", "arbitrary")),
    )(x, g, own, own, own, gather)


def _ffn1_others(place, hb, p_own, a_all, b_all, pack, gather, *, tile, name):
    s = hb.shape[0]
    nt = s // tile
    forward_at = max(nt - 6, 0)

    def body(place_ref, hb_ref, p_ref, *rest):
        w_refs = rest[:6]
        o_ref, a_ref, b_ref, gat_ref, send_sems, recv_sems = rest[9:]
        i = pl.program_id(0)
        start, forward, finish, _ = _gather_stages(gat_ref, send_sems, recv_sems)
        pl.when(i == 0)(start)
        hb = hb_ref[...]
        a_ref[0], b_ref[0], part2 = _swiglu_slab(hb, w_refs[0][0], w_refs[1][0], w_refs[2][0])
        a_ref[1], b_ref[1], part3 = _swiglu_slab(hb, w_refs[3][0], w_refs[4][0], w_refs[5][0])
        o_ref[...] = (p_ref[0] + p_ref[1]) + FFN_RES * (part2 + part3)
        pl.when(i == forward_at)(forward)
        pl.when(i == nt - 1)(finish)

    one = pl.Buffered(1)

    def wspec(t, kk):
        return pl.BlockSpec((1, FF_SH, D_MODEL), lambda i, pr: (jnp.bitwise_xor(pr[0], t + 2), kk, 0),
                            pipeline_mode=one)

    rows = lambda w: pl.BlockSpec((tile, w), lambda i, pr: (i, 0))
    ab = pl.BlockSpec((2, tile, FF_SH), lambda i, pr: (1, i, 0))
    ab_shape = jax.ShapeDtypeStruct((N_CHIPS, s, FF_SH), F32)
    hbm = pl.BlockSpec(memory_space=pl.ANY)
    grid_spec = pltpu.PrefetchScalarGridSpec(
        num_scalar_prefetch=1, grid=(nt,),
        in_specs=[rows(D_MODEL), pl.BlockSpec((2, tile, D_MODEL), lambda i, pr: (0, i, 0))]
                 + [wspec(t, kk) for t in range(2) for kk in range(3)] + [hbm, hbm, hbm],
        out_specs=[rows(D_MODEL), ab, ab, hbm],
        scratch_shapes=[pltpu.SemaphoreType.DMA((6,)), pltpu.SemaphoreType.DMA((6,))])
    return pl.pallas_call(
        body, name=name, grid_spec=grid_spec,
        out_shape=[jax.ShapeDtypeStruct(hb.shape, F32), ab_shape, ab_shape,
                   jax.ShapeDtypeStruct(gather.shape, gather.dtype)],
        input_output_aliases={9: 1, 10: 2, 11: 3},
        compiler_params=_cparams(("arbitrary",)),
    )(place, hb, p_own, *([pack] * 6), a_all, b_all, gather)


def _ffn_fwd_loss(x, g, pack, k0, gf, tgt, *, tile, name):
    s = x.shape[0]

    def body(x_ref, g_ref, wg_ref, wu_ref, wd_ref, gf_ref, t_ref, dx_ref, loss_ref, dgf_ref, hb_ref, a_ref, b_ref,
             do_ref):
        @pl.when(pl.program_id(0) == 0)
        def _():
            loss_ref[...] = jnp.zeros_like(loss_ref)
            dgf_ref[...] = jnp.zeros_like(dgf_ref)

        x3 = _ffn_tile(x_ref[...], g_ref[...], wg_ref, wu_ref, wd_ref, hb_ref, a_ref, b_ref)
        gf_v = gf_ref[...]
        out, r = _rms(x3, gf_v)
        diff = out - t_ref[...]
        part = jnp.sum(jnp.sum(diff * diff, axis=-1, keepdims=True), axis=0, keepdims=True)
        loss_ref[...] += jnp.broadcast_to(part * (0.5 / D_MODEL), loss_ref.shape)
        dx, dg = _rms_bwd(diff * (1.0 / D_MODEL), x3, r, gf_v)
        dx_ref[...] = dx
        do_ref[...] = (FFN_RES * dx).astype(BF16)
        dgf_ref[...] += dg

    saved_specs, saved_shapes = _ffn_saved_specs(s, tile)
    return pl.pallas_call(
        body, name=name, grid=(s // tile,),
        in_specs=[_rows(tile, D_MODEL), _const((1, D_MODEL))] + _ffn_weight_specs(k0)
                 + [_const((1, D_MODEL)), _rows(tile, D_MODEL)],
        out_specs=[_rows(tile, D_MODEL), _const((1, 128)), _const((1, D_MODEL))] + saved_specs
                  + [_rows(tile, D_MODEL)],
        out_shape=[jax.ShapeDtypeStruct(x.shape, F32),
                   jax.ShapeDtypeStruct((1, 128), F32),
                   jax.ShapeDtypeStruct((1, D_MODEL), F32)] + saved_shapes
                  + [jax.ShapeDtypeStruct(x.shape, BF16)],
        compiler_params=_cparams(("arbitrary",)),
    )(x, g, pack, pack, pack, gf, tgt)


def _ffn_bwd(place, hb, a, b, do, pack, region, land, mix_grads, ab_by_pass=False, *, tile, name):
    s = hb.shape[0]
    nt = s // tile
    land_rows = pl.ds(region * FFN_HALF, FFN_HALF)
    mix_rows = pl.ds(2 * FFN_HALF, MIX_HALF)
    with_mix = mix_grads is not None
    with_land = land is not None
    n_others = 2 * N_CHIPS - 1

    def body(place_ref, hb_ref, a_ref, b_ref, do_ref, wg_ref, wu_ref, wd_ref, *rest):
        rest = list(rest)
        mix_ref = rest.pop(0) if with_mix else None
        if with_land:
            rest.pop(0)
        dhp_ref, land_ref, acc_ref, stage_ref, send_sems, recv_sem, local_sem = rest[:7]
        t, i = pl.program_id(0), pl.program_id(1)
        xi, yi, c = lax.axis_index("x"), lax.axis_index("y"), lax.axis_index("c")
        dev = 4 * xi + 2 * yi + c
        tt = (t + 1) % N_CHIPS
        tx, ty = jnp.bitwise_xor(xi, tt // 2), jnp.bitwise_xor(yi, tt % 2)

        def remote(src, dst, ssem, rsem, to):
            return pltpu.make_async_remote_copy(src_ref=src, dst_ref=dst, send_sem=ssem, recv_sem=rsem,
                                                device_id=to, device_id_type=MESH)

        def stage_half(h):
            return stage_ref.at[pl.ds(pl.multiple_of(h * FFN_HALF, 16), FFN_HALF)]

        if with_mix:
            mix_send, mix_recv, mix_local = rest[7:10]

            @pl.when(jnp.logical_and(t == 0, i == 0))
            def _():
                for chip in range(N_CHIPS):
                    for h in range(2):
                        src = mix_ref.at[chip, pl.ds(h * MIX_HALF, MIX_HALF)]
                        dst = land_ref.at[dev, mix_rows]
                        mine = jnp.logical_and(2 * xi + yi == chip, c == h)

                        @pl.when(mine)
                        def _():
                            pltpu.make_async_copy(src, dst, mix_local).start()

                        @pl.when(jnp.logical_not(mine))
                        def _():
                            remote(src, dst, mix_send, mix_recv, (chip // 2, chip % 2, h)).start()

        @pl.when(i == 0)
        def _():
            acc_ref[...] = jnp.zeros_like(acc_ref)

        hb = hb_ref[...]
        dob = do_ref[...]
        wg_j, wu_j, wd_j = wg_ref[0], wu_ref[0], wd_ref[0]
        a = a_ref[0]
        b = b_ref[0]
        sg = jax.nn.sigmoid(a)
        sa = a * sg
        fb = (sa * b).astype(BF16)
        df = _dot_nt(dob, wd_j)
        dbb = (df * sa).astype(BF16)
        dab = (df * b * (sg + sa * (1.0 - sg))).astype(BF16)
        dhp_ref[0] = (_dot(dab, wg_j) + _dot(dbb, wu_j)).astype(BF16)
        acc_ref[0:FF_SH, :] += _dot_tn(dab, hb)
        acc_ref[FF_SH:2 * FF_SH, :] += _dot_tn(dbb, hb)
        acc_ref[2 * FF_SH:3 * FF_SH, :] += _dot_tn(fb, dob)

        @pl.when(i == nt - 1)
        def _():
            dst = land_ref.at[dev, land_rows]

            @pl.when(t > 0)
            def _():
                for h in range(2):
                    remote(stage_half(h), dst, send_sems.at[h], recv_sem, (tx, ty, h)).wait_send()

            def cast_rows(r, carry):
                rows = pl.ds(pl.multiple_of(r * MIX_HALF, 16), MIX_HALF)
                stage_ref[rows, :] = acc_ref[rows, :].astype(BF16)
                return carry

            lax.fori_loop(0, 3 * FF_SH // MIX_HALF, cast_rows, 0)

            @pl.when(t < N_CHIPS - 1)
            def _():
                for h in range(2):
                    remote(stage_half(h), dst, send_sems.at[h], recv_sem, (tx, ty, h)).start()

            @pl.when(t == N_CHIPS - 1)
            def _():
                own = pltpu.make_async_copy(stage_half(c), dst, local_sem)
                own.start()
                sib = remote(stage_half(1 - c), dst, send_sems.at[0], recv_sem, (xi, yi, 1 - c))
                sib.start()
                sib.wait_send()
                own.wait()
                arrivals = land_ref.at[pl.ds(0, n_others), land_rows]
                remote(arrivals, arrivals, send_sems.at[0], recv_sem, (xi, yi, 1 - c)).wait_recv()
                if with_mix:
                    seven = land_ref.at[pl.ds(0, n_others), mix_rows]
                    both = remote(seven, seven, mix_send, mix_recv, (xi, yi, 1 - c))
                    both.wait_send()
                    both.wait_recv()
                    pltpu.make_async_copy(mix_ref.at[0, pl.ds(0, MIX_HALF)], land_ref.at[dev, mix_rows],
                                          mix_local).wait()

    def wspec(kk):
        return pl.BlockSpec((1, FF_SH, D_MODEL),
                            lambda t, i, pr: (jnp.bitwise_xor(pr[0], (t + 1) % N_CHIPS), kk, 0))

    xspec = pl.BlockSpec((tile, D_MODEL), lambda t, i, pr: (i, 0))
    if ab_by_pass:
        abspec = pl.BlockSpec((1, tile, FF_SH), lambda t, i, pr: ((t + 1) % N_CHIPS, i, 0))
    else:
        abspec = pl.BlockSpec((1, tile, FF_SH), lambda t, i, pr: (jnp.bitwise_xor(pr[0], (t + 1) % N_CHIPS), i, 0))
    hbm = pl.BlockSpec(memory_space=pl.ANY)
    in_specs = [xspec, abspec, abspec, xspec, wspec(0), wspec(1), wspec(2)]
    operands = [place, hb, a, b, do, pack, pack, pack]
    scratch = [pltpu.VMEM((3 * FF_SH, D_MODEL), F32), pltpu.VMEM((3 * FF_SH, D_MODEL), BF16),
               pltpu.SemaphoreType.DMA((2,)), pltpu.SemaphoreType.DMA, pltpu.SemaphoreType.DMA]
    if with_mix:
        in_specs.append(hbm)
        operands.append(mix_grads)
        scratch += [pltpu.SemaphoreType.DMA, pltpu.SemaphoreType.DMA, pltpu.SemaphoreType.DMA]
    aliases = {}
    if with_land:
        in_specs.append(hbm)
        operands.append(land)
        aliases = {len(operands) - 1: 1}
    grid_spec = pltpu.PrefetchScalarGridSpec(
        num_scalar_prefetch=1, grid=(N_CHIPS, nt), in_specs=in_specs,
        out_specs=[pl.BlockSpec((1, tile, D_MODEL), lambda t, i, pr: (t, i, 0)), hbm],
        scratch_shapes=scratch)
    return pl.pallas_call(
        body, name=name, grid_spec=grid_spec,
        out_shape=[jax.ShapeDtypeStruct((N_CHIPS, s, D_MODEL), BF16),
                   jax.ShapeDtypeStruct((2 * N_CHIPS, HALF_ROWS, D_MODEL), BF16)],
        input_output_aliases=aliases,
        compiler_params=_cparams(("arbitrary", "arbitrary")),
    )(*operands)


def _mix_grads_pack(dw_in_t, dw_out, *, name):
    def body(a_ref, b_ref, o_ref):
        o_ref[0, 0:IN_SH, :] = a_ref[0].astype(BF16)
        o_ref[0, IN_SH:FF_SH, :] = b_ref[0].astype(BF16)

    return pl.pallas_call(
        body, name=name, grid=(N_CHIPS,),
        in_specs=[pl.BlockSpec((1, IN_SH, D_MODEL), lambda j: (j, 0, 0)),
                  pl.BlockSpec((1, OUT_SH, D_MODEL), lambda j: (j, 0, 0))],
        out_specs=pl.BlockSpec((1, FF_SH, D_MODEL), lambda j: (j, 0, 0)),
        out_shape=jax.ShapeDtypeStruct((N_CHIPS, FF_SH, D_MODEL), BF16),
        compiler_params=_cparams(("arbitrary",)),
    )(dw_in_t.reshape(N_CHIPS, IN_SH, D_MODEL), dw_out.reshape(N_CHIPS, OUT_SH, D_MODEL))


def _share_stages(o_ref, send_sems, recv_sems):
    x, y, c, _ = _mesh_place()

    def rows(k, core):
        if k < 2:
            return o_ref.at[pl.ds(pl.multiple_of(k * 2 * FFN_HALF + core * FFN_HALF, 8), FFN_HALF)]
        return o_ref.at[pl.ds(pl.multiple_of(4 * FFN_HALF + core * MIX_HALF, 8), MIX_HALF)]

    def copy(k, core):
        return pltpu.make_async_remote_copy(src_ref=rows(k, core), dst_ref=rows(k, core), send_sem=send_sems.at[k],
                                            recv_sem=recv_sems.at[k], device_id=(x, y, 1 - c), device_id_type=MESH)

    sends = [copy(k, c) for k in range(3)]

    def start():
        for cp in sends:
            cp.start()

    def finish():
        for k in range(3):
            copy(k, 1 - c).wait_recv()
        for cp in sends:
            cp.wait_send()

    return start, finish


def _norm_bwd(dhp, x, dy, g, *, tile, name):
    s = x.shape[0]

    def body(dhp_ref, x_ref, dy_ref, g_ref, dx_ref, dg_ref):
        @pl.when(pl.program_id(0) == 0)
        def _():
            dg_ref[...] = jnp.zeros_like(dg_ref)

        dh = ((dhp_ref[0].astype(F32) + dhp_ref[1].astype(F32))
              + (dhp_ref[2].astype(F32) + dhp_ref[3].astype(F32)))
        x_v = x_ref[...]
        r = lax.rsqrt(jnp.mean(x_v * x_v, axis=-1, keepdims=True) + EPS)
        dx, dg = _rms_bwd(dh, x_v, r, g_ref[...])
        dx_ref[...] = dy_ref[...] + dx
        dg_ref[...] += dg

    return pl.pallas_call(
        body, name=name, grid=(s // tile,),
        in_specs=[pl.BlockSpec((N_CHIPS, tile, D_MODEL), lambda i: (0, i, 0)),
                  _rows(tile, D_MODEL), _rows(tile, D_MODEL), _const((1, D_MODEL))],
        out_specs=[_rows(tile, D_MODEL), _const((1, D_MODEL))],
        out_shape=[jax.ShapeDtypeStruct(x.shape, F32), jax.ShapeDtypeStruct((1, D_MODEL), F32)],
        compiler_params=_cparams(("arbitrary",)),
    )(dhp, x, dy, g)


def _mix_in_bwd(x, dy, dq, dk, dv, dz, g, w_in_t, *, tile, name):
    s = x.shape[0]

    def body(x_ref, dy_ref, dq_ref, dk_ref, dv_ref, dz_ref, g_ref, w_ref, dx_ref, dw_ref, db_ref, dg_ref, do_ref):
        @pl.when(pl.program_id(0) == 0)
        def _():
            dw_ref[...] = jnp.zeros_like(dw_ref)
            db_ref[...] = jnp.zeros_like(db_ref)
            dg_ref[...] = jnp.zeros_like(dg_ref)

        dproj = jnp.concatenate([dq_ref[...], dk_ref[...], dv_ref[...], dz_ref[...]], axis=-1)
        db_ref[...] += jnp.sum(dproj, axis=0, keepdims=True)
        dpb = dproj.astype(BF16)
        x_v = x_ref[...]
        g_v = g_ref[...]
        h, r = _rms(x_v, g_v)
        dw_ref[...] += _dot_tn(dpb, h.astype(BF16))
        dh = _dot(dpb, w_ref[...])
        dxn, dg = _rms_bwd(dh, x_v, r, g_v)
        dx = dy_ref[...] + dxn
        dx_ref[...] = dx
        do_ref[...] = (FFN_RES * dx).astype(BF16)
        dg_ref[...] += dg

    return pl.pallas_call(
        body, name=name, grid=(s // tile,),
        in_specs=[_rows(tile, D_MODEL), _rows(tile, D_MODEL), _rows(tile, ATTN_W), _rows(tile, KV_W),
                  _rows(tile, KV_W), _rows(tile, 2 * GMLP_W), _const((1, D_MODEL)), _const((IN_W, D_MODEL))],
        out_specs=[_rows(tile, D_MODEL), _const((IN_W, D_MODEL)), _const((1, IN_W)), _const((1, D_MODEL)),
                   _rows(tile, D_MODEL)],
        out_shape=[jax.ShapeDtypeStruct(x.shape, F32), jax.ShapeDtypeStruct((IN_W, D_MODEL), F32),
                   jax.ShapeDtypeStruct((1, IN_W), F32), jax.ShapeDtypeStruct((1, D_MODEL), F32),
                   jax.ShapeDtypeStruct(x.shape, BF16)],
        compiler_params=_cparams(("arbitrary",)),
    )(x, dy, dq, dk, dv, dz, g, w_in_t)


_GELU_C = 0.7978845608028654
_GELU_A = 0.044715


def _gelu_tanh(x):
    x2 = x * x
    return jnp.tanh(_GELU_C * (x + _GELU_A * (x2 * x))), x2


def _band(ref, i):
    prev = jnp.maximum(i - 1, 0)
    return jnp.concatenate([ref[pl.ds(pl.multiple_of(prev * BLK, BLK), BLK), :],
                            ref[pl.ds(pl.multiple_of(i * BLK, BLK), BLK), :]], axis=0)


def _key_in_block():
    return lax.broadcasted_iota(jnp.int32, (BLK, BLK), 0) <= lax.broadcasted_iota(jnp.int32, (BLK, BLK), 1)


def _fold(band, own):
    return jnp.where(own, band[BLK:], band[:BLK])


def _unfold(a, own):
    zero = jnp.zeros_like(a)
    return jnp.concatenate([jnp.where(own, zero, a), jnp.where(own, a, zero)], axis=0).astype(BF16)


def _attn_fwd(q, kb, vb, i, sink_ref):
    own = _key_in_block()
    outs, saved = [], []
    for h in range(N_Q_HEADS):
        cols = slice((h // REP) * HEAD_DIM, (h // REP + 1) * HEAD_DIM)
        s2 = _dot_nt(kb[:, cols], q[:, h * HEAD_DIM:(h + 1) * HEAD_DIM])
        sc = jnp.where(own, s2[BLK:], jnp.where(i > 0, s2[:BLK], -jnp.inf))
        sink = sink_ref[h]
        m = jnp.maximum(jnp.max(sc, axis=0, keepdims=True), sink)
        p = jnp.exp(sc - m)
        es = jnp.exp(sink - m)
        inv = 1.0 / (jnp.sum(p, axis=0, keepdims=True) + es)
        pn = p * inv
        band = _unfold(pn, own)
        outs.append(_dot_tn(band, vb[:, cols]))
        saved.append((pn, band, es * inv))
    return jnp.concatenate(outs, axis=-1), saved


def _tril_mask():
    t = lax.broadcasted_iota(jnp.int32, (BLK, BLK), 0)
    s_ = lax.broadcasted_iota(jnp.int32, (BLK, BLK), 1)
    return s_ <= t


def _gmlp_fwd_parts(zg, lng, lnb, ws_ref, bs_full):
    th, zg2 = _gelu_tanh(zg)
    z = 0.5 * zg * (1.0 + th)
    u = z[:, :GMLP_W]
    zv = z[:, GMLP_W:]
    mu = jnp.mean(zv, axis=-1, keepdims=True)
    zc = zv - mu
    rstd = lax.rsqrt(jnp.mean(zc * zc, axis=-1, keepdims=True) + EPS)
    xh = zc * rstd
    vvb = (xh * lng + lnb).astype(BF16)
    tril = _tril_mask()
    wms, parts = [], []
    for gi in range(GMLP_GROUPS):
        wm = jnp.where(tril, ws_ref[gi], 0.0).astype(BF16)
        wms.append(wm)
        parts.append(_dot(wm, vvb[:, gi * GROUP_DIM:(gi + 1) * GROUP_DIM]))
    mixed = jnp.concatenate(parts, axis=-1) + bs_full
    gelu_grad = 0.5 * (1.0 + th) + 0.5 * zg * (1.0 - th * th) * (_GELU_C * (1.0 + 3.0 * _GELU_A * zg2))
    return u, xh, rstd, vvb, wms, mixed, gelu_grad


def _mixer_fwd(x1, g, w_in_t, b_in, sinks, lng, lnb, w_s, bs_full, gao, ggo, w_out, b_out, gather, *, name):
    s = x1.shape[0]
    nb = min(MIX_FWD_BLOCKS, s // BLK)
    step_rows = nb * BLK
    last = s // step_rows - 1

    def tile_of(i, lag):
        return jnp.clip(i - lag, 0, last)

    def body(sink_ref, xa_ref, xc_ref, g_ref, wi_ref, bi_ref, lng_ref, lnb_ref, ws_ref, bs_ref, gao_ref, ggo_ref,
             wo_ref, bo_ref, gin_ref, q_ref, k_ref, v_ref, z_ref, y_ref, o_ref, gat_ref, qs_ref, zs_ref, ys_ref,
             send_sems, recv_sems):
        i = pl.program_id(0)
        start, forward, finish, _ = _gather_stages(gat_ref, send_sems, recv_sems)

        @pl.when(i == 0)
        def _():
            for ref in (k_ref, v_ref, qs_ref, zs_ref, ys_ref):
                ref[...] = jnp.zeros_like(ref)
            start()

        slot_a, slot_b, slot_c = i % 2, (i + 1) % 2, i % 2

        o_ref[...] = xc_ref[...] + (_dot(ys_ref[slot_c], wo_ref[...]) + bo_ref[...])

        tile_b = tile_of(i, 1)
        for b in range(nb):
            blk = tile_b * nb + b
            rows = slice(b * BLK, (b + 1) * BLK)
            y_attn, _ = _attn_fwd(qs_ref[slot_b, rows, :], _band(k_ref, blk), _band(v_ref, blk), blk, sink_ref)
            u, _, _, _, _, mixed, _ = _gmlp_fwd_parts(zs_ref[slot_b, rows, :], lng_ref[...], lnb_ref[...], ws_ref,
                                                      bs_ref[...])
            ya, _ = _rms(y_attn, gao_ref[...])
            yg, _ = _rms(u * mixed, ggo_ref[...])
            y_blk = jnp.concatenate([ya, yg], axis=-1).astype(BF16)
            y_ref[rows, :] = y_blk
            ys_ref[slot_b, rows, :] = y_blk

        h, _ = _rms(xa_ref[...], g_ref[...])
        proj = _dot_nt(h.astype(BF16), wi_ref[...]) + bi_ref[...]
        q_t = (proj[:, :ATTN_W] * ATTN_SCALE).astype(BF16)
        z_t = proj[:, ATTN_W + 2 * KV_W:]
        here = pl.ds(pl.multiple_of(tile_of(i, 0) * step_rows, step_rows), step_rows)
        q_ref[...] = q_t
        z_ref[...] = z_t
        qs_ref[slot_a] = q_t
        zs_ref[slot_a] = z_t
        k_ref[here, :] = proj[:, ATTN_W:ATTN_W + KV_W].astype(BF16)
        v_ref[here, :] = proj[:, ATTN_W + KV_W:ATTN_W + 2 * KV_W].astype(BF16)
        pl.when(i == max(last - 3, 0))(forward)
        pl.when(i == last + 2)(finish)

    def lagged(width, lag):
        return pl.BlockSpec((step_rows, width), lambda i: (tile_of(i, lag), 0))

    return pl.pallas_call(
        body, name=name, grid=(last + 3,),
        in_specs=[pl.BlockSpec(memory_space=pltpu.SMEM),
                  lagged(D_MODEL, 0), lagged(D_MODEL, 2), _const((1, D_MODEL)), _const((IN_W, D_MODEL)),
                  _const((1, IN_W)), _const((1, GMLP_W)), _const((1, GMLP_W)), _const((GMLP_GROUPS, BLK, BLK)),
                  _const((BLK, GMLP_W)), _const((1, ATTN_W)), _const((1, GMLP_W)), _const((D_MODEL, D_MODEL)),
                  _const((1, D_MODEL)), pl.BlockSpec(memory_space=pl.ANY)],
        out_specs=[lagged(ATTN_W, 0), _const((s, KV_W)), _const((s, KV_W)), lagged(2 * GMLP_W, 0),
                   lagged(D_MODEL, 1), lagged(D_MODEL, 2), pl.BlockSpec(memory_space=pl.ANY)],
        out_shape=[jax.ShapeDtypeStruct((s, ATTN_W), BF16), jax.ShapeDtypeStruct((s, KV_W), BF16),
                   jax.ShapeDtypeStruct((s, KV_W), BF16), jax.ShapeDtypeStruct((s, 2 * GMLP_W), F32),
                   jax.ShapeDtypeStruct((s, D_MODEL), BF16), jax.ShapeDtypeStruct((s, D_MODEL), F32),
                   jax.ShapeDtypeStruct(gather.shape, gather.dtype)],
        input_output_aliases={14: 6},
        scratch_shapes=[pltpu.VMEM((2, step_rows, ATTN_W), BF16), pltpu.VMEM((2, step_rows, 2 * GMLP_W), F32),
                        pltpu.VMEM((2, step_rows, D_MODEL), BF16),
                        pltpu.SemaphoreType.DMA((6,)), pltpu.SemaphoreType.DMA((6,))],
        compiler_params=_cparams(("arbitrary",)),
    )(sinks, x1, x1, g, w_in_t, b_in, lng, lnb, w_s, bs_full, gao, ggo, w_out, b_out, gather)


def _norm_bwd_mix_out(dhp, x, dy, g, yb, w_out, *, tile, name):
    s = x.shape[0]

    def body(dhp_ref, x_ref, dy_ref, g_ref, y_ref, w_ref, dx_ref, dg_ref, dyy_ref, dw_ref, db_ref):
        @pl.when(pl.program_id(0) == 0)
        def _():
            dg_ref[...] = jnp.zeros_like(dg_ref)
            dw_ref[...] = jnp.zeros_like(dw_ref)
            db_ref[...] = jnp.zeros_like(db_ref)

        dh = ((dhp_ref[0].astype(F32) + dhp_ref[1].astype(F32))
              + (dhp_ref[2].astype(F32) + dhp_ref[3].astype(F32)))
        x_v = x_ref[...]
        r = lax.rsqrt(jnp.mean(x_v * x_v, axis=-1, keepdims=True) + EPS)
        dxn, dg = _rms_bwd(dh, x_v, r, g_ref[...])
        dx = dy_ref[...] + dxn
        dx_ref[...] = dx
        dg_ref[...] += dg
        dxb = dx.astype(BF16)
        db_ref[...] += jnp.sum(dx, axis=0, keepdims=True)
        dw_ref[...] += _dot_tn(y_ref[...], dxb)
        dyy_ref[...] = _dot_nt(dxb, w_ref[...])

    return pl.pallas_call(
        body, name=name, grid=(s // tile,),
        in_specs=[pl.BlockSpec((N_CHIPS, tile, D_MODEL), lambda i: (0, i, 0)),
                  _rows(tile, D_MODEL), _rows(tile, D_MODEL), _const((1, D_MODEL)), _rows(tile, D_MODEL),
                  _const((D_MODEL, D_MODEL))],
        out_specs=[_rows(tile, D_MODEL), _const((1, D_MODEL)), _rows(tile, D_MODEL), _const((D_MODEL, D_MODEL)),
                   _const((1, D_MODEL))],
        out_shape=[jax.ShapeDtypeStruct(x.shape, F32), jax.ShapeDtypeStruct((1, D_MODEL), F32),
                   jax.ShapeDtypeStruct(x.shape, F32), jax.ShapeDtypeStruct((D_MODEL, D_MODEL), F32),
                   jax.ShapeDtypeStruct((1, D_MODEL), F32)],
        compiler_params=_cparams(("arbitrary",)),
    )(dhp, x, dy, g, yb, w_out)


def _mix_core_bwd(dyy, q, k, v, zg, sinks, lng, lnb, w_s, bs_full, gao, ggo, *, name):
    s = dyy.shape[0]
    nb = min(MIX_BWD_BLOCKS, s // BLK)
    nsteps = s // (nb * BLK)

    def body(*refs):
        accumulators = refs[13:15] + refs[16:]

        @pl.when(pl.program_id(0) == 0)
        def _():
            for ref in accumulators:
                ref[...] = jnp.zeros_like(ref)

        for b in range(nb):
            one_block(pl.program_id(0) * nb + b, slice(b * BLK, (b + 1) * BLK), *refs)

        @pl.when(pl.program_id(0) == nsteps - 1)
        def _():
            tril = _tril_mask()
            for gi in range(GMLP_GROUPS):
                refs[20][gi] = jnp.where(tril, refs[20][gi], 0.0)

    def one_block(i, rows, sink_ref, dyy_ref, q_ref, k_ref, v_ref, z_ref, lng_ref, lnb_ref, ws_ref, bs_ref, gao_ref,
                  ggo_ref, dq_ref, dk_ref, dv_ref, dz_ref, dgao_ref, dggo_ref, dlng_ref, dlnb_ref, dws_ref, dms_ref,
                  dsk_ref):
        q_v = q_ref[rows, :]
        kb = _band(k_ref, i)
        vb = _band(v_ref, i)
        lng_v = lng_ref[...]
        gao_v = gao_ref[...]
        ggo_v = ggo_ref[...]

        y_attn, probs = _attn_fwd(q_v, kb, vb, i, sink_ref)
        u, xh, rstd, vvb, wms, mixed, gelu_grad = _gmlp_fwd_parts(z_ref[rows, :], lng_v, lnb_ref[...], ws_ref,
                                                                  bs_ref[...])
        y_gmlp = u * mixed
        ra = lax.rsqrt(jnp.mean(y_attn * y_attn, axis=-1, keepdims=True) + EPS)
        rg = lax.rsqrt(jnp.mean(y_gmlp * y_gmlp, axis=-1, keepdims=True) + EPS)

        dyy = dyy_ref[rows, :]
        d_attn, dgao = _rms_bwd(dyy[:, :ATTN_W], y_attn, ra, gao_v)
        d_gmlp, dggo = _rms_bwd(dyy[:, ATTN_W:], y_gmlp, rg, ggo_v)
        dgao_ref[...] += dgao
        dggo_ref[...] += dggo

        du = d_gmlp * mixed
        dmixed = d_gmlp * u
        dms_ref[...] += dmixed
        dmb = dmixed.astype(BF16)
        dvv_parts = []
        for gi in range(GMLP_GROUPS):
            sl = slice(gi * GROUP_DIM, (gi + 1) * GROUP_DIM)
            dws_ref[gi] += _dot_nt(dmb[:, sl], vvb[:, sl])
            dvv_parts.append(_dot_tn(wms[gi], dmb[:, sl]))
        dvv = jnp.concatenate(dvv_parts, axis=-1)
        dlng_ref[...] += jnp.sum(dvv * xh, axis=0, keepdims=True)
        dlnb_ref[...] += jnp.sum(dvv, axis=0, keepdims=True)
        dxh = dvv * lng_v
        dzv = rstd * (dxh - jnp.mean(dxh, axis=-1, keepdims=True)
                      - xh * jnp.mean(dxh * xh, axis=-1, keepdims=True))
        dz_ref[rows, :] = jnp.concatenate([du, dzv], axis=-1) * gelu_grad

        dab = d_attn.astype(BF16)
        own = _key_in_block()
        dq_parts = []
        dk_parts = []
        dv_parts = []
        for gi in range(N_KV_HEADS):
            cols = slice(gi * HEAD_DIM, (gi + 1) * HEAD_DIM)
            kg, vg = kb[:, cols], vb[:, cols]
            dkg = jnp.zeros((2 * BLK, HEAD_DIM), F32)
            dvg = jnp.zeros((2 * BLK, HEAD_DIM), F32)
            for rr in range(REP):
                h = gi * REP + rr
                hs = slice(h * HEAD_DIM, (h + 1) * HEAD_DIM)
                qh, doh = q_v[:, hs], dab[:, hs]
                pn, band, psink = probs[h]
                dp = _fold(_dot_nt(vg, doh), own)
                delta = jnp.sum(pn * dp, axis=0, keepdims=True)
                ds2 = _unfold(pn * (dp - delta), own)
                dsink = jnp.sum(-psink * delta, axis=-1, keepdims=True)
                dsk_ref[pl.ds(h, 1), :] += jnp.broadcast_to(dsink, (1, 128))
                dq_parts.append(_dot_tn(ds2, kg) * ATTN_SCALE)
                dkg = dkg + _dot(ds2, qh)
                dvg = dvg + _dot(band, doh)
            dk_parts.append(dkg)
            dv_parts.append(dvg)
        dq_ref[rows, :] = jnp.concatenate(dq_parts, axis=-1)
        dkb = jnp.concatenate(dk_parts, axis=-1)
        dvb = jnp.concatenate(dv_parts, axis=-1)
        prev = pl.ds(pl.multiple_of(jnp.maximum(i - 1, 0) * BLK, BLK), BLK)
        cur = pl.ds(pl.multiple_of(i * BLK, BLK), BLK)
        dk_ref[prev, :] += dkb[:BLK]
        dv_ref[prev, :] += dvb[:BLK]
        dk_ref[cur, :] += dkb[BLK:]
        dv_ref[cur, :] += dvb[BLK:]

    return pl.pallas_call(
        body, name=name, grid=(nsteps,),
        in_specs=[pl.BlockSpec(memory_space=pltpu.SMEM),
                  _rows(nb * BLK, D_MODEL), _rows(nb * BLK, ATTN_W), _const((s, KV_W)), _const((s, KV_W)),
                  _rows(nb * BLK, 2 * GMLP_W), _const((1, GMLP_W)), _const((1, GMLP_W)),
                  _const((GMLP_GROUPS, BLK, BLK)), _const((BLK, GMLP_W)), _const((1, ATTN_W)), _const((1, GMLP_W))],
        out_specs=[_rows(nb * BLK, ATTN_W), _const((s, KV_W)), _const((s, KV_W)), _rows(nb * BLK, 2 * GMLP_W),
                   _const((1, ATTN_W)), _const((1, GMLP_W)),
                   _const((1, GMLP_W)), _const((1, GMLP_W)), _const((GMLP_GROUPS, BLK, BLK)),
                   _const((BLK, GMLP_W)), _const((N_Q_HEADS, 128))],
        out_shape=[jax.ShapeDtypeStruct((s, ATTN_W), F32), jax.ShapeDtypeStruct((s, KV_W), F32),
                   jax.ShapeDtypeStruct((s, KV_W), F32), jax.ShapeDtypeStruct((s, 2 * GMLP_W), F32),
                   jax.ShapeDtypeStruct((1, ATTN_W), F32), jax.ShapeDtypeStruct((1, GMLP_W), F32),
                   jax.ShapeDtypeStruct((1, GMLP_W), F32), jax.ShapeDtypeStruct((1, GMLP_W), F32),
                   jax.ShapeDtypeStruct((GMLP_GROUPS, BLK, BLK), F32), jax.ShapeDtypeStruct((BLK, GMLP_W), F32),
                   jax.ShapeDtypeStruct((N_Q_HEADS, 128), F32)],
        compiler_params=_cparams(("arbitrary",)),
    )(sinks, dyy, q, k, v, zg, lng, lnb, w_s, bs_full, gao, ggo)


def _local_step(place, x, tgt, p, own_a, pack_a, pack_b, pack_m, *, tile=512, fwd_tile=256, bwd_tile=512,
                norm_tile=512):
    g = {}
    tile, fwd_tile, bwd_tile, norm_tile = (min(t_, x.shape[0]) for t_ in (tile, fwd_tile, bwd_tile, norm_tile))
    hb1, a1, b1, part1, pack_a = _ffn1_own(x, p["ffn1_norm_g"], own_a, pack_a, tile=tile, name="ffn1_own")
    x1, a1, b1, pack_m = _ffn1_others(place, hb1, part1, a1, b1, pack_a, pack_m, tile=tile, name="ffn1_fwd")
    w_in_t = pack_m[:, :IN_SH, :].reshape(IN_W, D_MODEL)
    w_out = pack_m[:, IN_SH:, :].reshape(D_MODEL, D_MODEL)
    q, k, v, zg, yb, x2, pack_b = _mixer_fwd(
        x1, p["mix_norm_g"], w_in_t, p["b_in"], p["attn_sinks"], p["gmlp_ln_g"], p["gmlp_ln_b"], p["gmlp_w_s"],
        p["bs_full"], p["attn_out_norm_g"], p["gmlp_out_norm_g"], w_out, p["b_out"], pack_b, name="mixer_fwd")
    mix_args = (q, k, v, zg, p["attn_sinks"], p["gmlp_ln_g"], p["gmlp_ln_b"], p["gmlp_w_s"], p["bs_full"],
                p["attn_out_norm_g"], p["gmlp_out_norm_g"])
    dx3, loss, g["final_norm_g"], hb2, a2, b2, do3 = _ffn_fwd_loss(
        x2, p["ffn2_norm_g"], pack_b, 0, p["final_norm_g"], tgt, tile=fwd_tile, name="ffn2_fwd_loss")

    dhp, land = _ffn_bwd(place, hb2, a2, b2, do3, pack_b, 1, None, None, tile=bwd_tile, name="ffn2_bwd")
    dx2, g["ffn2_norm_g"], dyy, dw_out, g["b_out"] = _norm_bwd_mix_out(
        dhp, x2, dx3, p["ffn2_norm_g"], yb, w_out, tile=norm_tile, name="ffn2_norm_bwd")

    (dq, dk, dv, dz, g["attn_out_norm_g"], g["gmlp_out_norm_g"], g["gmlp_ln_g"],
     g["gmlp_ln_b"], g["gmlp_w_s"], dmix_sum, dsinks) = _mix_core_bwd(dyy, *mix_args, name="mix_core_bwd")
    g["gmlp_b_s"] = dmix_sum
    g["attn_sinks"] = dsinks
    dx1, dw_in_t, g["b_in"], g["mix_norm_g"], do1 = _mix_in_bwd(
        x1, dx2, dq, dk, dv, dz, p["mix_norm_g"], w_in_t, tile=tile, name="mix_in_bwd")
    mix_grads = _mix_grads_pack(dw_in_t, dw_out, name="mix_grads_pack")

    dhp1, land = _ffn_bwd(place, hb1, a1, b1, do1, pack_a, 0, land, mix_grads, True, tile=bwd_tile, name="ffn1_bwd")
    dx0, g["ffn1_norm_g"] = _norm_bwd(dhp1, x, dx1, p["ffn1_norm_g"], tile=norm_tile, name="ffn1_norm_bwd")
    return loss, dx0, land, g


def _pack_cast(place, parts, *, name):
    def body(place_ref, *refs):
        oa_ref, ob_ref, om_ref, own_ref = refs[-4:]
        off = 0
        for k, (ref, rows) in enumerate(zip(refs[:-4], BIG_ROWS)):
            if k in (3, 6):
                off = 0
            cast = ref[...].astype(BF16)
            (oa_ref if k < 3 else ob_ref if k < 6 else om_ref)[0, off:off + rows, :] = cast
            if k < 3:
                own_ref[off:off + rows, :] = cast
            off += rows

    one = pl.Buffered(1)

    def slab(rows):
        return pl.BlockSpec((1, rows, D_MODEL), lambda i, pr: (pr[0], 0, 0), pipeline_mode=one)

    grid_spec = pltpu.PrefetchScalarGridSpec(
        num_scalar_prefetch=1, grid=(1,),
        in_specs=[pl.BlockSpec((rows, D_MODEL), lambda i, pr: (0, 0), pipeline_mode=one) for rows in BIG_ROWS],
        out_specs=[slab(PACK_A_ROWS), slab(PACK_B_ROWS), slab(PACK_M_ROWS),
                   pl.BlockSpec((PACK_A_ROWS, D_MODEL), lambda i, pr: (0, 0), pipeline_mode=one)])
    return pl.pallas_call(
        body, name=name, grid_spec=grid_spec,
        out_shape=[jax.ShapeDtypeStruct((N_CHIPS, PACK_A_ROWS, D_MODEL), BF16),
                   jax.ShapeDtypeStruct((N_CHIPS, PACK_B_ROWS, D_MODEL), BF16),
                   jax.ShapeDtypeStruct((N_CHIPS, PACK_M_ROWS, D_MODEL), BF16),
                   jax.ShapeDtypeStruct((PACK_A_ROWS, D_MODEL), BF16)],
        compiler_params=_cparams(("arbitrary",)),
    )(place, *parts)


def _shard_tile(i, c):
    return jnp.where(i < 3, 3 * c + i, jnp.where(i < 6, 3 + 3 * c + i, 12 + c))


def _rs_reduce(place, land, *, name):
    def body(place_ref, l_ref, o_ref):
        acc = l_ref[0].astype(F32)
        for d in range(1, 2 * N_CHIPS):
            acc = acc + l_ref[d].astype(F32)
        o_ref[...] = acc

    grid_spec = pltpu.PrefetchScalarGridSpec(
        num_scalar_prefetch=1, grid=(HALF_ROWS // MIX_HALF,),
        in_specs=[pl.BlockSpec((2 * N_CHIPS, MIX_HALF, D_MODEL), lambda i, pr: (0, i, 0))],
        out_specs=pl.BlockSpec((MIX_HALF, D_MODEL), lambda i, pr: (_shard_tile(i, pr[1]), 0)))
    return pl.pallas_call(
        body, name=name, grid_spec=grid_spec,
        out_shape=jax.ShapeDtypeStruct((PACK_ROWS, D_MODEL), F32),
        compiler_params=_cparams(("arbitrary",)),
    )(place, land)


def _small_all_reduce(packed, shard, *, name):
    rows = packed.shape[0]
    half = rows // 2

    def body(p_ref, sh_in_ref, o_ref, sh_ref, sib_ref, slots_ref, send_sems, recv_sems, share_send, share_recv):
        x, y, c, others = _mesh_place()
        me = 2 * x + y
        sibling = (x, y, 1 - c)
        share_start, share_finish = _share_stages(sh_ref, share_send, share_recv)
        share_start()

        def half_of(core):
            return pl.ds(pl.multiple_of(core * half, 8), half)

        def remote(k, src, dst, to):
            return pltpu.make_async_remote_copy(src_ref=src, dst_ref=dst, send_sem=send_sems.at[k],
                                                recv_sem=recv_sems.at[k], device_id=to, device_id_type=MESH)

        sib = remote(0, p_ref.at[half_of(1 - c)], sib_ref, sibling)
        sib.start()
        sib.wait()
        slots_ref[me] = p_ref[half_of(c), :] + sib_ref[...]
        sends = [remote(1 + j, slots_ref.at[me], slots_ref.at[me], (px, py, c)) for j, (px, py) in enumerate(others)]
        for cp in sends:
            cp.start()
        for j, (px, py) in enumerate(others):
            slab = slots_ref.at[2 * px + py]
            remote(1 + j, slab, slab, (px, py, c)).wait_recv()
        for cp in sends:
            cp.wait_send()
        o_ref[half_of(c), :] = (slots_ref[0] + slots_ref[1]) + (slots_ref[2] + slots_ref[3])
        back = remote(4, o_ref.at[half_of(c)], o_ref.at[half_of(c)], sibling)
        back.start()
        remote(4, o_ref.at[half_of(1 - c)], o_ref.at[half_of(1 - c)], sibling).wait_recv()
        back.wait_send()
        share_finish()

    vm = pl.BlockSpec(memory_space=pltpu.VMEM)
    hbm = pl.BlockSpec(memory_space=pl.ANY)
    return pl.pallas_call(
        body, name=name, in_specs=[vm, hbm], out_specs=[vm, hbm],
        out_shape=[jax.ShapeDtypeStruct((rows, 128), F32), jax.ShapeDtypeStruct(shard.shape, shard.dtype)],
        input_output_aliases={1: 1},
        scratch_shapes=[pltpu.VMEM((half, 128), F32), pltpu.VMEM((N_CHIPS, half, 128), F32),
                        pltpu.SemaphoreType.DMA((5,)), pltpu.SemaphoreType.DMA((5,)),
                        pltpu.SemaphoreType.DMA((3,)), pltpu.SemaphoreType.DMA((3,))],
    )(packed, shard)


def _adamw(w, g, m, v, *, g_row0, tile, name):
    rows, cols = w.shape
    assert g_row0 % tile == 0 and rows % tile == 0

    def body(w_ref, g_ref, m_ref, v_ref, go_ref, d_ref, nm_ref, nv_ref):
        g_v = g_ref[...]
        m_n = ADAM_B1 * m_ref[...] + (1.0 - ADAM_B1) * g_v
        v_n = ADAM_B2 * v_ref[...] + (1.0 - ADAM_B2) * (g_v * g_v)
        m_hat = m_n / (1.0 - ADAM_B1 ** ADAM_STEP)
        v_hat = v_n / (1.0 - ADAM_B2 ** ADAM_STEP)
        d_ref[...] = -ADAM_LR * (m_hat / (jnp.sqrt(v_hat) + ADAM_EPS) + ADAM_WD * w_ref[...])
        go_ref[...] = g_v
        nm_ref[...] = m_n
        nv_ref[...] = v_n

    spec = pl.BlockSpec((tile, cols), lambda i: (i, 0))
    gspec = pl.BlockSpec((tile, cols), lambda i: (g_row0 // tile + i, 0))
    shape = jax.ShapeDtypeStruct((rows, cols), F32)
    return pl.pallas_call(
        body, name=name, grid=(rows // tile,),
        in_specs=[spec, gspec, spec, spec], out_specs=[spec] * 4, out_shape=[shape] * 4,
        compiler_params=_cparams(("arbitrary",)),
    )(w, g, m, v)


def kernel(x, ffn1_norm_g, ffn1_w_gate, ffn1_w_up, ffn1_w_down, mix_norm_g, w_in, b_in, attn_sinks, gmlp_ln_g, gmlp_ln_b, gmlp_w_s, gmlp_b_s, attn_out_norm_g, gmlp_out_norm_g, w_out, b_out, ffn2_norm_g, ffn2_w_gate, ffn2_w_up, ffn2_w_down, final_norm_g, loss_target, m_ffn1_norm_g, m_ffn1_w_gate, m_ffn1_w_up, m_ffn1_w_down, m_mix_norm_g, m_w_in, m_b_in, m_attn_sinks, m_gmlp_ln_g, m_gmlp_ln_b, m_gmlp_w_s, m_gmlp_b_s, m_attn_out_norm_g, m_gmlp_out_norm_g, m_w_out, m_b_out, m_ffn2_norm_g, m_ffn2_w_gate, m_ffn2_w_up, m_ffn2_w_down, m_final_norm_g, v_ffn1_norm_g, v_ffn1_w_gate, v_ffn1_w_up, v_ffn1_w_down, v_mix_norm_g, v_w_in, v_b_in, v_attn_sinks, v_gmlp_ln_g, v_gmlp_ln_b, v_gmlp_w_s, v_gmlp_b_s, v_attn_out_norm_g, v_gmlp_out_norm_g, v_w_out, v_b_out, v_ffn2_norm_g, v_ffn2_w_gate, v_ffn2_w_up, v_ffn2_w_down, v_final_norm_g):
    f_args = dict(locals())
    weights = {n: f_args[n] for n in [nm for nm, _ in SMALL if nm != "loss"] + list(BIG)}
    shapes = {n: weights[n].shape for n in weights}
    shapes["loss"] = ()
    place = jnp.stack([2 * lax.axis_index("x") + lax.axis_index("y"), lax.axis_index("c")]).astype(jnp.int32)

    def with_cols(name, a):
        a2 = a.reshape(a.shape[-2], a.shape[-1])
        return a2.T if BIG_TRANSPOSED[BIG.index(name)] else a2

    def natural(name, a2):
        return (a2.T if BIG_TRANSPOSED[BIG.index(name)] else a2).reshape(shapes[name])

    pack_a, pack_b, pack_m, own_a = _pack_cast(place, [with_cols(n, weights[n]) for n in BIG], name="pack_cast")
    p = {n: weights[n].reshape(1, -1) for n in ("ffn1_norm_g", "mix_norm_g", "b_in", "gmlp_ln_g", "gmlp_ln_b",
                                                "attn_out_norm_g", "gmlp_out_norm_g", "b_out", "ffn2_norm_g",
                                                "final_norm_g")}
    p["attn_sinks"] = attn_sinks.reshape(N_Q_HEADS)
    p["gmlp_w_s"] = gmlp_w_s.reshape(GMLP_GROUPS, BLK, BLK)
    p["bs_full"] = jnp.broadcast_to(gmlp_b_s.reshape(GMLP_GROUPS, BLK).T[:, :, None],
                                    (BLK, GMLP_GROUPS, GROUP_DIM)).reshape(BLK, GMLP_W)

    loss_part, dx0, land, gs = _local_step(place, x[0], loss_target[0], p, own_a, pack_a, pack_b, pack_m)

    gs["gmlp_b_s"] = jnp.sum(gs["gmlp_b_s"].reshape(BLK, GMLP_GROUPS, GROUP_DIM), axis=-1).T
    gs["attn_sinks"] = gs["attn_sinks"][:, 0]
    gs["loss"] = loss_part[0, 0]
    small_sum, shard = _small_all_reduce(_pack_small(gs), _rs_reduce(place, land, name="rs_reduce"),
                                         name="small_all_reduce")

    grad_w, delta, new_m, new_v = {}, {}, {}, {}
    off = 0
    for n, rows in zip(BIG, BIG_ROWS):
        res = _adamw(with_cols(n, weights[n]), shard, with_cols(n, f_args["m_" + n]), with_cols(n, f_args["v_" + n]),
                     g_row0=off, tile=FF_SH // 2 if rows == FF_SH else 64, name="adamw_" + n)
        grad_w[n], delta[n], new_m[n], new_v[n] = [natural(n, a) for a in res]
        off += rows
    sm = {k: {n: f_args[k + n] for n, _ in SMALL if n != "loss"} for k in ("", "m_", "v_")}
    for k in sm:
        sm[k]["loss"] = jnp.zeros((), F32)
    res = _adamw(_pack_small(sm[""]), small_sum, _pack_small(sm["m_"]), _pack_small(sm["v_"]),
                 g_row0=0, tile=SMALL_ROWS, name="adamw_small")
    small = _unpack_small(res[0], shapes)
    for dst, packed in ((grad_w, res[0]), (delta, res[1]), (new_m, res[2]), (new_v, res[3])):
        dst.update({n: a for n, a in _unpack_small(packed, shapes).items() if n != "loss"})

    order = ('ffn1_norm_g', 'ffn1_w_gate', 'ffn1_w_up', 'ffn1_w_down', 'mix_norm_g', 'w_in', 'b_in', 'attn_sinks',
             'gmlp_ln_g', 'gmlp_ln_b', 'gmlp_w_s', 'gmlp_b_s', 'attn_out_norm_g', 'gmlp_out_norm_g', 'w_out', 'b_out',
             'ffn2_norm_g', 'ffn2_w_gate', 'ffn2_w_up', 'ffn2_w_down', 'final_norm_g')
    return (small["loss"], dx0.reshape(x.shape), *[grad_w[n] for n in order], *[delta[n] for n in order],
            *[new_m[n] for n in order], *[new_v[n] for n in order])
```

```python
import functools

import jax
import jax.numpy as jnp
from jax import lax
from jax.experimental import pallas as pl
from jax.experimental.pallas import tpu as pltpu

F32 = jnp.float32
BF16 = jnp.bfloat16

D_MODEL = 1024
D_FF = 2816
N_CHIPS = 4
FF_SH = D_FF // N_CHIPS
N_Q_HEADS = 8
N_KV_HEADS = 2
REP = N_Q_HEADS // N_KV_HEADS
HEAD_DIM = 64
ATTN_W = 512
KV_W = 128
GMLP_W = 512
GMLP_GROUPS = 8
GROUP_DIM = 64
BLK = 128
MIX_FWD_BLOCKS = 2
MIX_BWD_BLOCKS = 4
IN_W = 1792
IN_SH = IN_W // N_CHIPS
OUT_SH = D_MODEL // N_CHIPS
EPS = 1e-6
FFN_RES = 0.5
ATTN_SCALE = HEAD_DIM ** -0.5

ADAM_LR = 0.001
ADAM_B1 = 0.9
ADAM_B2 = 0.999
ADAM_EPS = 1e-08
ADAM_WD = 0.01
ADAM_STEP = 10

V7X_VMEM_LIMIT = 56 * 1024 * 1024
MESH = pl.DeviceIdType.MESH


def _cparams(sem):
    return pltpu.CompilerParams(dimension_semantics=sem, vmem_limit_bytes=V7X_VMEM_LIMIT)


def _dot(a, b):
    return jnp.dot(a, b, preferred_element_type=F32)


def _dot_nt(a, b):
    return lax.dot_general(a, b, (((1,), (1,)), ((), ())), preferred_element_type=F32)


def _dot_tn(a, b):
    return lax.dot_general(a, b, (((0,), (0,)), ((), ())), preferred_element_type=F32)


def _rms(x, g):
    r = lax.rsqrt(jnp.mean(x * x, axis=-1, keepdims=True) + EPS)
    return x * r * g, r


def _rms_bwd(dh, x, r, g):
    gy = dh * g
    dx = r * gy - x * (r * r * r) * jnp.mean(gy * x, axis=-1, keepdims=True)
    dg = jnp.sum(dh * x * r, axis=0, keepdims=True)
    return dx, dg


def _const(shape):
    nd = len(shape)
    return pl.BlockSpec(shape, lambda *_: (0,) * nd)


def _rows(t, w):
    return pl.BlockSpec((t, w), lambda i: (i, 0))


PACK_ROWS = 7 * FF_SH
HALF_ROWS = PACK_ROWS // 2
FFN_HALF = 3 * FF_SH // 2
MIX_HALF = FF_SH // 2
PACK_A_ROWS = 3 * FF_SH
PACK_B_ROWS = 3 * FF_SH
PACK_M_ROWS = FF_SH
BIG = ("ffn1_w_gate", "ffn1_w_up", "ffn1_w_down", "ffn2_w_gate", "ffn2_w_up", "ffn2_w_down", "w_in", "w_out")
BIG_ROWS = (FF_SH, FF_SH, FF_SH, FF_SH, FF_SH, FF_SH, IN_SH, OUT_SH)
BIG_TRANSPOSED = (True, True, False, True, True, False, True, False)

SMALL = (("ffn1_norm_g", 1024), ("mix_norm_g", 1024), ("b_in", 1792), ("attn_sinks", 8), ("gmlp_ln_g", 512),
         ("gmlp_ln_b", 512), ("gmlp_w_s", 131072), ("gmlp_b_s", 1024), ("attn_out_norm_g", 512),
         ("gmlp_out_norm_g", 512), ("b_out", 1024), ("ffn2_norm_g", 1024), ("final_norm_g", 1024), ("loss", 1))


def _small_rows(n):
    return -(-n // 1024) * 8


SMALL_USED_ROWS = sum(_small_rows(n) for _, n in SMALL)
SMALL_ROWS = -(-SMALL_USED_ROWS // 16) * 16


def _pack_small(parts):
    out = []
    for name, n in SMALL:
        flat = parts[name].reshape(-1).astype(F32)
        rows = _small_rows(n)
        out.append(jnp.pad(flat, (0, rows * 128 - n)).reshape(rows, 128))
    if SMALL_ROWS > SMALL_USED_ROWS:
        out.append(jnp.zeros((SMALL_ROWS - SMALL_USED_ROWS, 128), F32))
    return jnp.concatenate(out, axis=0)


def _unpack_small(packed, shapes):
    res, off = {}, 0
    for name, n in SMALL:
        rows = _small_rows(n)
        res[name] = packed[off:off + rows].reshape(-1)[:n].reshape(shapes[name])
        off += rows
    return res


def _ffn_tile(x, g, wg_ref, wu_ref, wd_ref, hb_ref, a_ref, b_ref):
    h, _ = _rms(x, g)
    hb = h.astype(BF16)
    hb_ref[...] = hb
    acc = jnp.zeros(x.shape, F32)
    for j in range(N_CHIPS):
        a = _dot_nt(hb, wg_ref[j])
        b = _dot_nt(hb, wu_ref[j])
        a_ref[j] = a
        b_ref[j] = b
        f = (a * jax.nn.sigmoid(a) * b).astype(BF16)
        acc = acc + _dot(f, wd_ref[j])
    return x + FFN_RES * acc


def _ffn_saved_specs(s, tile):
    ab = pl.BlockSpec((N_CHIPS, tile, FF_SH), lambda i: (0, i, 0))
    shape = jax.ShapeDtypeStruct((N_CHIPS, s, FF_SH), F32)
    return [_rows(tile, D_MODEL), ab, ab], [jax.ShapeDtypeStruct((s, D_MODEL), BF16), shape, shape]


def _ffn_weight_specs(k0):
    one = pl.Buffered(1)
    return [pl.BlockSpec((N_CHIPS, FF_SH, D_MODEL), functools.partial(lambda kk, i: (0, kk, 0), k0 + d),
                         pipeline_mode=one) for d in range(3)]


def _mesh_place():
    x, y, c = lax.axis_index("x"), lax.axis_index("y"), lax.axis_index("c")
    others = [(1 - x, y), (x, 1 - y), (1 - x, 1 - y)]
    return x, y, c, others


def _gather_stages(o_ref, send_sems, recv_sems):
    x, y, c, others = _mesh_place()
    me = 2 * x + y
    sibling = (x, y, 1 - c)
    half_rows = o_ref.shape[1] // 2

    def half(slab, core):
        return o_ref.at[slab, pl.ds(pl.multiple_of(core * half_rows, 16), half_rows)]

    def copy(k, rows, to):
        return pltpu.make_async_remote_copy(src_ref=rows, dst_ref=rows, send_sem=send_sems.at[k],
                                            recv_sem=recv_sems.at[k], device_id=to, device_id_type=MESH)

    first = [copy(j, half(me, c), (px, py, c)) for j, (px, py) in enumerate(others)]
    passed = [copy(3 + j, half(2 * px + py, c), sibling) for j, (px, py) in enumerate(others)]

    def landed(j):
        px, py = others[j]
        copy(j, half(2 * px + py, c), (px, py, c)).wait_recv()
        passed[j].start()

    def sibling_landed(j):
        px, py = others[j]
        copy(3 + j, half(2 * px + py, 1 - c), sibling).wait_recv()

    def start():
        for cp in first:
            cp.start()

    def forward():
        for j in range(len(others)):
            landed(j)

    def finish():
        for j in range(len(others)):
            sibling_landed(j)
        for cp in first + passed:
            cp.wait_send()

    return start, forward, finish, (first, passed, landed, sibling_landed)


def _swiglu_slab(hb, wg, wu, wd):
    a = _dot_nt(hb, wg)
    b = _dot_nt(hb, wu)
    return a, b, _dot((a * jax.nn.sigmoid(a) * b).astype(BF16), wd)


def _ffn1_own(x, g, own, gather, *, tile, name):
    s = x.shape[0]
    nt = s // tile
    y_neighbour = 1

    def body(x_ref, g_ref, wg_ref, wu_ref, wd_ref, gin_ref, hb_ref, a_ref, b_ref, p_ref, gat_ref, w2_ref, hb_all_ref,
             send_sems, recv_sems, w2_sem):
        ps, i = pl.program_id(0), pl.program_id(1)
        rows = pl.ds(pl.multiple_of(i * tile, tile), tile)
        xi, yi, _, _ = _mesh_place()
        _, _, _, (first, passed, landed, sibling_landed) = _gather_stages(gat_ref, send_sems, recv_sems)

        @pl.when(jnp.logical_and(ps == 0, i == 0))
        def _():
            first[0].start()
            first[1].start()

        @pl.when(jnp.logical_and(ps == 1, i == 0))
        def _():
            first[0].wait_send()
            first[1].wait_send()
            first[2].start()
            landed(y_neighbour)
            sibling_landed(y_neighbour)
            load = pltpu.make_async_copy(gat_ref.at[2 * xi + (1 - yi)], w2_ref, w2_sem)
            load.start()
            load.wait()

        @pl.when(ps == 0)
        def _():
            h, _ = _rms(x_ref[...], g_ref[...])
            hb = h.astype(BF16)
            hb_ref[...] = hb
            hb_all_ref[rows, :] = hb
            a_ref[0], b_ref[0], part = _swiglu_slab(hb, wg_ref[...], wu_ref[...], wd_ref[...])
            p_ref[0] = x_ref[...] + FFN_RES * part

        @pl.when(ps == 1)
        def _():
            a_ref[0], b_ref[0], part = _swiglu_slab(hb_all_ref[rows, :], w2_ref[0:FF_SH, :],
                                                    w2_ref[FF_SH:2 * FF_SH, :], w2_ref[2 * FF_SH:3 * FF_SH, :])
            p_ref[0] = FFN_RES * part

        @pl.when(jnp.logical_and(ps == 1, i == nt - 1))
        def _():
            for j in (0, 2):
                landed(j)
            for j in (0, 2):
                sibling_landed(j)
            for cp in [first[2]] + passed:
                cp.wait_send()

    one = pl.Buffered(1)
    wspecs = [pl.BlockSpec((FF_SH, D_MODEL), functools.partial(lambda kk, ps, i: (kk, 0), k), pipeline_mode=one)
              for k in range(3)]
    hbm = pl.BlockSpec(memory_space=pl.ANY)
    first_pass_tiles = pl.BlockSpec((tile, D_MODEL), lambda ps, i: (jnp.where(ps == 0, i, nt - 1), 0))
    by_pass = lambda w: pl.BlockSpec((1, tile, w), lambda ps, i: (ps, i, 0))
    return pl.pallas_call(
        body, name=name, grid=(2, nt),
        in_specs=[first_pass_tiles, pl.BlockSpec((1, D_MODEL), lambda ps, i: (0, 0))] + wspecs + [hbm],
        out_specs=[first_pass_tiles, by_pass(FF_SH), by_pass(FF_SH), by_pass(D_MODEL), hbm],
        out_shape=[jax.ShapeDtypeStruct((s, D_MODEL), BF16), jax.ShapeDtypeStruct((N_CHIPS, s, FF_SH), F32),
                   jax.ShapeDtypeStruct((N_CHIPS, s, FF_SH), F32), jax.ShapeDtypeStruct((2, s, D_MODEL), F32),
                   jax.ShapeDtypeStruct(gather.shape, gather.dtype)],
        input_output_aliases={5: 4},
        scratch_shapes=[pltpu.VMEM((PACK_A_ROWS, D_MODEL), BF16), pltpu.VMEM((s, D_MODEL), BF16),
                        pltpu.SemaphoreType.DMA((6,)), pltpu.SemaphoreType.DMA((6,)), pltpu.SemaphoreType.DMA],
        compiler_params=_cparams(("arbitrary", "arbitrary")),
    )(x, g, own, own, own, gather)


def _ffn1_others(place, hb, p_own, a_all, b_all, pack, gather, *, tile, name):
    s = hb.shape[0]
    nt = s // tile
    forward_at = max(nt - 6, 0)
    P_RING = 3

    def body(place_ref, hb_ref, p_hbm, *rest):
        w_refs = rest[:6]
        o_ref, a_ref, b_ref, gat_ref, send_sems, recv_sems, p_ring, p_sems = rest[9:]
        i = pl.program_id(0)
        start, forward, finish, _ = _gather_stages(gat_ref, send_sems, recv_sems)
        pl.when(i == 0)(start)

        def p_copy(step):
            slot = step % P_RING
            src = p_hbm.at[:, pl.ds(pl.multiple_of(step * tile, tile), tile), :]
            return pltpu.make_async_copy(src, p_ring.at[slot], p_sems.at[slot])

        @pl.when(i == 0)
        def _():
            for step in range(min(P_RING - 1, nt)):
                p_copy(step).start()

        @pl.when(i + P_RING - 1 < nt)
        def _():
            p_copy(i + P_RING - 1).start()

        hb = hb_ref[...]
        a_ref[0], b_ref[0], part2 = _swiglu_slab(hb, w_refs[0][0], w_refs[1][0], w_refs[2][0])
        a_ref[1], b_ref[1], part3 = _swiglu_slab(hb, w_refs[3][0], w_refs[4][0], w_refs[5][0])
        p_copy(i).wait()
        p = p_ring[i % P_RING]
        o_ref[...] = (p[0] + p[1]) + FFN_RES * (part2 + part3)
        pl.when(i == forward_at)(forward)
        pl.when(i == nt - 1)(finish)

    one = pl.Buffered(1)

    def wspec(t, kk):
        return pl.BlockSpec((1, FF_SH, D_MODEL), lambda i, pr: (jnp.bitwise_xor(pr[0], t + 2), kk, 0),
                            pipeline_mode=one)

    rows = lambda w: pl.BlockSpec((tile, w), lambda i, pr: (i, 0))
    ab = pl.BlockSpec((2, tile, FF_SH), lambda i, pr: (1, i, 0))
    ab_shape = jax.ShapeDtypeStruct((N_CHIPS, s, FF_SH), F32)
    hbm = pl.BlockSpec(memory_space=pl.ANY)
    grid_spec = pltpu.PrefetchScalarGridSpec(
        num_scalar_prefetch=1, grid=(nt,),
        in_specs=[rows(D_MODEL), hbm] + [wspec(t, kk) for t in range(2) for kk in range(3)] + [hbm, hbm, hbm],
        out_specs=[rows(D_MODEL), ab, ab, hbm],
        scratch_shapes=[pltpu.SemaphoreType.DMA((6,)), pltpu.SemaphoreType.DMA((6,)),
                        pltpu.VMEM((P_RING, 2, tile, D_MODEL), F32), pltpu.SemaphoreType.DMA((P_RING,))])
    return pl.pallas_call(
        body, name=name, grid_spec=grid_spec,
        out_shape=[jax.ShapeDtypeStruct(hb.shape, F32), ab_shape, ab_shape,
                   jax.ShapeDtypeStruct(gather.shape, gather.dtype)],
        input_output_aliases={9: 1, 10: 2, 11: 3},
        compiler_params=_cparams(("arbitrary",)),
    )(place, hb, p_own, *([pack] * 6), a_all, b_all, gather)


def _ffn_fwd_loss(x, g, pack, k0, gf, tgt, *, tile, name):
    s = x.shape[0]

    def body(x_ref, g_ref, wg_ref, wu_ref, wd_ref, gf_ref, t_ref, dx_ref, loss_ref, dgf_ref, hb_ref, a_ref, b_ref,
             do_ref):
        @pl.when(pl.program_id(0) == 0)
        def _():
            loss_ref[...] = jnp.zeros_like(loss_ref)
            dgf_ref[...] = jnp.zeros_like(dgf_ref)

        x3 = _ffn_tile(x_ref[...], g_ref[...], wg_ref, wu_ref, wd_ref, hb_ref, a_ref, b_ref)
        gf_v = gf_ref[...]
        out, r = _rms(x3, gf_v)
        diff = out - t_ref[...]
        part = jnp.sum(jnp.sum(diff * diff, axis=-1, keepdims=True), axis=0, keepdims=True)
        loss_ref[...] += jnp.broadcast_to(part * (0.5 / D_MODEL), loss_ref.shape)
        dx, dg = _rms_bwd(diff * (1.0 / D_MODEL), x3, r, gf_v)
        dx_ref[...] = dx
        do_ref[...] = (FFN_RES * dx).astype(BF16)
        dgf_ref[...] += dg

    saved_specs, saved_shapes = _ffn_saved_specs(s, tile)
    return pl.pallas_call(
        body, name=name, grid=(s // tile,),
        in_specs=[_rows(tile, D_MODEL), _const((1, D_MODEL))] + _ffn_weight_specs(k0)
                 + [_const((1, D_MODEL)), _rows(tile, D_MODEL)],
        out_specs=[_rows(tile, D_MODEL), _const((1, 128)), _const((1, D_MODEL))] + saved_specs
                  + [_rows(tile, D_MODEL)],
        out_shape=[jax.ShapeDtypeStruct(x.shape, F32),
                   jax.ShapeDtypeStruct((1, 128), F32),
                   jax.ShapeDtypeStruct((1, D_MODEL), F32)] + saved_shapes
                  + [jax.ShapeDtypeStruct(x.shape, BF16)],
        compiler_params=_cparams(("arbitrary",)),
    )(x, g, pack, pack, pack, gf, tgt)


def _ffn_bwd(place, hb, a, b, do, pack, region, land, mix_grads, ab_by_pass=False, *, tile, name):
    s = hb.shape[0]
    nt = s // tile
    land_rows = pl.ds(region * FFN_HALF, FFN_HALF)
    mix_rows = pl.ds(2 * FFN_HALF, MIX_HALF)
    with_mix = mix_grads is not None
    with_land = land is not None
    n_others = 2 * N_CHIPS - 1

    def body(place_ref, hb_ref, a_ref, b_ref, do_ref, wg_ref, wu_ref, wd_ref, *rest):
        rest = list(rest)
        mix_ref = rest.pop(0) if with_mix else None
        if with_land:
            rest.pop(0)
        dhp_ref, land_ref, acc_ref, stage_ref, send_sems, recv_sem, local_sem = rest[:7]
        t, i = pl.program_id(0), pl.program_id(1)
        xi, yi, c = lax.axis_index("x"), lax.axis_index("y"), lax.axis_index("c")
        dev = 4 * xi + 2 * yi + c
        tt = (t + 1) % N_CHIPS
        tx, ty = jnp.bitwise_xor(xi, tt // 2), jnp.bitwise_xor(yi, tt % 2)

        def remote(src, dst, ssem, rsem, to):
            return pltpu.make_async_remote_copy(src_ref=src, dst_ref=dst, send_sem=ssem, recv_sem=rsem,
                                                device_id=to, device_id_type=MESH)

        def stage_half(h):
            return stage_ref.at[pl.ds(pl.multiple_of(h * FFN_HALF, 16), FFN_HALF)]

        if with_mix:
            mix_send, mix_recv, mix_local = rest[7:10]

            @pl.when(jnp.logical_and(t == 0, i == 0))
            def _():
                for chip in range(N_CHIPS):
                    for h in range(2):
                        src = mix_ref.at[chip, pl.ds(h * MIX_HALF, MIX_HALF)]
                        dst = land_ref.at[dev, mix_rows]
                        mine = jnp.logical_and(2 * xi + yi == chip, c == h)

                        @pl.when(mine)
                        def _():
                            pltpu.make_async_copy(src, dst, mix_local).start()

                        @pl.when(jnp.logical_not(mine))
                        def _():
                            remote(src, dst, mix_send, mix_recv, (chip // 2, chip % 2, h)).start()

        @pl.when(i == 0)
        def _():
            acc_ref[...] = jnp.zeros_like(acc_ref)

        hb = hb_ref[...]
        dob = do_ref[...]
        wg_j, wu_j, wd_j = wg_ref[0], wu_ref[0], wd_ref[0]
        a = a_ref[0]
        b = b_ref[0]
        sg = jax.nn.sigmoid(a)
        sa = a * sg
        fb = (sa * b).astype(BF16)
        df = _dot_nt(dob, wd_j)
        dbb = (df * sa).astype(BF16)
        dab = (df * b * (sg + sa * (1.0 - sg))).astype(BF16)
        dhp_ref[0] = (_dot(dab, wg_j) + _dot(dbb, wu_j)).astype(BF16)
        acc_ref[0:FF_SH, :] += _dot_tn(dab, hb)
        acc_ref[FF_SH:2 * FF_SH, :] += _dot_tn(dbb, hb)
        acc_ref[2 * FF_SH:3 * FF_SH, :] += _dot_tn(fb, dob)

        @pl.when(i == nt - 1)
        def _():
            dst = land_ref.at[dev, land_rows]

            @pl.when(t > 0)
            def _():
                for h in range(2):
                    remote(stage_half(h), dst, send_sems.at[h], recv_sem, (tx, ty, h)).wait_send()

            def cast_rows(r, carry):
                rows = pl.ds(pl.multiple_of(r * MIX_HALF, 16), MIX_HALF)
                stage_ref[rows, :] = acc_ref[rows, :].astype(BF16)
                return carry

            lax.fori_loop(0, 3 * FF_SH // MIX_HALF, cast_rows, 0)

            @pl.when(t < N_CHIPS - 1)
            def _():
                for h in range(2):
                    remote(stage_half(h), dst, send_sems.at[h], recv_sem, (tx, ty, h)).start()

            @pl.when(t == N_CHIPS - 1)
            def _():
                own = pltpu.make_async_copy(stage_half(c), dst, local_sem)
                own.start()
                sib = remote(stage_half(1 - c), dst, send_sems.at[0], recv_sem, (xi, yi, 1 - c))
                sib.start()
                sib.wait_send()
                own.wait()
                arrivals = land_ref.at[pl.ds(0, n_others), land_rows]
                remote(arrivals, arrivals, send_sems.at[0], recv_sem, (xi, yi, 1 - c)).wait_recv()
                if with_mix:
                    seven = land_ref.at[pl.ds(0, n_others), mix_rows]
                    both = remote(seven, seven, mix_send, mix_recv, (xi, yi, 1 - c))
                    both.wait_send()
                    both.wait_recv()
                    pltpu.make_async_copy(mix_ref.at[0, pl.ds(0, MIX_HALF)], land_ref.at[dev, mix_rows],
                                          mix_local).wait()

    def wspec(kk):
        return pl.BlockSpec((1, FF_SH, D_MODEL),
                            lambda t, i, pr: (jnp.bitwise_xor(pr[0], (t + 1) % N_CHIPS), kk, 0))

    xspec = pl.BlockSpec((tile, D_MODEL), lambda t, i, pr: (i, 0))
    if ab_by_pass:
        abspec = pl.BlockSpec((1, tile, FF_SH), lambda t, i, pr: ((t + 1) % N_CHIPS, i, 0))
    else:
        abspec = pl.BlockSpec((1, tile, FF_SH), lambda t, i, pr: (jnp.bitwise_xor(pr[0], (t + 1) % N_CHIPS), i, 0))
    hbm = pl.BlockSpec(memory_space=pl.ANY)
    in_specs = [xspec, abspec, abspec, xspec, wspec(0), wspec(1), wspec(2)]
    operands = [place, hb, a, b, do, pack, pack, pack]
    scratch = [pltpu.VMEM((3 * FF_SH, D_MODEL), F32), pltpu.VMEM((3 * FF_SH, D_MODEL), BF16),
               pltpu.SemaphoreType.DMA((2,)), pltpu.SemaphoreType.DMA, pltpu.SemaphoreType.DMA]
    if with_mix:
        in_specs.append(hbm)
        operands.append(mix_grads)
        scratch += [pltpu.SemaphoreType.DMA, pltpu.SemaphoreType.DMA, pltpu.SemaphoreType.DMA]
    aliases = {}
    if with_land:
        in_specs.append(hbm)
        operands.append(land)
        aliases = {len(operands) - 1: 1}
    grid_spec = pltpu.PrefetchScalarGridSpec(
        num_scalar_prefetch=1, grid=(N_CHIPS, nt), in_specs=in_specs,
        out_specs=[pl.BlockSpec((1, tile, D_MODEL), lambda t, i, pr: (t, i, 0)), hbm],
        scratch_shapes=scratch)
    return pl.pallas_call(
        body, name=name, grid_spec=grid_spec,
        out_shape=[jax.ShapeDtypeStruct((N_CHIPS, s, D_MODEL), BF16),
                   jax.ShapeDtypeStruct((2 * N_CHIPS, HALF_ROWS, D_MODEL), BF16)],
        input_output_aliases=aliases,
        compiler_params=_cparams(("arbitrary", "arbitrary")),
    )(*operands)


def _mix_grads_pack(dw_in_t, dw_out, *, name):
    def body(a_ref, b_ref, o_ref):
        o_ref[0, 0:IN_SH, :] = a_ref[0].astype(BF16)
        o_ref[0, IN_SH:FF_SH, :] = b_ref[0].astype(BF16)

    return pl.pallas_call(
        body, name=name, grid=(N_CHIPS,),
        in_specs=[pl.BlockSpec((1, IN_SH, D_MODEL), lambda j: (j, 0, 0)),
                  pl.BlockSpec((1, OUT_SH, D_MODEL), lambda j: (j, 0, 0))],
        out_specs=pl.BlockSpec((1, FF_SH, D_MODEL), lambda j: (j, 0, 0)),
        out_shape=jax.ShapeDtypeStruct((N_CHIPS, FF_SH, D_MODEL), BF16),
        compiler_params=_cparams(("arbitrary",)),
    )(dw_in_t.reshape(N_CHIPS, IN_SH, D_MODEL), dw_out.reshape(N_CHIPS, OUT_SH, D_MODEL))


def _share_stages(o_ref, send_sems, recv_sems):
    x, y, c, _ = _mesh_place()

    def rows(k, core):
        if k < 2:
            return o_ref.at[pl.ds(pl.multiple_of(k * 2 * FFN_HALF + core * FFN_HALF, 8), FFN_HALF)]
        return o_ref.at[pl.ds(pl.multiple_of(4 * FFN_HALF + core * MIX_HALF, 8), MIX_HALF)]

    def copy(k, core):
        return pltpu.make_async_remote_copy(src_ref=rows(k, core), dst_ref=rows(k, core), send_sem=send_sems.at[k],
                                            recv_sem=recv_sems.at[k], device_id=(x, y, 1 - c), device_id_type=MESH)

    sends = [copy(k, c) for k in range(3)]

    def start():
        for cp in sends:
            cp.start()

    def finish():
        for k in range(3):
            copy(k, 1 - c).wait_recv()
        for cp in sends:
            cp.wait_send()

    return start, finish


def _norm_bwd(dhp, x, dy, g, *, tile, name):
    s = x.shape[0]

    def body(dhp_ref, x_ref, dy_ref, g_ref, dx_ref, dg_ref):
        @pl.when(pl.program_id(0) == 0)
        def _():
            dg_ref[...] = jnp.zeros_like(dg_ref)

        dh = ((dhp_ref[0].astype(F32) + dhp_ref[1].astype(F32))
              + (dhp_ref[2].astype(F32) + dhp_ref[3].astype(F32)))
        x_v = x_ref[...]
        r = lax.rsqrt(jnp.mean(x_v * x_v, axis=-1, keepdims=True) + EPS)
        dx, dg = _rms_bwd(dh, x_v, r, g_ref[...])
        dx_ref[...] = dy_ref[...] + dx
        dg_ref[...] += dg

    return pl.pallas_call(
        body, name=name, grid=(s // tile,),
        in_specs=[pl.BlockSpec((N_CHIPS, tile, D_MODEL), lambda i: (0, i, 0)),
                  _rows(tile, D_MODEL), _rows(tile, D_MODEL), _const((1, D_MODEL))],
        out_specs=[_rows(tile, D_MODEL), _const((1, D_MODEL))],
        out_shape=[jax.ShapeDtypeStruct(x.shape, F32), jax.ShapeDtypeStruct((1, D_MODEL), F32)],
        compiler_params=_cparams(("arbitrary",)),
    )(dhp, x, dy, g)


def _mix_in_bwd(x, dy, dq, dk, dv, dz, g, w_in_t, *, tile, name):
    s = x.shape[0]

    def body(x_ref, dy_ref, dq_ref, dk_ref, dv_ref, dz_ref, g_ref, w_ref, dx_ref, dw_ref, db_ref, dg_ref, do_ref):
        @pl.when(pl.program_id(0) == 0)
        def _():
            dw_ref[...] = jnp.zeros_like(dw_ref)
            db_ref[...] = jnp.zeros_like(db_ref)
            dg_ref[...] = jnp.zeros_like(dg_ref)

        dproj = jnp.concatenate([dq_ref[...], dk_ref[...], dv_ref[...], dz_ref[...]], axis=-1)
        db_ref[...] += jnp.sum(dproj, axis=0, keepdims=True)
        dpb = dproj.astype(BF16)
        x_v = x_ref[...]
        g_v = g_ref[...]
        h, r = _rms(x_v, g_v)
        dw_ref[...] += _dot_tn(dpb, h.astype(BF16))
        dh = _dot(dpb, w_ref[...])
        dxn, dg = _rms_bwd(dh, x_v, r, g_v)
        dx = dy_ref[...] + dxn
        dx_ref[...] = dx
        do_ref[...] = (FFN_RES * dx).astype(BF16)
        dg_ref[...] += dg

    return pl.pallas_call(
        body, name=name, grid=(s // tile,),
        in_specs=[_rows(tile, D_MODEL), _rows(tile, D_MODEL), _rows(tile, ATTN_W), _rows(tile, KV_W),
                  _rows(tile, KV_W), _rows(tile, 2 * GMLP_W), _const((1, D_MODEL)), _const((IN_W, D_MODEL))],
        out_specs=[_rows(tile, D_MODEL), _const((IN_W, D_MODEL)), _const((1, IN_W)), _const((1, D_MODEL)),
                   _rows(tile, D_MODEL)],
        out_shape=[jax.ShapeDtypeStruct(x.shape, F32), jax.ShapeDtypeStruct((IN_W, D_MODEL), F32),
                   jax.ShapeDtypeStruct((1, IN_W), F32), jax.ShapeDtypeStruct((1, D_MODEL), F32),
                   jax.ShapeDtypeStruct(x.shape, BF16)],
        compiler_params=_cparams(("arbitrary",)),
    )(x, dy, dq, dk, dv, dz, g, w_in_t)


_GELU_C = 0.7978845608028654
_GELU_A = 0.044715


def _gelu_tanh(x):
    x2 = x * x
    return jnp.tanh(_GELU_C * (x + _GELU_A * (x2 * x))), x2


def _band(ref, i):
    prev = jnp.maximum(i - 1, 0)
    return jnp.concatenate([ref[pl.ds(pl.multiple_of(prev * BLK, BLK), BLK), :],
                            ref[pl.ds(pl.multiple_of(i * BLK, BLK), BLK), :]], axis=0)


def _key_in_block():
    return lax.broadcasted_iota(jnp.int32, (BLK, BLK), 0) <= lax.broadcasted_iota(jnp.int32, (BLK, BLK), 1)


def _fold(band, own):
    return jnp.where(own, band[BLK:], band[:BLK])


def _unfold(a, own):
    zero = jnp.zeros_like(a)
    return jnp.concatenate([jnp.where(own, zero, a), jnp.where(own, a, zero)], axis=0).astype(BF16)


def _attn_fwd(q, kb, vb, i, sink_ref):
    own = _key_in_block()
    outs, saved = [], []
    for h in range(N_Q_HEADS):
        cols = slice((h // REP) * HEAD_DIM, (h // REP + 1) * HEAD_DIM)
        s2 = _dot_nt(kb[:, cols], q[:, h * HEAD_DIM:(h + 1) * HEAD_DIM])
        sc = jnp.where(own, s2[BLK:], jnp.where(i > 0, s2[:BLK], -jnp.inf))
        sink = sink_ref[h]
        m = jnp.maximum(jnp.max(sc, axis=0, keepdims=True), sink)
        p = jnp.exp(sc - m)
        es = jnp.exp(sink - m)
        inv = 1.0 / (jnp.sum(p, axis=0, keepdims=True) + es)
        pn = p * inv
        band = _unfold(pn, own)
        outs.append(_dot_tn(band, vb[:, cols]))
        saved.append((pn, band, es * inv))
    return jnp.concatenate(outs, axis=-1), saved


def _tril_mask():
    t = lax.broadcasted_iota(jnp.int32, (BLK, BLK), 0)
    s_ = lax.broadcasted_iota(jnp.int32, (BLK, BLK), 1)
    return s_ <= t


def _gmlp_fwd_parts(zg, lng, lnb, ws_ref, bs_full):
    th, zg2 = _gelu_tanh(zg)
    z = 0.5 * zg * (1.0 + th)
    u = z[:, :GMLP_W]
    zv = z[:, GMLP_W:]
    mu = jnp.mean(zv, axis=-1, keepdims=True)
    zc = zv - mu
    rstd = lax.rsqrt(jnp.mean(zc * zc, axis=-1, keepdims=True) + EPS)
    xh = zc * rstd
    vvb = (xh * lng + lnb).astype(BF16)
    tril = _tril_mask()
    wms, parts = [], []
    for gi in range(GMLP_GROUPS):
        wm = jnp.where(tril, ws_ref[gi], 0.0).astype(BF16)
        wms.append(wm)
        parts.append(_dot(wm, vvb[:, gi * GROUP_DIM:(gi + 1) * GROUP_DIM]))
    mixed = jnp.concatenate(parts, axis=-1) + bs_full
    gelu_grad = 0.5 * (1.0 + th) + 0.5 * zg * (1.0 - th * th) * (_GELU_C * (1.0 + 3.0 * _GELU_A * zg2))
    return u, xh, rstd, vvb, wms, mixed, gelu_grad


def _mixer_fwd(x1, g, w_in_t, b_in, sinks, lng, lnb, w_s, bs_full, gao, ggo, w_out, b_out, gather, *, name):
    s = x1.shape[0]
    nb = min(MIX_FWD_BLOCKS, s // BLK)
    step_rows = nb * BLK
    last = s // step_rows - 1

    def tile_of(i, lag):
        return jnp.clip(i - lag, 0, last)

    def body(sink_ref, xa_ref, xc_ref, g_ref, wi_ref, bi_ref, lng_ref, lnb_ref, ws_ref, bs_ref, gao_ref, ggo_ref,
             wo_ref, bo_ref, gin_ref, q_ref, k_ref, v_ref, z_ref, y_ref, o_ref, gat_ref, qs_ref, zs_ref, ys_ref,
             send_sems, recv_sems):
        i = pl.program_id(0)
        start, forward, finish, _ = _gather_stages(gat_ref, send_sems, recv_sems)

        @pl.when(i == 0)
        def _():
            for ref in (k_ref, v_ref, qs_ref, zs_ref, ys_ref):
                ref[...] = jnp.zeros_like(ref)
            start()

        slot_a, slot_b, slot_c = i % 2, (i + 1) % 2, i % 2

        o_ref[...] = xc_ref[...] + (_dot(ys_ref[slot_c], wo_ref[...]) + bo_ref[...])

        tile_b = tile_of(i, 1)
        for b in range(nb):
            blk = tile_b * nb + b
            rows = slice(b * BLK, (b + 1) * BLK)
            y_attn, _ = _attn_fwd(qs_ref[slot_b, rows, :], _band(k_ref, blk), _band(v_ref, blk), blk, sink_ref)
            u, _, _, _, _, mixed, _ = _gmlp_fwd_parts(zs_ref[slot_b, rows, :], lng_ref[...], lnb_ref[...], ws_ref,
                                                      bs_ref[...])
            ya, _ = _rms(y_attn, gao_ref[...])
            yg, _ = _rms(u * mixed, ggo_ref[...])
            y_blk = jnp.concatenate([ya, yg], axis=-1).astype(BF16)
            y_ref[rows, :] = y_blk
            ys_ref[slot_b, rows, :] = y_blk

        h, _ = _rms(xa_ref[...], g_ref[...])
        proj = _dot_nt(h.astype(BF16), wi_ref[...]) + bi_ref[...]
        q_t = (proj[:, :ATTN_W] * ATTN_SCALE).astype(BF16)
        z_t = proj[:, ATTN_W + 2 * KV_W:]
        here = pl.ds(pl.multiple_of(tile_of(i, 0) * step_rows, step_rows), step_rows)
        q_ref[...] = q_t
        z_ref[...] = z_t
        qs_ref[slot_a] = q_t
        zs_ref[slot_a] = z_t
        k_ref[here, :] = proj[:, ATTN_W:ATTN_W + KV_W].astype(BF16)
        v_ref[here, :] = proj[:, ATTN_W + KV_W:ATTN_W + 2 * KV_W].astype(BF16)
        pl.when(i == max(last - 3, 0))(forward)
        pl.when(i == last + 2)(finish)

    def lagged(width, lag):
        return pl.BlockSpec((step_rows, width), lambda i: (tile_of(i, lag), 0))

    return pl.pallas_call(
        body, name=name, grid=(last + 3,),
        in_specs=[pl.BlockSpec(memory_space=pltpu.SMEM),
                  lagged(D_MODEL, 0), lagged(D_MODEL, 2), _const((1, D_MODEL)), _const((IN_W, D_MODEL)),
                  _const((1, IN_W)), _const((1, GMLP_W)), _const((1, GMLP_W)), _const((GMLP_GROUPS, BLK, BLK)),
                  _const((BLK, GMLP_W)), _const((1, ATTN_W)), _const((1, GMLP_W)), _const((D_MODEL, D_MODEL)),
                  _const((1, D_MODEL)), pl.BlockSpec(memory_space=pl.ANY)],
        out_specs=[lagged(ATTN_W, 0), _const((s, KV_W)), _const((s, KV_W)), lagged(2 * GMLP_W, 0),
                   lagged(D_MODEL, 1), lagged(D_MODEL, 2), pl.BlockSpec(memory_space=pl.ANY)],
        out_shape=[jax.ShapeDtypeStruct((s, ATTN_W), BF16), jax.ShapeDtypeStruct((s, KV_W), BF16),
                   jax.ShapeDtypeStruct((s, KV_W), BF16), jax.ShapeDtypeStruct((s, 2 * GMLP_W), F32),
                   jax.ShapeDtypeStruct((s, D_MODEL), BF16), jax.ShapeDtypeStruct((s, D_MODEL), F32),
                   jax.ShapeDtypeStruct(gather.shape, gather.dtype)],
        input_output_aliases={14: 6},
        scratch_shapes=[pltpu.VMEM((2, step_rows, ATTN_W), BF16), pltpu.VMEM((2, step_rows, 2 * GMLP_W), F32),
                        pltpu.VMEM((2, step_rows, D_MODEL), BF16),
                        pltpu.SemaphoreType.DMA((6,)), pltpu.SemaphoreType.DMA((6,))],
        compiler_params=_cparams(("arbitrary",)),
    )(sinks, x1, x1, g, w_in_t, b_in, lng, lnb, w_s, bs_full, gao, ggo, w_out, b_out, gather)


def _norm_bwd_mix_out(dhp, x, dy, g, yb, w_out, *, tile, name):
    s = x.shape[0]

    def body(dhp_ref, x_ref, dy_ref, g_ref, y_ref, w_ref, dx_ref, dg_ref, dyy_ref, dw_ref, db_ref):
        @pl.when(pl.program_id(0) == 0)
        def _():
            dg_ref[...] = jnp.zeros_like(dg_ref)
            dw_ref[...] = jnp.zeros_like(dw_ref)
            db_ref[...] = jnp.zeros_like(db_ref)

        dh = ((dhp_ref[0].astype(F32) + dhp_ref[1].astype(F32))
              + (dhp_ref[2].astype(F32) + dhp_ref[3].astype(F32)))
        x_v = x_ref[...]
        r = lax.rsqrt(jnp.mean(x_v * x_v, axis=-1, keepdims=True) + EPS)
        dxn, dg = _rms_bwd(dh, x_v, r, g_ref[...])
        dx = dy_ref[...] + dxn
        dx_ref[...] = dx
        dg_ref[...] += dg
        dxb = dx.astype(BF16)
        db_ref[...] += jnp.sum(dx, axis=0, keepdims=True)
        dw_ref[...] += _dot_tn(y_ref[...], dxb)
        dyy_ref[...] = _dot_nt(dxb, w_ref[...])

    return pl.pallas_call(
        body, name=name, grid=(s // tile,),
        in_specs=[pl.BlockSpec((N_CHIPS, tile, D_MODEL), lambda i: (0, i, 0)),
                  _rows(tile, D_MODEL), _rows(tile, D_MODEL), _const((1, D_MODEL)), _rows(tile, D_MODEL),
                  _const((D_MODEL, D_MODEL))],
        out_specs=[_rows(tile, D_MODEL), _const((1, D_MODEL)), _rows(tile, D_MODEL), _const((D_MODEL, D_MODEL)),
                   _const((1, D_MODEL))],
        out_shape=[jax.ShapeDtypeStruct(x.shape, F32), jax.ShapeDtypeStruct((1, D_MODEL), F32),
                   jax.ShapeDtypeStruct(x.shape, F32), jax.ShapeDtypeStruct((D_MODEL, D_MODEL), F32),
                   jax.ShapeDtypeStruct((1, D_MODEL), F32)],
        compiler_params=_cparams(("arbitrary",)),
    )(dhp, x, dy, g, yb, w_out)


def _mix_core_bwd(dyy, q, k, v, zg, sinks, lng, lnb, w_s, bs_full, gao, ggo, *, name):
    s = dyy.shape[0]
    nb = min(MIX_BWD_BLOCKS, s // BLK)
    nsteps = s // (nb * BLK)

    def body(*refs):
        accumulators = refs[13:15] + refs[16:]

        @pl.when(pl.program_id(0) == 0)
        def _():
            for ref in accumulators:
                ref[...] = jnp.zeros_like(ref)

        for b in range(nb):
            one_block(pl.program_id(0) * nb + b, slice(b * BLK, (b + 1) * BLK), *refs)

        @pl.when(pl.program_id(0) == nsteps - 1)
        def _():
            tril = _tril_mask()
            for gi in range(GMLP_GROUPS):
                refs[20][gi] = jnp.where(tril, refs[20][gi], 0.0)

    def one_block(i, rows, sink_ref, dyy_ref, q_ref, k_ref, v_ref, z_ref, lng_ref, lnb_ref, ws_ref, bs_ref, gao_ref,
                  ggo_ref, dq_ref, dk_ref, dv_ref, dz_ref, dgao_ref, dggo_ref, dlng_ref, dlnb_ref, dws_ref, dms_ref,
                  dsk_ref):
        q_v = q_ref[rows, :]
        kb = _band(k_ref, i)
        vb = _band(v_ref, i)
        lng_v = lng_ref[...]
        gao_v = gao_ref[...]
        ggo_v = ggo_ref[...]

        y_attn, probs = _attn_fwd(q_v, kb, vb, i, sink_ref)
        u, xh, rstd, vvb, wms, mixed, gelu_grad = _gmlp_fwd_parts(z_ref[rows, :], lng_v, lnb_ref[...], ws_ref,
                                                                  bs_ref[...])
        y_gmlp = u * mixed
        ra = lax.rsqrt(jnp.mean(y_attn * y_attn, axis=-1, keepdims=True) + EPS)
        rg = lax.rsqrt(jnp.mean(y_gmlp * y_gmlp, axis=-1, keepdims=True) + EPS)

        dyy = dyy_ref[rows, :]
        d_attn, dgao = _rms_bwd(dyy[:, :ATTN_W], y_attn, ra, gao_v)
        d_gmlp, dggo = _rms_bwd(dyy[:, ATTN_W:], y_gmlp, rg, ggo_v)
        dgao_ref[...] += dgao
        dggo_ref[...] += dggo

        du = d_gmlp * mixed
        dmixed = d_gmlp * u
        dms_ref[...] += dmixed
        dmb = dmixed.astype(BF16)
        dvv_parts = []
        for gi in range(GMLP_GROUPS):
            sl = slice(gi * GROUP_DIM, (gi + 1) * GROUP_DIM)
            dws_ref[gi] += _dot_nt(dmb[:, sl], vvb[:, sl])
            dvv_parts.append(_dot_tn(wms[gi], dmb[:, sl]))
        dvv = jnp.concatenate(dvv_parts, axis=-1)
        dlng_ref[...] += jnp.sum(dvv * xh, axis=0, keepdims=True)
        dlnb_ref[...] += jnp.sum(dvv, axis=0, keepdims=True)
        dxh = dvv * lng_v
        dzv = rstd * (dxh - jnp.mean(dxh, axis=-1, keepdims=True)
                      - xh * jnp.mean(dxh * xh, axis=-1, keepdims=True))
        dz_ref[rows, :] = jnp.concatenate([du, dzv], axis=-1) * gelu_grad

        dab = d_attn.astype(BF16)
        own = _key_in_block()
        dq_parts = []
        dk_parts = []
        dv_parts = []
        for gi in range(N_KV_HEADS):
            cols = slice(gi * HEAD_DIM, (gi + 1) * HEAD_DIM)
            kg, vg = kb[:, cols], vb[:, cols]
            dkg = jnp.zeros((2 * BLK, HEAD_DIM), F32)
            dvg = jnp.zeros((2 * BLK, HEAD_DIM), F32)
            for rr in range(REP):
                h = gi * REP + rr
                hs = slice(h * HEAD_DIM, (h + 1) * HEAD_DIM)
                qh, doh = q_v[:, hs], dab[:, hs]
                pn, band, psink = probs[h]
                dp = _fold(_dot_nt(vg, doh), own)
                delta = jnp.sum(pn * dp, axis=0, keepdims=True)
                ds2 = _unfold(pn * (dp - delta), own)
                dsink = jnp.sum(-psink * delta, axis=-1, keepdims=True)
                dsk_ref[pl.ds(h, 1), :] += jnp.broadcast_to(dsink, (1, 128))
                dq_parts.append(_dot_tn(ds2, kg) * ATTN_SCALE)
                dkg = dkg + _dot(ds2, qh)
                dvg = dvg + _dot(band, doh)
            dk_parts.append(dkg)
            dv_parts.append(dvg)
        dq_ref[rows, :] = jnp.concatenate(dq_parts, axis=-1)
        dkb = jnp.concatenate(dk_parts, axis=-1)
        dvb = jnp.concatenate(dv_parts, axis=-1)
        prev = pl.ds(pl.multiple_of(jnp.maximum(i - 1, 0) * BLK, BLK), BLK)
        cur = pl.ds(pl.multiple_of(i * BLK, BLK), BLK)
        dk_ref[prev, :] += dkb[:BLK]
        dv_ref[prev, :] += dvb[:BLK]
        dk_ref[cur, :] += dkb[BLK:]
        dv_ref[cur, :] += dvb[BLK:]

    return pl.pallas_call(
        body, name=name, grid=(nsteps,),
        in_specs=[pl.BlockSpec(memory_space=pltpu.SMEM),
                  _rows(nb * BLK, D_MODEL), _rows(nb * BLK, ATTN_W), _const((s, KV_W)), _const((s, KV_W)),
                  _rows(nb * BLK, 2 * GMLP_W), _const((1, GMLP_W)), _const((1, GMLP_W)),
                  _const((GMLP_GROUPS, BLK, BLK)), _const((BLK, GMLP_W)), _const((1, ATTN_W)), _const((1, GMLP_W))],
        out_specs=[_rows(nb * BLK, ATTN_W), _const((s, KV_W)), _const((s, KV_W)), _rows(nb * BLK, 2 * GMLP_W),
                   _const((1, ATTN_W)), _const((1, GMLP_W)),
                   _const((1, GMLP_W)), _const((1, GMLP_W)), _const((GMLP_GROUPS, BLK, BLK)),
                   _const((BLK, GMLP_W)), _const((N_Q_HEADS, 128))],
        out_shape=[jax.ShapeDtypeStruct((s, ATTN_W), F32), jax.ShapeDtypeStruct((s, KV_W), F32),
                   jax.ShapeDtypeStruct((s, KV_W), F32), jax.ShapeDtypeStruct((s, 2 * GMLP_W), F32),
                   jax.ShapeDtypeStruct((1, ATTN_W), F32), jax.ShapeDtypeStruct((1, GMLP_W), F32),
                   jax.ShapeDtypeStruct((1, GMLP_W), F32), jax.ShapeDtypeStruct((1, GMLP_W), F32),
                   jax.ShapeDtypeStruct((GMLP_GROUPS, BLK, BLK), F32), jax.ShapeDtypeStruct((BLK, GMLP_W), F32),
                   jax.ShapeDtypeStruct((N_Q_HEADS, 128), F32)],
        compiler_params=_cparams(("arbitrary",)),
    )(sinks, dyy, q, k, v, zg, lng, lnb, w_s, bs_full, gao, ggo)


def _local_step(place, x, tgt, p, own_a, pack_a, pack_b, pack_m, *, tile=512, fwd_tile=256, bwd_tile=512,
                norm_tile=512):
    g = {}
    tile, fwd_tile, bwd_tile, norm_tile = (min(t_, x.shape[0]) for t_ in (tile, fwd_tile, bwd_tile, norm_tile))
    hb1, a1, b1, part1, pack_a = _ffn1_own(x, p["ffn1_norm_g"], own_a, pack_a, tile=tile, name="ffn1_own")
    x1, a1, b1, pack_m = _ffn1_others(place, hb1, part1, a1, b1, pack_a, pack_m, tile=tile, name="ffn1_fwd")
    w_in_t = pack_m[:, :IN_SH, :].reshape(IN_W, D_MODEL)
    w_out = pack_m[:, IN_SH:, :].reshape(D_MODEL, D_MODEL)
    q, k, v, zg, yb, x2, pack_b = _mixer_fwd(
        x1, p["mix_norm_g"], w_in_t, p["b_in"], p["attn_sinks"], p["gmlp_ln_g"], p["gmlp_ln_b"], p["gmlp_w_s"],
        p["bs_full"], p["attn_out_norm_g"], p["gmlp_out_norm_g"], w_out, p["b_out"], pack_b, name="mixer_fwd")
    mix_args = (q, k, v, zg, p["attn_sinks"], p["gmlp_ln_g"], p["gmlp_ln_b"], p["gmlp_w_s"], p["bs_full"],
                p["attn_out_norm_g"], p["gmlp_out_norm_g"])
    dx3, loss, g["final_norm_g"], hb2, a2, b2, do3 = _ffn_fwd_loss(
        x2, p["ffn2_norm_g"], pack_b, 0, p["final_norm_g"], tgt, tile=fwd_tile, name="ffn2_fwd_loss")

    dhp, land = _ffn_bwd(place, hb2, a2, b2, do3, pack_b, 1, None, None, tile=bwd_tile, name="ffn2_bwd")
    dx2, g["ffn2_norm_g"], dyy, dw_out, g["b_out"] = _norm_bwd_mix_out(
        dhp, x2, dx3, p["ffn2_norm_g"], yb, w_out, tile=norm_tile, name="ffn2_norm_bwd")

    (dq, dk, dv, dz, g["attn_out_norm_g"], g["gmlp_out_norm_g"], g["gmlp_ln_g"],
     g["gmlp_ln_b"], g["gmlp_w_s"], dmix_sum, dsinks) = _mix_core_bwd(dyy, *mix_args, name="mix_core_bwd")
    g["gmlp_b_s"] = dmix_sum
    g["attn_sinks"] = dsinks
    dx1, dw_in_t, g["b_in"], g["mix_norm_g"], do1 = _mix_in_bwd(
        x1, dx2, dq, dk, dv, dz, p["mix_norm_g"], w_in_t, tile=tile, name="mix_in_bwd")
    mix_grads = _mix_grads_pack(dw_in_t, dw_out, name="mix_grads_pack")

    dhp1, land = _ffn_bwd(place, hb1, a1, b1, do1, pack_a, 0, land, mix_grads, True, tile=bwd_tile, name="ffn1_bwd")
    dx0, g["ffn1_norm_g"] = _norm_bwd(dhp1, x, dx1, p["ffn1_norm_g"], tile=norm_tile, name="ffn1_norm_bwd")
    return loss, dx0, land, g


def _pack_cast(place, parts, *, name):
    def body(place_ref, *refs):
        oa_ref, ob_ref, om_ref, own_ref = refs[-4:]
        off = 0
        for k, (ref, rows) in enumerate(zip(refs[:-4], BIG_ROWS)):
            if k in (3, 6):
                off = 0
            cast = ref[...].astype(BF16)
            (oa_ref if k < 3 else ob_ref if k < 6 else om_ref)[0, off:off + rows, :] = cast
            if k < 3:
                own_ref[off:off + rows, :] = cast
            off += rows

    one = pl.Buffered(1)

    def slab(rows):
        return pl.BlockSpec((1, rows, D_MODEL), lambda i, pr: (pr[0], 0, 0), pipeline_mode=one)

    grid_spec = pltpu.PrefetchScalarGridSpec(
        num_scalar_prefetch=1, grid=(1,),
        in_specs=[pl.BlockSpec((rows, D_MODEL), lambda i, pr: (0, 0), pipeline_mode=one) for rows in BIG_ROWS],
        out_specs=[slab(PACK_A_ROWS), slab(PACK_B_ROWS), slab(PACK_M_ROWS),
                   pl.BlockSpec((PACK_A_ROWS, D_MODEL), lambda i, pr: (0, 0), pipeline_mode=one)])
    return pl.pallas_call(
        body, name=name, grid_spec=grid_spec,
        out_shape=[jax.ShapeDtypeStruct((N_CHIPS, PACK_A_ROWS, D_MODEL), BF16),
                   jax.ShapeDtypeStruct((N_CHIPS, PACK_B_ROWS, D_MODEL), BF16),
                   jax.ShapeDtypeStruct((N_CHIPS, PACK_M_ROWS, D_MODEL), BF16),
                   jax.ShapeDtypeStruct((PACK_A_ROWS, D_MODEL), BF16)],
        compiler_params=_cparams(("arbitrary",)),
    )(place, *parts)


def _shard_tile(i, c):
    return jnp.where(i < 3, 3 * c + i, jnp.where(i < 6, 3 + 3 * c + i, 12 + c))


def _rs_reduce(place, land, *, name):
    def body(place_ref, l_ref, o_ref):
        acc = l_ref[0].astype(F32)
        for d in range(1, 2 * N_CHIPS):
            acc = acc + l_ref[d].astype(F32)
        o_ref[...] = acc

    grid_spec = pltpu.PrefetchScalarGridSpec(
        num_scalar_prefetch=1, grid=(HALF_ROWS // MIX_HALF,),
        in_specs=[pl.BlockSpec((2 * N_CHIPS, MIX_HALF, D_MODEL), lambda i, pr: (0, i, 0))],
        out_specs=pl.BlockSpec((MIX_HALF, D_MODEL), lambda i, pr: (_shard_tile(i, pr[1]), 0)))
    return pl.pallas_call(
        body, name=name, grid_spec=grid_spec,
        out_shape=jax.ShapeDtypeStruct((PACK_ROWS, D_MODEL), F32),
        compiler_params=_cparams(("arbitrary",)),
    )(place, land)


def _small_all_reduce(packed, shard, *, name):
    rows = packed.shape[0]
    half = rows // 2

    def body(p_ref, sh_in_ref, o_ref, sh_ref, sib_ref, slots_ref, send_sems, recv_sems, share_send, share_recv):
        x, y, c, others = _mesh_place()
        me = 2 * x + y
        sibling = (x, y, 1 - c)
        share_start, share_finish = _share_stages(sh_ref, share_send, share_recv)
        share_start()

        def half_of(core):
            return pl.ds(pl.multiple_of(core * half, 8), half)

        def remote(k, src, dst, to):
            return pltpu.make_async_remote_copy(src_ref=src, dst_ref=dst, send_sem=send_sems.at[k],
                                                recv_sem=recv_sems.at[k], device_id=to, device_id_type=MESH)

        sib = remote(0, p_ref.at[half_of(1 - c)], sib_ref, sibling)
        sib.start()
        sib.wait()
        slots_ref[me] = p_ref[half_of(c), :] + sib_ref[...]
        sends = [remote(1 + j, slots_ref.at[me], slots_ref.at[me], (px, py, c)) for j, (px, py) in enumerate(others)]
        for cp in sends:
            cp.start()
        for j, (px, py) in enumerate(others):
            slab = slots_ref.at[2 * px + py]
            remote(1 + j, slab, slab, (px, py, c)).wait_recv()
        for cp in sends:
            cp.wait_send()
        o_ref[half_of(c), :] = (slots_ref[0] + slots_ref[1]) + (slots_ref[2] + slots_ref[3])
        back = remote(4, o_ref.at[half_of(c)], o_ref.at[half_of(c)], sibling)
        back.start()
        remote(4, o_ref.at[half_of(1 - c)], o_ref.at[half_of(1 - c)], sibling).wait_recv()
        back.wait_send()
        share_finish()

    vm = pl.BlockSpec(memory_space=pltpu.VMEM)
    hbm = pl.BlockSpec(memory_space=pl.ANY)
    return pl.pallas_call(
        body, name=name, in_specs=[vm, hbm], out_specs=[vm, hbm],
        out_shape=[jax.ShapeDtypeStruct((rows, 128), F32), jax.ShapeDtypeStruct(shard.shape, shard.dtype)],
        input_output_aliases={1: 1},
        scratch_shapes=[pltpu.VMEM((half, 128), F32), pltpu.VMEM((N_CHIPS, half, 128), F32),
                        pltpu.SemaphoreType.DMA((5,)), pltpu.SemaphoreType.DMA((5,)),
                        pltpu.SemaphoreType.DMA((3,)), pltpu.SemaphoreType.DMA((3,))],
    )(packed, shard)


def _adamw(w, g, m, v, *, g_row0, tile, name):
    rows, cols = w.shape
    assert g_row0 % tile == 0 and rows % tile == 0

    def body(w_ref, g_ref, m_ref, v_ref, go_ref, d_ref, nm_ref, nv_ref):
        g_v = g_ref[...]
        m_n = ADAM_B1 * m_ref[...] + (1.0 - ADAM_B1) * g_v
        v_n = ADAM_B2 * v_ref[...] + (1.0 - ADAM_B2) * (g_v * g_v)
        m_hat = m_n / (1.0 - ADAM_B1 ** ADAM_STEP)
        v_hat = v_n / (1.0 - ADAM_B2 ** ADAM_STEP)
        d_ref[...] = -ADAM_LR * (m_hat / (jnp.sqrt(v_hat) + ADAM_EPS) + ADAM_WD * w_ref[...])
        go_ref[...] = g_v
        nm_ref[...] = m_n
        nv_ref[...] = v_n

    spec = pl.BlockSpec((tile, cols), lambda i: (i, 0))
    gspec = pl.BlockSpec((tile, cols), lambda i: (g_row0 // tile + i, 0))
    shape = jax.ShapeDtypeStruct((rows, cols), F32)
    return pl.pallas_call(
        body, name=name, grid=(rows // tile,),
        in_specs=[spec, gspec, spec, spec], out_specs=[spec] * 4, out_shape=[shape] * 4,
        compiler_params=_cparams(("arbitrary",)),
    )(w, g, m, v)


def kernel(x, ffn1_norm_g, ffn1_w_gate, ffn1_w_up, ffn1_w_down, mix_norm_g, w_in, b_in, attn_sinks, gmlp_ln_g, gmlp_ln_b, gmlp_w_s, gmlp_b_s, attn_out_norm_g, gmlp_out_norm_g, w_out, b_out, ffn2_norm_g, ffn2_w_gate, ffn2_w_up, ffn2_w_down, final_norm_g, loss_target, m_ffn1_norm_g, m_ffn1_w_gate, m_ffn1_w_up, m_ffn1_w_down, m_mix_norm_g, m_w_in, m_b_in, m_attn_sinks, m_gmlp_ln_g, m_gmlp_ln_b, m_gmlp_w_s, m_gmlp_b_s, m_attn_out_norm_g, m_gmlp_out_norm_g, m_w_out, m_b_out, m_ffn2_norm_g, m_ffn2_w_gate, m_ffn2_w_up, m_ffn2_w_down, m_final_norm_g, v_ffn1_norm_g, v_ffn1_w_gate, v_ffn1_w_up, v_ffn1_w_down, v_mix_norm_g, v_w_in, v_b_in, v_attn_sinks, v_gmlp_ln_g, v_gmlp_ln_b, v_gmlp_w_s, v_gmlp_b_s, v_attn_out_norm_g, v_gmlp_out_norm_g, v_w_out, v_b_out, v_ffn2_norm_g, v_ffn2_w_gate, v_ffn2_w_up, v_ffn2_w_down, v_final_norm_g):
    f_args = dict(locals())
    weights = {n: f_args[n] for n in [nm for nm, _ in SMALL if nm != "loss"] + list(BIG)}
    shapes = {n: weights[n].shape for n in weights}
    shapes["loss"] = ()
    place = jnp.stack([2 * lax.axis_index("x") + lax.axis_index("y"), lax.axis_index("c")]).astype(jnp.int32)

    def with_cols(name, a):
        a2 = a.reshape(a.shape[-2], a.shape[-1])
        return a2.T if BIG_TRANSPOSED[BIG.index(name)] else a2

    def natural(name, a2):
        return (a2.T if BIG_TRANSPOSED[BIG.index(name)] else a2).reshape(shapes[name])

    pack_a, pack_b, pack_m, own_a = _pack_cast(place, [with_cols(n, weights[n]) for n in BIG], name="pack_cast")
    p = {n: weights[n].reshape(1, -1) for n in ("ffn1_norm_g", "mix_norm_g", "b_in", "gmlp_ln_g", "gmlp_ln_b",
                                                "attn_out_norm_g", "gmlp_out_norm_g", "b_out", "ffn2_norm_g",
                                                "final_norm_g")}
    p["attn_sinks"] = attn_sinks.reshape(N_Q_HEADS)
    p["gmlp_w_s"] = gmlp_w_s.reshape(GMLP_GROUPS, BLK, BLK)
    p["bs_full"] = jnp.broadcast_to(gmlp_b_s.reshape(GMLP_GROUPS, BLK).T[:, :, None],
                                    (BLK, GMLP_GROUPS, GROUP_DIM)).reshape(BLK, GMLP_W)

    loss_part, dx0, land, gs = _local_step(place, x[0], loss_target[0], p, own_a, pack_a, pack_b, pack_m)

    gs["gmlp_b_s"] = jnp.sum(gs["gmlp_b_s"].reshape(BLK, GMLP_GROUPS, GROUP_DIM), axis=-1).T
    gs["attn_sinks"] = gs["attn_sinks"][:, 0]
    gs["loss"] = loss_part[0, 0]
    small_sum, shard = _small_all_reduce(_pack_small(gs), _rs_reduce(place, land, name="rs_reduce"),
                                         name="small_all_reduce")

    grad_w, delta, new_m, new_v = {}, {}, {}, {}
    off = 0
    for n, rows in zip(BIG, BIG_ROWS):
        res = _adamw(with_cols(n, weights[n]), shard, with_cols(n, f_args["m_" + n]), with_cols(n, f_args["v_" + n]),
                     g_row0=off, tile=FF_SH // 2 if rows == FF_SH else 64, name="adamw_" + n)
        grad_w[n], delta[n], new_m[n], new_v[n] = [natural(n, a) for a in res]
        off += rows
    sm = {k: {n: f_args[k + n] for n, _ in SMALL if n != "loss"} for k in ("", "m_", "v_")}
    for k in sm:
        sm[k]["loss"] = jnp.zeros((), F32)
    res = _adamw(_pack_small(sm[""]), small_sum, _pack_small(sm["m_"]), _pack_small(sm["v_"]),
                 g_row0=0, tile=SMALL_ROWS, name="adamw_small")
    small = _unpack_small(res[0], shapes)
    for dst, packed in ((grad_w, res[0]), (delta, res[1]), (new_m, res[2]), (new_v, res[3])):
        dst.update({n: a for n, a in _unpack_small(packed, shapes).items() if n != "loss"})

    order = ('ffn1_norm_g', 'ffn1_w_gate', 'ffn1_w_up', 'ffn1_w_down', 'mix_norm_g', 'w_in', 'b_in', 'attn_sinks',
             'gmlp_ln_g', 'gmlp_ln_b', 'gmlp_w_s', 'gmlp_b_s', 'attn_out_norm_g', 'gmlp_out_norm_g', 'w_out', 'b_out',
             'ffn2_norm_g', 'ffn2_w_gate', 'ffn2_w_up', 'ffn2_w_down', 'final_norm_g')
    return (small["loss"], dx0.reshape(x.shape), *[grad_w[n] for n in order], *[delta[n] for n in order],
            *[new_m[n] for n in order], *[new_v[n] for n in order])
```

```python
import functools

import jax
import jax.numpy as jnp
from jax import lax
from jax.experimental import pallas as pl
from jax.experimental.pallas import tpu as pltpu

F32 = jnp.float32
BF16 = jnp.bfloat16

D_MODEL = 1024
D_FF = 2816
N_CHIPS = 4
FF_SH = D_FF // N_CHIPS
N_Q_HEADS = 8
N_KV_HEADS = 2
REP = N_Q_HEADS // N_KV_HEADS
HEAD_DIM = 64
ATTN_W = 512
KV_W = 128
GMLP_W = 512
GMLP_GROUPS = 8
GROUP_DIM = 64
BLK = 128
MIX_FWD_BLOCKS = 2
MIX_BWD_BLOCKS = 4
IN_W = 1792
IN_SH = IN_W // N_CHIPS
OUT_SH = D_MODEL // N_CHIPS
EPS = 1e-6
FFN_RES = 0.5
ATTN_SCALE = HEAD_DIM ** -0.5

ADAM_LR = 0.001
ADAM_B1 = 0.9
ADAM_B2 = 0.999
ADAM_EPS = 1e-08
ADAM_WD = 0.01
ADAM_STEP = 10

V7X_VMEM_LIMIT = 56 * 1024 * 1024
MESH = pl.DeviceIdType.MESH


def _cparams(sem):
    return pltpu.CompilerParams(dimension_semantics=sem, vmem_limit_bytes=V7X_VMEM_LIMIT)


def _dot(a, b):
    return jnp.dot(a, b, preferred_element_type=F32)


def _dot_nt(a, b):
    return lax.dot_general(a, b, (((1,), (1,)), ((), ())), preferred_element_type=F32)


def _dot_tn(a, b):
    return lax.dot_general(a, b, (((0,), (0,)), ((), ())), preferred_element_type=F32)


def _rms(x, g):
    r = lax.rsqrt(jnp.mean(x * x, axis=-1, keepdims=True) + EPS)
    return x * r * g, r


def _rms_bwd(dh, x, r, g):
    gy = dh * g
    dx = r * gy - x * (r * r * r) * jnp.mean(gy * x, axis=-1, keepdims=True)
    dg = jnp.sum(dh * x * r, axis=0, keepdims=True)
    return dx, dg


def _const(shape):
    nd = len(shape)
    return pl.BlockSpec(shape, lambda *_: (0,) * nd)


def _rows(t, w):
    return pl.BlockSpec((t, w), lambda i: (i, 0))


PACK_ROWS = 7 * FF_SH
HALF_ROWS = PACK_ROWS // 2
FFN_HALF = 3 * FF_SH // 2
MIX_HALF = FF_SH // 2
PACK_A_ROWS = 3 * FF_SH
PACK_B_ROWS = 3 * FF_SH
PACK_M_ROWS = FF_SH
BIG = ("ffn1_w_gate", "ffn1_w_up", "ffn1_w_down", "ffn2_w_gate", "ffn2_w_up", "ffn2_w_down", "w_in", "w_out")
BIG_ROWS = (FF_SH, FF_SH, FF_SH, FF_SH, FF_SH, FF_SH, IN_SH, OUT_SH)
BIG_TRANSPOSED = (True, True, False, True, True, False, True, False)

SMALL = (("ffn1_norm_g", 1024), ("mix_norm_g", 1024), ("b_in", 1792), ("attn_sinks", 8), ("gmlp_ln_g", 512),
         ("gmlp_ln_b", 512), ("gmlp_w_s", 131072), ("gmlp_b_s", 1024), ("attn_out_norm_g", 512),
         ("gmlp_out_norm_g", 512), ("b_out", 1024), ("ffn2_norm_g", 1024), ("final_norm_g", 1024), ("loss", 1))


def _small_rows(n):
    return -(-n // 1024) * 8


SMALL_USED_ROWS = sum(_small_rows(n) for _, n in SMALL)
SMALL_ROWS = -(-SMALL_USED_ROWS // 16) * 16


def _pack_small(parts):
    out = []
    for name, n in SMALL:
        flat = parts[name].reshape(-1).astype(F32)
        rows = _small_rows(n)
        out.append(jnp.pad(flat, (0, rows * 128 - n)).reshape(rows, 128))
    if SMALL_ROWS > SMALL_USED_ROWS:
        out.append(jnp.zeros((SMALL_ROWS - SMALL_USED_ROWS, 128), F32))
    return jnp.concatenate(out, axis=0)


def _unpack_small(packed, shapes):
    res, off = {}, 0
    for name, n in SMALL:
        rows = _small_rows(n)
        res[name] = packed[off:off + rows].reshape(-1)[:n].reshape(shapes[name])
        off += rows
    return res


def _ffn_tile(x, g, wg_ref, wu_ref, wd_ref, hb_ref, a_ref, b_ref):
    h, _ = _rms(x, g)
    hb = h.astype(BF16)
    hb_ref[...] = hb
    acc = jnp.zeros(x.shape, F32)
    for j in range(N_CHIPS):
        a = _dot_nt(hb, wg_ref[j])
        b = _dot_nt(hb, wu_ref[j])
        a_ref[j] = a
        b_ref[j] = b
        f = (a * jax.nn.sigmoid(a) * b).astype(BF16)
        acc = acc + _dot(f, wd_ref[j])
    return x + FFN_RES * acc


def _ffn_saved_specs(s, tile):
    ab = pl.BlockSpec((N_CHIPS, tile, FF_SH), lambda i: (0, i, 0))
    shape = jax.ShapeDtypeStruct((N_CHIPS, s, FF_SH), F32)
    return [_rows(tile, D_MODEL), ab, ab], [jax.ShapeDtypeStruct((s, D_MODEL), BF16), shape, shape]


def _ffn_weight_specs(k0):
    one = pl.Buffered(1)
    return [pl.BlockSpec((N_CHIPS, FF_SH, D_MODEL), functools.partial(lambda kk, i: (0, kk, 0), k0 + d),
                         pipeline_mode=one) for d in range(3)]


def _mesh_place():
    x, y, c = lax.axis_index("x"), lax.axis_index("y"), lax.axis_index("c")
    others = [(1 - x, y), (x, 1 - y), (1 - x, 1 - y)]
    return x, y, c, others


def _gather_stages(o_ref, send_sems, recv_sems):
    x, y, c, others = _mesh_place()
    me = 2 * x + y
    sibling = (x, y, 1 - c)
    half_rows = o_ref.shape[1] // 2

    def half(slab, core):
        return o_ref.at[slab, pl.ds(pl.multiple_of(core * half_rows, 16), half_rows)]

    def copy(k, rows, to):
        return pltpu.make_async_remote_copy(src_ref=rows, dst_ref=rows, send_sem=send_sems.at[k],
                                            recv_sem=recv_sems.at[k], device_id=to, device_id_type=MESH)

    first = [copy(j, half(me, c), (px, py, c)) for j, (px, py) in enumerate(others)]
    passed = [copy(3 + j, half(2 * px + py, c), sibling) for j, (px, py) in enumerate(others)]

    def landed(j):
        px, py = others[j]
        copy(j, half(2 * px + py, c), (px, py, c)).wait_recv()
        passed[j].start()

    def sibling_landed(j):
        px, py = others[j]
        copy(3 + j, half(2 * px + py, 1 - c), sibling).wait_recv()

    def start():
        for cp in first:
            cp.start()

    def forward():
        for j in range(len(others)):
            landed(j)

    def finish():
        for j in range(len(others)):
            sibling_landed(j)
        for cp in first + passed:
            cp.wait_send()

    return start, forward, finish, (first, passed, landed, sibling_landed)


def _swiglu_slab(hb, wg, wu, wd):
    a = _dot_nt(hb, wg)
    b = _dot_nt(hb, wu)
    return a, b, _dot((a * jax.nn.sigmoid(a) * b).astype(BF16), wd)


def _ffn1_own(x, g, own, gather, *, tile, name):
    s = x.shape[0]
    nt = s // tile
    y_neighbour = 1

    def body(x_ref, g_ref, wg_ref, wu_ref, wd_ref, gin_ref, hb_ref, a_ref, b_ref, p_ref, gat_ref, w2_ref, hb_all_ref,
             send_sems, recv_sems, w2_sem):
        ps, i = pl.program_id(0), pl.program_id(1)
        rows = pl.ds(pl.multiple_of(i * tile, tile), tile)
        xi, yi, _, _ = _mesh_place()
        _, _, _, (first, passed, landed, sibling_landed) = _gather_stages(gat_ref, send_sems, recv_sems)

        @pl.when(jnp.logical_and(ps == 0, i == 0))
        def _():
            first[0].start()
            first[1].start()

        @pl.when(jnp.logical_and(ps == 1, i == 0))
        def _():
            first[0].wait_send()
            first[1].wait_send()
            first[2].start()
            landed(y_neighbour)
            sibling_landed(y_neighbour)
            load = pltpu.make_async_copy(gat_ref.at[2 * xi + (1 - yi)], w2_ref, w2_sem)
            load.start()
            load.wait()

        @pl.when(ps == 0)
        def _():
            h, _ = _rms(x_ref[...], g_ref[...])
            hb = h.astype(BF16)
            hb_ref[...] = hb
            hb_all_ref[rows, :] = hb
            a_ref[0], b_ref[0], part = _swiglu_slab(hb, wg_ref[...], wu_ref[...], wd_ref[...])
            p_ref[0] = x_ref[...] + FFN_RES * part

        @pl.when(ps == 1)
        def _():
            a_ref[0], b_ref[0], part = _swiglu_slab(hb_all_ref[rows, :], w2_ref[0:FF_SH, :],
                                                    w2_ref[FF_SH:2 * FF_SH, :], w2_ref[2 * FF_SH:3 * FF_SH, :])
            p_ref[0] = FFN_RES * part

        @pl.when(jnp.logical_and(ps == 1, i == nt - 1))
        def _():
            for j in (0, 2):
                landed(j)
            for j in (0, 2):
                sibling_landed(j)
            for cp in [first[2]] + passed:
                cp.wait_send()

    one = pl.Buffered(1)
    wspecs = [pl.BlockSpec((FF_SH, D_MODEL), functools.partial(lambda kk, ps, i: (kk, 0), k), pipeline_mode=one)
              for k in range(3)]
    hbm = pl.BlockSpec(memory_space=pl.ANY)
    first_pass_tiles = pl.BlockSpec((tile, D_MODEL), lambda ps, i: (jnp.where(ps == 0, i, nt - 1), 0))
    by_pass = lambda w: pl.BlockSpec((1, tile, w), lambda ps, i: (ps, i, 0))
    return pl.pallas_call(
        body, name=name, grid=(2, nt),
        in_specs=[first_pass_tiles, pl.BlockSpec((1, D_MODEL), lambda ps, i: (0, 0))] + wspecs + [hbm],
        out_specs=[first_pass_tiles, by_pass(FF_SH), by_pass(FF_SH), by_pass(D_MODEL), hbm],
        out_shape=[jax.ShapeDtypeStruct((s, D_MODEL), BF16), jax.ShapeDtypeStruct((N_CHIPS, s, FF_SH), F32),
                   jax.ShapeDtypeStruct((N_CHIPS, s, FF_SH), F32), jax.ShapeDtypeStruct((2, s, D_MODEL), F32),
                   jax.ShapeDtypeStruct(gather.shape, gather.dtype)],
        input_output_aliases={5: 4},
        scratch_shapes=[pltpu.VMEM((PACK_A_ROWS, D_MODEL), BF16), pltpu.VMEM((s, D_MODEL), BF16),
                        pltpu.SemaphoreType.DMA((6,)), pltpu.SemaphoreType.DMA((6,)), pltpu.SemaphoreType.DMA],
        compiler_params=_cparams(("arbitrary", "arbitrary")),
    )(x, g, own, own, own, gather)


def _ffn1_others(place, hb, p_own, a_all, b_all, pack, gather, *, tile, name):
    s = hb.shape[0]
    nt = s // tile
    forward_at = max(nt - 6, 0)

    def body(place_ref, hb_ref, p_ref, *rest):
        w_refs = rest[:6]
        o_ref, a_ref, b_ref, gat_ref, send_sems, recv_sems = rest[9:]
        i = pl.program_id(0)
        start, forward, finish, _ = _gather_stages(gat_ref, send_sems, recv_sems)
        pl.when(i == 0)(start)
        hb = hb_ref[...]
        a_ref[0], b_ref[0], part2 = _swiglu_slab(hb, w_refs[0][0], w_refs[1][0], w_refs[2][0])
        a_ref[1], b_ref[1], part3 = _swiglu_slab(hb, w_refs[3][0], w_refs[4][0], w_refs[5][0])
        o_ref[...] = (p_ref[0] + p_ref[1]) + FFN_RES * (part2 + part3)
        pl.when(i == forward_at)(forward)
        pl.when(i == nt - 1)(finish)

    one = pl.Buffered(1)

    def wspec(t, kk):
        return pl.BlockSpec((1, FF_SH, D_MODEL), lambda i, pr: (jnp.bitwise_xor(pr[0], t + 2), kk, 0),
                            pipeline_mode=one)

    rows = lambda w: pl.BlockSpec((tile, w), lambda i, pr: (i, 0))
    ab = pl.BlockSpec((2, tile, FF_SH), lambda i, pr: (1, i, 0))
    ab_shape = jax.ShapeDtypeStruct((N_CHIPS, s, FF_SH), F32)
    hbm = pl.BlockSpec(memory_space=pl.ANY)
    grid_spec = pltpu.PrefetchScalarGridSpec(
        num_scalar_prefetch=1, grid=(nt,),
        in_specs=[rows(D_MODEL), pl.BlockSpec((2, tile, D_MODEL), lambda i, pr: (0, i, 0))]
                 + [wspec(t, kk) for t in range(2) for kk in range(3)] + [hbm, hbm, hbm],
        out_specs=[rows(D_MODEL), ab, ab, hbm],
        scratch_shapes=[pltpu.SemaphoreType.DMA((6,)), pltpu.SemaphoreType.DMA((6,))])
    return pl.pallas_call(
        body, name=name, grid_spec=grid_spec,
        out_shape=[jax.ShapeDtypeStruct(hb.shape, F32), ab_shape, ab_shape,
                   jax.ShapeDtypeStruct(gather.shape, gather.dtype)],
        input_output_aliases={9: 1, 10: 2, 11: 3},
        compiler_params=_cparams(("arbitrary",)),
    )(place, hb, p_own, *([pack] * 6), a_all, b_all, gather)


def _ffn_fwd_loss(x, g, pack, k0, gf, tgt, *, tile, name):
    s = x.shape[0]

    def body(x_ref, g_ref, wg_ref, wu_ref, wd_ref, gf_ref, t_ref, dx_ref, loss_ref, dgf_ref, hb_ref, a_ref, b_ref,
             do_ref):
        @pl.when(pl.program_id(0) == 0)
        def _():
            loss_ref[...] = jnp.zeros_like(loss_ref)
            dgf_ref[...] = jnp.zeros_like(dgf_ref)

        x3 = _ffn_tile(x_ref[...], g_ref[...], wg_ref, wu_ref, wd_ref, hb_ref, a_ref, b_ref)
        gf_v = gf_ref[...]
        out, r = _rms(x3, gf_v)
        diff = out - t_ref[...]
        part = jnp.sum(jnp.sum(diff * diff, axis=-1, keepdims=True), axis=0, keepdims=True)
        loss_ref[...] += jnp.broadcast_to(part * (0.5 / D_MODEL), loss_ref.shape)
        dx, dg = _rms_bwd(diff * (1.0 / D_MODEL), x3, r, gf_v)
        dx_ref[...] = dx
        do_ref[...] = (FFN_RES * dx).astype(BF16)
        dgf_ref[...] += dg

    saved_specs, saved_shapes = _ffn_saved_specs(s, tile)
    return pl.pallas_call(
        body, name=name, grid=(s // tile,),
        in_specs=[_rows(tile, D_MODEL), _const((1, D_MODEL))] + _ffn_weight_specs(k0)
                 + [_const((1, D_MODEL)), _rows(tile, D_MODEL)],
        out_specs=[_rows(tile, D_MODEL), _const((1, 128)), _const((1, D_MODEL))] + saved_specs
                  + [_rows(tile, D_MODEL)],
        out_shape=[jax.ShapeDtypeStruct(x.shape, F32),
                   jax.ShapeDtypeStruct((1, 128), F32),
                   jax.ShapeDtypeStruct((1, D_MODEL), F32)] + saved_shapes
                  + [jax.ShapeDtypeStruct(x.shape, BF16)],
        compiler_params=_cparams(("arbitrary",)),
    )(x, g, pack, pack, pack, gf, tgt)


def _ffn_bwd(place, hb, a, b, do, pack, region, land, mix_grads, ab_by_pass=False, *, tile, name):
    s = hb.shape[0]
    nt = s // tile
    land_rows = pl.ds(region * FFN_HALF, FFN_HALF)
    mix_rows = pl.ds(2 * FFN_HALF, MIX_HALF)
    with_mix = mix_grads is not None
    with_land = land is not None
    n_others = 2 * N_CHIPS - 1

    def body(place_ref, hb_ref, a_ref, b_ref, do_ref, wg_ref, wu_ref, wd_ref, *rest):
        rest = list(rest)
        mix_ref = rest.pop(0) if with_mix else None
        if with_land:
            rest.pop(0)
        dhp_ref, land_ref, acc_ref, stage_ref, send_sems, recv_sem, local_sem = rest[:7]
        t, i = pl.program_id(0), pl.program_id(1)
        xi, yi, c = lax.axis_index("x"), lax.axis_index("y"), lax.axis_index("c")
        dev = 4 * xi + 2 * yi + c
        tt = (t + 1) % N_CHIPS
        tx, ty = jnp.bitwise_xor(xi, tt // 2), jnp.bitwise_xor(yi, tt % 2)

        def remote(src, dst, ssem, rsem, to):
            return pltpu.make_async_remote_copy(src_ref=src, dst_ref=dst, send_sem=ssem, recv_sem=rsem,
                                                device_id=to, device_id_type=MESH)

        def stage_half(h):
            return stage_ref.at[pl.ds(pl.multiple_of(h * FFN_HALF, 16), FFN_HALF)]

        if with_mix:
            mix_send, mix_recv, mix_local = rest[7:10]

            @pl.when(jnp.logical_and(t == 0, i == 0))
            def _():
                for chip in range(N_CHIPS):
                    for h in range(2):
                        src = mix_ref.at[chip, pl.ds(h * MIX_HALF, MIX_HALF)]
                        dst = land_ref.at[dev, mix_rows]
                        mine = jnp.logical_and(2 * xi + yi == chip, c == h)

                        @pl.when(mine)
                        def _():
                            pltpu.make_async_copy(src, dst, mix_local).start()

                        @pl.when(jnp.logical_not(mine))
                        def _():
                            remote(src, dst, mix_send, mix_recv, (chip // 2, chip % 2, h)).start()

        @pl.when(i == 0)
        def _():
            acc_ref[...] = jnp.zeros_like(acc_ref)

        hb = hb_ref[...]
        dob = do_ref[...]
        wg_j, wu_j, wd_j = wg_ref[0], wu_ref[0], wd_ref[0]
        a = a_ref[0]
        b = b_ref[0]
        sg = jax.nn.sigmoid(a)
        sa = a * sg
        fb = (sa * b).astype(BF16)
        df = _dot_nt(dob, wd_j)
        dbb = (df * sa).astype(BF16)
        dab = (df * b * (sg + sa * (1.0 - sg))).astype(BF16)
        dhp_ref[0] = (_dot(dab, wg_j) + _dot(dbb, wu_j)).astype(BF16)
        acc_ref[0:FF_SH, :] += _dot_tn(dab, hb)
        acc_ref[FF_SH:2 * FF_SH, :] += _dot_tn(dbb, hb)
        acc_ref[2 * FF_SH:3 * FF_SH, :] += _dot_tn(fb, dob)

        @pl.when(i == nt - 1)
        def _():
            dst = land_ref.at[dev, land_rows]

            @pl.when(t > 0)
            def _():
                for h in range(2):
                    remote(stage_half(h), dst, send_sems.at[h], recv_sem, (tx, ty, h)).wait_send()

            def cast_rows(r, carry):
                rows = pl.ds(pl.multiple_of(r * MIX_HALF, 16), MIX_HALF)
                stage_ref[rows, :] = acc_ref[rows, :].astype(BF16)
                return carry

            lax.fori_loop(0, 3 * FF_SH // MIX_HALF, cast_rows, 0)

            @pl.when(t < N_CHIPS - 1)
            def _():
                for h in range(2):
                    remote(stage_half(h), dst, send_sems.at[h], recv_sem, (tx, ty, h)).start()

            @pl.when(t == N_CHIPS - 1)
            def _():
                own = pltpu.make_async_copy(stage_half(c), dst, local_sem)
                own.start()
                sib = remote(stage_half(1 - c), dst, send_sems.at[0], recv_sem, (xi, yi, 1 - c))
                sib.start()
                sib.wait_send()
                own.wait()
                arrivals = land_ref.at[pl.ds(0, n_others), land_rows]
                remote(arrivals, arrivals, send_sems.at[0], recv_sem, (xi, yi, 1 - c)).wait_recv()
                if with_mix:
                    seven = land_ref.at[pl.ds(0, n_others), mix_rows]
                    both = remote(seven, seven, mix_send, mix_recv, (xi, yi, 1 - c))
                    both.wait_send()
                    both.wait_recv()
                    pltpu.make_async_copy(mix_ref.at[0, pl.ds(0, MIX_HALF)], land_ref.at[dev, mix_rows],
                                          mix_local).wait()

    def wspec(kk):
        return pl.BlockSpec((1, FF_SH, D_MODEL),
                            lambda t, i, pr: (jnp.bitwise_xor(pr[0], (t + 1) % N_CHIPS), kk, 0))

    xspec = pl.BlockSpec((tile, D_MODEL), lambda t, i, pr: (i, 0))
    if ab_by_pass:
        abspec = pl.BlockSpec((1, tile, FF_SH), lambda t, i, pr: ((t + 1) % N_CHIPS, i, 0))
    else:
        abspec = pl.BlockSpec((1, tile, FF_SH), lambda t, i, pr: (jnp.bitwise_xor(pr[0], (t + 1) % N_CHIPS), i, 0))
    hbm = pl.BlockSpec(memory_space=pl.ANY)
    in_specs = [xspec, abspec, abspec, xspec, wspec(0), wspec(1), wspec(2)]
    operands = [place, hb, a, b, do, pack, pack, pack]
    scratch = [pltpu.VMEM((3 * FF_SH, D_MODEL), F32), pltpu.VMEM((3 * FF_SH, D_MODEL), BF16),
               pltpu.SemaphoreType.DMA((2,)), pltpu.SemaphoreType.DMA, pltpu.SemaphoreType.DMA]
    if with_mix:
        in_specs.append(hbm)
        operands.append(mix_grads)
        scratch += [pltpu.SemaphoreType.DMA, pltpu.SemaphoreType.DMA, pltpu.SemaphoreType.DMA]
    aliases = {}
    if with_land:
        in_specs.append(hbm)
        operands.append(land)
        aliases = {len(operands) - 1: 1}
    grid_spec = pltpu.PrefetchScalarGridSpec(
        num_scalar_prefetch=1, grid=(N_CHIPS, nt), in_specs=in_specs,
        out_specs=[pl.BlockSpec((1, tile, D_MODEL), lambda t, i, pr: (t, i, 0)), hbm],
        scratch_shapes=scratch)
    return pl.pallas_call(
        body, name=name, grid_spec=grid_spec,
        out_shape=[jax.ShapeDtypeStruct((N_CHIPS, s, D_MODEL), BF16),
                   jax.ShapeDtypeStruct((2 * N_CHIPS, HALF_ROWS, D_MODEL), BF16)],
        input_output_aliases=aliases,
        compiler_params=_cparams(("arbitrary", "arbitrary")),
    )(*operands)


def _mix_grads_pack(dw_in_t, dw_out, *, name):
    def body(a_ref, b_ref, o_ref):
        o_ref[0, 0:IN_SH, :] = a_ref[0].astype(BF16)
        o_ref[0, IN_SH:FF_SH, :] = b_ref[0].astype(BF16)

    return pl.pallas_call(
        body, name=name, grid=(N_CHIPS,),
        in_specs=[pl.BlockSpec((1, IN_SH, D_MODEL), lambda j: (j, 0, 0)),
                  pl.BlockSpec((1, OUT_SH, D_MODEL), lambda j: (j, 0, 0))],
        out_specs=pl.BlockSpec((1, FF_SH, D_MODEL), lambda j: (j, 0, 0)),
        out_shape=jax.ShapeDtypeStruct((N_CHIPS, FF_SH, D_MODEL), BF16),
        compiler_params=_cparams(("arbitrary",)),
    )(dw_in_t.reshape(N_CHIPS, IN_SH, D_MODEL), dw_out.reshape(N_CHIPS, OUT_SH, D_MODEL))


def _share_stages(o_ref, send_sems, recv_sems):
    x, y, c, _ = _mesh_place()

    def rows(k, core):
        if k < 2:
            return o_ref.at[pl.ds(pl.multiple_of(k * 2 * FFN_HALF + core * FFN_HALF, 8), FFN_HALF)]
        return o_ref.at[pl.ds(pl.multiple_of(4 * FFN_HALF + core * MIX_HALF, 8), MIX_HALF)]

    def copy(k, core):
        return pltpu.make_async_remote_copy(src_ref=rows(k, core), dst_ref=rows(k, core), send_sem=send_sems.at[k],
                                            recv_sem=recv_sems.at[k], device_id=(x, y, 1 - c), device_id_type=MESH)

    sends = [copy(k, c) for k in range(3)]

    def start():
        for cp in sends:
            cp.start()

    def finish():
        for k in range(3):
            copy(k, 1 - c).wait_recv()
        for cp in sends:
            cp.wait_send()

    return start, finish


def _norm_bwd(dhp, x, dy, g, *, tile, name):
    s = x.shape[0]

    def body(dhp_ref, x_ref, dy_ref, g_ref, dx_ref, dg_ref):
        @pl.when(pl.program_id(0) == 0)
        def _():
            dg_ref[...] = jnp.zeros_like(dg_ref)

        dh = ((dhp_ref[0].astype(F32) + dhp_ref[1].astype(F32))
              + (dhp_ref[2].astype(F32) + dhp_ref[3].astype(F32)))
        x_v = x_ref[...]
        r = lax.rsqrt(jnp.mean(x_v * x_v, axis=-1, keepdims=True) + EPS)
        dx, dg = _rms_bwd(dh, x_v, r, g_ref[...])
        dx_ref[...] = dy_ref[...] + dx
        dg_ref[...] += dg

    return pl.pallas_call(
        body, name=name, grid=(s // tile,),
        in_specs=[pl.BlockSpec((N_CHIPS, tile, D_MODEL), lambda i: (0, i, 0)),
                  _rows(tile, D_MODEL), _rows(tile, D_MODEL), _const((1, D_MODEL))],
        out_specs=[_rows(tile, D_MODEL), _const((1, D_MODEL))],
        out_shape=[jax.ShapeDtypeStruct(x.shape, F32), jax.ShapeDtypeStruct((1, D_MODEL), F32)],
        compiler_params=_cparams(("arbitrary",)),
    )(dhp, x, dy, g)


def _mix_in_bwd(x, dy, dq, dk, dv, dz, g, w_in_t, *, tile, name):
    s = x.shape[0]

    def body(x_ref, dy_ref, dq_ref, dk_ref, dv_ref, dz_ref, g_ref, w_ref, dx_ref, dw_ref, db_ref, dg_ref, do_ref):
        @pl.when(pl.program_id(0) == 0)
        def _():
            dw_ref[...] = jnp.zeros_like(dw_ref)
            db_ref[...] = jnp.zeros_like(db_ref)
            dg_ref[...] = jnp.zeros_like(dg_ref)

        dproj = jnp.concatenate([dq_ref[...], dk_ref[...], dv_ref[...], dz_ref[...]], axis=-1)
        db_ref[...] += jnp.sum(dproj, axis=0, keepdims=True)
        dpb = dproj.astype(BF16)
        x_v = x_ref[...]
        g_v = g_ref[...]
        h, r = _rms(x_v, g_v)
        dw_ref[...] += _dot_tn(dpb, h.astype(BF16))
        dh = _dot(dpb, w_ref[...])
        dxn, dg = _rms_bwd(dh, x_v, r, g_v)
        dx = dy_ref[...] + dxn
        dx_ref[...] = dx
        do_ref[...] = (FFN_RES * dx).astype(BF16)
        dg_ref[...] += dg

    return pl.pallas_call(
        body, name=name, grid=(s // tile,),
        in_specs=[_rows(tile, D_MODEL), _rows(tile, D_MODEL), _rows(tile, ATTN_W), _rows(tile, KV_W),
                  _rows(tile, KV_W), _rows(tile, 2 * GMLP_W), _const((1, D_MODEL)), _const((IN_W, D_MODEL))],
        out_specs=[_rows(tile, D_MODEL), _const((IN_W, D_MODEL)), _const((1, IN_W)), _const((1, D_MODEL)),
                   _rows(tile, D_MODEL)],
        out_shape=[jax.ShapeDtypeStruct(x.shape, F32), jax.ShapeDtypeStruct((IN_W, D_MODEL), F32),
                   jax.ShapeDtypeStruct((1, IN_W), F32), jax.ShapeDtypeStruct((1, D_MODEL), F32),
                   jax.ShapeDtypeStruct(x.shape, BF16)],
        compiler_params=_cparams(("arbitrary",)),
    )(x, dy, dq, dk, dv, dz, g, w_in_t)


_GELU_C = 0.7978845608028654
_GELU_A = 0.044715


def _gelu_tanh(x):
    x2 = x * x
    return jnp.tanh(_GELU_C * (x + _GELU_A * (x2 * x))), x2


def _band(ref, i):
    prev = jnp.maximum(i - 1, 0)
    return jnp.concatenate([ref[pl.ds(pl.multiple_of(prev * BLK, BLK), BLK), :],
                            ref[pl.ds(pl.multiple_of(i * BLK, BLK), BLK), :]], axis=0)


def _key_in_block():
    return lax.broadcasted_iota(jnp.int32, (BLK, BLK), 0) <= lax.broadcasted_iota(jnp.int32, (BLK, BLK), 1)


def _fold(band, own):
    return jnp.where(own, band[BLK:], band[:BLK])


def _unfold(a, own):
    zero = jnp.zeros_like(a)
    return jnp.concatenate([jnp.where(own, zero, a), jnp.where(own, a, zero)], axis=0).astype(BF16)


def _attn_fwd(q, kb, vb, i, sink_ref):
    own = _key_in_block()
    outs, saved = [], []
    for h in range(N_Q_HEADS):
        cols = slice((h // REP) * HEAD_DIM, (h // REP + 1) * HEAD_DIM)
        s2 = _dot_nt(kb[:, cols], q[:, h * HEAD_DIM:(h + 1) * HEAD_DIM])
        sc = jnp.where(own, s2[BLK:], jnp.where(i > 0, s2[:BLK], -jnp.inf))
        sink = sink_ref[h]
        m = jnp.maximum(jnp.max(sc, axis=0, keepdims=True), sink)
        p = jnp.exp(sc - m)
        es = jnp.exp(sink - m)
        inv = 1.0 / (jnp.sum(p, axis=0, keepdims=True) + es)
        pn = p * inv
        band = _unfold(pn, own)
        outs.append(_dot_tn(band, vb[:, cols]))
        saved.append((pn, band, es * inv))
    return jnp.concatenate(outs, axis=-1), saved


def _tril_mask():
    t = lax.broadcasted_iota(jnp.int32, (BLK, BLK), 0)
    s_ = lax.broadcasted_iota(jnp.int32, (BLK, BLK), 1)
    return s_ <= t


def _gmlp_fwd_parts(zg, lng, lnb, ws_ref, bs_full):
    th, zg2 = _gelu_tanh(zg)
    z = 0.5 * zg * (1.0 + th)
    u = z[:, :GMLP_W]
    zv = z[:, GMLP_W:]
    mu = jnp.mean(zv, axis=-1, keepdims=True)
    zc = zv - mu
    rstd = lax.rsqrt(jnp.mean(zc * zc, axis=-1, keepdims=True) + EPS)
    xh = zc * rstd
    vvb = (xh * lng + lnb).astype(BF16)
    tril = _tril_mask()
    wms, parts = [], []
    for gi in range(GMLP_GROUPS):
        wm = jnp.where(tril, ws_ref[gi], 0.0).astype(BF16)
        wms.append(wm)
        parts.append(_dot(wm, vvb[:, gi * GROUP_DIM:(gi + 1) * GROUP_DIM]))
    mixed = jnp.concatenate(parts, axis=-1) + bs_full
    gelu_grad = 0.5 * (1.0 + th) + 0.5 * zg * (1.0 - th * th) * (_GELU_C * (1.0 + 3.0 * _GELU_A * zg2))
    return u, xh, rstd, vvb, wms, mixed, gelu_grad


def _mixer_fwd(x1, g, w_in_t, b_in, sinks, lng, lnb, w_s, bs_full, gao, ggo, w_out, b_out, gather, *, name):
    s = x1.shape[0]
    nb = min(MIX_FWD_BLOCKS, s // BLK)
    step_rows = nb * BLK
    last = s // step_rows - 1

    def tile_of(i, lag):
        return jnp.clip(i - lag, 0, last)

    def body(sink_ref, xa_ref, xc_ref, g_ref, wi_ref, bi_ref, lng_ref, lnb_ref, ws_ref, bs_ref, gao_ref, ggo_ref,
             wo_ref, bo_ref, gin_ref, q_ref, k_ref, v_ref, z_ref, y_ref, o_ref, gat_ref, qs_ref, zs_ref, ys_ref,
             send_sems, recv_sems):
        i = pl.program_id(0)
        start, forward, finish, _ = _gather_stages(gat_ref, send_sems, recv_sems)

        @pl.when(i == 0)
        def _():
            for ref in (k_ref, v_ref, qs_ref, zs_ref, ys_ref):
                ref[...] = jnp.zeros_like(ref)
            start()

        slot_a, slot_b, slot_c = i % 2, (i + 1) % 2, i % 2

        o_ref[...] = xc_ref[...] + (_dot(ys_ref[slot_c], wo_ref[...]) + bo_ref[...])

        tile_b = tile_of(i, 1)
        for b in range(nb):
            blk = tile_b * nb + b
            rows = slice(b * BLK, (b + 1) * BLK)
            y_attn, _ = _attn_fwd(qs_ref[slot_b, rows, :], _band(k_ref, blk), _band(v_ref, blk), blk, sink_ref)
            u, _, _, _, _, mixed, _ = _gmlp_fwd_parts(zs_ref[slot_b, rows, :], lng_ref[...], lnb_ref[...], ws_ref,
                                                      bs_ref[...])
            ya, _ = _rms(y_attn, gao_ref[...])
            yg, _ = _rms(u * mixed, ggo_ref[...])
            y_blk = jnp.concatenate([ya, yg], axis=-1).astype(BF16)
            y_ref[rows, :] = y_blk
            ys_ref[slot_b, rows, :] = y_blk

        h, _ = _rms(xa_ref[...], g_ref[...])
        proj = _dot_nt(h.astype(BF16), wi_ref[...]) + bi_ref[...]
        q_t = (proj[:, :ATTN_W] * ATTN_SCALE).astype(BF16)
        z_t = proj[:, ATTN_W + 2 * KV_W:]
        here = pl.ds(pl.multiple_of(tile_of(i, 0) * step_rows, step_rows), step_rows)
        q_ref[...] = q_t
        z_ref[...] = z_t
        qs_ref[slot_a] = q_t
        zs_ref[slot_a] = z_t
        k_ref[here, :] = proj[:, ATTN_W:ATTN_W + KV_W].astype(BF16)
        v_ref[here, :] = proj[:, ATTN_W + KV_W:ATTN_W + 2 * KV_W].astype(BF16)
        pl.when(i == max(last - 3, 0))(forward)
        pl.when(i == last + 2)(finish)

    def lagged(width, lag):
        return pl.BlockSpec((step_rows, width), lambda i: (tile_of(i, lag), 0))

    return pl.pallas_call(
        body, name=name, grid=(last + 3,),
        in_specs=[pl.BlockSpec(memory_space=pltpu.SMEM),
                  lagged(D_MODEL, 0), lagged(D_MODEL, 2), _const((1, D_MODEL)), _const((IN_W, D_MODEL)),
                  _const((1, IN_W)), _const((1, GMLP_W)), _const((1, GMLP_W)), _const((GMLP_GROUPS, BLK, BLK)),
                  _const((BLK, GMLP_W)), _const((1, ATTN_W)), _const((1, GMLP_W)), _const((D_MODEL, D_MODEL)),
                  _const((1, D_MODEL)), pl.BlockSpec(memory_space=pl.ANY)],
        out_specs=[lagged(ATTN_W, 0), _const((s, KV_W)), _const((s, KV_W)), lagged(2 * GMLP_W, 0),
                   lagged(D_MODEL, 1), lagged(D_MODEL, 2), pl.BlockSpec(memory_space=pl.ANY)],
        out_shape=[jax.ShapeDtypeStruct((s, ATTN_W), BF16), jax.ShapeDtypeStruct((s, KV_W), BF16),
                   jax.ShapeDtypeStruct((s, KV_W), BF16), jax.ShapeDtypeStruct((s, 2 * GMLP_W), F32),
                   jax.ShapeDtypeStruct((s, D_MODEL), BF16), jax.ShapeDtypeStruct((s, D_MODEL), F32),
                   jax.ShapeDtypeStruct(gather.shape, gather.dtype)],
        input_output_aliases={14: 6},
        scratch_shapes=[pltpu.VMEM((2, step_rows, ATTN_W), BF16), pltpu.VMEM((2, step_rows, 2 * GMLP_W), F32),
                        pltpu.VMEM((2, step_rows, D_MODEL), BF16),
                        pltpu.SemaphoreType.DMA((6,)), pltpu.SemaphoreType.DMA((6,))],
        compiler_params=_cparams(("arbitrary",)),
    )(sinks, x1, x1, g, w_in_t, b_in, lng, lnb, w_s, bs_full, gao, ggo, w_out, b_out, gather)


def _norm_bwd_mix_out(dhp, x, dy, g, yb, w_out, *, tile, name):
    s = x.shape[0]
    nt = s // tile
    ring = 3

    def body(dhp_hbm, x_ref, dy_ref, g_ref, y_ref, w_ref, dx_ref, dg_ref, dyy_ref, dw_ref, db_ref, dhp_ring, sems):
        i = pl.program_id(0)

        def fetch(step):
            slot = step % ring
            src = dhp_hbm.at[:, pl.ds(pl.multiple_of(step * tile, tile), tile), :]
            return pltpu.make_async_copy(src, dhp_ring.at[slot], sems.at[slot])

        @pl.when(i == 0)
        def _():
            for step in range(min(ring - 1, nt)):
                fetch(step).start()
            dg_ref[...] = jnp.zeros_like(dg_ref)
            dw_ref[...] = jnp.zeros_like(dw_ref)
            db_ref[...] = jnp.zeros_like(db_ref)

        @pl.when(i + ring - 1 < nt)
        def _():
            fetch(i + ring - 1).start()

        fetch(i).wait()
        dhp_ref = dhp_ring.at[i % ring]
        dh = ((dhp_ref[0].astype(F32) + dhp_ref[1].astype(F32))
              + (dhp_ref[2].astype(F32) + dhp_ref[3].astype(F32)))
        x_v = x_ref[...]
        r = lax.rsqrt(jnp.mean(x_v * x_v, axis=-1, keepdims=True) + EPS)
        dxn, dg = _rms_bwd(dh, x_v, r, g_ref[...])
        dx = dy_ref[...] + dxn
        dx_ref[...] = dx
        dg_ref[...] += dg
        dxb = dx.astype(BF16)
        db_ref[...] += jnp.sum(dx, axis=0, keepdims=True)
        dw_ref[...] += _dot_tn(y_ref[...], dxb)
        dyy_ref[...] = _dot_nt(dxb, w_ref[...])

    return pl.pallas_call(
        body, name=name, grid=(nt,),
        in_specs=[pl.BlockSpec(memory_space=pl.ANY),
                  _rows(tile, D_MODEL), _rows(tile, D_MODEL), _const((1, D_MODEL)), _rows(tile, D_MODEL),
                  _const((D_MODEL, D_MODEL))],
        out_specs=[_rows(tile, D_MODEL), _const((1, D_MODEL)), _rows(tile, D_MODEL), _const((D_MODEL, D_MODEL)),
                   _const((1, D_MODEL))],
        scratch_shapes=[pltpu.VMEM((ring, N_CHIPS, tile, D_MODEL), dhp.dtype), pltpu.SemaphoreType.DMA((ring,))],
        out_shape=[jax.ShapeDtypeStruct(x.shape, F32), jax.ShapeDtypeStruct((1, D_MODEL), F32),
                   jax.ShapeDtypeStruct(x.shape, F32), jax.ShapeDtypeStruct((D_MODEL, D_MODEL), F32),
                   jax.ShapeDtypeStruct((1, D_MODEL), F32)],
        compiler_params=_cparams(("arbitrary",)),
    )(dhp, x, dy, g, yb, w_out)


def _mix_core_bwd(dyy, q, k, v, zg, sinks, lng, lnb, w_s, bs_full, gao, ggo, *, name):
    s = dyy.shape[0]
    nb = min(MIX_BWD_BLOCKS, s // BLK)
    nsteps = s // (nb * BLK)

    def body(*refs):
        accumulators = refs[13:15] + refs[16:]

        @pl.when(pl.program_id(0) == 0)
        def _():
            for ref in accumulators:
                ref[...] = jnp.zeros_like(ref)

        for b in range(nb):
            one_block(pl.program_id(0) * nb + b, slice(b * BLK, (b + 1) * BLK), *refs)

        @pl.when(pl.program_id(0) == nsteps - 1)
        def _():
            tril = _tril_mask()
            for gi in range(GMLP_GROUPS):
                refs[20][gi] = jnp.where(tril, refs[20][gi], 0.0)

    def one_block(i, rows, sink_ref, dyy_ref, q_ref, k_ref, v_ref, z_ref, lng_ref, lnb_ref, ws_ref, bs_ref, gao_ref,
                  ggo_ref, dq_ref, dk_ref, dv_ref, dz_ref, dgao_ref, dggo_ref, dlng_ref, dlnb_ref, dws_ref, dms_ref,
                  dsk_ref):
        q_v = q_ref[rows, :]
        kb = _band(k_ref, i)
        vb = _band(v_ref, i)
        lng_v = lng_ref[...]
        gao_v = gao_ref[...]
        ggo_v = ggo_ref[...]

        y_attn, probs = _attn_fwd(q_v, kb, vb, i, sink_ref)
        u, xh, rstd, vvb, wms, mixed, gelu_grad = _gmlp_fwd_parts(z_ref[rows, :], lng_v, lnb_ref[...], ws_ref,
                                                                  bs_ref[...])
        y_gmlp = u * mixed
        ra = lax.rsqrt(jnp.mean(y_attn * y_attn, axis=-1, keepdims=True) + EPS)
        rg = lax.rsqrt(jnp.mean(y_gmlp * y_gmlp, axis=-1, keepdims=True) + EPS)

        dyy = dyy_ref[rows, :]
        d_attn, dgao = _rms_bwd(dyy[:, :ATTN_W], y_attn, ra, gao_v)
        d_gmlp, dggo = _rms_bwd(dyy[:, ATTN_W:], y_gmlp, rg, ggo_v)
        dgao_ref[...] += dgao
        dggo_ref[...] += dggo

        du = d_gmlp * mixed
        dmixed = d_gmlp * u
        dms_ref[...] += dmixed
        dmb = dmixed.astype(BF16)
        dvv_parts = []
        for gi in range(GMLP_GROUPS):
            sl = slice(gi * GROUP_DIM, (gi + 1) * GROUP_DIM)
            dws_ref[gi] += _dot_nt(dmb[:, sl], vvb[:, sl])
            dvv_parts.append(_dot_tn(wms[gi], dmb[:, sl]))
        dvv = jnp.concatenate(dvv_parts, axis=-1)
        dlng_ref[...] += jnp.sum(dvv * xh, axis=0, keepdims=True)
        dlnb_ref[...] += jnp.sum(dvv, axis=0, keepdims=True)
        dxh = dvv * lng_v
        dzv = rstd * (dxh - jnp.mean(dxh, axis=-1, keepdims=True)
                      - xh * jnp.mean(dxh * xh, axis=-1, keepdims=True))
        dz_ref[rows, :] = jnp.concatenate([du, dzv], axis=-1) * gelu_grad

        dab = d_attn.astype(BF16)
        own = _key_in_block()
        dq_parts = []
        dk_parts = []
        dv_parts = []
        for gi in range(N_KV_HEADS):
            cols = slice(gi * HEAD_DIM, (gi + 1) * HEAD_DIM)
            kg, vg = kb[:, cols], vb[:, cols]
            dkg = jnp.zeros((2 * BLK, HEAD_DIM), F32)
            dvg = jnp.zeros((2 * BLK, HEAD_DIM), F32)
            for rr in range(REP):
                h = gi * REP + rr
                hs = slice(h * HEAD_DIM, (h + 1) * HEAD_DIM)
                qh, doh = q_v[:, hs], dab[:, hs]
                pn, band, psink = probs[h]
                dp = _fold(_dot_nt(vg, doh), own)
                delta = jnp.sum(pn * dp, axis=0, keepdims=True)
                ds2 = _unfold(pn * (dp - delta), own)
                dsink = jnp.sum(-psink * delta, axis=-1, keepdims=True)
                dsk_ref[pl.ds(h, 1), :] += jnp.broadcast_to(dsink, (1, 128))
                dq_parts.append(_dot_tn(ds2, kg) * ATTN_SCALE)
                dkg = dkg + _dot(ds2, qh)
                dvg = dvg + _dot(band, doh)
            dk_parts.append(dkg)
            dv_parts.append(dvg)
        dq_ref[rows, :] = jnp.concatenate(dq_parts, axis=-1)
        dkb = jnp.concatenate(dk_parts, axis=-1)
        dvb = jnp.concatenate(dv_parts, axis=-1)
        prev = pl.ds(pl.multiple_of(jnp.maximum(i - 1, 0) * BLK, BLK), BLK)
        cur = pl.ds(pl.multiple_of(i * BLK, BLK), BLK)
        dk_ref[prev, :] += dkb[:BLK]
        dv_ref[prev, :] += dvb[:BLK]
        dk_ref[cur, :] += dkb[BLK:]
        dv_ref[cur, :] += dvb[BLK:]

    return pl.pallas_call(
        body, name=name, grid=(nsteps,),
        in_specs=[pl.BlockSpec(memory_space=pltpu.SMEM),
                  _rows(nb * BLK, D_MODEL), _rows(nb * BLK, ATTN_W), _const((s, KV_W)), _const((s, KV_W)),
                  _rows(nb * BLK, 2 * GMLP_W), _const((1, GMLP_W)), _const((1, GMLP_W)),
                  _const((GMLP_GROUPS, BLK, BLK)), _const((BLK, GMLP_W)), _const((1, ATTN_W)), _const((1, GMLP_W))],
        out_specs=[_rows(nb * BLK, ATTN_W), _const((s, KV_W)), _const((s, KV_W)), _rows(nb * BLK, 2 * GMLP_W),
                   _const((1, ATTN_W)), _const((1, GMLP_W)),
                   _const((1, GMLP_W)), _const((1, GMLP_W)), _const((GMLP_GROUPS, BLK, BLK)),
                   _const((BLK, GMLP_W)), _const((N_Q_HEADS, 128))],
        out_shape=[jax.ShapeDtypeStruct((s, ATTN_W), F32), jax.ShapeDtypeStruct((s, KV_W), F32),
                   jax.ShapeDtypeStruct((s, KV_W), F32), jax.ShapeDtypeStruct((s, 2 * GMLP_W), F32),
                   jax.ShapeDtypeStruct((1, ATTN_W), F32), jax.ShapeDtypeStruct((1, GMLP_W), F32),
                   jax.ShapeDtypeStruct((1, GMLP_W), F32), jax.ShapeDtypeStruct((1, GMLP_W), F32),
                   jax.ShapeDtypeStruct((GMLP_GROUPS, BLK, BLK), F32), jax.ShapeDtypeStruct((BLK, GMLP_W), F32),
                   jax.ShapeDtypeStruct((N_Q_HEADS, 128), F32)],
        compiler_params=_cparams(("arbitrary",)),
    )(sinks, dyy, q, k, v, zg, lng, lnb, w_s, bs_full, gao, ggo)


def _local_step(place, x, tgt, p, own_a, pack_a, pack_b, pack_m, *, tile=512, fwd_tile=256, bwd_tile=512,
                norm_tile=512):
    g = {}
    tile, fwd_tile, bwd_tile, norm_tile = (min(t_, x.shape[0]) for t_ in (tile, fwd_tile, bwd_tile, norm_tile))
    hb1, a1, b1, part1, pack_a = _ffn1_own(x, p["ffn1_norm_g"], own_a, pack_a, tile=tile, name="ffn1_own")
    x1, a1, b1, pack_m = _ffn1_others(place, hb1, part1, a1, b1, pack_a, pack_m, tile=tile, name="ffn1_fwd")
    w_in_t = pack_m[:, :IN_SH, :].reshape(IN_W, D_MODEL)
    w_out = pack_m[:, IN_SH:, :].reshape(D_MODEL, D_MODEL)
    q, k, v, zg, yb, x2, pack_b = _mixer_fwd(
        x1, p["mix_norm_g"], w_in_t, p["b_in"], p["attn_sinks"], p["gmlp_ln_g"], p["gmlp_ln_b"], p["gmlp_w_s"],
        p["bs_full"], p["attn_out_norm_g"], p["gmlp_out_norm_g"], w_out, p["b_out"], pack_b, name="mixer_fwd")
    mix_args = (q, k, v, zg, p["attn_sinks"], p["gmlp_ln_g"], p["gmlp_ln_b"], p["gmlp_w_s"], p["bs_full"],
                p["attn_out_norm_g"], p["gmlp_out_norm_g"])
    dx3, loss, g["final_norm_g"], hb2, a2, b2, do3 = _ffn_fwd_loss(
        x2, p["ffn2_norm_g"], pack_b, 0, p["final_norm_g"], tgt, tile=fwd_tile, name="ffn2_fwd_loss")

    dhp, land = _ffn_bwd(place, hb2, a2, b2, do3, pack_b, 1, None, None, tile=bwd_tile, name="ffn2_bwd")
    dx2, g["ffn2_norm_g"], dyy, dw_out, g["b_out"] = _norm_bwd_mix_out(
        dhp, x2, dx3, p["ffn2_norm_g"], yb, w_out, tile=norm_tile, name="ffn2_norm_bwd")

    (dq, dk, dv, dz, g["attn_out_norm_g"], g["gmlp_out_norm_g"], g["gmlp_ln_g"],
     g["gmlp_ln_b"], g["gmlp_w_s"], dmix_sum, dsinks) = _mix_core_bwd(dyy, *mix_args, name="mix_core_bwd")
    g["gmlp_b_s"] = dmix_sum
    g["attn_sinks"] = dsinks
    dx1, dw_in_t, g["b_in"], g["mix_norm_g"], do1 = _mix_in_bwd(
        x1, dx2, dq, dk, dv, dz, p["mix_norm_g"], w_in_t, tile=tile, name="mix_in_bwd")
    mix_grads = _mix_grads_pack(dw_in_t, dw_out, name="mix_grads_pack")

    dhp1, land = _ffn_bwd(place, hb1, a1, b1, do1, pack_a, 0, land, mix_grads, True, tile=bwd_tile, name="ffn1_bwd")
    dx0, g["ffn1_norm_g"] = _norm_bwd(dhp1, x, dx1, p["ffn1_norm_g"], tile=norm_tile, name="ffn1_norm_bwd")
    return loss, dx0, land, g


def _pack_cast(place, parts, *, name):
    def body(place_ref, *refs):
        oa_ref, ob_ref, om_ref, own_ref = refs[-4:]
        off = 0
        for k, (ref, rows) in enumerate(zip(refs[:-4], BIG_ROWS)):
            if k in (3, 6):
                off = 0
            cast = ref[...].astype(BF16)
            (oa_ref if k < 3 else ob_ref if k < 6 else om_ref)[0, off:off + rows, :] = cast
            if k < 3:
                own_ref[off:off + rows, :] = cast
            off += rows

    one = pl.Buffered(1)

    def slab(rows):
        return pl.BlockSpec((1, rows, D_MODEL), lambda i, pr: (pr[0], 0, 0), pipeline_mode=one)

    grid_spec = pltpu.PrefetchScalarGridSpec(
        num_scalar_prefetch=1, grid=(1,),
        in_specs=[pl.BlockSpec((rows, D_MODEL), lambda i, pr: (0, 0), pipeline_mode=one) for rows in BIG_ROWS],
        out_specs=[slab(PACK_A_ROWS), slab(PACK_B_ROWS), slab(PACK_M_ROWS),
                   pl.BlockSpec((PACK_A_ROWS, D_MODEL), lambda i, pr: (0, 0), pipeline_mode=one)])
    return pl.pallas_call(
        body, name=name, grid_spec=grid_spec,
        out_shape=[jax.ShapeDtypeStruct((N_CHIPS, PACK_A_ROWS, D_MODEL), BF16),
                   jax.ShapeDtypeStruct((N_CHIPS, PACK_B_ROWS, D_MODEL), BF16),
                   jax.ShapeDtypeStruct((N_CHIPS, PACK_M_ROWS, D_MODEL), BF16),
                   jax.ShapeDtypeStruct((PACK_A_ROWS, D_MODEL), BF16)],
        compiler_params=_cparams(("arbitrary",)),
    )(place, *parts)


def _shard_tile(i, c):
    return jnp.where(i < 3, 3 * c + i, jnp.where(i < 6, 3 + 3 * c + i, 12 + c))


def _rs_reduce(place, land, *, name):
    def body(place_ref, l_ref, o_ref):
        acc = l_ref[0].astype(F32)
        for d in range(1, 2 * N_CHIPS):
            acc = acc + l_ref[d].astype(F32)
        o_ref[...] = acc

    grid_spec = pltpu.PrefetchScalarGridSpec(
        num_scalar_prefetch=1, grid=(HALF_ROWS // MIX_HALF,),
        in_specs=[pl.BlockSpec((2 * N_CHIPS, MIX_HALF, D_MODEL), lambda i, pr: (0, i, 0))],
        out_specs=pl.BlockSpec((MIX_HALF, D_MODEL), lambda i, pr: (_shard_tile(i, pr[1]), 0)))
    return pl.pallas_call(
        body, name=name, grid_spec=grid_spec,
        out_shape=jax.ShapeDtypeStruct((PACK_ROWS, D_MODEL), F32),
        compiler_params=_cparams(("arbitrary",)),
    )(place, land)


def _small_all_reduce(packed, shard, *, name):
    rows = packed.shape[0]
    half = rows // 2

    def body(p_ref, sh_in_ref, o_ref, sh_ref, sib_ref, slots_ref, send_sems, recv_sems, share_send, share_recv):
        x, y, c, others = _mesh_place()
        me = 2 * x + y
        sibling = (x, y, 1 - c)
        share_start, share_finish = _share_stages(sh_ref, share_send, share_recv)
        share_start()

        def half_of(core):
            return pl.ds(pl.multiple_of(core * half, 8), half)

        def remote(k, src, dst, to):
            return pltpu.make_async_remote_copy(src_ref=src, dst_ref=dst, send_sem=send_sems.at[k],
                                                recv_sem=recv_sems.at[k], device_id=to, device_id_type=MESH)

        sib = remote(0, p_ref.at[half_of(1 - c)], sib_ref, sibling)
        sib.start()
        sib.wait()
        slots_ref[me] = p_ref[half_of(c), :] + sib_ref[...]
        sends = [remote(1 + j, slots_ref.at[me], slots_ref.at[me], (px, py, c)) for j, (px, py) in enumerate(others)]
        for cp in sends:
            cp.start()
        for j, (px, py) in enumerate(others):
            slab = slots_ref.at[2 * px + py]
            remote(1 + j, slab, slab, (px, py, c)).wait_recv()
        for cp in sends:
            cp.wait_send()
        o_ref[half_of(c), :] = (slots_ref[0] + slots_ref[1]) + (slots_ref[2] + slots_ref[3])
        back = remote(4, o_ref.at[half_of(c)], o_ref.at[half_of(c)], sibling)
        back.start()
        remote(4, o_ref.at[half_of(1 - c)], o_ref.at[half_of(1 - c)], sibling).wait_recv()
        back.wait_send()
        share_finish()

    vm = pl.BlockSpec(memory_space=pltpu.VMEM)
    hbm = pl.BlockSpec(memory_space=pl.ANY)
    return pl.pallas_call(
        body, name=name, in_specs=[vm, hbm], out_specs=[vm, hbm],
        out_shape=[jax.ShapeDtypeStruct((rows, 128), F32), jax.ShapeDtypeStruct(shard.shape, shard.dtype)],
        input_output_aliases={1: 1},
        scratch_shapes=[pltpu.VMEM((half, 128), F32), pltpu.VMEM((N_CHIPS, half, 128), F32),
                        pltpu.SemaphoreType.DMA((5,)), pltpu.SemaphoreType.DMA((5,)),
                        pltpu.SemaphoreType.DMA((3,)), pltpu.SemaphoreType.DMA((3,))],
    )(packed, shard)


def _adamw(w, g, m, v, *, g_row0, tile, name):
    rows, cols = w.shape
    assert g_row0 % tile == 0 and rows % tile == 0

    def body(w_ref, g_ref, m_ref, v_ref, go_ref, d_ref, nm_ref, nv_ref):
        g_v = g_ref[...]
        m_n = ADAM_B1 * m_ref[...] + (1.0 - ADAM_B1) * g_v
        v_n = ADAM_B2 * v_ref[...] + (1.0 - ADAM_B2) * (g_v * g_v)
        m_hat = m_n / (1.0 - ADAM_B1 ** ADAM_STEP)
        v_hat = v_n / (1.0 - ADAM_B2 ** ADAM_STEP)
        d_ref[...] = -ADAM_LR * (m_hat / (jnp.sqrt(v_hat) + ADAM_EPS) + ADAM_WD * w_ref[...])
        go_ref[...] = g_v
        nm_ref[...] = m_n
        nv_ref[...] = v_n

    spec = pl.BlockSpec((tile, cols), lambda i: (i, 0))
    gspec = pl.BlockSpec((tile, cols), lambda i: (g_row0 // tile + i, 0))
    shape = jax.ShapeDtypeStruct((rows, cols), F32)
    return pl.pallas_call(
        body, name=name, grid=(rows // tile,),
        in_specs=[spec, gspec, spec, spec], out_specs=[spec] * 4, out_shape=[shape] * 4,
        compiler_params=_cparams(("arbitrary",)),
    )(w, g, m, v)


def kernel(x, ffn1_norm_g, ffn1_w_gate, ffn1_w_up, ffn1_w_down, mix_norm_g, w_in, b_in, attn_sinks, gmlp_ln_g, gmlp_ln_b, gmlp_w_s, gmlp_b_s, attn_out_norm_g, gmlp_out_norm_g, w_out, b_out, ffn2_norm_g, ffn2_w_gate, ffn2_w_up, ffn2_w_down, final_norm_g, loss_target, m_ffn1_norm_g, m_ffn1_w_gate, m_ffn1_w_up, m_ffn1_w_down, m_mix_norm_g, m_w_in, m_b_in, m_attn_sinks, m_gmlp_ln_g, m_gmlp_ln_b, m_gmlp_w_s, m_gmlp_b_s, m_attn_out_norm_g, m_gmlp_out_norm_g, m_w_out, m_b_out, m_ffn2_norm_g, m_ffn2_w_gate, m_ffn2_w_up, m_ffn2_w_down, m_final_norm_g, v_ffn1_norm_g, v_ffn1_w_gate, v_ffn1_w_up, v_ffn1_w_down, v_mix_norm_g, v_w_in, v_b_in, v_attn_sinks, v_gmlp_ln_g, v_gmlp_ln_b, v_gmlp_w_s, v_gmlp_b_s, v_attn_out_norm_g, v_gmlp_out_norm_g, v_w_out, v_b_out, v_ffn2_norm_g, v_ffn2_w_gate, v_ffn2_w_up, v_ffn2_w_down, v_final_norm_g):
    f_args = dict(locals())
    weights = {n: f_args[n] for n in [nm for nm, _ in SMALL if nm != "loss"] + list(BIG)}
    shapes = {n: weights[n].shape for n in weights}
    shapes["loss"] = ()
    place = jnp.stack([2 * lax.axis_index("x") + lax.axis_index("y"), lax.axis_index("c")]).astype(jnp.int32)

    def with_cols(name, a):
        a2 = a.reshape(a.shape[-2], a.shape[-1])
        return a2.T if BIG_TRANSPOSED[BIG.index(name)] else a2

    def natural(name, a2):
        return (a2.T if BIG_TRANSPOSED[BIG.index(name)] else a2).reshape(shapes[name])

    pack_a, pack_b, pack_m, own_a = _pack_cast(place, [with_cols(n, weights[n]) for n in BIG], name="pack_cast")
    p = {n: weights[n].reshape(1, -1) for n in ("ffn1_norm_g", "mix_norm_g", "b_in", "gmlp_ln_g", "gmlp_ln_b",
                                                "attn_out_norm_g", "gmlp_out_norm_g", "b_out", "ffn2_norm_g",
                                                "final_norm_g")}
    p["attn_sinks"] = attn_sinks.reshape(N_Q_HEADS)
    p["gmlp_w_s"] = gmlp_w_s.reshape(GMLP_GROUPS, BLK, BLK)
    p["bs_full"] = jnp.broadcast_to(gmlp_b_s.reshape(GMLP_GROUPS, BLK).T[:, :, None],
                                    (BLK, GMLP_GROUPS, GROUP_DIM)).reshape(BLK, GMLP_W)

    loss_part, dx0, land, gs = _local_step(place, x[0], loss_target[0], p, own_a, pack_a, pack_b, pack_m)

    gs["gmlp_b_s"] = jnp.sum(gs["gmlp_b_s"].reshape(BLK, GMLP_GROUPS, GROUP_DIM), axis=-1).T
    gs["attn_sinks"] = gs["attn_sinks"][:, 0]
    gs["loss"] = loss_part[0, 0]
    small_sum, shard = _small_all_reduce(_pack_small(gs), _rs_reduce(place, land, name="rs_reduce"),
                                         name="small_all_reduce")

    grad_w, delta, new_m, new_v = {}, {}, {}, {}
    off = 0
    for n, rows in zip(BIG, BIG_ROWS):
        res = _adamw(with_cols(n, weights[n]), shard, with_cols(n, f_args["m_" + n]), with_cols(n, f_args["v_" + n]),
                     g_row0=off, tile=FF_SH // 2 if rows == FF_SH else 64, name="adamw_" + n)
        grad_w[n], delta[n], new_m[n], new_v[n] = [natural(n, a) for a in res]
        off += rows
    sm = {k: {n: f_args[k + n] for n, _ in SMALL if n != "loss"} for k in ("", "m_", "v_")}
    for k in sm:
        sm[k]["loss"] = jnp.zeros((), F32)
    res = _adamw(_pack_small(sm[""]), small_sum, _pack_small(sm["m_"]), _pack_small(sm["v_"]),
                 g_row0=0, tile=SMALL_ROWS, name="adamw_small")
    small = _unpack_small(res[0], shapes)
    for dst, packed in ((grad_w, res[0]), (delta, res[1]), (new_m, res[2]), (new_v, res[3])):
        dst.update({n: a for n, a in _unpack_small(packed, shapes).items() if n != "loss"})

    order = ('ffn1_norm_g', 'ffn1_w_gate', 'ffn1_w_up', 'ffn1_w_down', 'mix_norm_g', 'w_in', 'b_in', 'attn_sinks',
             'gmlp_ln_g', 'gmlp_ln_b', 'gmlp_w_s', 'gmlp_b_s', 'attn_out_norm_g', 'gmlp_out_norm_g', 'w_out', 'b_out',
             'ffn2_norm_g', 'ffn2_w_gate', 'ffn2_w_up', 'ffn2_w_down', 'final_norm_g')
    return (small["loss"], dx0.reshape(x.shape), *[grad_w[n] for n in order], *[delta[n] for n in order],
            *[new_m[n] for n in order], *[new_v[n] for n in order])
```
